```python
import math
import jax
import jax.numpy as jnp
from jax import lax
import numpy as np

D_MODEL = 1024
BATCH = 8
SEQ = 4096
DEPTH = 4

CHUNK = 64
EPS = 1e-6
A_HEADS = 8
A_HEAD_DIM = 64
A_WIDTH = A_HEADS * A_HEAD_DIM
A_PAST_CHUNKS = 8
A_BAND = A_PAST_CHUNKS + 1
A_MAX_REL = 128
B_HEADS = 4
B_HEAD_DIM = 128
B_WIDTH = B_HEADS * B_HEAD_DIM
CONV_K = 4
FFN_HIDDEN = ((8 * D_MODEL + 3 * 256 - 1) // (3 * 256)) * 256
N_MOD = 6
IN_SIZES = (A_WIDTH, A_WIDTH, A_WIDTH, 3 * B_WIDTH, B_WIDTH, B_HEADS, B_HEADS, D_MODEL, D_MODEL)
IN_DIM = sum(IN_SIZES)
IN_SPLITS = tuple(int(s) for s in np.cumsum(IN_SIZES)[:-1])

kernel_name = "hybrid_chunk_attn_gated_deltanet_adaln"


def rmsnorm(x, g):
    xf = x.astype(jnp.float32)
    y = xf * lax.rsqrt(jnp.mean(xf * xf, axis=-1, keepdims=True) + EPS)
    return (y * g.astype(jnp.float32)).astype(x.dtype)


def l2norm(t):
    return t * lax.rsqrt(jnp.sum(t * t, axis=-1, keepdims=True) + EPS)


def chunk_band_attention(q, k, v, rel_table):
    bsz, seq, h, dh = q.shape
    nc = seq // CHUNK
    band = A_BAND * CHUNK
    qc = q.reshape(bsz, nc, CHUNK, h, dh)
    pad = ((0, 0), (A_PAST_CHUNKS, 0), (0, 0), (0, 0), (0, 0))
    kp = jnp.pad(k.reshape(bsz, nc, CHUNK, h, dh), pad)
    vp = jnp.pad(v.reshape(bsz, nc, CHUNK, h, dh), pad)
    band_idx = jnp.arange(nc)[:, None] + jnp.arange(A_BAND)[None, :]
    kb = kp[:, band_idx].reshape(bsz, nc, band, h, dh)
    vb = vp[:, band_idx].reshape(bsz, nc, band, h, dh)
    s = jnp.einsum('bnqhd,bnkhd->bnhqk', qc, kb).astype(jnp.float32) * (dh ** -0.5)
    qpos = A_PAST_CHUNKS * CHUNK + jnp.arange(CHUNK)
    rel = jnp.clip(qpos[:, None] - jnp.arange(band)[None, :], -A_MAX_REL, A_MAX_REL) + A_MAX_REL
    bias = rel_table.astype(jnp.float32)[:, rel]
    key_chunk = jnp.arange(nc)[:, None] - A_PAST_CHUNKS + (jnp.arange(band) // CHUNK)[None, :]
    valid = (key_chunk >= 0)[None, :, None, None, :]
    s = jnp.where(valid, s + bias[None, None], -1e30)
    p = jax.nn.softmax(s, axis=-1).astype(v.dtype)
    o = jnp.einsum('bnhqk,bnkhd->bnqhd', p, vb)
    return o.reshape(bsz, seq, h * dh)


def causal_depthwise_conv(x, w):
    ch = x.shape[-1]
    return lax.conv_general_dilated(
        x, w[:, None, :], window_strides=(1,), padding=[(CONV_K - 1, 0)],
        dimension_numbers=('NWC', 'WIO', 'NWC'), feature_group_count=ch)


def gated_delta_rule_chunked(q, k, v, g, beta):
    bsz, seq, h, dk = q.shape
    dv = v.shape[-1]
    nc = seq // CHUNK
    to_c = lambda t: t.reshape(bsz, nc, CHUNK, h, t.shape[-1]).transpose(0, 1, 3, 2, 4)
    q, k, v = to_c(q), to_c(k), to_c(v)
    g = g.reshape(bsz, nc, CHUNK, h).transpose(0, 1, 3, 2)
    beta = beta.reshape(bsz, nc, CHUNK, h).transpose(0, 1, 3, 2)
    g_cum = jnp.cumsum(g, axis=-1)
    incl = jnp.tril(jnp.ones((CHUNK, CHUNK), dtype=bool))
    strict = jnp.tril(jnp.ones((CHUNK, CHUNK), dtype=bool), k=-1)
    diff = g_cum[..., :, None] - g_cum[..., None, :]
    decay = jnp.exp(jnp.where(incl, diff, -jnp.inf))
    kk = jnp.einsum('bnhid,bnhjd->bnhij', k, k)
    lower = jnp.where(strict, beta[..., :, None] * kk * decay, 0.0)
    rhs = jnp.concatenate([v * beta[..., None], k * (beta * jnp.exp(g_cum))[..., None]], axis=-1)
    sol = lax.linalg.triangular_solve(lower, rhs, left_side=True, lower=True, unit_diagonal=True)
    u_intra, k_cumdecay = sol[..., :dv], sol[..., dv:]
    qk = jnp.einsum('bnhid,bnhjd->bnhij', q, k) * decay
    q_dec = q * jnp.exp(g_cum)[..., None]
    k_dec = k * jnp.exp(g_cum[..., -1:] - g_cum)[..., None]
    g_last = jnp.exp(g_cum[..., -1])

    def step(state, xs):
        u_i, w_i, qk_i, qd_i, kd_i, gl_i = xs
        u = u_i - jnp.einsum('bhck,bhkv->bhcv', w_i, state)
        o = jnp.einsum('bhck,bhkv->bhcv', qd_i, state) + jnp.einsum('bhij,bhjv->bhiv', qk_i, u)
        state = state * gl_i[..., None, None] + jnp.einsum('bhck,bhcv->bhkv', kd_i, u)
        return state, o

    xs = tuple(jnp.moveaxis(t, 1, 0) for t in (u_intra, k_cumdecay, qk, q_dec, k_dec, g_last))
    s0 = jnp.zeros((bsz, h, dk, dv), jnp.float32)
    _, o = lax.scan(step, s0, xs)
    return o.transpose(1, 0, 3, 2, 4).reshape(bsz, seq, h, dv)


def gated_deltanet(qkv_raw, z, b_raw, a_raw, w_conv, a_log, dt_bias, norm_g):
    bsz, seq, _ = qkv_raw.shape
    qkv = jax.nn.silu(causal_depthwise_conv(qkv_raw, w_conv)).astype(jnp.float32)
    q, k, v = jnp.split(qkv, 3, axis=-1)
    q = l2norm(q.reshape(bsz, seq, B_HEADS, B_HEAD_DIM)) * (B_HEAD_DIM ** -0.5)
    k = l2norm(k.reshape(bsz, seq, B_HEADS, B_HEAD_DIM))
    v = v.reshape(bsz, seq, B_HEADS, B_HEAD_DIM)
    beta = jax.nn.sigmoid(b_raw.astype(jnp.float32))
    g = -jnp.exp(a_log.astype(jnp.float32)) * jax.nn.softplus(
        a_raw.astype(jnp.float32) + dt_bias.astype(jnp.float32))
    o = gated_delta_rule_chunked(q, k, v, g, beta)
    zf = z.astype(jnp.float32).reshape(bsz, seq, B_HEADS, B_HEAD_DIM)
    o = (o * lax.rsqrt(jnp.mean(o * o, axis=-1, keepdims=True) + EPS)
         * norm_g.astype(jnp.float32) * jax.nn.silu(zf))
    return o.reshape(bsz, seq, B_WIDTH).astype(qkv_raw.dtype)


def _fwd_setup_inputs(seed: int = 0) -> dict:
    key = jax.random.key(seed)
    ks = jax.random.split(key, 18)
    f32 = jnp.float32

    def nrm(k, shape, scale):
        return jax.random.normal(k, shape, f32) * scale

    x = nrm(ks[0], (BATCH, SEQ, D_MODEL), 1.0)
    c = nrm(ks[1], (BATCH, D_MODEL), 1.0)
    w_ada = nrm(ks[2], (DEPTH, D_MODEL, N_MOD * D_MODEL), 0.5 * D_MODEL ** -0.5)
    b_ada = nrm(ks[3], (DEPTH, N_MOD * D_MODEL), 0.02)
    norm1_g = 1.0 + nrm(ks[4], (DEPTH, D_MODEL), 0.02)
    norm2_g = 1.0 + nrm(ks[5], (DEPTH, D_MODEL), 0.02)
    w_in = nrm(ks[6], (DEPTH, D_MODEL, IN_DIM), D_MODEL ** -0.5)
    rel_table = nrm(ks[7], (DEPTH, A_HEADS, 2 * A_MAX_REL + 1), 0.2)
    w_conv = nrm(ks[8], (DEPTH, CONV_K, 3 * B_WIDTH), CONV_K ** -0.5)
    a_log = jnp.log(jax.random.uniform(ks[9], (DEPTH, B_HEADS), f32, 1.0, 16.0))
    dt = jnp.exp(jax.random.uniform(ks[10], (DEPTH, B_HEADS), f32, math.log(1e-3), math.log(1e-1)))
    dt_bias = dt + jnp.log(-jnp.expm1(-dt))
    gdn_norm_g = 1.0 + nrm(ks[11], (DEPTH, B_HEAD_DIM), 0.02)
    w_branch_a = nrm(ks[12], (DEPTH, A_WIDTH, D_MODEL), A_WIDTH ** -0.5)
    w_branch_b = nrm(ks[13], (DEPTH, B_WIDTH, D_MODEL), B_WIDTH ** -0.5)
    w_out = nrm(ks[14], (DEPTH, D_MODEL, D_MODEL), D_MODEL ** -0.5)
    w_ffn_in = nrm(ks[15], (DEPTH, D_MODEL, 2 * FFN_HIDDEN), D_MODEL ** -0.5)
    w_ffn_out = nrm(ks[16], (DEPTH, FFN_HIDDEN, D_MODEL), FFN_HIDDEN ** -0.5)
    final_g = 1.0 + nrm(ks[17], (D_MODEL,), 0.02)
    return {"x": x, "c": c, "w_ada": w_ada, "b_ada": b_ada, "norm1_g": norm1_g,
            "norm2_g": norm2_g, "w_in": w_in, "rel_table": rel_table, "w_conv": w_conv,
            "a_log": a_log, "dt_bias": dt_bias, "gdn_norm_g": gdn_norm_g,
            "w_branch_a": w_branch_a, "w_branch_b": w_branch_b, "w_out": w_out,
            "w_ffn_in": w_ffn_in, "w_ffn_out": w_ffn_out, "final_g": final_g}


def _fwd_reference(x, c, w_ada, b_ada, norm1_g, norm2_g, w_in, rel_table, w_conv, a_log, dt_bias,
              gdn_norm_g, w_branch_a, w_branch_b, w_out, w_ffn_in, w_ffn_out, final_g):
    bsz, seq, _ = x.shape
    cond = jax.nn.silu(c)
    for l in range(DEPTH):
        mod = (cond @ w_ada[l] + b_ada[l])[:, None, :]
        sh1, sc1, gt1, sh2, sc2, gt2 = jnp.split(mod, N_MOD, axis=-1)
        h = rmsnorm(x, norm1_g[l]) * (1.0 + sc1) + sh1
        proj = h @ w_in[l]
        qa, ka, va, qkvb, zb, bb, ab, ga, gb = jnp.split(proj, IN_SPLITS, axis=-1)
        ya = chunk_band_attention(
            qa.reshape(bsz, seq, A_HEADS, A_HEAD_DIM),
            ka.reshape(bsz, seq, A_HEADS, A_HEAD_DIM),
            va.reshape(bsz, seq, A_HEADS, A_HEAD_DIM), rel_table[l])
        yb = gated_deltanet(qkvb, zb, bb, ab, w_conv[l], a_log[l], dt_bias[l], gdn_norm_g[l])
        merged = (jax.nn.sigmoid(ga) * (ya @ w_branch_a[l])
                  + jax.nn.sigmoid(gb) * (yb @ w_branch_b[l]))
        x = x + gt1 * (merged @ w_out[l])
        h = rmsnorm(x, norm2_g[l]) * (1.0 + sc2) + sh2
        gate, up = jnp.split(h @ w_ffn_in[l], 2, axis=-1)
        x = x + gt2 * ((jax.nn.silu(gate) * up) @ w_ffn_out[l])
    return rmsnorm(x, final_g)


import jax as _jax
import jax.numpy as _jnp

TWIN_FORMAT = 'train_step'
FWD_PARAMS = ['x', 'c', 'w_ada', 'b_ada', 'norm1_g', 'norm2_g', 'w_in', 'rel_table', 'w_conv', 'a_log', 'dt_bias', 'gdn_norm_g', 'w_branch_a', 'w_branch_b', 'w_out', 'w_ffn_in', 'w_ffn_out', 'final_g']
TWIN_WEIGHTS = ['w_ada', 'b_ada', 'norm1_g', 'norm2_g', 'w_in', 'rel_table', 'w_conv', 'a_log', 'dt_bias', 'gdn_norm_g', 'w_branch_a', 'w_branch_b', 'w_out', 'w_ffn_in', 'w_ffn_out', 'final_g']
TWIN_DIFF_INPUT = 'x'
TWIN_INPUTS = ['x', 'c', 'w_ada', 'b_ada', 'norm1_g', 'norm2_g', 'w_in', 'rel_table', 'w_conv', 'a_log', 'dt_bias', 'gdn_norm_g', 'w_branch_a', 'w_branch_b', 'w_out', 'w_ffn_in', 'w_ffn_out', 'final_g', 'loss_target', 'm_w_ada', 'm_b_ada', 'm_norm1_g', 'm_norm2_g', 'm_w_in', 'm_rel_table', 'm_w_conv', 'm_a_log', 'm_dt_bias', 'm_gdn_norm_g', 'm_w_branch_a', 'm_w_branch_b', 'm_w_out', 'm_w_ffn_in', 'm_w_ffn_out', 'm_final_g', 'v_w_ada', 'v_b_ada', 'v_norm1_g', 'v_norm2_g', 'v_w_in', 'v_rel_table', 'v_w_conv', 'v_a_log', 'v_dt_bias', 'v_gdn_norm_g', 'v_w_branch_a', 'v_w_branch_b', 'v_w_out', 'v_w_ffn_in', 'v_w_ffn_out', 'v_final_g']
TWIN_OUTPUTS = ['loss', 'grad_x', 'grad_w_ada', 'grad_b_ada', 'grad_norm1_g', 'grad_norm2_g', 'grad_w_in', 'grad_rel_table', 'grad_w_conv', 'grad_a_log', 'grad_dt_bias', 'grad_gdn_norm_g', 'grad_w_branch_a', 'grad_w_branch_b', 'grad_w_out', 'grad_w_ffn_in', 'grad_w_ffn_out', 'grad_final_g', 'delta_w_ada', 'delta_b_ada', 'delta_norm1_g', 'delta_norm2_g', 'delta_w_in', 'delta_rel_table', 'delta_w_conv', 'delta_a_log', 'delta_dt_bias', 'delta_gdn_norm_g', 'delta_w_branch_a', 'delta_w_branch_b', 'delta_w_out', 'delta_w_ffn_in', 'delta_w_ffn_out', 'delta_final_g', 'new_m_w_ada', 'new_m_b_ada', 'new_m_norm1_g', 'new_m_norm2_g', 'new_m_w_in', 'new_m_rel_table', 'new_m_w_conv', 'new_m_a_log', 'new_m_dt_bias', 'new_m_gdn_norm_g', 'new_m_w_branch_a', 'new_m_w_branch_b', 'new_m_w_out', 'new_m_w_ffn_in', 'new_m_w_ffn_out', 'new_m_final_g', 'new_v_w_ada', 'new_v_b_ada', 'new_v_norm1_g', 'new_v_norm2_g', 'new_v_w_in', 'new_v_rel_table', 'new_v_w_conv', 'new_v_a_log', 'new_v_dt_bias', 'new_v_gdn_norm_g', 'new_v_w_branch_a', 'new_v_w_branch_b', 'new_v_w_out', 'new_v_w_ffn_in', 'new_v_w_ffn_out', 'new_v_final_g']
TWIN_LEAF_KINDS = {'loss': 'loss', 'grad_x': 'grad_x', 'grad_w_ada': 'grad_w', 'grad_b_ada': 'grad_w', 'grad_norm1_g': 'grad_w', 'grad_norm2_g': 'grad_w', 'grad_w_in': 'grad_w', 'grad_rel_table': 'grad_w', 'grad_w_conv': 'grad_w', 'grad_a_log': 'grad_w', 'grad_dt_bias': 'grad_w', 'grad_gdn_norm_g': 'grad_w', 'grad_w_branch_a': 'grad_w', 'grad_w_branch_b': 'grad_w', 'grad_w_out': 'grad_w', 'grad_w_ffn_in': 'grad_w', 'grad_w_ffn_out': 'grad_w', 'grad_final_g': 'grad_w', 'delta_w_ada': 'delta_w', 'delta_b_ada': 'delta_w', 'delta_norm1_g': 'delta_w', 'delta_norm2_g': 'delta_w', 'delta_w_in': 'delta_w', 'delta_rel_table': 'delta_w', 'delta_w_conv': 'delta_w', 'delta_a_log': 'delta_w', 'delta_dt_bias': 'delta_w', 'delta_gdn_norm_g': 'delta_w', 'delta_w_branch_a': 'delta_w', 'delta_w_branch_b': 'delta_w', 'delta_w_out': 'delta_w', 'delta_w_ffn_in': 'delta_w', 'delta_w_ffn_out': 'delta_w', 'delta_final_g': 'delta_w', 'new_m_w_ada': 'new_m', 'new_m_b_ada': 'new_m', 'new_m_norm1_g': 'new_m', 'new_m_norm2_g': 'new_m', 'new_m_w_in': 'new_m', 'new_m_rel_table': 'new_m', 'new_m_w_conv': 'new_m', 'new_m_a_log': 'new_m', 'new_m_dt_bias': 'new_m', 'new_m_gdn_norm_g': 'new_m', 'new_m_w_branch_a': 'new_m', 'new_m_w_branch_b': 'new_m', 'new_m_w_out': 'new_m', 'new_m_w_ffn_in': 'new_m', 'new_m_w_ffn_out': 'new_m', 'new_m_final_g': 'new_m', 'new_v_w_ada': 'new_v', 'new_v_b_ada': 'new_v', 'new_v_norm1_g': 'new_v', 'new_v_norm2_g': 'new_v', 'new_v_w_in': 'new_v', 'new_v_rel_table': 'new_v', 'new_v_w_conv': 'new_v', 'new_v_a_log': 'new_v', 'new_v_dt_bias': 'new_v', 'new_v_gdn_norm_g': 'new_v', 'new_v_w_branch_a': 'new_v', 'new_v_w_branch_b': 'new_v', 'new_v_w_out': 'new_v', 'new_v_w_ffn_in': 'new_v', 'new_v_w_ffn_out': 'new_v', 'new_v_final_g': 'new_v'}


def _forward(args):
    return _fwd_reference(*[args[k] for k in FWD_PARAMS])


def _output_shape():
    def fwd():
        inp = _fwd_setup_inputs(0)
        return _fwd_reference(*[inp[k] for k in FWD_PARAMS])
    out = _jax.eval_shape(fwd)
    return out.shape, out.dtype

N_MICROBATCH = 1
ADAM_LR = 0.001
ADAM_B1 = 0.9
ADAM_B2 = 0.999
ADAM_EPS = 1e-08
ADAM_WD = 0.01
ADAM_STEP = 10
PER_EXAMPLE_BATCH_AXIS = {'x': 0, 'c': 0, 'loss_target': 0}
SHARED_INPUTS = []
_WEIGHT_DTYPES = {'w_ada': _jnp.float32, 'b_ada': _jnp.float32, 'norm1_g': _jnp.float32, 'norm2_g': _jnp.float32, 'w_in': _jnp.float32, 'rel_table': _jnp.float32, 'w_conv': _jnp.float32, 'a_log': _jnp.float32, 'dt_bias': _jnp.float32, 'gdn_norm_g': _jnp.float32, 'w_branch_a': _jnp.float32, 'w_branch_b': _jnp.float32, 'w_out': _jnp.float32, 'w_ffn_in': _jnp.float32, 'w_ffn_out': _jnp.float32, 'final_g': _jnp.float32}
MOMENT_SCALE = {'w_ada': 4.277311e-02, 'b_ada': 7.325292e-02, 'norm1_g': 3.293557e-02, 'norm2_g': 5.223473e-02, 'w_in': 1.527389e-02, 'rel_table': 3.307977e-03, 'w_conv': 2.072166e-02, 'a_log': 1.234059e-01, 'dt_bias': 1.188288e-01, 'gdn_norm_g': 5.609870e-02, 'w_branch_a': 1.077821e-02, 'w_branch_b': 1.913607e-02, 'w_out': 2.180331e-02, 'w_ffn_in': 2.310355e-02, 'w_ffn_out': 3.767916e-02, 'final_g': 3.204077e+01}


def _to_microbatches(a, axis):
    t = _jnp.moveaxis(a, axis, 0)
    t = t.reshape((N_MICROBATCH, t.shape[0] // N_MICROBATCH) + t.shape[1:])
    return _jnp.moveaxis(t, 1, axis + 1)


def setup_inputs(seed: int = 0) -> dict:
    inp = _fwd_setup_inputs(seed)
    key = _jax.random.fold_in(_jax.random.key(seed), 7919)
    shape, _ = _output_shape()
    out = dict(inp)
    out["loss_target"] = _jax.random.normal(_jax.random.fold_in(key, 0), shape, _jnp.float32)
    for i, name in enumerate(TWIN_WEIGHTS):
        w = inp[name].astype(_jnp.float32)
        if MOMENT_SCALE is None:
            s = _jnp.sqrt(_jnp.mean(_jnp.square(w)) + 1e-30)
        else:
            s = MOMENT_SCALE[name]
        km, kv = _jax.random.split(_jax.random.fold_in(key, i + 1))
        out[name] = w
        out["m_" + name] = s * _jax.random.normal(km, w.shape, _jnp.float32)
        out["v_" + name] = (s * s) * _jax.random.uniform(kv, w.shape, _jnp.float32, 0.5, 1.5)
    if N_MICROBATCH > 1:
        for name, axis in PER_EXAMPLE_BATCH_AXIS.items():
            out[name] = _to_microbatches(out[name], axis)
    return {'x': out['x'], 'c': out['c'], 'w_ada': out['w_ada'], 'b_ada': out['b_ada'], 'norm1_g': out['norm1_g'], 'norm2_g': out['norm2_g'], 'w_in': out['w_in'], 'rel_table': out['rel_table'], 'w_conv': out['w_conv'], 'a_log': out['a_log'], 'dt_bias': out['dt_bias'], 'gdn_norm_g': out['gdn_norm_g'], 'w_branch_a': out['w_branch_a'], 'w_branch_b': out['w_branch_b'], 'w_out': out['w_out'], 'w_ffn_in': out['w_ffn_in'], 'w_ffn_out': out['w_ffn_out'], 'final_g': out['final_g'], 'loss_target': out['loss_target'], 'm_w_ada': out['m_w_ada'], 'm_b_ada': out['m_b_ada'], 'm_norm1_g': out['m_norm1_g'], 'm_norm2_g': out['m_norm2_g'], 'm_w_in': out['m_w_in'], 'm_rel_table': out['m_rel_table'], 'm_w_conv': out['m_w_conv'], 'm_a_log': out['m_a_log'], 'm_dt_bias': out['m_dt_bias'], 'm_gdn_norm_g': out['m_gdn_norm_g'], 'm_w_branch_a': out['m_w_branch_a'], 'm_w_branch_b': out['m_w_branch_b'], 'm_w_out': out['m_w_out'], 'm_w_ffn_in': out['m_w_ffn_in'], 'm_w_ffn_out': out['m_w_ffn_out'], 'm_final_g': out['m_final_g'], 'v_w_ada': out['v_w_ada'], 'v_b_ada': out['v_b_ada'], 'v_norm1_g': out['v_norm1_g'], 'v_norm2_g': out['v_norm2_g'], 'v_w_in': out['v_w_in'], 'v_rel_table': out['v_rel_table'], 'v_w_conv': out['v_w_conv'], 'v_a_log': out['v_a_log'], 'v_dt_bias': out['v_dt_bias'], 'v_gdn_norm_g': out['v_gdn_norm_g'], 'v_w_branch_a': out['v_w_branch_a'], 'v_w_branch_b': out['v_w_branch_b'], 'v_w_out': out['v_w_out'], 'v_w_ffn_in': out['v_w_ffn_in'], 'v_w_ffn_out': out['v_w_ffn_out'], 'v_final_g': out['v_final_g']}


def _loss(weights, diff, rest, loss_target):
    with _jax.named_scope("forward"):
        args = {**rest, TWIN_DIFF_INPUT: diff, **{k: w.astype(_WEIGHT_DTYPES[k]) for k, w in weights.items()}}
        y = _forward(args)
    with _jax.named_scope("loss_head"):
        err = _jnp.square(y.astype(_jnp.float32) - loss_target)
        return 0.5 * _jnp.sum(_jnp.mean(err, axis=-1)) if err.ndim else 0.5 * err


def _adamw(w, g, m, v):
    m = ADAM_B1 * m + (1.0 - ADAM_B1) * g
    v = ADAM_B2 * v + (1.0 - ADAM_B2) * _jnp.square(g)
    m_hat = m / (1.0 - ADAM_B1 ** ADAM_STEP)
    v_hat = v / (1.0 - ADAM_B2 ** ADAM_STEP)
    delta = -ADAM_LR * (m_hat / (_jnp.sqrt(v_hat) + ADAM_EPS) + ADAM_WD * w)
    return delta, m, v


def reference(x, c, w_ada, b_ada, norm1_g, norm2_g, w_in, rel_table, w_conv, a_log, dt_bias, gdn_norm_g, w_branch_a, w_branch_b, w_out, w_ffn_in, w_ffn_out, final_g, loss_target, m_w_ada, m_b_ada, m_norm1_g, m_norm2_g, m_w_in, m_rel_table, m_w_conv, m_a_log, m_dt_bias, m_gdn_norm_g, m_w_branch_a, m_w_branch_b, m_w_out, m_w_ffn_in, m_w_ffn_out, m_final_g, v_w_ada, v_b_ada, v_norm1_g, v_norm2_g, v_w_in, v_rel_table, v_w_conv, v_a_log, v_dt_bias, v_gdn_norm_g, v_w_branch_a, v_w_branch_b, v_w_out, v_w_ffn_in, v_w_ffn_out, v_final_g):
    given = dict(x=x, c=c, w_ada=w_ada, b_ada=b_ada, norm1_g=norm1_g, norm2_g=norm2_g, w_in=w_in, rel_table=rel_table, w_conv=w_conv, a_log=a_log, dt_bias=dt_bias, gdn_norm_g=gdn_norm_g, w_branch_a=w_branch_a, w_branch_b=w_branch_b, w_out=w_out, w_ffn_in=w_ffn_in, w_ffn_out=w_ffn_out, final_g=final_g, loss_target=loss_target, m_w_ada=m_w_ada, m_b_ada=m_b_ada, m_norm1_g=m_norm1_g, m_norm2_g=m_norm2_g, m_w_in=m_w_in, m_rel_table=m_rel_table, m_w_conv=m_w_conv, m_a_log=m_a_log, m_dt_bias=m_dt_bias, m_gdn_norm_g=m_gdn_norm_g, m_w_branch_a=m_w_branch_a, m_w_branch_b=m_w_branch_b, m_w_out=m_w_out, m_w_ffn_in=m_w_ffn_in, m_w_ffn_out=m_w_ffn_out, m_final_g=m_final_g, v_w_ada=v_w_ada, v_b_ada=v_b_ada, v_norm1_g=v_norm1_g, v_norm2_g=v_norm2_g, v_w_in=v_w_in, v_rel_table=v_rel_table, v_w_conv=v_w_conv, v_a_log=v_a_log, v_dt_bias=v_dt_bias, v_gdn_norm_g=v_gdn_norm_g, v_w_branch_a=v_w_branch_a, v_w_branch_b=v_w_branch_b, v_w_out=v_w_out, v_w_ffn_in=v_w_ffn_in, v_w_ffn_out=v_w_ffn_out, v_final_g=v_final_g)
    weights = {n: given[n] for n in TWIN_WEIGHTS}
    shared = {n: given[n] for n in SHARED_INPUTS}
    per_example = {n: given[n] for n in ['x', 'c']}
    grad_fn = _jax.value_and_grad(_loss, argnums=(0, 1))

    def one_microbatch(ex, loss_target):
        ex = dict(ex)
        diff = ex.pop(TWIN_DIFF_INPUT)
        return grad_fn(weights, diff, {**shared, **ex}, loss_target)

    if N_MICROBATCH == 1:
        loss, (grad_w, grad_x) = one_microbatch(per_example, given["loss_target"])
    else:
        def body(carry, xs):
            loss_sum, grad_sum = carry
            l_k, (gw_k, gx_k) = one_microbatch(xs[0], xs[1])
            with _jax.named_scope("update"):
                return (loss_sum + l_k, _jax.tree.map(_jnp.add, grad_sum, gw_k)), gx_k

        init = (_jnp.zeros((), _jnp.float32), _jax.tree.map(_jnp.zeros_like, weights))
        (loss, grad_w), grad_x = _jax.lax.scan(body, init, (per_example, given["loss_target"]))
    with _jax.named_scope("update"):
        delta_w, new_m, new_v = {}, {}, {}
        for n in TWIN_WEIGHTS:
            delta_w[n], new_m[n], new_v[n] = _adamw(weights[n], grad_w[n], given["m_" + n], given["v_" + n])
    return (loss, grad_x, *[grad_w[n] for n in TWIN_WEIGHTS], *[delta_w[n] for n in TWIN_WEIGHTS],
            *[new_m[n] for n in TWIN_WEIGHTS], *[new_v[n] for n in TWIN_WEIGHTS])
```

```python
import functools

import numpy as np
import jax
import jax.numpy as jnp
from jax import lax
from jax.experimental import pallas as pl
from jax.experimental.pallas import tpu as pltpu

F32 = jnp.float32
BF16 = jnp.bfloat16
HI = lax.Precision.HIGHEST
MESH = pl.DeviceIdType.MESH

EPS = 1e-6
CHUNK = 64
A_HEADS = 8
A_DH = 64
A_PAST = 8
A_MAX_REL = 128
B_HEADS = 4
B_DH = 128
CONV_K = 4
LANE = 128
QBLK = 4 * CHUNK
KSPAN = QBLK + A_PAST * CHUNK
NEG = -1e30

ADAM_LR = 0.001
ADAM_B1 = 0.9
ADAM_B2 = 0.999
ADAM_EPS = 1e-08
ADAM_WD = 0.01
ADAM_STEP = 10

P_QKVA, P_QKVB, P_GA, P_GB, P_Z, P_BA, P_END = 0, 1536, 3072, 4096, 5120, 5632, 5760
W_IN_SHARD = 1410


def _sigmoid(x):
    return 1.0 / (1.0 + jnp.exp(-x))


def _nn(a, b, prec=None):
    return lax.dot_general(a, b, (((1,), (0,)), ((), ())), precision=prec, preferred_element_type=F32)


def _nt(a, b, prec=None):
    return lax.dot_general(a, b, (((1,), (1,)), ((), ())), precision=prec, preferred_element_type=F32)


def _tn(a, b, prec=None):
    return lax.dot_general(a, b, (((0,), (0,)), ((), ())), precision=prec, preferred_element_type=F32)


def _bnn(a, b):
    return _nn(a.astype(BF16), b.astype(BF16))


def _bnt(a, b):
    return _nt(a.astype(BF16), b.astype(BF16))


def _btn(a, b):
    return _tn(a.astype(BF16), b.astype(BF16))


def _pick(n, target, unit=LANE):
    best = None
    for t in range(unit, min(n, target) + 1, unit):
        if n % t == 0:
            best = t
    return best if best is not None else n


def _acc(ref, val, i):
    @pl.when(i == 0)
    def _():
        ref[...] = val

    @pl.when(i != 0)
    def _():
        ref[...] += val


def _arb(n):
    return pltpu.CompilerParams(dimension_semantics=("arbitrary",) * n)


def _par(n):
    return pltpu.CompilerParams(dimension_semantics=("parallel",) * n)


def _matmul(a, b, mode, out_dtype, name, tm=512, tn=1024, tk=1024, layer=None, stacked=False, out_stacked=False):
    bs = b.shape[1:] if layer is not None else b.shape
    if mode == "nn":
        M, K = a.shape
        N = 4 * bs[2] if stacked else bs[1]
        if stacked:
            tn = bs[2]
    elif mode == "nt":
        M, K = a.shape
        N = bs[1] if stacked else bs[0]
        if stacked:
            tk = bs[2]
    else:
        K, M = a.shape
        N = bs[1]
        if out_stacked:
            tn = N // 4
    tm, tn, tk = _pick(M, tm), _pick(N, tn), _pick(K, tk)
    nk = K // tk
    lead = () if layer is None else (layer,)
    lead_blk = () if layer is None else (None,)
    if mode == "nn":
        a_spec = pl.BlockSpec((tm, tk), lambda i, j, k: (i, k))
        if stacked:
            b_spec = pl.BlockSpec(lead_blk + (None, tk, tn), lambda i, j, k: lead + (j, k, 0))
        else:
            b_spec = pl.BlockSpec(lead_blk + (tk, tn), lambda i, j, k: lead + (k, j))
        dot = _nn
    elif mode == "nt":
        a_spec = pl.BlockSpec((tm, tk), lambda i, j, k: (i, k))
        if stacked:
            b_spec = pl.BlockSpec(lead_blk + (None, tn, tk), lambda i, j, k: lead + (k, j, 0))
        else:
            b_spec = pl.BlockSpec(lead_blk + (tn, tk), lambda i, j, k: lead + (j, k))
        dot = _nt
    else:
        a_spec = pl.BlockSpec((tk, tm), lambda i, j, k: (k, i))
        b_spec = pl.BlockSpec((tk, tn), lambda i, j, k: (k, j))
        dot = _tn
    if out_stacked:
        o_spec = pl.BlockSpec((None, tm, tn), lambda i, j, k: (j, i, 0))
        o_shape = jax.ShapeDtypeStruct((4, M, tn), out_dtype)
    else:
        o_spec = pl.BlockSpec((tm, tn), lambda i, j, k: (i, j))
        o_shape = jax.ShapeDtypeStruct((M, N), out_dtype)

    def body(a_ref, b_ref, o_ref, acc_ref):
        k = pl.program_id(2)

        @pl.when(k == 0)
        def _():
            acc_ref[...] = jnp.zeros_like(acc_ref)

        acc_ref[...] += dot(a_ref[...], b_ref[...])

        @pl.when(k == nk - 1)
        def _():
            o_ref[...] = acc_ref[...].astype(out_dtype)

    return pl.pallas_call(
        body, name=name, grid=(M // tm, N // tn, nk), in_specs=[a_spec, b_spec], out_specs=o_spec, out_shape=o_shape,
        scratch_shapes=[pltpu.VMEM((tm, tn), F32)],
        compiler_params=pltpu.CompilerParams(dimension_semantics=("parallel", "parallel", "arbitrary")),
    )(a, b)


def _rows(tm, n, col=0):
    return pl.BlockSpec((tm, n), lambda i: (i, col))


def _vec(n):
    return pl.BlockSpec((1, n), lambda i: (0, 0))


def _lnmod_fwd(x, g, sc, sh, name):
    S, D = x.shape
    tm = _pick(S, 512, 8)

    def body(x_ref, g_ref, sc_ref, sh_ref, o_ref):
        xv = x_ref[...]
        r = lax.rsqrt(jnp.mean(xv * xv, axis=-1, keepdims=True) + EPS)
        o_ref[...] = ((xv * r * g_ref[...]) * (1.0 + sc_ref[...]) + sh_ref[...]).astype(BF16)

    return pl.pallas_call(
        body, name=name, grid=(S // tm,),
        in_specs=[_rows(tm, D), _vec(D), _vec(D), _vec(D)], out_specs=_rows(tm, D),
        out_shape=jax.ShapeDtypeStruct((S, D), BF16), compiler_params=_par(1),
    )(x, g, sc, sh)


def _lnmod_bwd(dh, x, g, sc, dres, name):
    S, D = x.shape
    tm = _pick(S, 512, 8)

    def body(dh_ref, x_ref, g_ref, sc_ref, dres_ref, dx_ref, dsh_ref, dsc_ref, dg_ref):
        i = pl.program_id(0)
        xv = x_ref[...]
        dh_ = dh_ref[...]
        r = lax.rsqrt(jnp.mean(xv * xv, axis=-1, keepdims=True) + EPS)
        xhat = xv * r
        gv = g_ref[...]
        dn = dh_ * (1.0 + sc_ref[...])
        dxhat = dn * gv
        dx_ref[...] = dres_ref[...] + r * (dxhat - xhat * jnp.mean(dxhat * xhat, axis=-1, keepdims=True))
        _acc(dsh_ref, jnp.sum(dh_, axis=0, keepdims=True), i)
        _acc(dsc_ref, jnp.sum(dh_ * (xhat * gv), axis=0, keepdims=True), i)
        _acc(dg_ref, jnp.sum(dn * xhat, axis=0, keepdims=True), i)

    return pl.pallas_call(
        body, name=name, grid=(S // tm,),
        in_specs=[_rows(tm, D), _rows(tm, D), _vec(D), _vec(D), _rows(tm, D)],
        out_specs=[_rows(tm, D), _vec(D), _vec(D), _vec(D)],
        out_shape=[jax.ShapeDtypeStruct((S, D), F32)] + [jax.ShapeDtypeStruct((1, D), F32)] * 3,
        compiler_params=_arb(1),
    )(dh, x, g, sc, dres)


def _gate_fwd(x, y, gt, name):
    S, D = x.shape
    tm = _pick(S, 512, 8)

    def body(x_ref, y_ref, gt_ref, o_ref):
        o_ref[...] = x_ref[...] + gt_ref[...] * y_ref[...]

    return pl.pallas_call(
        body, name=name, grid=(S // tm,), in_specs=[_rows(tm, D), _rows(tm, D), _vec(D)], out_specs=_rows(tm, D),
        out_shape=jax.ShapeDtypeStruct((S, D), F32), compiler_params=_par(1),
    )(x, y, gt)


def _gate_bwd(dx, y, gt, name):
    S, D = dx.shape
    tm = _pick(S, 512, 8)

    def body(dx_ref, y_ref, gt_ref, dz_ref, dgt_ref):
        i = pl.program_id(0)
        d = dx_ref[...]
        dz_ref[...] = (d * gt_ref[...]).astype(BF16)
        _acc(dgt_ref, jnp.sum(d * y_ref[...], axis=0, keepdims=True), i)

    return pl.pallas_call(
        body, name=name, grid=(S // tm,), in_specs=[_rows(tm, D), _rows(tm, D), _vec(D)],
        out_specs=[_rows(tm, D), _vec(D)],
        out_shape=[jax.ShapeDtypeStruct((S, D), BF16), jax.ShapeDtypeStruct((1, D), F32)],
        compiler_params=_arb(1),
    )(dx, y, gt)


def _ffn_act_fwd(gu, name):
    S, H2 = gu.shape
    H = H2 // 2
    tm = _pick(S, 256, 8)

    def body(g_ref, u_ref, o_ref):
        gv = g_ref[...]
        o_ref[...] = (gv * _sigmoid(gv) * u_ref[...]).astype(BF16)

    return pl.pallas_call(
        body, name=name, grid=(S // tm,), in_specs=[_rows(tm, H, 0), _rows(tm, H, 1)], out_specs=_rows(tm, H),
        out_shape=jax.ShapeDtypeStruct((S, H), BF16), compiler_params=_par(1),
    )(gu, gu)


def _ffn_act_bwd(gu, dact, name):
    S, H2 = gu.shape
    H = H2 // 2
    tm = _pick(S, 256, 8)

    def body(g_ref, u_ref, da_ref, o_ref):
        gv = g_ref[...]
        s = _sigmoid(gv)
        da = da_ref[...]
        o_ref[:, :H] = (da * u_ref[...] * (s * (1.0 + gv * (1.0 - s)))).astype(BF16)
        o_ref[:, H:] = (da * (gv * s)).astype(BF16)

    return pl.pallas_call(
        body, name=name, grid=(S // tm,), in_specs=[_rows(tm, H, 0), _rows(tm, H, 1), _rows(tm, H)],
        out_specs=_rows(tm, H2), out_shape=jax.ShapeDtypeStruct((S, H2), BF16), compiler_params=_par(1),
    )(gu, gu, dact)


def _merge_fwd(proj, pa, pb, name):
    S, D = pa.shape
    tm = _pick(S, 512, 8)

    def body(ga_ref, gb_ref, pa_ref, pb_ref, o_ref):
        o_ref[...] = (_sigmoid(ga_ref[...]) * pa_ref[...] + _sigmoid(gb_ref[...]) * pb_ref[...]).astype(BF16)

    return pl.pallas_call(
        body, name=name, grid=(S // tm,),
        in_specs=[_rows(tm, D, P_GA // D), _rows(tm, D, P_GB // D), _rows(tm, D), _rows(tm, D)],
        out_specs=_rows(tm, D), out_shape=jax.ShapeDtypeStruct((S, D), BF16), compiler_params=_par(1),
    )(proj, proj, pa, pb)


def _merge_bwd(proj, pa, pb, dm, name):
    S, D = pa.shape
    tm = _pick(S, 512, 8)

    def body(ga_ref, gb_ref, pa_ref, pb_ref, dm_ref, dg_ref, dpa_ref, dpb_ref):
        d = dm_ref[...]
        sa = _sigmoid(ga_ref[...])
        sb = _sigmoid(gb_ref[...])
        dg_ref[:, :D] = (d * pa_ref[...] * sa * (1.0 - sa)).astype(BF16)
        dg_ref[:, D:] = (d * pb_ref[...] * sb * (1.0 - sb)).astype(BF16)
        dpa_ref[...] = (d * sa).astype(BF16)
        dpb_ref[...] = (d * sb).astype(BF16)

    return pl.pallas_call(
        body, name=name, grid=(S // tm,),
        in_specs=[_rows(tm, D, P_GA // D), _rows(tm, D, P_GB // D), _rows(tm, D), _rows(tm, D), _rows(tm, D)],
        out_specs=[_rows(tm, 2 * D), _rows(tm, D), _rows(tm, D)],
        out_shape=[jax.ShapeDtypeStruct((S, 2 * D), BF16), jax.ShapeDtypeStruct((S, D), BF16),
                   jax.ShapeDtypeStruct((S, D), BF16)],
        compiler_params=_par(1),
    )(proj, proj, pa, pb, dm)


def _loss_head(x, g, target, name):
    S, D = x.shape
    tm = _pick(S, 512, 8)

    def body(x_ref, g_ref, t_ref, dx_ref, loss_ref, dg_ref):
        i = pl.program_id(0)
        xv = x_ref[...]
        gv = g_ref[...]
        r = lax.rsqrt(jnp.mean(xv * xv, axis=-1, keepdims=True) + EPS)
        xhat = xv * r
        err = xhat * gv - t_ref[...]
        part = 0.5 * jnp.sum(jnp.mean(err * err, axis=-1, keepdims=True), axis=0, keepdims=True)
        _acc(loss_ref, jnp.broadcast_to(part, (1, LANE)), i)
        dy = err * (1.0 / D)
        _acc(dg_ref, jnp.sum(dy * xhat, axis=0, keepdims=True), i)
        dxhat = dy * gv
        dx_ref[...] = r * (dxhat - xhat * jnp.mean(dxhat * xhat, axis=-1, keepdims=True))

    return pl.pallas_call(
        body, name=name, grid=(S // tm,), in_specs=[_rows(tm, D), _vec(D), _rows(tm, D)],
        out_specs=[_rows(tm, D), _vec(LANE), _vec(D)],
        out_shape=[jax.ShapeDtypeStruct((S, D), F32), jax.ShapeDtypeStruct((1, LANE), F32),
                   jax.ShapeDtypeStruct((1, D), F32)],
        compiler_params=_arb(1),
    )(x, g, target)


HEADS_PER_SLAB = LANE // A_DH
N_SLABS = A_HEADS // HEADS_PER_SLAB
SPAN_BLOCKS = KSPAN // QBLK


def _attn_specs(seg):
    q_spec = pl.BlockSpec((QBLK, LANE), lambda p, m: (m, seg[0] * N_SLABS + p))
    k_specs = [pl.BlockSpec((QBLK, LANE), functools.partial(
        lambda j, p, m: (jnp.maximum(m - (SPAN_BLOCKS - 1) + j, 0), seg[1] * N_SLABS + p), j)) for j in range(SPAN_BLOCKS)]
    v_specs = [pl.BlockSpec((QBLK, LANE), functools.partial(
        lambda j, p, m: (jnp.maximum(m - (SPAN_BLOCKS - 1) + j, 0), seg[2] * N_SLABS + p), j)) for j in range(SPAN_BLOCKS)]
    b_spec = pl.BlockSpec((HEADS_PER_SLAB, QBLK, KSPAN), lambda p, m: (p, 0, 0))
    return q_spec, k_specs, v_specs, b_spec


def _head_lanes(t, hh):
    lane = lax.broadcasted_iota(jnp.int32, t.shape, 1)
    return jnp.where((lane // A_DH) == hh, t, jnp.zeros_like(t))


def _attn_probs(qh, k, bias, m):
    s = _nt(qh, k) * (A_DH ** -0.5) + bias
    key_pos = lax.broadcasted_iota(jnp.int32, (QBLK, KSPAN), 1) + (m - (SPAN_BLOCKS - 1)) * QBLK
    s = jnp.where(key_pos >= 0, s, NEG)
    p = jnp.exp(s - jnp.max(s, axis=-1, keepdims=True))
    return p / jnp.sum(p, axis=-1, keepdims=True)


def _attn_fwd(proj, big, name):
    S = proj.shape[0]
    q_spec, k_specs, v_specs, b_spec = _attn_specs((0, 1, 2))

    def body(q_ref, k0, k1, k2, v0, v1, v2, b_ref, o_ref):
        m = pl.program_id(1)
        q = q_ref[...].astype(BF16)
        k = jnp.concatenate([k0[...], k1[...], k2[...]], axis=0).astype(BF16)
        v = jnp.concatenate([v0[...], v1[...], v2[...]], axis=0).astype(BF16)
        outs = [_nn(_attn_probs(_head_lanes(q, hh), k, b_ref[hh], m).astype(BF16), v) for hh in range(HEADS_PER_SLAB)]
        lane = lax.broadcasted_iota(jnp.int32, (QBLK, LANE), 1)
        o_ref[...] = jnp.where(lane < A_DH, outs[0], outs[1]).astype(BF16)

    return pl.pallas_call(
        body, name=name, grid=(N_SLABS, S // QBLK), in_specs=[q_spec] + k_specs + v_specs + [b_spec],
        out_specs=pl.BlockSpec((QBLK, LANE), lambda p, m: (m, p)),
        out_shape=jax.ShapeDtypeStruct((S, A_HEADS * A_DH), BF16), compiler_params=_par(2),
    )(proj, proj, proj, proj, proj, proj, proj, big)


def _attn_bwd(proj, big, dya, name):
    S = proj.shape[0]
    W = A_HEADS * A_DH
    q_spec, k_specs, v_specs, b_spec = _attn_specs((0, 1, 2))
    out_q = pl.BlockSpec((QBLK, LANE), lambda p, m: (m, p))
    out_kv = pl.BlockSpec((S, LANE), lambda p, m: (0, p))

    def body(q_ref, k0, k1, k2, v0, v1, v2, b_ref, do_ref, dq_ref, dk_ref, dv_ref, db_ref):
        m = pl.program_id(1)

        @pl.when(m == 0)
        def _():
            dk_ref[...] = jnp.zeros_like(dk_ref)
            dv_ref[...] = jnp.zeros_like(dv_ref)
            db_ref[...] = jnp.zeros_like(db_ref)

        q = q_ref[...].astype(BF16)
        k = jnp.concatenate([k0[...], k1[...], k2[...]], axis=0).astype(BF16)
        v = jnp.concatenate([v0[...], v1[...], v2[...]], axis=0).astype(BF16)
        do = do_ref[...]
        dqs = []
        dk = jnp.zeros((KSPAN, LANE), F32)
        dv = jnp.zeros((KSPAN, LANE), F32)
        for hh in range(HEADS_PER_SLAB):
            qh = _head_lanes(q, hh)
            doh = _head_lanes(do, hh)
            p = _attn_probs(qh, k, b_ref[hh], m)
            dp = _nt(doh, v)
            ds = p * (dp - jnp.sum(p * dp, axis=-1, keepdims=True))
            db_ref[hh] += ds
            dsb = (ds * (A_DH ** -0.5)).astype(BF16)
            dqs.append(_nn(dsb, k))
            dk = dk + _tn(dsb, qh)
            dv = dv + _tn(p.astype(BF16), doh)
        lane = lax.broadcasted_iota(jnp.int32, (QBLK, LANE), 1)
        dq_ref[...] = jnp.where(lane < A_DH, dqs[0], dqs[1])
        for j in range(SPAN_BLOCKS):
            blk = m - (SPAN_BLOCKS - 1) + j

            @pl.when(blk >= 0)
            def _():
                off = pl.multiple_of(blk * QBLK, QBLK)
                dk_ref[pl.ds(off, QBLK), :] += dk[j * QBLK:(j + 1) * QBLK]
                dv_ref[pl.ds(off, QBLK), :] += dv[j * QBLK:(j + 1) * QBLK]

    return pl.pallas_call(
        body, name=name, grid=(N_SLABS, S // QBLK),
        in_specs=[q_spec] + k_specs + v_specs + [b_spec, pl.BlockSpec((QBLK, LANE), lambda p, m: (m, p))],
        out_specs=[out_q, out_kv, out_kv, b_spec],
        out_shape=[jax.ShapeDtypeStruct((S, W), F32)] * 3 + [jax.ShapeDtypeStruct((A_HEADS, QBLK, KSPAN), F32)],
        compiler_params=_arb(2),
    )(proj, proj, proj, proj, proj, proj, proj, big, dya)


NREL_PAD = 3 * LANE
SKEW_W = 1024


def _rel_table_grad(dbig, name):
    H, R, C = dbig.shape

    def body(d_ref, o_ref):
        x = jnp.concatenate([d_ref[...], jnp.zeros((R, SKEW_W - C), F32)], axis=1)
        row = lax.broadcasted_iota(jnp.int32, (R, SKEW_W), 0)
        for b in range(R.bit_length() - 1):
            x = jnp.where(((row >> b) & 1) == 1, pltpu.roll(x, SKEW_W - (1 << b), 1), x)
        e = jnp.sum(x, axis=0, keepdims=True)
        xi = lax.broadcasted_iota(jnp.int32, (SKEW_W, NREL_PAD), 0)
        r = lax.broadcasted_iota(jnp.int32, (SKEW_W, NREL_PAD), 1)
        diag = jnp.where(xi < C, xi, xi - SKEW_W)
        rel = jnp.clip(A_PAST * CHUNK - diag, -A_MAX_REL, A_MAX_REL) + A_MAX_REL
        o_ref[...] = _nn(e, jnp.where(rel == r, 1.0, 0.0).astype(F32), HI)

    return pl.pallas_call(
        body, name=name, grid=(H,), in_specs=[pl.BlockSpec((None, R, C), lambda h: (h, 0, 0))],
        out_specs=pl.BlockSpec((None, 1, NREL_PAD), lambda h: (h, 0, 0)),
        out_shape=jax.ShapeDtypeStruct((H, 1, NREL_PAD), F32), compiler_params=_par(1),
    )(dbig)


def _chunk_cumsum_matrix(n, reverse):
    j = lax.broadcasted_iota(jnp.int32, (n, n), 0)
    i = lax.broadcasted_iota(jnp.int32, (n, n), 1)
    same = (j // CHUNK) == (i // CHUNK)
    return jnp.where(same & ((j >= i) if reverse else (j <= i)), 1.0, 0.0).astype(F32)


def _gdn_gates_fwd(b_t, a_t, alog, dtb, name):
    Hh, S = b_t.shape
    tl = _pick(S, 512)
    row = pl.BlockSpec((Hh, tl), lambda i: (0, i))
    col = pl.BlockSpec((Hh, 1), lambda i: (0, 0))

    def body(b_ref, a_ref, al_ref, dt_ref, beta_ref, gam_ref):
        z = a_ref[...] + dt_ref[...]
        sp = jnp.maximum(z, 0.0) + jnp.log(1.0 + jnp.exp(-jnp.abs(z)))
        g = -jnp.exp(al_ref[...]) * sp
        beta_ref[...] = _sigmoid(b_ref[...])
        gam_ref[...] = _nn(g, _chunk_cumsum_matrix(tl, False), HI)

    return pl.pallas_call(
        body, name=name, grid=(S // tl,), in_specs=[row, row, col, col], out_specs=[row, row],
        out_shape=[jax.ShapeDtypeStruct((Hh, S), F32)] * 2, compiler_params=_par(1),
    )(b_t, a_t, alog, dtb)


def _gdn_gates_bwd(dbeta, dgam_a, dgam_b, b_t, a_t, alog, dtb, name):
    Hh, S = b_t.shape
    tl = _pick(S, 512)
    row = pl.BlockSpec((Hh, tl), lambda i: (0, i))
    col = pl.BlockSpec((Hh, 1), lambda i: (0, 0))
    accs = pl.BlockSpec((Hh, LANE), lambda i: (0, 0))

    def body(dbeta_ref, dga_ref, dgb_ref, b_ref, a_ref, al_ref, dt_ref, db_ref, da_ref, dal_ref, ddt_ref):
        i = pl.program_id(0)
        z = a_ref[...] + dt_ref[...]
        sp = jnp.maximum(z, 0.0) + jnp.log(1.0 + jnp.exp(-jnp.abs(z)))
        ea = jnp.exp(al_ref[...])
        dg = _nn(dga_ref[...] + dgb_ref[...], _chunk_cumsum_matrix(tl, True), HI)
        da = dg * (-ea) * _sigmoid(z)
        beta = _sigmoid(b_ref[...])
        db_ref[...] = dbeta_ref[...] * beta * (1.0 - beta)
        da_ref[...] = da
        _acc(dal_ref, jnp.broadcast_to(jnp.sum(dg * (-ea * sp), axis=1, keepdims=True), (Hh, LANE)), i)
        _acc(ddt_ref, jnp.broadcast_to(jnp.sum(da, axis=1, keepdims=True), (Hh, LANE)), i)

    return pl.pallas_call(
        body, name=name, grid=(S // tl,), in_specs=[row] * 5 + [col, col], out_specs=[row, row, accs, accs],
        out_shape=[jax.ShapeDtypeStruct((Hh, S), F32)] * 2 + [jax.ShapeDtypeStruct((Hh, LANE), F32)] * 2,
        compiler_params=_arb(1),
    )(dbeta, dgam_a, dgam_b, b_t, a_t, alog, dtb)


HALO = 8


def _conv_silu(xx_ref, w_ref, tm):
    y = w_ref[0:1, :] * xx_ref[pl.ds(HALO - CONV_K + 1, tm), :]
    for j in range(1, CONV_K):
        y = y + w_ref[j:j + 1, :] * xx_ref[pl.ds(HALO - CONV_K + 1 + j, tm), :]
    return y, y * _sigmoid(y)


def _fill_prev_halo(xx_ref, x_ref, prev_ref, i, tm):
    xx_ref[pl.ds(HALO, tm), :] = x_ref[...]

    @pl.when(i == 0)
    def _():
        xx_ref[pl.ds(0, HALO), :] = jnp.zeros((HALO, xx_ref.shape[1]), F32)

    @pl.when(i != 0)
    def _():
        xx_ref[pl.ds(0, HALO), :] = prev_ref[...]


def _gdn_pre_specs(tm, C, colblk):
    cur = pl.BlockSpec((tm, C), lambda i: (i, colblk))
    prev = pl.BlockSpec((HALO, C), lambda i: (jnp.maximum(i * (tm // HALO) - 1, 0), colblk))
    return cur, prev


def _gdn_pre_fwd(proj, wconv, name):
    S = proj.shape[0]
    C = 3 * B_HEADS * B_DH
    W = B_HEADS * B_DH
    tm = _pick(S, 256, 8)
    cur, prev = _gdn_pre_specs(tm, C, P_QKVB // C)

    def body(x_ref, prev_ref, w_ref, q_ref, k_ref, v_ref, xx_ref):
        i = pl.program_id(0)
        _fill_prev_halo(xx_ref, x_ref, prev_ref, i, tm)
        _, sl = _conv_silu(xx_ref, w_ref, tm)
        for h in range(B_HEADS):
            hs = slice(h * B_DH, (h + 1) * B_DH)
            q = sl[:, h * B_DH:(h + 1) * B_DH]
            k = sl[:, W + h * B_DH:W + (h + 1) * B_DH]
            q_ref[:, hs] = q * (lax.rsqrt(jnp.sum(q * q, axis=-1, keepdims=True) + EPS) * (B_DH ** -0.5))
            k_ref[:, hs] = k * lax.rsqrt(jnp.sum(k * k, axis=-1, keepdims=True) + EPS)
        v_ref[...] = sl[:, 2 * W:]

    return pl.pallas_call(
        body, name=name, grid=(S // tm,), in_specs=[cur, prev, pl.BlockSpec((CONV_K, C), lambda i: (0, 0))],
        out_specs=[_rows(tm, W)] * 3, out_shape=[jax.ShapeDtypeStruct((S, W), F32)] * 3,
        scratch_shapes=[pltpu.VMEM((HALO + tm, C), F32)], compiler_params=_par(1),
    )(proj, proj, wconv)


def _gdn_pre_bwd_a(proj, wconv, dqn, dkn, dv, name):
    S = proj.shape[0]
    C = 3 * B_HEADS * B_DH
    W = B_HEADS * B_DH
    tm = _pick(S, 256, 8)
    cur, prev = _gdn_pre_specs(tm, C, P_QKVB // C)

    def body(x_ref, prev_ref, w_ref, dq_ref, dk_ref, dv_ref, dy_ref, xx_ref):
        i = pl.program_id(0)
        _fill_prev_halo(xx_ref, x_ref, prev_ref, i, tm)
        y, sl = _conv_silu(xx_ref, w_ref, tm)
        sg = _sigmoid(y)
        dsilu = sg * (1.0 + y * (1.0 - sg))
        for h in range(B_HEADS):
            for base, d_ref, c in ((0, dq_ref, B_DH ** -0.5), (W, dk_ref, 1.0)):
                lo = base + h * B_DH
                t = sl[:, lo:lo + B_DH]
                d = d_ref[:, h * B_DH:(h + 1) * B_DH]
                r = lax.rsqrt(jnp.sum(t * t, axis=-1, keepdims=True) + EPS)
                dt = (c * r) * (d - t * (r * r) * jnp.sum(d * t, axis=-1, keepdims=True))
                dy_ref[:, lo:lo + B_DH] = dt * dsilu[:, lo:lo + B_DH]
        dy_ref[:, 2 * W:] = dv_ref[...] * dsilu[:, 2 * W:]

    return pl.pallas_call(
        body, name=name, grid=(S // tm,),
        in_specs=[cur, prev, pl.BlockSpec((CONV_K, C), lambda i: (0, 0))] + [_rows(tm, W)] * 3,
        out_specs=_rows(tm, C), out_shape=jax.ShapeDtypeStruct((S, C), F32),
        scratch_shapes=[pltpu.VMEM((HALO + tm, C), F32)], compiler_params=_par(1),
    )(proj, proj, wconv, dqn, dkn, dv)


def _gdn_pre_bwd_b(proj, wconv, dy, name):
    S = proj.shape[0]
    C = 3 * B_HEADS * B_DH
    tm = _pick(S, 256, 8)
    nt_ = S // tm
    cur, prev = _gdn_pre_specs(tm, C, P_QKVB // C)
    nxt = pl.BlockSpec((HALO, C), lambda i: (jnp.minimum((i + 1) * (tm // HALO), S // HALO - 1), 0))

    def body(x_ref, prev_ref, w_ref, dy_ref, next_ref, dx_ref, dw_ref, xx_ref, dd_ref):
        i = pl.program_id(0)
        _fill_prev_halo(xx_ref, x_ref, prev_ref, i, tm)
        dyv = dy_ref[...]
        dd_ref[pl.ds(0, tm), :] = dyv

        @pl.when(i == nt_ - 1)
        def _():
            dd_ref[pl.ds(tm, HALO), :] = jnp.zeros((HALO, C), F32)

        @pl.when(i != nt_ - 1)
        def _():
            dd_ref[pl.ds(tm, HALO), :] = next_ref[...]

        dx = w_ref[0:1, :] * dd_ref[pl.ds(CONV_K - 1, tm), :]
        for j in range(1, CONV_K):
            dx = dx + w_ref[j:j + 1, :] * dd_ref[pl.ds(CONV_K - 1 - j, tm), :]
        dx_ref[...] = dx.astype(BF16)
        dw = jnp.concatenate(
            [jnp.sum(dyv * xx_ref[pl.ds(HALO - CONV_K + 1 + j, tm), :], axis=0, keepdims=True) for j in range(CONV_K)],
            axis=0)
        _acc(dw_ref, dw, i)

    return pl.pallas_call(
        body, name=name, grid=(nt_,),
        in_specs=[cur, prev, pl.BlockSpec((CONV_K, C), lambda i: (0, 0)), _rows(tm, C), nxt],
        out_specs=[_rows(tm, C), pl.BlockSpec((CONV_K, C), lambda i: (0, 0))],
        out_shape=[jax.ShapeDtypeStruct((S, C), BF16), jax.ShapeDtypeStruct((CONV_K, C), F32)],
        scratch_shapes=[pltpu.VMEM((HALO + tm, C), F32), pltpu.VMEM((tm + HALO, C), F32)],
        compiler_params=_arb(1),
    )(proj, proj, wconv, dy, dy)


def _chunk_masks():
    row = lax.broadcasted_iota(jnp.int32, (CHUNK, CHUNK), 0)
    col = lax.broadcasted_iota(jnp.int32, (CHUNK, CHUNK), 1)
    return row >= col, row > col


def _chunk_local(q, k, vv, bc, gc, gr, tri):
    dm = jnp.where(tri, jnp.exp(jnp.where(tri, gc - gr, 0.0)), 0.0)
    kk = _bnt(k, k)
    glast = gr[:, CHUNK - 1:CHUNK]
    ep = jnp.exp(gc)
    em = jnp.exp(glast - gc)
    el = jnp.exp(glast)
    return dm, kk, ep, em, el, vv * bc, k * (bc * ep)


def _unit_lower_inverse(low):
    row = lax.broadcasted_iota(jnp.int32, (CHUNK, CHUNK), 0)
    col = lax.broadcasted_iota(jnp.int32, (CHUNK, CHUNK), 1)
    p = -low
    t = jnp.where(row == col, 1.0, 0.0).astype(F32) + p
    steps = CHUNK.bit_length() - 2
    for _ in range(steps):
        p = _nn(p, p, HI)
        t = t + _nn(t, p, HI)
    return t


def _gdn_chunk_specs(nc, rev):
    idx = (lambda i: nc - 1 - i) if rev else (lambda i: i)
    W = B_HEADS * B_DH
    tok = pl.BlockSpec((CHUNK, W), lambda i: (idx(i), 0))
    colv = pl.BlockSpec((None, CHUNK, B_HEADS), lambda i: (idx(i), 0, 0))
    rowv = pl.BlockSpec((None, B_HEADS, CHUNK), lambda i: (idx(i), 0, 0))
    tmat = pl.BlockSpec((None, B_HEADS, CHUNK, CHUNK), lambda i: (idx(i), 0, 0, 0))
    smat = pl.BlockSpec((None, B_HEADS, B_DH, B_DH), lambda i: (idx(i), 0, 0, 0))
    return tok, colv, rowv, tmat, smat


def _gdn_chunk_fwd(qn, kn, v, bcol, gcol, grow, name):
    S, W = qn.shape
    nc = S // CHUNK
    tok, colv, rowv, tmat, smat = _gdn_chunk_specs(nc, False)

    def body(q_ref, k_ref, v_ref, bc_ref, gc_ref, gr_ref, o_ref, t_ref, sh_ref, st_ref):
        i = pl.program_id(0)

        @pl.when(i == 0)
        def _():
            st_ref[...] = jnp.zeros_like(st_ref)

        tri, strict = _chunk_masks()
        bc_all = bc_ref[...]
        gc_all = gc_ref[...]
        for h in range(B_HEADS):
            hs = slice(h * B_DH, (h + 1) * B_DH)
            q, k, vv = q_ref[:, hs], k_ref[:, hs], v_ref[:, hs]
            bc, gc, gr = bc_all[:, h:h + 1], gc_all[:, h:h + 1], gr_ref[h:h + 1, :]
            dm, kk, ep, em, el, vb, kb = _chunk_local(q, k, vv, bc, gc, gr, tri)
            t = _unit_lower_inverse(jnp.where(strict, bc * kk * dm, 0.0))
            u = _nn(t, vb, HI)
            w = _nn(t, kb, HI)
            a = _bnt(q, k) * dm
            s0 = st_ref[h]
            ut = u - _bnn(w, s0)
            o_ref[:, hs] = _bnn(q * ep, s0) + _bnn(a, ut)
            st_ref[h] = el * s0 + _btn(k * em, ut)
            t_ref[h] = t
            sh_ref[h] = s0

    return pl.pallas_call(
        body, name=name, grid=(nc,), in_specs=[tok, tok, tok, colv, colv, rowv], out_specs=[tok, tmat, smat],
        out_shape=[jax.ShapeDtypeStruct((S, W), F32), jax.ShapeDtypeStruct((nc, B_HEADS, CHUNK, CHUNK), F32),
                   jax.ShapeDtypeStruct((nc, B_HEADS, B_DH, B_DH), F32)],
        scratch_shapes=[pltpu.VMEM((B_HEADS, B_DH, B_DH), F32)], compiler_params=_arb(1),
    )(qn, kn, v, bcol, gcol, grow)


def _gdn_chunk_bwd(qn, kn, v, bcol, gcol, grow, tsave, ssave, do, name):
    S, W = qn.shape
    nc = S // CHUNK
    tok, colv, rowv, tmat, smat = _gdn_chunk_specs(nc, True)

    def body(q_ref, k_ref, v_ref, bc_ref, gc_ref, gr_ref, t_ref, sh_ref, do_ref,
             dq_ref, dk_ref, dv_ref, dbc_ref, dgc_ref, dgr_ref, ds_ref):
        i = pl.program_id(0)

        @pl.when(i == 0)
        def _():
            ds_ref[...] = jnp.zeros_like(ds_ref)

        tri, strict = _chunk_masks()
        bc_all = bc_ref[...]
        gc_all = gc_ref[...]
        lane4 = lax.broadcasted_iota(jnp.int32, (CHUNK, B_HEADS), 1)
        sub4 = lax.broadcasted_iota(jnp.int32, (B_HEADS, CHUNK), 0)
        lane_last = lax.broadcasted_iota(jnp.int32, (1, CHUNK), 1) == CHUNK - 1
        dbc_acc = jnp.zeros((CHUNK, B_HEADS), F32)
        dgc_acc = jnp.zeros((CHUNK, B_HEADS), F32)
        dgr_acc = jnp.zeros((B_HEADS, CHUNK), F32)
        for h in range(B_HEADS):
            hs = slice(h * B_DH, (h + 1) * B_DH)
            q, k, vv = q_ref[:, hs], k_ref[:, hs], v_ref[:, hs]
            bc, gc, gr = bc_all[:, h:h + 1], gc_all[:, h:h + 1], gr_ref[h:h + 1, :]
            dm, kk, ep, em, el, vb, kb = _chunk_local(q, k, vv, bc, gc, gr, tri)
            t = t_ref[h]
            s0 = sh_ref[h]
            w = _nn(t, kb, HI)
            qk = _bnt(q, k)
            a = qk * dm
            ut = _nn(t, vb, HI) - _bnn(w, s0)
            qd = q * ep
            kd = k * em
            ds = ds_ref[h]
            dout = do_ref[:, hs]
            dut = _btn(a, dout) + _bnn(kd, ds)
            da = jnp.where(tri, _bnt(dout, ut), 0.0)
            dqd = _bnt(dout, s0)
            dkd = _bnt(ut, ds)
            d_el = jnp.sum(jnp.sum(s0 * ds, axis=1, keepdims=True), axis=0, keepdims=True)
            dw = -_bnt(dut, s0)
            ds_ref[h] = el * ds + _btn(qd, dout) - _btn(w, dut)
            dt = _nt(dut, vb, HI) + _nt(dw, kb, HI)
            dvb = _tn(t, dut, HI)
            dkb = _tn(t, dw, HI)
            dl = jnp.where(strict, -_tn(t, _nt(dt, t, HI), HI), 0.0)
            g1 = dl * dm
            dkb_k = jnp.sum(dkb * k, axis=1, keepdims=True)
            dbeta = jnp.sum(g1 * kk, axis=1, keepdims=True) + jnp.sum(dvb * vv, axis=1, keepdims=True) + dkb_k * ep
            dkk = g1 * bc
            ddm = dl * (bc * kk) + da * qk
            dqk = da * dm
            dq_ref[:, hs] = _bnn(dqk, k) + dqd * ep
            dk_ref[:, hs] = (_btn(dqk, q) + _bnn(dkk, k) + _btn(dkk, k) + dkb * (bc * ep) + dkd * em)
            dv_ref[:, hs] = dvb * bc
            dep = dkb_k * bc + jnp.sum(dqd * q, axis=1, keepdims=True)
            dem = jnp.sum(dkd * k, axis=1, keepdims=True)
            mm = ddm * dm
            dgam_c = jnp.sum(mm, axis=1, keepdims=True) + dep * ep - dem * em
            dglast = jnp.sum(dem * em, axis=0, keepdims=True) + d_el * el
            dgam_r = -jnp.sum(mm, axis=0, keepdims=True) + jnp.where(lane_last, dglast, 0.0)
            dbc_acc = jnp.where(lane4 == h, dbeta, dbc_acc)
            dgc_acc = jnp.where(lane4 == h, dgam_c, dgc_acc)
            dgr_acc = jnp.where(sub4 == h, dgam_r, dgr_acc)
        dbc_ref[...] = dbc_acc
        dgc_ref[...] = dgc_acc
        dgr_ref[...] = dgr_acc

    return pl.pallas_call(
        body, name=name, grid=(nc,), in_specs=[tok, tok, tok, colv, colv, rowv, tmat, smat, tok],
        out_specs=[tok, tok, tok, colv, colv, rowv],
        out_shape=[jax.ShapeDtypeStruct((S, W), F32)] * 3
        + [jax.ShapeDtypeStruct((nc, CHUNK, B_HEADS), F32)] * 2 + [jax.ShapeDtypeStruct((nc, B_HEADS, CHUNK), F32)],
        scratch_shapes=[pltpu.VMEM((B_HEADS, B_DH, B_DH), F32)], compiler_params=_arb(1),
    )(qn, kn, v, bcol, gcol, grow, tsave, ssave, do)


def _gdn_post_fwd(o, proj, ng, name):
    S, W = o.shape
    tm = _pick(S, 512, 8)

    def body(o_ref, z_ref, g_ref, y_ref):
        gv = g_ref[...]
        for h in range(B_HEADS):
            hs = slice(h * B_DH, (h + 1) * B_DH)
            oh = o_ref[:, hs]
            z = z_ref[:, hs]
            r = lax.rsqrt(jnp.mean(oh * oh, axis=-1, keepdims=True) + EPS)
            y_ref[:, hs] = (oh * r * gv * (z * _sigmoid(z))).astype(BF16)

    return pl.pallas_call(
        body, name=name, grid=(S // tm,), in_specs=[_rows(tm, W), _rows(tm, W, P_Z // W), _vec(B_DH)],
        out_specs=_rows(tm, W), out_shape=jax.ShapeDtypeStruct((S, W), BF16), compiler_params=_par(1),
    )(o, proj, ng)


def _gdn_post_bwd(dy, o, proj, ng, name):
    S, W = o.shape
    tm = _pick(S, 512, 8)

    def body(dy_ref, o_ref, z_ref, g_ref, do_ref, dz_ref, dg_ref):
        i = pl.program_id(0)
        gv = g_ref[...]
        dg = jnp.zeros((1, B_DH), F32)
        for h in range(B_HEADS):
            hs = slice(h * B_DH, (h + 1) * B_DH)
            oh = o_ref[:, hs]
            z = z_ref[:, hs]
            d = dy_ref[:, hs]
            r = lax.rsqrt(jnp.mean(oh * oh, axis=-1, keepdims=True) + EPS)
            n = oh * r
            sg = _sigmoid(z)
            sz = z * sg
            dn = d * gv * sz
            dg = dg + jnp.sum(d * n * sz, axis=0, keepdims=True)
            dz_ref[:, hs] = (d * n * gv * (sg * (1.0 + z * (1.0 - sg)))).astype(BF16)
            do_ref[:, hs] = r * (dn - n * jnp.mean(dn * n, axis=-1, keepdims=True))
        _acc(dg_ref, dg, i)

    return pl.pallas_call(
        body, name=name, grid=(S // tm,), in_specs=[_rows(tm, W), _rows(tm, W), _rows(tm, W, P_Z // W), _vec(B_DH)],
        out_specs=[_rows(tm, W), _rows(tm, W), _vec(B_DH)],
        out_shape=[jax.ShapeDtypeStruct((S, W), F32), jax.ShapeDtypeStruct((S, W), BF16),
                   jax.ShapeDtypeStruct((1, B_DH), F32)],
        compiler_params=_arb(1),
    )(dy, o, proj, ng)


def _ada_mod(c_all, w_ada, b_shard, name):
    L, D, Ns = w_ada.shape
    B = c_all.shape[0]

    def body(c_ref, w_ref, b_ref, o_ref):
        cv = c_ref[...]
        cond = (cv * _sigmoid(cv)).astype(BF16)
        o_ref[...] = _nn(cond, w_ref[...].astype(BF16)) + b_ref[...]

    return pl.pallas_call(
        body, name=name, grid=(L,),
        in_specs=[pl.BlockSpec((B, D), lambda l: (0, 0)), pl.BlockSpec((None, D, Ns), lambda l: (l, 0, 0)),
                  pl.BlockSpec((None, 1, Ns), lambda l: (l, 0, 0))],
        out_specs=pl.BlockSpec((None, B, Ns), lambda l: (l, 0, 0)),
        out_shape=jax.ShapeDtypeStruct((L, B, Ns), F32), compiler_params=_par(1),
    )(c_all, w_ada, b_shard)


def _ada_wgrad(c_all, dmod, name):
    L, B, Ns = dmod.shape
    D = c_all.shape[1]

    def body(c_ref, d_ref, o_ref):
        cv = c_ref[...]
        cond = (cv * _sigmoid(cv)).astype(BF16)
        o_ref[...] = _tn(cond, d_ref[...].astype(BF16))

    return pl.pallas_call(
        body, name=name, grid=(L,),
        in_specs=[pl.BlockSpec((B, D), lambda l: (0, 0)), pl.BlockSpec((None, B, Ns), lambda l: (l, 0, 0))],
        out_specs=pl.BlockSpec((None, D, Ns), lambda l: (l, 0, 0)),
        out_shape=jax.ShapeDtypeStruct((L, D, Ns), F32), compiler_params=_par(1),
    )(c_all, dmod)


W_IN_PIECES = ((0, 0, 1410), (1, 0, 1410), (2, 0, 252), (2, 772, 638), (3, 0, 1410), (2, 252, 512), (2, 764, 8))


def _reorder_w_in(w4, name):
    L, _, D, Cs = w4.shape
    tm = _pick(D, 256, 16)
    used = sum(p[2] for p in W_IN_PIECES)

    def body(w_ref, o_ref):
        shard = [w_ref[s] for s in range(4)]
        parts = [shard[s][:, lo:lo + n] for s, lo, n in W_IN_PIECES]
        o_ref[...] = jnp.concatenate(parts + [jnp.zeros((tm, P_END - used), w4.dtype)], axis=1)

    return pl.pallas_call(
        body, name=name, grid=(L, D // tm), in_specs=[pl.BlockSpec((None, 4, tm, Cs), lambda l, i: (l, 0, i, 0))],
        out_specs=pl.BlockSpec((None, tm, P_END), lambda l, i: (l, i, 0)),
        out_shape=jax.ShapeDtypeStruct((L, D, P_END), w4.dtype), compiler_params=_par(2),
    )(w4)


def _restore_w_in(g, name):
    D = g.shape[0]
    tm = _pick(D, 256, 8)

    def body(g_ref, o_ref):
        gv = g_ref[...]
        off = 0
        pieces = {}
        for s, lo, n in W_IN_PIECES:
            pieces.setdefault(s, []).append((lo, gv[:, off:off + n]))
            off += n
        for s, lst in pieces.items():
            lst.sort(key=lambda t: t[0])
            o_ref[s] = lst[0][1] if len(lst) == 1 else jnp.concatenate([t[1] for t in lst], axis=1)

    return pl.pallas_call(
        body, name=name, grid=(D // tm,), in_specs=[pl.BlockSpec((tm, P_END), lambda i: (i, 0))],
        out_specs=pl.BlockSpec((4, tm, W_IN_SHARD), lambda i: (0, i, 0)),
        out_shape=jax.ShapeDtypeStruct((4, D, W_IN_SHARD), g.dtype), compiler_params=_par(1),
    )(g)


def _adam_update(w, g, m, v):
    mn = ADAM_B1 * m + (1.0 - ADAM_B1) * g
    vn = ADAM_B2 * v + (1.0 - ADAM_B2) * (g * g)
    m_hat = mn / (1.0 - ADAM_B1 ** ADAM_STEP)
    v_hat = vn / (1.0 - ADAM_B2 ** ADAM_STEP)
    return -ADAM_LR * (m_hat / (jnp.sqrt(v_hat) + ADAM_EPS) + ADAM_WD * w), mn, vn


def _adamw(w, g, m, v, name):
    shape = w.shape
    C = shape[-1]
    R = w.size // C
    tm = _pick(R, 512, 8)
    spec = pl.BlockSpec((tm, C), lambda i: (i, 0))

    def body(w_ref, g_ref, m_ref, v_ref, d_ref, mo_ref, vo_ref):
        d_ref[...], mo_ref[...], vo_ref[...] = _adam_update(w_ref[...], g_ref[...], m_ref[...], v_ref[...])

    outs = pl.pallas_call(
        body, name=name, grid=(R // tm,), in_specs=[spec] * 4, out_specs=[spec] * 3,
        out_shape=[jax.ShapeDtypeStruct((R, C), F32)] * 3, compiler_params=_par(1),
    )(*(t.reshape(R, C) for t in (w, g, m, v)))
    return tuple(o.reshape(shape) for o in outs)


def _adamw_layers(w, gs, m, v, name):
    L, R, C = w.shape
    tm = _pick(R, 128, 8)
    spec = pl.BlockSpec((None, tm, C), lambda l, i: (l, i, 0))
    g_specs = [pl.BlockSpec((tm, C), functools.partial(lambda ll, l, i: (jnp.where(l == ll, i, 0), 0), ll))
               for ll in range(L)]

    def body(w_ref, m_ref, v_ref, *rest):
        g_refs, (go_ref, d_ref, mo_ref, vo_ref) = rest[:L], rest[L:]
        l = pl.program_id(0)
        for ll in range(L):
            @pl.when(l == ll)
            def _():
                g = g_refs[ll][...]
                go_ref[...] = g
                d_ref[...], mo_ref[...], vo_ref[...] = _adam_update(w_ref[...], g, m_ref[...], v_ref[...])

    return pl.pallas_call(
        body, name=name, grid=(L, R // tm), in_specs=[spec] * 3 + g_specs, out_specs=[spec] * 4,
        out_shape=[jax.ShapeDtypeStruct((L, R, C), F32)] * 4, compiler_params=_arb(2),
    )(w, m, v, *gs)


def _add_blocks(a, ci, b, name):
    NB, _, R, C = a.shape

    def body(ci_ref, a_ref, b_ref, o_ref):
        o_ref[...] = a_ref[...] + b_ref[...]

    return pl.pallas_call(
        body, name=name,
        grid_spec=pltpu.PrefetchScalarGridSpec(
            num_scalar_prefetch=1, grid=(NB,),
            in_specs=[pl.BlockSpec((None, None, R, C), lambda k, ci_ref: (k, ci_ref[0], 0, 0)),
                      pl.BlockSpec((None, R, C), lambda k, ci_ref: (k, 0, 0))],
            out_specs=pl.BlockSpec((None, R, C), lambda k, ci_ref: (k, 0, 0))),
        out_shape=jax.ShapeDtypeStruct((NB, R, C), F32), compiler_params=_par(1),
    )(ci, a, b)


def _sum_own_and_received(p, chip, recv, name):
    _, R, C = p.shape
    tm = _pick(R, 256, 8)

    def body(chip_ref, p_ref, r_ref, o_ref):
        o_ref[...] = ((p_ref[...] + r_ref[0]) + r_ref[1]) + r_ref[2]

    return pl.pallas_call(
        body, name=name,
        grid_spec=pltpu.PrefetchScalarGridSpec(
            num_scalar_prefetch=1, grid=(R // tm,),
            in_specs=[pl.BlockSpec((None, tm, C), lambda i, chip_ref: (chip_ref[0], i, 0)),
                      pl.BlockSpec((3, tm, C), lambda i, chip_ref: (0, i, 0))],
            out_specs=pl.BlockSpec((tm, C), lambda i, chip_ref: (i, 0))),
        out_shape=jax.ShapeDtypeStruct((R, C), F32), compiler_params=_par(1),
    )(chip, p, recv)


def _position():
    return lax.axis_index("x"), lax.axis_index("y"), lax.axis_index("c")


def _other_chips(x, y):
    return [(1 - x, y), (x, 1 - y), (1 - x, 1 - y)]


HBM = pl.BlockSpec(memory_space=pl.ANY)


def _allgather8(blk, name, reduce_rows=None):
    M, N = blk.shape

    def body(x_ref, out_ref, *rest):
        if reduce_rows is None:
            send_sems, recv_sems, local_sem = rest
        else:
            sum_ref, send_sems, recv_sems, local_sem = rest
        x, y, c = _position()
        me, sibling = (x, y, c), (x, y, 1 - c)
        chips = _other_chips(x, y)

        def rows(px, py, pc):
            return out_ref.at[pl.ds((4 * px + 2 * py + pc) * M, M), :]

        def copy(k, block, to, src=None):
            return pltpu.make_async_remote_copy(
                src_ref=rows(*block) if src is None else src, dst_ref=rows(*block),
                send_sem=send_sems.at[k], recv_sem=recv_sems.at[k], device_id=to, device_id_type=MESH)

        mine = pltpu.make_async_copy(x_ref, rows(*me), local_sem)
        mine.start()
        first = [copy(0, me, sibling, src=x_ref)]
        first += [copy(1 + j, me, (*chip, c), src=x_ref) for j, chip in enumerate(chips)]
        for cp in first:
            cp.start()
        passed = [copy(4 + j, (*chip, c), sibling) for j, chip in enumerate(chips)]
        for j, chip in enumerate(chips):
            copy(1 + j, (*chip, c), me).wait_recv()
            passed[j].start()
        copy(0, sibling, me).wait_recv()
        for j, chip in enumerate(chips):
            copy(4 + j, (*chip, 1 - c), me).wait_recv()
        for cp in first + passed:
            cp.wait_send()
        mine.wait()
        if reduce_rows is not None:
            tot = out_ref[pl.ds(0, reduce_rows), :]
            for d in range(1, 8):
                tot = tot + out_ref[pl.ds(d * M, reduce_rows), :]
            sum_ref[...] = tot

    vmem = pl.BlockSpec(memory_space=pltpu.VMEM)
    out_shape = [jax.ShapeDtypeStruct((8 * M, N), blk.dtype)]
    if reduce_rows is not None:
        out_shape.append(jax.ShapeDtypeStruct((reduce_rows, N), blk.dtype))
    res = pl.pallas_call(
        body, name=name, out_shape=out_shape, in_specs=[vmem], out_specs=[vmem] * len(out_shape),
        scratch_shapes=[pltpu.SemaphoreType.DMA((7,)), pltpu.SemaphoreType.DMA((7,)), pltpu.SemaphoreType.DMA],
    )(blk)
    return res[0] if reduce_rows is None else (res[0], res[1])


def _gather_weights(shards, name):
    n = len(shards)

    def body(*refs):
        srcs, outs = refs[:n], refs[n:2 * n]
        send_sems, recv_sems, local_sems = refs[2 * n:]
        x, y, c = _position()
        me, sibling = (x, y, c), (x, y, 1 - c)
        chips = _other_chips(x, y)

        def region(t, px, py, pc):
            return outs[t].at[:, 2 * px + py, pc]

        def copy(t, k, block, to, src=None):
            return pltpu.make_async_remote_copy(
                src_ref=region(t, *block) if src is None else src, dst_ref=region(t, *block),
                send_sem=send_sems.at[7 * t + k], recv_sem=recv_sems.at[7 * t + k], device_id=to, device_id_type=MESH)

        mine = [pltpu.make_async_copy(srcs[t].at[:, c], region(t, *me), local_sems.at[t]) for t in range(n)]
        for cp in mine:
            cp.start()
        first = []
        for t in range(n):
            first.append(copy(t, 0, me, sibling, src=srcs[t].at[:, c]))
            first += [copy(t, 1 + j, me, (*chip, c), src=srcs[t].at[:, c]) for j, chip in enumerate(chips)]
        for cp in first:
            cp.start()
        passed = []
        for j, chip in enumerate(chips):
            for t in range(n):
                copy(t, 1 + j, (*chip, c), me).wait_recv()
                cp = copy(t, 4 + j, (*chip, c), sibling)
                cp.start()
                passed.append(cp)
        for t in range(n):
            copy(t, 0, sibling, me).wait_recv()
        for j, chip in enumerate(chips):
            for t in range(n):
                copy(t, 4 + j, (*chip, 1 - c), me).wait_recv()
        for cp in first + passed:
            cp.wait_send()
        for cp in mine:
            cp.wait()

    out_shape = [jax.ShapeDtypeStruct((s.shape[0], 4) + s.shape[1:], s.dtype) for s in shards]
    return pl.pallas_call(
        body, name=name, out_shape=out_shape, in_specs=[HBM] * n, out_specs=[HBM] * n,
        scratch_shapes=[pltpu.SemaphoreType.DMA((7 * n,)), pltpu.SemaphoreType.DMA((7 * n,)),
                        pltpu.SemaphoreType.DMA((n,))],
    )(*shards)


def _rs_sibling(gs, name):
    n = len(gs)

    def body(*refs):
        srcs, outs = refs[:n], refs[n:2 * n]
        send_sems, recv_sems = refs[2 * n:]
        x, y, c = _position()
        copies = [pltpu.make_async_remote_copy(
            src_ref=srcs[t].at[k, 1 - c], dst_ref=outs[t].at[k], send_sem=send_sems.at[4 * t + k],
            recv_sem=recv_sems.at[4 * t + k], device_id=(x, y, 1 - c), device_id_type=MESH)
            for t in range(n) for k in range(4)]
        for cp in copies:
            cp.start()
        for cp in copies:
            cp.wait()

    out_shape = [jax.ShapeDtypeStruct((4,) + g.shape[2:], g.dtype) for g in gs]
    return pl.pallas_call(
        body, name=name, out_shape=out_shape, in_specs=[HBM] * n, out_specs=[HBM] * n,
        scratch_shapes=[pltpu.SemaphoreType.DMA((4 * n,)), pltpu.SemaphoreType.DMA((4 * n,))],
    )(*gs)


def _rs_chips(ps, name):
    n = len(ps)

    def body(*refs):
        srcs, outs = refs[:n], refs[n:2 * n]
        send_sems, recv_sems = refs[2 * n:]
        x, y, c = _position()
        copies = [pltpu.make_async_remote_copy(
            src_ref=srcs[t].at[2 * px + py], dst_ref=outs[t].at[j], send_sem=send_sems.at[3 * t + j],
            recv_sem=recv_sems.at[3 * t + j], device_id=(px, py, c), device_id_type=MESH)
            for t in range(n) for j, (px, py) in enumerate(_other_chips(x, y))]
        for cp in copies:
            cp.start()
        for cp in copies:
            cp.wait()

    out_shape = [jax.ShapeDtypeStruct((3,) + p.shape[1:], p.dtype) for p in ps]
    return pl.pallas_call(
        body, name=name, out_shape=out_shape, in_specs=[HBM] * n, out_specs=[HBM] * n,
        scratch_shapes=[pltpu.SemaphoreType.DMA((3 * n,)), pltpu.SemaphoreType.DMA((3 * n,))],
    )(*ps)


def _rs_pair(hs, name):
    n = len(hs)

    def body(*refs):
        srcs, outs = refs[:n], refs[n:2 * n]
        send_sems, recv_sems, local_sems = refs[2 * n:]
        x, y, c = _position()
        mine = [pltpu.make_async_copy(srcs[t], outs[t].at[c], local_sems.at[t]) for t in range(n)]
        for cp in mine:
            cp.start()
        sends = [pltpu.make_async_remote_copy(
            src_ref=srcs[t], dst_ref=outs[t].at[c], send_sem=send_sems.at[t], recv_sem=recv_sems.at[t],
            device_id=(x, y, 1 - c), device_id_type=MESH) for t in range(n)]
        for cp in sends:
            cp.start()
        for t in range(n):
            pltpu.make_async_remote_copy(
                src_ref=srcs[t], dst_ref=outs[t].at[1 - c], send_sem=send_sems.at[t], recv_sem=recv_sems.at[t],
                device_id=(x, y, 1 - c), device_id_type=MESH).wait_recv()
        for cp in sends:
            cp.wait_send()
        for cp in mine:
            cp.wait()

    out_shape = [jax.ShapeDtypeStruct((2,) + h.shape, h.dtype) for h in hs]
    return pl.pallas_call(
        body, name=name, out_shape=out_shape, in_specs=[HBM] * n, out_specs=[HBM] * n,
        scratch_shapes=[pltpu.SemaphoreType.DMA((n,)), pltpu.SemaphoreType.DMA((n,)), pltpu.SemaphoreType.DMA((n,))],
    )(*hs)


BIG = ("w_in", "w_branch_a", "w_branch_b", "w_out", "w_ffn_in", "w_ffn_out")


def _band_bias(rel_table):
    band = (A_PAST + 1) * CHUNK
    qpos = A_PAST * CHUNK + np.arange(CHUNK)
    rel = np.clip(qpos[:, None] - np.arange(band)[None, :], -A_MAX_REL, A_MAX_REL) + A_MAX_REL
    bias = rel_table[:, :, rel]
    big = jnp.full(rel_table.shape[:2] + (QBLK, KSPAN), NEG, F32)
    for a in range(QBLK // CHUNK):
        big = big.at[:, :, a * CHUNK:(a + 1) * CHUNK, a * CHUNK:a * CHUNK + band].set(bias)
    return big


def _col_row_forms(t, S):
    nc = S // CHUNK
    return t.T.reshape(nc, CHUNK, B_HEADS), t.reshape(B_HEADS, nc, CHUNK).transpose(1, 0, 2)


def _layer_fwd(l, x, mod, W, P, big):
    S, D = x.shape
    n = lambda s: f"{s}_l{l}"
    sh1, sc1, gt1, sh2, sc2, gt2 = (mod[i:i + 1] for i in range(6))
    h1 = _lnmod_fwd(x, P["norm1_g"][l:l + 1], sc1, sh1, n("ln1"))
    proj = _matmul(h1, W["w_in"], "nn", F32, n("proj"), tn=1152, layer=l)
    ya = _attn_fwd(proj, big, n("attn"))
    ba = proj[:, P_BA:P_BA + 2 * B_HEADS]
    b_t, a_t = ba[:, :B_HEADS].T, ba[:, B_HEADS:].T
    alog, dtb = P["a_log"][l].reshape(B_HEADS, 1), P["dt_bias"][l].reshape(B_HEADS, 1)
    beta, gam = _gdn_gates_fwd(b_t, a_t, alog, dtb, n("gates"))
    bcol, _ = _col_row_forms(beta, S)
    gcol, grow = _col_row_forms(gam, S)
    qn, kn, v = _gdn_pre_fwd(proj, P["w_conv"][l], n("gdnpre"))
    o, tsave, ssave = _gdn_chunk_fwd(qn, kn, v, bcol, gcol, grow, n("gdnchunk"))
    yb = _gdn_post_fwd(o, proj, P["gdn_norm_g"][l:l + 1], n("gdnpost"))
    pa = _matmul(ya, W["w_branch_a"], "nn", F32, n("pa"), layer=l, stacked=True)
    pb = _matmul(yb, W["w_branch_b"], "nn", F32, n("pb"), layer=l, stacked=True)
    merged = _merge_fwd(proj, pa, pb, n("merge"))
    ao = _matmul(merged, W["w_out"], "nn", F32, n("ao"), layer=l)
    x1 = _gate_fwd(x, ao, gt1, n("res1"))
    h2 = _lnmod_fwd(x1, P["norm2_g"][l:l + 1], sc2, sh2, n("ln2"))
    gu = _matmul(h2, W["w_ffn_in"], "nn", F32, n("gu"), layer=l, stacked=True)
    act = _ffn_act_fwd(gu, n("act"))
    fo = _matmul(act, W["w_ffn_out"], "nn", F32, n("fo"), tk=1408, layer=l)
    x2 = _gate_fwd(x1, fo, gt2, n("res2"))
    saved = dict(x=x, h1=h1, proj=proj, ya=ya, b_t=b_t, a_t=a_t, bcol=bcol, gcol=gcol, grow=grow,
                 qn=qn, kn=kn, v=v, o=o, tsave=tsave, ssave=ssave, yb=yb, pa=pa, pb=pb, merged=merged, ao=ao, x1=x1,
                 h2=h2, gu=gu, act=act, fo=fo)
    return x2, saved


def _layer_bwd(l, dx2, sv, mod, W, P, big):
    S, D = dx2.shape
    n = lambda s: f"{s}_l{l}"
    sh1, sc1, gt1, sh2, sc2, gt2 = (mod[i:i + 1] for i in range(6))
    g = {}
    view = lambda t: t.reshape((4, 2, t.shape[-2] // (2 if t.ndim == 3 else 8), t.shape[-1]))
    dz2, dgt2 = _gate_bwd(dx2, sv["fo"], gt2, n("dres2"))
    g["w_ffn_out"] = view(_matmul(sv["act"], dz2, "tn", F32, n("dwfo"), tm=1408))
    dact = _matmul(dz2, W["w_ffn_out"], "nt", F32, n("dact"), tn=1408, layer=l)
    dgu = _ffn_act_bwd(sv["gu"], dact, n("dgu"))
    g["w_ffn_in"] = view(_matmul(sv["h2"], dgu, "tn", F32, n("dwfi"), out_stacked=True))
    dh2 = _matmul(dgu, W["w_ffn_in"], "nt", F32, n("dh2"), layer=l, stacked=True)
    dx1, dsh2, dsc2, dn2 = _lnmod_bwd(dh2, sv["x1"], P["norm2_g"][l:l + 1], sc2, dx2, n("dln2"))
    dz1, dgt1 = _gate_bwd(dx1, sv["ao"], gt1, n("dres1"))
    g["w_out"] = view(_matmul(sv["merged"], dz1, "tn", F32, n("dwo")))
    dmerged = _matmul(dz1, W["w_out"], "nt", F32, n("dmerged"), layer=l)
    dgab, dpa, dpb = _merge_bwd(sv["proj"], sv["pa"], sv["pb"], dmerged, n("dmerge"))
    g["w_branch_a"] = view(_matmul(sv["ya"], dpa, "tn", F32, n("dwa"), out_stacked=True))
    g["w_branch_b"] = view(_matmul(sv["yb"], dpb, "tn", F32, n("dwb"), out_stacked=True))
    dya = _matmul(dpa, W["w_branch_a"], "nt", BF16, n("dya"), layer=l, stacked=True)
    dyb = _matmul(dpb, W["w_branch_b"], "nt", F32, n("dyb"), layer=l, stacked=True)
    dq, dk, dv, dbig = _attn_bwd(sv["proj"], big, dya, n("dattn"))
    g["rel_table"] = _rel_table_grad(dbig, n("drel"))[:, 0, :2 * A_MAX_REL + 1]
    do, dzb, dng = _gdn_post_bwd(dyb, sv["o"], sv["proj"], P["gdn_norm_g"][l:l + 1], n("dgdnpost"))
    g["gdn_norm_g"] = dng[0]
    dqn, dkn, dvv, dbc, dgc, dgr = _gdn_chunk_bwd(sv["qn"], sv["kn"], sv["v"], sv["bcol"], sv["gcol"], sv["grow"],
                                                  sv["tsave"], sv["ssave"], do, n("dgdnchunk"))
    dbeta_t = dbc.reshape(S, B_HEADS).T
    dgam_a = dgc.reshape(S, B_HEADS).T
    dgam_b = dgr.transpose(1, 0, 2).reshape(B_HEADS, S)
    alog, dtb = P["a_log"][l].reshape(B_HEADS, 1), P["dt_bias"][l].reshape(B_HEADS, 1)
    db_t, da_t, dal, ddt = _gdn_gates_bwd(dbeta_t, dgam_a, dgam_b, sv["b_t"], sv["a_t"], alog, dtb, n("dgates"))
    g["a_log"], g["dt_bias"] = dal[:, 0], ddt[:, 0]
    dy = _gdn_pre_bwd_a(sv["proj"], P["w_conv"][l], dqn, dkn, dvv, n("dgdnpre_a"))
    dqkvb, g["w_conv"] = _gdn_pre_bwd_b(sv["proj"], P["w_conv"][l], dy, n("dgdnpre_b"))
    dba = jnp.concatenate([db_t.T, da_t.T, jnp.zeros((S, P_END - P_BA - 2 * B_HEADS), F32)], axis=1)
    dproj = jnp.concatenate([dq.astype(BF16), dk.astype(BF16), dv.astype(BF16), dqkvb, dgab, dzb, dba.astype(BF16)],
                            axis=1)
    g["w_in"] = view(_restore_w_in(_matmul(sv["h1"], dproj, "tn", F32, n("dwin"), tn=1152), n("dwin_cols")))
    dh1 = _matmul(dproj, W["w_in"], "nt", F32, n("dh1"), tk=1152, layer=l)
    dx, dsh1, dsc1, dn1 = _lnmod_bwd(dh1, sv["x"], P["norm1_g"][l:l + 1], sc1, dx1, n("dln1"))
    g["norm1_g"], g["norm2_g"] = dn1[0], dn2[0]
    dmod = jnp.concatenate([dsh1, dsc1, dgt1, dsh2, dsc2, dgt2], axis=1)[0]
    return dx, g, dmod


SMALL = ("norm1_g", "norm2_g", "rel_table", "w_conv", "a_log", "dt_bias", "gdn_norm_g")
SMALL_PACK_C = 1024


def _as_rows(t):
    flat = t.reshape(-1)
    rows = -(-flat.shape[0] // SMALL_PACK_C)
    return jnp.pad(flat, (0, rows * SMALL_PACK_C - flat.shape[0])).reshape(rows, SMALL_PACK_C)


def _pack_rows(parts):
    blk = jnp.concatenate([_as_rows(p) for p in parts], axis=0)
    return jnp.pad(blk, ((0, -blk.shape[0] % 8), (0, 0)))


def _unpack_rows(blk, shapes):
    out, r = [], 0
    for shp in shapes:
        size = int(np.prod(shp))
        rows = -(-size // SMALL_PACK_C)
        out.append(blk[..., r:r + rows, :].reshape(blk.shape[:-2] + (rows * SMALL_PACK_C,))[..., :size]
                   .reshape(blk.shape[:-2] + tuple(shp)))
        r += rows
    return out


def kernel(x, c, w_ada, b_ada, norm1_g, norm2_g, w_in, rel_table, w_conv, a_log, dt_bias, gdn_norm_g, w_branch_a, w_branch_b, w_out, w_ffn_in, w_ffn_out, final_g, loss_target, m_w_ada, m_b_ada, m_norm1_g, m_norm2_g, m_w_in, m_rel_table, m_w_conv, m_a_log, m_dt_bias, m_gdn_norm_g, m_w_branch_a, m_w_branch_b, m_w_out, m_w_ffn_in, m_w_ffn_out, m_final_g, v_w_ada, v_b_ada, v_norm1_g, v_norm2_g, v_w_in, v_rel_table, v_w_conv, v_a_log, v_dt_bias, v_gdn_norm_g, v_w_branch_a, v_w_branch_b, v_w_out, v_w_ffn_in, v_w_ffn_out, v_final_g):
    weights = dict(w_ada=w_ada, b_ada=b_ada, norm1_g=norm1_g, norm2_g=norm2_g, w_in=w_in, rel_table=rel_table,
                   w_conv=w_conv, a_log=a_log, dt_bias=dt_bias, gdn_norm_g=gdn_norm_g, w_branch_a=w_branch_a,
                   w_branch_b=w_branch_b, w_out=w_out, w_ffn_in=w_ffn_in, w_ffn_out=w_ffn_out, final_g=final_g)
    mom_m = dict(w_ada=m_w_ada, b_ada=m_b_ada, norm1_g=m_norm1_g, norm2_g=m_norm2_g, w_in=m_w_in,
                 rel_table=m_rel_table, w_conv=m_w_conv, a_log=m_a_log, dt_bias=m_dt_bias, gdn_norm_g=m_gdn_norm_g,
                 w_branch_a=m_w_branch_a, w_branch_b=m_w_branch_b, w_out=m_w_out, w_ffn_in=m_w_ffn_in,
                 w_ffn_out=m_w_ffn_out, final_g=m_final_g)
    mom_v = dict(w_ada=v_w_ada, b_ada=v_b_ada, norm1_g=v_norm1_g, norm2_g=v_norm2_g, w_in=v_w_in,
                 rel_table=v_rel_table, w_conv=v_w_conv, a_log=v_a_log, dt_bias=v_dt_bias, gdn_norm_g=v_gdn_norm_g,
                 w_branch_a=v_w_branch_a, w_branch_b=v_w_branch_b, w_out=v_w_out, w_ffn_in=v_w_ffn_in,
                 w_ffn_out=v_w_ffn_out, final_g=v_final_g)
    xi, yi, ci = _position()
    chip = 2 * xi + yi
    dev = 2 * chip + ci
    L, D = norm1_g.shape
    NMOD = b_ada.shape[1] // D
    ns = w_ada.shape[2]
    cs = w_conv.shape[2]

    first_blk = _pack_rows([c, w_conv])
    first_all = _allgather8(first_blk, "gather_c").reshape(8, first_blk.shape[0], SMALL_PACK_C)
    c_all, w_conv_all = _unpack_rows(first_all, [(D,), w_conv.shape])
    w_conv_full = w_conv_all.reshape(4, 2, L, CONV_K, cs)[:, 0].transpose(1, 2, 0, 3).reshape(L, CONV_K, 4 * cs)
    b_shard = lax.dynamic_slice_in_dim(b_ada, chip * ns, ns, axis=1).reshape(L, 1, ns)
    mod_shard = _ada_mod(c_all, w_ada, b_shard, "ada_mod")
    mod_all = _allgather8(mod_shard.reshape(L * 8, ns), "gather_mod").reshape(4, 2, L, 8, ns)
    mod = lax.dynamic_index_in_dim(mod_all[:, 0], dev, axis=2, keepdims=False)
    mod = mod.transpose(1, 0, 2).reshape(L, NMOD, D)

    shards = [weights[k].astype(BF16) for k in BIG]
    shards = [s.reshape(s.shape[0], 2, s.shape[1] // 2, s.shape[2]) for s in shards]
    gathered = dict(zip(BIG, _gather_weights(shards, "gather_weights")))
    col_stacked = lambda t: t.reshape(t.shape[0], 4, 2 * t.shape[3], t.shape[4])
    row_joined = lambda t: t.reshape(t.shape[0], 8 * t.shape[3], t.shape[4])
    W = dict(w_in=_reorder_w_in(col_stacked(gathered["w_in"]), "w_in_cols"),
             w_branch_a=col_stacked(gathered["w_branch_a"]), w_branch_b=col_stacked(gathered["w_branch_b"]),
             w_ffn_in=col_stacked(gathered["w_ffn_in"]), w_out=row_joined(gathered["w_out"]),
             w_ffn_out=row_joined(gathered["w_ffn_out"]))
    P = dict(norm1_g=norm1_g, norm2_g=norm2_g, w_conv=w_conv_full, a_log=a_log, dt_bias=dt_bias,
             gdn_norm_g=gdn_norm_g)
    big = _band_bias(rel_table)

    xc = x[0]
    saved = []
    for l in range(L):
        xc, sv = _layer_fwd(l, xc, mod[l], W, P, big[l])
        saved.append(sv)
    dx, loss_dev, dfinal = _loss_head(xc, final_g.reshape(1, D), loss_target[0], "loss_head")

    ci1 = ci.astype(jnp.int32).reshape(1)
    chip1 = chip.astype(jnp.int32).reshape(1)
    grads = [None] * L
    dmods = [None] * L
    shard_grads = {k: [None] * L for k in BIG}
    for l in reversed(range(L)):
        dx, grads[l], dmods[l] = _layer_bwd(l, dx, saved[l], mod[l], W, P, big[l])
        gs = [grads[l][k] for k in BIG]
        from_sibling = _rs_sibling(gs, f"rs_sibling_l{l}")
        pair_sums = [_add_blocks(g_, ci1, r_, f"rs_pair_sum_{k}_l{l}") for k, g_, r_ in zip(BIG, gs, from_sibling)]
        from_chips = _rs_chips(pair_sums, f"rs_chips_l{l}")
        halves = [_sum_own_and_received(p_, chip1, r_, f"rs_sum_{k}_l{l}")
                  for k, p_, r_ in zip(BIG, pair_sums, from_chips)]
        for k, t in zip(BIG, _rs_pair(halves, f"rs_pair_l{l}")):
            shard_grads[k][l] = t.reshape(2 * t.shape[1], t.shape[2])
    dmod = jnp.stack(dmods)

    small = {k: jnp.stack([grads[l][k] for l in range(L)]) for k in SMALL}
    parts = [dmod] + [small[k] for k in SMALL] + [dfinal, loss_dev[0, :1]]
    small_blk = _pack_rows(parts)
    srows = small_blk.shape[0]
    small_all, small_sum = _allgather8(small_blk, "gather_small", reduce_rows=srows)
    shapes = [dmod.shape] + [small[k].shape for k in SMALL] + [(D,), (1,)]
    tot = _unpack_rows(small_sum, shapes)
    G = dict(zip(SMALL, tot[1:1 + len(SMALL)]))
    G["b_ada"] = tot[0].reshape(b_ada.shape)
    G["w_conv"] = lax.dynamic_slice_in_dim(G["w_conv"], chip * cs, cs, axis=2)
    G["final_g"] = tot[-2]
    loss = tot[-1][0]
    dmod_all = _unpack_rows(small_all.reshape(8, srows, SMALL_PACK_C), [dmod.shape])[0]
    dmod_cols = lax.dynamic_slice_in_dim(dmod_all, chip * ns, ns, axis=2).transpose(1, 0, 2)
    G["w_ada"] = _ada_wgrad(c_all, dmod_cols, "ada_wgrad")

    order = ["w_ada", "b_ada", "norm1_g", "norm2_g", "w_in", "rel_table", "w_conv", "a_log", "dt_bias", "gdn_norm_g",
             "w_branch_a", "w_branch_b", "w_out", "w_ffn_in", "w_ffn_out", "final_g"]
    deltas, new_m, new_v = {}, {}, {}
    for k in order:
        w = weights[k]
        if k in BIG:
            G[k], deltas[k], new_m[k], new_v[k] = _adamw_layers(w, shard_grads[k], mom_m[k], mom_v[k], f"adamw_{k}")
            continue
        as2d = (lambda t: t.reshape(1, -1)) if w.ndim == 1 else (lambda t: t)
        d_, m_, v_ = _adamw(as2d(w), as2d(G[k]), as2d(mom_m[k]), as2d(mom_v[k]), f"adamw_{k}")
        deltas[k], new_m[k], new_v[k] = d_.reshape(w.shape), m_.reshape(w.shape), v_.reshape(w.shape)
    return (loss, dx[None], *[G[k] for k in order], *[deltas[k] for k in order], *[new_m[k] for k in order],
            *[new_v[k] for k in order])
```

```python
import functools

import numpy as np
import jax
import jax.numpy as jnp
from jax import lax
from jax.experimental import pallas as pl
from jax.experimental.pallas import tpu as pltpu

F32 = jnp.float32
BF16 = jnp.bfloat16
HI = lax.Precision.HIGHEST
SOLVE_PREC = lax.Precision.HIGH
MESH = pl.DeviceIdType.MESH

EPS = 1e-6
CHUNK = 64
A_HEADS = 8
A_DH = 64
A_PAST = 8
A_MAX_REL = 128
B_HEADS = 4
B_DH = 128
CONV_K = 4
LANE = 128
QBLK = 4 * CHUNK
KSPAN = QBLK + A_PAST * CHUNK
NEG = -1e30

ADAM_LR = 0.001
ADAM_B1 = 0.9
ADAM_B2 = 0.999
ADAM_EPS = 1e-08
ADAM_WD = 0.01
ADAM_STEP = 10

P_QKVA, P_QKVB, P_GA, P_GB, P_Z, P_BA, P_END = 0, 1536, 3072, 4096, 5120, 5632, 5760
W_IN_SHARD = 1410


def _sigmoid(x):
    return 1.0 / (1.0 + jnp.exp(-x))


def _nn(a, b, prec=None):
    return lax.dot_general(a, b, (((1,), (0,)), ((), ())), precision=prec, preferred_element_type=F32)


def _nt(a, b, prec=None):
    return lax.dot_general(a, b, (((1,), (1,)), ((), ())), precision=prec, preferred_element_type=F32)


def _tn(a, b, prec=None):
    return lax.dot_general(a, b, (((0,), (0,)), ((), ())), precision=prec, preferred_element_type=F32)


def _bnn(a, b):
    return _nn(a.astype(BF16), b.astype(BF16))


def _bnt(a, b):
    return _nt(a.astype(BF16), b.astype(BF16))


def _btn(a, b):
    return _tn(a.astype(BF16), b.astype(BF16))


def _pick(n, target, unit=LANE):
    best = None
    for t in range(unit, min(n, target) + 1, unit):
        if n % t == 0:
            best = t
    return best if best is not None else n


def _acc(ref, val, i):
    @pl.when(i == 0)
    def _():
        ref[...] = val

    @pl.when(i != 0)
    def _():
        ref[...] += val


def _arb(n):
    return pltpu.CompilerParams(dimension_semantics=("arbitrary",) * n)


def _par(n):
    return pltpu.CompilerParams(dimension_semantics=("parallel",) * n)


def _matmul(a, b, mode, out_dtype, name, tm=1024, tn=1024, tk=1024, layer=None, stacked=False, out_stacked=False):
    bs = b.shape[1:] if layer is not None else b.shape
    if mode == "nn":
        M, K = a.shape
        N = 4 * bs[2] if stacked else bs[1]
        if stacked:
            tn = bs[2]
    elif mode == "nt":
        M, K = a.shape
        N = bs[1] if stacked else bs[0]
        if stacked:
            tk = bs[2]
    else:
        K, M = a.shape
        N = bs[1]
        if out_stacked:
            tn = N // 4
    tm, tn, tk = _pick(M, tm), _pick(N, tn), _pick(K, tk)
    nk = K // tk
    lead = () if layer is None else (layer,)
    lead_blk = () if layer is None else (None,)
    if mode == "nn":
        a_spec = pl.BlockSpec((tm, tk), lambda i, j, k: (i, k))
        if stacked:
            b_spec = pl.BlockSpec(lead_blk + (None, tk, tn), lambda i, j, k: lead + (j, k, 0))
        else:
            b_spec = pl.BlockSpec(lead_blk + (tk, tn), lambda i, j, k: lead + (k, j))
        dot = _nn
    elif mode == "nt":
        a_spec = pl.BlockSpec((tm, tk), lambda i, j, k: (i, k))
        if stacked:
            b_spec = pl.BlockSpec(lead_blk + (None, tn, tk), lambda i, j, k: lead + (k, j, 0))
        else:
            b_spec = pl.BlockSpec(lead_blk + (tn, tk), lambda i, j, k: lead + (j, k))
        dot = _nt
    else:
        a_spec = pl.BlockSpec((tk, tm), lambda i, j, k: (k, i))
        b_spec = pl.BlockSpec((tk, tn), lambda i, j, k: (k, j))
        dot = _tn
    if out_stacked:
        o_spec = pl.BlockSpec((None, tm, tn), lambda i, j, k: (j, i, 0))
        o_shape = jax.ShapeDtypeStruct((4, M, tn), out_dtype)
    else:
        o_spec = pl.BlockSpec((tm, tn), lambda i, j, k: (i, j))
        o_shape = jax.ShapeDtypeStruct((M, N), out_dtype)

    def body_single(a_ref, b_ref, o_ref):
        o_ref[...] = dot(a_ref[...], b_ref[...]).astype(out_dtype)

    def body(a_ref, b_ref, o_ref, acc_ref):
        k = pl.program_id(2)

        @pl.when(k == 0)
        def _():
            acc_ref[...] = jnp.zeros_like(acc_ref)

        acc_ref[...] += dot(a_ref[...], b_ref[...])

        @pl.when(k == nk - 1)
        def _():
            o_ref[...] = acc_ref[...].astype(out_dtype)

    return pl.pallas_call(
        body_single if nk == 1 else body, name=name, grid=(M // tm, N // tn, nk), in_specs=[a_spec, b_spec],
        out_specs=o_spec, out_shape=o_shape, scratch_shapes=[] if nk == 1 else [pltpu.VMEM((tm, tn), F32)],
        compiler_params=pltpu.CompilerParams(dimension_semantics=("parallel", "parallel", "arbitrary")),
    )(a, b)


def _rows(tm, n, col=0):
    return pl.BlockSpec((tm, n), lambda i: (i, col))


def _vec(n):
    return pl.BlockSpec((1, n), lambda i: (0, 0))


def _lnmod_fwd(x, g, sc, sh, name):
    S, D = x.shape
    tm = _pick(S, 512, 8)

    def body(x_ref, g_ref, sc_ref, sh_ref, o_ref):
        xv = x_ref[...]
        r = lax.rsqrt(jnp.mean(xv * xv, axis=-1, keepdims=True) + EPS)
        o_ref[...] = ((xv * r * g_ref[...]) * (1.0 + sc_ref[...]) + sh_ref[...]).astype(BF16)

    return pl.pallas_call(
        body, name=name, grid=(S // tm,),
        in_specs=[_rows(tm, D), _vec(D), _vec(D), _vec(D)], out_specs=_rows(tm, D),
        out_shape=jax.ShapeDtypeStruct((S, D), BF16), compiler_params=_par(1),
    )(x, g, sc, sh)


def _lnmod_bwd(dh, x, g, sc, dres, name):
    S, D = x.shape
    tm = _pick(S, 512, 8)

    def body(dh_ref, x_ref, g_ref, sc_ref, dres_ref, dx_ref, dsh_ref, dsc_ref, dg_ref):
        i = pl.program_id(0)
        xv = x_ref[...]
        dh_ = dh_ref[...]
        r = lax.rsqrt(jnp.mean(xv * xv, axis=-1, keepdims=True) + EPS)
        xhat = xv * r
        gv = g_ref[...]
        dn = dh_ * (1.0 + sc_ref[...])
        dxhat = dn * gv
        dx_ref[...] = dres_ref[...] + r * (dxhat - xhat * jnp.mean(dxhat * xhat, axis=-1, keepdims=True))
        _acc(dsh_ref, jnp.sum(dh_, axis=0, keepdims=True), i)
        _acc(dsc_ref, jnp.sum(dh_ * (xhat * gv), axis=0, keepdims=True), i)
        _acc(dg_ref, jnp.sum(dn * xhat, axis=0, keepdims=True), i)

    return pl.pallas_call(
        body, name=name, grid=(S // tm,),
        in_specs=[_rows(tm, D), _rows(tm, D), _vec(D), _vec(D), _rows(tm, D)],
        out_specs=[_rows(tm, D), _vec(D), _vec(D), _vec(D)],
        out_shape=[jax.ShapeDtypeStruct((S, D), F32)] + [jax.ShapeDtypeStruct((1, D), F32)] * 3,
        compiler_params=_arb(1),
    )(dh, x, g, sc, dres)


def _gate_fwd(x, y, gt, name):
    S, D = x.shape
    tm = _pick(S, 512, 8)

    def body(x_ref, y_ref, gt_ref, o_ref):
        o_ref[...] = x_ref[...] + gt_ref[...] * y_ref[...]

    return pl.pallas_call(
        body, name=name, grid=(S // tm,), in_specs=[_rows(tm, D), _rows(tm, D), _vec(D)], out_specs=_rows(tm, D),
        out_shape=jax.ShapeDtypeStruct((S, D), F32), compiler_params=_par(1),
    )(x, y, gt)


def _gate_bwd(dx, y, gt, name):
    S, D = dx.shape
    tm = _pick(S, 512, 8)

    def body(dx_ref, y_ref, gt_ref, dz_ref, dgt_ref):
        i = pl.program_id(0)
        d = dx_ref[...]
        dz_ref[...] = (d * gt_ref[...]).astype(BF16)
        _acc(dgt_ref, jnp.sum(d * y_ref[...], axis=0, keepdims=True), i)

    return pl.pallas_call(
        body, name=name, grid=(S // tm,), in_specs=[_rows(tm, D), _rows(tm, D), _vec(D)],
        out_specs=[_rows(tm, D), _vec(D)],
        out_shape=[jax.ShapeDtypeStruct((S, D), BF16), jax.ShapeDtypeStruct((1, D), F32)],
        compiler_params=_arb(1),
    )(dx, y, gt)


def _ffn_act_fwd(gu, name):
    S, H2 = gu.shape
    H = H2 // 2
    tm = _pick(S, 256, 8)

    def body(g_ref, u_ref, o_ref):
        gv = g_ref[...]
        o_ref[...] = (gv * _sigmoid(gv) * u_ref[...]).astype(BF16)

    return pl.pallas_call(
        body, name=name, grid=(S // tm,), in_specs=[_rows(tm, H, 0), _rows(tm, H, 1)], out_specs=_rows(tm, H),
        out_shape=jax.ShapeDtypeStruct((S, H), BF16), compiler_params=_par(1),
    )(gu, gu)


def _ffn_act_bwd(gu, dact, name):
    S, H2 = gu.shape
    H = H2 // 2
    tm = _pick(S, 256, 8)

    def body(g_ref, u_ref, da_ref, o_ref):
        gv = g_ref[...]
        s = _sigmoid(gv)
        da = da_ref[...]
        o_ref[:, :H] = (da * u_ref[...] * (s * (1.0 + gv * (1.0 - s)))).astype(BF16)
        o_ref[:, H:] = (da * (gv * s)).astype(BF16)

    return pl.pallas_call(
        body, name=name, grid=(S // tm,), in_specs=[_rows(tm, H, 0), _rows(tm, H, 1), _rows(tm, H)],
        out_specs=_rows(tm, H2), out_shape=jax.ShapeDtypeStruct((S, H2), BF16), compiler_params=_par(1),
    )(gu, gu, dact)


def _merge_fwd(proj, pa, pb, name):
    S, D = pa.shape
    tm = _pick(S, 512, 8)

    def body(ga_ref, gb_ref, pa_ref, pb_ref, o_ref):
        o_ref[...] = (_sigmoid(ga_ref[...]) * pa_ref[...] + _sigmoid(gb_ref[...]) * pb_ref[...]).astype(BF16)

    return pl.pallas_call(
        body, name=name, grid=(S // tm,),
        in_specs=[_rows(tm, D, P_GA // D), _rows(tm, D, P_GB // D), _rows(tm, D), _rows(tm, D)],
        out_specs=_rows(tm, D), out_shape=jax.ShapeDtypeStruct((S, D), BF16), compiler_params=_par(1),
    )(proj, proj, pa, pb)


def _merge_bwd(proj, pa, pb, dm, name):
    S, D = pa.shape
    tm = _pick(S, 512, 8)

    def body(ga_ref, gb_ref, pa_ref, pb_ref, dm_ref, dg_ref, dpa_ref, dpb_ref):
        d = dm_ref[...]
        sa = _sigmoid(ga_ref[...])
        sb = _sigmoid(gb_ref[...])
        dg_ref[:, :D] = (d * pa_ref[...] * sa * (1.0 - sa)).astype(BF16)
        dg_ref[:, D:] = (d * pb_ref[...] * sb * (1.0 - sb)).astype(BF16)
        dpa_ref[...] = (d * sa).astype(BF16)
        dpb_ref[...] = (d * sb).astype(BF16)

    return pl.pallas_call(
        body, name=name, grid=(S // tm,),
        in_specs=[_rows(tm, D, P_GA // D), _rows(tm, D, P_GB // D), _rows(tm, D), _rows(tm, D), _rows(tm, D)],
        out_specs=[_rows(tm, 2 * D), _rows(tm, D), _rows(tm, D)],
        out_shape=[jax.ShapeDtypeStruct((S, 2 * D), BF16), jax.ShapeDtypeStruct((S, D), BF16),
                   jax.ShapeDtypeStruct((S, D), BF16)],
        compiler_params=_par(1),
    )(proj, proj, pa, pb, dm)


def _loss_head(x, g, target, name):
    S, D = x.shape
    tm = _pick(S, 512, 8)

    def body(x_ref, g_ref, t_ref, dx_ref, loss_ref, dg_ref):
        i = pl.program_id(0)
        xv = x_ref[...]
        gv = g_ref[...]
        r = lax.rsqrt(jnp.mean(xv * xv, axis=-1, keepdims=True) + EPS)
        xhat = xv * r
        err = xhat * gv - t_ref[...]
        part = 0.5 * jnp.sum(jnp.mean(err * err, axis=-1, keepdims=True), axis=0, keepdims=True)
        _acc(loss_ref, jnp.broadcast_to(part, (1, LANE)), i)
        dy = err * (1.0 / D)
        _acc(dg_ref, jnp.sum(dy * xhat, axis=0, keepdims=True), i)
        dxhat = dy * gv
        dx_ref[...] = r * (dxhat - xhat * jnp.mean(dxhat * xhat, axis=-1, keepdims=True))

    return pl.pallas_call(
        body, name=name, grid=(S // tm,), in_specs=[_rows(tm, D), _vec(D), _rows(tm, D)],
        out_specs=[_rows(tm, D), _vec(LANE), _vec(D)],
        out_shape=[jax.ShapeDtypeStruct((S, D), F32), jax.ShapeDtypeStruct((1, LANE), F32),
                   jax.ShapeDtypeStruct((1, D), F32)],
        compiler_params=_arb(1),
    )(x, g, target)


HEADS_PER_SLAB = LANE // A_DH
N_SLABS = A_HEADS // HEADS_PER_SLAB
SPAN_BLOCKS = KSPAN // QBLK


def _attn_specs(seg):
    q_spec = pl.BlockSpec((QBLK, LANE), lambda p, m: (m, seg[0] * N_SLABS + p))
    k_specs = [pl.BlockSpec((QBLK, LANE), functools.partial(
        lambda j, p, m: (jnp.maximum(m - (SPAN_BLOCKS - 1) + j, 0), seg[1] * N_SLABS + p), j)) for j in range(SPAN_BLOCKS)]
    v_specs = [pl.BlockSpec((QBLK, LANE), functools.partial(
        lambda j, p, m: (jnp.maximum(m - (SPAN_BLOCKS - 1) + j, 0), seg[2] * N_SLABS + p), j)) for j in range(SPAN_BLOCKS)]
    b_spec = pl.BlockSpec((HEADS_PER_SLAB, QBLK, KSPAN), lambda p, m: (p, 0, 0))
    return q_spec, k_specs, v_specs, b_spec


def _head_lanes(t, hh):
    lane = lax.broadcasted_iota(jnp.int32, t.shape, 1)
    return jnp.where((lane // A_DH) == hh, t, jnp.zeros_like(t))


def _attn_probs(qh, k, bias, m):
    s = _nt(qh, k) * (A_DH ** -0.5) + bias
    key_pos = lax.broadcasted_iota(jnp.int32, (QBLK, KSPAN), 1) + (m - (SPAN_BLOCKS - 1)) * QBLK
    s = jnp.where(key_pos >= 0, s, NEG)
    p = jnp.exp(s - jnp.max(s, axis=-1, keepdims=True))
    return p / jnp.sum(p, axis=-1, keepdims=True)


def _attn_fwd(proj, big, name):
    S = proj.shape[0]
    q_spec, k_specs, v_specs, b_spec = _attn_specs((0, 1, 2))

    def body(q_ref, k0, k1, k2, v0, v1, v2, b_ref, o_ref):
        m = pl.program_id(1)
        q = q_ref[...].astype(BF16)
        k = jnp.concatenate([k0[...], k1[...], k2[...]], axis=0).astype(BF16)
        v = jnp.concatenate([v0[...], v1[...], v2[...]], axis=0).astype(BF16)
        outs = [_nn(_attn_probs(_head_lanes(q, hh), k, b_ref[hh], m).astype(BF16), v) for hh in range(HEADS_PER_SLAB)]
        lane = lax.broadcasted_iota(jnp.int32, (QBLK, LANE), 1)
        o_ref[...] = jnp.where(lane < A_DH, outs[0], outs[1]).astype(BF16)

    return pl.pallas_call(
        body, name=name, grid=(N_SLABS, S // QBLK), in_specs=[q_spec] + k_specs + v_specs + [b_spec],
        out_specs=pl.BlockSpec((QBLK, LANE), lambda p, m: (m, p)),
        out_shape=jax.ShapeDtypeStruct((S, A_HEADS * A_DH), BF16), compiler_params=_par(2),
    )(proj, proj, proj, proj, proj, proj, proj, big)


def _attn_bwd(proj, big, dya, name):
    S = proj.shape[0]
    W = A_HEADS * A_DH
    q_spec, k_specs, v_specs, b_spec = _attn_specs((0, 1, 2))
    out_q = pl.BlockSpec((QBLK, LANE), lambda p, m: (m, p))
    out_kv = pl.BlockSpec((S, LANE), lambda p, m: (0, p))

    def body(q_ref, k0, k1, k2, v0, v1, v2, b_ref, do_ref, dq_ref, dk_ref, dv_ref, db_ref):
        m = pl.program_id(1)

        @pl.when(m == 0)
        def _():
            dk_ref[...] = jnp.zeros_like(dk_ref)
            dv_ref[...] = jnp.zeros_like(dv_ref)
            db_ref[...] = jnp.zeros_like(db_ref)

        q = q_ref[...].astype(BF16)
        k = jnp.concatenate([k0[...], k1[...], k2[...]], axis=0).astype(BF16)
        v = jnp.concatenate([v0[...], v1[...], v2[...]], axis=0).astype(BF16)
        do = do_ref[...]
        dqs = []
        dk = jnp.zeros((KSPAN, LANE), F32)
        dv = jnp.zeros((KSPAN, LANE), F32)
        for hh in range(HEADS_PER_SLAB):
            qh = _head_lanes(q, hh)
            doh = _head_lanes(do, hh)
            p = _attn_probs(qh, k, b_ref[hh], m)
            dp = _nt(doh, v)
            ds = p * (dp - jnp.sum(p * dp, axis=-1, keepdims=True))
            db_ref[hh] += ds
            dsb = (ds * (A_DH ** -0.5)).astype(BF16)
            dqs.append(_nn(dsb, k))
            dk = dk + _tn(dsb, qh)
            dv = dv + _tn(p.astype(BF16), doh)
        lane = lax.broadcasted_iota(jnp.int32, (QBLK, LANE), 1)
        dq_ref[...] = jnp.where(lane < A_DH, dqs[0], dqs[1])
        for j in range(SPAN_BLOCKS):
            blk = m - (SPAN_BLOCKS - 1) + j

            @pl.when(blk >= 0)
            def _():
                off = pl.multiple_of(blk * QBLK, QBLK)
                dk_ref[pl.ds(off, QBLK), :] += dk[j * QBLK:(j + 1) * QBLK]
                dv_ref[pl.ds(off, QBLK), :] += dv[j * QBLK:(j + 1) * QBLK]

    return pl.pallas_call(
        body, name=name, grid=(N_SLABS, S // QBLK),
        in_specs=[q_spec] + k_specs + v_specs + [b_spec, pl.BlockSpec((QBLK, LANE), lambda p, m: (m, p))],
        out_specs=[out_q, out_kv, out_kv, b_spec],
        out_shape=[jax.ShapeDtypeStruct((S, W), F32)] * 3 + [jax.ShapeDtypeStruct((A_HEADS, QBLK, KSPAN), F32)],
        compiler_params=_arb(2),
    )(proj, proj, proj, proj, proj, proj, proj, big, dya)


NREL_PAD = 3 * LANE
SKEW_W = 1024


def _rel_table_grad(dbig, name):
    H, R, C = dbig.shape

    def body(d_ref, o_ref):
        x = jnp.concatenate([d_ref[...], jnp.zeros((R, SKEW_W - C), F32)], axis=1)
        row = lax.broadcasted_iota(jnp.int32, (R, SKEW_W), 0)
        for b in range(R.bit_length() - 1):
            x = jnp.where(((row >> b) & 1) == 1, pltpu.roll(x, SKEW_W - (1 << b), 1), x)
        e = jnp.sum(x, axis=0, keepdims=True)
        xi = lax.broadcasted_iota(jnp.int32, (SKEW_W, NREL_PAD), 0)
        r = lax.broadcasted_iota(jnp.int32, (SKEW_W, NREL_PAD), 1)
        diag = jnp.where(xi < C, xi, xi - SKEW_W)
        rel = jnp.clip(A_PAST * CHUNK - diag, -A_MAX_REL, A_MAX_REL) + A_MAX_REL
        o_ref[...] = _nn(e, jnp.where(rel == r, 1.0, 0.0).astype(F32), HI)

    return pl.pallas_call(
        body, name=name, grid=(H,), in_specs=[pl.BlockSpec((None, R, C), lambda h: (h, 0, 0))],
        out_specs=pl.BlockSpec((None, 1, NREL_PAD), lambda h: (h, 0, 0)),
        out_shape=jax.ShapeDtypeStruct((H, 1, NREL_PAD), F32), compiler_params=_par(1),
    )(dbig)


def _chunk_cumsum_matrix(n, reverse):
    j = lax.broadcasted_iota(jnp.int32, (n, n), 0)
    i = lax.broadcasted_iota(jnp.int32, (n, n), 1)
    same = (j // CHUNK) == (i // CHUNK)
    return jnp.where(same & ((j >= i) if reverse else (j <= i)), 1.0, 0.0).astype(F32)


def _gdn_gates_fwd(b_t, a_t, alog, dtb, name):
    Hh, S = b_t.shape
    tl = _pick(S, 512)
    row = pl.BlockSpec((Hh, tl), lambda i: (0, i))
    col = pl.BlockSpec((Hh, 1), lambda i: (0, 0))

    def body(b_ref, a_ref, al_ref, dt_ref, beta_ref, gam_ref):
        z = a_ref[...] + dt_ref[...]
        sp = jnp.maximum(z, 0.0) + jnp.log(1.0 + jnp.exp(-jnp.abs(z)))
        g = -jnp.exp(al_ref[...]) * sp
        beta_ref[...] = _sigmoid(b_ref[...])
        gam_ref[...] = _nn(g, _chunk_cumsum_matrix(tl, False), HI)

    return pl.pallas_call(
        body, name=name, grid=(S // tl,), in_specs=[row, row, col, col], out_specs=[row, row],
        out_shape=[jax.ShapeDtypeStruct((Hh, S), F32)] * 2, compiler_params=_par(1),
    )(b_t, a_t, alog, dtb)


def _gdn_gates_bwd(dbeta, dgam_a, dgam_b, b_t, a_t, alog, dtb, name):
    Hh, S = b_t.shape
    tl = _pick(S, 512)
    row = pl.BlockSpec((Hh, tl), lambda i: (0, i))
    col = pl.BlockSpec((Hh, 1), lambda i: (0, 0))
    accs = pl.BlockSpec((Hh, LANE), lambda i: (0, 0))

    def body(dbeta_ref, dga_ref, dgb_ref, b_ref, a_ref, al_ref, dt_ref, db_ref, da_ref, dal_ref, ddt_ref):
        i = pl.program_id(0)
        z = a_ref[...] + dt_ref[...]
        sp = jnp.maximum(z, 0.0) + jnp.log(1.0 + jnp.exp(-jnp.abs(z)))
        ea = jnp.exp(al_ref[...])
        dg = _nn(dga_ref[...] + dgb_ref[...], _chunk_cumsum_matrix(tl, True), HI)
        da = dg * (-ea) * _sigmoid(z)
        beta = _sigmoid(b_ref[...])
        db_ref[...] = dbeta_ref[...] * beta * (1.0 - beta)
        da_ref[...] = da
        _acc(dal_ref, jnp.broadcast_to(jnp.sum(dg * (-ea * sp), axis=1, keepdims=True), (Hh, LANE)), i)
        _acc(ddt_ref, jnp.broadcast_to(jnp.sum(da, axis=1, keepdims=True), (Hh, LANE)), i)

    return pl.pallas_call(
        body, name=name, grid=(S // tl,), in_specs=[row] * 5 + [col, col], out_specs=[row, row, accs, accs],
        out_shape=[jax.ShapeDtypeStruct((Hh, S), F32)] * 2 + [jax.ShapeDtypeStruct((Hh, LANE), F32)] * 2,
        compiler_params=_arb(1),
    )(dbeta, dgam_a, dgam_b, b_t, a_t, alog, dtb)


HALO = 8


def _conv_silu(xx_ref, w_ref, tm):
    y = w_ref[0:1, :] * xx_ref[pl.ds(HALO - CONV_K + 1, tm), :]
    for j in range(1, CONV_K):
        y = y + w_ref[j:j + 1, :] * xx_ref[pl.ds(HALO - CONV_K + 1 + j, tm), :]
    return y, y * _sigmoid(y)


def _fill_prev_halo(xx_ref, x_ref, prev_ref, i, tm):
    xx_ref[pl.ds(HALO, tm), :] = x_ref[...]

    @pl.when(i == 0)
    def _():
        xx_ref[pl.ds(0, HALO), :] = jnp.zeros((HALO, xx_ref.shape[1]), F32)

    @pl.when(i != 0)
    def _():
        xx_ref[pl.ds(0, HALO), :] = prev_ref[...]


def _gdn_pre_specs(tm, C, colblk):
    cur = pl.BlockSpec((tm, C), lambda i: (i, colblk))
    prev = pl.BlockSpec((HALO, C), lambda i: (jnp.maximum(i * (tm // HALO) - 1, 0), colblk))
    return cur, prev


def _gdn_pre_fwd(proj, wconv, name):
    S = proj.shape[0]
    C = 3 * B_HEADS * B_DH
    W = B_HEADS * B_DH
    tm = _pick(S, 256, 8)
    cur, prev = _gdn_pre_specs(tm, C, P_QKVB // C)

    def body(x_ref, prev_ref, w_ref, q_ref, k_ref, v_ref, xx_ref):
        i = pl.program_id(0)
        _fill_prev_halo(xx_ref, x_ref, prev_ref, i, tm)
        _, sl = _conv_silu(xx_ref, w_ref, tm)
        for h in range(B_HEADS):
            hs = slice(h * B_DH, (h + 1) * B_DH)
            q = sl[:, h * B_DH:(h + 1) * B_DH]
            k = sl[:, W + h * B_DH:W + (h + 1) * B_DH]
            q_ref[:, hs] = q * (lax.rsqrt(jnp.sum(q * q, axis=-1, keepdims=True) + EPS) * (B_DH ** -0.5))
            k_ref[:, hs] = k * lax.rsqrt(jnp.sum(k * k, axis=-1, keepdims=True) + EPS)
        v_ref[...] = sl[:, 2 * W:]

    return pl.pallas_call(
        body, name=name, grid=(S // tm,), in_specs=[cur, prev, pl.BlockSpec((CONV_K, C), lambda i: (0, 0))],
        out_specs=[_rows(tm, W)] * 3, out_shape=[jax.ShapeDtypeStruct((S, W), F32)] * 3,
        scratch_shapes=[pltpu.VMEM((HALO + tm, C), F32)], compiler_params=_par(1),
    )(proj, proj, wconv)


def _gdn_pre_bwd_a(proj, wconv, dqn, dkn, dv, name):
    S = proj.shape[0]
    C = 3 * B_HEADS * B_DH
    W = B_HEADS * B_DH
    tm = _pick(S, 256, 8)
    cur, prev = _gdn_pre_specs(tm, C, P_QKVB // C)

    def body(x_ref, prev_ref, w_ref, dq_ref, dk_ref, dv_ref, dy_ref, xx_ref):
        i = pl.program_id(0)
        _fill_prev_halo(xx_ref, x_ref, prev_ref, i, tm)
        y, sl = _conv_silu(xx_ref, w_ref, tm)
        sg = _sigmoid(y)
        dsilu = sg * (1.0 + y * (1.0 - sg))
        for h in range(B_HEADS):
            for base, d_ref, c in ((0, dq_ref, B_DH ** -0.5), (W, dk_ref, 1.0)):
                lo = base + h * B_DH
                t = sl[:, lo:lo + B_DH]
                d = d_ref[:, h * B_DH:(h + 1) * B_DH]
                r = lax.rsqrt(jnp.sum(t * t, axis=-1, keepdims=True) + EPS)
                dt = (c * r) * (d - t * (r * r) * jnp.sum(d * t, axis=-1, keepdims=True))
                dy_ref[:, lo:lo + B_DH] = dt * dsilu[:, lo:lo + B_DH]
        dy_ref[:, 2 * W:] = dv_ref[...] * dsilu[:, 2 * W:]

    return pl.pallas_call(
        body, name=name, grid=(S // tm,),
        in_specs=[cur, prev, pl.BlockSpec((CONV_K, C), lambda i: (0, 0))] + [_rows(tm, W)] * 3,
        out_specs=_rows(tm, C), out_shape=jax.ShapeDtypeStruct((S, C), F32),
        scratch_shapes=[pltpu.VMEM((HALO + tm, C), F32)], compiler_params=_par(1),
    )(proj, proj, wconv, dqn, dkn, dv)


def _gdn_pre_bwd_b(proj, wconv, dy, name):
    S = proj.shape[0]
    C = 3 * B_HEADS * B_DH
    tm = _pick(S, 256, 8)
    nt_ = S // tm
    cur, prev = _gdn_pre_specs(tm, C, P_QKVB // C)
    nxt = pl.BlockSpec((HALO, C), lambda i: (jnp.minimum((i + 1) * (tm // HALO), S // HALO - 1), 0))

    def body(x_ref, prev_ref, w_ref, dy_ref, next_ref, dx_ref, dw_ref, xx_ref, dd_ref):
        i = pl.program_id(0)
        _fill_prev_halo(xx_ref, x_ref, prev_ref, i, tm)
        dyv = dy_ref[...]
        dd_ref[pl.ds(0, tm), :] = dyv

        @pl.when(i == nt_ - 1)
        def _():
            dd_ref[pl.ds(tm, HALO), :] = jnp.zeros((HALO, C), F32)

        @pl.when(i != nt_ - 1)
        def _():
            dd_ref[pl.ds(tm, HALO), :] = next_ref[...]

        dx = w_ref[0:1, :] * dd_ref[pl.ds(CONV_K - 1, tm), :]
        for j in range(1, CONV_K):
            dx = dx + w_ref[j:j + 1, :] * dd_ref[pl.ds(CONV_K - 1 - j, tm), :]
        dx_ref[...] = dx.astype(BF16)
        dw = jnp.concatenate(
            [jnp.sum(dyv * xx_ref[pl.ds(HALO - CONV_K + 1 + j, tm), :], axis=0, keepdims=True) for j in range(CONV_K)],
            axis=0)
        _acc(dw_ref, dw, i)

    return pl.pallas_call(
        body, name=name, grid=(nt_,),
        in_specs=[cur, prev, pl.BlockSpec((CONV_K, C), lambda i: (0, 0)), _rows(tm, C), nxt],
        out_specs=[_rows(tm, C), pl.BlockSpec((CONV_K, C), lambda i: (0, 0))],
        out_shape=[jax.ShapeDtypeStruct((S, C), BF16), jax.ShapeDtypeStruct((CONV_K, C), F32)],
        scratch_shapes=[pltpu.VMEM((HALO + tm, C), F32), pltpu.VMEM((tm + HALO, C), F32)],
        compiler_params=_arb(1),
    )(proj, proj, wconv, dy, dy)


def _chunk_masks():
    row = lax.broadcasted_iota(jnp.int32, (CHUNK, CHUNK), 0)
    col = lax.broadcasted_iota(jnp.int32, (CHUNK, CHUNK), 1)
    return row >= col, row > col


def _chunk_local(q, k, vv, bc, gc, gr, tri):
    dm = jnp.where(tri, jnp.exp(jnp.where(tri, gc - gr, 0.0)), 0.0)
    kk = _bnt(k, k)
    glast = gr[:, CHUNK - 1:CHUNK]
    ep = jnp.exp(gc)
    em = jnp.exp(glast - gc)
    el = jnp.exp(glast)
    return dm, kk, ep, em, el, vv * bc, k * (bc * ep)


def _unit_lower_inverse(low):
    row = lax.broadcasted_iota(jnp.int32, (CHUNK, CHUNK), 0)
    col = lax.broadcasted_iota(jnp.int32, (CHUNK, CHUNK), 1)
    p = -low
    t = jnp.where(row == col, 1.0, 0.0).astype(F32) + p
    steps = CHUNK.bit_length() - 2
    for _ in range(steps):
        p = _nn(p, p, SOLVE_PREC)
        t = t + _nn(t, p, SOLVE_PREC)
    return t


def _gdn_chunk_specs(nc, rev):
    idx = (lambda i: nc - 1 - i) if rev else (lambda i: i)
    W = B_HEADS * B_DH
    tok = pl.BlockSpec((CHUNK, W), lambda i: (idx(i), 0))
    colv = pl.BlockSpec((None, CHUNK, B_HEADS), lambda i: (idx(i), 0, 0))
    rowv = pl.BlockSpec((None, B_HEADS, CHUNK), lambda i: (idx(i), 0, 0))
    tmat = pl.BlockSpec((None, B_HEADS, CHUNK, CHUNK), lambda i: (idx(i), 0, 0, 0))
    smat = pl.BlockSpec((None, B_HEADS, B_DH, B_DH), lambda i: (idx(i), 0, 0, 0))
    return tok, colv, rowv, tmat, smat


def _gdn_chunk_fwd(qn, kn, v, bcol, gcol, grow, name):
    S, W = qn.shape
    nc = S // CHUNK
    tok, colv, rowv, tmat, smat = _gdn_chunk_specs(nc, False)

    def body(q_ref, k_ref, v_ref, bc_ref, gc_ref, gr_ref, o_ref, t_ref, sh_ref, st_ref):
        i = pl.program_id(0)

        @pl.when(i == 0)
        def _():
            st_ref[...] = jnp.zeros_like(st_ref)

        tri, strict = _chunk_masks()
        bc_all = bc_ref[...]
        gc_all = gc_ref[...]
        for h in range(B_HEADS):
            hs = slice(h * B_DH, (h + 1) * B_DH)
            q, k, vv = q_ref[:, hs], k_ref[:, hs], v_ref[:, hs]
            bc, gc, gr = bc_all[:, h:h + 1], gc_all[:, h:h + 1], gr_ref[h:h + 1, :]
            dm, kk, ep, em, el, vb, kb = _chunk_local(q, k, vv, bc, gc, gr, tri)
            t = _unit_lower_inverse(jnp.where(strict, bc * kk * dm, 0.0))
            u = _nn(t, vb, SOLVE_PREC)
            w = _nn(t, kb, SOLVE_PREC)
            a = _bnt(q, k) * dm
            s0 = st_ref[h]
            ut = u - _bnn(w, s0)
            o_ref[:, hs] = _bnn(q * ep, s0) + _bnn(a, ut)
            st_ref[h] = el * s0 + _btn(k * em, ut)
            t_ref[h] = t
            sh_ref[h] = s0

    return pl.pallas_call(
        body, name=name, grid=(nc,), in_specs=[tok, tok, tok, colv, colv, rowv], out_specs=[tok, tmat, smat],
        out_shape=[jax.ShapeDtypeStruct((S, W), F32), jax.ShapeDtypeStruct((nc, B_HEADS, CHUNK, CHUNK), F32),
                   jax.ShapeDtypeStruct((nc, B_HEADS, B_DH, B_DH), F32)],
        scratch_shapes=[pltpu.VMEM((B_HEADS, B_DH, B_DH), F32)], compiler_params=_arb(1),
    )(qn, kn, v, bcol, gcol, grow)


def _gdn_chunk_bwd(qn, kn, v, bcol, gcol, grow, tsave, ssave, do, name):
    S, W = qn.shape
    nc = S // CHUNK
    tok, colv, rowv, tmat, smat = _gdn_chunk_specs(nc, True)

    def body(q_ref, k_ref, v_ref, bc_ref, gc_ref, gr_ref, t_ref, sh_ref, do_ref,
             dq_ref, dk_ref, dv_ref, dbc_ref, dgc_ref, dgr_ref, ds_ref):
        i = pl.program_id(0)

        @pl.when(i == 0)
        def _():
            ds_ref[...] = jnp.zeros_like(ds_ref)

        tri, strict = _chunk_masks()
        bc_all = bc_ref[...]
        gc_all = gc_ref[...]
        lane4 = lax.broadcasted_iota(jnp.int32, (CHUNK, B_HEADS), 1)
        sub4 = lax.broadcasted_iota(jnp.int32, (B_HEADS, CHUNK), 0)
        lane_last = lax.broadcasted_iota(jnp.int32, (1, CHUNK), 1) == CHUNK - 1
        dbc_acc = jnp.zeros((CHUNK, B_HEADS), F32)
        dgc_acc = jnp.zeros((CHUNK, B_HEADS), F32)
        dgr_acc = jnp.zeros((B_HEADS, CHUNK), F32)
        for h in range(B_HEADS):
            hs = slice(h * B_DH, (h + 1) * B_DH)
            q, k, vv = q_ref[:, hs], k_ref[:, hs], v_ref[:, hs]
            bc, gc, gr = bc_all[:, h:h + 1], gc_all[:, h:h + 1], gr_ref[h:h + 1, :]
            dm, kk, ep, em, el, vb, kb = _chunk_local(q, k, vv, bc, gc, gr, tri)
            t = t_ref[h]
            s0 = sh_ref[h]
            w = _nn(t, kb, SOLVE_PREC)
            qk = _bnt(q, k)
            a = qk * dm
            ut = _nn(t, vb, SOLVE_PREC) - _bnn(w, s0)
            qd = q * ep
            kd = k * em
            ds = ds_ref[h]
            dout = do_ref[:, hs]
            dut = _btn(a, dout) + _bnn(kd, ds)
            da = jnp.where(tri, _bnt(dout, ut), 0.0)
            dqd = _bnt(dout, s0)
            dkd = _bnt(ut, ds)
            d_el = jnp.sum(jnp.sum(s0 * ds, axis=1, keepdims=True), axis=0, keepdims=True)
            dw = -_bnt(dut, s0)
            ds_ref[h] = el * ds + _btn(qd, dout) - _btn(w, dut)
            dt = _nt(dut, vb, SOLVE_PREC) + _nt(dw, kb, SOLVE_PREC)
            dvb = _tn(t, dut, SOLVE_PREC)
            dkb = _tn(t, dw, SOLVE_PREC)
            dl = jnp.where(strict, -_tn(t, _nt(dt, t, SOLVE_PREC), SOLVE_PREC), 0.0)
            g1 = dl * dm
            dkb_k = jnp.sum(dkb * k, axis=1, keepdims=True)
            dbeta = jnp.sum(g1 * kk, axis=1, keepdims=True) + jnp.sum(dvb * vv, axis=1, keepdims=True) + dkb_k * ep
            dkk = g1 * bc
            ddm = dl * (bc * kk) + da * qk
            dqk = da * dm
            dq_ref[:, hs] = _bnn(dqk, k) + dqd * ep
            dk_ref[:, hs] = (_btn(dqk, q) + _bnn(dkk, k) + _btn(dkk, k) + dkb * (bc * ep) + dkd * em)
            dv_ref[:, hs] = dvb * bc
            dep = dkb_k * bc + jnp.sum(dqd * q, axis=1, keepdims=True)
            dem = jnp.sum(dkd * k, axis=1, keepdims=True)
            mm = ddm * dm
            dgam_c = jnp.sum(mm, axis=1, keepdims=True) + dep * ep - dem * em
            dglast = jnp.sum(dem * em, axis=0, keepdims=True) + d_el * el
            dgam_r = -jnp.sum(mm, axis=0, keepdims=True) + jnp.where(lane_last, dglast, 0.0)
            dbc_acc = jnp.where(lane4 == h, dbeta, dbc_acc)
            dgc_acc = jnp.where(lane4 == h, dgam_c, dgc_acc)
            dgr_acc = jnp.where(sub4 == h, dgam_r, dgr_acc)
        dbc_ref[...] = dbc_acc
        dgc_ref[...] = dgc_acc
        dgr_ref[...] = dgr_acc

    return pl.pallas_call(
        body, name=name, grid=(nc,), in_specs=[tok, tok, tok, colv, colv, rowv, tmat, smat, tok],
        out_specs=[tok, tok, tok, colv, colv, rowv],
        out_shape=[jax.ShapeDtypeStruct((S, W), F32)] * 3
        + [jax.ShapeDtypeStruct((nc, CHUNK, B_HEADS), F32)] * 2 + [jax.ShapeDtypeStruct((nc, B_HEADS, CHUNK), F32)],
        scratch_shapes=[pltpu.VMEM((B_HEADS, B_DH, B_DH), F32)], compiler_params=_arb(1),
    )(qn, kn, v, bcol, gcol, grow, tsave, ssave, do)


def _gdn_post_fwd(o, proj, ng, name):
    S, W = o.shape
    tm = _pick(S, 512, 8)

    def body(o_ref, z_ref, g_ref, y_ref):
        gv = g_ref[...]
        for h in range(B_HEADS):
            hs = slice(h * B_DH, (h + 1) * B_DH)
            oh = o_ref[:, hs]
            z = z_ref[:, hs]
            r = lax.rsqrt(jnp.mean(oh * oh, axis=-1, keepdims=True) + EPS)
            y_ref[:, hs] = (oh * r * gv * (z * _sigmoid(z))).astype(BF16)

    return pl.pallas_call(
        body, name=name, grid=(S // tm,), in_specs=[_rows(tm, W), _rows(tm, W, P_Z // W), _vec(B_DH)],
        out_specs=_rows(tm, W), out_shape=jax.ShapeDtypeStruct((S, W), BF16), compiler_params=_par(1),
    )(o, proj, ng)


def _gdn_post_bwd(dy, o, proj, ng, name):
    S, W = o.shape
    tm = _pick(S, 512, 8)

    def body(dy_ref, o_ref, z_ref, g_ref, do_ref, dz_ref, dg_ref):
        i = pl.program_id(0)
        gv = g_ref[...]
        dg = jnp.zeros((1, B_DH), F32)
        for h in range(B_HEADS):
            hs = slice(h * B_DH, (h + 1) * B_DH)
            oh = o_ref[:, hs]
            z = z_ref[:, hs]
            d = dy_ref[:, hs]
            r = lax.rsqrt(jnp.mean(oh * oh, axis=-1, keepdims=True) + EPS)
            n = oh * r
            sg = _sigmoid(z)
            sz = z * sg
            dn = d * gv * sz
            dg = dg + jnp.sum(d * n * sz, axis=0, keepdims=True)
            dz_ref[:, hs] = (d * n * gv * (sg * (1.0 + z * (1.0 - sg)))).astype(BF16)
            do_ref[:, hs] = r * (dn - n * jnp.mean(dn * n, axis=-1, keepdims=True))
        _acc(dg_ref, dg, i)

    return pl.pallas_call(
        body, name=name, grid=(S // tm,), in_specs=[_rows(tm, W), _rows(tm, W), _rows(tm, W, P_Z // W), _vec(B_DH)],
        out_specs=[_rows(tm, W), _rows(tm, W), _vec(B_DH)],
        out_shape=[jax.ShapeDtypeStruct((S, W), F32), jax.ShapeDtypeStruct((S, W), BF16),
                   jax.ShapeDtypeStruct((1, B_DH), F32)],
        compiler_params=_arb(1),
    )(dy, o, proj, ng)


def _ada_mod(c_all, w_ada, b_shard, name):
    L, D, Ns = w_ada.shape
    B = c_all.shape[0]

    def body(c_ref, w_ref, b_ref, o_ref):
        cv = c_ref[...]
        cond = (cv * _sigmoid(cv)).astype(BF16)
        o_ref[...] = _nn(cond, w_ref[...].astype(BF16)) + b_ref[...]

    return pl.pallas_call(
        body, name=name, grid=(L,),
        in_specs=[pl.BlockSpec((B, D), lambda l: (0, 0)), pl.BlockSpec((None, D, Ns), lambda l: (l, 0, 0)),
                  pl.BlockSpec((None, 1, Ns), lambda l: (l, 0, 0))],
        out_specs=pl.BlockSpec((None, B, Ns), lambda l: (l, 0, 0)),
        out_shape=jax.ShapeDtypeStruct((L, B, Ns), F32), compiler_params=_par(1),
    )(c_all, w_ada, b_shard)


def _ada_wgrad(c_all, dmod, name):
    L, B, Ns = dmod.shape
    D = c_all.shape[1]

    def body(c_ref, d_ref, o_ref):
        cv = c_ref[...]
        cond = (cv * _sigmoid(cv)).astype(BF16)
        o_ref[...] = _tn(cond, d_ref[...].astype(BF16))

    return pl.pallas_call(
        body, name=name, grid=(L,),
        in_specs=[pl.BlockSpec((B, D), lambda l: (0, 0)), pl.BlockSpec((None, B, Ns), lambda l: (l, 0, 0))],
        out_specs=pl.BlockSpec((None, D, Ns), lambda l: (l, 0, 0)),
        out_shape=jax.ShapeDtypeStruct((L, D, Ns), F32), compiler_params=_par(1),
    )(c_all, dmod)


W_IN_PIECES = ((0, 0, 1410), (1, 0, 1410), (2, 0, 252), (2, 772, 638), (3, 0, 1410), (2, 252, 512), (2, 764, 8))


def _reorder_w_in(w4, name):
    L, _, D, Cs = w4.shape
    tm = _pick(D, 256, 16)
    used = sum(p[2] for p in W_IN_PIECES)

    def body(w_ref, o_ref):
        shard = [w_ref[s] for s in range(4)]
        parts = [shard[s][:, lo:lo + n] for s, lo, n in W_IN_PIECES]
        o_ref[...] = jnp.concatenate(parts + [jnp.zeros((tm, P_END - used), w4.dtype)], axis=1)

    return pl.pallas_call(
        body, name=name, grid=(L, D // tm), in_specs=[pl.BlockSpec((None, 4, tm, Cs), lambda l, i: (l, 0, i, 0))],
        out_specs=pl.BlockSpec((None, tm, P_END), lambda l, i: (l, i, 0)),
        out_shape=jax.ShapeDtypeStruct((L, D, P_END), w4.dtype), compiler_params=_par(2),
    )(w4)


def _restore_w_in(g, name):
    D = g.shape[0]
    tm = _pick(D, 256, 8)

    def body(g_ref, o_ref):
        gv = g_ref[...]
        off = 0
        pieces = {}
        for s, lo, n in W_IN_PIECES:
            pieces.setdefault(s, []).append((lo, gv[:, off:off + n]))
            off += n
        for s, lst in pieces.items():
            lst.sort(key=lambda t: t[0])
            o_ref[s] = lst[0][1] if len(lst) == 1 else jnp.concatenate([t[1] for t in lst], axis=1)

    return pl.pallas_call(
        body, name=name, grid=(D // tm,), in_specs=[pl.BlockSpec((tm, P_END), lambda i: (i, 0))],
        out_specs=pl.BlockSpec((4, tm, W_IN_SHARD), lambda i: (0, i, 0)),
        out_shape=jax.ShapeDtypeStruct((4, D, W_IN_SHARD), g.dtype), compiler_params=_par(1),
    )(g)


def _adam_update(w, g, m, v):
    mn = ADAM_B1 * m + (1.0 - ADAM_B1) * g
    vn = ADAM_B2 * v + (1.0 - ADAM_B2) * (g * g)
    m_hat = mn / (1.0 - ADAM_B1 ** ADAM_STEP)
    v_hat = vn / (1.0 - ADAM_B2 ** ADAM_STEP)
    return -ADAM_LR * (m_hat / (jnp.sqrt(v_hat) + ADAM_EPS) + ADAM_WD * w), mn, vn


def _adamw(w, g, m, v, name):
    shape = w.shape
    C = shape[-1]
    R = w.size // C
    tm = _pick(R, 512, 8)
    spec = pl.BlockSpec((tm, C), lambda i: (i, 0))

    def body(w_ref, g_ref, m_ref, v_ref, d_ref, mo_ref, vo_ref):
        d_ref[...], mo_ref[...], vo_ref[...] = _adam_update(w_ref[...], g_ref[...], m_ref[...], v_ref[...])

    outs = pl.pallas_call(
        body, name=name, grid=(R // tm,), in_specs=[spec] * 4, out_specs=[spec] * 3,
        out_shape=[jax.ShapeDtypeStruct((R, C), F32)] * 3, compiler_params=_par(1),
    )(*(t.reshape(R, C) for t in (w, g, m, v)))
    return tuple(o.reshape(shape) for o in outs)


def _adamw_layers(w, gs, m, v, name):
    L, R, C = w.shape
    tm = _pick(R, 128, 8)
    spec = pl.BlockSpec((None, tm, C), lambda l, i: (l, i, 0))
    g_specs = [pl.BlockSpec((tm, C), functools.partial(lambda ll, l, i: (jnp.where(l == ll, i, 0), 0), ll))
               for ll in range(L)]

    def body(w_ref, m_ref, v_ref, *rest):
        g_refs, (go_ref, d_ref, mo_ref, vo_ref) = rest[:L], rest[L:]
        l = pl.program_id(0)
        for ll in range(L):
            @pl.when(l == ll)
            def _():
                g = g_refs[ll][...]
                go_ref[...] = g
                d_ref[...], mo_ref[...], vo_ref[...] = _adam_update(w_ref[...], g, m_ref[...], v_ref[...])

    return pl.pallas_call(
        body, name=name, grid=(L, R // tm), in_specs=[spec] * 3 + g_specs, out_specs=[spec] * 4,
        out_shape=[jax.ShapeDtypeStruct((L, R, C), F32)] * 4, compiler_params=_arb(2),
    )(w, m, v, *gs)


def _pair_sums(a, where, b, name):
    NB, _, R, C = a.shape

    def body(where_ref, a_ref, b_ref, p_ref, own_ref):
        s = a_ref[...] + b_ref[...].astype(F32)
        p_ref[...] = s.astype(BF16)

        @pl.when(pl.program_id(0) == where_ref[1])
        def _():
            own_ref[...] = s

    return pl.pallas_call(
        body, name=name,
        grid_spec=pltpu.PrefetchScalarGridSpec(
            num_scalar_prefetch=1, grid=(NB,),
            in_specs=[pl.BlockSpec((None, None, R, C), lambda k, w: (k, w[0], 0, 0)),
                      pl.BlockSpec((None, R, C), lambda k, w: (k, 0, 0))],
            out_specs=[pl.BlockSpec((None, R, C), lambda k, w: (k, 0, 0)), pl.BlockSpec((R, C), lambda k, w: (0, 0))]),
        out_shape=[jax.ShapeDtypeStruct((NB, R, C), BF16), jax.ShapeDtypeStruct((R, C), F32)],
        compiler_params=_arb(1),
    )(where, a, b)


def _sum_own_and_received(own, recv, name):
    R, C = own.shape
    tm = _pick(R, 256, 16)

    def body(p_ref, r_ref, o_ref):
        o_ref[...] = ((p_ref[...] + r_ref[0].astype(F32)) + r_ref[1].astype(F32)) + r_ref[2].astype(F32)

    return pl.pallas_call(
        body, name=name, grid=(R // tm,),
        in_specs=[pl.BlockSpec((tm, C), lambda i: (i, 0)), pl.BlockSpec((3, tm, C), lambda i: (0, i, 0))],
        out_specs=pl.BlockSpec((tm, C), lambda i: (i, 0)),
        out_shape=jax.ShapeDtypeStruct((R, C), F32), compiler_params=_par(1),
    )(own, recv)


def _position():
    return lax.axis_index("x"), lax.axis_index("y"), lax.axis_index("c")


def _other_chips(x, y):
    return [(1 - x, y), (x, 1 - y), (1 - x, 1 - y)]


HBM = pl.BlockSpec(memory_space=pl.ANY)


def _allgather8(blk, name, reduce_rows=None):
    M, N = blk.shape

    def body(x_ref, out_ref, *rest):
        if reduce_rows is None:
            send_sems, recv_sems, local_sem = rest
        else:
            sum_ref, send_sems, recv_sems, local_sem = rest
        x, y, c = _position()
        me, sibling = (x, y, c), (x, y, 1 - c)
        chips = _other_chips(x, y)

        def rows(px, py, pc):
            return out_ref.at[pl.ds((4 * px + 2 * py + pc) * M, M), :]

        def copy(k, block, to, src=None):
            return pltpu.make_async_remote_copy(
                src_ref=rows(*block) if src is None else src, dst_ref=rows(*block),
                send_sem=send_sems.at[k], recv_sem=recv_sems.at[k], device_id=to, device_id_type=MESH)

        mine = pltpu.make_async_copy(x_ref, rows(*me), local_sem)
        mine.start()
        first = [copy(0, me, sibling, src=x_ref)]
        first += [copy(1 + j, me, (*chip, c), src=x_ref) for j, chip in enumerate(chips)]
        for cp in first:
            cp.start()
        passed = [copy(4 + j, (*chip, c), sibling) for j, chip in enumerate(chips)]
        for j, chip in enumerate(chips):
            copy(1 + j, (*chip, c), me).wait_recv()
            passed[j].start()
        copy(0, sibling, me).wait_recv()
        for j, chip in enumerate(chips):
            copy(4 + j, (*chip, 1 - c), me).wait_recv()
        for cp in first + passed:
            cp.wait_send()
        mine.wait()
        if reduce_rows is not None:
            tot = out_ref[pl.ds(0, reduce_rows), :]
            for d in range(1, 8):
                tot = tot + out_ref[pl.ds(d * M, reduce_rows), :]
            sum_ref[...] = tot

    vmem = pl.BlockSpec(memory_space=pltpu.VMEM)
    out_shape = [jax.ShapeDtypeStruct((8 * M, N), blk.dtype)]
    if reduce_rows is not None:
        out_shape.append(jax.ShapeDtypeStruct((reduce_rows, N), blk.dtype))
    res = pl.pallas_call(
        body, name=name, out_shape=out_shape, in_specs=[vmem], out_specs=[vmem] * len(out_shape),
        scratch_shapes=[pltpu.SemaphoreType.DMA((7,)), pltpu.SemaphoreType.DMA((7,)), pltpu.SemaphoreType.DMA],
    )(blk)
    return res[0] if reduce_rows is None else (res[0], res[1])


def _gather_weights(shards, name):
    n = len(shards)

    def body(*refs):
        srcs, outs = refs[:n], refs[n:2 * n]
        send_sems, recv_sems, local_sems = refs[2 * n:]
        x, y, c = _position()
        me, sibling = (x, y, c), (x, y, 1 - c)
        chips = _other_chips(x, y)

        def region(t, px, py, pc):
            return outs[t].at[:, 2 * px + py, pc]

        def copy(t, k, block, to, src=None):
            return pltpu.make_async_remote_copy(
                src_ref=region(t, *block) if src is None else src, dst_ref=region(t, *block),
                send_sem=send_sems.at[7 * t + k], recv_sem=recv_sems.at[7 * t + k], device_id=to, device_id_type=MESH)

        mine = [pltpu.make_async_copy(srcs[t].at[:, c], region(t, *me), local_sems.at[t]) for t in range(n)]
        for cp in mine:
            cp.start()
        first = []
        for t in range(n):
            first.append(copy(t, 0, me, sibling, src=srcs[t].at[:, c]))
            first += [copy(t, 1 + j, me, (*chip, c), src=srcs[t].at[:, c]) for j, chip in enumerate(chips)]
        for cp in first:
            cp.start()
        passed = []
        for j, chip in enumerate(chips):
            for t in range(n):
                copy(t, 1 + j, (*chip, c), me).wait_recv()
                cp = copy(t, 4 + j, (*chip, c), sibling)
                cp.start()
                passed.append(cp)
        for t in range(n):
            copy(t, 0, sibling, me).wait_recv()
        for j, chip in enumerate(chips):
            for t in range(n):
                copy(t, 4 + j, (*chip, 1 - c), me).wait_recv()
        for cp in first + passed:
            cp.wait_send()
        for cp in mine:
            cp.wait()

    out_shape = [jax.ShapeDtypeStruct((s.shape[0], 4) + s.shape[1:], s.dtype) for s in shards]
    return pl.pallas_call(
        body, name=name, out_shape=out_shape, in_specs=[HBM] * n, out_specs=[HBM] * n,
        scratch_shapes=[pltpu.SemaphoreType.DMA((7 * n,)), pltpu.SemaphoreType.DMA((7 * n,)),
                        pltpu.SemaphoreType.DMA((n,))],
    )(*shards)


def _rs_sibling(gs, name):
    n = len(gs)

    def body(*refs):
        srcs, outs = refs[:n], refs[n:2 * n]
        send_sems, recv_sems = refs[2 * n:]
        x, y, c = _position()
        copies = [pltpu.make_async_remote_copy(
            src_ref=srcs[t].at[k, 1 - c], dst_ref=outs[t].at[k], send_sem=send_sems.at[4 * t + k],
            recv_sem=recv_sems.at[4 * t + k], device_id=(x, y, 1 - c), device_id_type=MESH)
            for t in range(n) for k in range(4)]
        for cp in copies:
            cp.start()
        for cp in copies:
            cp.wait()

    out_shape = [jax.ShapeDtypeStruct((4,) + g.shape[2:], g.dtype) for g in gs]
    return pl.pallas_call(
        body, name=name, out_shape=out_shape, in_specs=[HBM] * n, out_specs=[HBM] * n,
        scratch_shapes=[pltpu.SemaphoreType.DMA((4 * n,)), pltpu.SemaphoreType.DMA((4 * n,))],
    )(*gs)


def _rs_chips(ps, name):
    n = len(ps)

    def body(*refs):
        srcs, outs = refs[:n], refs[n:2 * n]
        send_sems, recv_sems = refs[2 * n:]
        x, y, c = _position()
        copies = [pltpu.make_async_remote_copy(
            src_ref=srcs[t].at[2 * px + py], dst_ref=outs[t].at[j], send_sem=send_sems.at[3 * t + j],
            recv_sem=recv_sems.at[3 * t + j], device_id=(px, py, c), device_id_type=MESH)
            for t in range(n) for j, (px, py) in enumerate(_other_chips(x, y))]
        for cp in copies:
            cp.start()
        for cp in copies:
            cp.wait()

    out_shape = [jax.ShapeDtypeStruct((3,) + p.shape[1:], p.dtype) for p in ps]
    return pl.pallas_call(
        body, name=name, out_shape=out_shape, in_specs=[HBM] * n, out_specs=[HBM] * n,
        scratch_shapes=[pltpu.SemaphoreType.DMA((3 * n,)), pltpu.SemaphoreType.DMA((3 * n,))],
    )(*ps)


def _rs_pair(hs, name):
    n = len(hs)

    def body(*refs):
        srcs, outs = refs[:n], refs[n:2 * n]
        send_sems, recv_sems, local_sems = refs[2 * n:]
        x, y, c = _position()
        mine = [pltpu.make_async_copy(srcs[t], outs[t].at[c], local_sems.at[t]) for t in range(n)]
        for cp in mine:
            cp.start()
        sends = [pltpu.make_async_remote_copy(
            src_ref=srcs[t], dst_ref=outs[t].at[c], send_sem=send_sems.at[t], recv_sem=recv_sems.at[t],
            device_id=(x, y, 1 - c), device_id_type=MESH) for t in range(n)]
        for cp in sends:
            cp.start()
        for t in range(n):
            pltpu.make_async_remote_copy(
                src_ref=srcs[t], dst_ref=outs[t].at[1 - c], send_sem=send_sems.at[t], recv_sem=recv_sems.at[t],
                device_id=(x, y, 1 - c), device_id_type=MESH).wait_recv()
        for cp in sends:
            cp.wait_send()
        for cp in mine:
            cp.wait()

    out_shape = [jax.ShapeDtypeStruct((2,) + h.shape, h.dtype) for h in hs]
    return pl.pallas_call(
        body, name=name, out_shape=out_shape, in_specs=[HBM] * n, out_specs=[HBM] * n,
        scratch_shapes=[pltpu.SemaphoreType.DMA((n,)), pltpu.SemaphoreType.DMA((n,)), pltpu.SemaphoreType.DMA((n,))],
    )(*hs)


BIG = ("w_in", "w_branch_a", "w_branch_b", "w_out", "w_ffn_in", "w_ffn_out")


def _band_bias(rel_table, name):
    L, H, n = rel_table.shape
    tab = jnp.pad(rel_table, ((0, 0), (0, 0), (0, NREL_PAD - n))).reshape(L * H, 1, NREL_PAD)
    band = (A_PAST + 1) * CHUNK

    def body(t_ref, o_ref):
        r = lax.broadcasted_iota(jnp.int32, (NREL_PAD, SKEW_W), 0)
        xi = lax.broadcasted_iota(jnp.int32, (NREL_PAD, SKEW_W), 1)
        diag = jnp.where(xi < KSPAN, xi, xi - SKEW_W)
        rel = jnp.clip(A_PAST * CHUNK - diag, -A_MAX_REL, A_MAX_REL) + A_MAX_REL
        e = _nn(t_ref[...], jnp.where(rel == r, 1.0, 0.0).astype(F32), HI)
        x = jnp.broadcast_to(e, (QBLK, SKEW_W))
        row = lax.broadcasted_iota(jnp.int32, (QBLK, SKEW_W), 0)
        for b in range(QBLK.bit_length() - 1):
            x = jnp.where(((row >> b) & 1) == 1, pltpu.roll(x, 1 << b, 1), x)
        x = x[:, :KSPAN]
        first = (lax.broadcasted_iota(jnp.int32, (QBLK, KSPAN), 0) // CHUNK) * CHUNK
        col = lax.broadcasted_iota(jnp.int32, (QBLK, KSPAN), 1)
        o_ref[...] = jnp.where((col >= first) & (col < first + band), x, NEG)

    out = pl.pallas_call(
        body, name=name, grid=(L * H,), in_specs=[pl.BlockSpec((None, 1, NREL_PAD), lambda i: (i, 0, 0))],
        out_specs=pl.BlockSpec((None, QBLK, KSPAN), lambda i: (i, 0, 0)),
        out_shape=jax.ShapeDtypeStruct((L * H, QBLK, KSPAN), F32), compiler_params=_par(1),
    )(tab)
    return out.reshape(L, H, QBLK, KSPAN)


def _col_row_forms(t, S):
    nc = S // CHUNK
    return t.T.reshape(nc, CHUNK, B_HEADS), t.reshape(B_HEADS, nc, CHUNK).transpose(1, 0, 2)


def _layer_fwd(l, x, mod, W, P, big):
    S, D = x.shape
    n = lambda s: f"{s}_l{l}"
    sh1, sc1, gt1, sh2, sc2, gt2 = (mod[i:i + 1] for i in range(6))
    h1 = _lnmod_fwd(x, P["norm1_g"][l:l + 1], sc1, sh1, n("ln1"))
    proj = _matmul(h1, W["w_in"], "nn", F32, n("proj"), tn=1152, layer=l)
    ya = _attn_fwd(proj, big, n("attn"))
    ba = proj[:, P_BA:P_BA + 2 * B_HEADS]
    b_t, a_t = ba[:, :B_HEADS].T, ba[:, B_HEADS:].T
    alog, dtb = P["a_log"][l].reshape(B_HEADS, 1), P["dt_bias"][l].reshape(B_HEADS, 1)
    beta, gam = _gdn_gates_fwd(b_t, a_t, alog, dtb, n("gates"))
    bcol, _ = _col_row_forms(beta, S)
    gcol, grow = _col_row_forms(gam, S)
    qn, kn, v = _gdn_pre_fwd(proj, P["w_conv"][l], n("gdnpre"))
    o, tsave, ssave = _gdn_chunk_fwd(qn, kn, v, bcol, gcol, grow, n("gdnchunk"))
    yb = _gdn_post_fwd(o, proj, P["gdn_norm_g"][l:l + 1], n("gdnpost"))
    pa = _matmul(ya, W["w_branch_a"], "nn", F32, n("pa"), tm=2048, layer=l, stacked=True)
    pb = _matmul(yb, W["w_branch_b"], "nn", F32, n("pb"), tm=2048, layer=l, stacked=True)
    merged = _merge_fwd(proj, pa, pb, n("merge"))
    ao = _matmul(merged, W["w_out"], "nn", F32, n("ao"), layer=l)
    x1 = _gate_fwd(x, ao, gt1, n("res1"))
    h2 = _lnmod_fwd(x1, P["norm2_g"][l:l + 1], sc2, sh2, n("ln2"))
    gu = _matmul(h2, W["w_ffn_in"], "nn", F32, n("gu"), layer=l, stacked=True)
    act = _ffn_act_fwd(gu, n("act"))
    fo = _matmul(act, W["w_ffn_out"], "nn", F32, n("fo"), tk=1408, layer=l)
    x2 = _gate_fwd(x1, fo, gt2, n("res2"))
    saved = dict(x=x, h1=h1, proj=proj, ya=ya, b_t=b_t, a_t=a_t, bcol=bcol, gcol=gcol, grow=grow,
                 qn=qn, kn=kn, v=v, o=o, tsave=tsave, ssave=ssave, yb=yb, pa=pa, pb=pb, merged=merged, ao=ao, x1=x1,
                 h2=h2, gu=gu, act=act, fo=fo)
    return x2, saved


def _layer_bwd(l, dx2, sv, mod, W, P, big):
    S, D = dx2.shape
    n = lambda s: f"{s}_l{l}"
    sh1, sc1, gt1, sh2, sc2, gt2 = (mod[i:i + 1] for i in range(6))
    g = {}
    view = lambda t: t.reshape((4, 2, t.shape[-2] // (2 if t.ndim == 3 else 8), t.shape[-1]))
    dz2, dgt2 = _gate_bwd(dx2, sv["fo"], gt2, n("dres2"))
    g["w_ffn_out"] = view(_matmul(sv["act"], dz2, "tn", F32, n("dwfo"), tm=1408))
    dact = _matmul(dz2, W["w_ffn_out"], "nt", F32, n("dact"), tn=1408, layer=l)
    dgu = _ffn_act_bwd(sv["gu"], dact, n("dgu"))
    g["w_ffn_in"] = view(_matmul(sv["h2"], dgu, "tn", F32, n("dwfi"), out_stacked=True))
    dh2 = _matmul(dgu, W["w_ffn_in"], "nt", F32, n("dh2"), layer=l, stacked=True)
    dx1, dsh2, dsc2, dn2 = _lnmod_bwd(dh2, sv["x1"], P["norm2_g"][l:l + 1], sc2, dx2, n("dln2"))
    dz1, dgt1 = _gate_bwd(dx1, sv["ao"], gt1, n("dres1"))
    g["w_out"] = view(_matmul(sv["merged"], dz1, "tn", F32, n("dwo")))
    dmerged = _matmul(dz1, W["w_out"], "nt", F32, n("dmerged"), layer=l)
    dgab, dpa, dpb = _merge_bwd(sv["proj"], sv["pa"], sv["pb"], dmerged, n("dmerge"))
    g["w_branch_a"] = view(_matmul(sv["ya"], dpa, "tn", F32, n("dwa"), out_stacked=True))
    g["w_branch_b"] = view(_matmul(sv["yb"], dpb, "tn", F32, n("dwb"), out_stacked=True))
    dya = _matmul(dpa, W["w_branch_a"], "nt", BF16, n("dya"), tm=2048, layer=l, stacked=True)
    dyb = _matmul(dpb, W["w_branch_b"], "nt", F32, n("dyb"), tm=2048, layer=l, stacked=True)
    dq, dk, dv, dbig = _attn_bwd(sv["proj"], big, dya, n("dattn"))
    g["rel_table"] = _rel_table_grad(dbig, n("drel"))[:, 0, :2 * A_MAX_REL + 1]
    do, dzb, dng = _gdn_post_bwd(dyb, sv["o"], sv["proj"], P["gdn_norm_g"][l:l + 1], n("dgdnpost"))
    g["gdn_norm_g"] = dng[0]
    dqn, dkn, dvv, dbc, dgc, dgr = _gdn_chunk_bwd(sv["qn"], sv["kn"], sv["v"], sv["bcol"], sv["gcol"], sv["grow"],
                                                  sv["tsave"], sv["ssave"], do, n("dgdnchunk"))
    dbeta_t = dbc.reshape(S, B_HEADS).T
    dgam_a = dgc.reshape(S, B_HEADS).T
    dgam_b = dgr.transpose(1, 0, 2).reshape(B_HEADS, S)
    alog, dtb = P["a_log"][l].reshape(B_HEADS, 1), P["dt_bias"][l].reshape(B_HEADS, 1)
    db_t, da_t, dal, ddt = _gdn_gates_bwd(dbeta_t, dgam_a, dgam_b, sv["b_t"], sv["a_t"], alog, dtb, n("dgates"))
    g["a_log"], g["dt_bias"] = dal[:, 0], ddt[:, 0]
    dy = _gdn_pre_bwd_a(sv["proj"], P["w_conv"][l], dqn, dkn, dvv, n("dgdnpre_a"))
    dqkvb, g["w_conv"] = _gdn_pre_bwd_b(sv["proj"], P["w_conv"][l], dy, n("dgdnpre_b"))
    dba = jnp.concatenate([db_t.T, da_t.T, jnp.zeros((S, P_END - P_BA - 2 * B_HEADS), F32)], axis=1)
    dproj = jnp.concatenate([dq.astype(BF16), dk.astype(BF16), dv.astype(BF16), dqkvb, dgab, dzb, dba.astype(BF16)],
                            axis=1)
    g["w_in"] = view(_restore_w_in(_matmul(sv["h1"], dproj, "tn", F32, n("dwin"), tn=1152), n("dwin_cols")))
    dh1 = _matmul(dproj, W["w_in"], "nt", F32, n("dh1"), tk=1152, layer=l)
    dx, dsh1, dsc1, dn1 = _lnmod_bwd(dh1, sv["x"], P["norm1_g"][l:l + 1], sc1, dx1, n("dln1"))
    g["norm1_g"], g["norm2_g"] = dn1[0], dn2[0]
    dmod = jnp.concatenate([dsh1, dsc1, dgt1, dsh2, dsc2, dgt2], axis=1)[0]
    return dx, g, dmod


SMALL = ("norm1_g", "norm2_g", "rel_table", "w_conv", "a_log", "dt_bias", "gdn_norm_g")
SMALL_PACK_C = 1024


def _as_rows(t):
    flat = t.reshape(-1)
    rows = -(-flat.shape[0] // SMALL_PACK_C)
    return jnp.pad(flat, (0, rows * SMALL_PACK_C - flat.shape[0])).reshape(rows, SMALL_PACK_C)


def _pack_rows(parts):
    blk = jnp.concatenate([_as_rows(p) for p in parts], axis=0)
    return jnp.pad(blk, ((0, -blk.shape[0] % 8), (0, 0)))


def _unpack_rows(blk, shapes):
    out, r = [], 0
    for shp in shapes:
        size = int(np.prod(shp))
        rows = -(-size // SMALL_PACK_C)
        out.append(blk[..., r:r + rows, :].reshape(blk.shape[:-2] + (rows * SMALL_PACK_C,))[..., :size]
                   .reshape(blk.shape[:-2] + tuple(shp)))
        r += rows
    return out


def kernel(x, c, w_ada, b_ada, norm1_g, norm2_g, w_in, rel_table, w_conv, a_log, dt_bias, gdn_norm_g, w_branch_a, w_branch_b, w_out, w_ffn_in, w_ffn_out, final_g, loss_target, m_w_ada, m_b_ada, m_norm1_g, m_norm2_g, m_w_in, m_rel_table, m_w_conv, m_a_log, m_dt_bias, m_gdn_norm_g, m_w_branch_a, m_w_branch_b, m_w_out, m_w_ffn_in, m_w_ffn_out, m_final_g, v_w_ada, v_b_ada, v_norm1_g, v_norm2_g, v_w_in, v_rel_table, v_w_conv, v_a_log, v_dt_bias, v_gdn_norm_g, v_w_branch_a, v_w_branch_b, v_w_out, v_w_ffn_in, v_w_ffn_out, v_final_g):
    weights = dict(w_ada=w_ada, b_ada=b_ada, norm1_g=norm1_g, norm2_g=norm2_g, w_in=w_in, rel_table=rel_table,
                   w_conv=w_conv, a_log=a_log, dt_bias=dt_bias, gdn_norm_g=gdn_norm_g, w_branch_a=w_branch_a,
                   w_branch_b=w_branch_b, w_out=w_out, w_ffn_in=w_ffn_in, w_ffn_out=w_ffn_out, final_g=final_g)
    mom_m = dict(w_ada=m_w_ada, b_ada=m_b_ada, norm1_g=m_norm1_g, norm2_g=m_norm2_g, w_in=m_w_in,
                 rel_table=m_rel_table, w_conv=m_w_conv, a_log=m_a_log, dt_bias=m_dt_bias, gdn_norm_g=m_gdn_norm_g,
                 w_branch_a=m_w_branch_a, w_branch_b=m_w_branch_b, w_out=m_w_out, w_ffn_in=m_w_ffn_in,
                 w_ffn_out=m_w_ffn_out, final_g=m_final_g)
    mom_v = dict(w_ada=v_w_ada, b_ada=v_b_ada, norm1_g=v_norm1_g, norm2_g=v_norm2_g, w_in=v_w_in,
                 rel_table=v_rel_table, w_conv=v_w_conv, a_log=v_a_log, dt_bias=v_dt_bias, gdn_norm_g=v_gdn_norm_g,
                 w_branch_a=v_w_branch_a, w_branch_b=v_w_branch_b, w_out=v_w_out, w_ffn_in=v_w_ffn_in,
                 w_ffn_out=v_w_ffn_out, final_g=v_final_g)
    xi, yi, ci = _position()
    chip = 2 * xi + yi
    dev = 2 * chip + ci
    L, D = norm1_g.shape
    NMOD = b_ada.shape[1] // D
    ns = w_ada.shape[2]
    cs = w_conv.shape[2]

    first_blk = _pack_rows([c, w_conv])
    first_all = _allgather8(first_blk, "gather_c").reshape(8, first_blk.shape[0], SMALL_PACK_C)
    c_all, w_conv_all = _unpack_rows(first_all, [(D,), w_conv.shape])
    w_conv_full = w_conv_all.reshape(4, 2, L, CONV_K, cs)[:, 0].transpose(1, 2, 0, 3).reshape(L, CONV_K, 4 * cs)
    b_shard = lax.dynamic_slice_in_dim(b_ada, chip * ns, ns, axis=1).reshape(L, 1, ns)
    mod_shard = _ada_mod(c_all, w_ada, b_shard, "ada_mod")
    mod_all = _allgather8(mod_shard.reshape(L * 8, ns), "gather_mod").reshape(4, 2, L, 8, ns)
    mod = lax.dynamic_index_in_dim(mod_all[:, 0], dev, axis=2, keepdims=False)
    mod = mod.transpose(1, 0, 2).reshape(L, NMOD, D)

    shards = [weights[k].astype(BF16) for k in BIG]
    shards = [s.reshape(s.shape[0], 2, s.shape[1] // 2, s.shape[2]) for s in shards]
    gathered = dict(zip(BIG, _gather_weights(shards, "gather_weights")))
    col_stacked = lambda t: t.reshape(t.shape[0], 4, 2 * t.shape[3], t.shape[4])
    row_joined = lambda t: t.reshape(t.shape[0], 8 * t.shape[3], t.shape[4])
    W = dict(w_in=_reorder_w_in(col_stacked(gathered["w_in"]), "w_in_cols"),
             w_branch_a=col_stacked(gathered["w_branch_a"]), w_branch_b=col_stacked(gathered["w_branch_b"]),
             w_ffn_in=col_stacked(gathered["w_ffn_in"]), w_out=row_joined(gathered["w_out"]),
             w_ffn_out=row_joined(gathered["w_ffn_out"]))
    P = dict(norm1_g=norm1_g, norm2_g=norm2_g, w_conv=w_conv_full, a_log=a_log, dt_bias=dt_bias,
             gdn_norm_g=gdn_norm_g)
    big = _band_bias(rel_table, "band_bias")

    xc = x[0]
    saved = []
    for l in range(L):
        xc, sv = _layer_fwd(l, xc, mod[l], W, P, big[l])
        saved.append(sv)
    dx, loss_dev, dfinal = _loss_head(xc, final_g.reshape(1, D), loss_target[0], "loss_head")

    where = jnp.stack([ci, chip]).astype(jnp.int32)
    grads = [None] * L
    dmods = [None] * L
    shard_grads = {k: [None] * L for k in BIG}
    for l in reversed(range(L)):
        dx, grads[l], dmods[l] = _layer_bwd(l, dx, saved[l], mod[l], W, P, big[l])
        gs = [grads[l][k] for k in BIG]
        from_sibling = _rs_sibling([g_.astype(BF16) for g_ in gs], f"rs_sibling_l{l}")
        sums = [_pair_sums(g_, where, r_, f"rs_pair_sum_{k}_l{l}") for k, g_, r_ in zip(BIG, gs, from_sibling)]
        from_chips = _rs_chips([s_[0] for s_ in sums], f"rs_chips_l{l}")
        halves = [_sum_own_and_received(s_[1], r_, f"rs_sum_{k}_l{l}") for k, s_, r_ in zip(BIG, sums, from_chips)]
        for k, t in zip(BIG, _rs_pair(halves, f"rs_pair_l{l}")):
            shard_grads[k][l] = t.reshape(2 * t.shape[1], t.shape[2])
    dmod = jnp.stack(dmods)

    small = {k: jnp.stack([grads[l][k] for l in range(L)]) for k in SMALL}
    parts = [dmod] + [small[k] for k in SMALL] + [dfinal, loss_dev[0, :1]]
    small_blk = _pack_rows(parts)
    srows = small_blk.shape[0]
    small_all, small_sum = _allgather8(small_blk, "gather_small", reduce_rows=srows)
    shapes = [dmod.shape] + [small[k].shape for k in SMALL] + [(D,), (1,)]
    tot = _unpack_rows(small_sum, shapes)
    G = dict(zip(SMALL, tot[1:1 + len(SMALL)]))
    G["b_ada"] = tot[0].reshape(b_ada.shape)
    G["w_conv"] = lax.dynamic_slice_in_dim(G["w_conv"], chip * cs, cs, axis=2)
    G["final_g"] = tot[-2]
    loss = tot[-1][0]
    dmod_all = _unpack_rows(small_all.reshape(8, srows, SMALL_PACK_C), [dmod.shape])[0]
    dmod_cols = lax.dynamic_slice_in_dim(dmod_all, chip * ns, ns, axis=2).transpose(1, 0, 2)
    G["w_ada"] = _ada_wgrad(c_all, dmod_cols, "ada_wgrad")

    order = ["w_ada", "b_ada", "norm1_g", "norm2_g", "w_in", "rel_table", "w_conv", "a_log", "dt_bias", "gdn_norm_g",
             "w_branch_a", "w_branch_b", "w_out", "w_ffn_in", "w_ffn_out", "final_g"]
    deltas, new_m, new_v = {}, {}, {}
    for k in order:
        w = weights[k]
        if k in BIG:
            G[k], deltas[k], new_m[k], new_v[k] = _adamw_layers(w, shard_grads[k], mom_m[k], mom_v[k], f"adamw_{k}")
            continue
        as2d = (lambda t: t.reshape(1, -1)) if w.ndim == 1 else (lambda t: t)
        d_, m_, v_ = _adamw(as2d(w), as2d(G[k]), as2d(mom_m[k]), as2d(mom_v[k]), f"adamw_{k}")
        deltas[k], new_m[k], new_v[k] = d_.reshape(w.shape), m_.reshape(w.shape), v_.reshape(w.shape)
    return (loss, dx[None], *[G[k] for k in order], *[deltas[k] for k in order], *[new_m[k] for k in order],
            *[new_v[k] for k in order])
```

```python
import functools

import numpy as np
import jax
import jax.numpy as jnp
from jax import lax
from jax.experimental import pallas as pl
from jax.experimental.pallas import tpu as pltpu

F32 = jnp.float32
BF16 = jnp.bfloat16
HI = lax.Precision.HIGHEST
SOLVE_PREC = lax.Precision.HIGH
MESH = pl.DeviceIdType.MESH

EPS = 1e-6
CHUNK = 64
A_HEADS = 8
A_DH = 64
A_PAST = 8
A_MAX_REL = 128
B_HEADS = 4
B_DH = 128
CONV_K = 4
LANE = 128
QBLK = 4 * CHUNK
KSPAN = QBLK + A_PAST * CHUNK
NEG = -1e30

ADAM_LR = 0.001
ADAM_B1 = 0.9
ADAM_B2 = 0.999
ADAM_EPS = 1e-08
ADAM_WD = 0.01
ADAM_STEP = 10

P_QKVA, P_QKVB, P_GA, P_GB, P_Z, P_BA, P_END = 0, 1536, 3072, 4096, 5120, 5632, 5760
W_IN_SHARD = 1410


def _sigmoid(x):
    return 1.0 / (1.0 + jnp.exp(-x))


def _nn(a, b, prec=None):
    return lax.dot_general(a, b, (((1,), (0,)), ((), ())), precision=prec, preferred_element_type=F32)


def _nt(a, b, prec=None):
    return lax.dot_general(a, b, (((1,), (1,)), ((), ())), precision=prec, preferred_element_type=F32)


def _tn(a, b, prec=None):
    return lax.dot_general(a, b, (((0,), (0,)), ((), ())), precision=prec, preferred_element_type=F32)


def _bnn(a, b):
    return _nn(a.astype(BF16), b.astype(BF16))


def _bnt(a, b):
    return _nt(a.astype(BF16), b.astype(BF16))


def _btn(a, b):
    return _tn(a.astype(BF16), b.astype(BF16))


def _pick(n, target, unit=LANE):
    best = None
    for t in range(unit, min(n, target) + 1, unit):
        if n % t == 0:
            best = t
    return best if best is not None else n


def _acc(ref, val, i):
    @pl.when(i == 0)
    def _():
        ref[...] = val

    @pl.when(i != 0)
    def _():
        ref[...] += val


def _arb(n):
    return pltpu.CompilerParams(dimension_semantics=("arbitrary",) * n)


def _par(n):
    return pltpu.CompilerParams(dimension_semantics=("parallel",) * n)


def _matmul(a, b, mode, out_dtype, name, tm=1024, tn=1024, tk=1024, layer=None, stacked=False, out_stacked=False):
    bs = b.shape[1:] if layer is not None else b.shape
    if mode == "nn":
        M, K = a.shape
        N = 4 * bs[2] if stacked else bs[1]
        if stacked:
            tn = bs[2]
    elif mode == "nt":
        M, K = a.shape
        N = bs[1] if stacked else bs[0]
        if stacked:
            tk = bs[2]
    else:
        K, M = a.shape
        N = bs[1]
        if out_stacked:
            tn = N // 4
    tm, tn, tk = _pick(M, tm), _pick(N, tn), _pick(K, tk)
    nk = K // tk
    lead = () if layer is None else (layer,)
    lead_blk = () if layer is None else (None,)
    if mode == "nn":
        a_spec = pl.BlockSpec((tm, tk), lambda i, j, k: (i, k))
        if stacked:
            b_spec = pl.BlockSpec(lead_blk + (None, tk, tn), lambda i, j, k: lead + (j, k, 0))
        else:
            b_spec = pl.BlockSpec(lead_blk + (tk, tn), lambda i, j, k: lead + (k, j))
        dot = _nn
    elif mode == "nt":
        a_spec = pl.BlockSpec((tm, tk), lambda i, j, k: (i, k))
        if stacked:
            b_spec = pl.BlockSpec(lead_blk + (None, tn, tk), lambda i, j, k: lead + (k, j, 0))
        else:
            b_spec = pl.BlockSpec(lead_blk + (tn, tk), lambda i, j, k: lead + (j, k))
        dot = _nt
    else:
        a_spec = pl.BlockSpec((tk, tm), lambda i, j, k: (k, i))
        b_spec = pl.BlockSpec((tk, tn), lambda i, j, k: (k, j))
        dot = _tn
    if out_stacked:
        o_spec = pl.BlockSpec((None, tm, tn), lambda i, j, k: (j, i, 0))
        o_shape = jax.ShapeDtypeStruct((4, M, tn), out_dtype)
    else:
        o_spec = pl.BlockSpec((tm, tn), lambda i, j, k: (i, j))
        o_shape = jax.ShapeDtypeStruct((M, N), out_dtype)

    def body_single(a_ref, b_ref, o_ref):
        o_ref[...] = dot(a_ref[...], b_ref[...]).astype(out_dtype)

    def body(a_ref, b_ref, o_ref, acc_ref):
        k = pl.program_id(2)

        @pl.when(k == 0)
        def _():
            acc_ref[...] = jnp.zeros_like(acc_ref)

        acc_ref[...] += dot(a_ref[...], b_ref[...])

        @pl.when(k == nk - 1)
        def _():
            o_ref[...] = acc_ref[...].astype(out_dtype)

    return pl.pallas_call(
        body_single if nk == 1 else body, name=name, grid=(M // tm, N // tn, nk), in_specs=[a_spec, b_spec],
        out_specs=o_spec, out_shape=o_shape, scratch_shapes=[] if nk == 1 else [pltpu.VMEM((tm, tn), F32)],
        compiler_params=pltpu.CompilerParams(dimension_semantics=("parallel", "parallel", "arbitrary")),
    )(a, b)


def _rows(tm, n, col=0):
    return pl.BlockSpec((tm, n), lambda i: (i, col))


def _vec(n):
    return pl.BlockSpec((1, n), lambda i: (0, 0))


def _lnmod_fwd(x, g, sc, sh, name):
    S, D = x.shape
    tm = _pick(S, 512, 8)

    def body(x_ref, g_ref, sc_ref, sh_ref, o_ref):
        xv = x_ref[...]
        r = lax.rsqrt(jnp.mean(xv * xv, axis=-1, keepdims=True) + EPS)
        o_ref[...] = ((xv * r * g_ref[...]) * (1.0 + sc_ref[...]) + sh_ref[...]).astype(BF16)

    return pl.pallas_call(
        body, name=name, grid=(S // tm,),
        in_specs=[_rows(tm, D), _vec(D), _vec(D), _vec(D)], out_specs=_rows(tm, D),
        out_shape=jax.ShapeDtypeStruct((S, D), BF16), compiler_params=_par(1),
    )(x, g, sc, sh)


def _lnmod_bwd(dh, x, g, sc, dres, name):
    S, D = x.shape
    tm = _pick(S, 512, 8)

    def body(dh_ref, x_ref, g_ref, sc_ref, dres_ref, dx_ref, dsh_ref, dsc_ref, dg_ref):
        i = pl.program_id(0)
        xv = x_ref[...]
        dh_ = dh_ref[...]
        r = lax.rsqrt(jnp.mean(xv * xv, axis=-1, keepdims=True) + EPS)
        xhat = xv * r
        gv = g_ref[...]
        dn = dh_ * (1.0 + sc_ref[...])
        dxhat = dn * gv
        dx_ref[...] = dres_ref[...] + r * (dxhat - xhat * jnp.mean(dxhat * xhat, axis=-1, keepdims=True))
        _acc(dsh_ref, jnp.sum(dh_, axis=0, keepdims=True), i)
        _acc(dsc_ref, jnp.sum(dh_ * (xhat * gv), axis=0, keepdims=True), i)
        _acc(dg_ref, jnp.sum(dn * xhat, axis=0, keepdims=True), i)

    return pl.pallas_call(
        body, name=name, grid=(S // tm,),
        in_specs=[_rows(tm, D), _rows(tm, D), _vec(D), _vec(D), _rows(tm, D)],
        out_specs=[_rows(tm, D), _vec(D), _vec(D), _vec(D)],
        out_shape=[jax.ShapeDtypeStruct((S, D), F32)] + [jax.ShapeDtypeStruct((1, D), F32)] * 3,
        compiler_params=_arb(1),
    )(dh, x, g, sc, dres)


def _gate_fwd(x, y, gt, name):
    S, D = x.shape
    tm = _pick(S, 512, 8)

    def body(x_ref, y_ref, gt_ref, o_ref):
        o_ref[...] = x_ref[...] + gt_ref[...] * y_ref[...]

    return pl.pallas_call(
        body, name=name, grid=(S // tm,), in_specs=[_rows(tm, D), _rows(tm, D), _vec(D)], out_specs=_rows(tm, D),
        out_shape=jax.ShapeDtypeStruct((S, D), F32), compiler_params=_par(1),
    )(x, y, gt)


def _gate_bwd(dx, y, gt, name):
    S, D = dx.shape
    tm = _pick(S, 512, 8)

    def body(dx_ref, y_ref, gt_ref, dz_ref, dgt_ref):
        i = pl.program_id(0)
        d = dx_ref[...]
        dz_ref[...] = (d * gt_ref[...]).astype(BF16)
        _acc(dgt_ref, jnp.sum(d * y_ref[...], axis=0, keepdims=True), i)

    return pl.pallas_call(
        body, name=name, grid=(S // tm,), in_specs=[_rows(tm, D), _rows(tm, D), _vec(D)],
        out_specs=[_rows(tm, D), _vec(D)],
        out_shape=[jax.ShapeDtypeStruct((S, D), BF16), jax.ShapeDtypeStruct((1, D), F32)],
        compiler_params=_arb(1),
    )(dx, y, gt)


def _ffn_act_fwd(gu, name):
    S, H2 = gu.shape
    H = H2 // 2
    tm = _pick(S, 256, 8)

    def body(g_ref, u_ref, o_ref):
        gv = g_ref[...]
        o_ref[...] = (gv * _sigmoid(gv) * u_ref[...]).astype(BF16)

    return pl.pallas_call(
        body, name=name, grid=(S // tm,), in_specs=[_rows(tm, H, 0), _rows(tm, H, 1)], out_specs=_rows(tm, H),
        out_shape=jax.ShapeDtypeStruct((S, H), BF16), compiler_params=_par(1),
    )(gu, gu)


def _ffn_act_bwd(gu, dact, name):
    S, H2 = gu.shape
    H = H2 // 2
    tm = _pick(S, 256, 8)

    def body(g_ref, u_ref, da_ref, o_ref):
        gv = g_ref[...]
        s = _sigmoid(gv)
        da = da_ref[...]
        o_ref[:, :H] = (da * u_ref[...] * (s * (1.0 + gv * (1.0 - s)))).astype(BF16)
        o_ref[:, H:] = (da * (gv * s)).astype(BF16)

    return pl.pallas_call(
        body, name=name, grid=(S // tm,), in_specs=[_rows(tm, H, 0), _rows(tm, H, 1), _rows(tm, H)],
        out_specs=_rows(tm, H2), out_shape=jax.ShapeDtypeStruct((S, H2), BF16), compiler_params=_par(1),
    )(gu, gu, dact)


def _merge_fwd(proj, pa, pb, name):
    S, D = pa.shape
    tm = _pick(S, 512, 8)

    def body(ga_ref, gb_ref, pa_ref, pb_ref, o_ref):
        o_ref[...] = (_sigmoid(ga_ref[...]) * pa_ref[...] + _sigmoid(gb_ref[...]) * pb_ref[...]).astype(BF16)

    return pl.pallas_call(
        body, name=name, grid=(S // tm,),
        in_specs=[_rows(tm, D, P_GA // D), _rows(tm, D, P_GB // D), _rows(tm, D), _rows(tm, D)],
        out_specs=_rows(tm, D), out_shape=jax.ShapeDtypeStruct((S, D), BF16), compiler_params=_par(1),
    )(proj, proj, pa, pb)


def _merge_bwd(proj, pa, pb, dm, name):
    S, D = pa.shape
    tm = _pick(S, 512, 8)

    def body(ga_ref, gb_ref, pa_ref, pb_ref, dm_ref, dg_ref, dpa_ref, dpb_ref):
        d = dm_ref[...]
        sa = _sigmoid(ga_ref[...])
        sb = _sigmoid(gb_ref[...])
        dg_ref[:, :D] = (d * pa_ref[...] * sa * (1.0 - sa)).astype(BF16)
        dg_ref[:, D:] = (d * pb_ref[...] * sb * (1.0 - sb)).astype(BF16)
        dpa_ref[...] = (d * sa).astype(BF16)
        dpb_ref[...] = (d * sb).astype(BF16)

    return pl.pallas_call(
        body, name=name, grid=(S // tm,),
        in_specs=[_rows(tm, D, P_GA // D), _rows(tm, D, P_GB // D), _rows(tm, D), _rows(tm, D), _rows(tm, D)],
        out_specs=[_rows(tm, 2 * D), _rows(tm, D), _rows(tm, D)],
        out_shape=[jax.ShapeDtypeStruct((S, 2 * D), BF16), jax.ShapeDtypeStruct((S, D), BF16),
                   jax.ShapeDtypeStruct((S, D), BF16)],
        compiler_params=_par(1),
    )(proj, proj, pa, pb, dm)


def _loss_head(x, g, target, name):
    S, D = x.shape
    tm = _pick(S, 512, 8)

    def body(x_ref, g_ref, t_ref, dx_ref, loss_ref, dg_ref):
        i = pl.program_id(0)
        xv = x_ref[...]
        gv = g_ref[...]
        r = lax.rsqrt(jnp.mean(xv * xv, axis=-1, keepdims=True) + EPS)
        xhat = xv * r
        err = xhat * gv - t_ref[...]
        part = 0.5 * jnp.sum(jnp.mean(err * err, axis=-1, keepdims=True), axis=0, keepdims=True)
        _acc(loss_ref, jnp.broadcast_to(part, (1, LANE)), i)
        dy = err * (1.0 / D)
        _acc(dg_ref, jnp.sum(dy * xhat, axis=0, keepdims=True), i)
        dxhat = dy * gv
        dx_ref[...] = r * (dxhat - xhat * jnp.mean(dxhat * xhat, axis=-1, keepdims=True))

    return pl.pallas_call(
        body, name=name, grid=(S // tm,), in_specs=[_rows(tm, D), _vec(D), _rows(tm, D)],
        out_specs=[_rows(tm, D), _vec(LANE), _vec(D)],
        out_shape=[jax.ShapeDtypeStruct((S, D), F32), jax.ShapeDtypeStruct((1, LANE), F32),
                   jax.ShapeDtypeStruct((1, D), F32)],
        compiler_params=_arb(1),
    )(x, g, target)


HEADS_PER_SLAB = LANE // A_DH
N_SLABS = A_HEADS // HEADS_PER_SLAB
SPAN_BLOCKS = KSPAN // QBLK


def _attn_specs(seg):
    q_spec = pl.BlockSpec((QBLK, LANE), lambda p, m: (m, seg[0] * N_SLABS + p))
    k_specs = [pl.BlockSpec((QBLK, LANE), functools.partial(
        lambda j, p, m: (jnp.maximum(m - (SPAN_BLOCKS - 1) + j, 0), seg[1] * N_SLABS + p), j)) for j in range(SPAN_BLOCKS)]
    v_specs = [pl.BlockSpec((QBLK, LANE), functools.partial(
        lambda j, p, m: (jnp.maximum(m - (SPAN_BLOCKS - 1) + j, 0), seg[2] * N_SLABS + p), j)) for j in range(SPAN_BLOCKS)]
    b_spec = pl.BlockSpec((HEADS_PER_SLAB, QBLK, KSPAN), lambda p, m: (p, 0, 0))
    return q_spec, k_specs, v_specs, b_spec


def _head_lanes(t, hh):
    lane = lax.broadcasted_iota(jnp.int32, t.shape, 1)
    return jnp.where((lane // A_DH) == hh, t, jnp.zeros_like(t))


def _attn_probs(qh, k, bias, m):
    s = _nt(qh, k) * (A_DH ** -0.5) + bias
    key_pos = lax.broadcasted_iota(jnp.int32, (QBLK, KSPAN), 1) + (m - (SPAN_BLOCKS - 1)) * QBLK
    s = jnp.where(key_pos >= 0, s, NEG)
    p = jnp.exp(s - jnp.max(s, axis=-1, keepdims=True))
    return p / jnp.sum(p, axis=-1, keepdims=True)


def _attn_fwd(proj, big, name):
    S = proj.shape[0]
    q_spec, k_specs, v_specs, b_spec = _attn_specs((0, 1, 2))

    def body(q_ref, k0, k1, k2, v0, v1, v2, b_ref, o_ref):
        m = pl.program_id(1)
        q = q_ref[...].astype(BF16)
        k = jnp.concatenate([k0[...], k1[...], k2[...]], axis=0).astype(BF16)
        v = jnp.concatenate([v0[...], v1[...], v2[...]], axis=0).astype(BF16)
        outs = [_nn(_attn_probs(_head_lanes(q, hh), k, b_ref[hh], m).astype(BF16), v) for hh in range(HEADS_PER_SLAB)]
        lane = lax.broadcasted_iota(jnp.int32, (QBLK, LANE), 1)
        o_ref[...] = jnp.where(lane < A_DH, outs[0], outs[1]).astype(BF16)

    return pl.pallas_call(
        body, name=name, grid=(N_SLABS, S // QBLK), in_specs=[q_spec] + k_specs + v_specs + [b_spec],
        out_specs=pl.BlockSpec((QBLK, LANE), lambda p, m: (m, p)),
        out_shape=jax.ShapeDtypeStruct((S, A_HEADS * A_DH), BF16), compiler_params=_par(2),
    )(proj, proj, proj, proj, proj, proj, proj, big)


def _attn_bwd(proj, big, dya, name):
    S = proj.shape[0]
    W = A_HEADS * A_DH
    q_spec, k_specs, v_specs, b_spec = _attn_specs((0, 1, 2))
    out_q = pl.BlockSpec((QBLK, LANE), lambda p, m: (m, p))
    out_kv = pl.BlockSpec((S, LANE), lambda p, m: (0, p))

    def body(q_ref, k0, k1, k2, v0, v1, v2, b_ref, do_ref, dq_ref, dk_ref, dv_ref, db_ref):
        m = pl.program_id(1)

        @pl.when(m == 0)
        def _():
            dk_ref[...] = jnp.zeros_like(dk_ref)
            dv_ref[...] = jnp.zeros_like(dv_ref)
            db_ref[...] = jnp.zeros_like(db_ref)

        q = q_ref[...].astype(BF16)
        k = jnp.concatenate([k0[...], k1[...], k2[...]], axis=0).astype(BF16)
        v = jnp.concatenate([v0[...], v1[...], v2[...]], axis=0).astype(BF16)
        do = do_ref[...]
        dqs = []
        dk = jnp.zeros((KSPAN, LANE), F32)
        dv = jnp.zeros((KSPAN, LANE), F32)
        for hh in range(HEADS_PER_SLAB):
            qh = _head_lanes(q, hh)
            doh = _head_lanes(do, hh)
            p = _attn_probs(qh, k, b_ref[hh], m)
            dp = _nt(doh, v)
            ds = p * (dp - jnp.sum(p * dp, axis=-1, keepdims=True))
            db_ref[hh] += ds
            dsb = (ds * (A_DH ** -0.5)).astype(BF16)
            dqs.append(_nn(dsb, k))
            dk = dk + _tn(dsb, qh)
            dv = dv + _tn(p.astype(BF16), doh)
        lane = lax.broadcasted_iota(jnp.int32, (QBLK, LANE), 1)
        dq_ref[...] = jnp.where(lane < A_DH, dqs[0], dqs[1])
        for j in range(SPAN_BLOCKS):
            blk = m - (SPAN_BLOCKS - 1) + j

            @pl.when(blk >= 0)
            def _():
                off = pl.multiple_of(blk * QBLK, QBLK)
                dk_ref[pl.ds(off, QBLK), :] += dk[j * QBLK:(j + 1) * QBLK]
                dv_ref[pl.ds(off, QBLK), :] += dv[j * QBLK:(j + 1) * QBLK]

    return pl.pallas_call(
        body, name=name, grid=(N_SLABS, S // QBLK),
        in_specs=[q_spec] + k_specs + v_specs + [b_spec, pl.BlockSpec((QBLK, LANE), lambda p, m: (m, p))],
        out_specs=[out_q, out_kv, out_kv, b_spec],
        out_shape=[jax.ShapeDtypeStruct((S, W), F32)] * 3 + [jax.ShapeDtypeStruct((A_HEADS, QBLK, KSPAN), F32)],
        compiler_params=_arb(2),
    )(proj, proj, proj, proj, proj, proj, proj, big, dya)


NREL_PAD = 3 * LANE
SKEW_W = 1024


def _rel_table_grad(dbig, name):
    H, R, C = dbig.shape

    def body(d_ref, o_ref):
        x = jnp.concatenate([d_ref[...], jnp.zeros((R, SKEW_W - C), F32)], axis=1)
        row = lax.broadcasted_iota(jnp.int32, (R, SKEW_W), 0)
        for b in range(R.bit_length() - 1):
            x = jnp.where(((row >> b) & 1) == 1, pltpu.roll(x, SKEW_W - (1 << b), 1), x)
        e = jnp.sum(x, axis=0, keepdims=True)
        xi = lax.broadcasted_iota(jnp.int32, (SKEW_W, NREL_PAD), 0)
        r = lax.broadcasted_iota(jnp.int32, (SKEW_W, NREL_PAD), 1)
        diag = jnp.where(xi < C, xi, xi - SKEW_W)
        rel = jnp.clip(A_PAST * CHUNK - diag, -A_MAX_REL, A_MAX_REL) + A_MAX_REL
        o_ref[...] = _nn(e, jnp.where(rel == r, 1.0, 0.0).astype(F32), HI)

    return pl.pallas_call(
        body, name=name, grid=(H,), in_specs=[pl.BlockSpec((None, R, C), lambda h: (h, 0, 0))],
        out_specs=pl.BlockSpec((None, 1, NREL_PAD), lambda h: (h, 0, 0)),
        out_shape=jax.ShapeDtypeStruct((H, 1, NREL_PAD), F32), compiler_params=_par(1),
    )(dbig)


def _chunk_cumsum_matrix(n, reverse):
    j = lax.broadcasted_iota(jnp.int32, (n, n), 0)
    i = lax.broadcasted_iota(jnp.int32, (n, n), 1)
    same = (j // CHUNK) == (i // CHUNK)
    return jnp.where(same & ((j >= i) if reverse else (j <= i)), 1.0, 0.0).astype(F32)


def _gdn_gates_fwd(b_t, a_t, alog, dtb, name):
    Hh, S = b_t.shape
    tl = _pick(S, 512)
    row = pl.BlockSpec((Hh, tl), lambda i: (0, i))
    col = pl.BlockSpec((Hh, 1), lambda i: (0, 0))

    def body(b_ref, a_ref, al_ref, dt_ref, beta_ref, gam_ref):
        z = a_ref[...] + dt_ref[...]
        sp = jnp.maximum(z, 0.0) + jnp.log(1.0 + jnp.exp(-jnp.abs(z)))
        g = -jnp.exp(al_ref[...]) * sp
        beta_ref[...] = _sigmoid(b_ref[...])
        gam_ref[...] = _nn(g, _chunk_cumsum_matrix(tl, False), HI)

    return pl.pallas_call(
        body, name=name, grid=(S // tl,), in_specs=[row, row, col, col], out_specs=[row, row],
        out_shape=[jax.ShapeDtypeStruct((Hh, S), F32)] * 2, compiler_params=_par(1),
    )(b_t, a_t, alog, dtb)


def _gdn_gates_bwd(dbeta, dgam_a, dgam_b, b_t, a_t, alog, dtb, name):
    Hh, S = b_t.shape
    tl = _pick(S, 512)
    row = pl.BlockSpec((Hh, tl), lambda i: (0, i))
    col = pl.BlockSpec((Hh, 1), lambda i: (0, 0))
    accs = pl.BlockSpec((Hh, LANE), lambda i: (0, 0))

    def body(dbeta_ref, dga_ref, dgb_ref, b_ref, a_ref, al_ref, dt_ref, db_ref, da_ref, dal_ref, ddt_ref):
        i = pl.program_id(0)
        z = a_ref[...] + dt_ref[...]
        sp = jnp.maximum(z, 0.0) + jnp.log(1.0 + jnp.exp(-jnp.abs(z)))
        ea = jnp.exp(al_ref[...])
        dg = _nn(dga_ref[...] + dgb_ref[...], _chunk_cumsum_matrix(tl, True), HI)
        da = dg * (-ea) * _sigmoid(z)
        beta = _sigmoid(b_ref[...])
        db_ref[...] = dbeta_ref[...] * beta * (1.0 - beta)
        da_ref[...] = da
        _acc(dal_ref, jnp.broadcast_to(jnp.sum(dg * (-ea * sp), axis=1, keepdims=True), (Hh, LANE)), i)
        _acc(ddt_ref, jnp.broadcast_to(jnp.sum(da, axis=1, keepdims=True), (Hh, LANE)), i)

    return pl.pallas_call(
        body, name=name, grid=(S // tl,), in_specs=[row] * 5 + [col, col], out_specs=[row, row, accs, accs],
        out_shape=[jax.ShapeDtypeStruct((Hh, S), F32)] * 2 + [jax.ShapeDtypeStruct((Hh, LANE), F32)] * 2,
        compiler_params=_arb(1),
    )(dbeta, dgam_a, dgam_b, b_t, a_t, alog, dtb)


HALO = 8


def _conv_silu(xx_ref, w_ref, tm):
    y = w_ref[0:1, :] * xx_ref[pl.ds(HALO - CONV_K + 1, tm), :]
    for j in range(1, CONV_K):
        y = y + w_ref[j:j + 1, :] * xx_ref[pl.ds(HALO - CONV_K + 1 + j, tm), :]
    return y, y * _sigmoid(y)


def _fill_prev_halo(xx_ref, x_ref, prev_ref, i, tm):
    xx_ref[pl.ds(HALO, tm), :] = x_ref[...]

    @pl.when(i == 0)
    def _():
        xx_ref[pl.ds(0, HALO), :] = jnp.zeros((HALO, xx_ref.shape[1]), F32)

    @pl.when(i != 0)
    def _():
        xx_ref[pl.ds(0, HALO), :] = prev_ref[...]


def _gdn_pre_specs(tm, C, colblk):
    cur = pl.BlockSpec((tm, C), lambda i: (i, colblk))
    prev = pl.BlockSpec((HALO, C), lambda i: (jnp.maximum(i * (tm // HALO) - 1, 0), colblk))
    return cur, prev


def _gdn_pre_fwd(proj, wconv, name):
    S = proj.shape[0]
    C = 3 * B_HEADS * B_DH
    W = B_HEADS * B_DH
    tm = _pick(S, 256, 8)
    cur, prev = _gdn_pre_specs(tm, C, P_QKVB // C)

    def body(x_ref, prev_ref, w_ref, q_ref, k_ref, v_ref, xx_ref):
        i = pl.program_id(0)
        _fill_prev_halo(xx_ref, x_ref, prev_ref, i, tm)
        _, sl = _conv_silu(xx_ref, w_ref, tm)
        for h in range(B_HEADS):
            hs = slice(h * B_DH, (h + 1) * B_DH)
            q = sl[:, h * B_DH:(h + 1) * B_DH]
            k = sl[:, W + h * B_DH:W + (h + 1) * B_DH]
            q_ref[:, hs] = q * (lax.rsqrt(jnp.sum(q * q, axis=-1, keepdims=True) + EPS) * (B_DH ** -0.5))
            k_ref[:, hs] = k * lax.rsqrt(jnp.sum(k * k, axis=-1, keepdims=True) + EPS)
        v_ref[...] = sl[:, 2 * W:]

    return pl.pallas_call(
        body, name=name, grid=(S // tm,), in_specs=[cur, prev, pl.BlockSpec((CONV_K, C), lambda i: (0, 0))],
        out_specs=[_rows(tm, W)] * 3, out_shape=[jax.ShapeDtypeStruct((S, W), F32)] * 3,
        scratch_shapes=[pltpu.VMEM((HALO + tm, C), F32)], compiler_params=_par(1),
    )(proj, proj, wconv)


def _gdn_pre_bwd_a(proj, wconv, dqn, dkn, dv, name):
    S = proj.shape[0]
    C = 3 * B_HEADS * B_DH
    W = B_HEADS * B_DH
    tm = _pick(S, 256, 8)
    cur, prev = _gdn_pre_specs(tm, C, P_QKVB // C)

    def body(x_ref, prev_ref, w_ref, dq_ref, dk_ref, dv_ref, dy_ref, xx_ref):
        i = pl.program_id(0)
        _fill_prev_halo(xx_ref, x_ref, prev_ref, i, tm)
        y, sl = _conv_silu(xx_ref, w_ref, tm)
        sg = _sigmoid(y)
        dsilu = sg * (1.0 + y * (1.0 - sg))
        for h in range(B_HEADS):
            for base, d_ref, c in ((0, dq_ref, B_DH ** -0.5), (W, dk_ref, 1.0)):
                lo = base + h * B_DH
                t = sl[:, lo:lo + B_DH]
                d = d_ref[:, h * B_DH:(h + 1) * B_DH]
                r = lax.rsqrt(jnp.sum(t * t, axis=-1, keepdims=True) + EPS)
                dt = (c * r) * (d - t * (r * r) * jnp.sum(d * t, axis=-1, keepdims=True))
                dy_ref[:, lo:lo + B_DH] = dt * dsilu[:, lo:lo + B_DH]
        dy_ref[:, 2 * W:] = dv_ref[...] * dsilu[:, 2 * W:]

    return pl.pallas_call(
        body, name=name, grid=(S // tm,),
        in_specs=[cur, prev, pl.BlockSpec((CONV_K, C), lambda i: (0, 0))] + [_rows(tm, W)] * 3,
        out_specs=_rows(tm, C), out_shape=jax.ShapeDtypeStruct((S, C), F32),
        scratch_shapes=[pltpu.VMEM((HALO + tm, C), F32)], compiler_params=_par(1),
    )(proj, proj, wconv, dqn, dkn, dv)


def _gdn_pre_bwd_b(proj, wconv, dy, name):
    S = proj.shape[0]
    C = 3 * B_HEADS * B_DH
    tm = _pick(S, 256, 8)
    nt_ = S // tm
    cur, prev = _gdn_pre_specs(tm, C, P_QKVB // C)
    nxt = pl.BlockSpec((HALO, C), lambda i: (jnp.minimum((i + 1) * (tm // HALO), S // HALO - 1), 0))

    def body(x_ref, prev_ref, w_ref, dy_ref, next_ref, dx_ref, dw_ref, xx_ref, dd_ref):
        i = pl.program_id(0)
        _fill_prev_halo(xx_ref, x_ref, prev_ref, i, tm)
        dyv = dy_ref[...]
        dd_ref[pl.ds(0, tm), :] = dyv

        @pl.when(i == nt_ - 1)
        def _():
            dd_ref[pl.ds(tm, HALO), :] = jnp.zeros((HALO, C), F32)

        @pl.when(i != nt_ - 1)
        def _():
            dd_ref[pl.ds(tm, HALO), :] = next_ref[...]

        dx = w_ref[0:1, :] * dd_ref[pl.ds(CONV_K - 1, tm), :]
        for j in range(1, CONV_K):
            dx = dx + w_ref[j:j + 1, :] * dd_ref[pl.ds(CONV_K - 1 - j, tm), :]
        dx_ref[...] = dx.astype(BF16)
        dw = jnp.concatenate(
            [jnp.sum(dyv * xx_ref[pl.ds(HALO - CONV_K + 1 + j, tm), :], axis=0, keepdims=True) for j in range(CONV_K)],
            axis=0)
        _acc(dw_ref, dw, i)

    return pl.pallas_call(
        body, name=name, grid=(nt_,),
        in_specs=[cur, prev, pl.BlockSpec((CONV_K, C), lambda i: (0, 0)), _rows(tm, C), nxt],
        out_specs=[_rows(tm, C), pl.BlockSpec((CONV_K, C), lambda i: (0, 0))],
        out_shape=[jax.ShapeDtypeStruct((S, C), BF16), jax.ShapeDtypeStruct((CONV_K, C), F32)],
        scratch_shapes=[pltpu.VMEM((HALO + tm, C), F32), pltpu.VMEM((tm + HALO, C), F32)],
        compiler_params=_arb(1),
    )(proj, proj, wconv, dy, dy)


def _chunk_masks():
    row = lax.broadcasted_iota(jnp.int32, (CHUNK, CHUNK), 0)
    col = lax.broadcasted_iota(jnp.int32, (CHUNK, CHUNK), 1)
    return row >= col, row > col


def _chunk_local(q, k, vv, bc, gc, gr, tri):
    dm = jnp.where(tri, jnp.exp(jnp.where(tri, gc - gr, 0.0)), 0.0)
    kk = _bnt(k, k)
    glast = gr[:, CHUNK - 1:CHUNK]
    ep = jnp.exp(gc)
    em = jnp.exp(glast - gc)
    el = jnp.exp(glast)
    return dm, kk, ep, em, el, vv * bc, k * (bc * ep)


def _unit_lower_inverse(low):
    row = lax.broadcasted_iota(jnp.int32, (CHUNK, CHUNK), 0)
    col = lax.broadcasted_iota(jnp.int32, (CHUNK, CHUNK), 1)
    p = -low
    t = jnp.where(row == col, 1.0, 0.0).astype(F32) + p
    steps = CHUNK.bit_length() - 2
    for _ in range(steps):
        p = _nn(p, p, SOLVE_PREC)
        t = t + _nn(t, p, SOLVE_PREC)
    return t


def _gdn_chunk_specs(nc, rev):
    idx = (lambda i: nc - 1 - i) if rev else (lambda i: i)
    W = B_HEADS * B_DH
    tok = pl.BlockSpec((CHUNK, W), lambda i: (idx(i), 0))
    colv = pl.BlockSpec((None, CHUNK, B_HEADS), lambda i: (idx(i), 0, 0))
    rowv = pl.BlockSpec((None, B_HEADS, CHUNK), lambda i: (idx(i), 0, 0))
    tmat = pl.BlockSpec((None, B_HEADS, CHUNK, CHUNK), lambda i: (idx(i), 0, 0, 0))
    smat = pl.BlockSpec((None, B_HEADS, B_DH, B_DH), lambda i: (idx(i), 0, 0, 0))
    return tok, colv, rowv, tmat, smat


def _gdn_chunk_fwd(qn, kn, v, bcol, gcol, grow, name, gather=None):
    S, W = qn.shape
    nc = S // CHUNK
    tok, colv, rowv, tmat, smat = _gdn_chunk_specs(nc, False)
    shards, layer = gather if gather is not None else ((), None)
    ng = len(shards)

    def body(q_ref, k_ref, v_ref, bc_ref, gc_ref, gr_ref, *rest):
        srcs, (o_ref, t_ref, sh_ref), gouts = rest[:ng], rest[ng:ng + 3], rest[ng + 3:2 * ng + 3]
        st_ref, sems = rest[2 * ng + 3], rest[2 * ng + 4:]
        i = pl.program_id(0)
        if ng:
            start, forward, finish = _gather_phases(layer, srcs, gouts, *sems)
            pl.when(i == 0)(start)
            pl.when(i == nc // 2)(forward)

        @pl.when(i == 0)
        def _():
            st_ref[...] = jnp.zeros_like(st_ref)

        tri, strict = _chunk_masks()
        bc_all = bc_ref[...]
        gc_all = gc_ref[...]
        for h in range(B_HEADS):
            hs = slice(h * B_DH, (h + 1) * B_DH)
            q, k, vv = q_ref[:, hs], k_ref[:, hs], v_ref[:, hs]
            bc, gc, gr = bc_all[:, h:h + 1], gc_all[:, h:h + 1], gr_ref[h:h + 1, :]
            dm, kk, ep, em, el, vb, kb = _chunk_local(q, k, vv, bc, gc, gr, tri)
            t = _unit_lower_inverse(jnp.where(strict, bc * kk * dm, 0.0))
            u = _nn(t, vb, SOLVE_PREC)
            w = _nn(t, kb, SOLVE_PREC)
            a = _bnt(q, k) * dm
            s0 = st_ref[h]
            ut = u - _bnn(w, s0)
            o_ref[:, hs] = _bnn(q * ep, s0) + _bnn(a, ut)
            st_ref[h] = el * s0 + _btn(k * em, ut)
            t_ref[h] = t
            sh_ref[h] = s0
        if ng:
            pl.when(i == nc - 1)(finish)

    res = pl.pallas_call(
        body, name=name, grid=(nc,), in_specs=[tok, tok, tok, colv, colv, rowv] + [HBM] * ng,
        out_specs=[tok, tmat, smat] + [HBM] * ng,
        out_shape=[jax.ShapeDtypeStruct((S, W), F32), jax.ShapeDtypeStruct((nc, B_HEADS, CHUNK, CHUNK), F32),
                   jax.ShapeDtypeStruct((nc, B_HEADS, B_DH, B_DH), F32)] + _gather_out_shapes(shards),
        scratch_shapes=[pltpu.VMEM((B_HEADS, B_DH, B_DH), F32)] + (_gather_scratch(ng) if ng else []),
        compiler_params=_arb(1),
    )(qn, kn, v, bcol, gcol, grow, *shards)
    return res[0], res[1], res[2], list(res[3:])


def _gdn_chunk_bwd(qn, kn, v, bcol, gcol, grow, tsave, ssave, do, name, exchange=()):
    S, W = qn.shape
    nc = S // CHUNK
    tok, colv, rowv, tmat, smat = _gdn_chunk_specs(nc, True)
    ne = len(exchange)

    def body(q_ref, k_ref, v_ref, bc_ref, gc_ref, gr_ref, t_ref, sh_ref, do_ref, *rest):
        srcs, (dq_ref, dk_ref, dv_ref, dbc_ref, dgc_ref, dgr_ref) = rest[:ne], rest[ne:ne + 6]
        eouts, ds_ref, sems = rest[ne + 6:2 * ne + 6], rest[2 * ne + 6], rest[2 * ne + 7:]
        i = pl.program_id(0)
        if ne:
            start, finish = _chips_phases(srcs, eouts, *sems)
            pl.when(i == 0)(start)

        @pl.when(i == 0)
        def _():
            ds_ref[...] = jnp.zeros_like(ds_ref)

        tri, strict = _chunk_masks()
        bc_all = bc_ref[...]
        gc_all = gc_ref[...]
        lane4 = lax.broadcasted_iota(jnp.int32, (CHUNK, B_HEADS), 1)
        sub4 = lax.broadcasted_iota(jnp.int32, (B_HEADS, CHUNK), 0)
        lane_last = lax.broadcasted_iota(jnp.int32, (1, CHUNK), 1) == CHUNK - 1
        dbc_acc = jnp.zeros((CHUNK, B_HEADS), F32)
        dgc_acc = jnp.zeros((CHUNK, B_HEADS), F32)
        dgr_acc = jnp.zeros((B_HEADS, CHUNK), F32)
        for h in range(B_HEADS):
            hs = slice(h * B_DH, (h + 1) * B_DH)
            q, k, vv = q_ref[:, hs], k_ref[:, hs], v_ref[:, hs]
            bc, gc, gr = bc_all[:, h:h + 1], gc_all[:, h:h + 1], gr_ref[h:h + 1, :]
            dm, kk, ep, em, el, vb, kb = _chunk_local(q, k, vv, bc, gc, gr, tri)
            t = t_ref[h]
            s0 = sh_ref[h]
            w = _nn(t, kb, SOLVE_PREC)
            qk = _bnt(q, k)
            a = qk * dm
            ut = _nn(t, vb, SOLVE_PREC) - _bnn(w, s0)
            qd = q * ep
            kd = k * em
            ds = ds_ref[h]
            dout = do_ref[:, hs]
            dut = _btn(a, dout) + _bnn(kd, ds)
            da = jnp.where(tri, _bnt(dout, ut), 0.0)
            dqd = _bnt(dout, s0)
            dkd = _bnt(ut, ds)
            d_el = jnp.sum(jnp.sum(s0 * ds, axis=1, keepdims=True), axis=0, keepdims=True)
            dw = -_bnt(dut, s0)
            ds_ref[h] = el * ds + _btn(qd, dout) - _btn(w, dut)
            dt = _nt(dut, vb, SOLVE_PREC) + _nt(dw, kb, SOLVE_PREC)
            dvb = _tn(t, dut, SOLVE_PREC)
            dkb = _tn(t, dw, SOLVE_PREC)
            dl = jnp.where(strict, -_tn(t, _nt(dt, t, SOLVE_PREC), SOLVE_PREC), 0.0)
            g1 = dl * dm
            dkb_k = jnp.sum(dkb * k, axis=1, keepdims=True)
            dbeta = jnp.sum(g1 * kk, axis=1, keepdims=True) + jnp.sum(dvb * vv, axis=1, keepdims=True) + dkb_k * ep
            dkk = g1 * bc
            ddm = dl * (bc * kk) + da * qk
            dqk = da * dm
            dq_ref[:, hs] = _bnn(dqk, k) + dqd * ep
            dk_ref[:, hs] = (_btn(dqk, q) + _bnn(dkk, k) + _btn(dkk, k) + dkb * (bc * ep) + dkd * em)
            dv_ref[:, hs] = dvb * bc
            dep = dkb_k * bc + jnp.sum(dqd * q, axis=1, keepdims=True)
            dem = jnp.sum(dkd * k, axis=1, keepdims=True)
            mm = ddm * dm
            dgam_c = jnp.sum(mm, axis=1, keepdims=True) + dep * ep - dem * em
            dglast = jnp.sum(dem * em, axis=0, keepdims=True) + d_el * el
            dgam_r = -jnp.sum(mm, axis=0, keepdims=True) + jnp.where(lane_last, dglast, 0.0)
            dbc_acc = jnp.where(lane4 == h, dbeta, dbc_acc)
            dgc_acc = jnp.where(lane4 == h, dgam_c, dgc_acc)
            dgr_acc = jnp.where(sub4 == h, dgam_r, dgr_acc)
        dbc_ref[...] = dbc_acc
        dgc_ref[...] = dgc_acc
        dgr_ref[...] = dgr_acc
        if ne:
            pl.when(i == nc - 1)(finish)

    res = pl.pallas_call(
        body, name=name, grid=(nc,), in_specs=[tok, tok, tok, colv, colv, rowv, tmat, smat, tok] + [HBM] * ne,
        out_specs=[tok, tok, tok, colv, colv, rowv] + [HBM] * ne,
        out_shape=[jax.ShapeDtypeStruct((S, W), F32)] * 3
        + [jax.ShapeDtypeStruct((nc, CHUNK, B_HEADS), F32)] * 2 + [jax.ShapeDtypeStruct((nc, B_HEADS, CHUNK), F32)]
        + _chips_out_shapes(exchange),
        scratch_shapes=[pltpu.VMEM((B_HEADS, B_DH, B_DH), F32)] + (_chips_scratch(ne) if ne else []),
        compiler_params=_arb(1),
    )(qn, kn, v, bcol, gcol, grow, tsave, ssave, do, *exchange)
    return tuple(res[:6]) + (list(res[6:]),)


def _gdn_post_fwd(o, proj, ng, name):
    S, W = o.shape
    tm = _pick(S, 512, 8)

    def body(o_ref, z_ref, g_ref, y_ref):
        gv = g_ref[...]
        for h in range(B_HEADS):
            hs = slice(h * B_DH, (h + 1) * B_DH)
            oh = o_ref[:, hs]
            z = z_ref[:, hs]
            r = lax.rsqrt(jnp.mean(oh * oh, axis=-1, keepdims=True) + EPS)
            y_ref[:, hs] = (oh * r * gv * (z * _sigmoid(z))).astype(BF16)

    return pl.pallas_call(
        body, name=name, grid=(S // tm,), in_specs=[_rows(tm, W), _rows(tm, W, P_Z // W), _vec(B_DH)],
        out_specs=_rows(tm, W), out_shape=jax.ShapeDtypeStruct((S, W), BF16), compiler_params=_par(1),
    )(o, proj, ng)


def _gdn_post_bwd(dy, o, proj, ng, name):
    S, W = o.shape
    tm = _pick(S, 512, 8)

    def body(dy_ref, o_ref, z_ref, g_ref, do_ref, dz_ref, dg_ref):
        i = pl.program_id(0)
        gv = g_ref[...]
        dg = jnp.zeros((1, B_DH), F32)
        for h in range(B_HEADS):
            hs = slice(h * B_DH, (h + 1) * B_DH)
            oh = o_ref[:, hs]
            z = z_ref[:, hs]
            d = dy_ref[:, hs]
            r = lax.rsqrt(jnp.mean(oh * oh, axis=-1, keepdims=True) + EPS)
            n = oh * r
            sg = _sigmoid(z)
            sz = z * sg
            dn = d * gv * sz
            dg = dg + jnp.sum(d * n * sz, axis=0, keepdims=True)
            dz_ref[:, hs] = (d * n * gv * (sg * (1.0 + z * (1.0 - sg)))).astype(BF16)
            do_ref[:, hs] = r * (dn - n * jnp.mean(dn * n, axis=-1, keepdims=True))
        _acc(dg_ref, dg, i)

    return pl.pallas_call(
        body, name=name, grid=(S // tm,), in_specs=[_rows(tm, W), _rows(tm, W), _rows(tm, W, P_Z // W), _vec(B_DH)],
        out_specs=[_rows(tm, W), _rows(tm, W), _vec(B_DH)],
        out_shape=[jax.ShapeDtypeStruct((S, W), F32), jax.ShapeDtypeStruct((S, W), BF16),
                   jax.ShapeDtypeStruct((1, B_DH), F32)],
        compiler_params=_arb(1),
    )(dy, o, proj, ng)


def _ada_mod(c_all, w_ada, b_shard, name):
    L, D, Ns = w_ada.shape
    B = c_all.shape[0]

    def body(c_ref, w_ref, b_ref, o_ref):
        cv = c_ref[...]
        cond = (cv * _sigmoid(cv)).astype(BF16)
        o_ref[...] = _nn(cond, w_ref[...].astype(BF16)) + b_ref[...]

    return pl.pallas_call(
        body, name=name, grid=(L,),
        in_specs=[pl.BlockSpec((B, D), lambda l: (0, 0)), pl.BlockSpec((None, D, Ns), lambda l: (l, 0, 0)),
                  pl.BlockSpec((None, 1, Ns), lambda l: (l, 0, 0))],
        out_specs=pl.BlockSpec((None, B, Ns), lambda l: (l, 0, 0)),
        out_shape=jax.ShapeDtypeStruct((L, B, Ns), F32), compiler_params=_par(1),
    )(c_all, w_ada, b_shard)


def _ada_wgrad(c_all, dmod, name):
    L, B, Ns = dmod.shape
    D = c_all.shape[1]

    def body(c_ref, d_ref, o_ref):
        cv = c_ref[...]
        cond = (cv * _sigmoid(cv)).astype(BF16)
        o_ref[...] = _tn(cond, d_ref[...].astype(BF16))

    return pl.pallas_call(
        body, name=name, grid=(L,),
        in_specs=[pl.BlockSpec((B, D), lambda l: (0, 0)), pl.BlockSpec((None, B, Ns), lambda l: (l, 0, 0))],
        out_specs=pl.BlockSpec((None, D, Ns), lambda l: (l, 0, 0)),
        out_shape=jax.ShapeDtypeStruct((L, D, Ns), F32), compiler_params=_par(1),
    )(c_all, dmod)


W_IN_PIECES = ((0, 0, 1410), (1, 0, 1410), (2, 0, 252), (2, 772, 638), (3, 0, 1410), (2, 252, 512), (2, 764, 8))


def _reorder_w_in(w4, name):
    L, _, D, Cs = w4.shape
    tm = _pick(D, 256, 16)
    used = sum(p[2] for p in W_IN_PIECES)

    def body(w_ref, o_ref):
        shard = [w_ref[s] for s in range(4)]
        parts = [shard[s][:, lo:lo + n] for s, lo, n in W_IN_PIECES]
        o_ref[...] = jnp.concatenate(parts + [jnp.zeros((tm, P_END - used), w4.dtype)], axis=1)

    return pl.pallas_call(
        body, name=name, grid=(L, D // tm), in_specs=[pl.BlockSpec((None, 4, tm, Cs), lambda l, i: (l, 0, i, 0))],
        out_specs=pl.BlockSpec((None, tm, P_END), lambda l, i: (l, i, 0)),
        out_shape=jax.ShapeDtypeStruct((L, D, P_END), w4.dtype), compiler_params=_par(2),
    )(w4)


def _restore_w_in(g, name):
    D = g.shape[0]
    tm = _pick(D, 256, 8)

    def body(g_ref, o_ref):
        gv = g_ref[...]
        off = 0
        pieces = {}
        for s, lo, n in W_IN_PIECES:
            pieces.setdefault(s, []).append((lo, gv[:, off:off + n]))
            off += n
        for s, lst in pieces.items():
            lst.sort(key=lambda t: t[0])
            o_ref[s] = lst[0][1] if len(lst) == 1 else jnp.concatenate([t[1] for t in lst], axis=1)

    return pl.pallas_call(
        body, name=name, grid=(D // tm,), in_specs=[pl.BlockSpec((tm, P_END), lambda i: (i, 0))],
        out_specs=pl.BlockSpec((4, tm, W_IN_SHARD), lambda i: (0, i, 0)),
        out_shape=jax.ShapeDtypeStruct((4, D, W_IN_SHARD), g.dtype), compiler_params=_par(1),
    )(g)


def _adam_update(w, g, m, v):
    mn = ADAM_B1 * m + (1.0 - ADAM_B1) * g
    vn = ADAM_B2 * v + (1.0 - ADAM_B2) * (g * g)
    m_hat = mn / (1.0 - ADAM_B1 ** ADAM_STEP)
    v_hat = vn / (1.0 - ADAM_B2 ** ADAM_STEP)
    return -ADAM_LR * (m_hat / (jnp.sqrt(v_hat) + ADAM_EPS) + ADAM_WD * w), mn, vn


def _adamw(w, g, m, v, name):
    shape = w.shape
    C = shape[-1]
    R = w.size // C
    tm = _pick(R, 512, 8)
    spec = pl.BlockSpec((tm, C), lambda i: (i, 0))

    def body(w_ref, g_ref, m_ref, v_ref, d_ref, mo_ref, vo_ref):
        d_ref[...], mo_ref[...], vo_ref[...] = _adam_update(w_ref[...], g_ref[...], m_ref[...], v_ref[...])

    outs = pl.pallas_call(
        body, name=name, grid=(R // tm,), in_specs=[spec] * 4, out_specs=[spec] * 3,
        out_shape=[jax.ShapeDtypeStruct((R, C), F32)] * 3, compiler_params=_par(1),
    )(*(t.reshape(R, C) for t in (w, g, m, v)))
    return tuple(o.reshape(shape) for o in outs)


def _adamw_layers(w, gs, m, v, name):
    L, R, C = w.shape
    tm = _pick(R, 128, 8)
    spec = pl.BlockSpec((None, tm, C), lambda l, i: (l, i, 0))
    g_specs = [pl.BlockSpec((tm, C), functools.partial(lambda ll, l, i: (jnp.where(l == ll, i, 0), 0), ll))
               for ll in range(L)]

    def body(w_ref, m_ref, v_ref, *rest):
        g_refs, (go_ref, d_ref, mo_ref, vo_ref) = rest[:L], rest[L:]
        l = pl.program_id(0)
        for ll in range(L):
            @pl.when(l == ll)
            def _():
                g = g_refs[ll][...]
                go_ref[...] = g
                d_ref[...], mo_ref[...], vo_ref[...] = _adam_update(w_ref[...], g, m_ref[...], v_ref[...])

    return pl.pallas_call(
        body, name=name, grid=(L, R // tm), in_specs=[spec] * 3 + g_specs, out_specs=[spec] * 4,
        out_shape=[jax.ShapeDtypeStruct((L, R, C), F32)] * 4, compiler_params=_arb(2),
    )(w, m, v, *gs)


def _pair_sums(a, where, b, name):
    NB, _, R, C = a.shape

    def body(where_ref, a_ref, b_ref, p_ref, own_ref):
        s = a_ref[...] + b_ref[...].astype(F32)
        p_ref[...] = s.astype(BF16)

        @pl.when(pl.program_id(0) == where_ref[1])
        def _():
            own_ref[...] = s

    return pl.pallas_call(
        body, name=name,
        grid_spec=pltpu.PrefetchScalarGridSpec(
            num_scalar_prefetch=1, grid=(NB,),
            in_specs=[pl.BlockSpec((None, None, R, C), lambda k, w: (k, w[0], 0, 0)),
                      pl.BlockSpec((None, R, C), lambda k, w: (k, 0, 0))],
            out_specs=[pl.BlockSpec((None, R, C), lambda k, w: (k, 0, 0)), pl.BlockSpec((R, C), lambda k, w: (0, 0))]),
        out_shape=[jax.ShapeDtypeStruct((NB, R, C), BF16), jax.ShapeDtypeStruct((R, C), F32)],
        compiler_params=_arb(1),
    )(where, a, b)


def _sum_own_and_received(own, recv, where, name):
    R, C = own.shape
    tm = _pick(R, 256, 16)

    def body(where_ref, p_ref, r_ref, o_ref):
        o_ref[...] = ((p_ref[...] + r_ref[0].astype(F32)) + r_ref[1].astype(F32)) + r_ref[2].astype(F32)

    return pl.pallas_call(
        body, name=name,
        grid_spec=pltpu.PrefetchScalarGridSpec(
            num_scalar_prefetch=1, grid=(R // tm,),
            in_specs=[pl.BlockSpec((tm, C), lambda i, w: (i, 0)), pl.BlockSpec((3, tm, C), lambda i, w: (0, i, 0))],
            out_specs=pl.BlockSpec((None, tm, C), lambda i, w: (w[0], i, 0))),
        out_shape=jax.ShapeDtypeStruct((2, R, C), F32), compiler_params=_par(1),
    )(where, own, recv)


def _position():
    return lax.axis_index("x"), lax.axis_index("y"), lax.axis_index("c")


def _other_chips(x, y):
    return [(1 - x, y), (x, 1 - y), (1 - x, 1 - y)]


HBM = pl.BlockSpec(memory_space=pl.ANY)


def _allgather8(blk, name, reduce_rows=None):
    M, N = blk.shape

    def body(x_ref, out_ref, *rest):
        if reduce_rows is None:
            send_sems, recv_sems, local_sem = rest
        else:
            sum_ref, send_sems, recv_sems, local_sem = rest
        x, y, c = _position()
        me, sibling = (x, y, c), (x, y, 1 - c)
        chips = _other_chips(x, y)

        def rows(px, py, pc):
            return out_ref.at[pl.ds((4 * px + 2 * py + pc) * M, M), :]

        def copy(k, block, to, src=None):
            return pltpu.make_async_remote_copy(
                src_ref=rows(*block) if src is None else src, dst_ref=rows(*block),
                send_sem=send_sems.at[k], recv_sem=recv_sems.at[k], device_id=to, device_id_type=MESH)

        mine = pltpu.make_async_copy(x_ref, rows(*me), local_sem)
        mine.start()
        first = [copy(0, me, sibling, src=x_ref)]
        first += [copy(1 + j, me, (*chip, c), src=x_ref) for j, chip in enumerate(chips)]
        for cp in first:
            cp.start()
        passed = [copy(4 + j, (*chip, c), sibling) for j, chip in enumerate(chips)]
        for j, chip in enumerate(chips):
            copy(1 + j, (*chip, c), me).wait_recv()
            passed[j].start()
        copy(0, sibling, me).wait_recv()
        for j, chip in enumerate(chips):
            copy(4 + j, (*chip, 1 - c), me).wait_recv()
        for cp in first + passed:
            cp.wait_send()
        mine.wait()
        if reduce_rows is not None:
            tot = out_ref[pl.ds(0, reduce_rows), :]
            for d in range(1, 8):
                tot = tot + out_ref[pl.ds(d * M, reduce_rows), :]
            sum_ref[...] = tot

    vmem = pl.BlockSpec(memory_space=pltpu.VMEM)
    out_shape = [jax.ShapeDtypeStruct((8 * M, N), blk.dtype)]
    if reduce_rows is not None:
        out_shape.append(jax.ShapeDtypeStruct((reduce_rows, N), blk.dtype))
    res = pl.pallas_call(
        body, name=name, out_shape=out_shape, in_specs=[vmem], out_specs=[vmem] * len(out_shape),
        scratch_shapes=[pltpu.SemaphoreType.DMA((7,)), pltpu.SemaphoreType.DMA((7,)), pltpu.SemaphoreType.DMA],
    )(blk)
    return res[0] if reduce_rows is None else (res[0], res[1])


def _gather_phases(layer, srcs, outs, send_sems, recv_sems, local_sems):
    n = len(srcs)
    x, y, c = _position()
    me, sibling = (x, y, c), (x, y, 1 - c)
    chips = _other_chips(x, y)

    def region(t, px, py, pc):
        return outs[t].at[2 * px + py, pc]

    def copy(t, k, block, to, own=False):
        return pltpu.make_async_remote_copy(
            src_ref=srcs[t].at[layer, c] if own else region(t, *block), dst_ref=region(t, *block),
            send_sem=send_sems.at[7 * t + k], recv_sem=recv_sems.at[7 * t + k], device_id=to, device_id_type=MESH)

    def local(t):
        return pltpu.make_async_copy(srcs[t].at[layer, c], region(t, *me), local_sems.at[t])

    def first(t):
        return [copy(t, 0, me, sibling, own=True)] + [copy(t, 1 + j, me, (*chip, c), own=True)
                                                       for j, chip in enumerate(chips)]

    def start():
        for t in range(n):
            local(t).start()
        for t in range(n):
            for cp in first(t):
                cp.start()

    def forward():
        for j, chip in enumerate(chips):
            for t in range(n):
                copy(t, 1 + j, (*chip, c), me).wait_recv()
                copy(t, 4 + j, (*chip, c), sibling).start()

    def finish():
        for t in range(n):
            copy(t, 0, sibling, me).wait_recv()
        for j, chip in enumerate(chips):
            for t in range(n):
                copy(t, 4 + j, (*chip, 1 - c), me).wait_recv()
        for t in range(n):
            for cp in first(t) + [copy(t, 4 + j, (*chip, c), sibling) for j, chip in enumerate(chips)]:
                cp.wait_send()
            local(t).wait()

    return start, forward, finish


def _gather_scratch(n):
    return [pltpu.SemaphoreType.DMA((7 * n,)), pltpu.SemaphoreType.DMA((7 * n,)), pltpu.SemaphoreType.DMA((n,))]


def _gather_out_shapes(shards):
    return [jax.ShapeDtypeStruct((4,) + s.shape[1:], s.dtype) for s in shards]


def _gather_weights(shards, layer, name):
    n = len(shards)

    def body(*refs):
        start, forward, finish = _gather_phases(layer, refs[:n], refs[n:2 * n], *refs[2 * n:])
        start()
        forward()
        finish()

    return pl.pallas_call(
        body, name=name, out_shape=_gather_out_shapes(shards), in_specs=[HBM] * n, out_specs=[HBM] * n,
        scratch_shapes=_gather_scratch(n),
    )(*shards)


def _rs_sibling(gs, name):
    n = len(gs)

    def body(*refs):
        srcs, outs = refs[:n], refs[n:2 * n]
        send_sems, recv_sems = refs[2 * n:]
        x, y, c = _position()
        copies = [pltpu.make_async_remote_copy(
            src_ref=srcs[t].at[k, 1 - c], dst_ref=outs[t].at[k], send_sem=send_sems.at[4 * t + k],
            recv_sem=recv_sems.at[4 * t + k], device_id=(x, y, 1 - c), device_id_type=MESH)
            for t in range(n) for k in range(4)]
        for cp in copies:
            cp.start()
        for cp in copies:
            cp.wait()

    out_shape = [jax.ShapeDtypeStruct((4,) + g.shape[2:], g.dtype) for g in gs]
    return pl.pallas_call(
        body, name=name, out_shape=out_shape, in_specs=[HBM] * n, out_specs=[HBM] * n,
        scratch_shapes=[pltpu.SemaphoreType.DMA((4 * n,)), pltpu.SemaphoreType.DMA((4 * n,))],
    )(*gs)


def _rs_chips(ps, name):
    n = len(ps)

    def body(*refs):
        start, finish = _chips_phases(refs[:n], refs[n:2 * n], *refs[2 * n:])
        start()
        finish()

    return pl.pallas_call(
        body, name=name, out_shape=_chips_out_shapes(ps), in_specs=[HBM] * n, out_specs=[HBM] * n,
        scratch_shapes=_chips_scratch(n),
    )(*ps)


def _chips_phases(srcs, outs, send_sems, recv_sems):
    x, y, c = _position()
    copies = [pltpu.make_async_remote_copy(
        src_ref=srcs[t].at[2 * px + py], dst_ref=outs[t].at[j], send_sem=send_sems.at[3 * t + j],
        recv_sem=recv_sems.at[3 * t + j], device_id=(px, py, c), device_id_type=MESH)
        for t in range(len(srcs)) for j, (px, py) in enumerate(_other_chips(x, y))]

    def start():
        for cp in copies:
            cp.start()

    def finish():
        for cp in copies:
            cp.wait()

    return start, finish


def _chips_scratch(n):
    return [pltpu.SemaphoreType.DMA((3 * n,)), pltpu.SemaphoreType.DMA((3 * n,))]


def _chips_out_shapes(ps):
    return [jax.ShapeDtypeStruct((3,) + p.shape[1:], p.dtype) for p in ps]


def _rs_pair(hs, name):
    n = len(hs)

    def body(*refs):
        bufs = refs[n:2 * n]
        send_sems, recv_sems = refs[2 * n:]
        x, y, c = _position()

        def copy(t, half):
            return pltpu.make_async_remote_copy(
                src_ref=bufs[t].at[half], dst_ref=bufs[t].at[half], send_sem=send_sems.at[t], recv_sem=recv_sems.at[t],
                device_id=(x, y, 1 - c), device_id_type=MESH)

        for t in range(n):
            copy(t, c).start()
        for t in range(n):
            copy(t, 1 - c).wait_recv()
        for t in range(n):
            copy(t, c).wait_send()

    out_shape = [jax.ShapeDtypeStruct(h.shape, h.dtype) for h in hs]
    return pl.pallas_call(
        body, name=name, out_shape=out_shape, in_specs=[HBM] * n, out_specs=[HBM] * n,
        input_output_aliases={t: t for t in range(n)},
        scratch_shapes=[pltpu.SemaphoreType.DMA((n,)), pltpu.SemaphoreType.DMA((n,))],
    )(*hs)


BIG = ("w_in", "w_branch_a", "w_branch_b", "w_out", "w_ffn_in", "w_ffn_out")


def _band_bias(rel_table, name):
    L, H, n = rel_table.shape
    tab = jnp.pad(rel_table, ((0, 0), (0, 0), (0, NREL_PAD - n))).reshape(L * H, 1, NREL_PAD)
    band = (A_PAST + 1) * CHUNK

    def body(t_ref, o_ref):
        r = lax.broadcasted_iota(jnp.int32, (NREL_PAD, SKEW_W), 0)
        xi = lax.broadcasted_iota(jnp.int32, (NREL_PAD, SKEW_W), 1)
        diag = jnp.where(xi < KSPAN, xi, xi - SKEW_W)
        rel = jnp.clip(A_PAST * CHUNK - diag, -A_MAX_REL, A_MAX_REL) + A_MAX_REL
        e = _nn(t_ref[...], jnp.where(rel == r, 1.0, 0.0).astype(F32), HI)
        x = jnp.broadcast_to(e, (QBLK, SKEW_W))
        row = lax.broadcasted_iota(jnp.int32, (QBLK, SKEW_W), 0)
        for b in range(QBLK.bit_length() - 1):
            x = jnp.where(((row >> b) & 1) == 1, pltpu.roll(x, 1 << b, 1), x)
        x = x[:, :KSPAN]
        first = (lax.broadcasted_iota(jnp.int32, (QBLK, KSPAN), 0) // CHUNK) * CHUNK
        col = lax.broadcasted_iota(jnp.int32, (QBLK, KSPAN), 1)
        o_ref[...] = jnp.where((col >= first) & (col < first + band), x, NEG)

    out = pl.pallas_call(
        body, name=name, grid=(L * H,), in_specs=[pl.BlockSpec((None, 1, NREL_PAD), lambda i: (i, 0, 0))],
        out_specs=pl.BlockSpec((None, QBLK, KSPAN), lambda i: (i, 0, 0)),
        out_shape=jax.ShapeDtypeStruct((L * H, QBLK, KSPAN), F32), compiler_params=_par(1),
    )(tab)
    return out.reshape(L, H, QBLK, KSPAN)


def _col_row_forms(t, S):
    nc = S // CHUNK
    return t.T.reshape(nc, CHUNK, B_HEADS), t.reshape(B_HEADS, nc, CHUNK).transpose(1, 0, 2)


def _layer_fwd(l, x, mod, W, P, big, gather=None):
    S, D = x.shape
    n = lambda s: f"{s}_l{l}"
    sh1, sc1, gt1, sh2, sc2, gt2 = (mod[i:i + 1] for i in range(6))
    h1 = _lnmod_fwd(x, P["norm1_g"][l:l + 1], sc1, sh1, n("ln1"))
    proj = _matmul(h1, W["w_in"], "nn", F32, n("proj"), tn=1152)
    ya = _attn_fwd(proj, big, n("attn"))
    ba = proj[:, P_BA:P_BA + 2 * B_HEADS]
    b_t, a_t = ba[:, :B_HEADS].T, ba[:, B_HEADS:].T
    alog, dtb = P["a_log"][l].reshape(B_HEADS, 1), P["dt_bias"][l].reshape(B_HEADS, 1)
    beta, gam = _gdn_gates_fwd(b_t, a_t, alog, dtb, n("gates"))
    bcol, _ = _col_row_forms(beta, S)
    gcol, grow = _col_row_forms(gam, S)
    qn, kn, v = _gdn_pre_fwd(proj, P["w_conv"][l], n("gdnpre"))
    o, tsave, ssave, gathered = _gdn_chunk_fwd(qn, kn, v, bcol, gcol, grow, n("gdnchunk"), gather)
    yb = _gdn_post_fwd(o, proj, P["gdn_norm_g"][l:l + 1], n("gdnpost"))
    pa = _matmul(ya, W["w_branch_a"], "nn", F32, n("pa"), tm=2048, stacked=True)
    pb = _matmul(yb, W["w_branch_b"], "nn", F32, n("pb"), tm=2048, stacked=True)
    merged = _merge_fwd(proj, pa, pb, n("merge"))
    ao = _matmul(merged, W["w_out"], "nn", F32, n("ao"))
    x1 = _gate_fwd(x, ao, gt1, n("res1"))
    h2 = _lnmod_fwd(x1, P["norm2_g"][l:l + 1], sc2, sh2, n("ln2"))
    gu = _matmul(h2, W["w_ffn_in"], "nn", F32, n("gu"), stacked=True)
    act = _ffn_act_fwd(gu, n("act"))
    fo = _matmul(act, W["w_ffn_out"], "nn", F32, n("fo"), tk=1408)
    x2 = _gate_fwd(x1, fo, gt2, n("res2"))
    saved = dict(x=x, h1=h1, proj=proj, ya=ya, b_t=b_t, a_t=a_t, bcol=bcol, gcol=gcol, grow=grow,
                 qn=qn, kn=kn, v=v, o=o, tsave=tsave, ssave=ssave, yb=yb, pa=pa, pb=pb, merged=merged, ao=ao, x1=x1,
                 h2=h2, gu=gu, act=act, fo=fo)
    return x2, saved, gathered


def _layer_bwd(l, dx2, sv, mod, W, P, big, exchange=()):
    S, D = dx2.shape
    n = lambda s: f"{s}_l{l}"
    sh1, sc1, gt1, sh2, sc2, gt2 = (mod[i:i + 1] for i in range(6))
    g = {}
    view = lambda t: t.reshape((4, 2, t.shape[-2] // (2 if t.ndim == 3 else 8), t.shape[-1]))
    dz2, dgt2 = _gate_bwd(dx2, sv["fo"], gt2, n("dres2"))
    g["w_ffn_out"] = view(_matmul(sv["act"], dz2, "tn", F32, n("dwfo"), tm=1408))
    dact = _matmul(dz2, W["w_ffn_out"], "nt", F32, n("dact"), tn=1408)
    dgu = _ffn_act_bwd(sv["gu"], dact, n("dgu"))
    g["w_ffn_in"] = view(_matmul(sv["h2"], dgu, "tn", F32, n("dwfi"), out_stacked=True))
    dh2 = _matmul(dgu, W["w_ffn_in"], "nt", F32, n("dh2"), stacked=True)
    dx1, dsh2, dsc2, dn2 = _lnmod_bwd(dh2, sv["x1"], P["norm2_g"][l:l + 1], sc2, dx2, n("dln2"))
    dz1, dgt1 = _gate_bwd(dx1, sv["ao"], gt1, n("dres1"))
    g["w_out"] = view(_matmul(sv["merged"], dz1, "tn", F32, n("dwo")))
    dmerged = _matmul(dz1, W["w_out"], "nt", F32, n("dmerged"))
    dgab, dpa, dpb = _merge_bwd(sv["proj"], sv["pa"], sv["pb"], dmerged, n("dmerge"))
    g["w_branch_a"] = view(_matmul(sv["ya"], dpa, "tn", F32, n("dwa"), out_stacked=True))
    g["w_branch_b"] = view(_matmul(sv["yb"], dpb, "tn", F32, n("dwb"), out_stacked=True))
    dya = _matmul(dpa, W["w_branch_a"], "nt", BF16, n("dya"), tm=2048, stacked=True)
    dyb = _matmul(dpb, W["w_branch_b"], "nt", F32, n("dyb"), tm=2048, stacked=True)
    dq, dk, dv, dbig = _attn_bwd(sv["proj"], big, dya, n("dattn"))
    g["rel_table"] = _rel_table_grad(dbig, n("drel"))[:, 0, :2 * A_MAX_REL + 1]
    do, dzb, dng = _gdn_post_bwd(dyb, sv["o"], sv["proj"], P["gdn_norm_g"][l:l + 1], n("dgdnpost"))
    g["gdn_norm_g"] = dng[0]
    dqn, dkn, dvv, dbc, dgc, dgr, received = _gdn_chunk_bwd(
        sv["qn"], sv["kn"], sv["v"], sv["bcol"], sv["gcol"], sv["grow"], sv["tsave"], sv["ssave"], do, n("dgdnchunk"),
        exchange)
    dbeta_t = dbc.reshape(S, B_HEADS).T
    dgam_a = dgc.reshape(S, B_HEADS).T
    dgam_b = dgr.transpose(1, 0, 2).reshape(B_HEADS, S)
    alog, dtb = P["a_log"][l].reshape(B_HEADS, 1), P["dt_bias"][l].reshape(B_HEADS, 1)
    db_t, da_t, dal, ddt = _gdn_gates_bwd(dbeta_t, dgam_a, dgam_b, sv["b_t"], sv["a_t"], alog, dtb, n("dgates"))
    g["a_log"], g["dt_bias"] = dal[:, 0], ddt[:, 0]
    dy = _gdn_pre_bwd_a(sv["proj"], P["w_conv"][l], dqn, dkn, dvv, n("dgdnpre_a"))
    dqkvb, g["w_conv"] = _gdn_pre_bwd_b(sv["proj"], P["w_conv"][l], dy, n("dgdnpre_b"))
    dba = jnp.concatenate([db_t.T, da_t.T, jnp.zeros((S, P_END - P_BA - 2 * B_HEADS), F32)], axis=1)
    dproj = jnp.concatenate([dq.astype(BF16), dk.astype(BF16), dv.astype(BF16), dqkvb, dgab, dzb, dba.astype(BF16)],
                            axis=1)
    g["w_in"] = view(_restore_w_in(_matmul(sv["h1"], dproj, "tn", F32, n("dwin"), tn=1152), n("dwin_cols")))
    dh1 = _matmul(dproj, W["w_in"], "nt", F32, n("dh1"), tk=1152)
    dx, dsh1, dsc1, dn1 = _lnmod_bwd(dh1, sv["x"], P["norm1_g"][l:l + 1], sc1, dx1, n("dln1"))
    g["norm1_g"], g["norm2_g"] = dn1[0], dn2[0]
    dmod = jnp.concatenate([dsh1, dsc1, dgt1, dsh2, dsc2, dgt2], axis=1)[0]
    return dx, g, dmod, received


SMALL = ("norm1_g", "norm2_g", "rel_table", "w_conv", "a_log", "dt_bias", "gdn_norm_g")
SMALL_PACK_C = 1024


def _as_rows(t):
    flat = t.reshape(-1)
    rows = -(-flat.shape[0] // SMALL_PACK_C)
    return jnp.pad(flat, (0, rows * SMALL_PACK_C - flat.shape[0])).reshape(rows, SMALL_PACK_C)


def _pack_rows(parts):
    blk = jnp.concatenate([_as_rows(p) for p in parts], axis=0)
    return jnp.pad(blk, ((0, -blk.shape[0] % 8), (0, 0)))


def _unpack_rows(blk, shapes):
    out, r = [], 0
    for shp in shapes:
        size = int(np.prod(shp))
        rows = -(-size // SMALL_PACK_C)
        out.append(blk[..., r:r + rows, :].reshape(blk.shape[:-2] + (rows * SMALL_PACK_C,))[..., :size]
                   .reshape(blk.shape[:-2] + tuple(shp)))
        r += rows
    return out


def kernel(x, c, w_ada, b_ada, norm1_g, norm2_g, w_in, rel_table, w_conv, a_log, dt_bias, gdn_norm_g, w_branch_a, w_branch_b, w_out, w_ffn_in, w_ffn_out, final_g, loss_target, m_w_ada, m_b_ada, m_norm1_g, m_norm2_g, m_w_in, m_rel_table, m_w_conv, m_a_log, m_dt_bias, m_gdn_norm_g, m_w_branch_a, m_w_branch_b, m_w_out, m_w_ffn_in, m_w_ffn_out, m_final_g, v_w_ada, v_b_ada, v_norm1_g, v_norm2_g, v_w_in, v_rel_table, v_w_conv, v_a_log, v_dt_bias, v_gdn_norm_g, v_w_branch_a, v_w_branch_b, v_w_out, v_w_ffn_in, v_w_ffn_out, v_final_g):
    weights = dict(w_ada=w_ada, b_ada=b_ada, norm1_g=norm1_g, norm2_g=norm2_g, w_in=w_in, rel_table=rel_table,
                   w_conv=w_conv, a_log=a_log, dt_bias=dt_bias, gdn_norm_g=gdn_norm_g, w_branch_a=w_branch_a,
                   w_branch_b=w_branch_b, w_out=w_out, w_ffn_in=w_ffn_in, w_ffn_out=w_ffn_out, final_g=final_g)
    mom_m = dict(w_ada=m_w_ada, b_ada=m_b_ada, norm1_g=m_norm1_g, norm2_g=m_norm2_g, w_in=m_w_in,
                 rel_table=m_rel_table, w_conv=m_w_conv, a_log=m_a_log, dt_bias=m_dt_bias, gdn_norm_g=m_gdn_norm_g,
                 w_branch_a=m_w_branch_a, w_branch_b=m_w_branch_b, w_out=m_w_out, w_ffn_in=m_w_ffn_in,
                 w_ffn_out=m_w_ffn_out, final_g=m_final_g)
    mom_v = dict(w_ada=v_w_ada, b_ada=v_b_ada, norm1_g=v_norm1_g, norm2_g=v_norm2_g, w_in=v_w_in,
                 rel_table=v_rel_table, w_conv=v_w_conv, a_log=v_a_log, dt_bias=v_dt_bias, gdn_norm_g=v_gdn_norm_g,
                 w_branch_a=v_w_branch_a, w_branch_b=v_w_branch_b, w_out=v_w_out, w_ffn_in=v_w_ffn_in,
                 w_ffn_out=v_w_ffn_out, final_g=v_final_g)
    xi, yi, ci = _position()
    chip = 2 * xi + yi
    dev = 2 * chip + ci
    L, D = norm1_g.shape
    NMOD = b_ada.shape[1] // D
    ns = w_ada.shape[2]
    cs = w_conv.shape[2]

    first_blk = _pack_rows([c, w_conv])
    first_all = _allgather8(first_blk, "gather_c").reshape(8, first_blk.shape[0], SMALL_PACK_C)
    c_all, w_conv_all = _unpack_rows(first_all, [(D,), w_conv.shape])
    w_conv_full = w_conv_all.reshape(4, 2, L, CONV_K, cs)[:, 0].transpose(1, 2, 0, 3).reshape(L, CONV_K, 4 * cs)
    b_shard = lax.dynamic_slice_in_dim(b_ada, chip * ns, ns, axis=1).reshape(L, 1, ns)
    mod_shard = _ada_mod(c_all, w_ada, b_shard, "ada_mod")
    mod_all = _allgather8(mod_shard.reshape(L * 8, ns), "gather_mod").reshape(4, 2, L, 8, ns)
    mod = lax.dynamic_index_in_dim(mod_all[:, 0], dev, axis=2, keepdims=False)
    mod = mod.transpose(1, 0, 2).reshape(L, NMOD, D)

    shards = [weights[k].astype(BF16) for k in BIG]
    shards = [s.reshape(s.shape[0], 2, s.shape[1] // 2, s.shape[2]) for s in shards]
    col_stacked = lambda t: t.reshape(4, 2 * t.shape[2], t.shape[3])
    row_joined = lambda t: t.reshape(8 * t.shape[2], t.shape[3])

    def layer_weights(l, gathered):
        gd = dict(zip(BIG, gathered))
        return dict(w_in=_reorder_w_in(col_stacked(gd["w_in"])[None], f"w_in_cols_l{l}")[0],
                    w_branch_a=col_stacked(gd["w_branch_a"]), w_branch_b=col_stacked(gd["w_branch_b"]),
                    w_ffn_in=col_stacked(gd["w_ffn_in"]), w_out=row_joined(gd["w_out"]),
                    w_ffn_out=row_joined(gd["w_ffn_out"]))

    P = dict(norm1_g=norm1_g, norm2_g=norm2_g, w_conv=w_conv_full, a_log=a_log, dt_bias=dt_bias,
             gdn_norm_g=gdn_norm_g)
    big = _band_bias(rel_table, "band_bias")

    W = [layer_weights(0, _gather_weights(shards, 0, "gather_weights_l0"))]
    xc = x[0]
    saved = []
    for l in range(L):
        xc, sv, gathered = _layer_fwd(l, xc, mod[l], W[l], P, big[l], (shards, l + 1) if l + 1 < L else None)
        saved.append(sv)
        if l + 1 < L:
            W.append(layer_weights(l + 1, gathered))
    dx, loss_dev, dfinal = _loss_head(xc, final_g.reshape(1, D), loss_target[0], "loss_head")

    where = jnp.stack([ci, chip]).astype(jnp.int32)
    grads = [None] * L
    dmods = [None] * L
    shard_grads = {k: [None] * L for k in BIG}

    def finish_reduce_scatter(l, sums, from_chips):
        halves = [_sum_own_and_received(s_[1], r_, where, f"rs_sum_{k}_l{l}")
                  for k, s_, r_ in zip(BIG, sums, from_chips)]
        for k, t in zip(BIG, _rs_pair(halves, f"rs_pair_l{l}")):
            shard_grads[k][l] = t.reshape(2 * t.shape[1], t.shape[2])

    pending = None
    for l in reversed(range(L)):
        exchange = [s_[0] for s_ in pending] if pending is not None else ()
        dx, grads[l], dmods[l], received = _layer_bwd(l, dx, saved[l], mod[l], W[l], P, big[l], exchange)
        if pending is not None:
            finish_reduce_scatter(l + 1, pending, received)
        gs = [grads[l][k] for k in BIG]
        from_sibling = _rs_sibling([g_.astype(BF16) for g_ in gs], f"rs_sibling_l{l}")
        pending = [_pair_sums(g_, where, r_, f"rs_pair_sum_{k}_l{l}") for k, g_, r_ in zip(BIG, gs, from_sibling)]
    finish_reduce_scatter(0, pending, _rs_chips([s_[0] for s_ in pending], "rs_chips_l0"))
    dmod = jnp.stack(dmods)

    small = {k: jnp.stack([grads[l][k] for l in range(L)]) for k in SMALL}
    parts = [dmod] + [small[k] for k in SMALL] + [dfinal, loss_dev[0, :1]]
    small_blk = _pack_rows(parts)
    srows = small_blk.shape[0]
    small_all, small_sum = _allgather8(small_blk, "gather_small", reduce_rows=srows)
    shapes = [dmod.shape] + [small[k].shape for k in SMALL] + [(D,), (1,)]
    tot = _unpack_rows(small_sum, shapes)
    G = dict(zip(SMALL, tot[1:1 + len(SMALL)]))
    G["b_ada"] = tot[0].reshape(b_ada.shape)
    G["w_conv"] = lax.dynamic_slice_in_dim(G["w_conv"], chip * cs, cs, axis=2)
    G["final_g"] = tot[-2]
    loss = tot[-1][0]
    dmod_all = _unpack_rows(small_all.reshape(8, srows, SMALL_PACK_C), [dmod.shape])[0]
    dmod_cols = lax.dynamic_slice_in_dim(dmod_all, chip * ns, ns, axis=2).transpose(1, 0, 2)
    G["w_ada"] = _ada_wgrad(c_all, dmod_cols, "ada_wgrad")

    order = ["w_ada", "b_ada", "norm1_g", "norm2_g", "w_in", "rel_table", "w_conv", "a_log", "dt_bias", "gdn_norm_g",
             "w_branch_a", "w_branch_b", "w_out", "w_ffn_in", "w_ffn_out", "final_g"]
    deltas, new_m, new_v = {}, {}, {}
    for k in order:
        w = weights[k]
        if k in BIG:
            G[k], deltas[k], new_m[k], new_v[k] = _adamw_layers(w, shard_grads[k], mom_m[k], mom_v[k], f"adamw_{k}")
            continue
        as2d = (lambda t: t.reshape(1, -1)) if w.ndim == 1 else (lambda t: t)
        d_, m_, v_ = _adamw(as2d(w), as2d(G[k]), as2d(mom_m[k]), as2d(mom_v[k]), f"adamw_{k}")
        deltas[k], new_m[k], new_v[k] = d_.reshape(w.shape), m_.reshape(w.shape), v_.reshape(w.shape)
    return (loss, dx[None], *[G[k] for k in order], *[deltas[k] for k in order], *[new_m[k] for k in order],
            *[new_v[k] for k in order])
```

```python
import functools

import numpy as np
import jax
import jax.numpy as jnp
from jax import lax
from jax.experimental import pallas as pl
from jax.experimental.pallas import tpu as pltpu

F32 = jnp.float32
BF16 = jnp.bfloat16
HI = lax.Precision.HIGHEST
SOLVE_PREC = lax.Precision.HIGH
MESH = pl.DeviceIdType.MESH

EPS = 1e-6
CHUNK = 64
A_HEADS = 8
A_DH = 64
A_PAST = 8
A_MAX_REL = 128
B_HEADS = 4
B_DH = 128
CONV_K = 4
LANE = 128
QBLK = 4 * CHUNK
KSPAN = QBLK + A_PAST * CHUNK
NEG = -1e30

ADAM_LR = 0.001
ADAM_B1 = 0.9
ADAM_B2 = 0.999
ADAM_EPS = 1e-08
ADAM_WD = 0.01
ADAM_STEP = 10

P_QKVA, P_QKVB, P_GA, P_GB, P_Z, P_BA, P_END = 0, 1536, 3072, 4096, 5120, 5632, 5760
W_IN_SHARD = 1410


def _sigmoid(x):
    return 1.0 / (1.0 + jnp.exp(-x))


def _nn(a, b, prec=None):
    return lax.dot_general(a, b, (((1,), (0,)), ((), ())), precision=prec, preferred_element_type=F32)


def _nt(a, b, prec=None):
    return lax.dot_general(a, b, (((1,), (1,)), ((), ())), precision=prec, preferred_element_type=F32)


def _tn(a, b, prec=None):
    return lax.dot_general(a, b, (((0,), (0,)), ((), ())), precision=prec, preferred_element_type=F32)


def _bnn(a, b):
    return _nn(a.astype(BF16), b.astype(BF16))


def _bnt(a, b):
    return _nt(a.astype(BF16), b.astype(BF16))


def _btn(a, b):
    return _tn(a.astype(BF16), b.astype(BF16))


def _pick(n, target, unit=LANE):
    best = None
    for t in range(unit, min(n, target) + 1, unit):
        if n % t == 0:
            best = t
    return best if best is not None else n


def _acc(ref, val, i):
    @pl.when(i == 0)
    def _():
        ref[...] = val

    @pl.when(i != 0)
    def _():
        ref[...] += val


def _arb(n):
    return pltpu.CompilerParams(dimension_semantics=("arbitrary",) * n)


def _par(n):
    return pltpu.CompilerParams(dimension_semantics=("parallel",) * n)


def _matmul(a, b, mode, out_dtype, name, tm=1024, tn=1024, tk=1024, layer=None, stacked=False, out_stacked=False):
    bs = b.shape[1:] if layer is not None else b.shape
    if mode == "nn":
        M, K = a.shape
        N = 4 * bs[2] if stacked else bs[1]
        if stacked:
            tn = bs[2]
    elif mode == "nt":
        M, K = a.shape
        N = bs[1] if stacked else bs[0]
        if stacked:
            tk = bs[2]
    else:
        K, M = a.shape
        N = bs[1]
        if out_stacked:
            tn = N // 4
    tm, tn, tk = _pick(M, tm), _pick(N, tn), _pick(K, tk)
    nk = K // tk
    lead = () if layer is None else (layer,)
    lead_blk = () if layer is None else (None,)
    if mode == "nn":
        a_spec = pl.BlockSpec((tm, tk), lambda i, j, k: (i, k))
        if stacked:
            b_spec = pl.BlockSpec(lead_blk + (None, tk, tn), lambda i, j, k: lead + (j, k, 0))
        else:
            b_spec = pl.BlockSpec(lead_blk + (tk, tn), lambda i, j, k: lead + (k, j))
        dot = _nn
    elif mode == "nt":
        a_spec = pl.BlockSpec((tm, tk), lambda i, j, k: (i, k))
        if stacked:
            b_spec = pl.BlockSpec(lead_blk + (None, tn, tk), lambda i, j, k: lead + (k, j, 0))
        else:
            b_spec = pl.BlockSpec(lead_blk + (tn, tk), lambda i, j, k: lead + (j, k))
        dot = _nt
    else:
        a_spec = pl.BlockSpec((tk, tm), lambda i, j, k: (k, i))
        b_spec = pl.BlockSpec((tk, tn), lambda i, j, k: (k, j))
        dot = _tn
    if out_stacked:
        o_spec = pl.BlockSpec((None, tm, tn), lambda i, j, k: (j, i, 0))
        o_shape = jax.ShapeDtypeStruct((4, M, tn), out_dtype)
    else:
        o_spec = pl.BlockSpec((tm, tn), lambda i, j, k: (i, j))
        o_shape = jax.ShapeDtypeStruct((M, N), out_dtype)

    def body_single(a_ref, b_ref, o_ref):
        o_ref[...] = dot(a_ref[...], b_ref[...]).astype(out_dtype)

    def body(a_ref, b_ref, o_ref, acc_ref):
        k = pl.program_id(2)

        @pl.when(k == 0)
        def _():
            acc_ref[...] = jnp.zeros_like(acc_ref)

        acc_ref[...] += dot(a_ref[...], b_ref[...])

        @pl.when(k == nk - 1)
        def _():
            o_ref[...] = acc_ref[...].astype(out_dtype)

    return pl.pallas_call(
        body_single if nk == 1 else body, name=name, grid=(M // tm, N // tn, nk), in_specs=[a_spec, b_spec],
        out_specs=o_spec, out_shape=o_shape, scratch_shapes=[] if nk == 1 else [pltpu.VMEM((tm, tn), F32)],
        compiler_params=pltpu.CompilerParams(dimension_semantics=("parallel", "parallel", "arbitrary")),
    )(a, b)


def _rows(tm, n, col=0):
    return pl.BlockSpec((tm, n), lambda i: (i, col))


def _vec(n):
    return pl.BlockSpec((1, n), lambda i: (0, 0))


def _lnmod_fwd(x, g, sc, sh, name):
    S, D = x.shape
    tm = _pick(S, 512, 8)

    def body(x_ref, g_ref, sc_ref, sh_ref, o_ref):
        xv = x_ref[...]
        r = lax.rsqrt(jnp.mean(xv * xv, axis=-1, keepdims=True) + EPS)
        o_ref[...] = ((xv * r * g_ref[...]) * (1.0 + sc_ref[...]) + sh_ref[...]).astype(BF16)

    return pl.pallas_call(
        body, name=name, grid=(S // tm,),
        in_specs=[_rows(tm, D), _vec(D), _vec(D), _vec(D)], out_specs=_rows(tm, D),
        out_shape=jax.ShapeDtypeStruct((S, D), BF16), compiler_params=_par(1),
    )(x, g, sc, sh)


def _lnmod_bwd(dh, x, g, sc, dres, name):
    S, D = x.shape
    tm = _pick(S, 512, 8)

    def body(dh_ref, x_ref, g_ref, sc_ref, dres_ref, dx_ref, dsh_ref, dsc_ref, dg_ref):
        i = pl.program_id(0)
        xv = x_ref[...]
        dh_ = dh_ref[...]
        r = lax.rsqrt(jnp.mean(xv * xv, axis=-1, keepdims=True) + EPS)
        xhat = xv * r
        gv = g_ref[...]
        dn = dh_ * (1.0 + sc_ref[...])
        dxhat = dn * gv
        dx_ref[...] = dres_ref[...] + r * (dxhat - xhat * jnp.mean(dxhat * xhat, axis=-1, keepdims=True))
        _acc(dsh_ref, jnp.sum(dh_, axis=0, keepdims=True), i)
        _acc(dsc_ref, jnp.sum(dh_ * (xhat * gv), axis=0, keepdims=True), i)
        _acc(dg_ref, jnp.sum(dn * xhat, axis=0, keepdims=True), i)

    return pl.pallas_call(
        body, name=name, grid=(S // tm,),
        in_specs=[_rows(tm, D), _rows(tm, D), _vec(D), _vec(D), _rows(tm, D)],
        out_specs=[_rows(tm, D), _vec(D), _vec(D), _vec(D)],
        out_shape=[jax.ShapeDtypeStruct((S, D), F32)] + [jax.ShapeDtypeStruct((1, D), F32)] * 3,
        compiler_params=_arb(1),
    )(dh, x, g, sc, dres)


def _gate_fwd(x, y, gt, name):
    S, D = x.shape
    tm = _pick(S, 512, 8)

    def body(x_ref, y_ref, gt_ref, o_ref):
        o_ref[...] = x_ref[...] + gt_ref[...] * y_ref[...]

    return pl.pallas_call(
        body, name=name, grid=(S // tm,), in_specs=[_rows(tm, D), _rows(tm, D), _vec(D)], out_specs=_rows(tm, D),
        out_shape=jax.ShapeDtypeStruct((S, D), F32), compiler_params=_par(1),
    )(x, y, gt)


def _gate_bwd(dx, y, gt, name):
    S, D = dx.shape
    tm = _pick(S, 512, 8)

    def body(dx_ref, y_ref, gt_ref, dz_ref, dgt_ref):
        i = pl.program_id(0)
        d = dx_ref[...]
        dz_ref[...] = (d * gt_ref[...]).astype(BF16)
        _acc(dgt_ref, jnp.sum(d * y_ref[...], axis=0, keepdims=True), i)

    return pl.pallas_call(
        body, name=name, grid=(S // tm,), in_specs=[_rows(tm, D), _rows(tm, D), _vec(D)],
        out_specs=[_rows(tm, D), _vec(D)],
        out_shape=[jax.ShapeDtypeStruct((S, D), BF16), jax.ShapeDtypeStruct((1, D), F32)],
        compiler_params=_arb(1),
    )(dx, y, gt)


def _ffn_act_fwd(gu, name):
    S, H2 = gu.shape
    H = H2 // 2
    tm = _pick(S, 256, 8)

    def body(g_ref, u_ref, o_ref):
        gv = g_ref[...]
        o_ref[...] = (gv * _sigmoid(gv) * u_ref[...]).astype(BF16)

    return pl.pallas_call(
        body, name=name, grid=(S // tm,), in_specs=[_rows(tm, H, 0), _rows(tm, H, 1)], out_specs=_rows(tm, H),
        out_shape=jax.ShapeDtypeStruct((S, H), BF16), compiler_params=_par(1),
    )(gu, gu)


def _ffn_act_bwd(gu, dact, name):
    S, H2 = gu.shape
    H = H2 // 2
    tm = _pick(S, 256, 8)

    def body(g_ref, u_ref, da_ref, o_ref):
        gv = g_ref[...]
        s = _sigmoid(gv)
        da = da_ref[...]
        o_ref[:, :H] = (da * u_ref[...] * (s * (1.0 + gv * (1.0 - s)))).astype(BF16)
        o_ref[:, H:] = (da * (gv * s)).astype(BF16)

    return pl.pallas_call(
        body, name=name, grid=(S // tm,), in_specs=[_rows(tm, H, 0), _rows(tm, H, 1), _rows(tm, H)],
        out_specs=_rows(tm, H2), out_shape=jax.ShapeDtypeStruct((S, H2), BF16), compiler_params=_par(1),
    )(gu, gu, dact)


def _merge_fwd(proj, pa, pb, name):
    S, D = pa.shape
    tm = _pick(S, 512, 8)

    def body(ga_ref, gb_ref, pa_ref, pb_ref, o_ref):
        o_ref[...] = (_sigmoid(ga_ref[...]) * pa_ref[...] + _sigmoid(gb_ref[...]) * pb_ref[...]).astype(BF16)

    return pl.pallas_call(
        body, name=name, grid=(S // tm,),
        in_specs=[_rows(tm, D, P_GA // D), _rows(tm, D, P_GB // D), _rows(tm, D), _rows(tm, D)],
        out_specs=_rows(tm, D), out_shape=jax.ShapeDtypeStruct((S, D), BF16), compiler_params=_par(1),
    )(proj, proj, pa, pb)


def _merge_bwd(proj, pa, pb, dm, name):
    S, D = pa.shape
    tm = _pick(S, 512, 8)

    def body(ga_ref, gb_ref, pa_ref, pb_ref, dm_ref, dg_ref, dpa_ref, dpb_ref):
        d = dm_ref[...]
        sa = _sigmoid(ga_ref[...])
        sb = _sigmoid(gb_ref[...])
        dg_ref[:, :D] = (d * pa_ref[...] * sa * (1.0 - sa)).astype(BF16)
        dg_ref[:, D:] = (d * pb_ref[...] * sb * (1.0 - sb)).astype(BF16)
        dpa_ref[...] = (d * sa).astype(BF16)
        dpb_ref[...] = (d * sb).astype(BF16)

    return pl.pallas_call(
        body, name=name, grid=(S // tm,),
        in_specs=[_rows(tm, D, P_GA // D), _rows(tm, D, P_GB // D), _rows(tm, D), _rows(tm, D), _rows(tm, D)],
        out_specs=[_rows(tm, 2 * D), _rows(tm, D), _rows(tm, D)],
        out_shape=[jax.ShapeDtypeStruct((S, 2 * D), BF16), jax.ShapeDtypeStruct((S, D), BF16),
                   jax.ShapeDtypeStruct((S, D), BF16)],
        compiler_params=_par(1),
    )(proj, proj, pa, pb, dm)


def _loss_head(x, g, target, name):
    S, D = x.shape
    tm = _pick(S, 512, 8)

    def body(x_ref, g_ref, t_ref, dx_ref, loss_ref, dg_ref):
        i = pl.program_id(0)
        xv = x_ref[...]
        gv = g_ref[...]
        r = lax.rsqrt(jnp.mean(xv * xv, axis=-1, keepdims=True) + EPS)
        xhat = xv * r
        err = xhat * gv - t_ref[...]
        part = 0.5 * jnp.sum(jnp.mean(err * err, axis=-1, keepdims=True), axis=0, keepdims=True)
        _acc(loss_ref, jnp.broadcast_to(part, (1, LANE)), i)
        dy = err * (1.0 / D)
        _acc(dg_ref, jnp.sum(dy * xhat, axis=0, keepdims=True), i)
        dxhat = dy * gv
        dx_ref[...] = r * (dxhat - xhat * jnp.mean(dxhat * xhat, axis=-1, keepdims=True))

    return pl.pallas_call(
        body, name=name, grid=(S // tm,), in_specs=[_rows(tm, D), _vec(D), _rows(tm, D)],
        out_specs=[_rows(tm, D), _vec(LANE), _vec(D)],
        out_shape=[jax.ShapeDtypeStruct((S, D), F32), jax.ShapeDtypeStruct((1, LANE), F32),
                   jax.ShapeDtypeStruct((1, D), F32)],
        compiler_params=_arb(1),
    )(x, g, target)


HEADS_PER_SLAB = LANE // A_DH
N_SLABS = A_HEADS // HEADS_PER_SLAB
SPAN_BLOCKS = KSPAN // QBLK


def _attn_specs(seg):
    q_spec = pl.BlockSpec((QBLK, LANE), lambda p, m: (m, seg[0] * N_SLABS + p))
    k_specs = [pl.BlockSpec((QBLK, LANE), functools.partial(
        lambda j, p, m: (jnp.maximum(m - (SPAN_BLOCKS - 1) + j, 0), seg[1] * N_SLABS + p), j)) for j in range(SPAN_BLOCKS)]
    v_specs = [pl.BlockSpec((QBLK, LANE), functools.partial(
        lambda j, p, m: (jnp.maximum(m - (SPAN_BLOCKS - 1) + j, 0), seg[2] * N_SLABS + p), j)) for j in range(SPAN_BLOCKS)]
    b_spec = pl.BlockSpec((HEADS_PER_SLAB, QBLK, KSPAN), lambda p, m: (p, 0, 0))
    return q_spec, k_specs, v_specs, b_spec


def _head_lanes(t, hh):
    lane = lax.broadcasted_iota(jnp.int32, t.shape, 1)
    return jnp.where((lane // A_DH) == hh, t, jnp.zeros_like(t))


def _attn_probs(qh, k, bias, m):
    s = _nt(qh, k) * (A_DH ** -0.5) + bias
    key_pos = lax.broadcasted_iota(jnp.int32, (QBLK, KSPAN), 1) + (m - (SPAN_BLOCKS - 1)) * QBLK
    s = jnp.where(key_pos >= 0, s, NEG)
    p = jnp.exp(s - jnp.max(s, axis=-1, keepdims=True))
    return p / jnp.sum(p, axis=-1, keepdims=True)


def _grid_ends(nq):
    p, m = pl.program_id(0), pl.program_id(1)
    return (p == 0) & (m == 0), (p == N_SLABS // 2) & (m == 0), (p == N_SLABS - 1) & (m == nq - 1)


def _attn_fwd(proj, big, name, gather=None):
    S = proj.shape[0]
    q_spec, k_specs, v_specs, b_spec = _attn_specs((0, 1, 2))
    shards, layer = gather if gather is not None else ((), None)
    ng = len(shards)

    def body(q_ref, k0, k1, k2, v0, v1, v2, b_ref, *rest):
        srcs, o_ref, gouts, sems = rest[:ng], rest[ng], rest[ng + 1:2 * ng + 1], rest[2 * ng + 1:]
        done = _carry(_gather_phases(layer, srcs, gouts, *sems), *_grid_ends(S // QBLK)) if ng else None
        m = pl.program_id(1)
        q = q_ref[...].astype(BF16)
        k = jnp.concatenate([k0[...], k1[...], k2[...]], axis=0).astype(BF16)
        v = jnp.concatenate([v0[...], v1[...], v2[...]], axis=0).astype(BF16)
        outs = [_nn(_attn_probs(_head_lanes(q, hh), k, b_ref[hh], m).astype(BF16), v) for hh in range(HEADS_PER_SLAB)]
        lane = lax.broadcasted_iota(jnp.int32, (QBLK, LANE), 1)
        o_ref[...] = jnp.where(lane < A_DH, outs[0], outs[1]).astype(BF16)
        if ng:
            done()

    res = pl.pallas_call(
        body, name=name, grid=(N_SLABS, S // QBLK), in_specs=[q_spec] + k_specs + v_specs + [b_spec] + [HBM] * ng,
        out_specs=[pl.BlockSpec((QBLK, LANE), lambda p, m: (m, p))] + [HBM] * ng,
        out_shape=[jax.ShapeDtypeStruct((S, A_HEADS * A_DH), BF16)] + _gather_out_shapes(shards),
        scratch_shapes=_gather_scratch(ng) if ng else [], compiler_params=_arb(2),
    )(proj, proj, proj, proj, proj, proj, proj, big, *shards)
    return res[0], list(res[1:])


def _attn_bwd(proj, big, dya, name, exchange=()):
    S = proj.shape[0]
    W = A_HEADS * A_DH
    q_spec, k_specs, v_specs, b_spec = _attn_specs((0, 1, 2))
    out_q = pl.BlockSpec((QBLK, LANE), lambda p, m: (m, p))
    out_kv = pl.BlockSpec((S, LANE), lambda p, m: (0, p))
    ne = len(exchange)

    def body(q_ref, k0, k1, k2, v0, v1, v2, b_ref, do_ref, *rest):
        srcs, (dq_ref, dk_ref, dv_ref, db_ref), eouts, sems = rest[:ne], rest[ne:ne + 4], rest[ne + 4:2 * ne + 4], rest[2 * ne + 4:]
        done = _carry(_chips_phases(srcs, eouts, *sems), *_grid_ends(S // QBLK)) if ne else None
        m = pl.program_id(1)

        @pl.when(m == 0)
        def _():
            dk_ref[...] = jnp.zeros_like(dk_ref)
            dv_ref[...] = jnp.zeros_like(dv_ref)
            db_ref[...] = jnp.zeros_like(db_ref)

        q = q_ref[...].astype(BF16)
        k = jnp.concatenate([k0[...], k1[...], k2[...]], axis=0).astype(BF16)
        v = jnp.concatenate([v0[...], v1[...], v2[...]], axis=0).astype(BF16)
        do = do_ref[...]
        dqs = []
        dk = jnp.zeros((KSPAN, LANE), F32)
        dv = jnp.zeros((KSPAN, LANE), F32)
        for hh in range(HEADS_PER_SLAB):
            qh = _head_lanes(q, hh)
            doh = _head_lanes(do, hh)
            p = _attn_probs(qh, k, b_ref[hh], m)
            dp = _nt(doh, v)
            ds = p * (dp - jnp.sum(p * dp, axis=-1, keepdims=True))
            db_ref[hh] += ds
            dsb = (ds * (A_DH ** -0.5)).astype(BF16)
            dqs.append(_nn(dsb, k))
            dk = dk + _tn(dsb, qh)
            dv = dv + _tn(p.astype(BF16), doh)
        lane = lax.broadcasted_iota(jnp.int32, (QBLK, LANE), 1)
        dq_ref[...] = jnp.where(lane < A_DH, dqs[0], dqs[1])
        for j in range(SPAN_BLOCKS):
            blk = m - (SPAN_BLOCKS - 1) + j

            @pl.when(blk >= 0)
            def _():
                off = pl.multiple_of(blk * QBLK, QBLK)
                dk_ref[pl.ds(off, QBLK), :] += dk[j * QBLK:(j + 1) * QBLK]
                dv_ref[pl.ds(off, QBLK), :] += dv[j * QBLK:(j + 1) * QBLK]
        if ne:
            done()

    res = pl.pallas_call(
        body, name=name, grid=(N_SLABS, S // QBLK),
        in_specs=[q_spec] + k_specs + v_specs + [b_spec, pl.BlockSpec((QBLK, LANE), lambda p, m: (m, p))] + [HBM] * ne,
        out_specs=[out_q, out_kv, out_kv, b_spec] + [HBM] * ne,
        out_shape=[jax.ShapeDtypeStruct((S, W), F32)] * 3 + [jax.ShapeDtypeStruct((A_HEADS, QBLK, KSPAN), F32)]
        + _chips_out_shapes(exchange),
        scratch_shapes=_chips_scratch(ne) if ne else [], compiler_params=_arb(2),
    )(proj, proj, proj, proj, proj, proj, proj, big, dya, *exchange)
    return tuple(res[:4]) + (list(res[4:]),)


NREL_PAD = 3 * LANE
SKEW_W = 1024


def _rel_table_grad(dbig, name):
    H, R, C = dbig.shape

    def body(d_ref, o_ref):
        x = jnp.concatenate([d_ref[...], jnp.zeros((R, SKEW_W - C), F32)], axis=1)
        row = lax.broadcasted_iota(jnp.int32, (R, SKEW_W), 0)
        for b in range(R.bit_length() - 1):
            x = jnp.where(((row >> b) & 1) == 1, pltpu.roll(x, SKEW_W - (1 << b), 1), x)
        e = jnp.sum(x, axis=0, keepdims=True)
        xi = lax.broadcasted_iota(jnp.int32, (SKEW_W, NREL_PAD), 0)
        r = lax.broadcasted_iota(jnp.int32, (SKEW_W, NREL_PAD), 1)
        diag = jnp.where(xi < C, xi, xi - SKEW_W)
        rel = jnp.clip(A_PAST * CHUNK - diag, -A_MAX_REL, A_MAX_REL) + A_MAX_REL
        o_ref[...] = _nn(e, jnp.where(rel == r, 1.0, 0.0).astype(F32), HI)

    return pl.pallas_call(
        body, name=name, grid=(H,), in_specs=[pl.BlockSpec((None, R, C), lambda h: (h, 0, 0))],
        out_specs=pl.BlockSpec((None, 1, NREL_PAD), lambda h: (h, 0, 0)),
        out_shape=jax.ShapeDtypeStruct((H, 1, NREL_PAD), F32), compiler_params=_par(1),
    )(dbig)


def _chunk_cumsum_matrix(n, reverse):
    j = lax.broadcasted_iota(jnp.int32, (n, n), 0)
    i = lax.broadcasted_iota(jnp.int32, (n, n), 1)
    same = (j // CHUNK) == (i // CHUNK)
    return jnp.where(same & ((j >= i) if reverse else (j <= i)), 1.0, 0.0).astype(F32)


def _gdn_gates_fwd(b_t, a_t, alog, dtb, name):
    Hh, S = b_t.shape
    tl = _pick(S, 512)
    row = pl.BlockSpec((Hh, tl), lambda i: (0, i))
    col = pl.BlockSpec((Hh, 1), lambda i: (0, 0))

    def body(b_ref, a_ref, al_ref, dt_ref, beta_ref, gam_ref):
        z = a_ref[...] + dt_ref[...]
        sp = jnp.maximum(z, 0.0) + jnp.log(1.0 + jnp.exp(-jnp.abs(z)))
        g = -jnp.exp(al_ref[...]) * sp
        beta_ref[...] = _sigmoid(b_ref[...])
        gam_ref[...] = _nn(g, _chunk_cumsum_matrix(tl, False), HI)

    return pl.pallas_call(
        body, name=name, grid=(S // tl,), in_specs=[row, row, col, col], out_specs=[row, row],
        out_shape=[jax.ShapeDtypeStruct((Hh, S), F32)] * 2, compiler_params=_par(1),
    )(b_t, a_t, alog, dtb)


def _gdn_gates_bwd(dbeta, dgam_a, dgam_b, b_t, a_t, alog, dtb, name):
    Hh, S = b_t.shape
    tl = _pick(S, 512)
    row = pl.BlockSpec((Hh, tl), lambda i: (0, i))
    col = pl.BlockSpec((Hh, 1), lambda i: (0, 0))
    accs = pl.BlockSpec((Hh, LANE), lambda i: (0, 0))

    def body(dbeta_ref, dga_ref, dgb_ref, b_ref, a_ref, al_ref, dt_ref, db_ref, da_ref, dal_ref, ddt_ref):
        i = pl.program_id(0)
        z = a_ref[...] + dt_ref[...]
        sp = jnp.maximum(z, 0.0) + jnp.log(1.0 + jnp.exp(-jnp.abs(z)))
        ea = jnp.exp(al_ref[...])
        dg = _nn(dga_ref[...] + dgb_ref[...], _chunk_cumsum_matrix(tl, True), HI)
        da = dg * (-ea) * _sigmoid(z)
        beta = _sigmoid(b_ref[...])
        db_ref[...] = dbeta_ref[...] * beta * (1.0 - beta)
        da_ref[...] = da
        _acc(dal_ref, jnp.broadcast_to(jnp.sum(dg * (-ea * sp), axis=1, keepdims=True), (Hh, LANE)), i)
        _acc(ddt_ref, jnp.broadcast_to(jnp.sum(da, axis=1, keepdims=True), (Hh, LANE)), i)

    return pl.pallas_call(
        body, name=name, grid=(S // tl,), in_specs=[row] * 5 + [col, col], out_specs=[row, row, accs, accs],
        out_shape=[jax.ShapeDtypeStruct((Hh, S), F32)] * 2 + [jax.ShapeDtypeStruct((Hh, LANE), F32)] * 2,
        compiler_params=_arb(1),
    )(dbeta, dgam_a, dgam_b, b_t, a_t, alog, dtb)


HALO = 8


def _conv_silu(xx_ref, w_ref, tm):
    y = w_ref[0:1, :] * xx_ref[pl.ds(HALO - CONV_K + 1, tm), :]
    for j in range(1, CONV_K):
        y = y + w_ref[j:j + 1, :] * xx_ref[pl.ds(HALO - CONV_K + 1 + j, tm), :]
    return y, y * _sigmoid(y)


def _fill_prev_halo(xx_ref, x_ref, prev_ref, i, tm):
    xx_ref[pl.ds(HALO, tm), :] = x_ref[...]

    @pl.when(i == 0)
    def _():
        xx_ref[pl.ds(0, HALO), :] = jnp.zeros((HALO, xx_ref.shape[1]), F32)

    @pl.when(i != 0)
    def _():
        xx_ref[pl.ds(0, HALO), :] = prev_ref[...]


def _gdn_pre_specs(tm, C, colblk):
    cur = pl.BlockSpec((tm, C), lambda i: (i, colblk))
    prev = pl.BlockSpec((HALO, C), lambda i: (jnp.maximum(i * (tm // HALO) - 1, 0), colblk))
    return cur, prev


def _gdn_pre_fwd(proj, wconv, name):
    S = proj.shape[0]
    C = 3 * B_HEADS * B_DH
    W = B_HEADS * B_DH
    tm = _pick(S, 256, 8)
    cur, prev = _gdn_pre_specs(tm, C, P_QKVB // C)

    def body(x_ref, prev_ref, w_ref, q_ref, k_ref, v_ref, xx_ref):
        i = pl.program_id(0)
        _fill_prev_halo(xx_ref, x_ref, prev_ref, i, tm)
        _, sl = _conv_silu(xx_ref, w_ref, tm)
        for h in range(B_HEADS):
            hs = slice(h * B_DH, (h + 1) * B_DH)
            q = sl[:, h * B_DH:(h + 1) * B_DH]
            k = sl[:, W + h * B_DH:W + (h + 1) * B_DH]
            q_ref[:, hs] = q * (lax.rsqrt(jnp.sum(q * q, axis=-1, keepdims=True) + EPS) * (B_DH ** -0.5))
            k_ref[:, hs] = k * lax.rsqrt(jnp.sum(k * k, axis=-1, keepdims=True) + EPS)
        v_ref[...] = sl[:, 2 * W:]

    return pl.pallas_call(
        body, name=name, grid=(S // tm,), in_specs=[cur, prev, pl.BlockSpec((CONV_K, C), lambda i: (0, 0))],
        out_specs=[_rows(tm, W)] * 3, out_shape=[jax.ShapeDtypeStruct((S, W), F32)] * 3,
        scratch_shapes=[pltpu.VMEM((HALO + tm, C), F32)], compiler_params=_par(1),
    )(proj, proj, wconv)


def _gdn_pre_bwd_a(proj, wconv, dqn, dkn, dv, name):
    S = proj.shape[0]
    C = 3 * B_HEADS * B_DH
    W = B_HEADS * B_DH
    tm = _pick(S, 256, 8)
    cur, prev = _gdn_pre_specs(tm, C, P_QKVB // C)

    def body(x_ref, prev_ref, w_ref, dq_ref, dk_ref, dv_ref, dy_ref, xx_ref):
        i = pl.program_id(0)
        _fill_prev_halo(xx_ref, x_ref, prev_ref, i, tm)
        y, sl = _conv_silu(xx_ref, w_ref, tm)
        sg = _sigmoid(y)
        dsilu = sg * (1.0 + y * (1.0 - sg))
        for h in range(B_HEADS):
            for base, d_ref, c in ((0, dq_ref, B_DH ** -0.5), (W, dk_ref, 1.0)):
                lo = base + h * B_DH
                t = sl[:, lo:lo + B_DH]
                d = d_ref[:, h * B_DH:(h + 1) * B_DH]
                r = lax.rsqrt(jnp.sum(t * t, axis=-1, keepdims=True) + EPS)
                dt = (c * r) * (d - t * (r * r) * jnp.sum(d * t, axis=-1, keepdims=True))
                dy_ref[:, lo:lo + B_DH] = dt * dsilu[:, lo:lo + B_DH]
        dy_ref[:, 2 * W:] = dv_ref[...] * dsilu[:, 2 * W:]

    return pl.pallas_call(
        body, name=name, grid=(S // tm,),
        in_specs=[cur, prev, pl.BlockSpec((CONV_K, C), lambda i: (0, 0))] + [_rows(tm, W)] * 3,
        out_specs=_rows(tm, C), out_shape=jax.ShapeDtypeStruct((S, C), F32),
        scratch_shapes=[pltpu.VMEM((HALO + tm, C), F32)], compiler_params=_par(1),
    )(proj, proj, wconv, dqn, dkn, dv)


def _gdn_pre_bwd_b(proj, wconv, dy, name):
    S = proj.shape[0]
    C = 3 * B_HEADS * B_DH
    tm = _pick(S, 256, 8)
    nt_ = S // tm
    cur, prev = _gdn_pre_specs(tm, C, P_QKVB // C)
    nxt = pl.BlockSpec((HALO, C), lambda i: (jnp.minimum((i + 1) * (tm // HALO), S // HALO - 1), 0))

    def body(x_ref, prev_ref, w_ref, dy_ref, next_ref, dx_ref, dw_ref, xx_ref, dd_ref):
        i = pl.program_id(0)
        _fill_prev_halo(xx_ref, x_ref, prev_ref, i, tm)
        dyv = dy_ref[...]
        dd_ref[pl.ds(0, tm), :] = dyv

        @pl.when(i == nt_ - 1)
        def _():
            dd_ref[pl.ds(tm, HALO), :] = jnp.zeros((HALO, C), F32)

        @pl.when(i != nt_ - 1)
        def _():
            dd_ref[pl.ds(tm, HALO), :] = next_ref[...]

        dx = w_ref[0:1, :] * dd_ref[pl.ds(CONV_K - 1, tm), :]
        for j in range(1, CONV_K):
            dx = dx + w_ref[j:j + 1, :] * dd_ref[pl.ds(CONV_K - 1 - j, tm), :]
        dx_ref[...] = dx.astype(BF16)
        dw = jnp.concatenate(
            [jnp.sum(dyv * xx_ref[pl.ds(HALO - CONV_K + 1 + j, tm), :], axis=0, keepdims=True) for j in range(CONV_K)],
            axis=0)
        _acc(dw_ref, dw, i)

    return pl.pallas_call(
        body, name=name, grid=(nt_,),
        in_specs=[cur, prev, pl.BlockSpec((CONV_K, C), lambda i: (0, 0)), _rows(tm, C), nxt],
        out_specs=[_rows(tm, C), pl.BlockSpec((CONV_K, C), lambda i: (0, 0))],
        out_shape=[jax.ShapeDtypeStruct((S, C), BF16), jax.ShapeDtypeStruct((CONV_K, C), F32)],
        scratch_shapes=[pltpu.VMEM((HALO + tm, C), F32), pltpu.VMEM((tm + HALO, C), F32)],
        compiler_params=_arb(1),
    )(proj, proj, wconv, dy, dy)


def _chunk_masks():
    row = lax.broadcasted_iota(jnp.int32, (CHUNK, CHUNK), 0)
    col = lax.broadcasted_iota(jnp.int32, (CHUNK, CHUNK), 1)
    return row >= col, row > col


def _chunk_local(q, k, vv, bc, gc, gr, tri):
    dm = jnp.where(tri, jnp.exp(jnp.where(tri, gc - gr, 0.0)), 0.0)
    kk = _bnt(k, k)
    glast = gr[:, CHUNK - 1:CHUNK]
    ep = jnp.exp(gc)
    em = jnp.exp(glast - gc)
    el = jnp.exp(glast)
    return dm, kk, ep, em, el, vv * bc, k * (bc * ep)


def _unit_lower_inverse(low):
    row = lax.broadcasted_iota(jnp.int32, (CHUNK, CHUNK), 0)
    col = lax.broadcasted_iota(jnp.int32, (CHUNK, CHUNK), 1)
    p = -low
    t = jnp.where(row == col, 1.0, 0.0).astype(F32) + p
    steps = CHUNK.bit_length() - 2
    for _ in range(steps):
        p = _nn(p, p, SOLVE_PREC)
        t = t + _nn(t, p, SOLVE_PREC)
    return t


GROUP = 4


def _carry(phases, first, middle, last):
    if len(phases) == 3:
        pl.when(first)(phases[0])
        pl.when(middle)(phases[1])
        return lambda: pl.when(last)(phases[2])
    pl.when(first)(phases[0])
    return lambda: pl.when(last)(phases[1])


def _gdn_group_specs(ng_steps, W):
    tok = pl.BlockSpec((GROUP * CHUNK, W), lambda i: (i, 0))
    colv = pl.BlockSpec((GROUP, CHUNK, B_HEADS), lambda i: (i, 0, 0))
    rowv = pl.BlockSpec((GROUP, B_HEADS, CHUNK), lambda i: (i, 0, 0))
    mat = pl.BlockSpec((GROUP, B_HEADS, CHUNK, CHUNK), lambda i: (i, 0, 0, 0))
    return tok, colv, rowv, mat


def _gdn_local_fwd(qn, kn, v, bcol, gcol, grow, name, gather=None):
    S, Wd = qn.shape
    nc = S // CHUNK
    steps = nc // GROUP
    tok, colv, rowv, mat = _gdn_group_specs(steps, Wd)
    shards, layer = gather if gather is not None else ((), None)
    ng = len(shards)

    def body(q_ref, k_ref, v_ref, bc_ref, gc_ref, gr_ref, *rest):
        srcs, (t_ref, a_ref, u_ref, w_ref), gouts, sems = rest[:ng], rest[ng:ng + 4], rest[ng + 4:2 * ng + 4], rest[2 * ng + 4:]
        i = pl.program_id(0)
        done = _carry(_gather_phases(layer, srcs, gouts, *sems), i == 0, i == steps // 2, i == steps - 1) if ng else None
        tri, strict = _chunk_masks()
        for c in range(GROUP):
            rs = slice(c * CHUNK, (c + 1) * CHUNK)
            bc_all, gc_all = bc_ref[c], gc_ref[c]
            for h in range(B_HEADS):
                hs = slice(h * B_DH, (h + 1) * B_DH)
                q, k, vv = q_ref[rs, hs], k_ref[rs, hs], v_ref[rs, hs]
                bc, gc, gr = bc_all[:, h:h + 1], gc_all[:, h:h + 1], gr_ref[c, h:h + 1, :]
                dm, kk, ep, em, el, vb, kb = _chunk_local(q, k, vv, bc, gc, gr, tri)
                t = _unit_lower_inverse(jnp.where(strict, bc * kk * dm, 0.0))
                t_ref[c, h] = t
                a_ref[c, h] = _bnt(q, k) * dm
                u_ref[rs, hs] = _nn(t, vb, SOLVE_PREC)
                w_ref[rs, hs] = _nn(t, kb, SOLVE_PREC)
        if ng:
            done()

    res = pl.pallas_call(
        body, name=name, grid=(steps,), in_specs=[tok, tok, tok, colv, colv, rowv] + [HBM] * ng,
        out_specs=[mat, mat, tok, tok] + [HBM] * ng,
        out_shape=[jax.ShapeDtypeStruct((nc, B_HEADS, CHUNK, CHUNK), F32)] * 2 + [jax.ShapeDtypeStruct((S, Wd), F32)] * 2
        + _gather_out_shapes(shards),
        scratch_shapes=_gather_scratch(ng) if ng else [], compiler_params=_arb(1),
    )(qn, kn, v, bcol, gcol, grow, *shards)
    return res[0], res[1], res[2], res[3], list(res[4:])


def _scan_decays(gc, gr):
    glast = gr[:, CHUNK - 1:CHUNK]
    return jnp.exp(gc), jnp.exp(glast - gc), jnp.exp(glast)


def _gdn_scan_specs(nc, rev):
    idx = (lambda i: nc - 1 - i) if rev else (lambda i: i)
    W = B_HEADS * B_DH
    tok = pl.BlockSpec((CHUNK, W), lambda i: (idx(i), 0))
    colv = pl.BlockSpec((None, CHUNK, B_HEADS), lambda i: (idx(i), 0, 0))
    rowv = pl.BlockSpec((None, B_HEADS, CHUNK), lambda i: (idx(i), 0, 0))
    mat = pl.BlockSpec((None, B_HEADS, CHUNK, CHUNK), lambda i: (idx(i), 0, 0, 0))
    smat = pl.BlockSpec((None, B_HEADS, B_DH, B_DH), lambda i: (idx(i), 0, 0, 0))
    return tok, colv, rowv, mat, smat


def _gdn_scan_fwd(qn, kn, u, w, a, gcol, grow, name, gather=None):
    S, Wd = qn.shape
    nc = S // CHUNK
    tok, colv, rowv, mat, smat = _gdn_scan_specs(nc, False)
    shards, layer = gather if gather is not None else ((), None)
    ng = len(shards)

    def body(q_ref, k_ref, u_ref, w_ref, a_ref, gc_ref, gr_ref, *rest):
        srcs, (o_ref, sh_ref), gouts = rest[:ng], rest[ng:ng + 2], rest[ng + 2:2 * ng + 2]
        st_ref, sems = rest[2 * ng + 2], rest[2 * ng + 3:]
        i = pl.program_id(0)
        done = _carry(_gather_phases(layer, srcs, gouts, *sems), i == 0, i == nc // 2, i == nc - 1) if ng else None

        @pl.when(i == 0)
        def _():
            st_ref[...] = jnp.zeros_like(st_ref)

        gc_all = gc_ref[...]
        for h in range(B_HEADS):
            hs = slice(h * B_DH, (h + 1) * B_DH)
            ep, em, el = _scan_decays(gc_all[:, h:h + 1], gr_ref[h:h + 1, :])
            s0 = st_ref[h]
            ut = u_ref[:, hs] - _bnn(w_ref[:, hs], s0)
            o_ref[:, hs] = _bnn(q_ref[:, hs] * ep, s0) + _bnn(a_ref[h], ut)
            st_ref[h] = el * s0 + _btn(k_ref[:, hs] * em, ut)
            sh_ref[h] = s0
        if ng:
            done()

    res = pl.pallas_call(
        body, name=name, grid=(nc,), in_specs=[tok, tok, tok, tok, mat, colv, rowv] + [HBM] * ng,
        out_specs=[tok, smat] + [HBM] * ng,
        out_shape=[jax.ShapeDtypeStruct((S, Wd), F32), jax.ShapeDtypeStruct((nc, B_HEADS, B_DH, B_DH), F32)]
        + _gather_out_shapes(shards),
        scratch_shapes=[pltpu.VMEM((B_HEADS, B_DH, B_DH), F32)] + (_gather_scratch(ng) if ng else []),
        compiler_params=_arb(1),
    )(qn, kn, u, w, a, gcol, grow, *shards)
    return res[0], res[1], list(res[2:])


def _gdn_scan_bwd(qn, kn, u, w, a, gcol, grow, ssave, do, name):
    S, Wd = qn.shape
    nc = S // CHUNK
    tok, colv, rowv, mat, smat = _gdn_scan_specs(nc, True)

    def body(q_ref, k_ref, u_ref, w_ref, a_ref, gc_ref, gr_ref, sh_ref, do_ref,
             du_ref, dw_ref, dqd_ref, dkd_ref, da_ref, dgl_ref, ds_ref):
        i = pl.program_id(0)

        @pl.when(i == 0)
        def _():
            ds_ref[...] = jnp.zeros_like(ds_ref)

        tri, _ = _chunk_masks()
        gc_all = gc_ref[...]
        sub4 = lax.broadcasted_iota(jnp.int32, (B_HEADS, CHUNK), 0)
        lane_last = lax.broadcasted_iota(jnp.int32, (1, CHUNK), 1) == CHUNK - 1
        dgl_acc = jnp.zeros((B_HEADS, CHUNK), F32)
        for h in range(B_HEADS):
            hs = slice(h * B_DH, (h + 1) * B_DH)
            ep, em, el = _scan_decays(gc_all[:, h:h + 1], gr_ref[h:h + 1, :])
            s0 = sh_ref[h]
            ds = ds_ref[h]
            dout = do_ref[:, hs]
            wv = w_ref[:, hs]
            ut = u_ref[:, hs] - _bnn(wv, s0)
            dut = _btn(a_ref[h], dout) + _bnn(k_ref[:, hs] * em, ds)
            ds_ref[h] = el * ds + _btn(q_ref[:, hs] * ep, dout) - _btn(wv, dut)
            du_ref[:, hs] = dut
            dw_ref[:, hs] = -_bnt(dut, s0)
            dqd_ref[:, hs] = _bnt(dout, s0)
            dkd_ref[:, hs] = _bnt(ut, ds)
            da_ref[h] = jnp.where(tri, _bnt(dout, ut), 0.0)
            d_el = jnp.sum(jnp.sum(s0 * ds, axis=1, keepdims=True), axis=0, keepdims=True)
            dgl_acc = jnp.where(sub4 == h, jnp.where(lane_last, d_el * el, 0.0), dgl_acc)
        dgl_ref[...] = dgl_acc

    return pl.pallas_call(
        body, name=name, grid=(nc,), in_specs=[tok, tok, tok, tok, mat, colv, rowv, smat, tok],
        out_specs=[tok, tok, tok, tok, mat, rowv],
        out_shape=[jax.ShapeDtypeStruct((S, Wd), F32)] * 4 + [jax.ShapeDtypeStruct((nc, B_HEADS, CHUNK, CHUNK), F32),
                                                             jax.ShapeDtypeStruct((nc, B_HEADS, CHUNK), F32)],
        scratch_shapes=[pltpu.VMEM((B_HEADS, B_DH, B_DH), F32)], compiler_params=_arb(1),
    )(qn, kn, u, w, a, gcol, grow, ssave, do)


def _gdn_local_bwd(qn, kn, v, bcol, gcol, grow, tsave, du, dw, dqd, dkd, da, dgl, name, exchange=()):
    S, Wd = qn.shape
    nc = S // CHUNK
    steps = nc // GROUP
    tok, colv, rowv, mat = _gdn_group_specs(steps, Wd)
    ne = len(exchange)

    def body(q_ref, k_ref, v_ref, bc_ref, gc_ref, gr_ref, t_ref, du_ref, dw_ref, dqd_ref, dkd_ref, da_ref, dgl_ref, *rest):
        srcs, (dq_ref, dk_ref, dv_ref, dbc_ref, dgc_ref, dgr_ref) = rest[:ne], rest[ne:ne + 6]
        eouts, sems = rest[ne + 6:2 * ne + 6], rest[2 * ne + 6:]
        i = pl.program_id(0)
        done = _carry(_chips_phases(srcs, eouts, *sems), i == 0, None, i == steps - 1) if ne else None
        tri, strict = _chunk_masks()
        lane4 = lax.broadcasted_iota(jnp.int32, (CHUNK, B_HEADS), 1)
        sub4 = lax.broadcasted_iota(jnp.int32, (B_HEADS, CHUNK), 0)
        lane_last = lax.broadcasted_iota(jnp.int32, (1, CHUNK), 1) == CHUNK - 1
        for c in range(GROUP):
            rs = slice(c * CHUNK, (c + 1) * CHUNK)
            bc_all, gc_all = bc_ref[c], gc_ref[c]
            dbc_acc = jnp.zeros((CHUNK, B_HEADS), F32)
            dgc_acc = jnp.zeros((CHUNK, B_HEADS), F32)
            dgr_acc = jnp.zeros((B_HEADS, CHUNK), F32)
            for h in range(B_HEADS):
                hs = slice(h * B_DH, (h + 1) * B_DH)
                q, k, vv = q_ref[rs, hs], k_ref[rs, hs], v_ref[rs, hs]
                bc, gc, gr = bc_all[:, h:h + 1], gc_all[:, h:h + 1], gr_ref[c, h:h + 1, :]
                dm, kk, ep, em, el, vb, kb = _chunk_local(q, k, vv, bc, gc, gr, tri)
                t = t_ref[c, h]
                qk = _bnt(q, k)
                dut, dwv, dqd, dkd, dav = du_ref[rs, hs], dw_ref[rs, hs], dqd_ref[rs, hs], dkd_ref[rs, hs], da_ref[c, h]
                dt = _nt(dut, vb, SOLVE_PREC) + _nt(dwv, kb, SOLVE_PREC)
                dvb = _tn(t, dut, SOLVE_PREC)
                dkb = _tn(t, dwv, SOLVE_PREC)
                dl = jnp.where(strict, -_tn(t, _nt(dt, t, SOLVE_PREC), SOLVE_PREC), 0.0)
                g1 = dl * dm
                dkb_k = jnp.sum(dkb * k, axis=1, keepdims=True)
                dbeta = jnp.sum(g1 * kk, axis=1, keepdims=True) + jnp.sum(dvb * vv, axis=1, keepdims=True) + dkb_k * ep
                dkk = g1 * bc
                ddm = dl * (bc * kk) + dav * qk
                dqk = dav * dm
                dq_ref[rs, hs] = _bnn(dqk, k) + dqd * ep
                dk_ref[rs, hs] = _btn(dqk, q) + _bnn(dkk, k) + _btn(dkk, k) + dkb * (bc * ep) + dkd * em
                dv_ref[rs, hs] = dvb * bc
                dep = dkb_k * bc + jnp.sum(dqd * q, axis=1, keepdims=True)
                dem = jnp.sum(dkd * k, axis=1, keepdims=True)
                mm = ddm * dm
                dgam_c = jnp.sum(mm, axis=1, keepdims=True) + dep * ep - dem * em
                dglast = jnp.sum(dem * em, axis=0, keepdims=True)
                dgam_r = -jnp.sum(mm, axis=0, keepdims=True) + jnp.where(lane_last, dglast, 0.0) + dgl_ref[c, h:h + 1, :]
                dbc_acc = jnp.where(lane4 == h, dbeta, dbc_acc)
                dgc_acc = jnp.where(lane4 == h, dgam_c, dgc_acc)
                dgr_acc = jnp.where(sub4 == h, dgam_r, dgr_acc)
            dbc_ref[c] = dbc_acc
            dgc_ref[c] = dgc_acc
            dgr_ref[c] = dgr_acc
        if ne:
            done()

    res = pl.pallas_call(
        body, name=name, grid=(steps,),
        in_specs=[tok, tok, tok, colv, colv, rowv, mat, tok, tok, tok, tok, mat, rowv] + [HBM] * ne,
        out_specs=[tok, tok, tok, colv, colv, rowv] + [HBM] * ne,
        out_shape=[jax.ShapeDtypeStruct((S, Wd), F32)] * 3
        + [jax.ShapeDtypeStruct((nc, CHUNK, B_HEADS), F32)] * 2 + [jax.ShapeDtypeStruct((nc, B_HEADS, CHUNK), F32)]
        + _chips_out_shapes(exchange),
        scratch_shapes=_chips_scratch(ne) if ne else [], compiler_params=_arb(1),
    )(qn, kn, v, bcol, gcol, grow, tsave, du, dw, dqd, dkd, da, dgl, *exchange)
    return tuple(res[:6]) + (list(res[6:]),)


def _gdn_post_fwd(o, proj, ng, name):
    S, W = o.shape
    tm = _pick(S, 512, 8)

    def body(o_ref, z_ref, g_ref, y_ref):
        gv = g_ref[...]
        for h in range(B_HEADS):
            hs = slice(h * B_DH, (h + 1) * B_DH)
            oh = o_ref[:, hs]
            z = z_ref[:, hs]
            r = lax.rsqrt(jnp.mean(oh * oh, axis=-1, keepdims=True) + EPS)
            y_ref[:, hs] = (oh * r * gv * (z * _sigmoid(z))).astype(BF16)

    return pl.pallas_call(
        body, name=name, grid=(S // tm,), in_specs=[_rows(tm, W), _rows(tm, W, P_Z // W), _vec(B_DH)],
        out_specs=_rows(tm, W), out_shape=jax.ShapeDtypeStruct((S, W), BF16), compiler_params=_par(1),
    )(o, proj, ng)


def _gdn_post_bwd(dy, o, proj, ng, name):
    S, W = o.shape
    tm = _pick(S, 512, 8)

    def body(dy_ref, o_ref, z_ref, g_ref, do_ref, dz_ref, dg_ref):
        i = pl.program_id(0)
        gv = g_ref[...]
        dg = jnp.zeros((1, B_DH), F32)
        for h in range(B_HEADS):
            hs = slice(h * B_DH, (h + 1) * B_DH)
            oh = o_ref[:, hs]
            z = z_ref[:, hs]
            d = dy_ref[:, hs]
            r = lax.rsqrt(jnp.mean(oh * oh, axis=-1, keepdims=True) + EPS)
            n = oh * r
            sg = _sigmoid(z)
            sz = z * sg
            dn = d * gv * sz
            dg = dg + jnp.sum(d * n * sz, axis=0, keepdims=True)
            dz_ref[:, hs] = (d * n * gv * (sg * (1.0 + z * (1.0 - sg)))).astype(BF16)
            do_ref[:, hs] = r * (dn - n * jnp.mean(dn * n, axis=-1, keepdims=True))
        _acc(dg_ref, dg, i)

    return pl.pallas_call(
        body, name=name, grid=(S // tm,), in_specs=[_rows(tm, W), _rows(tm, W), _rows(tm, W, P_Z // W), _vec(B_DH)],
        out_specs=[_rows(tm, W), _rows(tm, W), _vec(B_DH)],
        out_shape=[jax.ShapeDtypeStruct((S, W), F32), jax.ShapeDtypeStruct((S, W), BF16),
                   jax.ShapeDtypeStruct((1, B_DH), F32)],
        compiler_params=_arb(1),
    )(dy, o, proj, ng)


def _ada_mod(c_all, w_ada, b_shard, name):
    L, D, Ns = w_ada.shape
    B = c_all.shape[0]

    def body(c_ref, w_ref, b_ref, o_ref):
        cv = c_ref[...]
        cond = (cv * _sigmoid(cv)).astype(BF16)
        o_ref[...] = _nn(cond, w_ref[...].astype(BF16)) + b_ref[...]

    return pl.pallas_call(
        body, name=name, grid=(L,),
        in_specs=[pl.BlockSpec((B, D), lambda l: (0, 0)), pl.BlockSpec((None, D, Ns), lambda l: (l, 0, 0)),
                  pl.BlockSpec((None, 1, Ns), lambda l: (l, 0, 0))],
        out_specs=pl.BlockSpec((None, B, Ns), lambda l: (l, 0, 0)),
        out_shape=jax.ShapeDtypeStruct((L, B, Ns), F32), compiler_params=_par(1),
    )(c_all, w_ada, b_shard)


def _ada_wgrad(c_all, dmod, name):
    L, B, Ns = dmod.shape
    D = c_all.shape[1]

    def body(c_ref, d_ref, o_ref):
        cv = c_ref[...]
        cond = (cv * _sigmoid(cv)).astype(BF16)
        o_ref[...] = _tn(cond, d_ref[...].astype(BF16))

    return pl.pallas_call(
        body, name=name, grid=(L,),
        in_specs=[pl.BlockSpec((B, D), lambda l: (0, 0)), pl.BlockSpec((None, B, Ns), lambda l: (l, 0, 0))],
        out_specs=pl.BlockSpec((None, D, Ns), lambda l: (l, 0, 0)),
        out_shape=jax.ShapeDtypeStruct((L, D, Ns), F32), compiler_params=_par(1),
    )(c_all, dmod)


W_IN_PIECES = ((0, 0, 1410), (1, 0, 1410), (2, 0, 252), (2, 772, 638), (3, 0, 1410), (2, 252, 512), (2, 764, 8))


def _reorder_w_in(w4, name):
    L, _, D, Cs = w4.shape
    tm = _pick(D, 256, 16)
    used = sum(p[2] for p in W_IN_PIECES)

    def body(w_ref, o_ref):
        shard = [w_ref[s] for s in range(4)]
        parts = [shard[s][:, lo:lo + n] for s, lo, n in W_IN_PIECES]
        o_ref[...] = jnp.concatenate(parts + [jnp.zeros((tm, P_END - used), w4.dtype)], axis=1)

    return pl.pallas_call(
        body, name=name, grid=(L, D // tm), in_specs=[pl.BlockSpec((None, 4, tm, Cs), lambda l, i: (l, 0, i, 0))],
        out_specs=pl.BlockSpec((None, tm, P_END), lambda l, i: (l, i, 0)),
        out_shape=jax.ShapeDtypeStruct((L, D, P_END), w4.dtype), compiler_params=_par(2),
    )(w4)


def _restore_w_in(g, name):
    D = g.shape[0]
    tm = _pick(D, 256, 8)

    def body(g_ref, o_ref):
        gv = g_ref[...]
        off = 0
        pieces = {}
        for s, lo, n in W_IN_PIECES:
            pieces.setdefault(s, []).append((lo, gv[:, off:off + n]))
            off += n
        for s, lst in pieces.items():
            lst.sort(key=lambda t: t[0])
            o_ref[s] = lst[0][1] if len(lst) == 1 else jnp.concatenate([t[1] for t in lst], axis=1)

    return pl.pallas_call(
        body, name=name, grid=(D // tm,), in_specs=[pl.BlockSpec((tm, P_END), lambda i: (i, 0))],
        out_specs=pl.BlockSpec((4, tm, W_IN_SHARD), lambda i: (0, i, 0)),
        out_shape=jax.ShapeDtypeStruct((4, D, W_IN_SHARD), g.dtype), compiler_params=_par(1),
    )(g)


def _adam_update(w, g, m, v):
    mn = ADAM_B1 * m + (1.0 - ADAM_B1) * g
    vn = ADAM_B2 * v + (1.0 - ADAM_B2) * (g * g)
    m_hat = mn / (1.0 - ADAM_B1 ** ADAM_STEP)
    v_hat = vn / (1.0 - ADAM_B2 ** ADAM_STEP)
    return -ADAM_LR * (m_hat / (jnp.sqrt(v_hat) + ADAM_EPS) + ADAM_WD * w), mn, vn


def _adamw(w, g, m, v, name):
    shape = w.shape
    C = shape[-1]
    R = w.size // C
    tm = _pick(R, 512, 8)
    spec = pl.BlockSpec((tm, C), lambda i: (i, 0))

    def body(w_ref, g_ref, m_ref, v_ref, d_ref, mo_ref, vo_ref):
        d_ref[...], mo_ref[...], vo_ref[...] = _adam_update(w_ref[...], g_ref[...], m_ref[...], v_ref[...])

    outs = pl.pallas_call(
        body, name=name, grid=(R // tm,), in_specs=[spec] * 4, out_specs=[spec] * 3,
        out_shape=[jax.ShapeDtypeStruct((R, C), F32)] * 3, compiler_params=_par(1),
    )(*(t.reshape(R, C) for t in (w, g, m, v)))
    return tuple(o.reshape(shape) for o in outs)


def _adamw_layers(w, gs, m, v, name):
    L, R, C = w.shape
    tm = _pick(R, 128, 8)
    spec = pl.BlockSpec((None, tm, C), lambda l, i: (l, i, 0))
    g_specs = [pl.BlockSpec((tm, C), functools.partial(lambda ll, l, i: (jnp.where(l == ll, i, 0), 0), ll))
               for ll in range(L)]

    def body(w_ref, m_ref, v_ref, *rest):
        g_refs, (go_ref, d_ref, mo_ref, vo_ref) = rest[:L], rest[L:]
        l = pl.program_id(0)
        for ll in range(L):
            @pl.when(l == ll)
            def _():
                g = g_refs[ll][...]
                go_ref[...] = g
                d_ref[...], mo_ref[...], vo_ref[...] = _adam_update(w_ref[...], g, m_ref[...], v_ref[...])

    return pl.pallas_call(
        body, name=name, grid=(L, R // tm), in_specs=[spec] * 3 + g_specs, out_specs=[spec] * 4,
        out_shape=[jax.ShapeDtypeStruct((L, R, C), F32)] * 4, compiler_params=_arb(2),
    )(w, m, v, *gs)


def _pair_sums(a, where, b, name):
    NB, _, R, C = a.shape

    def body(where_ref, a_ref, b_ref, p_ref, own_ref):
        s = a_ref[...] + b_ref[...].astype(F32)
        p_ref[...] = s.astype(BF16)

        @pl.when(pl.program_id(0) == where_ref[1])
        def _():
            own_ref[...] = s

    return pl.pallas_call(
        body, name=name,
        grid_spec=pltpu.PrefetchScalarGridSpec(
            num_scalar_prefetch=1, grid=(NB,),
            in_specs=[pl.BlockSpec((None, None, R, C), lambda k, w: (k, w[0], 0, 0)),
                      pl.BlockSpec((None, R, C), lambda k, w: (k, 0, 0))],
            out_specs=[pl.BlockSpec((None, R, C), lambda k, w: (k, 0, 0)), pl.BlockSpec((R, C), lambda k, w: (0, 0))]),
        out_shape=[jax.ShapeDtypeStruct((NB, R, C), BF16), jax.ShapeDtypeStruct((R, C), F32)],
        compiler_params=_arb(1),
    )(where, a, b)


def _sum_own_and_received(own, recv, where, name):
    R, C = own.shape
    tm = _pick(R, 256, 16)

    def body(where_ref, p_ref, r_ref, o_ref):
        o_ref[...] = ((p_ref[...] + r_ref[0].astype(F32)) + r_ref[1].astype(F32)) + r_ref[2].astype(F32)

    return pl.pallas_call(
        body, name=name,
        grid_spec=pltpu.PrefetchScalarGridSpec(
            num_scalar_prefetch=1, grid=(R // tm,),
            in_specs=[pl.BlockSpec((tm, C), lambda i, w: (i, 0)), pl.BlockSpec((3, tm, C), lambda i, w: (0, i, 0))],
            out_specs=pl.BlockSpec((None, tm, C), lambda i, w: (w[0], i, 0))),
        out_shape=jax.ShapeDtypeStruct((2, R, C), F32), compiler_params=_par(1),
    )(where, own, recv)


def _position():
    return lax.axis_index("x"), lax.axis_index("y"), lax.axis_index("c")


def _other_chips(x, y):
    return [(1 - x, y), (x, 1 - y), (1 - x, 1 - y)]


HBM = pl.BlockSpec(memory_space=pl.ANY)


def _allgather8(blk, name, reduce_rows=None):
    M, N = blk.shape

    def body(x_ref, out_ref, *rest):
        if reduce_rows is None:
            send_sems, recv_sems, local_sem = rest
        else:
            sum_ref, send_sems, recv_sems, local_sem = rest
        x, y, c = _position()
        me, sibling = (x, y, c), (x, y, 1 - c)
        chips = _other_chips(x, y)

        def rows(px, py, pc):
            return out_ref.at[pl.ds((4 * px + 2 * py + pc) * M, M), :]

        def copy(k, block, to, src=None):
            return pltpu.make_async_remote_copy(
                src_ref=rows(*block) if src is None else src, dst_ref=rows(*block),
                send_sem=send_sems.at[k], recv_sem=recv_sems.at[k], device_id=to, device_id_type=MESH)

        mine = pltpu.make_async_copy(x_ref, rows(*me), local_sem)
        mine.start()
        first = [copy(0, me, sibling, src=x_ref)]
        first += [copy(1 + j, me, (*chip, c), src=x_ref) for j, chip in enumerate(chips)]
        for cp in first:
            cp.start()
        passed = [copy(4 + j, (*chip, c), sibling) for j, chip in enumerate(chips)]
        for j, chip in enumerate(chips):
            copy(1 + j, (*chip, c), me).wait_recv()
            passed[j].start()
        copy(0, sibling, me).wait_recv()
        for j, chip in enumerate(chips):
            copy(4 + j, (*chip, 1 - c), me).wait_recv()
        for cp in first + passed:
            cp.wait_send()
        mine.wait()
        if reduce_rows is not None:
            tot = out_ref[pl.ds(0, reduce_rows), :]
            for d in range(1, 8):
                tot = tot + out_ref[pl.ds(d * M, reduce_rows), :]
            sum_ref[...] = tot

    vmem = pl.BlockSpec(memory_space=pltpu.VMEM)
    out_shape = [jax.ShapeDtypeStruct((8 * M, N), blk.dtype)]
    if reduce_rows is not None:
        out_shape.append(jax.ShapeDtypeStruct((reduce_rows, N), blk.dtype))
    res = pl.pallas_call(
        body, name=name, out_shape=out_shape, in_specs=[vmem], out_specs=[vmem] * len(out_shape),
        scratch_shapes=[pltpu.SemaphoreType.DMA((7,)), pltpu.SemaphoreType.DMA((7,)), pltpu.SemaphoreType.DMA],
    )(blk)
    return res[0] if reduce_rows is None else (res[0], res[1])


def _gather_phases(layer, srcs, outs, send_sems, recv_sems, local_sems):
    n = len(srcs)
    x, y, c = _position()
    me, sibling = (x, y, c), (x, y, 1 - c)
    chips = _other_chips(x, y)

    def region(t, px, py, pc):
        return outs[t].at[2 * px + py, pc]

    def copy(t, k, block, to, own=False):
        return pltpu.make_async_remote_copy(
            src_ref=srcs[t].at[layer, c] if own else region(t, *block), dst_ref=region(t, *block),
            send_sem=send_sems.at[7 * t + k], recv_sem=recv_sems.at[7 * t + k], device_id=to, device_id_type=MESH)

    def local(t):
        return pltpu.make_async_copy(srcs[t].at[layer, c], region(t, *me), local_sems.at[t])

    def first(t):
        return [copy(t, 0, me, sibling, own=True)] + [copy(t, 1 + j, me, (*chip, c), own=True)
                                                       for j, chip in enumerate(chips)]

    def start():
        for t in range(n):
            local(t).start()
        for t in range(n):
            for cp in first(t):
                cp.start()

    def forward():
        for j, chip in enumerate(chips):
            for t in range(n):
                copy(t, 1 + j, (*chip, c), me).wait_recv()
                copy(t, 4 + j, (*chip, c), sibling).start()

    def finish():
        for t in range(n):
            copy(t, 0, sibling, me).wait_recv()
        for j, chip in enumerate(chips):
            for t in range(n):
                copy(t, 4 + j, (*chip, 1 - c), me).wait_recv()
        for t in range(n):
            for cp in first(t) + [copy(t, 4 + j, (*chip, c), sibling) for j, chip in enumerate(chips)]:
                cp.wait_send()
            local(t).wait()

    return start, forward, finish


def _gather_scratch(n):
    return [pltpu.SemaphoreType.DMA((7 * n,)), pltpu.SemaphoreType.DMA((7 * n,)), pltpu.SemaphoreType.DMA((n,))]


def _gather_out_shapes(shards):
    return [jax.ShapeDtypeStruct((4,) + s.shape[1:], s.dtype) for s in shards]


def _gather_weights(shards, layer, name):
    n = len(shards)

    def body(*refs):
        start, forward, finish = _gather_phases(layer, refs[:n], refs[n:2 * n], *refs[2 * n:])
        start()
        forward()
        finish()

    return pl.pallas_call(
        body, name=name, out_shape=_gather_out_shapes(shards), in_specs=[HBM] * n, out_specs=[HBM] * n,
        scratch_shapes=_gather_scratch(n),
    )(*shards)


def _rs_sibling(gs, name):
    n = len(gs)

    def body(*refs):
        srcs, outs = refs[:n], refs[n:2 * n]
        send_sems, recv_sems = refs[2 * n:]
        x, y, c = _position()
        copies = [pltpu.make_async_remote_copy(
            src_ref=srcs[t].at[k, 1 - c], dst_ref=outs[t].at[k], send_sem=send_sems.at[4 * t + k],
            recv_sem=recv_sems.at[4 * t + k], device_id=(x, y, 1 - c), device_id_type=MESH)
            for t in range(n) for k in range(4)]
        for cp in copies:
            cp.start()
        for cp in copies:
            cp.wait()

    out_shape = [jax.ShapeDtypeStruct((4,) + g.shape[2:], g.dtype) for g in gs]
    return pl.pallas_call(
        body, name=name, out_shape=out_shape, in_specs=[HBM] * n, out_specs=[HBM] * n,
        scratch_shapes=[pltpu.SemaphoreType.DMA((4 * n,)), pltpu.SemaphoreType.DMA((4 * n,))],
    )(*gs)


def _rs_chips(ps, name):
    n = len(ps)

    def body(*refs):
        start, finish = _chips_phases(refs[:n], refs[n:2 * n], *refs[2 * n:])
        start()
        finish()

    return pl.pallas_call(
        body, name=name, out_shape=_chips_out_shapes(ps), in_specs=[HBM] * n, out_specs=[HBM] * n,
        scratch_shapes=_chips_scratch(n),
    )(*ps)


def _chips_phases(srcs, outs, send_sems, recv_sems):
    x, y, c = _position()
    copies = [pltpu.make_async_remote_copy(
        src_ref=srcs[t].at[2 * px + py], dst_ref=outs[t].at[j], send_sem=send_sems.at[3 * t + j],
        recv_sem=recv_sems.at[3 * t + j], device_id=(px, py, c), device_id_type=MESH)
        for t in range(len(srcs)) for j, (px, py) in enumerate(_other_chips(x, y))]

    def start():
        for cp in copies:
            cp.start()

    def finish():
        for cp in copies:
            cp.wait()

    return start, finish


def _chips_scratch(n):
    return [pltpu.SemaphoreType.DMA((3 * n,)), pltpu.SemaphoreType.DMA((3 * n,))]


def _chips_out_shapes(ps):
    return [jax.ShapeDtypeStruct((3,) + p.shape[1:], p.dtype) for p in ps]


def _rs_pair(hs, name):
    n = len(hs)

    def body(*refs):
        bufs = refs[n:2 * n]
        send_sems, recv_sems = refs[2 * n:]
        x, y, c = _position()

        def copy(t, half):
            return pltpu.make_async_remote_copy(
                src_ref=bufs[t].at[half], dst_ref=bufs[t].at[half], send_sem=send_sems.at[t], recv_sem=recv_sems.at[t],
                device_id=(x, y, 1 - c), device_id_type=MESH)

        for t in range(n):
            copy(t, c).start()
        for t in range(n):
            copy(t, 1 - c).wait_recv()
        for t in range(n):
            copy(t, c).wait_send()

    out_shape = [jax.ShapeDtypeStruct(h.shape, h.dtype) for h in hs]
    return pl.pallas_call(
        body, name=name, out_shape=out_shape, in_specs=[HBM] * n, out_specs=[HBM] * n,
        input_output_aliases={t: t for t in range(n)},
        scratch_shapes=[pltpu.SemaphoreType.DMA((n,)), pltpu.SemaphoreType.DMA((n,))],
    )(*hs)


BIG = ("w_in", "w_branch_a", "w_branch_b", "w_out", "w_ffn_in", "w_ffn_out")
CARRY_ATTN = ["w_in"]
CARRY_LOCAL = ["w_ffn_in"]
CARRY_SCAN = ["w_branch_a", "w_branch_b", "w_out", "w_ffn_out"]
CARRY_DATTN = ["w_in", "w_ffn_in"]
CARRY_DLOCAL = ["w_branch_a", "w_branch_b", "w_out", "w_ffn_out"]


def _band_bias(rel_table, name):
    L, H, n = rel_table.shape
    tab = jnp.pad(rel_table, ((0, 0), (0, 0), (0, NREL_PAD - n))).reshape(L * H, 1, NREL_PAD)
    band = (A_PAST + 1) * CHUNK

    def body(t_ref, o_ref):
        r = lax.broadcasted_iota(jnp.int32, (NREL_PAD, SKEW_W), 0)
        xi = lax.broadcasted_iota(jnp.int32, (NREL_PAD, SKEW_W), 1)
        diag = jnp.where(xi < KSPAN, xi, xi - SKEW_W)
        rel = jnp.clip(A_PAST * CHUNK - diag, -A_MAX_REL, A_MAX_REL) + A_MAX_REL
        e = _nn(t_ref[...], jnp.where(rel == r, 1.0, 0.0).astype(F32), HI)
        x = jnp.broadcast_to(e, (QBLK, SKEW_W))
        row = lax.broadcasted_iota(jnp.int32, (QBLK, SKEW_W), 0)
        for b in range(QBLK.bit_length() - 1):
            x = jnp.where(((row >> b) & 1) == 1, pltpu.roll(x, 1 << b, 1), x)
        x = x[:, :KSPAN]
        first = (lax.broadcasted_iota(jnp.int32, (QBLK, KSPAN), 0) // CHUNK) * CHUNK
        col = lax.broadcasted_iota(jnp.int32, (QBLK, KSPAN), 1)
        o_ref[...] = jnp.where((col >= first) & (col < first + band), x, NEG)

    out = pl.pallas_call(
        body, name=name, grid=(L * H,), in_specs=[pl.BlockSpec((None, 1, NREL_PAD), lambda i: (i, 0, 0))],
        out_specs=pl.BlockSpec((None, QBLK, KSPAN), lambda i: (i, 0, 0)),
        out_shape=jax.ShapeDtypeStruct((L * H, QBLK, KSPAN), F32), compiler_params=_par(1),
    )(tab)
    return out.reshape(L, H, QBLK, KSPAN)


def _col_row_forms(t, S):
    nc = S // CHUNK
    return t.T.reshape(nc, CHUNK, B_HEADS), t.reshape(B_HEADS, nc, CHUNK).transpose(1, 0, 2)


def _layer_fwd(l, x, mod, W, P, big, gather=None):
    S, D = x.shape
    n = lambda s: f"{s}_l{l}"
    sh1, sc1, gt1, sh2, sc2, gt2 = (mod[i:i + 1] for i in range(6))
    h1 = _lnmod_fwd(x, P["norm1_g"][l:l + 1], sc1, sh1, n("ln1"))
    proj = _matmul(h1, W["w_in"], "nn", F32, n("proj"), tn=1152)
    part = (lambda names: ([gather[0][BIG.index(k)] for k in names], gather[1])) if gather is not None else (lambda names: None)
    ya, got_a = _attn_fwd(proj, big, n("attn"), part(CARRY_ATTN))
    ba = proj[:, P_BA:P_BA + 2 * B_HEADS]
    b_t, a_t = ba[:, :B_HEADS].T, ba[:, B_HEADS:].T
    alog, dtb = P["a_log"][l].reshape(B_HEADS, 1), P["dt_bias"][l].reshape(B_HEADS, 1)
    beta, gam = _gdn_gates_fwd(b_t, a_t, alog, dtb, n("gates"))
    bcol, _ = _col_row_forms(beta, S)
    gcol, grow = _col_row_forms(gam, S)
    qn, kn, v = _gdn_pre_fwd(proj, P["w_conv"][l], n("gdnpre"))
    tsave, amat, u, w, got_l = _gdn_local_fwd(qn, kn, v, bcol, gcol, grow, n("gdnlocal"), part(CARRY_LOCAL))
    o, ssave, got_s = _gdn_scan_fwd(qn, kn, u, w, amat, gcol, grow, n("gdnscan"), part(CARRY_SCAN))
    got = dict(zip(CARRY_ATTN + CARRY_LOCAL + CARRY_SCAN, got_a + got_l + got_s))
    gathered = [got[k] for k in BIG] if gather is not None else None
    yb = _gdn_post_fwd(o, proj, P["gdn_norm_g"][l:l + 1], n("gdnpost"))
    pa = _matmul(ya, W["w_branch_a"], "nn", F32, n("pa"), tm=2048, stacked=True)
    pb = _matmul(yb, W["w_branch_b"], "nn", F32, n("pb"), tm=2048, stacked=True)
    merged = _merge_fwd(proj, pa, pb, n("merge"))
    ao = _matmul(merged, W["w_out"], "nn", F32, n("ao"))
    x1 = _gate_fwd(x, ao, gt1, n("res1"))
    h2 = _lnmod_fwd(x1, P["norm2_g"][l:l + 1], sc2, sh2, n("ln2"))
    gu = _matmul(h2, W["w_ffn_in"], "nn", F32, n("gu"), stacked=True)
    act = _ffn_act_fwd(gu, n("act"))
    fo = _matmul(act, W["w_ffn_out"], "nn", F32, n("fo"), tk=1408)
    x2 = _gate_fwd(x1, fo, gt2, n("res2"))
    saved = dict(x=x, h1=h1, proj=proj, ya=ya, b_t=b_t, a_t=a_t, bcol=bcol, gcol=gcol, grow=grow,
                 qn=qn, kn=kn, v=v, o=o, tsave=tsave, ssave=ssave, amat=amat, u=u, w=w, yb=yb, pa=pa, pb=pb,
                 merged=merged, ao=ao, x1=x1,
                 h2=h2, gu=gu, act=act, fo=fo)
    return x2, saved, gathered


def _layer_bwd(l, dx2, sv, mod, W, P, big, exchange=()):
    S, D = dx2.shape
    n = lambda s: f"{s}_l{l}"
    sh1, sc1, gt1, sh2, sc2, gt2 = (mod[i:i + 1] for i in range(6))
    g = {}
    view = lambda t: t.reshape((4, 2, t.shape[-2] // (2 if t.ndim == 3 else 8), t.shape[-1]))
    dz2, dgt2 = _gate_bwd(dx2, sv["fo"], gt2, n("dres2"))
    g["w_ffn_out"] = view(_matmul(sv["act"], dz2, "tn", F32, n("dwfo"), tm=1408))
    dact = _matmul(dz2, W["w_ffn_out"], "nt", F32, n("dact"), tn=1408)
    dgu = _ffn_act_bwd(sv["gu"], dact, n("dgu"))
    g["w_ffn_in"] = view(_matmul(sv["h2"], dgu, "tn", F32, n("dwfi"), out_stacked=True))
    dh2 = _matmul(dgu, W["w_ffn_in"], "nt", F32, n("dh2"), stacked=True)
    dx1, dsh2, dsc2, dn2 = _lnmod_bwd(dh2, sv["x1"], P["norm2_g"][l:l + 1], sc2, dx2, n("dln2"))
    dz1, dgt1 = _gate_bwd(dx1, sv["ao"], gt1, n("dres1"))
    g["w_out"] = view(_matmul(sv["merged"], dz1, "tn", F32, n("dwo")))
    dmerged = _matmul(dz1, W["w_out"], "nt", F32, n("dmerged"))
    dgab, dpa, dpb = _merge_bwd(sv["proj"], sv["pa"], sv["pb"], dmerged, n("dmerge"))
    g["w_branch_a"] = view(_matmul(sv["ya"], dpa, "tn", F32, n("dwa"), out_stacked=True))
    g["w_branch_b"] = view(_matmul(sv["yb"], dpb, "tn", F32, n("dwb"), out_stacked=True))
    dya = _matmul(dpa, W["w_branch_a"], "nt", BF16, n("dya"), tm=2048, stacked=True)
    dyb = _matmul(dpb, W["w_branch_b"], "nt", F32, n("dyb"), tm=2048, stacked=True)
    ex = (lambda names: [exchange[BIG.index(k)] for k in names]) if len(exchange) else (lambda names: ())
    dq, dk, dv, dbig, rec_a = _attn_bwd(sv["proj"], big, dya, n("dattn"), ex(CARRY_DATTN))
    g["rel_table"] = _rel_table_grad(dbig, n("drel"))[:, 0, :2 * A_MAX_REL + 1]
    do, dzb, dng = _gdn_post_bwd(dyb, sv["o"], sv["proj"], P["gdn_norm_g"][l:l + 1], n("dgdnpost"))
    g["gdn_norm_g"] = dng[0]
    du, dw, dqd, dkd, da, dgl = _gdn_scan_bwd(sv["qn"], sv["kn"], sv["u"], sv["w"], sv["amat"], sv["gcol"], sv["grow"],
                                              sv["ssave"], do, n("dgdnscan"))
    dqn, dkn, dvv, dbc, dgc, dgr, rec_l = _gdn_local_bwd(
        sv["qn"], sv["kn"], sv["v"], sv["bcol"], sv["gcol"], sv["grow"], sv["tsave"], du, dw, dqd, dkd, da, dgl,
        n("dgdnlocal"), ex(CARRY_DLOCAL))
    rec = dict(zip(CARRY_DATTN + CARRY_DLOCAL, rec_a + rec_l))
    received = [rec[k] for k in BIG] if len(exchange) else None
    dbeta_t = dbc.reshape(S, B_HEADS).T
    dgam_a = dgc.reshape(S, B_HEADS).T
    dgam_b = dgr.transpose(1, 0, 2).reshape(B_HEADS, S)
    alog, dtb = P["a_log"][l].reshape(B_HEADS, 1), P["dt_bias"][l].reshape(B_HEADS, 1)
    db_t, da_t, dal, ddt = _gdn_gates_bwd(dbeta_t, dgam_a, dgam_b, sv["b_t"], sv["a_t"], alog, dtb, n("dgates"))
    g["a_log"], g["dt_bias"] = dal[:, 0], ddt[:, 0]
    dy = _gdn_pre_bwd_a(sv["proj"], P["w_conv"][l], dqn, dkn, dvv, n("dgdnpre_a"))
    dqkvb, g["w_conv"] = _gdn_pre_bwd_b(sv["proj"], P["w_conv"][l], dy, n("dgdnpre_b"))
    dba = jnp.concatenate([db_t.T, da_t.T, jnp.zeros((S, P_END - P_BA - 2 * B_HEADS), F32)], axis=1)
    dproj = jnp.concatenate([dq.astype(BF16), dk.astype(BF16), dv.astype(BF16), dqkvb, dgab, dzb, dba.astype(BF16)],
                            axis=1)
    g["w_in"] = view(_restore_w_in(_matmul(sv["h1"], dproj, "tn", F32, n("dwin"), tn=1152), n("dwin_cols")))
    dh1 = _matmul(dproj, W["w_in"], "nt", F32, n("dh1"), tk=1152)
    dx, dsh1, dsc1, dn1 = _lnmod_bwd(dh1, sv["x"], P["norm1_g"][l:l + 1], sc1, dx1, n("dln1"))
    g["norm1_g"], g["norm2_g"] = dn1[0], dn2[0]
    dmod = jnp.concatenate([dsh1, dsc1, dgt1, dsh2, dsc2, dgt2], axis=1)[0]
    return dx, g, dmod, received


SMALL = ("norm1_g", "norm2_g", "rel_table", "w_conv", "a_log", "dt_bias", "gdn_norm_g")
SMALL_PACK_C = 1024


def _as_rows(t):
    flat = t.reshape(-1)
    rows = -(-flat.shape[0] // SMALL_PACK_C)
    return jnp.pad(flat, (0, rows * SMALL_PACK_C - flat.shape[0])).reshape(rows, SMALL_PACK_C)


def _pack_rows(parts):
    blk = jnp.concatenate([_as_rows(p) for p in parts], axis=0)
    return jnp.pad(blk, ((0, -blk.shape[0] % 8), (0, 0)))


def _unpack_rows(blk, shapes):
    out, r = [], 0
    for shp in shapes:
        size = int(np.prod(shp))
        rows = -(-size // SMALL_PACK_C)
        out.append(blk[..., r:r + rows, :].reshape(blk.shape[:-2] + (rows * SMALL_PACK_C,))[..., :size]
                   .reshape(blk.shape[:-2] + tuple(shp)))
        r += rows
    return out


def kernel(x, c, w_ada, b_ada, norm1_g, norm2_g, w_in, rel_table, w_conv, a_log, dt_bias, gdn_norm_g, w_branch_a, w_branch_b, w_out, w_ffn_in, w_ffn_out, final_g, loss_target, m_w_ada, m_b_ada, m_norm1_g, m_norm2_g, m_w_in, m_rel_table, m_w_conv, m_a_log, m_dt_bias, m_gdn_norm_g, m_w_branch_a, m_w_branch_b, m_w_out, m_w_ffn_in, m_w_ffn_out, m_final_g, v_w_ada, v_b_ada, v_norm1_g, v_norm2_g, v_w_in, v_rel_table, v_w_conv, v_a_log, v_dt_bias, v_gdn_norm_g, v_w_branch_a, v_w_branch_b, v_w_out, v_w_ffn_in, v_w_ffn_out, v_final_g):
    weights = dict(w_ada=w_ada, b_ada=b_ada, norm1_g=norm1_g, norm2_g=norm2_g, w_in=w_in, rel_table=rel_table,
                   w_conv=w_conv, a_log=a_log, dt_bias=dt_bias, gdn_norm_g=gdn_norm_g, w_branch_a=w_branch_a,
                   w_branch_b=w_branch_b, w_out=w_out, w_ffn_in=w_ffn_in, w_ffn_out=w_ffn_out, final_g=final_g)
    mom_m = dict(w_ada=m_w_ada, b_ada=m_b_ada, norm1_g=m_norm1_g, norm2_g=m_norm2_g, w_in=m_w_in,
                 rel_table=m_rel_table, w_conv=m_w_conv, a_log=m_a_log, dt_bias=m_dt_bias, gdn_norm_g=m_gdn_norm_g,
                 w_branch_a=m_w_branch_a, w_branch_b=m_w_branch_b, w_out=m_w_out, w_ffn_in=m_w_ffn_in,
                 w_ffn_out=m_w_ffn_out, final_g=m_final_g)
    mom_v = dict(w_ada=v_w_ada, b_ada=v_b_ada, norm1_g=v_norm1_g, norm2_g=v_norm2_g, w_in=v_w_in,
                 rel_table=v_rel_table, w_conv=v_w_conv, a_log=v_a_log, dt_bias=v_dt_bias, gdn_norm_g=v_gdn_norm_g,
                 w_branch_a=v_w_branch_a, w_branch_b=v_w_branch_b, w_out=v_w_out, w_ffn_in=v_w_ffn_in,
                 w_ffn_out=v_w_ffn_out, final_g=v_final_g)
    xi, yi, ci = _position()
    chip = 2 * xi + yi
    dev = 2 * chip + ci
    L, D = norm1_g.shape
    NMOD = b_ada.shape[1] // D
    ns = w_ada.shape[2]
    cs = w_conv.shape[2]

    first_blk = _pack_rows([c, w_conv])
    first_all = _allgather8(first_blk, "gather_c").reshape(8, first_blk.shape[0], SMALL_PACK_C)
    c_all, w_conv_all = _unpack_rows(first_all, [(D,), w_conv.shape])
    w_conv_full = w_conv_all.reshape(4, 2, L, CONV_K, cs)[:, 0].transpose(1, 2, 0, 3).reshape(L, CONV_K, 4 * cs)
    b_shard = lax.dynamic_slice_in_dim(b_ada, chip * ns, ns, axis=1).reshape(L, 1, ns)
    mod_shard = _ada_mod(c_all, w_ada, b_shard, "ada_mod")
    mod_all = _allgather8(mod_shard.reshape(L * 8, ns), "gather_mod").reshape(4, 2, L, 8, ns)
    mod = lax.dynamic_index_in_dim(mod_all[:, 0], dev, axis=2, keepdims=False)
    mod = mod.transpose(1, 0, 2).reshape(L, NMOD, D)

    shards = [weights[k].astype(BF16) for k in BIG]
    shards = [s.reshape(s.shape[0], 2, s.shape[1] // 2, s.shape[2]) for s in shards]
    col_stacked = lambda t: t.reshape(4, 2 * t.shape[2], t.shape[3])
    row_joined = lambda t: t.reshape(8 * t.shape[2], t.shape[3])

    def layer_weights(l, gathered):
        gd = dict(zip(BIG, gathered))
        return dict(w_in=_reorder_w_in(col_stacked(gd["w_in"])[None], f"w_in_cols_l{l}")[0],
                    w_branch_a=col_stacked(gd["w_branch_a"]), w_branch_b=col_stacked(gd["w_branch_b"]),
                    w_ffn_in=col_stacked(gd["w_ffn_in"]), w_out=row_joined(gd["w_out"]),
                    w_ffn_out=row_joined(gd["w_ffn_out"]))

    P = dict(norm1_g=norm1_g, norm2_g=norm2_g, w_conv=w_conv_full, a_log=a_log, dt_bias=dt_bias,
             gdn_norm_g=gdn_norm_g)
    big = _band_bias(rel_table, "band_bias")

    W = [layer_weights(0, _gather_weights(shards, 0, "gather_weights_l0"))]
    xc = x[0]
    saved = []
    for l in range(L):
        xc, sv, gathered = _layer_fwd(l, xc, mod[l], W[l], P, big[l], (shards, l + 1) if l + 1 < L else None)
        saved.append(sv)
        if l + 1 < L:
            W.append(layer_weights(l + 1, gathered))
    dx, loss_dev, dfinal = _loss_head(xc, final_g.reshape(1, D), loss_target[0], "loss_head")

    where = jnp.stack([ci, chip]).astype(jnp.int32)
    grads = [None] * L
    dmods = [None] * L
    shard_grads = {k: [None] * L for k in BIG}

    def finish_reduce_scatter(l, sums, from_chips):
        halves = [_sum_own_and_received(s_[1], r_, where, f"rs_sum_{k}_l{l}")
                  for k, s_, r_ in zip(BIG, sums, from_chips)]
        for k, t in zip(BIG, _rs_pair(halves, f"rs_pair_l{l}")):
            shard_grads[k][l] = t.reshape(2 * t.shape[1], t.shape[2])

    pending = None
    for l in reversed(range(L)):
        exchange = [s_[0] for s_ in pending] if pending is not None else ()
        dx, grads[l], dmods[l], received = _layer_bwd(l, dx, saved[l], mod[l], W[l], P, big[l], exchange)
        if pending is not None:
            finish_reduce_scatter(l + 1, pending, received)
        gs = [grads[l][k] for k in BIG]
        from_sibling = _rs_sibling([g_.astype(BF16) for g_ in gs], f"rs_sibling_l{l}")
        pending = [_pair_sums(g_, where, r_, f"rs_pair_sum_{k}_l{l}") for k, g_, r_ in zip(BIG, gs, from_sibling)]
    finish_reduce_scatter(0, pending, _rs_chips([s_[0] for s_ in pending], "rs_chips_l0"))
    dmod = jnp.stack(dmods)

    small = {k: jnp.stack([grads[l][k] for l in range(L)]) for k in SMALL}
    parts = [dmod] + [small[k] for k in SMALL] + [dfinal, loss_dev[0, :1]]
    small_blk = _pack_rows(parts)
    srows = small_blk.shape[0]
    small_all, small_sum = _allgather8(small_blk, "gather_small", reduce_rows=srows)
    shapes = [dmod.shape] + [small[k].shape for k in SMALL] + [(D,), (1,)]
    tot = _unpack_rows(small_sum, shapes)
    G = dict(zip(SMALL, tot[1:1 + len(SMALL)]))
    G["b_ada"] = tot[0].reshape(b_ada.shape)
    G["w_conv"] = lax.dynamic_slice_in_dim(G["w_conv"], chip * cs, cs, axis=2)
    G["final_g"] = tot[-2]
    loss = tot[-1][0]
    dmod_all = _unpack_rows(small_all.reshape(8, srows, SMALL_PACK_C), [dmod.shape])[0]
    dmod_cols = lax.dynamic_slice_in_dim(dmod_all, chip * ns, ns, axis=2).transpose(1, 0, 2)
    G["w_ada"] = _ada_wgrad(c_all, dmod_cols, "ada_wgrad")

    order = ["w_ada", "b_ada", "norm1_g", "norm2_g", "w_in", "rel_table", "w_conv", "a_log", "dt_bias", "gdn_norm_g",
             "w_branch_a", "w_branch_b", "w_out", "w_ffn_in", "w_ffn_out", "final_g"]
    deltas, new_m, new_v = {}, {}, {}
    for k in order:
        w = weights[k]
        if k in BIG:
            G[k], deltas[k], new_m[k], new_v[k] = _adamw_layers(w, shard_grads[k], mom_m[k], mom_v[k], f"adamw_{k}")
            continue
        as2d = (lambda t: t.reshape(1, -1)) if w.ndim == 1 else (lambda t: t)
        d_, m_, v_ = _adamw(as2d(w), as2d(G[k]), as2d(mom_m[k]), as2d(mom_v[k]), f"adamw_{k}")
        deltas[k], new_m[k], new_v[k] = d_.reshape(w.shape), m_.reshape(w.shape), v_.reshape(w.shape)
    return (loss, dx[None], *[G[k] for k in order], *[deltas[k] for k in order], *[new_m[k] for k in order],
            *[new_v[k] for k in order])
```

```python
import functools

import numpy as np
import jax
import jax.numpy as jnp
from jax import lax
from jax.experimental import pallas as pl
from jax.experimental.pallas import tpu as pltpu

F32 = jnp.float32
BF16 = jnp.bfloat16
HI = lax.Precision.HIGHEST
SOLVE_PREC = lax.Precision.HIGH
MESH = pl.DeviceIdType.MESH

EPS = 1e-6
CHUNK = 64
A_HEADS = 8
A_DH = 64
A_PAST = 8
A_MAX_REL = 128
B_HEADS = 4
B_DH = 128
CONV_K = 4
LANE = 128
QBLK = 4 * CHUNK
KSPAN = QBLK + A_PAST * CHUNK
NEG = -1e30

ADAM_LR = 0.001
ADAM_B1 = 0.9
ADAM_B2 = 0.999
ADAM_EPS = 1e-08
ADAM_WD = 0.01
ADAM_STEP = 10

P_QKVA, P_QKVB, P_GA, P_GB, P_Z, P_BA, P_END = 0, 1536, 3072, 4096, 5120, 5632, 5760
W_IN_SHARD = 1410


def _sigmoid(x):
    return 1.0 / (1.0 + jnp.exp(-x))


def _dot(a, b, ca, cb, prec):
    lead = a.ndim - 2
    batch = ((0,), (0,)) if lead else ((), ())
    return lax.dot_general(a, b, (((ca + lead,), (cb + lead,)), batch), precision=prec, preferred_element_type=F32)


def _nn(a, b, prec=None):
    return _dot(a, b, 1, 0, prec)


def _nt(a, b, prec=None):
    return _dot(a, b, 1, 1, prec)


def _tn(a, b, prec=None):
    return _dot(a, b, 0, 0, prec)


def _bnn(a, b):
    return _nn(a.astype(BF16), b.astype(BF16))


def _bnt(a, b):
    return _nt(a.astype(BF16), b.astype(BF16))


def _btn(a, b):
    return _tn(a.astype(BF16), b.astype(BF16))


def _pick(n, target, unit=LANE):
    best = None
    for t in range(unit, min(n, target) + 1, unit):
        if n % t == 0:
            best = t
    return best if best is not None else n


def _acc(ref, val, i):
    @pl.when(i == 0)
    def _():
        ref[...] = val

    @pl.when(i != 0)
    def _():
        ref[...] += val


def _arb(n):
    return pltpu.CompilerParams(dimension_semantics=("arbitrary",) * n)


def _par(n):
    return pltpu.CompilerParams(dimension_semantics=("parallel",) * n)


def _matmul(a, b, mode, out_dtype, name, tm=1024, tn=1024, tk=1024, layer=None, stacked=False, out_stacked=False):
    bs = b.shape[1:] if layer is not None else b.shape
    if mode == "nn":
        M, K = a.shape
        N = 4 * bs[2] if stacked else bs[1]
        if stacked:
            tn = bs[2]
    elif mode == "nt":
        M, K = a.shape
        N = bs[1] if stacked else bs[0]
        if stacked:
            tk = bs[2]
    else:
        K, M = a.shape
        N = bs[1]
        if out_stacked:
            tn = N // 4
    tm, tn, tk = _pick(M, tm), _pick(N, tn), _pick(K, tk)
    nk = K // tk
    lead = () if layer is None else (layer,)
    lead_blk = () if layer is None else (None,)
    if mode == "nn":
        a_spec = pl.BlockSpec((tm, tk), lambda i, j, k: (i, k))
        if stacked:
            b_spec = pl.BlockSpec(lead_blk + (None, tk, tn), lambda i, j, k: lead + (j, k, 0))
        else:
            b_spec = pl.BlockSpec(lead_blk + (tk, tn), lambda i, j, k: lead + (k, j))
        dot = _nn
    elif mode == "nt":
        a_spec = pl.BlockSpec((tm, tk), lambda i, j, k: (i, k))
        if stacked:
            b_spec = pl.BlockSpec(lead_blk + (None, tn, tk), lambda i, j, k: lead + (k, j, 0))
        else:
            b_spec = pl.BlockSpec(lead_blk + (tn, tk), lambda i, j, k: lead + (j, k))
        dot = _nt
    else:
        a_spec = pl.BlockSpec((tk, tm), lambda i, j, k: (k, i))
        b_spec = pl.BlockSpec((tk, tn), lambda i, j, k: (k, j))
        dot = _tn
    if out_stacked:
        o_spec = pl.BlockSpec((None, tm, tn), lambda i, j, k: (j, i, 0))
        o_shape = jax.ShapeDtypeStruct((4, M, tn), out_dtype)
    else:
        o_spec = pl.BlockSpec((tm, tn), lambda i, j, k: (i, j))
        o_shape = jax.ShapeDtypeStruct((M, N), out_dtype)

    def body_single(a_ref, b_ref, o_ref):
        o_ref[...] = dot(a_ref[...], b_ref[...]).astype(out_dtype)

    def body(a_ref, b_ref, o_ref, acc_ref):
        k = pl.program_id(2)

        @pl.when(k == 0)
        def _():
            acc_ref[...] = jnp.zeros_like(acc_ref)

        acc_ref[...] += dot(a_ref[...], b_ref[...])

        @pl.when(k == nk - 1)
        def _():
            o_ref[...] = acc_ref[...].astype(out_dtype)

    return pl.pallas_call(
        body_single if nk == 1 else body, name=name, grid=(M // tm, N // tn, nk), in_specs=[a_spec, b_spec],
        out_specs=o_spec, out_shape=o_shape, scratch_shapes=[] if nk == 1 else [pltpu.VMEM((tm, tn), F32)],
        compiler_params=pltpu.CompilerParams(dimension_semantics=("parallel", "parallel", "arbitrary")),
    )(a, b)


def _rows(tm, n, col=0):
    return pl.BlockSpec((tm, n), lambda i: (i, col))


def _vec(n):
    return pl.BlockSpec((1, n), lambda i: (0, 0))


def _lnmod_fwd(x, g, sc, sh, name):
    S, D = x.shape
    tm = _pick(S, 512, 8)

    def body(x_ref, g_ref, sc_ref, sh_ref, o_ref):
        xv = x_ref[...]
        r = lax.rsqrt(jnp.mean(xv * xv, axis=-1, keepdims=True) + EPS)
        o_ref[...] = ((xv * r * g_ref[...]) * (1.0 + sc_ref[...]) + sh_ref[...]).astype(BF16)

    return pl.pallas_call(
        body, name=name, grid=(S // tm,),
        in_specs=[_rows(tm, D), _vec(D), _vec(D), _vec(D)], out_specs=_rows(tm, D),
        out_shape=jax.ShapeDtypeStruct((S, D), BF16), compiler_params=_par(1),
    )(x, g, sc, sh)


def _lnmod_bwd(dh, x, g, sc, dres, name):
    S, D = x.shape
    tm = _pick(S, 512, 8)

    def body(dh_ref, x_ref, g_ref, sc_ref, dres_ref, dx_ref, dsh_ref, dsc_ref, dg_ref):
        i = pl.program_id(0)
        xv = x_ref[...]
        dh_ = dh_ref[...]
        r = lax.rsqrt(jnp.mean(xv * xv, axis=-1, keepdims=True) + EPS)
        xhat = xv * r
        gv = g_ref[...]
        dn = dh_ * (1.0 + sc_ref[...])
        dxhat = dn * gv
        dx_ref[...] = dres_ref[...] + r * (dxhat - xhat * jnp.mean(dxhat * xhat, axis=-1, keepdims=True))
        _acc(dsh_ref, jnp.sum(dh_, axis=0, keepdims=True), i)
        _acc(dsc_ref, jnp.sum(dh_ * (xhat * gv), axis=0, keepdims=True), i)
        _acc(dg_ref, jnp.sum(dn * xhat, axis=0, keepdims=True), i)

    return pl.pallas_call(
        body, name=name, grid=(S // tm,),
        in_specs=[_rows(tm, D), _rows(tm, D), _vec(D), _vec(D), _rows(tm, D)],
        out_specs=[_rows(tm, D), _vec(D), _vec(D), _vec(D)],
        out_shape=[jax.ShapeDtypeStruct((S, D), F32)] + [jax.ShapeDtypeStruct((1, D), F32)] * 3,
        compiler_params=_arb(1),
    )(dh, x, g, sc, dres)


def _gate_fwd(x, y, gt, name):
    S, D = x.shape
    tm = _pick(S, 512, 8)

    def body(x_ref, y_ref, gt_ref, o_ref):
        o_ref[...] = x_ref[...] + gt_ref[...] * y_ref[...]

    return pl.pallas_call(
        body, name=name, grid=(S // tm,), in_specs=[_rows(tm, D), _rows(tm, D), _vec(D)], out_specs=_rows(tm, D),
        out_shape=jax.ShapeDtypeStruct((S, D), F32), compiler_params=_par(1),
    )(x, y, gt)


def _gate_bwd(dx, y, gt, name):
    S, D = dx.shape
    tm = _pick(S, 512, 8)

    def body(dx_ref, y_ref, gt_ref, dz_ref, dgt_ref):
        i = pl.program_id(0)
        d = dx_ref[...]
        dz_ref[...] = (d * gt_ref[...]).astype(BF16)
        _acc(dgt_ref, jnp.sum(d * y_ref[...], axis=0, keepdims=True), i)

    return pl.pallas_call(
        body, name=name, grid=(S // tm,), in_specs=[_rows(tm, D), _rows(tm, D), _vec(D)],
        out_specs=[_rows(tm, D), _vec(D)],
        out_shape=[jax.ShapeDtypeStruct((S, D), BF16), jax.ShapeDtypeStruct((1, D), F32)],
        compiler_params=_arb(1),
    )(dx, y, gt)


def _ffn_act_fwd(gu, name):
    S, H2 = gu.shape
    H = H2 // 2
    tm = _pick(S, 256, 8)

    def body(g_ref, u_ref, o_ref):
        gv = g_ref[...]
        o_ref[...] = (gv * _sigmoid(gv) * u_ref[...]).astype(BF16)

    return pl.pallas_call(
        body, name=name, grid=(S // tm,), in_specs=[_rows(tm, H, 0), _rows(tm, H, 1)], out_specs=_rows(tm, H),
        out_shape=jax.ShapeDtypeStruct((S, H), BF16), compiler_params=_par(1),
    )(gu, gu)


def _ffn_act_bwd(gu, dact, name):
    S, H2 = gu.shape
    H = H2 // 2
    tm = _pick(S, 256, 8)

    def body(g_ref, u_ref, da_ref, o_ref):
        gv = g_ref[...]
        s = _sigmoid(gv)
        da = da_ref[...]
        o_ref[:, :H] = (da * u_ref[...] * (s * (1.0 + gv * (1.0 - s)))).astype(BF16)
        o_ref[:, H:] = (da * (gv * s)).astype(BF16)

    return pl.pallas_call(
        body, name=name, grid=(S // tm,), in_specs=[_rows(tm, H, 0), _rows(tm, H, 1), _rows(tm, H)],
        out_specs=_rows(tm, H2), out_shape=jax.ShapeDtypeStruct((S, H2), BF16), compiler_params=_par(1),
    )(gu, gu, dact)


def _merge_fwd(proj, pa, pb, name):
    S, D = pa.shape
    tm = _pick(S, 512, 8)

    def body(ga_ref, gb_ref, pa_ref, pb_ref, o_ref):
        o_ref[...] = (_sigmoid(ga_ref[...]) * pa_ref[...] + _sigmoid(gb_ref[...]) * pb_ref[...]).astype(BF16)

    return pl.pallas_call(
        body, name=name, grid=(S // tm,),
        in_specs=[_rows(tm, D, P_GA // D), _rows(tm, D, P_GB // D), _rows(tm, D), _rows(tm, D)],
        out_specs=_rows(tm, D), out_shape=jax.ShapeDtypeStruct((S, D), BF16), compiler_params=_par(1),
    )(proj, proj, pa, pb)


def _merge_bwd(proj, pa, pb, dm, name):
    S, D = pa.shape
    tm = _pick(S, 512, 8)

    def body(ga_ref, gb_ref, pa_ref, pb_ref, dm_ref, dg_ref, dpa_ref, dpb_ref):
        d = dm_ref[...]
        sa = _sigmoid(ga_ref[...])
        sb = _sigmoid(gb_ref[...])
        dg_ref[:, :D] = (d * pa_ref[...] * sa * (1.0 - sa)).astype(BF16)
        dg_ref[:, D:] = (d * pb_ref[...] * sb * (1.0 - sb)).astype(BF16)
        dpa_ref[...] = (d * sa).astype(BF16)
        dpb_ref[...] = (d * sb).astype(BF16)

    return pl.pallas_call(
        body, name=name, grid=(S // tm,),
        in_specs=[_rows(tm, D, P_GA // D), _rows(tm, D, P_GB // D), _rows(tm, D), _rows(tm, D), _rows(tm, D)],
        out_specs=[_rows(tm, 2 * D), _rows(tm, D), _rows(tm, D)],
        out_shape=[jax.ShapeDtypeStruct((S, 2 * D), BF16), jax.ShapeDtypeStruct((S, D), BF16),
                   jax.ShapeDtypeStruct((S, D), BF16)],
        compiler_params=_par(1),
    )(proj, proj, pa, pb, dm)


def _loss_head(x, g, target, name):
    S, D = x.shape
    tm = _pick(S, 512, 8)

    def body(x_ref, g_ref, t_ref, dx_ref, loss_ref, dg_ref):
        i = pl.program_id(0)
        xv = x_ref[...]
        gv = g_ref[...]
        r = lax.rsqrt(jnp.mean(xv * xv, axis=-1, keepdims=True) + EPS)
        xhat = xv * r
        err = xhat * gv - t_ref[...]
        part = 0.5 * jnp.sum(jnp.mean(err * err, axis=-1, keepdims=True), axis=0, keepdims=True)
        _acc(loss_ref, jnp.broadcast_to(part, (1, LANE)), i)
        dy = err * (1.0 / D)
        _acc(dg_ref, jnp.sum(dy * xhat, axis=0, keepdims=True), i)
        dxhat = dy * gv
        dx_ref[...] = r * (dxhat - xhat * jnp.mean(dxhat * xhat, axis=-1, keepdims=True))

    return pl.pallas_call(
        body, name=name, grid=(S // tm,), in_specs=[_rows(tm, D), _vec(D), _rows(tm, D)],
        out_specs=[_rows(tm, D), _vec(LANE), _vec(D)],
        out_shape=[jax.ShapeDtypeStruct((S, D), F32), jax.ShapeDtypeStruct((1, LANE), F32),
                   jax.ShapeDtypeStruct((1, D), F32)],
        compiler_params=_arb(1),
    )(x, g, target)


HEADS_PER_SLAB = LANE // A_DH
N_SLABS = A_HEADS // HEADS_PER_SLAB
SPAN_BLOCKS = KSPAN // QBLK


def _attn_specs(seg):
    q_spec = pl.BlockSpec((QBLK, LANE), lambda p, m: (m, seg[0] * N_SLABS + p))
    k_specs = [pl.BlockSpec((QBLK, LANE), functools.partial(
        lambda j, p, m: (jnp.maximum(m - (SPAN_BLOCKS - 1) + j, 0), seg[1] * N_SLABS + p), j)) for j in range(SPAN_BLOCKS)]
    v_specs = [pl.BlockSpec((QBLK, LANE), functools.partial(
        lambda j, p, m: (jnp.maximum(m - (SPAN_BLOCKS - 1) + j, 0), seg[2] * N_SLABS + p), j)) for j in range(SPAN_BLOCKS)]
    b_spec = pl.BlockSpec((HEADS_PER_SLAB, QBLK, KSPAN), lambda p, m: (p, 0, 0))
    return q_spec, k_specs, v_specs, b_spec


def _head_lanes(t, hh):
    lane = lax.broadcasted_iota(jnp.int32, t.shape, 1)
    return jnp.where((lane // A_DH) == hh, t, jnp.zeros_like(t))


def _attn_probs(qh, k, bias, m):
    s = _nt(qh, k) * (A_DH ** -0.5) + bias
    key_pos = lax.broadcasted_iota(jnp.int32, (QBLK, KSPAN), 1) + (m - (SPAN_BLOCKS - 1)) * QBLK
    s = jnp.where(key_pos >= 0, s, NEG)
    p = jnp.exp(s - jnp.max(s, axis=-1, keepdims=True))
    return p / jnp.sum(p, axis=-1, keepdims=True)


def _grid_ends(nq):
    p, m = pl.program_id(0), pl.program_id(1)
    return (p == 0) & (m == 0), (p == N_SLABS // 2) & (m == 0), (p == N_SLABS - 1) & (m == nq - 1)


def _attn_fwd(proj, big, name, gather=None):
    S = proj.shape[0]
    q_spec, k_specs, v_specs, b_spec = _attn_specs((0, 1, 2))
    shards, layer = gather if gather is not None else ((), None)
    ng = len(shards)

    def body(q_ref, k0, k1, k2, v0, v1, v2, b_ref, *rest):
        srcs, o_ref, gouts, sems = rest[:ng], rest[ng], rest[ng + 1:2 * ng + 1], rest[2 * ng + 1:]
        done = _carry(_gather_phases(layer, srcs, gouts, *sems), *_grid_ends(S // QBLK)) if ng else None
        m = pl.program_id(1)
        q = q_ref[...].astype(BF16)
        k = jnp.concatenate([k0[...], k1[...], k2[...]], axis=0).astype(BF16)
        v = jnp.concatenate([v0[...], v1[...], v2[...]], axis=0).astype(BF16)
        outs = [_nn(_attn_probs(_head_lanes(q, hh), k, b_ref[hh], m).astype(BF16), v) for hh in range(HEADS_PER_SLAB)]
        lane = lax.broadcasted_iota(jnp.int32, (QBLK, LANE), 1)
        o_ref[...] = jnp.where(lane < A_DH, outs[0], outs[1]).astype(BF16)
        if ng:
            done()

    res = pl.pallas_call(
        body, name=name, grid=(N_SLABS, S // QBLK), in_specs=[q_spec] + k_specs + v_specs + [b_spec] + [HBM] * ng,
        out_specs=[pl.BlockSpec((QBLK, LANE), lambda p, m: (m, p))] + [HBM] * ng,
        out_shape=[jax.ShapeDtypeStruct((S, A_HEADS * A_DH), BF16)] + _gather_out_shapes(shards),
        scratch_shapes=_gather_scratch(ng) if ng else [], compiler_params=_arb(2),
    )(proj, proj, proj, proj, proj, proj, proj, big, *shards)
    return res[0], list(res[1:])


def _attn_bwd(proj, big, dya, name, exchange=()):
    S = proj.shape[0]
    W = A_HEADS * A_DH
    q_spec, k_specs, v_specs, b_spec = _attn_specs((0, 1, 2))
    out_q = pl.BlockSpec((QBLK, LANE), lambda p, m: (m, p))
    out_kv = pl.BlockSpec((S, LANE), lambda p, m: (0, p))
    ne = len(exchange)

    def body(q_ref, k0, k1, k2, v0, v1, v2, b_ref, do_ref, *rest):
        srcs, (dq_ref, dk_ref, dv_ref, db_ref), eouts, sems = rest[:ne], rest[ne:ne + 4], rest[ne + 4:2 * ne + 4], rest[2 * ne + 4:]
        done = _carry(_chips_phases(srcs, eouts, *sems), *_grid_ends(S // QBLK)) if ne else None
        m = pl.program_id(1)

        @pl.when(m == 0)
        def _():
            dk_ref[...] = jnp.zeros_like(dk_ref)
            dv_ref[...] = jnp.zeros_like(dv_ref)
            db_ref[...] = jnp.zeros_like(db_ref)

        q = q_ref[...].astype(BF16)
        k = jnp.concatenate([k0[...], k1[...], k2[...]], axis=0).astype(BF16)
        v = jnp.concatenate([v0[...], v1[...], v2[...]], axis=0).astype(BF16)
        do = do_ref[...]
        dqs = []
        dk = jnp.zeros((KSPAN, LANE), F32)
        dv = jnp.zeros((KSPAN, LANE), F32)
        for hh in range(HEADS_PER_SLAB):
            qh = _head_lanes(q, hh)
            doh = _head_lanes(do, hh)
            p = _attn_probs(qh, k, b_ref[hh], m)
            dp = _nt(doh, v)
            ds = p * (dp - jnp.sum(p * dp, axis=-1, keepdims=True))
            db_ref[hh] += ds
            dsb = (ds * (A_DH ** -0.5)).astype(BF16)
            dqs.append(_nn(dsb, k))
            dk = dk + _tn(dsb, qh)
            dv = dv + _tn(p.astype(BF16), doh)
        lane = lax.broadcasted_iota(jnp.int32, (QBLK, LANE), 1)
        dq_ref[...] = jnp.where(lane < A_DH, dqs[0], dqs[1])
        for j in range(SPAN_BLOCKS):
            blk = m - (SPAN_BLOCKS - 1) + j

            @pl.when(blk >= 0)
            def _():
                off = pl.multiple_of(blk * QBLK, QBLK)
                dk_ref[pl.ds(off, QBLK), :] += dk[j * QBLK:(j + 1) * QBLK]
                dv_ref[pl.ds(off, QBLK), :] += dv[j * QBLK:(j + 1) * QBLK]
        if ne:
            done()

    res = pl.pallas_call(
        body, name=name, grid=(N_SLABS, S // QBLK),
        in_specs=[q_spec] + k_specs + v_specs + [b_spec, pl.BlockSpec((QBLK, LANE), lambda p, m: (m, p))] + [HBM] * ne,
        out_specs=[out_q, out_kv, out_kv, b_spec] + [HBM] * ne,
        out_shape=[jax.ShapeDtypeStruct((S, W), F32)] * 3 + [jax.ShapeDtypeStruct((A_HEADS, QBLK, KSPAN), F32)]
        + _chips_out_shapes(exchange),
        scratch_shapes=_chips_scratch(ne) if ne else [], compiler_params=_arb(2),
    )(proj, proj, proj, proj, proj, proj, proj, big, dya, *exchange)
    return tuple(res[:4]) + (list(res[4:]),)


NREL_PAD = 3 * LANE
SKEW_W = 1024


def _rel_table_grad(dbig, name):
    H, R, C = dbig.shape

    def body(d_ref, o_ref):
        x = jnp.concatenate([d_ref[...], jnp.zeros((R, SKEW_W - C), F32)], axis=1)
        row = lax.broadcasted_iota(jnp.int32, (R, SKEW_W), 0)
        for b in range(R.bit_length() - 1):
            x = jnp.where(((row >> b) & 1) == 1, pltpu.roll(x, SKEW_W - (1 << b), 1), x)
        e = jnp.sum(x, axis=0, keepdims=True)
        xi = lax.broadcasted_iota(jnp.int32, (SKEW_W, NREL_PAD), 0)
        r = lax.broadcasted_iota(jnp.int32, (SKEW_W, NREL_PAD), 1)
        diag = jnp.where(xi < C, xi, xi - SKEW_W)
        rel = jnp.clip(A_PAST * CHUNK - diag, -A_MAX_REL, A_MAX_REL) + A_MAX_REL
        o_ref[...] = _nn(e, jnp.where(rel == r, 1.0, 0.0).astype(F32), HI)

    return pl.pallas_call(
        body, name=name, grid=(H,), in_specs=[pl.BlockSpec((None, R, C), lambda h: (h, 0, 0))],
        out_specs=pl.BlockSpec((None, 1, NREL_PAD), lambda h: (h, 0, 0)),
        out_shape=jax.ShapeDtypeStruct((H, 1, NREL_PAD), F32), compiler_params=_par(1),
    )(dbig)


def _chunk_cumsum_matrix(n, reverse):
    j = lax.broadcasted_iota(jnp.int32, (n, n), 0)
    i = lax.broadcasted_iota(jnp.int32, (n, n), 1)
    same = (j // CHUNK) == (i // CHUNK)
    return jnp.where(same & ((j >= i) if reverse else (j <= i)), 1.0, 0.0).astype(F32)


def _gdn_gates_fwd(b_t, a_t, alog, dtb, name):
    Hh, S = b_t.shape
    tl = _pick(S, 512)
    row = pl.BlockSpec((Hh, tl), lambda i: (0, i))
    col = pl.BlockSpec((Hh, 1), lambda i: (0, 0))

    def body(b_ref, a_ref, al_ref, dt_ref, beta_ref, gam_ref):
        z = a_ref[...] + dt_ref[...]
        sp = jnp.maximum(z, 0.0) + jnp.log(1.0 + jnp.exp(-jnp.abs(z)))
        g = -jnp.exp(al_ref[...]) * sp
        beta_ref[...] = _sigmoid(b_ref[...])
        gam_ref[...] = _nn(g, _chunk_cumsum_matrix(tl, False), HI)

    return pl.pallas_call(
        body, name=name, grid=(S // tl,), in_specs=[row, row, col, col], out_specs=[row, row],
        out_shape=[jax.ShapeDtypeStruct((Hh, S), F32)] * 2, compiler_params=_par(1),
    )(b_t, a_t, alog, dtb)


def _gdn_gates_bwd(dbeta, dgam_a, dgam_b, b_t, a_t, alog, dtb, name):
    Hh, S = b_t.shape
    tl = _pick(S, 512)
    row = pl.BlockSpec((Hh, tl), lambda i: (0, i))
    col = pl.BlockSpec((Hh, 1), lambda i: (0, 0))
    accs = pl.BlockSpec((Hh, LANE), lambda i: (0, 0))

    def body(dbeta_ref, dga_ref, dgb_ref, b_ref, a_ref, al_ref, dt_ref, db_ref, da_ref, dal_ref, ddt_ref):
        i = pl.program_id(0)
        z = a_ref[...] + dt_ref[...]
        sp = jnp.maximum(z, 0.0) + jnp.log(1.0 + jnp.exp(-jnp.abs(z)))
        ea = jnp.exp(al_ref[...])
        dg = _nn(dga_ref[...] + dgb_ref[...], _chunk_cumsum_matrix(tl, True), HI)
        da = dg * (-ea) * _sigmoid(z)
        beta = _sigmoid(b_ref[...])
        db_ref[...] = dbeta_ref[...] * beta * (1.0 - beta)
        da_ref[...] = da
        _acc(dal_ref, jnp.broadcast_to(jnp.sum(dg * (-ea * sp), axis=1, keepdims=True), (Hh, LANE)), i)
        _acc(ddt_ref, jnp.broadcast_to(jnp.sum(da, axis=1, keepdims=True), (Hh, LANE)), i)

    return pl.pallas_call(
        body, name=name, grid=(S // tl,), in_specs=[row] * 5 + [col, col], out_specs=[row, row, accs, accs],
        out_shape=[jax.ShapeDtypeStruct((Hh, S), F32)] * 2 + [jax.ShapeDtypeStruct((Hh, LANE), F32)] * 2,
        compiler_params=_arb(1),
    )(dbeta, dgam_a, dgam_b, b_t, a_t, alog, dtb)


HALO = 8


def _conv_silu(xx_ref, w_ref, tm):
    y = w_ref[0:1, :] * xx_ref[pl.ds(HALO - CONV_K + 1, tm), :]
    for j in range(1, CONV_K):
        y = y + w_ref[j:j + 1, :] * xx_ref[pl.ds(HALO - CONV_K + 1 + j, tm), :]
    return y, y * _sigmoid(y)


def _fill_prev_halo(xx_ref, x_ref, prev_ref, i, tm):
    xx_ref[pl.ds(HALO, tm), :] = x_ref[...]

    @pl.when(i == 0)
    def _():
        xx_ref[pl.ds(0, HALO), :] = jnp.zeros((HALO, xx_ref.shape[1]), F32)

    @pl.when(i != 0)
    def _():
        xx_ref[pl.ds(0, HALO), :] = prev_ref[...]


def _gdn_pre_specs(tm, C, colblk):
    cur = pl.BlockSpec((tm, C), lambda i: (i, colblk))
    prev = pl.BlockSpec((HALO, C), lambda i: (jnp.maximum(i * (tm // HALO) - 1, 0), colblk))
    return cur, prev


def _gdn_pre_fwd(proj, wconv, name):
    S = proj.shape[0]
    C = 3 * B_HEADS * B_DH
    W = B_HEADS * B_DH
    tm = _pick(S, 256, 8)
    cur, prev = _gdn_pre_specs(tm, C, P_QKVB // C)

    def body(x_ref, prev_ref, w_ref, q_ref, k_ref, v_ref, xx_ref):
        i = pl.program_id(0)
        _fill_prev_halo(xx_ref, x_ref, prev_ref, i, tm)
        _, sl = _conv_silu(xx_ref, w_ref, tm)
        for h in range(B_HEADS):
            hs = slice(h * B_DH, (h + 1) * B_DH)
            q = sl[:, h * B_DH:(h + 1) * B_DH]
            k = sl[:, W + h * B_DH:W + (h + 1) * B_DH]
            q_ref[:, hs] = q * (lax.rsqrt(jnp.sum(q * q, axis=-1, keepdims=True) + EPS) * (B_DH ** -0.5))
            k_ref[:, hs] = k * lax.rsqrt(jnp.sum(k * k, axis=-1, keepdims=True) + EPS)
        v_ref[...] = sl[:, 2 * W:]

    return pl.pallas_call(
        body, name=name, grid=(S // tm,), in_specs=[cur, prev, pl.BlockSpec((CONV_K, C), lambda i: (0, 0))],
        out_specs=[_rows(tm, W)] * 3, out_shape=[jax.ShapeDtypeStruct((S, W), F32)] * 3,
        scratch_shapes=[pltpu.VMEM((HALO + tm, C), F32)], compiler_params=_par(1),
    )(proj, proj, wconv)


def _gdn_pre_bwd_a(proj, wconv, dqn, dkn, dv, name):
    S = proj.shape[0]
    C = 3 * B_HEADS * B_DH
    W = B_HEADS * B_DH
    tm = _pick(S, 256, 8)
    cur, prev = _gdn_pre_specs(tm, C, P_QKVB // C)

    def body(x_ref, prev_ref, w_ref, dq_ref, dk_ref, dv_ref, dy_ref, xx_ref):
        i = pl.program_id(0)
        _fill_prev_halo(xx_ref, x_ref, prev_ref, i, tm)
        y, sl = _conv_silu(xx_ref, w_ref, tm)
        sg = _sigmoid(y)
        dsilu = sg * (1.0 + y * (1.0 - sg))
        for h in range(B_HEADS):
            for base, d_ref, c in ((0, dq_ref, B_DH ** -0.5), (W, dk_ref, 1.0)):
                lo = base + h * B_DH
                t = sl[:, lo:lo + B_DH]
                d = d_ref[:, h * B_DH:(h + 1) * B_DH]
                r = lax.rsqrt(jnp.sum(t * t, axis=-1, keepdims=True) + EPS)
                dt = (c * r) * (d - t * (r * r) * jnp.sum(d * t, axis=-1, keepdims=True))
                dy_ref[:, lo:lo + B_DH] = dt * dsilu[:, lo:lo + B_DH]
        dy_ref[:, 2 * W:] = dv_ref[...] * dsilu[:, 2 * W:]

    return pl.pallas_call(
        body, name=name, grid=(S // tm,),
        in_specs=[cur, prev, pl.BlockSpec((CONV_K, C), lambda i: (0, 0))] + [_rows(tm, W)] * 3,
        out_specs=_rows(tm, C), out_shape=jax.ShapeDtypeStruct((S, C), F32),
        scratch_shapes=[pltpu.VMEM((HALO + tm, C), F32)], compiler_params=_par(1),
    )(proj, proj, wconv, dqn, dkn, dv)


def _gdn_pre_bwd_b(proj, wconv, dy, name):
    S = proj.shape[0]
    C = 3 * B_HEADS * B_DH
    tm = _pick(S, 256, 8)
    nt_ = S // tm
    cur, prev = _gdn_pre_specs(tm, C, P_QKVB // C)
    nxt = pl.BlockSpec((HALO, C), lambda i: (jnp.minimum((i + 1) * (tm // HALO), S // HALO - 1), 0))

    def body(x_ref, prev_ref, w_ref, dy_ref, next_ref, dx_ref, dw_ref, xx_ref, dd_ref):
        i = pl.program_id(0)
        _fill_prev_halo(xx_ref, x_ref, prev_ref, i, tm)
        dyv = dy_ref[...]
        dd_ref[pl.ds(0, tm), :] = dyv

        @pl.when(i == nt_ - 1)
        def _():
            dd_ref[pl.ds(tm, HALO), :] = jnp.zeros((HALO, C), F32)

        @pl.when(i != nt_ - 1)
        def _():
            dd_ref[pl.ds(tm, HALO), :] = next_ref[...]

        dx = w_ref[0:1, :] * dd_ref[pl.ds(CONV_K - 1, tm), :]
        for j in range(1, CONV_K):
            dx = dx + w_ref[j:j + 1, :] * dd_ref[pl.ds(CONV_K - 1 - j, tm), :]
        dx_ref[...] = dx.astype(BF16)
        dw = jnp.concatenate(
            [jnp.sum(dyv * xx_ref[pl.ds(HALO - CONV_K + 1 + j, tm), :], axis=0, keepdims=True) for j in range(CONV_K)],
            axis=0)
        _acc(dw_ref, dw, i)

    return pl.pallas_call(
        body, name=name, grid=(nt_,),
        in_specs=[cur, prev, pl.BlockSpec((CONV_K, C), lambda i: (0, 0)), _rows(tm, C), nxt],
        out_specs=[_rows(tm, C), pl.BlockSpec((CONV_K, C), lambda i: (0, 0))],
        out_shape=[jax.ShapeDtypeStruct((S, C), BF16), jax.ShapeDtypeStruct((CONV_K, C), F32)],
        scratch_shapes=[pltpu.VMEM((HALO + tm, C), F32), pltpu.VMEM((tm + HALO, C), F32)],
        compiler_params=_arb(1),
    )(proj, proj, wconv, dy, dy)


def _chunk_masks():
    row = lax.broadcasted_iota(jnp.int32, (CHUNK, CHUNK), 0)
    col = lax.broadcasted_iota(jnp.int32, (CHUNK, CHUNK), 1)
    return row >= col, row > col


def _chunk_local(q, k, vv, bc, gc, gr, tri):
    dm = jnp.where(tri, jnp.exp(jnp.where(tri, gc - gr, 0.0)), 0.0)
    kk = _bnt(k, k)
    glast = gr[..., CHUNK - 1:CHUNK]
    ep = jnp.exp(gc)
    em = jnp.exp(glast - gc)
    el = jnp.exp(glast)
    return dm, kk, ep, em, el, vv * bc, k * (bc * ep)


def _unit_lower_inverse(low):
    row = lax.broadcasted_iota(jnp.int32, (CHUNK, CHUNK), 0)
    col = lax.broadcasted_iota(jnp.int32, (CHUNK, CHUNK), 1)
    p = -low
    t = jnp.where(row == col, 1.0, 0.0).astype(F32) + p
    steps = CHUNK.bit_length() - 2
    for _ in range(steps):
        p = _nn(p, p, SOLVE_PREC)
        t = t + _nn(t, p, SOLVE_PREC)
    return t


GROUP = 4


def _carry(phases, first, middle, last):
    if len(phases) == 3:
        pl.when(first)(phases[0])
        pl.when(middle)(phases[1])
        return lambda: pl.when(last)(phases[2])
    pl.when(first)(phases[0])
    return lambda: pl.when(last)(phases[1])


def _pairs(nchunks):
    return [(c, h) for c in range(nchunks) for h in range(B_HEADS)]


def _tok(c):
    return slice(c * CHUNK, (c + 1) * CHUNK)


def _head(h):
    return slice(h * B_DH, (h + 1) * B_DH)


def _stack_tokens(ref, nchunks):
    return jnp.stack([ref[_tok(c), _head(h)] for c, h in _pairs(nchunks)])


def _stack_cols(ref, nchunks):
    per_chunk = [ref[c] for c in range(nchunks)] if len(ref.shape) == 3 else [ref[...]]
    return jnp.stack([per_chunk[c][:, h:h + 1] for c, h in _pairs(nchunks)])


def _stack_rows(ref, nchunks):
    if len(ref.shape) == 3:
        return jnp.stack([ref[c, h:h + 1, :] for c, h in _pairs(nchunks)])
    return jnp.stack([ref[h:h + 1, :] for _, h in _pairs(1)])


def _gdn_group_specs(ng_steps, W):
    tok = pl.BlockSpec((GROUP * CHUNK, W), lambda i: (i, 0))
    colv = pl.BlockSpec((GROUP, CHUNK, B_HEADS), lambda i: (i, 0, 0))
    rowv = pl.BlockSpec((GROUP, B_HEADS, CHUNK), lambda i: (i, 0, 0))
    mat = pl.BlockSpec((GROUP, B_HEADS, CHUNK, CHUNK), lambda i: (i, 0, 0, 0))
    return tok, colv, rowv, mat


def _gdn_local_fwd(qn, kn, v, bcol, gcol, grow, name, gather=None):
    S, Wd = qn.shape
    nc = S // CHUNK
    steps = nc // GROUP
    tok, colv, rowv, mat = _gdn_group_specs(steps, Wd)
    shards, layer = gather if gather is not None else ((), None)
    ng = len(shards)

    def body(q_ref, k_ref, v_ref, bc_ref, gc_ref, gr_ref, *rest):
        srcs, (t_ref, a_ref, u_ref, w_ref), gouts, sems = rest[:ng], rest[ng:ng + 4], rest[ng + 4:2 * ng + 4], rest[2 * ng + 4:]
        i = pl.program_id(0)
        done = _carry(_gather_phases(layer, srcs, gouts, *sems), i == 0, i == steps // 2, i == steps - 1) if ng else None
        tri, strict = _chunk_masks()
        q, k, vv = (_stack_tokens(r, GROUP) for r in (q_ref, k_ref, v_ref))
        bc, gc, gr = _stack_cols(bc_ref, GROUP), _stack_cols(gc_ref, GROUP), _stack_rows(gr_ref, GROUP)
        dm, kk, ep, em, el, vb, kb = _chunk_local(q, k, vv, bc, gc, gr, tri)
        t = _unit_lower_inverse(jnp.where(strict, bc * kk * dm, 0.0))
        a = _bnt(q, k) * dm
        u = _nn(t, vb, SOLVE_PREC)
        w = _nn(t, kb, SOLVE_PREC)
        for n, (c, h) in enumerate(_pairs(GROUP)):
            t_ref[c, h] = t[n]
            a_ref[c, h] = a[n]
            u_ref[_tok(c), _head(h)] = u[n]
            w_ref[_tok(c), _head(h)] = w[n]
        if ng:
            done()

    res = pl.pallas_call(
        body, name=name, grid=(steps,), in_specs=[tok, tok, tok, colv, colv, rowv] + [HBM] * ng,
        out_specs=[mat, mat, tok, tok] + [HBM] * ng,
        out_shape=[jax.ShapeDtypeStruct((nc, B_HEADS, CHUNK, CHUNK), F32)] * 2 + [jax.ShapeDtypeStruct((S, Wd), F32)] * 2
        + _gather_out_shapes(shards),
        scratch_shapes=_gather_scratch(ng) if ng else [], compiler_params=_arb(1),
    )(qn, kn, v, bcol, gcol, grow, *shards)
    return res[0], res[1], res[2], res[3], list(res[4:])


def _scan_decays(gc, gr):
    glast = gr[..., CHUNK - 1:CHUNK]
    return jnp.exp(gc), jnp.exp(glast - gc), jnp.exp(glast)


def _gdn_scan_specs(nc, rev):
    idx = (lambda i: nc - 1 - i) if rev else (lambda i: i)
    W = B_HEADS * B_DH
    tok = pl.BlockSpec((CHUNK, W), lambda i: (idx(i), 0))
    colv = pl.BlockSpec((None, CHUNK, B_HEADS), lambda i: (idx(i), 0, 0))
    rowv = pl.BlockSpec((None, B_HEADS, CHUNK), lambda i: (idx(i), 0, 0))
    mat = pl.BlockSpec((None, B_HEADS, CHUNK, CHUNK), lambda i: (idx(i), 0, 0, 0))
    smat = pl.BlockSpec((None, B_HEADS, B_DH, B_DH), lambda i: (idx(i), 0, 0, 0))
    return tok, colv, rowv, mat, smat


def _gdn_scan_fwd(qn, kn, u, w, a, gcol, grow, name, gather=None):
    S, Wd = qn.shape
    nc = S // CHUNK
    tok, colv, rowv, mat, smat = _gdn_scan_specs(nc, False)
    shards, layer = gather if gather is not None else ((), None)
    ng = len(shards)

    def body(q_ref, k_ref, u_ref, w_ref, a_ref, gc_ref, gr_ref, *rest):
        srcs, (o_ref, sh_ref), gouts = rest[:ng], rest[ng:ng + 2], rest[ng + 2:2 * ng + 2]
        st_ref, sems = rest[2 * ng + 2], rest[2 * ng + 3:]
        i = pl.program_id(0)
        done = _carry(_gather_phases(layer, srcs, gouts, *sems), i == 0, i == nc // 2, i == nc - 1) if ng else None

        @pl.when(i == 0)
        def _():
            st_ref[...] = jnp.zeros_like(st_ref)

        ep, em, el = _scan_decays(_stack_cols(gc_ref, 1), _stack_rows(gr_ref, 1))
        q, k, u, w = (_stack_tokens(r, 1) for r in (q_ref, k_ref, u_ref, w_ref))
        s0 = st_ref[...]
        ut = u - _bnn(w, s0)
        o = _bnn(q * ep, s0) + _bnn(a_ref[...], ut)
        st_ref[...] = el * s0 + _btn(k * em, ut)
        sh_ref[...] = s0
        for h in range(B_HEADS):
            o_ref[:, _head(h)] = o[h]
        if ng:
            done()

    res = pl.pallas_call(
        body, name=name, grid=(nc,), in_specs=[tok, tok, tok, tok, mat, colv, rowv] + [HBM] * ng,
        out_specs=[tok, smat] + [HBM] * ng,
        out_shape=[jax.ShapeDtypeStruct((S, Wd), F32), jax.ShapeDtypeStruct((nc, B_HEADS, B_DH, B_DH), F32)]
        + _gather_out_shapes(shards),
        scratch_shapes=[pltpu.VMEM((B_HEADS, B_DH, B_DH), F32)] + (_gather_scratch(ng) if ng else []),
        compiler_params=_arb(1),
    )(qn, kn, u, w, a, gcol, grow, *shards)
    return res[0], res[1], list(res[2:])


def _gdn_scan_bwd(qn, kn, u, w, a, gcol, grow, ssave, do, name):
    S, Wd = qn.shape
    nc = S // CHUNK
    tok, colv, rowv, mat, smat = _gdn_scan_specs(nc, True)

    def body(q_ref, k_ref, u_ref, w_ref, a_ref, gc_ref, gr_ref, sh_ref, do_ref,
             du_ref, dw_ref, dqd_ref, dkd_ref, da_ref, dgl_ref, ds_ref):
        i = pl.program_id(0)

        @pl.when(i == 0)
        def _():
            ds_ref[...] = jnp.zeros_like(ds_ref)

        tri, _ = _chunk_masks()
        sub4 = lax.broadcasted_iota(jnp.int32, (B_HEADS, CHUNK), 0)
        lane_last = lax.broadcasted_iota(jnp.int32, (1, CHUNK), 1) == CHUNK - 1
        ep, em, el = _scan_decays(_stack_cols(gc_ref, 1), _stack_rows(gr_ref, 1))
        q, k, u, w, dout = (_stack_tokens(r, 1) for r in (q_ref, k_ref, u_ref, w_ref, do_ref))
        s0 = sh_ref[...]
        ds = ds_ref[...]
        ut = u - _bnn(w, s0)
        dut = _btn(a_ref[...], dout) + _bnn(k * em, ds)
        ds_ref[...] = el * ds + _btn(q * ep, dout) - _btn(w, dut)
        dw = -_bnt(dut, s0)
        dqd = _bnt(dout, s0)
        dkd = _bnt(ut, ds)
        da_ref[...] = jnp.where(tri, _bnt(dout, ut), 0.0)
        d_el = jnp.sum(jnp.sum(s0 * ds, axis=-1, keepdims=True), axis=-2, keepdims=True)
        last = d_el * el
        dgl_acc = jnp.zeros((B_HEADS, CHUNK), F32)
        for h in range(B_HEADS):
            du_ref[:, _head(h)] = dut[h]
            dw_ref[:, _head(h)] = dw[h]
            dqd_ref[:, _head(h)] = dqd[h]
            dkd_ref[:, _head(h)] = dkd[h]
            dgl_acc = jnp.where(sub4 == h, jnp.where(lane_last, last[h], 0.0), dgl_acc)
        dgl_ref[...] = dgl_acc

    return pl.pallas_call(
        body, name=name, grid=(nc,), in_specs=[tok, tok, tok, tok, mat, colv, rowv, smat, tok],
        out_specs=[tok, tok, tok, tok, mat, rowv],
        out_shape=[jax.ShapeDtypeStruct((S, Wd), F32)] * 4 + [jax.ShapeDtypeStruct((nc, B_HEADS, CHUNK, CHUNK), F32),
                                                             jax.ShapeDtypeStruct((nc, B_HEADS, CHUNK), F32)],
        scratch_shapes=[pltpu.VMEM((B_HEADS, B_DH, B_DH), F32)], compiler_params=_arb(1),
    )(qn, kn, u, w, a, gcol, grow, ssave, do)


def _gdn_local_bwd(qn, kn, v, bcol, gcol, grow, tsave, du, dw, dqd, dkd, da, dgl, name, exchange=()):
    S, Wd = qn.shape
    nc = S // CHUNK
    steps = nc // GROUP
    tok, colv, rowv, mat = _gdn_group_specs(steps, Wd)
    ne = len(exchange)

    def body(q_ref, k_ref, v_ref, bc_ref, gc_ref, gr_ref, t_ref, du_ref, dw_ref, dqd_ref, dkd_ref, da_ref, dgl_ref, *rest):
        srcs, (dq_ref, dk_ref, dv_ref, dbc_ref, dgc_ref, dgr_ref) = rest[:ne], rest[ne:ne + 6]
        eouts, sems = rest[ne + 6:2 * ne + 6], rest[2 * ne + 6:]
        i = pl.program_id(0)
        done = _carry(_chips_phases(srcs, eouts, *sems), i == 0, None, i == steps - 1) if ne else None
        tri, strict = _chunk_masks()
        lane4 = lax.broadcasted_iota(jnp.int32, (CHUNK, B_HEADS), 1)
        sub4 = lax.broadcasted_iota(jnp.int32, (B_HEADS, CHUNK), 0)
        lane_last = lax.broadcasted_iota(jnp.int32, (1, CHUNK), 1) == CHUNK - 1
        q, k, vv, dut, dwv, dqd, dkd = (_stack_tokens(r, GROUP)
                                        for r in (q_ref, k_ref, v_ref, du_ref, dw_ref, dqd_ref, dkd_ref))
        bc, gc, gr = _stack_cols(bc_ref, GROUP), _stack_cols(gc_ref, GROUP), _stack_rows(gr_ref, GROUP)
        dm, kk, ep, em, el, vb, kb = _chunk_local(q, k, vv, bc, gc, gr, tri)
        t = jnp.stack([t_ref[c, h] for c, h in _pairs(GROUP)])
        dav = jnp.stack([da_ref[c, h] for c, h in _pairs(GROUP)])
        qk = _bnt(q, k)
        dt = _nt(dut, vb, SOLVE_PREC) + _nt(dwv, kb, SOLVE_PREC)
        dvb = _tn(t, dut, SOLVE_PREC)
        dkb = _tn(t, dwv, SOLVE_PREC)
        dl = jnp.where(strict, -_tn(t, _nt(dt, t, SOLVE_PREC), SOLVE_PREC), 0.0)
        g1 = dl * dm
        dkb_k = jnp.sum(dkb * k, axis=-1, keepdims=True)
        dbeta = jnp.sum(g1 * kk, axis=-1, keepdims=True) + jnp.sum(dvb * vv, axis=-1, keepdims=True) + dkb_k * ep
        dkk = g1 * bc
        ddm = dl * (bc * kk) + dav * qk
        dqk = dav * dm
        dq = _bnn(dqk, k) + dqd * ep
        dk = _btn(dqk, q) + _bnn(dkk, k) + _btn(dkk, k) + dkb * (bc * ep) + dkd * em
        dv = dvb * bc
        dep = dkb_k * bc + jnp.sum(dqd * q, axis=-1, keepdims=True)
        dem = jnp.sum(dkd * k, axis=-1, keepdims=True)
        mm = ddm * dm
        dgam_c = jnp.sum(mm, axis=-1, keepdims=True) + dep * ep - dem * em
        dglast = jnp.sum(dem * em, axis=-2, keepdims=True)
        dgam_r = -jnp.sum(mm, axis=-2, keepdims=True) + jnp.where(lane_last, dglast, 0.0)
        for c in range(GROUP):
            dbc_acc = jnp.zeros((CHUNK, B_HEADS), F32)
            dgc_acc = jnp.zeros((CHUNK, B_HEADS), F32)
            dgr_acc = jnp.zeros((B_HEADS, CHUNK), F32)
            for h in range(B_HEADS):
                n = c * B_HEADS + h
                dq_ref[_tok(c), _head(h)] = dq[n]
                dk_ref[_tok(c), _head(h)] = dk[n]
                dv_ref[_tok(c), _head(h)] = dv[n]
                dbc_acc = jnp.where(lane4 == h, dbeta[n], dbc_acc)
                dgc_acc = jnp.where(lane4 == h, dgam_c[n], dgc_acc)
                dgr_acc = jnp.where(sub4 == h, dgam_r[n], dgr_acc)
            dbc_ref[c] = dbc_acc
            dgc_ref[c] = dgc_acc
            dgr_ref[c] = dgr_acc + dgl_ref[c]
        if ne:
            done()

    res = pl.pallas_call(
        body, name=name, grid=(steps,),
        in_specs=[tok, tok, tok, colv, colv, rowv, mat, tok, tok, tok, tok, mat, rowv] + [HBM] * ne,
        out_specs=[tok, tok, tok, colv, colv, rowv] + [HBM] * ne,
        out_shape=[jax.ShapeDtypeStruct((S, Wd), F32)] * 3
        + [jax.ShapeDtypeStruct((nc, CHUNK, B_HEADS), F32)] * 2 + [jax.ShapeDtypeStruct((nc, B_HEADS, CHUNK), F32)]
        + _chips_out_shapes(exchange),
        scratch_shapes=_chips_scratch(ne) if ne else [], compiler_params=_arb(1),
    )(qn, kn, v, bcol, gcol, grow, tsave, du, dw, dqd, dkd, da, dgl, *exchange)
    return tuple(res[:6]) + (list(res[6:]),)


def _gdn_post_fwd(o, proj, ng, name):
    S, W = o.shape
    tm = _pick(S, 512, 8)

    def body(o_ref, z_ref, g_ref, y_ref):
        gv = g_ref[...]
        for h in range(B_HEADS):
            hs = slice(h * B_DH, (h + 1) * B_DH)
            oh = o_ref[:, hs]
            z = z_ref[:, hs]
            r = lax.rsqrt(jnp.mean(oh * oh, axis=-1, keepdims=True) + EPS)
            y_ref[:, hs] = (oh * r * gv * (z * _sigmoid(z))).astype(BF16)

    return pl.pallas_call(
        body, name=name, grid=(S // tm,), in_specs=[_rows(tm, W), _rows(tm, W, P_Z // W), _vec(B_DH)],
        out_specs=_rows(tm, W), out_shape=jax.ShapeDtypeStruct((S, W), BF16), compiler_params=_par(1),
    )(o, proj, ng)


def _gdn_post_bwd(dy, o, proj, ng, name):
    S, W = o.shape
    tm = _pick(S, 512, 8)

    def body(dy_ref, o_ref, z_ref, g_ref, do_ref, dz_ref, dg_ref):
        i = pl.program_id(0)
        gv = g_ref[...]
        dg = jnp.zeros((1, B_DH), F32)
        for h in range(B_HEADS):
            hs = slice(h * B_DH, (h + 1) * B_DH)
            oh = o_ref[:, hs]
            z = z_ref[:, hs]
            d = dy_ref[:, hs]
            r = lax.rsqrt(jnp.mean(oh * oh, axis=-1, keepdims=True) + EPS)
            n = oh * r
            sg = _sigmoid(z)
            sz = z * sg
            dn = d * gv * sz
            dg = dg + jnp.sum(d * n * sz, axis=0, keepdims=True)
            dz_ref[:, hs] = (d * n * gv * (sg * (1.0 + z * (1.0 - sg)))).astype(BF16)
            do_ref[:, hs] = r * (dn - n * jnp.mean(dn * n, axis=-1, keepdims=True))
        _acc(dg_ref, dg, i)

    return pl.pallas_call(
        body, name=name, grid=(S // tm,), in_specs=[_rows(tm, W), _rows(tm, W), _rows(tm, W, P_Z // W), _vec(B_DH)],
        out_specs=[_rows(tm, W), _rows(tm, W), _vec(B_DH)],
        out_shape=[jax.ShapeDtypeStruct((S, W), F32), jax.ShapeDtypeStruct((S, W), BF16),
                   jax.ShapeDtypeStruct((1, B_DH), F32)],
        compiler_params=_arb(1),
    )(dy, o, proj, ng)


def _ada_mod(c_all, w_ada, b_shard, name):
    L, D, Ns = w_ada.shape
    B = c_all.shape[0]

    def body(c_ref, w_ref, b_ref, o_ref):
        cv = c_ref[...]
        cond = (cv * _sigmoid(cv)).astype(BF16)
        o_ref[...] = _nn(cond, w_ref[...].astype(BF16)) + b_ref[...]

    return pl.pallas_call(
        body, name=name, grid=(L,),
        in_specs=[pl.BlockSpec((B, D), lambda l: (0, 0)), pl.BlockSpec((None, D, Ns), lambda l: (l, 0, 0)),
                  pl.BlockSpec((None, 1, Ns), lambda l: (l, 0, 0))],
        out_specs=pl.BlockSpec((None, B, Ns), lambda l: (l, 0, 0)),
        out_shape=jax.ShapeDtypeStruct((L, B, Ns), F32), compiler_params=_par(1),
    )(c_all, w_ada, b_shard)


def _ada_wgrad(c_all, dmod, name):
    L, B, Ns = dmod.shape
    D = c_all.shape[1]

    def body(c_ref, d_ref, o_ref):
        cv = c_ref[...]
        cond = (cv * _sigmoid(cv)).astype(BF16)
        o_ref[...] = _tn(cond, d_ref[...].astype(BF16))

    return pl.pallas_call(
        body, name=name, grid=(L,),
        in_specs=[pl.BlockSpec((B, D), lambda l: (0, 0)), pl.BlockSpec((None, B, Ns), lambda l: (l, 0, 0))],
        out_specs=pl.BlockSpec((None, D, Ns), lambda l: (l, 0, 0)),
        out_shape=jax.ShapeDtypeStruct((L, D, Ns), F32), compiler_params=_par(1),
    )(c_all, dmod)


W_IN_PIECES = ((0, 0, 1410), (1, 0, 1410), (2, 0, 252), (2, 772, 638), (3, 0, 1410), (2, 252, 512), (2, 764, 8))


def _reorder_w_in(w4, name):
    L, _, D, Cs = w4.shape
    tm = _pick(D, 256, 16)
    used = sum(p[2] for p in W_IN_PIECES)

    def body(w_ref, o_ref):
        shard = [w_ref[s] for s in range(4)]
        parts = [shard[s][:, lo:lo + n] for s, lo, n in W_IN_PIECES]
        o_ref[...] = jnp.concatenate(parts + [jnp.zeros((tm, P_END - used), w4.dtype)], axis=1)

    return pl.pallas_call(
        body, name=name, grid=(L, D // tm), in_specs=[pl.BlockSpec((None, 4, tm, Cs), lambda l, i: (l, 0, i, 0))],
        out_specs=pl.BlockSpec((None, tm, P_END), lambda l, i: (l, i, 0)),
        out_shape=jax.ShapeDtypeStruct((L, D, P_END), w4.dtype), compiler_params=_par(2),
    )(w4)


def _restore_w_in(g, name):
    D = g.shape[0]
    tm = _pick(D, 256, 8)

    def body(g_ref, o_ref):
        gv = g_ref[...]
        off = 0
        pieces = {}
        for s, lo, n in W_IN_PIECES:
            pieces.setdefault(s, []).append((lo, gv[:, off:off + n]))
            off += n
        for s, lst in pieces.items():
            lst.sort(key=lambda t: t[0])
            o_ref[s] = lst[0][1] if len(lst) == 1 else jnp.concatenate([t[1] for t in lst], axis=1)

    return pl.pallas_call(
        body, name=name, grid=(D // tm,), in_specs=[pl.BlockSpec((tm, P_END), lambda i: (i, 0))],
        out_specs=pl.BlockSpec((4, tm, W_IN_SHARD), lambda i: (0, i, 0)),
        out_shape=jax.ShapeDtypeStruct((4, D, W_IN_SHARD), g.dtype), compiler_params=_par(1),
    )(g)


def _adam_update(w, g, m, v):
    mn = ADAM_B1 * m + (1.0 - ADAM_B1) * g
    vn = ADAM_B2 * v + (1.0 - ADAM_B2) * (g * g)
    m_hat = mn / (1.0 - ADAM_B1 ** ADAM_STEP)
    v_hat = vn / (1.0 - ADAM_B2 ** ADAM_STEP)
    return -ADAM_LR * (m_hat / (jnp.sqrt(v_hat) + ADAM_EPS) + ADAM_WD * w), mn, vn


def _adamw(w, g, m, v, name):
    shape = w.shape
    C = shape[-1]
    R = w.size // C
    tm = _pick(R, 512, 8)
    spec = pl.BlockSpec((tm, C), lambda i: (i, 0))

    def body(w_ref, g_ref, m_ref, v_ref, d_ref, mo_ref, vo_ref):
        d_ref[...], mo_ref[...], vo_ref[...] = _adam_update(w_ref[...], g_ref[...], m_ref[...], v_ref[...])

    outs = pl.pallas_call(
        body, name=name, grid=(R // tm,), in_specs=[spec] * 4, out_specs=[spec] * 3,
        out_shape=[jax.ShapeDtypeStruct((R, C), F32)] * 3, compiler_params=_par(1),
    )(*(t.reshape(R, C) for t in (w, g, m, v)))
    return tuple(o.reshape(shape) for o in outs)


def _adamw_layers(w, gs, m, v, name):
    L, R, C = w.shape
    tm = _pick(R, 128, 8)
    spec = pl.BlockSpec((None, tm, C), lambda l, i: (l, i, 0))
    g_specs = [pl.BlockSpec((tm, C), functools.partial(lambda ll, l, i: (jnp.where(l == ll, i, 0), 0), ll))
               for ll in range(L)]

    def body(w_ref, m_ref, v_ref, *rest):
        g_refs, (go_ref, d_ref, mo_ref, vo_ref) = rest[:L], rest[L:]
        l = pl.program_id(0)
        for ll in range(L):
            @pl.when(l == ll)
            def _():
                g = g_refs[ll][...]
                go_ref[...] = g
                d_ref[...], mo_ref[...], vo_ref[...] = _adam_update(w_ref[...], g, m_ref[...], v_ref[...])

    return pl.pallas_call(
        body, name=name, grid=(L, R // tm), in_specs=[spec] * 3 + g_specs, out_specs=[spec] * 4,
        out_shape=[jax.ShapeDtypeStruct((L, R, C), F32)] * 4, compiler_params=_arb(2),
    )(w, m, v, *gs)


def _pair_sums(a, where, b, name):
    NB, _, R, C = a.shape

    def body(where_ref, a_ref, b_ref, p_ref, own_ref):
        s = a_ref[...] + b_ref[...].astype(F32)
        p_ref[...] = s.astype(BF16)

        @pl.when(pl.program_id(0) == where_ref[1])
        def _():
            own_ref[...] = s

    return pl.pallas_call(
        body, name=name,
        grid_spec=pltpu.PrefetchScalarGridSpec(
            num_scalar_prefetch=1, grid=(NB,),
            in_specs=[pl.BlockSpec((None, None, R, C), lambda k, w: (k, w[0], 0, 0)),
                      pl.BlockSpec((None, R, C), lambda k, w: (k, 0, 0))],
            out_specs=[pl.BlockSpec((None, R, C), lambda k, w: (k, 0, 0)), pl.BlockSpec((R, C), lambda k, w: (0, 0))]),
        out_shape=[jax.ShapeDtypeStruct((NB, R, C), BF16), jax.ShapeDtypeStruct((R, C), F32)],
        compiler_params=_arb(1),
    )(where, a, b)


def _sum_own_and_received(own, recv, where, name):
    R, C = own.shape
    tm = _pick(R, 256, 16)

    def body(where_ref, p_ref, r_ref, o_ref):
        o_ref[...] = ((p_ref[...] + r_ref[0].astype(F32)) + r_ref[1].astype(F32)) + r_ref[2].astype(F32)

    return pl.pallas_call(
        body, name=name,
        grid_spec=pltpu.PrefetchScalarGridSpec(
            num_scalar_prefetch=1, grid=(R // tm,),
            in_specs=[pl.BlockSpec((tm, C), lambda i, w: (i, 0)), pl.BlockSpec((3, tm, C), lambda i, w: (0, i, 0))],
            out_specs=pl.BlockSpec((None, tm, C), lambda i, w: (w[0], i, 0))),
        out_shape=jax.ShapeDtypeStruct((2, R, C), F32), compiler_params=_par(1),
    )(where, own, recv)


def _position():
    return lax.axis_index("x"), lax.axis_index("y"), lax.axis_index("c")


def _other_chips(x, y):
    return [(1 - x, y), (x, 1 - y), (1 - x, 1 - y)]


HBM = pl.BlockSpec(memory_space=pl.ANY)


def _allgather8(blk, name, reduce_rows=None):
    M, N = blk.shape

    def body(x_ref, out_ref, *rest):
        if reduce_rows is None:
            send_sems, recv_sems, local_sem = rest
        else:
            sum_ref, send_sems, recv_sems, local_sem = rest
        x, y, c = _position()
        me, sibling = (x, y, c), (x, y, 1 - c)
        chips = _other_chips(x, y)

        def rows(px, py, pc):
            return out_ref.at[pl.ds((4 * px + 2 * py + pc) * M, M), :]

        def copy(k, block, to, src=None):
            return pltpu.make_async_remote_copy(
                src_ref=rows(*block) if src is None else src, dst_ref=rows(*block),
                send_sem=send_sems.at[k], recv_sem=recv_sems.at[k], device_id=to, device_id_type=MESH)

        mine = pltpu.make_async_copy(x_ref, rows(*me), local_sem)
        mine.start()
        first = [copy(0, me, sibling, src=x_ref)]
        first += [copy(1 + j, me, (*chip, c), src=x_ref) for j, chip in enumerate(chips)]
        for cp in first:
            cp.start()
        passed = [copy(4 + j, (*chip, c), sibling) for j, chip in enumerate(chips)]
        for j, chip in enumerate(chips):
            copy(1 + j, (*chip, c), me).wait_recv()
            passed[j].start()
        copy(0, sibling, me).wait_recv()
        for j, chip in enumerate(chips):
            copy(4 + j, (*chip, 1 - c), me).wait_recv()
        for cp in first + passed:
            cp.wait_send()
        mine.wait()
        if reduce_rows is not None:
            tot = out_ref[pl.ds(0, reduce_rows), :]
            for d in range(1, 8):
                tot = tot + out_ref[pl.ds(d * M, reduce_rows), :]
            sum_ref[...] = tot

    vmem = pl.BlockSpec(memory_space=pltpu.VMEM)
    out_shape = [jax.ShapeDtypeStruct((8 * M, N), blk.dtype)]
    if reduce_rows is not None:
        out_shape.append(jax.ShapeDtypeStruct((reduce_rows, N), blk.dtype))
    res = pl.pallas_call(
        body, name=name, out_shape=out_shape, in_specs=[vmem], out_specs=[vmem] * len(out_shape),
        scratch_shapes=[pltpu.SemaphoreType.DMA((7,)), pltpu.SemaphoreType.DMA((7,)), pltpu.SemaphoreType.DMA],
    )(blk)
    return res[0] if reduce_rows is None else (res[0], res[1])


def _gather_phases(layer, srcs, outs, send_sems, recv_sems, local_sems):
    n = len(srcs)
    x, y, c = _position()
    me, sibling = (x, y, c), (x, y, 1 - c)
    chips = _other_chips(x, y)

    def region(t, px, py, pc):
        return outs[t].at[2 * px + py, pc]

    def copy(t, k, block, to, own=False):
        return pltpu.make_async_remote_copy(
            src_ref=srcs[t].at[layer, c] if own else region(t, *block), dst_ref=region(t, *block),
            send_sem=send_sems.at[7 * t + k], recv_sem=recv_sems.at[7 * t + k], device_id=to, device_id_type=MESH)

    def local(t):
        return pltpu.make_async_copy(srcs[t].at[layer, c], region(t, *me), local_sems.at[t])

    def first(t):
        return [copy(t, 0, me, sibling, own=True)] + [copy(t, 1 + j, me, (*chip, c), own=True)
                                                       for j, chip in enumerate(chips)]

    def start():
        for t in range(n):
            local(t).start()
        for t in range(n):
            for cp in first(t):
                cp.start()

    def forward():
        for j, chip in enumerate(chips):
            for t in range(n):
                copy(t, 1 + j, (*chip, c), me).wait_recv()
                copy(t, 4 + j, (*chip, c), sibling).start()

    def finish():
        for t in range(n):
            copy(t, 0, sibling, me).wait_recv()
        for j, chip in enumerate(chips):
            for t in range(n):
                copy(t, 4 + j, (*chip, 1 - c), me).wait_recv()
        for t in range(n):
            for cp in first(t) + [copy(t, 4 + j, (*chip, c), sibling) for j, chip in enumerate(chips)]:
                cp.wait_send()
            local(t).wait()

    return start, forward, finish


def _gather_scratch(n):
    return [pltpu.SemaphoreType.DMA((7 * n,)), pltpu.SemaphoreType.DMA((7 * n,)), pltpu.SemaphoreType.DMA((n,))]


def _gather_out_shapes(shards):
    return [jax.ShapeDtypeStruct((4,) + s.shape[1:], s.dtype) for s in shards]


def _gather_weights(shards, layer, name):
    n = len(shards)

    def body(*refs):
        start, forward, finish = _gather_phases(layer, refs[:n], refs[n:2 * n], *refs[2 * n:])
        start()
        forward()
        finish()

    return pl.pallas_call(
        body, name=name, out_shape=_gather_out_shapes(shards), in_specs=[HBM] * n, out_specs=[HBM] * n,
        scratch_shapes=_gather_scratch(n),
    )(*shards)


def _rs_sibling(gs, name):
    n = len(gs)

    def body(*refs):
        srcs, outs = refs[:n], refs[n:2 * n]
        send_sems, recv_sems = refs[2 * n:]
        x, y, c = _position()
        copies = [pltpu.make_async_remote_copy(
            src_ref=srcs[t].at[k, 1 - c], dst_ref=outs[t].at[k], send_sem=send_sems.at[4 * t + k],
            recv_sem=recv_sems.at[4 * t + k], device_id=(x, y, 1 - c), device_id_type=MESH)
            for t in range(n) for k in range(4)]
        for cp in copies:
            cp.start()
        for cp in copies:
            cp.wait()

    out_shape = [jax.ShapeDtypeStruct((4,) + g.shape[2:], g.dtype) for g in gs]
    return pl.pallas_call(
        body, name=name, out_shape=out_shape, in_specs=[HBM] * n, out_specs=[HBM] * n,
        scratch_shapes=[pltpu.SemaphoreType.DMA((4 * n,)), pltpu.SemaphoreType.DMA((4 * n,))],
    )(*gs)


def _rs_chips(ps, name):
    n = len(ps)

    def body(*refs):
        start, finish = _chips_phases(refs[:n], refs[n:2 * n], *refs[2 * n:])
        start()
        finish()

    return pl.pallas_call(
        body, name=name, out_shape=_chips_out_shapes(ps), in_specs=[HBM] * n, out_specs=[HBM] * n,
        scratch_shapes=_chips_scratch(n),
    )(*ps)


def _chips_phases(srcs, outs, send_sems, recv_sems):
    x, y, c = _position()
    copies = [pltpu.make_async_remote_copy(
        src_ref=srcs[t].at[2 * px + py], dst_ref=outs[t].at[j], send_sem=send_sems.at[3 * t + j],
        recv_sem=recv_sems.at[3 * t + j], device_id=(px, py, c), device_id_type=MESH)
        for t in range(len(srcs)) for j, (px, py) in enumerate(_other_chips(x, y))]

    def start():
        for cp in copies:
            cp.start()

    def finish():
        for cp in copies:
            cp.wait()

    return start, finish


def _chips_scratch(n):
    return [pltpu.SemaphoreType.DMA((3 * n,)), pltpu.SemaphoreType.DMA((3 * n,))]


def _chips_out_shapes(ps):
    return [jax.ShapeDtypeStruct((3,) + p.shape[1:], p.dtype) for p in ps]


def _rs_pair(hs, name):
    n = len(hs)

    def body(*refs):
        bufs = refs[n:2 * n]
        send_sems, recv_sems = refs[2 * n:]
        x, y, c = _position()

        def copy(t, half):
            return pltpu.make_async_remote_copy(
                src_ref=bufs[t].at[half], dst_ref=bufs[t].at[half], send_sem=send_sems.at[t], recv_sem=recv_sems.at[t],
                device_id=(x, y, 1 - c), device_id_type=MESH)

        for t in range(n):
            copy(t, c).start()
        for t in range(n):
            copy(t, 1 - c).wait_recv()
        for t in range(n):
            copy(t, c).wait_send()

    out_shape = [jax.ShapeDtypeStruct(h.shape, h.dtype) for h in hs]
    return pl.pallas_call(
        body, name=name, out_shape=out_shape, in_specs=[HBM] * n, out_specs=[HBM] * n,
        input_output_aliases={t: t for t in range(n)},
        scratch_shapes=[pltpu.SemaphoreType.DMA((n,)), pltpu.SemaphoreType.DMA((n,))],
    )(*hs)


BIG = ("w_in", "w_branch_a", "w_branch_b", "w_out", "w_ffn_in", "w_ffn_out")
CARRY_ATTN = ["w_in"]
CARRY_LOCAL = ["w_ffn_in"]
CARRY_SCAN = ["w_branch_a", "w_branch_b", "w_out", "w_ffn_out"]
CARRY_DATTN = ["w_in", "w_ffn_in"]
CARRY_DLOCAL = ["w_branch_a", "w_branch_b", "w_out", "w_ffn_out"]


def _band_bias(rel_table, name):
    L, H, n = rel_table.shape
    tab = jnp.pad(rel_table, ((0, 0), (0, 0), (0, NREL_PAD - n))).reshape(L * H, 1, NREL_PAD)
    band = (A_PAST + 1) * CHUNK

    def body(t_ref, o_ref):
        r = lax.broadcasted_iota(jnp.int32, (NREL_PAD, SKEW_W), 0)
        xi = lax.broadcasted_iota(jnp.int32, (NREL_PAD, SKEW_W), 1)
        diag = jnp.where(xi < KSPAN, xi, xi - SKEW_W)
        rel = jnp.clip(A_PAST * CHUNK - diag, -A_MAX_REL, A_MAX_REL) + A_MAX_REL
        e = _nn(t_ref[...], jnp.where(rel == r, 1.0, 0.0).astype(F32), HI)
        x = jnp.broadcast_to(e, (QBLK, SKEW_W))
        row = lax.broadcasted_iota(jnp.int32, (QBLK, SKEW_W), 0)
        for b in range(QBLK.bit_length() - 1):
            x = jnp.where(((row >> b) & 1) == 1, pltpu.roll(x, 1 << b, 1), x)
        x = x[:, :KSPAN]
        first = (lax.broadcasted_iota(jnp.int32, (QBLK, KSPAN), 0) // CHUNK) * CHUNK
        col = lax.broadcasted_iota(jnp.int32, (QBLK, KSPAN), 1)
        o_ref[...] = jnp.where((col >= first) & (col < first + band), x, NEG)

    out = pl.pallas_call(
        body, name=name, grid=(L * H,), in_specs=[pl.BlockSpec((None, 1, NREL_PAD), lambda i: (i, 0, 0))],
        out_specs=pl.BlockSpec((None, QBLK, KSPAN), lambda i: (i, 0, 0)),
        out_shape=jax.ShapeDtypeStruct((L * H, QBLK, KSPAN), F32), compiler_params=_par(1),
    )(tab)
    return out.reshape(L, H, QBLK, KSPAN)


def _col_row_forms(t, S):
    nc = S // CHUNK
    return t.T.reshape(nc, CHUNK, B_HEADS), t.reshape(B_HEADS, nc, CHUNK).transpose(1, 0, 2)


def _layer_fwd(l, x, mod, W, P, big, gather=None):
    S, D = x.shape
    n = lambda s: f"{s}_l{l}"
    sh1, sc1, gt1, sh2, sc2, gt2 = (mod[i:i + 1] for i in range(6))
    h1 = _lnmod_fwd(x, P["norm1_g"][l:l + 1], sc1, sh1, n("ln1"))
    proj = _matmul(h1, W["w_in"], "nn", F32, n("proj"), tn=1152)
    part = (lambda names: ([gather[0][BIG.index(k)] for k in names], gather[1])) if gather is not None else (lambda names: None)
    ya, got_a = _attn_fwd(proj, big, n("attn"), part(CARRY_ATTN))
    ba = proj[:, P_BA:P_BA + 2 * B_HEADS]
    b_t, a_t = ba[:, :B_HEADS].T, ba[:, B_HEADS:].T
    alog, dtb = P["a_log"][l].reshape(B_HEADS, 1), P["dt_bias"][l].reshape(B_HEADS, 1)
    beta, gam = _gdn_gates_fwd(b_t, a_t, alog, dtb, n("gates"))
    bcol, _ = _col_row_forms(beta, S)
    gcol, grow = _col_row_forms(gam, S)
    qn, kn, v = _gdn_pre_fwd(proj, P["w_conv"][l], n("gdnpre"))
    tsave, amat, u, w, got_l = _gdn_local_fwd(qn, kn, v, bcol, gcol, grow, n("gdnlocal"), part(CARRY_LOCAL))
    o, ssave, got_s = _gdn_scan_fwd(qn, kn, u, w, amat, gcol, grow, n("gdnscan"), part(CARRY_SCAN))
    got = dict(zip(CARRY_ATTN + CARRY_LOCAL + CARRY_SCAN, got_a + got_l + got_s))
    gathered = [got[k] for k in BIG] if gather is not None else None
    yb = _gdn_post_fwd(o, proj, P["gdn_norm_g"][l:l + 1], n("gdnpost"))
    pa = _matmul(ya, W["w_branch_a"], "nn", F32, n("pa"), tm=2048, stacked=True)
    pb = _matmul(yb, W["w_branch_b"], "nn", F32, n("pb"), tm=2048, stacked=True)
    merged = _merge_fwd(proj, pa, pb, n("merge"))
    ao = _matmul(merged, W["w_out"], "nn", F32, n("ao"))
    x1 = _gate_fwd(x, ao, gt1, n("res1"))
    h2 = _lnmod_fwd(x1, P["norm2_g"][l:l + 1], sc2, sh2, n("ln2"))
    gu = _matmul(h2, W["w_ffn_in"], "nn", F32, n("gu"), stacked=True)
    act = _ffn_act_fwd(gu, n("act"))
    fo = _matmul(act, W["w_ffn_out"], "nn", F32, n("fo"), tk=1408)
    x2 = _gate_fwd(x1, fo, gt2, n("res2"))
    saved = dict(x=x, h1=h1, proj=proj, ya=ya, b_t=b_t, a_t=a_t, bcol=bcol, gcol=gcol, grow=grow,
                 qn=qn, kn=kn, v=v, o=o, tsave=tsave, ssave=ssave, amat=amat, u=u, w=w, yb=yb, pa=pa, pb=pb,
                 merged=merged, ao=ao, x1=x1,
                 h2=h2, gu=gu, act=act, fo=fo)
    return x2, saved, gathered


def _layer_bwd(l, dx2, sv, mod, W, P, big, exchange=()):
    S, D = dx2.shape
    n = lambda s: f"{s}_l{l}"
    sh1, sc1, gt1, sh2, sc2, gt2 = (mod[i:i + 1] for i in range(6))
    g = {}
    view = lambda t: t.reshape((4, 2, t.shape[-2] // (2 if t.ndim == 3 else 8), t.shape[-1]))
    dz2, dgt2 = _gate_bwd(dx2, sv["fo"], gt2, n("dres2"))
    g["w_ffn_out"] = view(_matmul(sv["act"], dz2, "tn", F32, n("dwfo"), tm=1408))
    dact = _matmul(dz2, W["w_ffn_out"], "nt", F32, n("dact"), tn=1408)
    dgu = _ffn_act_bwd(sv["gu"], dact, n("dgu"))
    g["w_ffn_in"] = view(_matmul(sv["h2"], dgu, "tn", F32, n("dwfi"), out_stacked=True))
    dh2 = _matmul(dgu, W["w_ffn_in"], "nt", F32, n("dh2"), stacked=True)
    dx1, dsh2, dsc2, dn2 = _lnmod_bwd(dh2, sv["x1"], P["norm2_g"][l:l + 1], sc2, dx2, n("dln2"))
    dz1, dgt1 = _gate_bwd(dx1, sv["ao"], gt1, n("dres1"))
    g["w_out"] = view(_matmul(sv["merged"], dz1, "tn", F32, n("dwo")))
    dmerged = _matmul(dz1, W["w_out"], "nt", F32, n("dmerged"))
    dgab, dpa, dpb = _merge_bwd(sv["proj"], sv["pa"], sv["pb"], dmerged, n("dmerge"))
    g["w_branch_a"] = view(_matmul(sv["ya"], dpa, "tn", F32, n("dwa"), out_stacked=True))
    g["w_branch_b"] = view(_matmul(sv["yb"], dpb, "tn", F32, n("dwb"), out_stacked=True))
    dya = _matmul(dpa, W["w_branch_a"], "nt", BF16, n("dya"), tm=2048, stacked=True)
    dyb = _matmul(dpb, W["w_branch_b"], "nt", F32, n("dyb"), tm=2048, stacked=True)
    ex = (lambda names: [exchange[BIG.index(k)] for k in names]) if len(exchange) else (lambda names: ())
    dq, dk, dv, dbig, rec_a = _attn_bwd(sv["proj"], big, dya, n("dattn"), ex(CARRY_DATTN))
    g["rel_table"] = _rel_table_grad(dbig, n("drel"))[:, 0, :2 * A_MAX_REL + 1]
    do, dzb, dng = _gdn_post_bwd(dyb, sv["o"], sv["proj"], P["gdn_norm_g"][l:l + 1], n("dgdnpost"))
    g["gdn_norm_g"] = dng[0]
    du, dw, dqd, dkd, da, dgl = _gdn_scan_bwd(sv["qn"], sv["kn"], sv["u"], sv["w"], sv["amat"], sv["gcol"], sv["grow"],
                                              sv["ssave"], do, n("dgdnscan"))
    dqn, dkn, dvv, dbc, dgc, dgr, rec_l = _gdn_local_bwd(
        sv["qn"], sv["kn"], sv["v"], sv["bcol"], sv["gcol"], sv["grow"], sv["tsave"], du, dw, dqd, dkd, da, dgl,
        n("dgdnlocal"), ex(CARRY_DLOCAL))
    rec = dict(zip(CARRY_DATTN + CARRY_DLOCAL, rec_a + rec_l))
    received = [rec[k] for k in BIG] if len(exchange) else None
    dbeta_t = dbc.reshape(S, B_HEADS).T
    dgam_a = dgc.reshape(S, B_HEADS).T
    dgam_b = dgr.transpose(1, 0, 2).reshape(B_HEADS, S)
    alog, dtb = P["a_log"][l].reshape(B_HEADS, 1), P["dt_bias"][l].reshape(B_HEADS, 1)
    db_t, da_t, dal, ddt = _gdn_gates_bwd(dbeta_t, dgam_a, dgam_b, sv["b_t"], sv["a_t"], alog, dtb, n("dgates"))
    g["a_log"], g["dt_bias"] = dal[:, 0], ddt[:, 0]
    dy = _gdn_pre_bwd_a(sv["proj"], P["w_conv"][l], dqn, dkn, dvv, n("dgdnpre_a"))
    dqkvb, g["w_conv"] = _gdn_pre_bwd_b(sv["proj"], P["w_conv"][l], dy, n("dgdnpre_b"))
    dba = jnp.concatenate([db_t.T, da_t.T, jnp.zeros((S, P_END - P_BA - 2 * B_HEADS), F32)], axis=1)
    dproj = jnp.concatenate([dq.astype(BF16), dk.astype(BF16), dv.astype(BF16), dqkvb, dgab, dzb, dba.astype(BF16)],
                            axis=1)
    g["w_in"] = view(_restore_w_in(_matmul(sv["h1"], dproj, "tn", F32, n("dwin"), tn=1152), n("dwin_cols")))
    dh1 = _matmul(dproj, W["w_in"], "nt", F32, n("dh1"), tk=1152)
    dx, dsh1, dsc1, dn1 = _lnmod_bwd(dh1, sv["x"], P["norm1_g"][l:l + 1], sc1, dx1, n("dln1"))
    g["norm1_g"], g["norm2_g"] = dn1[0], dn2[0]
    dmod = jnp.concatenate([dsh1, dsc1, dgt1, dsh2, dsc2, dgt2], axis=1)[0]
    return dx, g, dmod, received


SMALL = ("norm1_g", "norm2_g", "rel_table", "w_conv", "a_log", "dt_bias", "gdn_norm_g")
SMALL_PACK_C = 1024


def _as_rows(t):
    flat = t.reshape(-1)
    rows = -(-flat.shape[0] // SMALL_PACK_C)
    return jnp.pad(flat, (0, rows * SMALL_PACK_C - flat.shape[0])).reshape(rows, SMALL_PACK_C)


def _pack_rows(parts):
    blk = jnp.concatenate([_as_rows(p) for p in parts], axis=0)
    return jnp.pad(blk, ((0, -blk.shape[0] % 8), (0, 0)))


def _unpack_rows(blk, shapes):
    out, r = [], 0
    for shp in shapes:
        size = int(np.prod(shp))
        rows = -(-size // SMALL_PACK_C)
        out.append(blk[..., r:r + rows, :].reshape(blk.shape[:-2] + (rows * SMALL_PACK_C,))[..., :size]
                   .reshape(blk.shape[:-2] + tuple(shp)))
        r += rows
    return out


def kernel(x, c, w_ada, b_ada, norm1_g, norm2_g, w_in, rel_table, w_conv, a_log, dt_bias, gdn_norm_g, w_branch_a, w_branch_b, w_out, w_ffn_in, w_ffn_out, final_g, loss_target, m_w_ada, m_b_ada, m_norm1_g, m_norm2_g, m_w_in, m_rel_table, m_w_conv, m_a_log, m_dt_bias, m_gdn_norm_g, m_w_branch_a, m_w_branch_b, m_w_out, m_w_ffn_in, m_w_ffn_out, m_final_g, v_w_ada, v_b_ada, v_norm1_g, v_norm2_g, v_w_in, v_rel_table, v_w_conv, v_a_log, v_dt_bias, v_gdn_norm_g, v_w_branch_a, v_w_branch_b, v_w_out, v_w_ffn_in, v_w_ffn_out, v_final_g):
    weights = dict(w_ada=w_ada, b_ada=b_ada, norm1_g=norm1_g, norm2_g=norm2_g, w_in=w_in, rel_table=rel_table,
                   w_conv=w_conv, a_log=a_log, dt_bias=dt_bias, gdn_norm_g=gdn_norm_g, w_branch_a=w_branch_a,
                   w_branch_b=w_branch_b, w_out=w_out, w_ffn_in=w_ffn_in, w_ffn_out=w_ffn_out, final_g=final_g)
    mom_m = dict(w_ada=m_w_ada, b_ada=m_b_ada, norm1_g=m_norm1_g, norm2_g=m_norm2_g, w_in=m_w_in,
                 rel_table=m_rel_table, w_conv=m_w_conv, a_log=m_a_log, dt_bias=m_dt_bias, gdn_norm_g=m_gdn_norm_g,
                 w_branch_a=m_w_branch_a, w_branch_b=m_w_branch_b, w_out=m_w_out, w_ffn_in=m_w_ffn_in,
                 w_ffn_out=m_w_ffn_out, final_g=m_final_g)
    mom_v = dict(w_ada=v_w_ada, b_ada=v_b_ada, norm1_g=v_norm1_g, norm2_g=v_norm2_g, w_in=v_w_in,
                 rel_table=v_rel_table, w_conv=v_w_conv, a_log=v_a_log, dt_bias=v_dt_bias, gdn_norm_g=v_gdn_norm_g,
                 w_branch_a=v_w_branch_a, w_branch_b=v_w_branch_b, w_out=v_w_out, w_ffn_in=v_w_ffn_in,
                 w_ffn_out=v_w_ffn_out, final_g=v_final_g)
    xi, yi, ci = _position()
    chip = 2 * xi + yi
    dev = 2 * chip + ci
    L, D = norm1_g.shape
    NMOD = b_ada.shape[1] // D
    ns = w_ada.shape[2]
    cs = w_conv.shape[2]

    first_blk = _pack_rows([c, w_conv])
    first_all = _allgather8(first_blk, "gather_c").reshape(8, first_blk.shape[0], SMALL_PACK_C)
    c_all, w_conv_all = _unpack_rows(first_all, [(D,), w_conv.shape])
    w_conv_full = w_conv_all.reshape(4, 2, L, CONV_K, cs)[:, 0].transpose(1, 2, 0, 3).reshape(L, CONV_K, 4 * cs)
    b_shard = lax.dynamic_slice_in_dim(b_ada, chip * ns, ns, axis=1).reshape(L, 1, ns)
    mod_shard = _ada_mod(c_all, w_ada, b_shard, "ada_mod")
    mod_all = _allgather8(mod_shard.reshape(L * 8, ns), "gather_mod").reshape(4, 2, L, 8, ns)
    mod = lax.dynamic_index_in_dim(mod_all[:, 0], dev, axis=2, keepdims=False)
    mod = mod.transpose(1, 0, 2).reshape(L, NMOD, D)

    shards = [weights[k].astype(BF16) for k in BIG]
    shards = [s.reshape(s.shape[0], 2, s.shape[1] // 2, s.shape[2]) for s in shards]
    col_stacked = lambda t: t.reshape(4, 2 * t.shape[2], t.shape[3])
    row_joined = lambda t: t.reshape(8 * t.shape[2], t.shape[3])

    def layer_weights(l, gathered):
        gd = dict(zip(BIG, gathered))
        return dict(w_in=_reorder_w_in(col_stacked(gd["w_in"])[None], f"w_in_cols_l{l}")[0],
                    w_branch_a=col_stacked(gd["w_branch_a"]), w_branch_b=col_stacked(gd["w_branch_b"]),
                    w_ffn_in=col_stacked(gd["w_ffn_in"]), w_out=row_joined(gd["w_out"]),
                    w_ffn_out=row_joined(gd["w_ffn_out"]))

    P = dict(norm1_g=norm1_g, norm2_g=norm2_g, w_conv=w_conv_full, a_log=a_log, dt_bias=dt_bias,
             gdn_norm_g=gdn_norm_g)
    big = _band_bias(rel_table, "band_bias")

    W = [layer_weights(0, _gather_weights(shards, 0, "gather_weights_l0"))]
    xc = x[0]
    saved = []
    for l in range(L):
        xc, sv, gathered = _layer_fwd(l, xc, mod[l], W[l], P, big[l], (shards, l + 1) if l + 1 < L else None)
        saved.append(sv)
        if l + 1 < L:
            W.append(layer_weights(l + 1, gathered))
    dx, loss_dev, dfinal = _loss_head(xc, final_g.reshape(1, D), loss_target[0], "loss_head")

    where = jnp.stack([ci, chip]).astype(jnp.int32)
    grads = [None] * L
    dmods = [None] * L
    shard_grads = {k: [None] * L for k in BIG}

    def finish_reduce_scatter(l, sums, from_chips):
        halves = [_sum_own_and_received(s_[1], r_, where, f"rs_sum_{k}_l{l}")
                  for k, s_, r_ in zip(BIG, sums, from_chips)]
        for k, t in zip(BIG, _rs_pair(halves, f"rs_pair_l{l}")):
            shard_grads[k][l] = t.reshape(2 * t.shape[1], t.shape[2])

    pending = None
    for l in reversed(range(L)):
        exchange = [s_[0] for s_ in pending] if pending is not None else ()
        dx, grads[l], dmods[l], received = _layer_bwd(l, dx, saved[l], mod[l], W[l], P, big[l], exchange)
        if pending is not None:
            finish_reduce_scatter(l + 1, pending, received)
        gs = [grads[l][k] for k in BIG]
        from_sibling = _rs_sibling([g_.astype(BF16) for g_ in gs], f"rs_sibling_l{l}")
        pending = [_pair_sums(g_, where, r_, f"rs_pair_sum_{k}_l{l}") for k, g_, r_ in zip(BIG, gs, from_sibling)]
    finish_reduce_scatter(0, pending, _rs_chips([s_[0] for s_ in pending], "rs_chips_l0"))
    dmod = jnp.stack(dmods)

    small = {k: jnp.stack([grads[l][k] for l in range(L)]) for k in SMALL}
    parts = [dmod] + [small[k] for k in SMALL] + [dfinal, loss_dev[0, :1]]
    small_blk = _pack_rows(parts)
    srows = small_blk.shape[0]
    small_all, small_sum = _allgather8(small_blk, "gather_small", reduce_rows=srows)
    shapes = [dmod.shape] + [small[k].shape for k in SMALL] + [(D,), (1,)]
    tot = _unpack_rows(small_sum, shapes)
    G = dict(zip(SMALL, tot[1:1 + len(SMALL)]))
    G["b_ada"] = tot[0].reshape(b_ada.shape)
    G["w_conv"] = lax.dynamic_slice_in_dim(G["w_conv"], chip * cs, cs, axis=2)
    G["final_g"] = tot[-2]
    loss = tot[-1][0]
    dmod_all = _unpack_rows(small_all.reshape(8, srows, SMALL_PACK_C), [dmod.shape])[0]
    dmod_cols = lax.dynamic_slice_in_dim(dmod_all, chip * ns, ns, axis=2).transpose(1, 0, 2)
    G["w_ada"] = _ada_wgrad(c_all, dmod_cols, "ada_wgrad")

    order = ["w_ada", "b_ada", "norm1_g", "norm2_g", "w_in", "rel_table", "w_conv", "a_log", "dt_bias", "gdn_norm_g",
             "w_branch_a", "w_branch_b", "w_out", "w_ffn_in", "w_ffn_out", "final_g"]
    deltas, new_m, new_v = {}, {}, {}
    for k in order:
        w = weights[k]
        if k in BIG:
            G[k], deltas[k], new_m[k], new_v[k] = _adamw_layers(w, shard_grads[k], mom_m[k], mom_v[k], f"adamw_{k}")
            continue
        as2d = (lambda t: t.reshape(1, -1)) if w.ndim == 1 else (lambda t: t)
        d_, m_, v_ = _adamw(as2d(w), as2d(G[k]), as2d(mom_m[k]), as2d(mom_v[k]), f"adamw_{k}")
        deltas[k], new_m[k], new_v[k] = d_.reshape(w.shape), m_.reshape(w.shape), v_.reshape(w.shape)
    return (loss, dx[None], *[G[k] for k in order], *[deltas[k] for k in order], *[new_m[k] for k in order],
            *[new_v[k] for k in order])
```

```python
import functools

import numpy as np
import jax
import jax.numpy as jnp
from jax import lax
from jax.experimental import pallas as pl
from jax.experimental.pallas import tpu as pltpu

F32 = jnp.float32
BF16 = jnp.bfloat16
HI = lax.Precision.HIGHEST
SOLVE_PREC = lax.Precision.HIGH
MESH = pl.DeviceIdType.MESH

EPS = 1e-6
CHUNK = 64
A_HEADS = 8
A_DH = 64
A_PAST = 8
A_MAX_REL = 128
B_HEADS = 4
B_DH = 128
CONV_K = 4
LANE = 128
QBLK = 4 * CHUNK
KSPAN = QBLK + A_PAST * CHUNK
NEG = -1e30

ADAM_LR = 0.001
ADAM_B1 = 0.9
ADAM_B2 = 0.999
ADAM_EPS = 1e-08
ADAM_WD = 0.01
ADAM_STEP = 10

P_QKVA, P_QKVB, P_GA, P_GB, P_Z, P_BA, P_END = 0, 1536, 3072, 4096, 5120, 5632, 5760
W_IN_SHARD = 1410


def _sigmoid(x):
    return 1.0 / (1.0 + jnp.exp(-x))


def _dot(a, b, ca, cb, prec):
    lead = a.ndim - 2
    batch = ((0,), (0,)) if lead else ((), ())
    return lax.dot_general(a, b, (((ca + lead,), (cb + lead,)), batch), precision=prec, preferred_element_type=F32)


def _nn(a, b, prec=None):
    return _dot(a, b, 1, 0, prec)


def _nt(a, b, prec=None):
    return _dot(a, b, 1, 1, prec)


def _tn(a, b, prec=None):
    return _dot(a, b, 0, 0, prec)


def _bnn(a, b):
    return _nn(a.astype(BF16), b.astype(BF16))


def _bnt(a, b):
    return _nt(a.astype(BF16), b.astype(BF16))


def _btn(a, b):
    return _tn(a.astype(BF16), b.astype(BF16))


def _pick(n, target, unit=LANE):
    best = None
    for t in range(unit, min(n, target) + 1, unit):
        if n % t == 0:
            best = t
    return best if best is not None else n


def _acc(ref, val, i):
    @pl.when(i == 0)
    def _():
        ref[...] = val

    @pl.when(i != 0)
    def _():
        ref[...] += val


def _arb(n):
    return pltpu.CompilerParams(dimension_semantics=("arbitrary",) * n)


def _par(n):
    return pltpu.CompilerParams(dimension_semantics=("parallel",) * n)


def _matmul(a, b, mode, out_dtype, name, tm=1024, tn=1024, tk=1024, layer=None, stacked=False, out_stacked=False,
            also_bf16=False):
    bs = b.shape[1:] if layer is not None else b.shape
    if mode == "nn":
        M, K = a.shape
        N = 4 * bs[2] if stacked else bs[1]
        if stacked:
            tn = bs[2]
    elif mode == "nt":
        M, K = a.shape
        N = bs[1] if stacked else bs[0]
        if stacked:
            tk = bs[2]
    else:
        K, M = a.shape
        N = bs[1]
        if out_stacked:
            tn = N // 4
    tm, tn, tk = _pick(M, tm), _pick(N, tn), _pick(K, tk)
    nk = K // tk
    lead = () if layer is None else (layer,)
    lead_blk = () if layer is None else (None,)
    if mode == "nn":
        a_spec = pl.BlockSpec((tm, tk), lambda i, j, k: (i, k))
        if stacked:
            b_spec = pl.BlockSpec(lead_blk + (None, tk, tn), lambda i, j, k: lead + (j, k, 0))
        else:
            b_spec = pl.BlockSpec(lead_blk + (tk, tn), lambda i, j, k: lead + (k, j))
        dot = _nn
    elif mode == "nt":
        a_spec = pl.BlockSpec((tm, tk), lambda i, j, k: (i, k))
        if stacked:
            b_spec = pl.BlockSpec(lead_blk + (None, tn, tk), lambda i, j, k: lead + (k, j, 0))
        else:
            b_spec = pl.BlockSpec(lead_blk + (tn, tk), lambda i, j, k: lead + (j, k))
        dot = _nt
    else:
        a_spec = pl.BlockSpec((tk, tm), lambda i, j, k: (k, i))
        b_spec = pl.BlockSpec((tk, tn), lambda i, j, k: (k, j))
        dot = _tn
    if out_stacked:
        o_spec = pl.BlockSpec((None, tm, tn), lambda i, j, k: (j, i, 0))
        o_shape = jax.ShapeDtypeStruct((4, M, tn), out_dtype)
    else:
        o_spec = pl.BlockSpec((tm, tn), lambda i, j, k: (i, j))
        o_shape = jax.ShapeDtypeStruct((M, N), out_dtype)

    def write(o_refs, val):
        for o_ref in o_refs:
            o_ref[...] = val.astype(o_ref.dtype)

    def body_single(a_ref, b_ref, *o_refs):
        write(o_refs, dot(a_ref[...], b_ref[...]))

    def body(a_ref, b_ref, *refs):
        o_refs, acc_ref = refs[:-1], refs[-1]
        k = pl.program_id(2)

        @pl.when(k == 0)
        def _():
            acc_ref[...] = jnp.zeros_like(acc_ref)

        acc_ref[...] += dot(a_ref[...], b_ref[...])

        @pl.when(k == nk - 1)
        def _():
            write(o_refs, acc_ref[...])

    o_shapes = [o_shape] + ([jax.ShapeDtypeStruct(o_shape.shape, BF16)] if also_bf16 else [])
    res = pl.pallas_call(
        body_single if nk == 1 else body, name=name, grid=(M // tm, N // tn, nk), in_specs=[a_spec, b_spec],
        out_specs=[o_spec] * len(o_shapes), out_shape=o_shapes,
        scratch_shapes=[] if nk == 1 else [pltpu.VMEM((tm, tn), F32)],
        compiler_params=pltpu.CompilerParams(dimension_semantics=("parallel", "parallel", "arbitrary")),
    )(a, b)
    return tuple(res) if also_bf16 else res[0]


def _rows(tm, n, col=0):
    return pl.BlockSpec((tm, n), lambda i: (i, col))


def _vec(n):
    return pl.BlockSpec((1, n), lambda i: (0, 0))


def _lnmod_fwd(x, g, sc, sh, name):
    S, D = x.shape
    tm = _pick(S, 512, 8)

    def body(x_ref, g_ref, sc_ref, sh_ref, o_ref):
        xv = x_ref[...]
        r = lax.rsqrt(jnp.mean(xv * xv, axis=-1, keepdims=True) + EPS)
        o_ref[...] = ((xv * r * g_ref[...]) * (1.0 + sc_ref[...]) + sh_ref[...]).astype(BF16)

    return pl.pallas_call(
        body, name=name, grid=(S // tm,),
        in_specs=[_rows(tm, D), _vec(D), _vec(D), _vec(D)], out_specs=_rows(tm, D),
        out_shape=jax.ShapeDtypeStruct((S, D), BF16), compiler_params=_par(1),
    )(x, g, sc, sh)


def _lnmod_bwd(dh, x, g, sc, dres, name):
    S, D = x.shape
    tm = _pick(S, 512, 8)

    def body(dh_ref, x_ref, g_ref, sc_ref, dres_ref, dx_ref, dsh_ref, dsc_ref, dg_ref):
        i = pl.program_id(0)
        xv = x_ref[...]
        dh_ = dh_ref[...]
        r = lax.rsqrt(jnp.mean(xv * xv, axis=-1, keepdims=True) + EPS)
        xhat = xv * r
        gv = g_ref[...]
        dn = dh_ * (1.0 + sc_ref[...])
        dxhat = dn * gv
        dx_ref[...] = dres_ref[...] + r * (dxhat - xhat * jnp.mean(dxhat * xhat, axis=-1, keepdims=True))
        _acc(dsh_ref, jnp.sum(dh_, axis=0, keepdims=True), i)
        _acc(dsc_ref, jnp.sum(dh_ * (xhat * gv), axis=0, keepdims=True), i)
        _acc(dg_ref, jnp.sum(dn * xhat, axis=0, keepdims=True), i)

    return pl.pallas_call(
        body, name=name, grid=(S // tm,),
        in_specs=[_rows(tm, D), _rows(tm, D), _vec(D), _vec(D), _rows(tm, D)],
        out_specs=[_rows(tm, D), _vec(D), _vec(D), _vec(D)],
        out_shape=[jax.ShapeDtypeStruct((S, D), F32)] + [jax.ShapeDtypeStruct((1, D), F32)] * 3,
        compiler_params=_arb(1),
    )(dh, x, g, sc, dres)


def _gate_fwd(x, y, gt, name):
    S, D = x.shape
    tm = _pick(S, 512, 8)

    def body(x_ref, y_ref, gt_ref, o_ref):
        o_ref[...] = x_ref[...] + gt_ref[...] * y_ref[...]

    return pl.pallas_call(
        body, name=name, grid=(S // tm,), in_specs=[_rows(tm, D), _rows(tm, D), _vec(D)], out_specs=_rows(tm, D),
        out_shape=jax.ShapeDtypeStruct((S, D), F32), compiler_params=_par(1),
    )(x, y, gt)


def _gate_bwd(dx, y, gt, name):
    S, D = dx.shape
    tm = _pick(S, 512, 8)

    def body(dx_ref, y_ref, gt_ref, dz_ref, dgt_ref):
        i = pl.program_id(0)
        d = dx_ref[...]
        dz_ref[...] = (d * gt_ref[...]).astype(BF16)
        _acc(dgt_ref, jnp.sum(d * y_ref[...], axis=0, keepdims=True), i)

    return pl.pallas_call(
        body, name=name, grid=(S // tm,), in_specs=[_rows(tm, D), _rows(tm, D), _vec(D)],
        out_specs=[_rows(tm, D), _vec(D)],
        out_shape=[jax.ShapeDtypeStruct((S, D), BF16), jax.ShapeDtypeStruct((1, D), F32)],
        compiler_params=_arb(1),
    )(dx, y, gt)


def _ffn_act_fwd(gu, name):
    S, H2 = gu.shape
    H = H2 // 2
    tm = _pick(S, 256, 8)

    def body(g_ref, u_ref, o_ref):
        gv = g_ref[...].astype(F32)
        o_ref[...] = (gv * _sigmoid(gv) * u_ref[...].astype(F32)).astype(BF16)

    return pl.pallas_call(
        body, name=name, grid=(S // tm,), in_specs=[_rows(tm, H, 0), _rows(tm, H, 1)], out_specs=_rows(tm, H),
        out_shape=jax.ShapeDtypeStruct((S, H), BF16), compiler_params=_par(1),
    )(gu, gu)


def _ffn_act_bwd(gu, dact, name):
    S, H2 = gu.shape
    H = H2 // 2
    tm = _pick(S, 256, 8)

    def body(g_ref, u_ref, da_ref, o_ref):
        gv = g_ref[...].astype(F32)
        s = _sigmoid(gv)
        da = da_ref[...].astype(F32)
        o_ref[:, :H] = (da * u_ref[...].astype(F32) * (s * (1.0 + gv * (1.0 - s)))).astype(BF16)
        o_ref[:, H:] = (da * (gv * s)).astype(BF16)

    return pl.pallas_call(
        body, name=name, grid=(S // tm,), in_specs=[_rows(tm, H, 0), _rows(tm, H, 1), _rows(tm, H)],
        out_specs=_rows(tm, H2), out_shape=jax.ShapeDtypeStruct((S, H2), BF16), compiler_params=_par(1),
    )(gu, gu, dact)


def _merge_fwd(proj, pa, pb, name):
    S, D = pa.shape
    tm = _pick(S, 512, 8)

    def body(ga_ref, gb_ref, pa_ref, pb_ref, o_ref):
        o_ref[...] = (_sigmoid(ga_ref[...]) * pa_ref[...].astype(F32)
                      + _sigmoid(gb_ref[...]) * pb_ref[...].astype(F32)).astype(BF16)

    return pl.pallas_call(
        body, name=name, grid=(S // tm,),
        in_specs=[_rows(tm, D, P_GA // D), _rows(tm, D, P_GB // D), _rows(tm, D), _rows(tm, D)],
        out_specs=_rows(tm, D), out_shape=jax.ShapeDtypeStruct((S, D), BF16), compiler_params=_par(1),
    )(proj, proj, pa, pb)


def _merge_bwd(proj, pa, pb, dm, name):
    S, D = pa.shape
    tm = _pick(S, 512, 8)

    def body(ga_ref, gb_ref, pa_ref, pb_ref, dm_ref, dg_ref, dpa_ref, dpb_ref):
        d = dm_ref[...].astype(F32)
        sa = _sigmoid(ga_ref[...])
        sb = _sigmoid(gb_ref[...])
        dg_ref[:, :D] = (d * pa_ref[...].astype(F32) * sa * (1.0 - sa)).astype(BF16)
        dg_ref[:, D:] = (d * pb_ref[...].astype(F32) * sb * (1.0 - sb)).astype(BF16)
        dpa_ref[...] = (d * sa).astype(BF16)
        dpb_ref[...] = (d * sb).astype(BF16)

    return pl.pallas_call(
        body, name=name, grid=(S // tm,),
        in_specs=[_rows(tm, D, P_GA // D), _rows(tm, D, P_GB // D), _rows(tm, D), _rows(tm, D), _rows(tm, D)],
        out_specs=[_rows(tm, 2 * D), _rows(tm, D), _rows(tm, D)],
        out_shape=[jax.ShapeDtypeStruct((S, 2 * D), BF16), jax.ShapeDtypeStruct((S, D), BF16),
                   jax.ShapeDtypeStruct((S, D), BF16)],
        compiler_params=_par(1),
    )(proj, proj, pa, pb, dm)


def _loss_head(x, g, target, name):
    S, D = x.shape
    tm = _pick(S, 512, 8)

    def body(x_ref, g_ref, t_ref, dx_ref, loss_ref, dg_ref):
        i = pl.program_id(0)
        xv = x_ref[...]
        gv = g_ref[...]
        r = lax.rsqrt(jnp.mean(xv * xv, axis=-1, keepdims=True) + EPS)
        xhat = xv * r
        err = xhat * gv - t_ref[...]
        part = 0.5 * jnp.sum(jnp.mean(err * err, axis=-1, keepdims=True), axis=0, keepdims=True)
        _acc(loss_ref, jnp.broadcast_to(part, (1, LANE)), i)
        dy = err * (1.0 / D)
        _acc(dg_ref, jnp.sum(dy * xhat, axis=0, keepdims=True), i)
        dxhat = dy * gv
        dx_ref[...] = r * (dxhat - xhat * jnp.mean(dxhat * xhat, axis=-1, keepdims=True))

    return pl.pallas_call(
        body, name=name, grid=(S // tm,), in_specs=[_rows(tm, D), _vec(D), _rows(tm, D)],
        out_specs=[_rows(tm, D), _vec(LANE), _vec(D)],
        out_shape=[jax.ShapeDtypeStruct((S, D), F32), jax.ShapeDtypeStruct((1, LANE), F32),
                   jax.ShapeDtypeStruct((1, D), F32)],
        compiler_params=_arb(1),
    )(x, g, target)


HEADS_PER_SLAB = LANE // A_DH
N_SLABS = A_HEADS // HEADS_PER_SLAB
SPAN_BLOCKS = KSPAN // QBLK


def _attn_specs(seg):
    q_spec = pl.BlockSpec((QBLK, LANE), lambda p, m: (m, seg[0] * N_SLABS + p))
    k_specs = [pl.BlockSpec((QBLK, LANE), functools.partial(
        lambda j, p, m: (jnp.maximum(m - (SPAN_BLOCKS - 1) + j, 0), seg[1] * N_SLABS + p), j)) for j in range(SPAN_BLOCKS)]
    v_specs = [pl.BlockSpec((QBLK, LANE), functools.partial(
        lambda j, p, m: (jnp.maximum(m - (SPAN_BLOCKS - 1) + j, 0), seg[2] * N_SLABS + p), j)) for j in range(SPAN_BLOCKS)]
    b_spec = pl.BlockSpec((HEADS_PER_SLAB, QBLK, KSPAN), lambda p, m: (p, 0, 0))
    return q_spec, k_specs, v_specs, b_spec


def _head_lanes(t, hh):
    lane = lax.broadcasted_iota(jnp.int32, t.shape, 1)
    return jnp.where((lane // A_DH) == hh, t, jnp.zeros_like(t))


def _front_mask(m):
    col = lax.broadcasted_iota(jnp.int32, (QBLK, KSPAN), 1)
    return jnp.where(col < (SPAN_BLOCKS - 1 - m) * QBLK, NEG, 0.0)


def _attn_probs(qh, k, bias, front):
    s = _nt(qh, k) * (A_DH ** -0.5) + (bias + front)
    p = jnp.exp(s - jnp.max(s, axis=-1, keepdims=True))
    return p * (1.0 / jnp.sum(p, axis=-1, keepdims=True))


def _grid_ends(nq):
    p, m = pl.program_id(0), pl.program_id(1)
    return (p == 0) & (m == 0), (p == N_SLABS // 2) & (m == 0), (p == N_SLABS - 1) & (m == nq - 1)


def _attn_fwd(proj, big, name, gather=None):
    S = proj.shape[0]
    q_spec, k_specs, v_specs, b_spec = _attn_specs((0, 1, 2))
    shards, layer = gather if gather is not None else ((), None)
    ng = len(shards)

    def body(q_ref, k0, k1, k2, v0, v1, v2, b_ref, *rest):
        srcs, o_ref, gouts, sems = rest[:ng], rest[ng], rest[ng + 1:2 * ng + 1], rest[2 * ng + 1:]
        done = _carry(_gather_phases(layer, srcs, gouts, *sems), *_grid_ends(S // QBLK)) if ng else None
        m = pl.program_id(1)
        q = q_ref[...].astype(BF16)
        k = jnp.concatenate([k0[...], k1[...], k2[...]], axis=0).astype(BF16)
        v = jnp.concatenate([v0[...], v1[...], v2[...]], axis=0).astype(BF16)
        front = _front_mask(m)
        outs = [_nn(_attn_probs(_head_lanes(q, hh), k, b_ref[hh], front).astype(BF16), v)
                for hh in range(HEADS_PER_SLAB)]
        lane = lax.broadcasted_iota(jnp.int32, (QBLK, LANE), 1)
        o_ref[...] = jnp.where(lane < A_DH, outs[0], outs[1]).astype(BF16)
        if ng:
            done()

    res = pl.pallas_call(
        body, name=name, grid=(N_SLABS, S // QBLK), in_specs=[q_spec] + k_specs + v_specs + [b_spec] + [HBM] * ng,
        out_specs=[pl.BlockSpec((QBLK, LANE), lambda p, m: (m, p))] + [HBM] * ng,
        out_shape=[jax.ShapeDtypeStruct((S, A_HEADS * A_DH), BF16)] + _gather_out_shapes(shards),
        scratch_shapes=_gather_scratch(ng) if ng else [], compiler_params=_arb(2),
    )(proj, proj, proj, proj, proj, proj, proj, big, *shards)
    return res[0], list(res[1:])


def _attn_bwd(proj, big, dya, name, exchange=()):
    S = proj.shape[0]
    W = A_HEADS * A_DH
    q_spec, k_specs, v_specs, b_spec = _attn_specs((0, 1, 2))
    out_q = pl.BlockSpec((QBLK, LANE), lambda p, m: (m, p))
    out_kv = pl.BlockSpec((S, LANE), lambda p, m: (0, p))
    ne = len(exchange)

    def body(q_ref, k0, k1, k2, v0, v1, v2, b_ref, do_ref, *rest):
        srcs, (dq_ref, dk_ref, dv_ref, db_ref), eouts, sems = rest[:ne], rest[ne:ne + 4], rest[ne + 4:2 * ne + 4], rest[2 * ne + 4:]
        done = _carry(_chips_phases(srcs, eouts, *sems), *_grid_ends(S // QBLK)) if ne else None
        m = pl.program_id(1)

        @pl.when(m == 0)
        def _():
            dk_ref[...] = jnp.zeros_like(dk_ref)
            dv_ref[...] = jnp.zeros_like(dv_ref)
            db_ref[...] = jnp.zeros_like(db_ref)

        q = q_ref[...].astype(BF16)
        k = jnp.concatenate([k0[...], k1[...], k2[...]], axis=0).astype(BF16)
        v = jnp.concatenate([v0[...], v1[...], v2[...]], axis=0).astype(BF16)
        do = do_ref[...]
        dqs = []
        dk = jnp.zeros((KSPAN, LANE), F32)
        dv = jnp.zeros((KSPAN, LANE), F32)
        front = _front_mask(m)
        for hh in range(HEADS_PER_SLAB):
            qh = _head_lanes(q, hh)
            doh = _head_lanes(do, hh)
            p = _attn_probs(qh, k, b_ref[hh], front)
            dp = _nt(doh, v)
            ds = p * (dp - jnp.sum(p * dp, axis=-1, keepdims=True))
            db_ref[hh] += ds
            dsb = (ds * (A_DH ** -0.5)).astype(BF16)
            dqs.append(_nn(dsb, k))
            dk = dk + _tn(dsb, qh)
            dv = dv + _tn(p.astype(BF16), doh)
        lane = lax.broadcasted_iota(jnp.int32, (QBLK, LANE), 1)
        dq_ref[...] = jnp.where(lane < A_DH, dqs[0], dqs[1])
        for j in range(SPAN_BLOCKS):
            blk = m - (SPAN_BLOCKS - 1) + j

            @pl.when(blk >= 0)
            def _():
                off = pl.multiple_of(blk * QBLK, QBLK)
                dk_ref[pl.ds(off, QBLK), :] += dk[j * QBLK:(j + 1) * QBLK]
                dv_ref[pl.ds(off, QBLK), :] += dv[j * QBLK:(j + 1) * QBLK]
        if ne:
            done()

    res = pl.pallas_call(
        body, name=name, grid=(N_SLABS, S // QBLK),
        in_specs=[q_spec] + k_specs + v_specs + [b_spec, pl.BlockSpec((QBLK, LANE), lambda p, m: (m, p))] + [HBM] * ne,
        out_specs=[out_q, out_kv, out_kv, b_spec] + [HBM] * ne,
        out_shape=[jax.ShapeDtypeStruct((S, W), F32)] * 3 + [jax.ShapeDtypeStruct((A_HEADS, QBLK, KSPAN), F32)]
        + _chips_out_shapes(exchange),
        scratch_shapes=_chips_scratch(ne) if ne else [], compiler_params=_arb(2),
    )(proj, proj, proj, proj, proj, proj, proj, big, dya, *exchange)
    return tuple(res[:4]) + (list(res[4:]),)


NREL_PAD = 3 * LANE
SKEW_W = 1024


def _rel_table_grad(dbig, name):
    H, R, C = dbig.shape

    def body(d_ref, o_ref):
        x = jnp.concatenate([d_ref[...], jnp.zeros((R, SKEW_W - C), F32)], axis=1)
        row = lax.broadcasted_iota(jnp.int32, (R, SKEW_W), 0)
        for b in range(R.bit_length() - 1):
            x = jnp.where(((row >> b) & 1) == 1, pltpu.roll(x, SKEW_W - (1 << b), 1), x)
        e = jnp.sum(x, axis=0, keepdims=True)
        xi = lax.broadcasted_iota(jnp.int32, (SKEW_W, NREL_PAD), 0)
        r = lax.broadcasted_iota(jnp.int32, (SKEW_W, NREL_PAD), 1)
        diag = jnp.where(xi < C, xi, xi - SKEW_W)
        rel = jnp.clip(A_PAST * CHUNK - diag, -A_MAX_REL, A_MAX_REL) + A_MAX_REL
        o_ref[...] = _nn(e, jnp.where(rel == r, 1.0, 0.0).astype(F32), HI)

    return pl.pallas_call(
        body, name=name, grid=(H,), in_specs=[pl.BlockSpec((None, R, C), lambda h: (h, 0, 0))],
        out_specs=pl.BlockSpec((None, 1, NREL_PAD), lambda h: (h, 0, 0)),
        out_shape=jax.ShapeDtypeStruct((H, 1, NREL_PAD), F32), compiler_params=_par(1),
    )(dbig)


def _chunk_cumsum_matrix(n, reverse):
    j = lax.broadcasted_iota(jnp.int32, (n, n), 0)
    i = lax.broadcasted_iota(jnp.int32, (n, n), 1)
    same = (j // CHUNK) == (i // CHUNK)
    return jnp.where(same & ((j >= i) if reverse else (j <= i)), 1.0, 0.0).astype(F32)


def _gdn_gates_fwd(b_t, a_t, alog, dtb, name):
    Hh, S = b_t.shape
    tl = _pick(S, 512)
    row = pl.BlockSpec((Hh, tl), lambda i: (0, i))
    col = pl.BlockSpec((Hh, 1), lambda i: (0, 0))

    def body(b_ref, a_ref, al_ref, dt_ref, beta_ref, gam_ref):
        z = a_ref[...] + dt_ref[...]
        sp = jnp.maximum(z, 0.0) + jnp.log(1.0 + jnp.exp(-jnp.abs(z)))
        g = -jnp.exp(al_ref[...]) * sp
        beta_ref[...] = _sigmoid(b_ref[...])
        gam_ref[...] = _nn(g, _chunk_cumsum_matrix(tl, False), HI)

    return pl.pallas_call(
        body, name=name, grid=(S // tl,), in_specs=[row, row, col, col], out_specs=[row, row],
        out_shape=[jax.ShapeDtypeStruct((Hh, S), F32)] * 2, compiler_params=_par(1),
    )(b_t, a_t, alog, dtb)


def _gdn_gates_bwd(dbeta, dgam_a, dgam_b, b_t, a_t, alog, dtb, name):
    Hh, S = b_t.shape
    tl = _pick(S, 512)
    row = pl.BlockSpec((Hh, tl), lambda i: (0, i))
    col = pl.BlockSpec((Hh, 1), lambda i: (0, 0))
    accs = pl.BlockSpec((Hh, LANE), lambda i: (0, 0))

    def body(dbeta_ref, dga_ref, dgb_ref, b_ref, a_ref, al_ref, dt_ref, db_ref, da_ref, dal_ref, ddt_ref):
        i = pl.program_id(0)
        z = a_ref[...] + dt_ref[...]
        sp = jnp.maximum(z, 0.0) + jnp.log(1.0 + jnp.exp(-jnp.abs(z)))
        ea = jnp.exp(al_ref[...])
        dg = _nn(dga_ref[...] + dgb_ref[...], _chunk_cumsum_matrix(tl, True), HI)
        da = dg * (-ea) * _sigmoid(z)
        beta = _sigmoid(b_ref[...])
        db_ref[...] = dbeta_ref[...] * beta * (1.0 - beta)
        da_ref[...] = da
        _acc(dal_ref, jnp.broadcast_to(jnp.sum(dg * (-ea * sp), axis=1, keepdims=True), (Hh, LANE)), i)
        _acc(ddt_ref, jnp.broadcast_to(jnp.sum(da, axis=1, keepdims=True), (Hh, LANE)), i)

    return pl.pallas_call(
        body, name=name, grid=(S // tl,), in_specs=[row] * 5 + [col, col], out_specs=[row, row, accs, accs],
        out_shape=[jax.ShapeDtypeStruct((Hh, S), F32)] * 2 + [jax.ShapeDtypeStruct((Hh, LANE), F32)] * 2,
        compiler_params=_arb(1),
    )(dbeta, dgam_a, dgam_b, b_t, a_t, alog, dtb)


HALO = 8


def _conv_silu(xx_ref, w_ref, tm):
    y = w_ref[0:1, :] * xx_ref[pl.ds(HALO - CONV_K + 1, tm), :]
    for j in range(1, CONV_K):
        y = y + w_ref[j:j + 1, :] * xx_ref[pl.ds(HALO - CONV_K + 1 + j, tm), :]
    return y, y * _sigmoid(y)


def _fill_prev_halo(xx_ref, x_ref, prev_ref, i, tm):
    xx_ref[pl.ds(HALO, tm), :] = x_ref[...]

    @pl.when(i == 0)
    def _():
        xx_ref[pl.ds(0, HALO), :] = jnp.zeros((HALO, xx_ref.shape[1]), F32)

    @pl.when(i != 0)
    def _():
        xx_ref[pl.ds(0, HALO), :] = prev_ref[...]


def _gdn_pre_specs(tm, C, colblk):
    cur = pl.BlockSpec((tm, C), lambda i: (i, colblk))
    prev = pl.BlockSpec((HALO, C), lambda i: (jnp.maximum(i * (tm // HALO) - 1, 0), colblk))
    return cur, prev


def _gdn_pre_fwd(proj, wconv, name):
    S = proj.shape[0]
    C = 3 * B_HEADS * B_DH
    W = B_HEADS * B_DH
    tm = _pick(S, 256, 8)
    cur, prev = _gdn_pre_specs(tm, C, P_QKVB // C)

    def body(x_ref, prev_ref, w_ref, q_ref, k_ref, v_ref, xx_ref):
        i = pl.program_id(0)
        _fill_prev_halo(xx_ref, x_ref, prev_ref, i, tm)
        _, sl = _conv_silu(xx_ref, w_ref, tm)
        for h in range(B_HEADS):
            hs = slice(h * B_DH, (h + 1) * B_DH)
            q = sl[:, h * B_DH:(h + 1) * B_DH]
            k = sl[:, W + h * B_DH:W + (h + 1) * B_DH]
            q_ref[:, hs] = q * (lax.rsqrt(jnp.sum(q * q, axis=-1, keepdims=True) + EPS) * (B_DH ** -0.5))
            k_ref[:, hs] = k * lax.rsqrt(jnp.sum(k * k, axis=-1, keepdims=True) + EPS)
        v_ref[...] = sl[:, 2 * W:]

    return pl.pallas_call(
        body, name=name, grid=(S // tm,), in_specs=[cur, prev, pl.BlockSpec((CONV_K, C), lambda i: (0, 0))],
        out_specs=[_rows(tm, W)] * 3, out_shape=[jax.ShapeDtypeStruct((S, W), F32)] * 3,
        scratch_shapes=[pltpu.VMEM((HALO + tm, C), F32)], compiler_params=_par(1),
    )(proj, proj, wconv)


def _gdn_pre_bwd_a(proj, wconv, dqn, dkn, dv, name):
    S = proj.shape[0]
    C = 3 * B_HEADS * B_DH
    W = B_HEADS * B_DH
    tm = _pick(S, 256, 8)
    cur, prev = _gdn_pre_specs(tm, C, P_QKVB // C)

    def body(x_ref, prev_ref, w_ref, dq_ref, dk_ref, dv_ref, dy_ref, xx_ref):
        i = pl.program_id(0)
        _fill_prev_halo(xx_ref, x_ref, prev_ref, i, tm)
        y, sl = _conv_silu(xx_ref, w_ref, tm)
        sg = _sigmoid(y)
        dsilu = sg * (1.0 + y * (1.0 - sg))
        for h in range(B_HEADS):
            for base, d_ref, c in ((0, dq_ref, B_DH ** -0.5), (W, dk_ref, 1.0)):
                lo = base + h * B_DH
                t = sl[:, lo:lo + B_DH]
                d = d_ref[:, h * B_DH:(h + 1) * B_DH]
                r = lax.rsqrt(jnp.sum(t * t, axis=-1, keepdims=True) + EPS)
                dt = (c * r) * (d - t * (r * r) * jnp.sum(d * t, axis=-1, keepdims=True))
                dy_ref[:, lo:lo + B_DH] = dt * dsilu[:, lo:lo + B_DH]
        dy_ref[:, 2 * W:] = dv_ref[...] * dsilu[:, 2 * W:]

    return pl.pallas_call(
        body, name=name, grid=(S // tm,),
        in_specs=[cur, prev, pl.BlockSpec((CONV_K, C), lambda i: (0, 0))] + [_rows(tm, W)] * 3,
        out_specs=_rows(tm, C), out_shape=jax.ShapeDtypeStruct((S, C), F32),
        scratch_shapes=[pltpu.VMEM((HALO + tm, C), F32)], compiler_params=_par(1),
    )(proj, proj, wconv, dqn, dkn, dv)


def _gdn_pre_bwd_b(proj, wconv, dy, name):
    S = proj.shape[0]
    C = 3 * B_HEADS * B_DH
    tm = _pick(S, 256, 8)
    nt_ = S // tm
    cur, prev = _gdn_pre_specs(tm, C, P_QKVB // C)
    nxt = pl.BlockSpec((HALO, C), lambda i: (jnp.minimum((i + 1) * (tm // HALO), S // HALO - 1), 0))

    def body(x_ref, prev_ref, w_ref, dy_ref, next_ref, dx_ref, dw_ref, xx_ref, dd_ref):
        i = pl.program_id(0)
        _fill_prev_halo(xx_ref, x_ref, prev_ref, i, tm)
        dyv = dy_ref[...]
        dd_ref[pl.ds(0, tm), :] = dyv

        @pl.when(i == nt_ - 1)
        def _():
            dd_ref[pl.ds(tm, HALO), :] = jnp.zeros((HALO, C), F32)

        @pl.when(i != nt_ - 1)
        def _():
            dd_ref[pl.ds(tm, HALO), :] = next_ref[...]

        dx = w_ref[0:1, :] * dd_ref[pl.ds(CONV_K - 1, tm), :]
        for j in range(1, CONV_K):
            dx = dx + w_ref[j:j + 1, :] * dd_ref[pl.ds(CONV_K - 1 - j, tm), :]
        dx_ref[...] = dx.astype(BF16)
        dw = jnp.concatenate(
            [jnp.sum(dyv * xx_ref[pl.ds(HALO - CONV_K + 1 + j, tm), :], axis=0, keepdims=True) for j in range(CONV_K)],
            axis=0)
        _acc(dw_ref, dw, i)

    return pl.pallas_call(
        body, name=name, grid=(nt_,),
        in_specs=[cur, prev, pl.BlockSpec((CONV_K, C), lambda i: (0, 0)), _rows(tm, C), nxt],
        out_specs=[_rows(tm, C), pl.BlockSpec((CONV_K, C), lambda i: (0, 0))],
        out_shape=[jax.ShapeDtypeStruct((S, C), BF16), jax.ShapeDtypeStruct((CONV_K, C), F32)],
        scratch_shapes=[pltpu.VMEM((HALO + tm, C), F32), pltpu.VMEM((tm + HALO, C), F32)],
        compiler_params=_arb(1),
    )(proj, proj, wconv, dy, dy)


def _chunk_masks():
    row = lax.broadcasted_iota(jnp.int32, (CHUNK, CHUNK), 0)
    col = lax.broadcasted_iota(jnp.int32, (CHUNK, CHUNK), 1)
    return row >= col, row > col


def _chunk_local(q, k, vv, bc, gc, gr, tri):
    dm = jnp.where(tri, jnp.exp(jnp.where(tri, gc - gr, 0.0)), 0.0)
    kk = _bnt(k, k)
    glast = gr[..., CHUNK - 1:CHUNK]
    ep = jnp.exp(gc)
    em = jnp.exp(glast - gc)
    el = jnp.exp(glast)
    return dm, kk, ep, em, el, vv * bc, k * (bc * ep)


def _unit_lower_inverse(low):
    row = lax.broadcasted_iota(jnp.int32, (CHUNK, CHUNK), 0)
    col = lax.broadcasted_iota(jnp.int32, (CHUNK, CHUNK), 1)
    p = -low
    t = jnp.where(row == col, 1.0, 0.0).astype(F32) + p
    steps = CHUNK.bit_length() - 2
    for _ in range(steps):
        p = _nn(p, p, SOLVE_PREC)
        t = t + _nn(t, p, SOLVE_PREC)
    return t


GROUP = 4


def _carry(phases, first, middle, last):
    if len(phases) == 3:
        pl.when(first)(phases[0])
        pl.when(middle)(phases[1])
        return lambda: pl.when(last)(phases[2])
    pl.when(first)(phases[0])
    return lambda: pl.when(last)(phases[1])


def _pairs(nchunks):
    return [(c, h) for c in range(nchunks) for h in range(B_HEADS)]


def _tok(c):
    return slice(c * CHUNK, (c + 1) * CHUNK)


def _head(h):
    return slice(h * B_DH, (h + 1) * B_DH)


def _stack_tokens(ref, nchunks):
    return jnp.stack([ref[_tok(c), _head(h)] for c, h in _pairs(nchunks)])


def _stack_cols(ref, nchunks):
    per_chunk = [ref[c] for c in range(nchunks)] if len(ref.shape) == 3 else [ref[...]]
    return jnp.stack([per_chunk[c][:, h:h + 1] for c, h in _pairs(nchunks)])


def _stack_rows(ref, nchunks):
    if len(ref.shape) == 3:
        return jnp.stack([ref[c, h:h + 1, :] for c, h in _pairs(nchunks)])
    return jnp.stack([ref[h:h + 1, :] for _, h in _pairs(1)])


def _gdn_group_specs(ng_steps, W):
    tok = pl.BlockSpec((GROUP * CHUNK, W), lambda i: (i, 0))
    colv = pl.BlockSpec((GROUP, CHUNK, B_HEADS), lambda i: (i, 0, 0))
    rowv = pl.BlockSpec((GROUP, B_HEADS, CHUNK), lambda i: (i, 0, 0))
    mat = pl.BlockSpec((GROUP, B_HEADS, CHUNK, CHUNK), lambda i: (i, 0, 0, 0))
    return tok, colv, rowv, mat


def _gdn_local_fwd(qn, kn, v, bcol, gcol, grow, name, gather=None):
    S, Wd = qn.shape
    nc = S // CHUNK
    steps = nc // GROUP
    tok, colv, rowv, mat = _gdn_group_specs(steps, Wd)
    shards, layer = gather if gather is not None else ((), None)
    ng = len(shards)

    def body(q_ref, k_ref, v_ref, bc_ref, gc_ref, gr_ref, *rest):
        srcs, (t_ref, a_ref, u_ref, w_ref), gouts, sems = rest[:ng], rest[ng:ng + 4], rest[ng + 4:2 * ng + 4], rest[2 * ng + 4:]
        i = pl.program_id(0)
        done = _carry(_gather_phases(layer, srcs, gouts, *sems), i == 0, i == steps // 2, i == steps - 1) if ng else None
        tri, strict = _chunk_masks()
        q, k, vv = (_stack_tokens(r, GROUP) for r in (q_ref, k_ref, v_ref))
        bc, gc, gr = _stack_cols(bc_ref, GROUP), _stack_cols(gc_ref, GROUP), _stack_rows(gr_ref, GROUP)
        dm, kk, ep, em, el, vb, kb = _chunk_local(q, k, vv, bc, gc, gr, tri)
        t = _unit_lower_inverse(jnp.where(strict, bc * kk * dm, 0.0))
        a = _bnt(q, k) * dm
        u = _nn(t, vb, SOLVE_PREC)
        w = _nn(t, kb, SOLVE_PREC)
        for n, (c, h) in enumerate(_pairs(GROUP)):
            t_ref[c, h] = t[n]
            a_ref[c, h] = a[n]
            u_ref[_tok(c), _head(h)] = u[n]
            w_ref[_tok(c), _head(h)] = w[n]
        if ng:
            done()

    res = pl.pallas_call(
        body, name=name, grid=(steps,), in_specs=[tok, tok, tok, colv, colv, rowv] + [HBM] * ng,
        out_specs=[mat, mat, tok, tok] + [HBM] * ng,
        out_shape=[jax.ShapeDtypeStruct((nc, B_HEADS, CHUNK, CHUNK), F32)] * 2 + [jax.ShapeDtypeStruct((S, Wd), F32)] * 2
        + _gather_out_shapes(shards),
        scratch_shapes=_gather_scratch(ng) if ng else [], compiler_params=_arb(1),
    )(qn, kn, v, bcol, gcol, grow, *shards)
    return res[0], res[1], res[2], res[3], list(res[4:])


def _scan_decays(gc, gr):
    glast = gr[..., CHUNK - 1:CHUNK]
    return jnp.exp(gc), jnp.exp(glast - gc), jnp.exp(glast)


def _gdn_scan_specs(nc, rev):
    idx = (lambda i: nc - 1 - i) if rev else (lambda i: i)
    W = B_HEADS * B_DH
    tok = pl.BlockSpec((CHUNK, W), lambda i: (idx(i), 0))
    colv = pl.BlockSpec((None, CHUNK, B_HEADS), lambda i: (idx(i), 0, 0))
    rowv = pl.BlockSpec((None, B_HEADS, CHUNK), lambda i: (idx(i), 0, 0))
    mat = pl.BlockSpec((None, B_HEADS, CHUNK, CHUNK), lambda i: (idx(i), 0, 0, 0))
    smat = pl.BlockSpec((None, B_HEADS, B_DH, B_DH), lambda i: (idx(i), 0, 0, 0))
    return tok, colv, rowv, mat, smat


def _gdn_scan_fwd(qn, kn, u, w, a, gcol, grow, name, gather=None):
    S, Wd = qn.shape
    nc = S // CHUNK
    tok, colv, rowv, mat, smat = _gdn_scan_specs(nc, False)
    shards, layer = gather if gather is not None else ((), None)
    ng = len(shards)

    def body(q_ref, k_ref, u_ref, w_ref, a_ref, gc_ref, gr_ref, *rest):
        srcs, (o_ref, sh_ref), gouts = rest[:ng], rest[ng:ng + 2], rest[ng + 2:2 * ng + 2]
        st_ref, sems = rest[2 * ng + 2], rest[2 * ng + 3:]
        i = pl.program_id(0)
        done = _carry(_gather_phases(layer, srcs, gouts, *sems), i == 0, i == nc // 2, i == nc - 1) if ng else None

        @pl.when(i == 0)
        def _():
            st_ref[...] = jnp.zeros_like(st_ref)

        ep, em, el = _scan_decays(_stack_cols(gc_ref, 1), _stack_rows(gr_ref, 1))
        q, k, u, w = (_stack_tokens(r, 1) for r in (q_ref, k_ref, u_ref, w_ref))
        s0 = st_ref[...]
        ut = u - _bnn(w, s0)
        o = _bnn(q * ep, s0) + _bnn(a_ref[...], ut)
        st_ref[...] = el * s0 + _btn(k * em, ut)
        sh_ref[...] = s0
        for h in range(B_HEADS):
            o_ref[:, _head(h)] = o[h]
        if ng:
            done()

    res = pl.pallas_call(
        body, name=name, grid=(nc,), in_specs=[tok, tok, tok, tok, mat, colv, rowv] + [HBM] * ng,
        out_specs=[tok, smat] + [HBM] * ng,
        out_shape=[jax.ShapeDtypeStruct((S, Wd), F32), jax.ShapeDtypeStruct((nc, B_HEADS, B_DH, B_DH), F32)]
        + _gather_out_shapes(shards),
        scratch_shapes=[pltpu.VMEM((B_HEADS, B_DH, B_DH), F32)] + (_gather_scratch(ng) if ng else []),
        compiler_params=_arb(1),
    )(qn, kn, u, w, a, gcol, grow, *shards)
    return res[0], res[1], list(res[2:])


def _gdn_scan_bwd(qn, kn, u, w, a, gcol, grow, ssave, do, name):
    S, Wd = qn.shape
    nc = S // CHUNK
    tok, colv, rowv, mat, smat = _gdn_scan_specs(nc, True)

    def body(q_ref, k_ref, u_ref, w_ref, a_ref, gc_ref, gr_ref, sh_ref, do_ref,
             du_ref, dw_ref, dqd_ref, dkd_ref, da_ref, dgl_ref, ds_ref):
        i = pl.program_id(0)

        @pl.when(i == 0)
        def _():
            ds_ref[...] = jnp.zeros_like(ds_ref)

        tri, _ = _chunk_masks()
        sub4 = lax.broadcasted_iota(jnp.int32, (B_HEADS, CHUNK), 0)
        lane_last = lax.broadcasted_iota(jnp.int32, (1, CHUNK), 1) == CHUNK - 1
        ep, em, el = _scan_decays(_stack_cols(gc_ref, 1), _stack_rows(gr_ref, 1))
        q, k, u, w, dout = (_stack_tokens(r, 1) for r in (q_ref, k_ref, u_ref, w_ref, do_ref))
        s0 = sh_ref[...]
        ds = ds_ref[...]
        ut = u - _bnn(w, s0)
        dut = _btn(a_ref[...], dout) + _bnn(k * em, ds)
        ds_ref[...] = el * ds + _btn(q * ep, dout) - _btn(w, dut)
        dw = -_bnt(dut, s0)
        dqd = _bnt(dout, s0)
        dkd = _bnt(ut, ds)
        da_ref[...] = jnp.where(tri, _bnt(dout, ut), 0.0)
        d_el = jnp.sum(jnp.sum(s0 * ds, axis=-1, keepdims=True), axis=-2, keepdims=True)
        last = d_el * el
        dgl_acc = jnp.zeros((B_HEADS, CHUNK), F32)
        for h in range(B_HEADS):
            du_ref[:, _head(h)] = dut[h]
            dw_ref[:, _head(h)] = dw[h]
            dqd_ref[:, _head(h)] = dqd[h]
            dkd_ref[:, _head(h)] = dkd[h]
            dgl_acc = jnp.where(sub4 == h, jnp.where(lane_last, last[h], 0.0), dgl_acc)
        dgl_ref[...] = dgl_acc

    return pl.pallas_call(
        body, name=name, grid=(nc,), in_specs=[tok, tok, tok, tok, mat, colv, rowv, smat, tok],
        out_specs=[tok, tok, tok, tok, mat, rowv],
        out_shape=[jax.ShapeDtypeStruct((S, Wd), F32)] * 4 + [jax.ShapeDtypeStruct((nc, B_HEADS, CHUNK, CHUNK), F32),
                                                             jax.ShapeDtypeStruct((nc, B_HEADS, CHUNK), F32)],
        scratch_shapes=[pltpu.VMEM((B_HEADS, B_DH, B_DH), F32)], compiler_params=_arb(1),
    )(qn, kn, u, w, a, gcol, grow, ssave, do)


def _gdn_local_bwd(qn, kn, v, bcol, gcol, grow, tsave, du, dw, dqd, dkd, da, dgl, name, exchange=()):
    S, Wd = qn.shape
    nc = S // CHUNK
    steps = nc // GROUP
    tok, colv, rowv, mat = _gdn_group_specs(steps, Wd)
    ne = len(exchange)

    def body(q_ref, k_ref, v_ref, bc_ref, gc_ref, gr_ref, t_ref, du_ref, dw_ref, dqd_ref, dkd_ref, da_ref, dgl_ref, *rest):
        srcs, (dq_ref, dk_ref, dv_ref, dbc_ref, dgc_ref, dgr_ref) = rest[:ne], rest[ne:ne + 6]
        eouts, sems = rest[ne + 6:2 * ne + 6], rest[2 * ne + 6:]
        i = pl.program_id(0)
        done = _carry(_chips_phases(srcs, eouts, *sems), i == 0, None, i == steps - 1) if ne else None
        tri, strict = _chunk_masks()
        lane4 = lax.broadcasted_iota(jnp.int32, (CHUNK, B_HEADS), 1)
        sub4 = lax.broadcasted_iota(jnp.int32, (B_HEADS, CHUNK), 0)
        lane_last = lax.broadcasted_iota(jnp.int32, (1, CHUNK), 1) == CHUNK - 1
        q, k, vv, dut, dwv, dqd, dkd = (_stack_tokens(r, GROUP)
                                        for r in (q_ref, k_ref, v_ref, du_ref, dw_ref, dqd_ref, dkd_ref))
        bc, gc, gr = _stack_cols(bc_ref, GROUP), _stack_cols(gc_ref, GROUP), _stack_rows(gr_ref, GROUP)
        dm, kk, ep, em, el, vb, kb = _chunk_local(q, k, vv, bc, gc, gr, tri)
        t = jnp.stack([t_ref[c, h] for c, h in _pairs(GROUP)])
        dav = jnp.stack([da_ref[c, h] for c, h in _pairs(GROUP)])
        qk = _bnt(q, k)
        dt = _nt(dut, vb, SOLVE_PREC) + _nt(dwv, kb, SOLVE_PREC)
        dvb = _tn(t, dut, SOLVE_PREC)
        dkb = _tn(t, dwv, SOLVE_PREC)
        dl = jnp.where(strict, -_tn(t, _nt(dt, t, SOLVE_PREC), SOLVE_PREC), 0.0)
        g1 = dl * dm
        dkb_k = jnp.sum(dkb * k, axis=-1, keepdims=True)
        dbeta = jnp.sum(g1 * kk, axis=-1, keepdims=True) + jnp.sum(dvb * vv, axis=-1, keepdims=True) + dkb_k * ep
        dkk = g1 * bc
        ddm = dl * (bc * kk) + dav * qk
        dqk = dav * dm
        dq = _bnn(dqk, k) + dqd * ep
        dk = _btn(dqk, q) + _bnn(dkk, k) + _btn(dkk, k) + dkb * (bc * ep) + dkd * em
        dv = dvb * bc
        dep = dkb_k * bc + jnp.sum(dqd * q, axis=-1, keepdims=True)
        dem = jnp.sum(dkd * k, axis=-1, keepdims=True)
        mm = ddm * dm
        dgam_c = jnp.sum(mm, axis=-1, keepdims=True) + dep * ep - dem * em
        dglast = jnp.sum(dem * em, axis=-2, keepdims=True)
        dgam_r = -jnp.sum(mm, axis=-2, keepdims=True) + jnp.where(lane_last, dglast, 0.0)
        for c in range(GROUP):
            dbc_acc = jnp.zeros((CHUNK, B_HEADS), F32)
            dgc_acc = jnp.zeros((CHUNK, B_HEADS), F32)
            dgr_acc = jnp.zeros((B_HEADS, CHUNK), F32)
            for h in range(B_HEADS):
                n = c * B_HEADS + h
                dq_ref[_tok(c), _head(h)] = dq[n]
                dk_ref[_tok(c), _head(h)] = dk[n]
                dv_ref[_tok(c), _head(h)] = dv[n]
                dbc_acc = jnp.where(lane4 == h, dbeta[n], dbc_acc)
                dgc_acc = jnp.where(lane4 == h, dgam_c[n], dgc_acc)
                dgr_acc = jnp.where(sub4 == h, dgam_r[n], dgr_acc)
            dbc_ref[c] = dbc_acc
            dgc_ref[c] = dgc_acc
            dgr_ref[c] = dgr_acc + dgl_ref[c]
        if ne:
            done()

    res = pl.pallas_call(
        body, name=name, grid=(steps,),
        in_specs=[tok, tok, tok, colv, colv, rowv, mat, tok, tok, tok, tok, mat, rowv] + [HBM] * ne,
        out_specs=[tok, tok, tok, colv, colv, rowv] + [HBM] * ne,
        out_shape=[jax.ShapeDtypeStruct((S, Wd), F32)] * 3
        + [jax.ShapeDtypeStruct((nc, CHUNK, B_HEADS), F32)] * 2 + [jax.ShapeDtypeStruct((nc, B_HEADS, CHUNK), F32)]
        + _chips_out_shapes(exchange),
        scratch_shapes=_chips_scratch(ne) if ne else [], compiler_params=_arb(1),
    )(qn, kn, v, bcol, gcol, grow, tsave, du, dw, dqd, dkd, da, dgl, *exchange)
    return tuple(res[:6]) + (list(res[6:]),)


def _gdn_post_fwd(o, proj, ng, name):
    S, W = o.shape
    tm = _pick(S, 512, 8)

    def body(o_ref, z_ref, g_ref, y_ref):
        gv = g_ref[...]
        for h in range(B_HEADS):
            hs = slice(h * B_DH, (h + 1) * B_DH)
            oh = o_ref[:, hs]
            z = z_ref[:, hs]
            r = lax.rsqrt(jnp.mean(oh * oh, axis=-1, keepdims=True) + EPS)
            y_ref[:, hs] = (oh * r * gv * (z * _sigmoid(z))).astype(BF16)

    return pl.pallas_call(
        body, name=name, grid=(S // tm,), in_specs=[_rows(tm, W), _rows(tm, W, P_Z // W), _vec(B_DH)],
        out_specs=_rows(tm, W), out_shape=jax.ShapeDtypeStruct((S, W), BF16), compiler_params=_par(1),
    )(o, proj, ng)


def _gdn_post_bwd(dy, o, proj, ng, name):
    S, W = o.shape
    tm = _pick(S, 512, 8)

    def body(dy_ref, o_ref, z_ref, g_ref, do_ref, dz_ref, dg_ref):
        i = pl.program_id(0)
        gv = g_ref[...]
        dg = jnp.zeros((1, B_DH), F32)
        for h in range(B_HEADS):
            hs = slice(h * B_DH, (h + 1) * B_DH)
            oh = o_ref[:, hs]
            z = z_ref[:, hs]
            d = dy_ref[:, hs]
            r = lax.rsqrt(jnp.mean(oh * oh, axis=-1, keepdims=True) + EPS)
            n = oh * r
            sg = _sigmoid(z)
            sz = z * sg
            dn = d * gv * sz
            dg = dg + jnp.sum(d * n * sz, axis=0, keepdims=True)
            dz_ref[:, hs] = (d * n * gv * (sg * (1.0 + z * (1.0 - sg)))).astype(BF16)
            do_ref[:, hs] = r * (dn - n * jnp.mean(dn * n, axis=-1, keepdims=True))
        _acc(dg_ref, dg, i)

    return pl.pallas_call(
        body, name=name, grid=(S // tm,), in_specs=[_rows(tm, W), _rows(tm, W), _rows(tm, W, P_Z // W), _vec(B_DH)],
        out_specs=[_rows(tm, W), _rows(tm, W), _vec(B_DH)],
        out_shape=[jax.ShapeDtypeStruct((S, W), F32), jax.ShapeDtypeStruct((S, W), BF16),
                   jax.ShapeDtypeStruct((1, B_DH), F32)],
        compiler_params=_arb(1),
    )(dy, o, proj, ng)


def _ada_mod(c_all, w_ada, b_shard, name):
    L, D, Ns = w_ada.shape
    B = c_all.shape[0]

    def body(c_ref, w_ref, b_ref, o_ref):
        cv = c_ref[...]
        cond = (cv * _sigmoid(cv)).astype(BF16)
        o_ref[...] = _nn(cond, w_ref[...].astype(BF16)) + b_ref[...]

    return pl.pallas_call(
        body, name=name, grid=(L,),
        in_specs=[pl.BlockSpec((B, D), lambda l: (0, 0)), pl.BlockSpec((None, D, Ns), lambda l: (l, 0, 0)),
                  pl.BlockSpec((None, 1, Ns), lambda l: (l, 0, 0))],
        out_specs=pl.BlockSpec((None, B, Ns), lambda l: (l, 0, 0)),
        out_shape=jax.ShapeDtypeStruct((L, B, Ns), F32), compiler_params=_par(1),
    )(c_all, w_ada, b_shard)


def _ada_wgrad(c_all, dmod, name):
    L, B, Ns = dmod.shape
    D = c_all.shape[1]

    def body(c_ref, d_ref, o_ref):
        cv = c_ref[...]
        cond = (cv * _sigmoid(cv)).astype(BF16)
        o_ref[...] = _tn(cond, d_ref[...].astype(BF16))

    return pl.pallas_call(
        body, name=name, grid=(L,),
        in_specs=[pl.BlockSpec((B, D), lambda l: (0, 0)), pl.BlockSpec((None, B, Ns), lambda l: (l, 0, 0))],
        out_specs=pl.BlockSpec((None, D, Ns), lambda l: (l, 0, 0)),
        out_shape=jax.ShapeDtypeStruct((L, D, Ns), F32), compiler_params=_par(1),
    )(c_all, dmod)


W_IN_PIECES = ((0, 0, 1410), (1, 0, 1410), (2, 0, 252), (2, 772, 638), (3, 0, 1410), (2, 252, 512), (2, 764, 8))


def _reorder_w_in(w4, name):
    L, _, D, Cs = w4.shape
    tm = _pick(D, 256, 16)
    used = sum(p[2] for p in W_IN_PIECES)

    def body(w_ref, o_ref):
        shard = [w_ref[s] for s in range(4)]
        parts = [shard[s][:, lo:lo + n] for s, lo, n in W_IN_PIECES]
        o_ref[...] = jnp.concatenate(parts + [jnp.zeros((tm, P_END - used), w4.dtype)], axis=1)

    return pl.pallas_call(
        body, name=name, grid=(L, D // tm), in_specs=[pl.BlockSpec((None, 4, tm, Cs), lambda l, i: (l, 0, i, 0))],
        out_specs=pl.BlockSpec((None, tm, P_END), lambda l, i: (l, i, 0)),
        out_shape=jax.ShapeDtypeStruct((L, D, P_END), w4.dtype), compiler_params=_par(2),
    )(w4)


def _restore_w_in(g, name):
    D = g.shape[0]
    tm = _pick(D, 256, 16)

    def body(g_ref, o_ref, ob_ref):
        gv = g_ref[...]
        off = 0
        pieces = {}
        for s, lo, n in W_IN_PIECES:
            pieces.setdefault(s, []).append((lo, gv[:, off:off + n]))
            off += n
        for s, lst in pieces.items():
            lst.sort(key=lambda t: t[0])
            shard = lst[0][1] if len(lst) == 1 else jnp.concatenate([t[1] for t in lst], axis=1)
            o_ref[s] = shard
            ob_ref[s] = shard.astype(BF16)

    spec = pl.BlockSpec((4, tm, W_IN_SHARD), lambda i: (0, i, 0))
    return pl.pallas_call(
        body, name=name, grid=(D // tm,), in_specs=[pl.BlockSpec((tm, P_END), lambda i: (i, 0))],
        out_specs=[spec, spec],
        out_shape=[jax.ShapeDtypeStruct((4, D, W_IN_SHARD), g.dtype), jax.ShapeDtypeStruct((4, D, W_IN_SHARD), BF16)],
        compiler_params=_par(1),
    )(g)


def _adam_update(w, g, m, v):
    mn = ADAM_B1 * m + (1.0 - ADAM_B1) * g
    vn = ADAM_B2 * v + (1.0 - ADAM_B2) * (g * g)
    m_hat = mn / (1.0 - ADAM_B1 ** ADAM_STEP)
    v_hat = vn / (1.0 - ADAM_B2 ** ADAM_STEP)
    return -ADAM_LR * (m_hat / (jnp.sqrt(v_hat) + ADAM_EPS) + ADAM_WD * w), mn, vn


def _adamw(w, g, m, v, name):
    shape = w.shape
    C = shape[-1]
    R = w.size // C
    tm = _pick(R, 512, 8)
    spec = pl.BlockSpec((tm, C), lambda i: (i, 0))

    def body(w_ref, g_ref, m_ref, v_ref, d_ref, mo_ref, vo_ref):
        d_ref[...], mo_ref[...], vo_ref[...] = _adam_update(w_ref[...], g_ref[...], m_ref[...], v_ref[...])

    outs = pl.pallas_call(
        body, name=name, grid=(R // tm,), in_specs=[spec] * 4, out_specs=[spec] * 3,
        out_shape=[jax.ShapeDtypeStruct((R, C), F32)] * 3, compiler_params=_par(1),
    )(*(t.reshape(R, C) for t in (w, g, m, v)))
    return tuple(o.reshape(shape) for o in outs)


def _adamw_layers(w, gs, m, v, name):
    L, R, C = w.shape
    tm = _pick(R, 128, 8)
    spec = pl.BlockSpec((None, tm, C), lambda l, i: (l, i, 0))
    g_specs = [pl.BlockSpec((tm, C), functools.partial(lambda ll, l, i: (jnp.where(l == ll, i, 0), 0), ll))
               for ll in range(L)]

    def body(w_ref, m_ref, v_ref, *rest):
        g_refs, (go_ref, d_ref, mo_ref, vo_ref) = rest[:L], rest[L:]
        l = pl.program_id(0)
        for ll in range(L):
            @pl.when(l == ll)
            def _():
                g = g_refs[ll][...]
                go_ref[...] = g
                d_ref[...], mo_ref[...], vo_ref[...] = _adam_update(w_ref[...], g, m_ref[...], v_ref[...])

    return pl.pallas_call(
        body, name=name, grid=(L, R // tm), in_specs=[spec] * 3 + g_specs, out_specs=[spec] * 4,
        out_shape=[jax.ShapeDtypeStruct((L, R, C), F32)] * 4, compiler_params=_arb(2),
    )(w, m, v, *gs)


def _pair_sums(a, where, b, name):
    NB, _, R, C = a.shape

    def body(where_ref, a_ref, b_ref, p_ref, own_ref):
        s = a_ref[...] + b_ref[...].astype(F32)
        p_ref[...] = s.astype(BF16)

        @pl.when(pl.program_id(0) == where_ref[1])
        def _():
            own_ref[...] = s

    return pl.pallas_call(
        body, name=name,
        grid_spec=pltpu.PrefetchScalarGridSpec(
            num_scalar_prefetch=1, grid=(NB,),
            in_specs=[pl.BlockSpec((None, None, R, C), lambda k, w: (k, w[0], 0, 0)),
                      pl.BlockSpec((None, R, C), lambda k, w: (k, 0, 0))],
            out_specs=[pl.BlockSpec((None, R, C), lambda k, w: (k, 0, 0)), pl.BlockSpec((R, C), lambda k, w: (0, 0))]),
        out_shape=[jax.ShapeDtypeStruct((NB, R, C), BF16), jax.ShapeDtypeStruct((R, C), F32)],
        compiler_params=_arb(1),
    )(where, a, b)


def _sum_own_and_received(own, recv, where, name):
    R, C = own.shape
    tm = _pick(R, 256, 16)

    def body(where_ref, p_ref, r_ref, o_ref):
        o_ref[...] = ((p_ref[...] + r_ref[0].astype(F32)) + r_ref[1].astype(F32)) + r_ref[2].astype(F32)

    return pl.pallas_call(
        body, name=name,
        grid_spec=pltpu.PrefetchScalarGridSpec(
            num_scalar_prefetch=1, grid=(R // tm,),
            in_specs=[pl.BlockSpec((tm, C), lambda i, w: (i, 0)), pl.BlockSpec((3, tm, C), lambda i, w: (0, i, 0))],
            out_specs=pl.BlockSpec((None, tm, C), lambda i, w: (w[0], i, 0))),
        out_shape=jax.ShapeDtypeStruct((2, R, C), F32), compiler_params=_par(1),
    )(where, own, recv)


def _position():
    return lax.axis_index("x"), lax.axis_index("y"), lax.axis_index("c")


def _other_chips(x, y):
    return [(1 - x, y), (x, 1 - y), (1 - x, 1 - y)]


HBM = pl.BlockSpec(memory_space=pl.ANY)


def _allgather8(blk, name, reduce_rows=None):
    M, N = blk.shape

    def body(x_ref, out_ref, *rest):
        if reduce_rows is None:
            send_sems, recv_sems, local_sem = rest
        else:
            sum_ref, send_sems, recv_sems, local_sem = rest
        x, y, c = _position()
        me, sibling = (x, y, c), (x, y, 1 - c)
        chips = _other_chips(x, y)

        def rows(px, py, pc):
            return out_ref.at[pl.ds((4 * px + 2 * py + pc) * M, M), :]

        def copy(k, block, to, src=None):
            return pltpu.make_async_remote_copy(
                src_ref=rows(*block) if src is None else src, dst_ref=rows(*block),
                send_sem=send_sems.at[k], recv_sem=recv_sems.at[k], device_id=to, device_id_type=MESH)

        mine = pltpu.make_async_copy(x_ref, rows(*me), local_sem)
        mine.start()
        first = [copy(0, me, sibling, src=x_ref)]
        first += [copy(1 + j, me, (*chip, c), src=x_ref) for j, chip in enumerate(chips)]
        for cp in first:
            cp.start()
        passed = [copy(4 + j, (*chip, c), sibling) for j, chip in enumerate(chips)]
        for j, chip in enumerate(chips):
            copy(1 + j, (*chip, c), me).wait_recv()
            passed[j].start()
        copy(0, sibling, me).wait_recv()
        for j, chip in enumerate(chips):
            copy(4 + j, (*chip, 1 - c), me).wait_recv()
        for cp in first + passed:
            cp.wait_send()
        mine.wait()
        if reduce_rows is not None:
            tot = out_ref[pl.ds(0, reduce_rows), :]
            for d in range(1, 8):
                tot = tot + out_ref[pl.ds(d * M, reduce_rows), :]
            sum_ref[...] = tot

    vmem = pl.BlockSpec(memory_space=pltpu.VMEM)
    out_shape = [jax.ShapeDtypeStruct((8 * M, N), blk.dtype)]
    if reduce_rows is not None:
        out_shape.append(jax.ShapeDtypeStruct((reduce_rows, N), blk.dtype))
    res = pl.pallas_call(
        body, name=name, out_shape=out_shape, in_specs=[vmem], out_specs=[vmem] * len(out_shape),
        scratch_shapes=[pltpu.SemaphoreType.DMA((7,)), pltpu.SemaphoreType.DMA((7,)), pltpu.SemaphoreType.DMA],
    )(blk)
    return res[0] if reduce_rows is None else (res[0], res[1])


def _gather_phases(layer, srcs, outs, send_sems, recv_sems, local_sems):
    n = len(srcs)
    x, y, c = _position()
    me, sibling = (x, y, c), (x, y, 1 - c)
    chips = _other_chips(x, y)

    def region(t, px, py, pc):
        return outs[t].at[2 * px + py, pc]

    def copy(t, k, block, to, own=False):
        return pltpu.make_async_remote_copy(
            src_ref=srcs[t].at[layer, c] if own else region(t, *block), dst_ref=region(t, *block),
            send_sem=send_sems.at[7 * t + k], recv_sem=recv_sems.at[7 * t + k], device_id=to, device_id_type=MESH)

    def local(t):
        return pltpu.make_async_copy(srcs[t].at[layer, c], region(t, *me), local_sems.at[t])

    def first(t):
        return [copy(t, 0, me, sibling, own=True)] + [copy(t, 1 + j, me, (*chip, c), own=True)
                                                       for j, chip in enumerate(chips)]

    def start():
        for t in range(n):
            local(t).start()
        for t in range(n):
            for cp in first(t):
                cp.start()

    def forward():
        for j, chip in enumerate(chips):
            for t in range(n):
                copy(t, 1 + j, (*chip, c), me).wait_recv()
                copy(t, 4 + j, (*chip, c), sibling).start()

    def finish():
        for t in range(n):
            copy(t, 0, sibling, me).wait_recv()
        for j, chip in enumerate(chips):
            for t in range(n):
                copy(t, 4 + j, (*chip, 1 - c), me).wait_recv()
        for t in range(n):
            for cp in first(t) + [copy(t, 4 + j, (*chip, c), sibling) for j, chip in enumerate(chips)]:
                cp.wait_send()
            local(t).wait()

    return start, forward, finish


def _gather_scratch(n):
    return [pltpu.SemaphoreType.DMA((7 * n,)), pltpu.SemaphoreType.DMA((7 * n,)), pltpu.SemaphoreType.DMA((n,))]


def _gather_out_shapes(shards):
    return [jax.ShapeDtypeStruct((4,) + s.shape[1:], s.dtype) for s in shards]


def _gather_weights(shards, layer, name):
    n = len(shards)

    def body(*refs):
        start, forward, finish = _gather_phases(layer, refs[:n], refs[n:2 * n], *refs[2 * n:])
        start()
        forward()
        finish()

    return pl.pallas_call(
        body, name=name, out_shape=_gather_out_shapes(shards), in_specs=[HBM] * n, out_specs=[HBM] * n,
        scratch_shapes=_gather_scratch(n),
    )(*shards)


def _rs_sibling(gs, name):
    n = len(gs)

    def body(*refs):
        srcs, outs = refs[:n], refs[n:2 * n]
        send_sems, recv_sems = refs[2 * n:]
        x, y, c = _position()
        copies = [pltpu.make_async_remote_copy(
            src_ref=srcs[t].at[k, 1 - c], dst_ref=outs[t].at[k], send_sem=send_sems.at[4 * t + k],
            recv_sem=recv_sems.at[4 * t + k], device_id=(x, y, 1 - c), device_id_type=MESH)
            for t in range(n) for k in range(4)]
        for cp in copies:
            cp.start()
        for cp in copies:
            cp.wait()

    out_shape = [jax.ShapeDtypeStruct((4,) + g.shape[2:], g.dtype) for g in gs]
    return pl.pallas_call(
        body, name=name, out_shape=out_shape, in_specs=[HBM] * n, out_specs=[HBM] * n,
        scratch_shapes=[pltpu.SemaphoreType.DMA((4 * n,)), pltpu.SemaphoreType.DMA((4 * n,))],
    )(*gs)


def _rs_chips(ps, name):
    n = len(ps)

    def body(*refs):
        start, finish = _chips_phases(refs[:n], refs[n:2 * n], *refs[2 * n:])
        start()
        finish()

    return pl.pallas_call(
        body, name=name, out_shape=_chips_out_shapes(ps), in_specs=[HBM] * n, out_specs=[HBM] * n,
        scratch_shapes=_chips_scratch(n),
    )(*ps)


def _chips_phases(srcs, outs, send_sems, recv_sems):
    x, y, c = _position()
    copies = [pltpu.make_async_remote_copy(
        src_ref=srcs[t].at[2 * px + py], dst_ref=outs[t].at[j], send_sem=send_sems.at[3 * t + j],
        recv_sem=recv_sems.at[3 * t + j], device_id=(px, py, c), device_id_type=MESH)
        for t in range(len(srcs)) for j, (px, py) in enumerate(_other_chips(x, y))]

    def start():
        for cp in copies:
            cp.start()

    def finish():
        for cp in copies:
            cp.wait()

    return start, finish


def _chips_scratch(n):
    return [pltpu.SemaphoreType.DMA((3 * n,)), pltpu.SemaphoreType.DMA((3 * n,))]


def _chips_out_shapes(ps):
    return [jax.ShapeDtypeStruct((3,) + p.shape[1:], p.dtype) for p in ps]


def _rs_pair(hs, name):
    n = len(hs)

    def body(*refs):
        bufs = refs[n:2 * n]
        send_sems, recv_sems = refs[2 * n:]
        x, y, c = _position()

        def copy(t, half):
            return pltpu.make_async_remote_copy(
                src_ref=bufs[t].at[half], dst_ref=bufs[t].at[half], send_sem=send_sems.at[t], recv_sem=recv_sems.at[t],
                device_id=(x, y, 1 - c), device_id_type=MESH)

        for t in range(n):
            copy(t, c).start()
        for t in range(n):
            copy(t, 1 - c).wait_recv()
        for t in range(n):
            copy(t, c).wait_send()

    out_shape = [jax.ShapeDtypeStruct(h.shape, h.dtype) for h in hs]
    return pl.pallas_call(
        body, name=name, out_shape=out_shape, in_specs=[HBM] * n, out_specs=[HBM] * n,
        input_output_aliases={t: t for t in range(n)},
        scratch_shapes=[pltpu.SemaphoreType.DMA((n,)), pltpu.SemaphoreType.DMA((n,))],
    )(*hs)


BIG = ("w_in", "w_branch_a", "w_branch_b", "w_out", "w_ffn_in", "w_ffn_out")
CARRY_ATTN = ["w_in"]
CARRY_LOCAL = ["w_ffn_in"]
CARRY_SCAN = ["w_branch_a", "w_branch_b", "w_out", "w_ffn_out"]
CARRY_DATTN = ["w_in", "w_ffn_in"]
CARRY_DLOCAL = ["w_branch_a", "w_branch_b", "w_out", "w_ffn_out"]


def _band_bias(rel_table, name):
    L, H, n = rel_table.shape
    tab = jnp.pad(rel_table, ((0, 0), (0, 0), (0, NREL_PAD - n))).reshape(L * H, 1, NREL_PAD)
    band = (A_PAST + 1) * CHUNK

    def body(t_ref, o_ref):
        r = lax.broadcasted_iota(jnp.int32, (NREL_PAD, SKEW_W), 0)
        xi = lax.broadcasted_iota(jnp.int32, (NREL_PAD, SKEW_W), 1)
        diag = jnp.where(xi < KSPAN, xi, xi - SKEW_W)
        rel = jnp.clip(A_PAST * CHUNK - diag, -A_MAX_REL, A_MAX_REL) + A_MAX_REL
        e = _nn(t_ref[...], jnp.where(rel == r, 1.0, 0.0).astype(F32), HI)
        x = jnp.broadcast_to(e, (QBLK, SKEW_W))
        row = lax.broadcasted_iota(jnp.int32, (QBLK, SKEW_W), 0)
        for b in range(QBLK.bit_length() - 1):
            x = jnp.where(((row >> b) & 1) == 1, pltpu.roll(x, 1 << b, 1), x)
        x = x[:, :KSPAN]
        first = (lax.broadcasted_iota(jnp.int32, (QBLK, KSPAN), 0) // CHUNK) * CHUNK
        col = lax.broadcasted_iota(jnp.int32, (QBLK, KSPAN), 1)
        o_ref[...] = jnp.where((col >= first) & (col < first + band), x, NEG)

    out = pl.pallas_call(
        body, name=name, grid=(L * H,), in_specs=[pl.BlockSpec((None, 1, NREL_PAD), lambda i: (i, 0, 0))],
        out_specs=pl.BlockSpec((None, QBLK, KSPAN), lambda i: (i, 0, 0)),
        out_shape=jax.ShapeDtypeStruct((L * H, QBLK, KSPAN), F32), compiler_params=_par(1),
    )(tab)
    return out.reshape(L, H, QBLK, KSPAN)


def _col_row_forms(t, S):
    nc = S // CHUNK
    return t.T.reshape(nc, CHUNK, B_HEADS), t.reshape(B_HEADS, nc, CHUNK).transpose(1, 0, 2)


def _layer_fwd(l, x, mod, W, P, big, gather=None):
    S, D = x.shape
    n = lambda s: f"{s}_l{l}"
    sh1, sc1, gt1, sh2, sc2, gt2 = (mod[i:i + 1] for i in range(6))
    h1 = _lnmod_fwd(x, P["norm1_g"][l:l + 1], sc1, sh1, n("ln1"))
    proj = _matmul(h1, W["w_in"], "nn", F32, n("proj"), tn=1152)
    part = (lambda names: ([gather[0][BIG.index(k)] for k in names], gather[1])) if gather is not None else (lambda names: None)
    ya, got_a = _attn_fwd(proj, big, n("attn"), part(CARRY_ATTN))
    ba = proj[:, P_BA:P_BA + 2 * B_HEADS]
    b_t, a_t = ba[:, :B_HEADS].T, ba[:, B_HEADS:].T
    alog, dtb = P["a_log"][l].reshape(B_HEADS, 1), P["dt_bias"][l].reshape(B_HEADS, 1)
    beta, gam = _gdn_gates_fwd(b_t, a_t, alog, dtb, n("gates"))
    bcol, _ = _col_row_forms(beta, S)
    gcol, grow = _col_row_forms(gam, S)
    qn, kn, v = _gdn_pre_fwd(proj, P["w_conv"][l], n("gdnpre"))
    tsave, amat, u, w, got_l = _gdn_local_fwd(qn, kn, v, bcol, gcol, grow, n("gdnlocal"), part(CARRY_LOCAL))
    o, ssave, got_s = _gdn_scan_fwd(qn, kn, u, w, amat, gcol, grow, n("gdnscan"), part(CARRY_SCAN))
    got = dict(zip(CARRY_ATTN + CARRY_LOCAL + CARRY_SCAN, got_a + got_l + got_s))
    gathered = [got[k] for k in BIG] if gather is not None else None
    yb = _gdn_post_fwd(o, proj, P["gdn_norm_g"][l:l + 1], n("gdnpost"))
    pa = _matmul(ya, W["w_branch_a"], "nn", BF16, n("pa"), tm=2048, stacked=True)
    pb = _matmul(yb, W["w_branch_b"], "nn", BF16, n("pb"), tm=2048, stacked=True)
    merged = _merge_fwd(proj, pa, pb, n("merge"))
    ao = _matmul(merged, W["w_out"], "nn", F32, n("ao"))
    x1 = _gate_fwd(x, ao, gt1, n("res1"))
    h2 = _lnmod_fwd(x1, P["norm2_g"][l:l + 1], sc2, sh2, n("ln2"))
    gu = _matmul(h2, W["w_ffn_in"], "nn", BF16, n("gu"), stacked=True)
    act = _ffn_act_fwd(gu, n("act"))
    fo = _matmul(act, W["w_ffn_out"], "nn", F32, n("fo"), tk=1408)
    x2 = _gate_fwd(x1, fo, gt2, n("res2"))
    saved = dict(x=x, h1=h1, proj=proj, ya=ya, b_t=b_t, a_t=a_t, bcol=bcol, gcol=gcol, grow=grow,
                 qn=qn, kn=kn, v=v, o=o, tsave=tsave, ssave=ssave, amat=amat, u=u, w=w, yb=yb, pa=pa, pb=pb,
                 merged=merged, ao=ao, x1=x1,
                 h2=h2, gu=gu, act=act, fo=fo)
    return x2, saved, gathered


def _layer_bwd(l, dx2, sv, mod, W, P, big, exchange=()):
    S, D = dx2.shape
    n = lambda s: f"{s}_l{l}"
    sh1, sc1, gt1, sh2, sc2, gt2 = (mod[i:i + 1] for i in range(6))
    g, pay = {}, {}
    view = lambda t: t.reshape((4, 2, t.shape[-2] // (2 if t.ndim == 3 else 8), t.shape[-1]))
    dz2, dgt2 = _gate_bwd(dx2, sv["fo"], gt2, n("dres2"))
    g["w_ffn_out"], pay["w_ffn_out"] = map(view, _matmul(sv["act"], dz2, "tn", F32, n("dwfo"), tm=1408, also_bf16=True))
    dact = _matmul(dz2, W["w_ffn_out"], "nt", BF16, n("dact"), tn=1408)
    dgu = _ffn_act_bwd(sv["gu"], dact, n("dgu"))
    g["w_ffn_in"], pay["w_ffn_in"] = map(view, _matmul(sv["h2"], dgu, "tn", F32, n("dwfi"), out_stacked=True,
                                                       also_bf16=True))
    dh2 = _matmul(dgu, W["w_ffn_in"], "nt", F32, n("dh2"), stacked=True)
    dx1, dsh2, dsc2, dn2 = _lnmod_bwd(dh2, sv["x1"], P["norm2_g"][l:l + 1], sc2, dx2, n("dln2"))
    dz1, dgt1 = _gate_bwd(dx1, sv["ao"], gt1, n("dres1"))
    g["w_out"], pay["w_out"] = map(view, _matmul(sv["merged"], dz1, "tn", F32, n("dwo"), also_bf16=True))
    dmerged = _matmul(dz1, W["w_out"], "nt", BF16, n("dmerged"))
    dgab, dpa, dpb = _merge_bwd(sv["proj"], sv["pa"], sv["pb"], dmerged, n("dmerge"))
    g["w_branch_a"], pay["w_branch_a"] = map(view, _matmul(sv["ya"], dpa, "tn", F32, n("dwa"), out_stacked=True,
                                                           also_bf16=True))
    g["w_branch_b"], pay["w_branch_b"] = map(view, _matmul(sv["yb"], dpb, "tn", F32, n("dwb"), out_stacked=True,
                                                           also_bf16=True))
    dya = _matmul(dpa, W["w_branch_a"], "nt", BF16, n("dya"), tm=2048, stacked=True)
    dyb = _matmul(dpb, W["w_branch_b"], "nt", F32, n("dyb"), tm=2048, stacked=True)
    ex = (lambda names: [exchange[BIG.index(k)] for k in names]) if len(exchange) else (lambda names: ())
    dq, dk, dv, dbig, rec_a = _attn_bwd(sv["proj"], big, dya, n("dattn"), ex(CARRY_DATTN))
    g["rel_table"] = _rel_table_grad(dbig, n("drel"))[:, 0, :2 * A_MAX_REL + 1]
    do, dzb, dng = _gdn_post_bwd(dyb, sv["o"], sv["proj"], P["gdn_norm_g"][l:l + 1], n("dgdnpost"))
    g["gdn_norm_g"] = dng[0]
    du, dw, dqd, dkd, da, dgl = _gdn_scan_bwd(sv["qn"], sv["kn"], sv["u"], sv["w"], sv["amat"], sv["gcol"], sv["grow"],
                                              sv["ssave"], do, n("dgdnscan"))
    dqn, dkn, dvv, dbc, dgc, dgr, rec_l = _gdn_local_bwd(
        sv["qn"], sv["kn"], sv["v"], sv["bcol"], sv["gcol"], sv["grow"], sv["tsave"], du, dw, dqd, dkd, da, dgl,
        n("dgdnlocal"), ex(CARRY_DLOCAL))
    rec = dict(zip(CARRY_DATTN + CARRY_DLOCAL, rec_a + rec_l))
    received = [rec[k] for k in BIG] if len(exchange) else None
    dbeta_t = dbc.reshape(S, B_HEADS).T
    dgam_a = dgc.reshape(S, B_HEADS).T
    dgam_b = dgr.transpose(1, 0, 2).reshape(B_HEADS, S)
    alog, dtb = P["a_log"][l].reshape(B_HEADS, 1), P["dt_bias"][l].reshape(B_HEADS, 1)
    db_t, da_t, dal, ddt = _gdn_gates_bwd(dbeta_t, dgam_a, dgam_b, sv["b_t"], sv["a_t"], alog, dtb, n("dgates"))
    g["a_log"], g["dt_bias"] = dal[:, 0], ddt[:, 0]
    dy = _gdn_pre_bwd_a(sv["proj"], P["w_conv"][l], dqn, dkn, dvv, n("dgdnpre_a"))
    dqkvb, g["w_conv"] = _gdn_pre_bwd_b(sv["proj"], P["w_conv"][l], dy, n("dgdnpre_b"))
    dba = jnp.concatenate([db_t.T, da_t.T, jnp.zeros((S, P_END - P_BA - 2 * B_HEADS), F32)], axis=1)
    dproj = jnp.concatenate([dq.astype(BF16), dk.astype(BF16), dv.astype(BF16), dqkvb, dgab, dzb, dba.astype(BF16)],
                            axis=1)
    g["w_in"], pay["w_in"] = map(view, _restore_w_in(_matmul(sv["h1"], dproj, "tn", F32, n("dwin"), tn=1152),
                                                     n("dwin_cols")))
    dh1 = _matmul(dproj, W["w_in"], "nt", F32, n("dh1"), tk=1152)
    dx, dsh1, dsc1, dn1 = _lnmod_bwd(dh1, sv["x"], P["norm1_g"][l:l + 1], sc1, dx1, n("dln1"))
    g["norm1_g"], g["norm2_g"] = dn1[0], dn2[0]
    dmod = jnp.concatenate([dsh1, dsc1, dgt1, dsh2, dsc2, dgt2], axis=1)[0]
    return dx, g, pay, dmod, received


SMALL = ("norm1_g", "norm2_g", "rel_table", "w_conv", "a_log", "dt_bias", "gdn_norm_g")
SMALL_PACK_C = 1024


def _as_rows(t):
    flat = t.reshape(-1)
    rows = -(-flat.shape[0] // SMALL_PACK_C)
    return jnp.pad(flat, (0, rows * SMALL_PACK_C - flat.shape[0])).reshape(rows, SMALL_PACK_C)


def _pack_rows(parts):
    blk = jnp.concatenate([_as_rows(p) for p in parts], axis=0)
    return jnp.pad(blk, ((0, -blk.shape[0] % 8), (0, 0)))


def _unpack_rows(blk, shapes):
    out, r = [], 0
    for shp in shapes:
        size = int(np.prod(shp))
        rows = -(-size // SMALL_PACK_C)
        out.append(blk[..., r:r + rows, :].reshape(blk.shape[:-2] + (rows * SMALL_PACK_C,))[..., :size]
                   .reshape(blk.shape[:-2] + tuple(shp)))
        r += rows
    return out


def kernel(x, c, w_ada, b_ada, norm1_g, norm2_g, w_in, rel_table, w_conv, a_log, dt_bias, gdn_norm_g, w_branch_a, w_branch_b, w_out, w_ffn_in, w_ffn_out, final_g, loss_target, m_w_ada, m_b_ada, m_norm1_g, m_norm2_g, m_w_in, m_rel_table, m_w_conv, m_a_log, m_dt_bias, m_gdn_norm_g, m_w_branch_a, m_w_branch_b, m_w_out, m_w_ffn_in, m_w_ffn_out, m_final_g, v_w_ada, v_b_ada, v_norm1_g, v_norm2_g, v_w_in, v_rel_table, v_w_conv, v_a_log, v_dt_bias, v_gdn_norm_g, v_w_branch_a, v_w_branch_b, v_w_out, v_w_ffn_in, v_w_ffn_out, v_final_g):
    weights = dict(w_ada=w_ada, b_ada=b_ada, norm1_g=norm1_g, norm2_g=norm2_g, w_in=w_in, rel_table=rel_table,
                   w_conv=w_conv, a_log=a_log, dt_bias=dt_bias, gdn_norm_g=gdn_norm_g, w_branch_a=w_branch_a,
                   w_branch_b=w_branch_b, w_out=w_out, w_ffn_in=w_ffn_in, w_ffn_out=w_ffn_out, final_g=final_g)
    mom_m = dict(w_ada=m_w_ada, b_ada=m_b_ada, norm1_g=m_norm1_g, norm2_g=m_norm2_g, w_in=m_w_in,
                 rel_table=m_rel_table, w_conv=m_w_conv, a_log=m_a_log, dt_bias=m_dt_bias, gdn_norm_g=m_gdn_norm_g,
                 w_branch_a=m_w_branch_a, w_branch_b=m_w_branch_b, w_out=m_w_out, w_ffn_in=m_w_ffn_in,
                 w_ffn_out=m_w_ffn_out, final_g=m_final_g)
    mom_v = dict(w_ada=v_w_ada, b_ada=v_b_ada, norm1_g=v_norm1_g, norm2_g=v_norm2_g, w_in=v_w_in,
                 rel_table=v_rel_table, w_conv=v_w_conv, a_log=v_a_log, dt_bias=v_dt_bias, gdn_norm_g=v_gdn_norm_g,
                 w_branch_a=v_w_branch_a, w_branch_b=v_w_branch_b, w_out=v_w_out, w_ffn_in=v_w_ffn_in,
                 w_ffn_out=v_w_ffn_out, final_g=v_final_g)
    xi, yi, ci = _position()
    chip = 2 * xi + yi
    dev = 2 * chip + ci
    L, D = norm1_g.shape
    NMOD = b_ada.shape[1] // D
    ns = w_ada.shape[2]
    cs = w_conv.shape[2]

    first_blk = _pack_rows([c, w_conv])
    first_all = _allgather8(first_blk, "gather_c").reshape(8, first_blk.shape[0], SMALL_PACK_C)
    c_all, w_conv_all = _unpack_rows(first_all, [(D,), w_conv.shape])
    w_conv_full = w_conv_all.reshape(4, 2, L, CONV_K, cs)[:, 0].transpose(1, 2, 0, 3).reshape(L, CONV_K, 4 * cs)
    b_shard = lax.dynamic_slice_in_dim(b_ada, chip * ns, ns, axis=1).reshape(L, 1, ns)
    mod_shard = _ada_mod(c_all, w_ada, b_shard, "ada_mod")
    mod_all = _allgather8(mod_shard.reshape(L * 8, ns), "gather_mod").reshape(4, 2, L, 8, ns)
    mod = lax.dynamic_index_in_dim(mod_all[:, 0], dev, axis=2, keepdims=False)
    mod = mod.transpose(1, 0, 2).reshape(L, NMOD, D)

    shards = [weights[k].astype(BF16) for k in BIG]
    shards = [s.reshape(s.shape[0], 2, s.shape[1] // 2, s.shape[2]) for s in shards]
    col_stacked = lambda t: t.reshape(4, 2 * t.shape[2], t.shape[3])
    row_joined = lambda t: t.reshape(8 * t.shape[2], t.shape[3])

    def layer_weights(l, gathered):
        gd = dict(zip(BIG, gathered))
        return dict(w_in=_reorder_w_in(col_stacked(gd["w_in"])[None], f"w_in_cols_l{l}")[0],
                    w_branch_a=col_stacked(gd["w_branch_a"]), w_branch_b=col_stacked(gd["w_branch_b"]),
                    w_ffn_in=col_stacked(gd["w_ffn_in"]), w_out=row_joined(gd["w_out"]),
                    w_ffn_out=row_joined(gd["w_ffn_out"]))

    P = dict(norm1_g=norm1_g, norm2_g=norm2_g, w_conv=w_conv_full, a_log=a_log, dt_bias=dt_bias,
             gdn_norm_g=gdn_norm_g)
    big = _band_bias(rel_table, "band_bias")

    W = [layer_weights(0, _gather_weights(shards, 0, "gather_weights_l0"))]
    xc = x[0]
    saved = []
    for l in range(L):
        xc, sv, gathered = _layer_fwd(l, xc, mod[l], W[l], P, big[l], (shards, l + 1) if l + 1 < L else None)
        saved.append(sv)
        if l + 1 < L:
            W.append(layer_weights(l + 1, gathered))
    dx, loss_dev, dfinal = _loss_head(xc, final_g.reshape(1, D), loss_target[0], "loss_head")

    where = jnp.stack([ci, chip]).astype(jnp.int32)
    grads = [None] * L
    dmods = [None] * L
    shard_grads = {k: [None] * L for k in BIG}

    def finish_reduce_scatter(l, sums, from_chips):
        halves = [_sum_own_and_received(s_[1], r_, where, f"rs_sum_{k}_l{l}")
                  for k, s_, r_ in zip(BIG, sums, from_chips)]
        for k, t in zip(BIG, _rs_pair(halves, f"rs_pair_l{l}")):
            shard_grads[k][l] = t.reshape(2 * t.shape[1], t.shape[2])

    pending = None
    for l in reversed(range(L)):
        exchange = [s_[0] for s_ in pending] if pending is not None else ()
        dx, grads[l], pay, dmods[l], received = _layer_bwd(l, dx, saved[l], mod[l], W[l], P, big[l], exchange)
        if pending is not None:
            finish_reduce_scatter(l + 1, pending, received)
        gs = [grads[l][k] for k in BIG]
        from_sibling = _rs_sibling([pay[k] for k in BIG], f"rs_sibling_l{l}")
        pending = [_pair_sums(g_, where, r_, f"rs_pair_sum_{k}_l{l}") for k, g_, r_ in zip(BIG, gs, from_sibling)]
    finish_reduce_scatter(0, pending, _rs_chips([s_[0] for s_ in pending], "rs_chips_l0"))
    dmod = jnp.stack(dmods)

    small = {k: jnp.stack([grads[l][k] for l in range(L)]) for k in SMALL}
    parts = [dmod] + [small[k] for k in SMALL] + [dfinal, loss_dev[0, :1]]
    small_blk = _pack_rows(parts)
    srows = small_blk.shape[0]
    small_all, small_sum = _allgather8(small_blk, "gather_small", reduce_rows=srows)
    shapes = [dmod.shape] + [small[k].shape for k in SMALL] + [(D,), (1,)]
    tot = _unpack_rows(small_sum, shapes)
    G = dict(zip(SMALL, tot[1:1 + len(SMALL)]))
    G["b_ada"] = tot[0].reshape(b_ada.shape)
    G["w_conv"] = lax.dynamic_slice_in_dim(G["w_conv"], chip * cs, cs, axis=2)
    G["final_g"] = tot[-2]
    loss = tot[-1][0]
    dmod_all = _unpack_rows(small_all.reshape(8, srows, SMALL_PACK_C), [dmod.shape])[0]
    dmod_cols = lax.dynamic_slice_in_dim(dmod_all, chip * ns, ns, axis=2).transpose(1, 0, 2)
    G["w_ada"] = _ada_wgrad(c_all, dmod_cols, "ada_wgrad")

    order = ["w_ada", "b_ada", "norm1_g", "norm2_g", "w_in", "rel_table", "w_conv", "a_log", "dt_bias", "gdn_norm_g",
             "w_branch_a", "w_branch_b", "w_out", "w_ffn_in", "w_ffn_out", "final_g"]
    deltas, new_m, new_v = {}, {}, {}
    for k in order:
        w = weights[k]
        if k in BIG:
            G[k], deltas[k], new_m[k], new_v[k] = _adamw_layers(w, shard_grads[k], mom_m[k], mom_v[k], f"adamw_{k}")
            continue
        as2d = (lambda t: t.reshape(1, -1)) if w.ndim == 1 else (lambda t: t)
        d_, m_, v_ = _adamw(as2d(w), as2d(G[k]), as2d(mom_m[k]), as2d(mom_v[k]), f"adamw_{k}")
        deltas[k], new_m[k], new_v[k] = d_.reshape(w.shape), m_.reshape(w.shape), v_.reshape(w.shape)
    return (loss, dx[None], *[G[k] for k in order], *[deltas[k] for k in order], *[new_m[k] for k in order],
            *[new_v[k] for k in order])
```

```python
import functools

import numpy as np
import jax
import jax.numpy as jnp
from jax import lax
from jax.experimental import pallas as pl
from jax.experimental.pallas import tpu as pltpu

F32 = jnp.float32
BF16 = jnp.bfloat16
HI = lax.Precision.HIGHEST
SOLVE_PREC = lax.Precision.HIGH
MESH = pl.DeviceIdType.MESH

EPS = 1e-6
CHUNK = 64
A_HEADS = 8
A_DH = 64
A_PAST = 8
A_MAX_REL = 128
B_HEADS = 4
B_DH = 128
CONV_K = 4
LANE = 128
QBLK = 4 * CHUNK
KSPAN = QBLK + A_PAST * CHUNK
NEG = -1e30

ADAM_LR = 0.001
ADAM_B1 = 0.9
ADAM_B2 = 0.999
ADAM_EPS = 1e-08
ADAM_WD = 0.01
ADAM_STEP = 10

P_QKVA, P_QKVB, P_GA, P_GB, P_Z, P_BA, P_END = 0, 1536, 3072, 4096, 5120, 5632, 5760
W_IN_SHARD = 1410


def _sigmoid(x):
    return 1.0 / (1.0 + jnp.exp(-x))


def _dot(a, b, ca, cb, prec):
    lead = a.ndim - 2
    batch = ((0,), (0,)) if lead else ((), ())
    return lax.dot_general(a, b, (((ca + lead,), (cb + lead,)), batch), precision=prec, preferred_element_type=F32)


def _nn(a, b, prec=None):
    return _dot(a, b, 1, 0, prec)


def _nt(a, b, prec=None):
    return _dot(a, b, 1, 1, prec)


def _tn(a, b, prec=None):
    return _dot(a, b, 0, 0, prec)


def _bnn(a, b):
    return _nn(a.astype(BF16), b.astype(BF16))


def _bnt(a, b):
    return _nt(a.astype(BF16), b.astype(BF16))


def _btn(a, b):
    return _tn(a.astype(BF16), b.astype(BF16))


def _pick(n, target, unit=LANE):
    best = None
    for t in range(unit, min(n, target) + 1, unit):
        if n % t == 0:
            best = t
    return best if best is not None else n


def _acc(ref, val, i):
    @pl.when(i == 0)
    def _():
        ref[...] = val

    @pl.when(i != 0)
    def _():
        ref[...] += val


def _arb(n):
    return pltpu.CompilerParams(dimension_semantics=("arbitrary",) * n)


def _par(n):
    return pltpu.CompilerParams(dimension_semantics=("parallel",) * n)


def _matmul(a, b, mode, out_dtype, name, tm=1024, tn=1024, tk=1024, layer=None, stacked=False, out_stacked=False,
            also_bf16=False):
    bs = b.shape[1:] if layer is not None else b.shape
    if mode == "nn":
        M, K = a.shape
        N = 4 * bs[2] if stacked else bs[1]
        if stacked:
            tn = bs[2]
    elif mode == "nt":
        M, K = a.shape
        N = bs[1] if stacked else bs[0]
        if stacked:
            tk = bs[2]
    else:
        K, M = a.shape
        N = bs[1]
        if out_stacked:
            tn = N // 4
    tm, tn, tk = _pick(M, tm), _pick(N, tn), _pick(K, tk)
    nk = K // tk
    lead = () if layer is None else (layer,)
    lead_blk = () if layer is None else (None,)
    if mode == "nn":
        a_spec = pl.BlockSpec((tm, tk), lambda i, j, k: (i, k))
        if stacked:
            b_spec = pl.BlockSpec(lead_blk + (None, tk, tn), lambda i, j, k: lead + (j, k, 0))
        else:
            b_spec = pl.BlockSpec(lead_blk + (tk, tn), lambda i, j, k: lead + (k, j))
        dot = _nn
    elif mode == "nt":
        a_spec = pl.BlockSpec((tm, tk), lambda i, j, k: (i, k))
        if stacked:
            b_spec = pl.BlockSpec(lead_blk + (None, tn, tk), lambda i, j, k: lead + (k, j, 0))
        else:
            b_spec = pl.BlockSpec(lead_blk + (tn, tk), lambda i, j, k: lead + (j, k))
        dot = _nt
    else:
        a_spec = pl.BlockSpec((tk, tm), lambda i, j, k: (k, i))
        b_spec = pl.BlockSpec((tk, tn), lambda i, j, k: (k, j))
        dot = _tn
    if out_stacked:
        o_spec = pl.BlockSpec((None, tm, tn), lambda i, j, k: (j, i, 0))
        o_shape = jax.ShapeDtypeStruct((4, M, tn), out_dtype)
    else:
        o_spec = pl.BlockSpec((tm, tn), lambda i, j, k: (i, j))
        o_shape = jax.ShapeDtypeStruct((M, N), out_dtype)

    def write(o_refs, val):
        for o_ref in o_refs:
            o_ref[...] = val.astype(o_ref.dtype)

    def body_single(a_ref, b_ref, *o_refs):
        write(o_refs, dot(a_ref[...], b_ref[...]))

    def body(a_ref, b_ref, *refs):
        o_refs, acc_ref = refs[:-1], refs[-1]
        k = pl.program_id(2)

        @pl.when(k == 0)
        def _():
            acc_ref[...] = jnp.zeros_like(acc_ref)

        acc_ref[...] += dot(a_ref[...], b_ref[...])

        @pl.when(k == nk - 1)
        def _():
            write(o_refs, acc_ref[...])

    o_shapes = [o_shape] + ([jax.ShapeDtypeStruct(o_shape.shape, BF16)] if also_bf16 else [])
    res = pl.pallas_call(
        body_single if nk == 1 else body, name=name, grid=(M // tm, N // tn, nk), in_specs=[a_spec, b_spec],
        out_specs=[o_spec] * len(o_shapes), out_shape=o_shapes,
        scratch_shapes=[] if nk == 1 else [pltpu.VMEM((tm, tn), F32)],
        compiler_params=pltpu.CompilerParams(dimension_semantics=("parallel", "parallel", "arbitrary")),
    )(a, b)
    return tuple(res) if also_bf16 else res[0]


def _rows(tm, n, col=0):
    return pl.BlockSpec((tm, n), lambda i: (i, col))


def _vec(n):
    return pl.BlockSpec((1, n), lambda i: (0, 0))


def _lnmod_fwd(x, g, sc, sh, name):
    S, D = x.shape
    tm = _pick(S, 512, 8)

    def body(x_ref, g_ref, sc_ref, sh_ref, o_ref):
        xv = x_ref[...]
        r = lax.rsqrt(jnp.mean(xv * xv, axis=-1, keepdims=True) + EPS)
        o_ref[...] = ((xv * r * g_ref[...]) * (1.0 + sc_ref[...]) + sh_ref[...]).astype(BF16)

    return pl.pallas_call(
        body, name=name, grid=(S // tm,),
        in_specs=[_rows(tm, D), _vec(D), _vec(D), _vec(D)], out_specs=_rows(tm, D),
        out_shape=jax.ShapeDtypeStruct((S, D), BF16), compiler_params=_par(1),
    )(x, g, sc, sh)


def _lnmod_bwd(dh, x, g, sc, dres, name):
    S, D = x.shape
    tm = _pick(S, 512, 8)

    def body(dh_ref, x_ref, g_ref, sc_ref, dres_ref, dx_ref, dsh_ref, dsc_ref, dg_ref):
        i = pl.program_id(0)
        xv = x_ref[...]
        dh_ = dh_ref[...]
        r = lax.rsqrt(jnp.mean(xv * xv, axis=-1, keepdims=True) + EPS)
        xhat = xv * r
        gv = g_ref[...]
        dn = dh_ * (1.0 + sc_ref[...])
        dxhat = dn * gv
        dx_ref[...] = dres_ref[...] + r * (dxhat - xhat * jnp.mean(dxhat * xhat, axis=-1, keepdims=True))
        _acc(dsh_ref, jnp.sum(dh_, axis=0, keepdims=True), i)
        _acc(dsc_ref, jnp.sum(dh_ * (xhat * gv), axis=0, keepdims=True), i)
        _acc(dg_ref, jnp.sum(dn * xhat, axis=0, keepdims=True), i)

    return pl.pallas_call(
        body, name=name, grid=(S // tm,),
        in_specs=[_rows(tm, D), _rows(tm, D), _vec(D), _vec(D), _rows(tm, D)],
        out_specs=[_rows(tm, D), _vec(D), _vec(D), _vec(D)],
        out_shape=[jax.ShapeDtypeStruct((S, D), F32)] + [jax.ShapeDtypeStruct((1, D), F32)] * 3,
        compiler_params=_arb(1),
    )(dh, x, g, sc, dres)


def _gate_fwd(x, y, gt, name):
    S, D = x.shape
    tm = _pick(S, 512, 8)

    def body(x_ref, y_ref, gt_ref, o_ref):
        o_ref[...] = x_ref[...] + gt_ref[...] * y_ref[...]

    return pl.pallas_call(
        body, name=name, grid=(S // tm,), in_specs=[_rows(tm, D), _rows(tm, D), _vec(D)], out_specs=_rows(tm, D),
        out_shape=jax.ShapeDtypeStruct((S, D), F32), compiler_params=_par(1),
    )(x, y, gt)


def _gate_bwd(dx, y, gt, name):
    S, D = dx.shape
    tm = _pick(S, 512, 8)

    def body(dx_ref, y_ref, gt_ref, dz_ref, dgt_ref):
        i = pl.program_id(0)
        d = dx_ref[...]
        dz_ref[...] = (d * gt_ref[...]).astype(BF16)
        _acc(dgt_ref, jnp.sum(d * y_ref[...], axis=0, keepdims=True), i)

    return pl.pallas_call(
        body, name=name, grid=(S // tm,), in_specs=[_rows(tm, D), _rows(tm, D), _vec(D)],
        out_specs=[_rows(tm, D), _vec(D)],
        out_shape=[jax.ShapeDtypeStruct((S, D), BF16), jax.ShapeDtypeStruct((1, D), F32)],
        compiler_params=_arb(1),
    )(dx, y, gt)


def _ffn_act_fwd(gu, name):
    S, H2 = gu.shape
    H = H2 // 2
    tm = _pick(S, 256, 8)

    def body(g_ref, u_ref, o_ref):
        gv = g_ref[...].astype(F32)
        o_ref[...] = (gv * _sigmoid(gv) * u_ref[...].astype(F32)).astype(BF16)

    return pl.pallas_call(
        body, name=name, grid=(S // tm,), in_specs=[_rows(tm, H, 0), _rows(tm, H, 1)], out_specs=_rows(tm, H),
        out_shape=jax.ShapeDtypeStruct((S, H), BF16), compiler_params=_par(1),
    )(gu, gu)


def _ffn_act_bwd(gu, dact, name):
    S, H2 = gu.shape
    H = H2 // 2
    tm = _pick(S, 256, 8)

    def body(g_ref, u_ref, da_ref, o_ref):
        gv = g_ref[...].astype(F32)
        s = _sigmoid(gv)
        da = da_ref[...].astype(F32)
        o_ref[:, :H] = (da * u_ref[...].astype(F32) * (s * (1.0 + gv * (1.0 - s)))).astype(BF16)
        o_ref[:, H:] = (da * (gv * s)).astype(BF16)

    return pl.pallas_call(
        body, name=name, grid=(S // tm,), in_specs=[_rows(tm, H, 0), _rows(tm, H, 1), _rows(tm, H)],
        out_specs=_rows(tm, H2), out_shape=jax.ShapeDtypeStruct((S, H2), BF16), compiler_params=_par(1),
    )(gu, gu, dact)


def _merge_fwd(proj, pa, pb, name):
    S, D = pa.shape
    tm = _pick(S, 512, 8)

    def body(ga_ref, gb_ref, pa_ref, pb_ref, o_ref):
        o_ref[...] = (_sigmoid(ga_ref[...]) * pa_ref[...].astype(F32)
                      + _sigmoid(gb_ref[...]) * pb_ref[...].astype(F32)).astype(BF16)

    return pl.pallas_call(
        body, name=name, grid=(S // tm,),
        in_specs=[_rows(tm, D, P_GA // D), _rows(tm, D, P_GB // D), _rows(tm, D), _rows(tm, D)],
        out_specs=_rows(tm, D), out_shape=jax.ShapeDtypeStruct((S, D), BF16), compiler_params=_par(1),
    )(proj, proj, pa, pb)


def _merge_bwd(proj, pa, pb, dm, name):
    S, D = pa.shape
    tm = _pick(S, 512, 8)

    def body(ga_ref, gb_ref, pa_ref, pb_ref, dm_ref, dg_ref, dpa_ref, dpb_ref):
        d = dm_ref[...].astype(F32)
        sa = _sigmoid(ga_ref[...])
        sb = _sigmoid(gb_ref[...])
        dg_ref[:, :D] = (d * pa_ref[...].astype(F32) * sa * (1.0 - sa)).astype(BF16)
        dg_ref[:, D:] = (d * pb_ref[...].astype(F32) * sb * (1.0 - sb)).astype(BF16)
        dpa_ref[...] = (d * sa).astype(BF16)
        dpb_ref[...] = (d * sb).astype(BF16)

    return pl.pallas_call(
        body, name=name, grid=(S // tm,),
        in_specs=[_rows(tm, D, P_GA // D), _rows(tm, D, P_GB // D), _rows(tm, D), _rows(tm, D), _rows(tm, D)],
        out_specs=[_rows(tm, 2 * D), _rows(tm, D), _rows(tm, D)],
        out_shape=[jax.ShapeDtypeStruct((S, 2 * D), BF16), jax.ShapeDtypeStruct((S, D), BF16),
                   jax.ShapeDtypeStruct((S, D), BF16)],
        compiler_params=_par(1),
    )(proj, proj, pa, pb, dm)


def _loss_head(x, g, target, name):
    S, D = x.shape
    tm = _pick(S, 512, 8)

    def body(x_ref, g_ref, t_ref, dx_ref, loss_ref, dg_ref):
        i = pl.program_id(0)
        xv = x_ref[...]
        gv = g_ref[...]
        r = lax.rsqrt(jnp.mean(xv * xv, axis=-1, keepdims=True) + EPS)
        xhat = xv * r
        err = xhat * gv - t_ref[...]
        part = 0.5 * jnp.sum(jnp.mean(err * err, axis=-1, keepdims=True), axis=0, keepdims=True)
        _acc(loss_ref, jnp.broadcast_to(part, (1, LANE)), i)
        dy = err * (1.0 / D)
        _acc(dg_ref, jnp.sum(dy * xhat, axis=0, keepdims=True), i)
        dxhat = dy * gv
        dx_ref[...] = r * (dxhat - xhat * jnp.mean(dxhat * xhat, axis=-1, keepdims=True))

    return pl.pallas_call(
        body, name=name, grid=(S // tm,), in_specs=[_rows(tm, D), _vec(D), _rows(tm, D)],
        out_specs=[_rows(tm, D), _vec(LANE), _vec(D)],
        out_shape=[jax.ShapeDtypeStruct((S, D), F32), jax.ShapeDtypeStruct((1, LANE), F32),
                   jax.ShapeDtypeStruct((1, D), F32)],
        compiler_params=_arb(1),
    )(x, g, target)


HEADS_PER_SLAB = LANE // A_DH
N_SLABS = A_HEADS // HEADS_PER_SLAB
SPAN_BLOCKS = KSPAN // QBLK


def _attn_specs(seg):
    q_spec = pl.BlockSpec((QBLK, LANE), lambda p, m: (m, seg[0] * N_SLABS + p))
    k_specs = [pl.BlockSpec((QBLK, LANE), functools.partial(
        lambda j, p, m: (jnp.maximum(m - (SPAN_BLOCKS - 1) + j, 0), seg[1] * N_SLABS + p), j)) for j in range(SPAN_BLOCKS)]
    v_specs = [pl.BlockSpec((QBLK, LANE), functools.partial(
        lambda j, p, m: (jnp.maximum(m - (SPAN_BLOCKS - 1) + j, 0), seg[2] * N_SLABS + p), j)) for j in range(SPAN_BLOCKS)]
    b_spec = pl.BlockSpec((HEADS_PER_SLAB, QBLK, KSPAN), lambda p, m: (p, 0, 0))
    return q_spec, k_specs, v_specs, b_spec


def _head_lanes(t, hh):
    lane = lax.broadcasted_iota(jnp.int32, t.shape, 1)
    return jnp.where((lane // A_DH) == hh, t, jnp.zeros_like(t))


def _front_mask(m):
    col = lax.broadcasted_iota(jnp.int32, (QBLK, KSPAN), 1)
    return jnp.where(col < (SPAN_BLOCKS - 1 - m) * QBLK, NEG, 0.0)


def _attn_probs(qk, bias, front):
    s = qk * (A_DH ** -0.5) + (bias + front)
    p = jnp.exp(s - jnp.max(s, axis=-1, keepdims=True))
    return p * (1.0 / jnp.sum(p, axis=-1, keepdims=True))


def _grid_ends(nq):
    p, m = pl.program_id(0), pl.program_id(1)
    return (p == 0) & (m == 0), (p == N_SLABS // 2) & (m == 0), (p == N_SLABS - 1) & (m == nq - 1)


def _attn_fwd(proj, big, name, gather=None):
    S = proj.shape[0]
    q_spec, k_specs, v_specs, b_spec = _attn_specs((0, 1, 2))
    shards, layer = gather if gather is not None else ((), None)
    ng = len(shards)

    def body(q_ref, k0, k1, k2, v0, v1, v2, b_ref, *rest):
        srcs, o_ref, gouts, sems = rest[:ng], rest[ng], rest[ng + 1:2 * ng + 1], rest[2 * ng + 1:]
        done = _carry(_gather_phases(layer, srcs, gouts, *sems), *_grid_ends(S // QBLK)) if ng else None
        m = pl.program_id(1)
        q = q_ref[...].astype(BF16)
        k = jnp.concatenate([k0[...], k1[...], k2[...]], axis=0).astype(BF16)
        v = jnp.concatenate([v0[...], v1[...], v2[...]], axis=0).astype(BF16)
        front = _front_mask(m)
        heads = range(HEADS_PER_SLAB)
        scores = [_nt(_head_lanes(q, hh), k) for hh in heads]
        probs = [_attn_probs(scores[hh], b_ref[hh], front).astype(BF16) for hh in heads]
        outs = [_nn(probs[hh], v) for hh in heads]
        lane = lax.broadcasted_iota(jnp.int32, (QBLK, LANE), 1)
        o_ref[...] = jnp.where(lane < A_DH, outs[0], outs[1]).astype(BF16)
        if ng:
            done()

    res = pl.pallas_call(
        body, name=name, grid=(N_SLABS, S // QBLK), in_specs=[q_spec] + k_specs + v_specs + [b_spec] + [HBM] * ng,
        out_specs=[pl.BlockSpec((QBLK, LANE), lambda p, m: (m, p))] + [HBM] * ng,
        out_shape=[jax.ShapeDtypeStruct((S, A_HEADS * A_DH), BF16)] + _gather_out_shapes(shards),
        scratch_shapes=_gather_scratch(ng) if ng else [], compiler_params=_arb(2),
    )(proj, proj, proj, proj, proj, proj, proj, big, *shards)
    return res[0], list(res[1:])


def _attn_bwd(proj, big, dya, name, exchange=()):
    S = proj.shape[0]
    W = A_HEADS * A_DH
    q_spec, k_specs, v_specs, b_spec = _attn_specs((0, 1, 2))
    out_q = pl.BlockSpec((QBLK, LANE), lambda p, m: (m, p))
    out_kv = pl.BlockSpec((S, LANE), lambda p, m: (0, p))
    ne = len(exchange)

    def body(q_ref, k0, k1, k2, v0, v1, v2, b_ref, do_ref, *rest):
        srcs, (dq_ref, dk_ref, dv_ref, db_ref), eouts, sems = rest[:ne], rest[ne:ne + 4], rest[ne + 4:2 * ne + 4], rest[2 * ne + 4:]
        done = _carry(_chips_phases(srcs, eouts, *sems), *_grid_ends(S // QBLK)) if ne else None
        m = pl.program_id(1)

        @pl.when(m == 0)
        def _():
            dk_ref[...] = jnp.zeros_like(dk_ref)
            dv_ref[...] = jnp.zeros_like(dv_ref)
            db_ref[...] = jnp.zeros_like(db_ref)

        q = q_ref[...].astype(BF16)
        k = jnp.concatenate([k0[...], k1[...], k2[...]], axis=0).astype(BF16)
        v = jnp.concatenate([v0[...], v1[...], v2[...]], axis=0).astype(BF16)
        do = do_ref[...]
        front = _front_mask(m)
        heads = range(HEADS_PER_SLAB)
        qh = [_head_lanes(q, hh) for hh in heads]
        doh = [_head_lanes(do, hh) for hh in heads]
        scores = [_nt(qh[hh], k) for hh in heads]
        dps = [_nt(doh[hh], v) for hh in heads]
        ps = [_attn_probs(scores[hh], b_ref[hh], front) for hh in heads]
        dss = [ps[hh] * (dps[hh] - jnp.sum(ps[hh] * dps[hh], axis=-1, keepdims=True)) for hh in heads]
        for hh in heads:
            db_ref[hh] += dss[hh]
        dsb = [(dss[hh] * (A_DH ** -0.5)).astype(BF16) for hh in heads]
        dqs = [_nn(dsb[hh], k) for hh in heads]
        dk = sum(_tn(dsb[hh], qh[hh]) for hh in heads)
        dv = sum(_tn(ps[hh].astype(BF16), doh[hh]) for hh in heads)
        lane = lax.broadcasted_iota(jnp.int32, (QBLK, LANE), 1)
        dq_ref[...] = jnp.where(lane < A_DH, dqs[0], dqs[1])
        for j in range(SPAN_BLOCKS):
            blk = m - (SPAN_BLOCKS - 1) + j

            @pl.when(blk >= 0)
            def _():
                off = pl.multiple_of(blk * QBLK, QBLK)
                dk_ref[pl.ds(off, QBLK), :] += dk[j * QBLK:(j + 1) * QBLK]
                dv_ref[pl.ds(off, QBLK), :] += dv[j * QBLK:(j + 1) * QBLK]
        if ne:
            done()

    res = pl.pallas_call(
        body, name=name, grid=(N_SLABS, S // QBLK),
        in_specs=[q_spec] + k_specs + v_specs + [b_spec, pl.BlockSpec((QBLK, LANE), lambda p, m: (m, p))] + [HBM] * ne,
        out_specs=[out_q, out_kv, out_kv, b_spec] + [HBM] * ne,
        out_shape=[jax.ShapeDtypeStruct((S, W), F32)] * 3 + [jax.ShapeDtypeStruct((A_HEADS, QBLK, KSPAN), F32)]
        + _chips_out_shapes(exchange),
        scratch_shapes=_chips_scratch(ne) if ne else [], compiler_params=_arb(2),
    )(proj, proj, proj, proj, proj, proj, proj, big, dya, *exchange)
    return tuple(res[:4]) + (list(res[4:]),)


NREL_PAD = 3 * LANE
SKEW_W = 1024


def _rel_table_grad(dbig, name):
    H, R, C = dbig.shape

    def body(d_ref, o_ref):
        x = jnp.concatenate([d_ref[...], jnp.zeros((R, SKEW_W - C), F32)], axis=1)
        row = lax.broadcasted_iota(jnp.int32, (R, SKEW_W), 0)
        for b in range(R.bit_length() - 1):
            x = jnp.where(((row >> b) & 1) == 1, pltpu.roll(x, SKEW_W - (1 << b), 1), x)
        e = jnp.sum(x, axis=0, keepdims=True)
        xi = lax.broadcasted_iota(jnp.int32, (SKEW_W, NREL_PAD), 0)
        r = lax.broadcasted_iota(jnp.int32, (SKEW_W, NREL_PAD), 1)
        diag = jnp.where(xi < C, xi, xi - SKEW_W)
        rel = jnp.clip(A_PAST * CHUNK - diag, -A_MAX_REL, A_MAX_REL) + A_MAX_REL
        o_ref[...] = _nn(e, jnp.where(rel == r, 1.0, 0.0).astype(F32), HI)

    return pl.pallas_call(
        body, name=name, grid=(H,), in_specs=[pl.BlockSpec((None, R, C), lambda h: (h, 0, 0))],
        out_specs=pl.BlockSpec((None, 1, NREL_PAD), lambda h: (h, 0, 0)),
        out_shape=jax.ShapeDtypeStruct((H, 1, NREL_PAD), F32), compiler_params=_par(1),
    )(dbig)


def _chunk_cumsum_matrix(n, reverse):
    j = lax.broadcasted_iota(jnp.int32, (n, n), 0)
    i = lax.broadcasted_iota(jnp.int32, (n, n), 1)
    same = (j // CHUNK) == (i // CHUNK)
    return jnp.where(same & ((j >= i) if reverse else (j <= i)), 1.0, 0.0).astype(F32)


def _gdn_gates_fwd(b_t, a_t, alog, dtb, name):
    Hh, S = b_t.shape
    tl = _pick(S, 512)
    row = pl.BlockSpec((Hh, tl), lambda i: (0, i))
    col = pl.BlockSpec((Hh, 1), lambda i: (0, 0))

    def body(b_ref, a_ref, al_ref, dt_ref, beta_ref, gam_ref):
        z = a_ref[...] + dt_ref[...]
        sp = jnp.maximum(z, 0.0) + jnp.log(1.0 + jnp.exp(-jnp.abs(z)))
        g = -jnp.exp(al_ref[...]) * sp
        beta_ref[...] = _sigmoid(b_ref[...])
        gam_ref[...] = _nn(g, _chunk_cumsum_matrix(tl, False), HI)

    return pl.pallas_call(
        body, name=name, grid=(S // tl,), in_specs=[row, row, col, col], out_specs=[row, row],
        out_shape=[jax.ShapeDtypeStruct((Hh, S), F32)] * 2, compiler_params=_par(1),
    )(b_t, a_t, alog, dtb)


def _gdn_gates_bwd(dbeta, dgam_a, dgam_b, b_t, a_t, alog, dtb, name):
    Hh, S = b_t.shape
    tl = _pick(S, 512)
    row = pl.BlockSpec((Hh, tl), lambda i: (0, i))
    col = pl.BlockSpec((Hh, 1), lambda i: (0, 0))
    accs = pl.BlockSpec((Hh, LANE), lambda i: (0, 0))

    def body(dbeta_ref, dga_ref, dgb_ref, b_ref, a_ref, al_ref, dt_ref, db_ref, da_ref, dal_ref, ddt_ref):
        i = pl.program_id(0)
        z = a_ref[...] + dt_ref[...]
        sp = jnp.maximum(z, 0.0) + jnp.log(1.0 + jnp.exp(-jnp.abs(z)))
        ea = jnp.exp(al_ref[...])
        dg = _nn(dga_ref[...] + dgb_ref[...], _chunk_cumsum_matrix(tl, True), HI)
        da = dg * (-ea) * _sigmoid(z)
        beta = _sigmoid(b_ref[...])
        db_ref[...] = dbeta_ref[...] * beta * (1.0 - beta)
        da_ref[...] = da
        _acc(dal_ref, jnp.broadcast_to(jnp.sum(dg * (-ea * sp), axis=1, keepdims=True), (Hh, LANE)), i)
        _acc(ddt_ref, jnp.broadcast_to(jnp.sum(da, axis=1, keepdims=True), (Hh, LANE)), i)

    return pl.pallas_call(
        body, name=name, grid=(S // tl,), in_specs=[row] * 5 + [col, col], out_specs=[row, row, accs, accs],
        out_shape=[jax.ShapeDtypeStruct((Hh, S), F32)] * 2 + [jax.ShapeDtypeStruct((Hh, LANE), F32)] * 2,
        compiler_params=_arb(1),
    )(dbeta, dgam_a, dgam_b, b_t, a_t, alog, dtb)


HALO = 8


def _conv_silu(xx_ref, w_ref, tm):
    y = w_ref[0:1, :] * xx_ref[pl.ds(HALO - CONV_K + 1, tm), :]
    for j in range(1, CONV_K):
        y = y + w_ref[j:j + 1, :] * xx_ref[pl.ds(HALO - CONV_K + 1 + j, tm), :]
    return y, y * _sigmoid(y)


def _fill_prev_halo(xx_ref, x_ref, prev_ref, i, tm):
    xx_ref[pl.ds(HALO, tm), :] = x_ref[...]

    @pl.when(i == 0)
    def _():
        xx_ref[pl.ds(0, HALO), :] = jnp.zeros((HALO, xx_ref.shape[1]), F32)

    @pl.when(i != 0)
    def _():
        xx_ref[pl.ds(0, HALO), :] = prev_ref[...]


def _gdn_pre_specs(tm, C, colblk):
    cur = pl.BlockSpec((tm, C), lambda i: (i, colblk))
    prev = pl.BlockSpec((HALO, C), lambda i: (jnp.maximum(i * (tm // HALO) - 1, 0), colblk))
    return cur, prev


def _gdn_pre_fwd(proj, wconv, name):
    S = proj.shape[0]
    C = 3 * B_HEADS * B_DH
    W = B_HEADS * B_DH
    tm = _pick(S, 256, 8)
    cur, prev = _gdn_pre_specs(tm, C, P_QKVB // C)

    def body(x_ref, prev_ref, w_ref, q_ref, k_ref, v_ref, xx_ref):
        i = pl.program_id(0)
        _fill_prev_halo(xx_ref, x_ref, prev_ref, i, tm)
        _, sl = _conv_silu(xx_ref, w_ref, tm)
        for h in range(B_HEADS):
            hs = slice(h * B_DH, (h + 1) * B_DH)
            q = sl[:, h * B_DH:(h + 1) * B_DH]
            k = sl[:, W + h * B_DH:W + (h + 1) * B_DH]
            q_ref[:, hs] = q * (lax.rsqrt(jnp.sum(q * q, axis=-1, keepdims=True) + EPS) * (B_DH ** -0.5))
            k_ref[:, hs] = k * lax.rsqrt(jnp.sum(k * k, axis=-1, keepdims=True) + EPS)
        v_ref[...] = sl[:, 2 * W:]

    return pl.pallas_call(
        body, name=name, grid=(S // tm,), in_specs=[cur, prev, pl.BlockSpec((CONV_K, C), lambda i: (0, 0))],
        out_specs=[_rows(tm, W)] * 3, out_shape=[jax.ShapeDtypeStruct((S, W), F32)] * 3,
        scratch_shapes=[pltpu.VMEM((HALO + tm, C), F32)], compiler_params=_par(1),
    )(proj, proj, wconv)


def _gdn_pre_bwd_a(proj, wconv, dqn, dkn, dv, name):
    S = proj.shape[0]
    C = 3 * B_HEADS * B_DH
    W = B_HEADS * B_DH
    tm = _pick(S, 256, 8)
    cur, prev = _gdn_pre_specs(tm, C, P_QKVB // C)

    def body(x_ref, prev_ref, w_ref, dq_ref, dk_ref, dv_ref, dy_ref, xx_ref):
        i = pl.program_id(0)
        _fill_prev_halo(xx_ref, x_ref, prev_ref, i, tm)
        y, sl = _conv_silu(xx_ref, w_ref, tm)
        sg = _sigmoid(y)
        dsilu = sg * (1.0 + y * (1.0 - sg))
        for h in range(B_HEADS):
            for base, d_ref, c in ((0, dq_ref, B_DH ** -0.5), (W, dk_ref, 1.0)):
                lo = base + h * B_DH
                t = sl[:, lo:lo + B_DH]
                d = d_ref[:, h * B_DH:(h + 1) * B_DH]
                r = lax.rsqrt(jnp.sum(t * t, axis=-1, keepdims=True) + EPS)
                dt = (c * r) * (d - t * (r * r) * jnp.sum(d * t, axis=-1, keepdims=True))
                dy_ref[:, lo:lo + B_DH] = dt * dsilu[:, lo:lo + B_DH]
        dy_ref[:, 2 * W:] = dv_ref[...] * dsilu[:, 2 * W:]

    return pl.pallas_call(
        body, name=name, grid=(S // tm,),
        in_specs=[cur, prev, pl.BlockSpec((CONV_K, C), lambda i: (0, 0))] + [_rows(tm, W)] * 3,
        out_specs=_rows(tm, C), out_shape=jax.ShapeDtypeStruct((S, C), F32),
        scratch_shapes=[pltpu.VMEM((HALO + tm, C), F32)], compiler_params=_par(1),
    )(proj, proj, wconv, dqn, dkn, dv)


def _gdn_pre_bwd_b(proj, wconv, dy, name):
    S = proj.shape[0]
    C = 3 * B_HEADS * B_DH
    tm = _pick(S, 256, 8)
    nt_ = S // tm
    cur, prev = _gdn_pre_specs(tm, C, P_QKVB // C)
    nxt = pl.BlockSpec((HALO, C), lambda i: (jnp.minimum((i + 1) * (tm // HALO), S // HALO - 1), 0))

    def body(x_ref, prev_ref, w_ref, dy_ref, next_ref, dx_ref, dw_ref, xx_ref, dd_ref):
        i = pl.program_id(0)
        _fill_prev_halo(xx_ref, x_ref, prev_ref, i, tm)
        dyv = dy_ref[...]
        dd_ref[pl.ds(0, tm), :] = dyv

        @pl.when(i == nt_ - 1)
        def _():
            dd_ref[pl.ds(tm, HALO), :] = jnp.zeros((HALO, C), F32)

        @pl.when(i != nt_ - 1)
        def _():
            dd_ref[pl.ds(tm, HALO), :] = next_ref[...]

        dx = w_ref[0:1, :] * dd_ref[pl.ds(CONV_K - 1, tm), :]
        for j in range(1, CONV_K):
            dx = dx + w_ref[j:j + 1, :] * dd_ref[pl.ds(CONV_K - 1 - j, tm), :]
        dx_ref[...] = dx.astype(BF16)
        dw = jnp.concatenate(
            [jnp.sum(dyv * xx_ref[pl.ds(HALO - CONV_K + 1 + j, tm), :], axis=0, keepdims=True) for j in range(CONV_K)],
            axis=0)
        _acc(dw_ref, dw, i)

    return pl.pallas_call(
        body, name=name, grid=(nt_,),
        in_specs=[cur, prev, pl.BlockSpec((CONV_K, C), lambda i: (0, 0)), _rows(tm, C), nxt],
        out_specs=[_rows(tm, C), pl.BlockSpec((CONV_K, C), lambda i: (0, 0))],
        out_shape=[jax.ShapeDtypeStruct((S, C), BF16), jax.ShapeDtypeStruct((CONV_K, C), F32)],
        scratch_shapes=[pltpu.VMEM((HALO + tm, C), F32), pltpu.VMEM((tm + HALO, C), F32)],
        compiler_params=_arb(1),
    )(proj, proj, wconv, dy, dy)


def _chunk_masks():
    row = lax.broadcasted_iota(jnp.int32, (CHUNK, CHUNK), 0)
    col = lax.broadcasted_iota(jnp.int32, (CHUNK, CHUNK), 1)
    return row >= col, row > col


def _chunk_local(q, k, vv, bc, gc, gr, tri):
    dm = jnp.where(tri, jnp.exp(jnp.where(tri, gc - gr, 0.0)), 0.0)
    kk = _bnt(k, k)
    glast = gr[..., CHUNK - 1:CHUNK]
    ep = jnp.exp(gc)
    em = jnp.exp(glast - gc)
    el = jnp.exp(glast)
    return dm, kk, ep, em, el, vv * bc, k * (bc * ep)


def _unit_lower_inverse(low):
    row = lax.broadcasted_iota(jnp.int32, (CHUNK, CHUNK), 0)
    col = lax.broadcasted_iota(jnp.int32, (CHUNK, CHUNK), 1)
    p = -low
    t = jnp.where(row == col, 1.0, 0.0).astype(F32) + p
    steps = CHUNK.bit_length() - 2
    for _ in range(steps):
        p = _nn(p, p, SOLVE_PREC)
        t = t + _nn(t, p, SOLVE_PREC)
    return t


GROUP = 4


def _carry(phases, first, middle, last):
    if len(phases) == 3:
        pl.when(first)(phases[0])
        pl.when(middle)(phases[1])
        return lambda: pl.when(last)(phases[2])
    pl.when(first)(phases[0])
    return lambda: pl.when(last)(phases[1])


def _pairs(nchunks):
    return [(c, h) for c in range(nchunks) for h in range(B_HEADS)]


def _tok(c):
    return slice(c * CHUNK, (c + 1) * CHUNK)


def _head(h):
    return slice(h * B_DH, (h + 1) * B_DH)


def _stack_tokens(ref, nchunks):
    return jnp.stack([ref[_tok(c), _head(h)] for c, h in _pairs(nchunks)])


def _stack_cols(ref, nchunks):
    per_chunk = [ref[c] for c in range(nchunks)] if len(ref.shape) == 3 else [ref[...]]
    return jnp.stack([per_chunk[c][:, h:h + 1] for c, h in _pairs(nchunks)])


def _stack_rows(ref, nchunks):
    if len(ref.shape) == 3:
        return jnp.stack([ref[c, h:h + 1, :] for c, h in _pairs(nchunks)])
    return jnp.stack([ref[h:h + 1, :] for _, h in _pairs(1)])


def _gdn_group_specs(ng_steps, W):
    tok = pl.BlockSpec((GROUP * CHUNK, W), lambda i: (i, 0))
    colv = pl.BlockSpec((GROUP, CHUNK, B_HEADS), lambda i: (i, 0, 0))
    rowv = pl.BlockSpec((GROUP, B_HEADS, CHUNK), lambda i: (i, 0, 0))
    mat = pl.BlockSpec((GROUP, B_HEADS, CHUNK, CHUNK), lambda i: (i, 0, 0, 0))
    return tok, colv, rowv, mat


def _gdn_local_fwd(qn, kn, v, bcol, gcol, grow, name, gather=None):
    S, Wd = qn.shape
    nc = S // CHUNK
    steps = nc // GROUP
    tok, colv, rowv, mat = _gdn_group_specs(steps, Wd)
    shards, layer = gather if gather is not None else ((), None)
    ng = len(shards)

    def body(q_ref, k_ref, v_ref, bc_ref, gc_ref, gr_ref, *rest):
        srcs, (t_ref, a_ref, u_ref, w_ref), gouts, sems = rest[:ng], rest[ng:ng + 4], rest[ng + 4:2 * ng + 4], rest[2 * ng + 4:]
        i = pl.program_id(0)
        done = _carry(_gather_phases(layer, srcs, gouts, *sems), i == 0, i == steps // 2, i == steps - 1) if ng else None
        tri, strict = _chunk_masks()
        q, k, vv = (_stack_tokens(r, GROUP) for r in (q_ref, k_ref, v_ref))
        bc, gc, gr = _stack_cols(bc_ref, GROUP), _stack_cols(gc_ref, GROUP), _stack_rows(gr_ref, GROUP)
        dm, kk, ep, em, el, vb, kb = _chunk_local(q, k, vv, bc, gc, gr, tri)
        t = _unit_lower_inverse(jnp.where(strict, bc * kk * dm, 0.0))
        a = _bnt(q, k) * dm
        u = _nn(t, vb, SOLVE_PREC)
        w = _nn(t, kb, SOLVE_PREC)
        for n, (c, h) in enumerate(_pairs(GROUP)):
            t_ref[c, h] = t[n]
            a_ref[c, h] = a[n]
            u_ref[_tok(c), _head(h)] = u[n]
            w_ref[_tok(c), _head(h)] = w[n]
        if ng:
            done()

    res = pl.pallas_call(
        body, name=name, grid=(steps,), in_specs=[tok, tok, tok, colv, colv, rowv] + [HBM] * ng,
        out_specs=[mat, mat, tok, tok] + [HBM] * ng,
        out_shape=[jax.ShapeDtypeStruct((nc, B_HEADS, CHUNK, CHUNK), F32)] * 2 + [jax.ShapeDtypeStruct((S, Wd), F32)] * 2
        + _gather_out_shapes(shards),
        scratch_shapes=_gather_scratch(ng) if ng else [], compiler_params=_arb(1),
    )(qn, kn, v, bcol, gcol, grow, *shards)
    return res[0], res[1], res[2], res[3], list(res[4:])


def _scan_decays(gc, gr):
    glast = gr[..., CHUNK - 1:CHUNK]
    return jnp.exp(gc), jnp.exp(glast - gc), jnp.exp(glast)


def _gdn_scan_specs(nc, rev):
    idx = (lambda i: nc - 1 - i) if rev else (lambda i: i)
    W = B_HEADS * B_DH
    tok = pl.BlockSpec((CHUNK, W), lambda i: (idx(i), 0))
    colv = pl.BlockSpec((None, CHUNK, B_HEADS), lambda i: (idx(i), 0, 0))
    rowv = pl.BlockSpec((None, B_HEADS, CHUNK), lambda i: (idx(i), 0, 0))
    mat = pl.BlockSpec((None, B_HEADS, CHUNK, CHUNK), lambda i: (idx(i), 0, 0, 0))
    smat = pl.BlockSpec((None, B_HEADS, B_DH, B_DH), lambda i: (idx(i), 0, 0, 0))
    return tok, colv, rowv, mat, smat


def _gdn_scan_fwd(qn, kn, u, w, a, gcol, grow, name, gather=None):
    S, Wd = qn.shape
    nc = S // CHUNK
    tok, colv, rowv, mat, smat = _gdn_scan_specs(nc, False)
    shards, layer = gather if gather is not None else ((), None)
    ng = len(shards)

    def body(q_ref, k_ref, u_ref, w_ref, a_ref, gc_ref, gr_ref, *rest):
        srcs, (o_ref, sh_ref), gouts = rest[:ng], rest[ng:ng + 2], rest[ng + 2:2 * ng + 2]
        st_ref, sems = rest[2 * ng + 2], rest[2 * ng + 3:]
        i = pl.program_id(0)
        done = _carry(_gather_phases(layer, srcs, gouts, *sems), i == 0, i == nc // 2, i == nc - 1) if ng else None

        @pl.when(i == 0)
        def _():
            st_ref[...] = jnp.zeros_like(st_ref)

        ep, em, el = _scan_decays(_stack_cols(gc_ref, 1), _stack_rows(gr_ref, 1))
        q, k, u, w = (_stack_tokens(r, 1) for r in (q_ref, k_ref, u_ref, w_ref))
        s0 = st_ref[...]
        ut = u - _bnn(w, s0)
        o = _bnn(q * ep, s0) + _bnn(a_ref[...], ut)
        st_ref[...] = el * s0 + _btn(k * em, ut)
        sh_ref[...] = s0
        for h in range(B_HEADS):
            o_ref[:, _head(h)] = o[h]
        if ng:
            done()

    res = pl.pallas_call(
        body, name=name, grid=(nc,), in_specs=[tok, tok, tok, tok, mat, colv, rowv] + [HBM] * ng,
        out_specs=[tok, smat] + [HBM] * ng,
        out_shape=[jax.ShapeDtypeStruct((S, Wd), F32), jax.ShapeDtypeStruct((nc, B_HEADS, B_DH, B_DH), F32)]
        + _gather_out_shapes(shards),
        scratch_shapes=[pltpu.VMEM((B_HEADS, B_DH, B_DH), F32)] + (_gather_scratch(ng) if ng else []),
        compiler_params=_arb(1),
    )(qn, kn, u, w, a, gcol, grow, *shards)
    return res[0], res[1], list(res[2:])


def _gdn_scan_bwd(qn, kn, u, w, a, gcol, grow, ssave, do, name):
    S, Wd = qn.shape
    nc = S // CHUNK
    tok, colv, rowv, mat, smat = _gdn_scan_specs(nc, True)

    def body(q_ref, k_ref, u_ref, w_ref, a_ref, gc_ref, gr_ref, sh_ref, do_ref,
             du_ref, dw_ref, dqd_ref, dkd_ref, da_ref, dgl_ref, ds_ref):
        i = pl.program_id(0)

        @pl.when(i == 0)
        def _():
            ds_ref[...] = jnp.zeros_like(ds_ref)

        tri, _ = _chunk_masks()
        sub4 = lax.broadcasted_iota(jnp.int32, (B_HEADS, CHUNK), 0)
        lane_last = lax.broadcasted_iota(jnp.int32, (1, CHUNK), 1) == CHUNK - 1
        ep, em, el = _scan_decays(_stack_cols(gc_ref, 1), _stack_rows(gr_ref, 1))
        q, k, u, w, dout = (_stack_tokens(r, 1) for r in (q_ref, k_ref, u_ref, w_ref, do_ref))
        s0 = sh_ref[...]
        ds = ds_ref[...]
        ut = u - _bnn(w, s0)
        dut = _btn(a_ref[...], dout) + _bnn(k * em, ds)
        ds_ref[...] = el * ds + _btn(q * ep, dout) - _btn(w, dut)
        dw = -_bnt(dut, s0)
        dqd = _bnt(dout, s0)
        dkd = _bnt(ut, ds)
        da_ref[...] = jnp.where(tri, _bnt(dout, ut), 0.0)
        d_el = jnp.sum(jnp.sum(s0 * ds, axis=-1, keepdims=True), axis=-2, keepdims=True)
        last = d_el * el
        dgl_acc = jnp.zeros((B_HEADS, CHUNK), F32)
        for h in range(B_HEADS):
            du_ref[:, _head(h)] = dut[h]
            dw_ref[:, _head(h)] = dw[h]
            dqd_ref[:, _head(h)] = dqd[h]
            dkd_ref[:, _head(h)] = dkd[h]
            dgl_acc = jnp.where(sub4 == h, jnp.where(lane_last, last[h], 0.0), dgl_acc)
        dgl_ref[...] = dgl_acc

    return pl.pallas_call(
        body, name=name, grid=(nc,), in_specs=[tok, tok, tok, tok, mat, colv, rowv, smat, tok],
        out_specs=[tok, tok, tok, tok, mat, rowv],
        out_shape=[jax.ShapeDtypeStruct((S, Wd), F32)] * 4 + [jax.ShapeDtypeStruct((nc, B_HEADS, CHUNK, CHUNK), F32),
                                                             jax.ShapeDtypeStruct((nc, B_HEADS, CHUNK), F32)],
        scratch_shapes=[pltpu.VMEM((B_HEADS, B_DH, B_DH), F32)], compiler_params=_arb(1),
    )(qn, kn, u, w, a, gcol, grow, ssave, do)


def _gdn_local_bwd(qn, kn, v, bcol, gcol, grow, tsave, du, dw, dqd, dkd, da, dgl, name, exchange=()):
    S, Wd = qn.shape
    nc = S // CHUNK
    steps = nc // GROUP
    tok, colv, rowv, mat = _gdn_group_specs(steps, Wd)
    ne = len(exchange)

    def body(q_ref, k_ref, v_ref, bc_ref, gc_ref, gr_ref, t_ref, du_ref, dw_ref, dqd_ref, dkd_ref, da_ref, dgl_ref, *rest):
        srcs, (dq_ref, dk_ref, dv_ref, dbc_ref, dgc_ref, dgr_ref) = rest[:ne], rest[ne:ne + 6]
        eouts, sems = rest[ne + 6:2 * ne + 6], rest[2 * ne + 6:]
        i = pl.program_id(0)
        done = _carry(_chips_phases(srcs, eouts, *sems), i == 0, None, i == steps - 1) if ne else None
        tri, strict = _chunk_masks()
        lane4 = lax.broadcasted_iota(jnp.int32, (CHUNK, B_HEADS), 1)
        sub4 = lax.broadcasted_iota(jnp.int32, (B_HEADS, CHUNK), 0)
        lane_last = lax.broadcasted_iota(jnp.int32, (1, CHUNK), 1) == CHUNK - 1
        q, k, vv, dut, dwv, dqd, dkd = (_stack_tokens(r, GROUP)
                                        for r in (q_ref, k_ref, v_ref, du_ref, dw_ref, dqd_ref, dkd_ref))
        bc, gc, gr = _stack_cols(bc_ref, GROUP), _stack_cols(gc_ref, GROUP), _stack_rows(gr_ref, GROUP)
        dm, kk, ep, em, el, vb, kb = _chunk_local(q, k, vv, bc, gc, gr, tri)
        t = jnp.stack([t_ref[c, h] for c, h in _pairs(GROUP)])
        dav = jnp.stack([da_ref[c, h] for c, h in _pairs(GROUP)])
        qk = _bnt(q, k)
        dt = _nt(dut, vb, SOLVE_PREC) + _nt(dwv, kb, SOLVE_PREC)
        dvb = _tn(t, dut, SOLVE_PREC)
        dkb = _tn(t, dwv, SOLVE_PREC)
        dl = jnp.where(strict, -_tn(t, _nt(dt, t, SOLVE_PREC), SOLVE_PREC), 0.0)
        g1 = dl * dm
        dkb_k = jnp.sum(dkb * k, axis=-1, keepdims=True)
        dbeta = jnp.sum(g1 * kk, axis=-1, keepdims=True) + jnp.sum(dvb * vv, axis=-1, keepdims=True) + dkb_k * ep
        dkk = g1 * bc
        ddm = dl * (bc * kk) + dav * qk
        dqk = dav * dm
        dq = _bnn(dqk, k) + dqd * ep
        dk = _btn(dqk, q) + _bnn(dkk, k) + _btn(dkk, k) + dkb * (bc * ep) + dkd * em
        dv = dvb * bc
        dep = dkb_k * bc + jnp.sum(dqd * q, axis=-1, keepdims=True)
        dem = jnp.sum(dkd * k, axis=-1, keepdims=True)
        mm = ddm * dm
        dgam_c = jnp.sum(mm, axis=-1, keepdims=True) + dep * ep - dem * em
        dglast = jnp.sum(dem * em, axis=-2, keepdims=True)
        dgam_r = -jnp.sum(mm, axis=-2, keepdims=True) + jnp.where(lane_last, dglast, 0.0)
        for c in range(GROUP):
            dbc_acc = jnp.zeros((CHUNK, B_HEADS), F32)
            dgc_acc = jnp.zeros((CHUNK, B_HEADS), F32)
            dgr_acc = jnp.zeros((B_HEADS, CHUNK), F32)
            for h in range(B_HEADS):
                n = c * B_HEADS + h
                dq_ref[_tok(c), _head(h)] = dq[n]
                dk_ref[_tok(c), _head(h)] = dk[n]
                dv_ref[_tok(c), _head(h)] = dv[n]
                dbc_acc = jnp.where(lane4 == h, dbeta[n], dbc_acc)
                dgc_acc = jnp.where(lane4 == h, dgam_c[n], dgc_acc)
                dgr_acc = jnp.where(sub4 == h, dgam_r[n], dgr_acc)
            dbc_ref[c] = dbc_acc
            dgc_ref[c] = dgc_acc
            dgr_ref[c] = dgr_acc + dgl_ref[c]
        if ne:
            done()

    res = pl.pallas_call(
        body, name=name, grid=(steps,),
        in_specs=[tok, tok, tok, colv, colv, rowv, mat, tok, tok, tok, tok, mat, rowv] + [HBM] * ne,
        out_specs=[tok, tok, tok, colv, colv, rowv] + [HBM] * ne,
        out_shape=[jax.ShapeDtypeStruct((S, Wd), F32)] * 3
        + [jax.ShapeDtypeStruct((nc, CHUNK, B_HEADS), F32)] * 2 + [jax.ShapeDtypeStruct((nc, B_HEADS, CHUNK), F32)]
        + _chips_out_shapes(exchange),
        scratch_shapes=_chips_scratch(ne) if ne else [], compiler_params=_arb(1),
    )(qn, kn, v, bcol, gcol, grow, tsave, du, dw, dqd, dkd, da, dgl, *exchange)
    return tuple(res[:6]) + (list(res[6:]),)


def _gdn_post_fwd(o, proj, ng, name):
    S, W = o.shape
    tm = _pick(S, 512, 8)

    def body(o_ref, z_ref, g_ref, y_ref):
        gv = g_ref[...]
        for h in range(B_HEADS):
            hs = slice(h * B_DH, (h + 1) * B_DH)
            oh = o_ref[:, hs]
            z = z_ref[:, hs]
            r = lax.rsqrt(jnp.mean(oh * oh, axis=-1, keepdims=True) + EPS)
            y_ref[:, hs] = (oh * r * gv * (z * _sigmoid(z))).astype(BF16)

    return pl.pallas_call(
        body, name=name, grid=(S // tm,), in_specs=[_rows(tm, W), _rows(tm, W, P_Z // W), _vec(B_DH)],
        out_specs=_rows(tm, W), out_shape=jax.ShapeDtypeStruct((S, W), BF16), compiler_params=_par(1),
    )(o, proj, ng)


def _gdn_post_bwd(dy, o, proj, ng, name):
    S, W = o.shape
    tm = _pick(S, 512, 8)

    def body(dy_ref, o_ref, z_ref, g_ref, do_ref, dz_ref, dg_ref):
        i = pl.program_id(0)
        gv = g_ref[...]
        dg = jnp.zeros((1, B_DH), F32)
        for h in range(B_HEADS):
            hs = slice(h * B_DH, (h + 1) * B_DH)
            oh = o_ref[:, hs]
            z = z_ref[:, hs]
            d = dy_ref[:, hs]
            r = lax.rsqrt(jnp.mean(oh * oh, axis=-1, keepdims=True) + EPS)
            n = oh * r
            sg = _sigmoid(z)
            sz = z * sg
            dn = d * gv * sz
            dg = dg + jnp.sum(d * n * sz, axis=0, keepdims=True)
            dz_ref[:, hs] = (d * n * gv * (sg * (1.0 + z * (1.0 - sg)))).astype(BF16)
            do_ref[:, hs] = r * (dn - n * jnp.mean(dn * n, axis=-1, keepdims=True))
        _acc(dg_ref, dg, i)

    return pl.pallas_call(
        body, name=name, grid=(S // tm,), in_specs=[_rows(tm, W), _rows(tm, W), _rows(tm, W, P_Z // W), _vec(B_DH)],
        out_specs=[_rows(tm, W), _rows(tm, W), _vec(B_DH)],
        out_shape=[jax.ShapeDtypeStruct((S, W), F32), jax.ShapeDtypeStruct((S, W), BF16),
                   jax.ShapeDtypeStruct((1, B_DH), F32)],
        compiler_params=_arb(1),
    )(dy, o, proj, ng)


def _ada_mod(c_all, w_ada, b_shard, name):
    L, D, Ns = w_ada.shape
    B = c_all.shape[0]

    def body(c_ref, w_ref, b_ref, o_ref):
        cv = c_ref[...]
        cond = (cv * _sigmoid(cv)).astype(BF16)
        o_ref[...] = _nn(cond, w_ref[...].astype(BF16)) + b_ref[...]

    return pl.pallas_call(
        body, name=name, grid=(L,),
        in_specs=[pl.BlockSpec((B, D), lambda l: (0, 0)), pl.BlockSpec((None, D, Ns), lambda l: (l, 0, 0)),
                  pl.BlockSpec((None, 1, Ns), lambda l: (l, 0, 0))],
        out_specs=pl.BlockSpec((None, B, Ns), lambda l: (l, 0, 0)),
        out_shape=jax.ShapeDtypeStruct((L, B, Ns), F32), compiler_params=_par(1),
    )(c_all, w_ada, b_shard)


def _ada_wgrad(c_all, dmod, name):
    L, B, Ns = dmod.shape
    D = c_all.shape[1]

    def body(c_ref, d_ref, o_ref):
        cv = c_ref[...]
        cond = (cv * _sigmoid(cv)).astype(BF16)
        o_ref[...] = _tn(cond, d_ref[...].astype(BF16))

    return pl.pallas_call(
        body, name=name, grid=(L,),
        in_specs=[pl.BlockSpec((B, D), lambda l: (0, 0)), pl.BlockSpec((None, B, Ns), lambda l: (l, 0, 0))],
        out_specs=pl.BlockSpec((None, D, Ns), lambda l: (l, 0, 0)),
        out_shape=jax.ShapeDtypeStruct((L, D, Ns), F32), compiler_params=_par(1),
    )(c_all, dmod)


W_IN_PIECES = ((0, 0, 1410), (1, 0, 1410), (2, 0, 252), (2, 772, 638), (3, 0, 1410), (2, 252, 512), (2, 764, 8))


def _reorder_w_in(w4, name):
    L, _, D, Cs = w4.shape
    tm = _pick(D, 256, 16)
    used = sum(p[2] for p in W_IN_PIECES)

    def body(w_ref, o_ref):
        shard = [w_ref[s] for s in range(4)]
        parts = [shard[s][:, lo:lo + n] for s, lo, n in W_IN_PIECES]
        o_ref[...] = jnp.concatenate(parts + [jnp.zeros((tm, P_END - used), w4.dtype)], axis=1)

    return pl.pallas_call(
        body, name=name, grid=(L, D // tm), in_specs=[pl.BlockSpec((None, 4, tm, Cs), lambda l, i: (l, 0, i, 0))],
        out_specs=pl.BlockSpec((None, tm, P_END), lambda l, i: (l, i, 0)),
        out_shape=jax.ShapeDtypeStruct((L, D, P_END), w4.dtype), compiler_params=_par(2),
    )(w4)


def _restore_w_in(g, name):
    D = g.shape[0]
    tm = _pick(D, 256, 16)

    def body(g_ref, o_ref, ob_ref):
        gv = g_ref[...]
        off = 0
        pieces = {}
        for s, lo, n in W_IN_PIECES:
            pieces.setdefault(s, []).append((lo, gv[:, off:off + n]))
            off += n
        for s, lst in pieces.items():
            lst.sort(key=lambda t: t[0])
            shard = lst[0][1] if len(lst) == 1 else jnp.concatenate([t[1] for t in lst], axis=1)
            o_ref[s] = shard
            ob_ref[s] = shard.astype(BF16)

    spec = pl.BlockSpec((4, tm, W_IN_SHARD), lambda i: (0, i, 0))
    return pl.pallas_call(
        body, name=name, grid=(D // tm,), in_specs=[pl.BlockSpec((tm, P_END), lambda i: (i, 0))],
        out_specs=[spec, spec],
        out_shape=[jax.ShapeDtypeStruct((4, D, W_IN_SHARD), g.dtype), jax.ShapeDtypeStruct((4, D, W_IN_SHARD), BF16)],
        compiler_params=_par(1),
    )(g)


def _adam_update(w, g, m, v):
    mn = ADAM_B1 * m + (1.0 - ADAM_B1) * g
    vn = ADAM_B2 * v + (1.0 - ADAM_B2) * (g * g)
    m_hat = mn / (1.0 - ADAM_B1 ** ADAM_STEP)
    v_hat = vn / (1.0 - ADAM_B2 ** ADAM_STEP)
    return -ADAM_LR * (m_hat / (jnp.sqrt(v_hat) + ADAM_EPS) + ADAM_WD * w), mn, vn


def _adamw(w, g, m, v, name):
    shape = w.shape
    C = shape[-1]
    R = w.size // C
    tm = _pick(R, 512, 8)
    spec = pl.BlockSpec((tm, C), lambda i: (i, 0))

    def body(w_ref, g_ref, m_ref, v_ref, d_ref, mo_ref, vo_ref):
        d_ref[...], mo_ref[...], vo_ref[...] = _adam_update(w_ref[...], g_ref[...], m_ref[...], v_ref[...])

    outs = pl.pallas_call(
        body, name=name, grid=(R // tm,), in_specs=[spec] * 4, out_specs=[spec] * 3,
        out_shape=[jax.ShapeDtypeStruct((R, C), F32)] * 3, compiler_params=_par(1),
    )(*(t.reshape(R, C) for t in (w, g, m, v)))
    return tuple(o.reshape(shape) for o in outs)


def _adamw_lead(w, g, m, v, name, tl):
    A, B, C = w.shape
    spec = pl.BlockSpec((tl, B, C), lambda i: (i, 0, 0))

    def body(w_ref, g_ref, m_ref, v_ref, d_ref, mo_ref, vo_ref):
        d_ref[...], mo_ref[...], vo_ref[...] = _adam_update(w_ref[...], g_ref[...], m_ref[...], v_ref[...])

    return pl.pallas_call(
        body, name=name, grid=(A // tl,), in_specs=[spec] * 4, out_specs=[spec] * 3,
        out_shape=[jax.ShapeDtypeStruct((A, B, C), F32)] * 3, compiler_params=_par(1),
    )(w, g, m, v)


def _adamw_layers(w, gs, m, v, name):
    L, R, C = w.shape
    tm = _pick(R, 128, 8)
    spec = pl.BlockSpec((None, tm, C), lambda l, i: (l, i, 0))
    g_specs = [pl.BlockSpec((tm, C), functools.partial(lambda ll, l, i: (jnp.where(l == ll, i, 0), 0), ll))
               for ll in range(L)]

    def body(w_ref, m_ref, v_ref, *rest):
        g_refs, (go_ref, d_ref, mo_ref, vo_ref) = rest[:L], rest[L:]
        l = pl.program_id(0)
        for ll in range(L):
            @pl.when(l == ll)
            def _():
                g = g_refs[ll][...]
                go_ref[...] = g
                d_ref[...], mo_ref[...], vo_ref[...] = _adam_update(w_ref[...], g, m_ref[...], v_ref[...])

    return pl.pallas_call(
        body, name=name, grid=(L, R // tm), in_specs=[spec] * 3 + g_specs, out_specs=[spec] * 4,
        out_shape=[jax.ShapeDtypeStruct((L, R, C), F32)] * 4, compiler_params=_arb(2),
    )(w, m, v, *gs)


def _pair_sums(a, where, b, name):
    NB, _, R, C = a.shape

    def body(where_ref, a_ref, b_ref, p_ref, own_ref):
        s = a_ref[...] + b_ref[...].astype(F32)
        p_ref[...] = s.astype(BF16)

        @pl.when(pl.program_id(0) == where_ref[1])
        def _():
            own_ref[...] = s

    return pl.pallas_call(
        body, name=name,
        grid_spec=pltpu.PrefetchScalarGridSpec(
            num_scalar_prefetch=1, grid=(NB,),
            in_specs=[pl.BlockSpec((None, None, R, C), lambda k, w: (k, w[0], 0, 0)),
                      pl.BlockSpec((None, R, C), lambda k, w: (k, 0, 0))],
            out_specs=[pl.BlockSpec((None, R, C), lambda k, w: (k, 0, 0)), pl.BlockSpec((R, C), lambda k, w: (0, 0))]),
        out_shape=[jax.ShapeDtypeStruct((NB, R, C), BF16), jax.ShapeDtypeStruct((R, C), F32)],
        compiler_params=_arb(1),
    )(where, a, b)


def _sum_own_and_received(own, recv, where, name):
    R, C = own.shape
    tm = _pick(R, 256, 16)

    def body(where_ref, p_ref, r_ref, o_ref):
        o_ref[...] = ((p_ref[...] + r_ref[0].astype(F32)) + r_ref[1].astype(F32)) + r_ref[2].astype(F32)

    return pl.pallas_call(
        body, name=name,
        grid_spec=pltpu.PrefetchScalarGridSpec(
            num_scalar_prefetch=1, grid=(R // tm,),
            in_specs=[pl.BlockSpec((tm, C), lambda i, w: (i, 0)), pl.BlockSpec((3, tm, C), lambda i, w: (0, i, 0))],
            out_specs=pl.BlockSpec((None, tm, C), lambda i, w: (w[0], i, 0))),
        out_shape=jax.ShapeDtypeStruct((2, R, C), F32), compiler_params=_par(1),
    )(where, own, recv)


def _position():
    return lax.axis_index("x"), lax.axis_index("y"), lax.axis_index("c")


def _other_chips(x, y):
    return [(1 - x, y), (x, 1 - y), (1 - x, 1 - y)]


HBM = pl.BlockSpec(memory_space=pl.ANY)


def _allgather8(blk, name, reduce_rows=None):
    M, N = blk.shape

    def body(x_ref, out_ref, *rest):
        if reduce_rows is None:
            send_sems, recv_sems, local_sem = rest
        else:
            sum_ref, send_sems, recv_sems, local_sem = rest
        x, y, c = _position()
        me, sibling = (x, y, c), (x, y, 1 - c)
        chips = _other_chips(x, y)

        def rows(px, py, pc):
            return out_ref.at[pl.ds((4 * px + 2 * py + pc) * M, M), :]

        def copy(k, block, to, src=None):
            return pltpu.make_async_remote_copy(
                src_ref=rows(*block) if src is None else src, dst_ref=rows(*block),
                send_sem=send_sems.at[k], recv_sem=recv_sems.at[k], device_id=to, device_id_type=MESH)

        mine = pltpu.make_async_copy(x_ref, rows(*me), local_sem)
        mine.start()
        first = [copy(0, me, sibling, src=x_ref)]
        first += [copy(1 + j, me, (*chip, c), src=x_ref) for j, chip in enumerate(chips)]
        for cp in first:
            cp.start()
        passed = [copy(4 + j, (*chip, c), sibling) for j, chip in enumerate(chips)]
        for j, chip in enumerate(chips):
            copy(1 + j, (*chip, c), me).wait_recv()
            passed[j].start()
        copy(0, sibling, me).wait_recv()
        for j, chip in enumerate(chips):
            copy(4 + j, (*chip, 1 - c), me).wait_recv()
        for cp in first + passed:
            cp.wait_send()
        mine.wait()
        if reduce_rows is not None:
            tot = out_ref[pl.ds(0, reduce_rows), :]
            for d in range(1, 8):
                tot = tot + out_ref[pl.ds(d * M, reduce_rows), :]
            sum_ref[...] = tot

    vmem = pl.BlockSpec(memory_space=pltpu.VMEM)
    out_shape = [jax.ShapeDtypeStruct((8 * M, N), blk.dtype)]
    if reduce_rows is not None:
        out_shape.append(jax.ShapeDtypeStruct((reduce_rows, N), blk.dtype))
    res = pl.pallas_call(
        body, name=name, out_shape=out_shape, in_specs=[vmem], out_specs=[vmem] * len(out_shape),
        scratch_shapes=[pltpu.SemaphoreType.DMA((7,)), pltpu.SemaphoreType.DMA((7,)), pltpu.SemaphoreType.DMA],
    )(blk)
    return res[0] if reduce_rows is None else (res[0], res[1])


def _gather_phases(layer, srcs, outs, send_sems, recv_sems, local_sems):
    n = len(srcs)
    x, y, c = _position()
    me, sibling = (x, y, c), (x, y, 1 - c)
    chips = _other_chips(x, y)

    def region(t, px, py, pc):
        return outs[t].at[2 * px + py, pc]

    def copy(t, k, block, to, own=False):
        return pltpu.make_async_remote_copy(
            src_ref=srcs[t].at[layer, c] if own else region(t, *block), dst_ref=region(t, *block),
            send_sem=send_sems.at[7 * t + k], recv_sem=recv_sems.at[7 * t + k], device_id=to, device_id_type=MESH)

    def local(t):
        return pltpu.make_async_copy(srcs[t].at[layer, c], region(t, *me), local_sems.at[t])

    def first(t):
        return [copy(t, 0, me, sibling, own=True)] + [copy(t, 1 + j, me, (*chip, c), own=True)
                                                       for j, chip in enumerate(chips)]

    def start():
        for t in range(n):
            local(t).start()
        for t in range(n):
            for cp in first(t):
                cp.start()

    def forward():
        for j, chip in enumerate(chips):
            for t in range(n):
                copy(t, 1 + j, (*chip, c), me).wait_recv()
                copy(t, 4 + j, (*chip, c), sibling).start()

    def finish():
        for t in range(n):
            copy(t, 0, sibling, me).wait_recv()
        for j, chip in enumerate(chips):
            for t in range(n):
                copy(t, 4 + j, (*chip, 1 - c), me).wait_recv()
        for t in range(n):
            for cp in first(t) + [copy(t, 4 + j, (*chip, c), sibling) for j, chip in enumerate(chips)]:
                cp.wait_send()
            local(t).wait()

    return start, forward, finish


def _gather_scratch(n):
    return [pltpu.SemaphoreType.DMA((7 * n,)), pltpu.SemaphoreType.DMA((7 * n,)), pltpu.SemaphoreType.DMA((n,))]


def _gather_out_shapes(shards):
    return [jax.ShapeDtypeStruct((4,) + s.shape[1:], s.dtype) for s in shards]


def _gather_weights(shards, layer, name):
    n = len(shards)

    def body(*refs):
        start, forward, finish = _gather_phases(layer, refs[:n], refs[n:2 * n], *refs[2 * n:])
        start()
        forward()
        finish()

    return pl.pallas_call(
        body, name=name, out_shape=_gather_out_shapes(shards), in_specs=[HBM] * n, out_specs=[HBM] * n,
        scratch_shapes=_gather_scratch(n),
    )(*shards)


def _rs_sibling(gs, name):
    n = len(gs)

    def body(*refs):
        srcs, outs = refs[:n], refs[n:2 * n]
        send_sems, recv_sems = refs[2 * n:]
        x, y, c = _position()
        copies = [pltpu.make_async_remote_copy(
            src_ref=srcs[t].at[k, 1 - c], dst_ref=outs[t].at[k], send_sem=send_sems.at[4 * t + k],
            recv_sem=recv_sems.at[4 * t + k], device_id=(x, y, 1 - c), device_id_type=MESH)
            for t in range(n) for k in range(4)]
        for cp in copies:
            cp.start()
        for cp in copies:
            cp.wait()

    out_shape = [jax.ShapeDtypeStruct((4,) + g.shape[2:], g.dtype) for g in gs]
    return pl.pallas_call(
        body, name=name, out_shape=out_shape, in_specs=[HBM] * n, out_specs=[HBM] * n,
        scratch_shapes=[pltpu.SemaphoreType.DMA((4 * n,)), pltpu.SemaphoreType.DMA((4 * n,))],
    )(*gs)


def _rs_chips(ps, name):
    n = len(ps)

    def body(*refs):
        start, finish = _chips_phases(refs[:n], refs[n:2 * n], *refs[2 * n:])
        start()
        finish()

    return pl.pallas_call(
        body, name=name, out_shape=_chips_out_shapes(ps), in_specs=[HBM] * n, out_specs=[HBM] * n,
        scratch_shapes=_chips_scratch(n),
    )(*ps)


def _chips_phases(srcs, outs, send_sems, recv_sems):
    x, y, c = _position()
    copies = [pltpu.make_async_remote_copy(
        src_ref=srcs[t].at[2 * px + py], dst_ref=outs[t].at[j], send_sem=send_sems.at[3 * t + j],
        recv_sem=recv_sems.at[3 * t + j], device_id=(px, py, c), device_id_type=MESH)
        for t in range(len(srcs)) for j, (px, py) in enumerate(_other_chips(x, y))]

    def start():
        for cp in copies:
            cp.start()

    def finish():
        for cp in copies:
            cp.wait()

    return start, finish


def _chips_scratch(n):
    return [pltpu.SemaphoreType.DMA((3 * n,)), pltpu.SemaphoreType.DMA((3 * n,))]


def _chips_out_shapes(ps):
    return [jax.ShapeDtypeStruct((3,) + p.shape[1:], p.dtype) for p in ps]


def _rs_pair(hs, name):
    n = len(hs)

    def body(*refs):
        bufs = refs[n:2 * n]
        send_sems, recv_sems = refs[2 * n:]
        x, y, c = _position()

        def copy(t, half):
            return pltpu.make_async_remote_copy(
                src_ref=bufs[t].at[half], dst_ref=bufs[t].at[half], send_sem=send_sems.at[t], recv_sem=recv_sems.at[t],
                device_id=(x, y, 1 - c), device_id_type=MESH)

        for t in range(n):
            copy(t, c).start()
        for t in range(n):
            copy(t, 1 - c).wait_recv()
        for t in range(n):
            copy(t, c).wait_send()

    out_shape = [jax.ShapeDtypeStruct(h.shape, h.dtype) for h in hs]
    return pl.pallas_call(
        body, name=name, out_shape=out_shape, in_specs=[HBM] * n, out_specs=[HBM] * n,
        input_output_aliases={t: t for t in range(n)},
        scratch_shapes=[pltpu.SemaphoreType.DMA((n,)), pltpu.SemaphoreType.DMA((n,))],
    )(*hs)


BIG = ("w_in", "w_branch_a", "w_branch_b", "w_out", "w_ffn_in", "w_ffn_out")
CARRY_ATTN = ["w_in"]
CARRY_LOCAL = ["w_ffn_in"]
CARRY_SCAN = ["w_branch_a", "w_branch_b", "w_out", "w_ffn_out"]
CARRY_DATTN = ["w_in", "w_ffn_in"]
CARRY_DLOCAL = ["w_branch_a", "w_branch_b", "w_out", "w_ffn_out"]


def _band_bias(rel_table, name):
    L, H, n = rel_table.shape
    tab = jnp.pad(rel_table, ((0, 0), (0, 0), (0, NREL_PAD - n))).reshape(L * H, 1, NREL_PAD)
    band = (A_PAST + 1) * CHUNK

    def body(t_ref, o_ref):
        r = lax.broadcasted_iota(jnp.int32, (NREL_PAD, SKEW_W), 0)
        xi = lax.broadcasted_iota(jnp.int32, (NREL_PAD, SKEW_W), 1)
        diag = jnp.where(xi < KSPAN, xi, xi - SKEW_W)
        rel = jnp.clip(A_PAST * CHUNK - diag, -A_MAX_REL, A_MAX_REL) + A_MAX_REL
        e = _nn(t_ref[...], jnp.where(rel == r, 1.0, 0.0).astype(F32), HI)
        x = jnp.broadcast_to(e, (QBLK, SKEW_W))
        row = lax.broadcasted_iota(jnp.int32, (QBLK, SKEW_W), 0)
        for b in range(QBLK.bit_length() - 1):
            x = jnp.where(((row >> b) & 1) == 1, pltpu.roll(x, 1 << b, 1), x)
        x = x[:, :KSPAN]
        first = (lax.broadcasted_iota(jnp.int32, (QBLK, KSPAN), 0) // CHUNK) * CHUNK
        col = lax.broadcasted_iota(jnp.int32, (QBLK, KSPAN), 1)
        o_ref[...] = jnp.where((col >= first) & (col < first + band), x, NEG)

    out = pl.pallas_call(
        body, name=name, grid=(L * H,), in_specs=[pl.BlockSpec((None, 1, NREL_PAD), lambda i: (i, 0, 0))],
        out_specs=pl.BlockSpec((None, QBLK, KSPAN), lambda i: (i, 0, 0)),
        out_shape=jax.ShapeDtypeStruct((L * H, QBLK, KSPAN), F32), compiler_params=_par(1),
    )(tab)
    return out.reshape(L, H, QBLK, KSPAN)


def _col_row_forms(t, S):
    nc = S // CHUNK
    return t.T.reshape(nc, CHUNK, B_HEADS), t.reshape(B_HEADS, nc, CHUNK).transpose(1, 0, 2)


def _layer_fwd(l, x, mod, W, P, big, gather=None):
    S, D = x.shape
    n = lambda s: f"{s}_l{l}"
    sh1, sc1, gt1, sh2, sc2, gt2 = (mod[i:i + 1] for i in range(6))
    h1 = _lnmod_fwd(x, P["norm1_g"][l:l + 1], sc1, sh1, n("ln1"))
    proj = _matmul(h1, W["w_in"], "nn", F32, n("proj"), tn=1152)
    part = (lambda names: ([gather[0][BIG.index(k)] for k in names], gather[1])) if gather is not None else (lambda names: None)
    ya, got_a = _attn_fwd(proj, big, n("attn"), part(CARRY_ATTN))
    ba = proj[:, P_BA:P_BA + 2 * B_HEADS]
    b_t, a_t = ba[:, :B_HEADS].T, ba[:, B_HEADS:].T
    alog, dtb = P["a_log"][l].reshape(B_HEADS, 1), P["dt_bias"][l].reshape(B_HEADS, 1)
    beta, gam = _gdn_gates_fwd(b_t, a_t, alog, dtb, n("gates"))
    bcol, _ = _col_row_forms(beta, S)
    gcol, grow = _col_row_forms(gam, S)
    qn, kn, v = _gdn_pre_fwd(proj, P["w_conv"][l], n("gdnpre"))
    tsave, amat, u, w, got_l = _gdn_local_fwd(qn, kn, v, bcol, gcol, grow, n("gdnlocal"), part(CARRY_LOCAL))
    o, ssave, got_s = _gdn_scan_fwd(qn, kn, u, w, amat, gcol, grow, n("gdnscan"), part(CARRY_SCAN))
    got = dict(zip(CARRY_ATTN + CARRY_LOCAL + CARRY_SCAN, got_a + got_l + got_s))
    gathered = [got[k] for k in BIG] if gather is not None else None
    yb = _gdn_post_fwd(o, proj, P["gdn_norm_g"][l:l + 1], n("gdnpost"))
    pa = _matmul(ya, W["w_branch_a"], "nn", BF16, n("pa"), tm=2048, stacked=True)
    pb = _matmul(yb, W["w_branch_b"], "nn", BF16, n("pb"), tm=2048, stacked=True)
    merged = _merge_fwd(proj, pa, pb, n("merge"))
    ao = _matmul(merged, W["w_out"], "nn", F32, n("ao"))
    x1 = _gate_fwd(x, ao, gt1, n("res1"))
    h2 = _lnmod_fwd(x1, P["norm2_g"][l:l + 1], sc2, sh2, n("ln2"))
    gu = _matmul(h2, W["w_ffn_in"], "nn", BF16, n("gu"), stacked=True)
    act = _ffn_act_fwd(gu, n("act"))
    fo = _matmul(act, W["w_ffn_out"], "nn", F32, n("fo"), tk=1408)
    x2 = _gate_fwd(x1, fo, gt2, n("res2"))
    saved = dict(x=x, h1=h1, proj=proj, ya=ya, b_t=b_t, a_t=a_t, bcol=bcol, gcol=gcol, grow=grow,
                 qn=qn, kn=kn, v=v, o=o, tsave=tsave, ssave=ssave, amat=amat, u=u, w=w, yb=yb, pa=pa, pb=pb,
                 merged=merged, ao=ao, x1=x1,
                 h2=h2, gu=gu, act=act, fo=fo)
    return x2, saved, gathered


def _layer_bwd(l, dx2, sv, mod, W, P, big, exchange=()):
    S, D = dx2.shape
    n = lambda s: f"{s}_l{l}"
    sh1, sc1, gt1, sh2, sc2, gt2 = (mod[i:i + 1] for i in range(6))
    g, pay = {}, {}
    view = lambda t: t.reshape((4, 2, t.shape[-2] // (2 if t.ndim == 3 else 8), t.shape[-1]))
    dz2, dgt2 = _gate_bwd(dx2, sv["fo"], gt2, n("dres2"))
    g["w_ffn_out"], pay["w_ffn_out"] = map(view, _matmul(sv["act"], dz2, "tn", F32, n("dwfo"), tm=1408, also_bf16=True))
    dact = _matmul(dz2, W["w_ffn_out"], "nt", BF16, n("dact"), tn=1408)
    dgu = _ffn_act_bwd(sv["gu"], dact, n("dgu"))
    g["w_ffn_in"], pay["w_ffn_in"] = map(view, _matmul(sv["h2"], dgu, "tn", F32, n("dwfi"), out_stacked=True,
                                                       also_bf16=True))
    dh2 = _matmul(dgu, W["w_ffn_in"], "nt", F32, n("dh2"), stacked=True)
    dx1, dsh2, dsc2, dn2 = _lnmod_bwd(dh2, sv["x1"], P["norm2_g"][l:l + 1], sc2, dx2, n("dln2"))
    dz1, dgt1 = _gate_bwd(dx1, sv["ao"], gt1, n("dres1"))
    g["w_out"], pay["w_out"] = map(view, _matmul(sv["merged"], dz1, "tn", F32, n("dwo"), also_bf16=True))
    dmerged = _matmul(dz1, W["w_out"], "nt", BF16, n("dmerged"))
    dgab, dpa, dpb = _merge_bwd(sv["proj"], sv["pa"], sv["pb"], dmerged, n("dmerge"))
    g["w_branch_a"], pay["w_branch_a"] = map(view, _matmul(sv["ya"], dpa, "tn", F32, n("dwa"), out_stacked=True,
                                                           also_bf16=True))
    g["w_branch_b"], pay["w_branch_b"] = map(view, _matmul(sv["yb"], dpb, "tn", F32, n("dwb"), out_stacked=True,
                                                           also_bf16=True))
    dya = _matmul(dpa, W["w_branch_a"], "nt", BF16, n("dya"), tm=2048, stacked=True)
    dyb = _matmul(dpb, W["w_branch_b"], "nt", F32, n("dyb"), tm=2048, stacked=True)
    ex = (lambda names: [exchange[BIG.index(k)] for k in names]) if len(exchange) else (lambda names: ())
    dq, dk, dv, dbig, rec_a = _attn_bwd(sv["proj"], big, dya, n("dattn"), ex(CARRY_DATTN))
    g["rel_table"] = _rel_table_grad(dbig, n("drel"))[:, 0, :2 * A_MAX_REL + 1]
    do, dzb, dng = _gdn_post_bwd(dyb, sv["o"], sv["proj"], P["gdn_norm_g"][l:l + 1], n("dgdnpost"))
    g["gdn_norm_g"] = dng[0]
    du, dw, dqd, dkd, da, dgl = _gdn_scan_bwd(sv["qn"], sv["kn"], sv["u"], sv["w"], sv["amat"], sv["gcol"], sv["grow"],
                                              sv["ssave"], do, n("dgdnscan"))
    dqn, dkn, dvv, dbc, dgc, dgr, rec_l = _gdn_local_bwd(
        sv["qn"], sv["kn"], sv["v"], sv["bcol"], sv["gcol"], sv["grow"], sv["tsave"], du, dw, dqd, dkd, da, dgl,
        n("dgdnlocal"), ex(CARRY_DLOCAL))
    rec = dict(zip(CARRY_DATTN + CARRY_DLOCAL, rec_a + rec_l))
    received = [rec[k] for k in BIG] if len(exchange) else None
    dbeta_t = dbc.reshape(S, B_HEADS).T
    dgam_a = dgc.reshape(S, B_HEADS).T
    dgam_b = dgr.transpose(1, 0, 2).reshape(B_HEADS, S)
    alog, dtb = P["a_log"][l].reshape(B_HEADS, 1), P["dt_bias"][l].reshape(B_HEADS, 1)
    db_t, da_t, dal, ddt = _gdn_gates_bwd(dbeta_t, dgam_a, dgam_b, sv["b_t"], sv["a_t"], alog, dtb, n("dgates"))
    g["a_log"], g["dt_bias"] = dal[:, 0], ddt[:, 0]
    dy = _gdn_pre_bwd_a(sv["proj"], P["w_conv"][l], dqn, dkn, dvv, n("dgdnpre_a"))
    dqkvb, g["w_conv"] = _gdn_pre_bwd_b(sv["proj"], P["w_conv"][l], dy, n("dgdnpre_b"))
    dba = jnp.concatenate([db_t.T, da_t.T, jnp.zeros((S, P_END - P_BA - 2 * B_HEADS), F32)], axis=1)
    dproj = jnp.concatenate([dq.astype(BF16), dk.astype(BF16), dv.astype(BF16), dqkvb, dgab, dzb, dba.astype(BF16)],
                            axis=1)
    g["w_in"], pay["w_in"] = map(view, _restore_w_in(_matmul(sv["h1"], dproj, "tn", F32, n("dwin"), tn=1152),
                                                     n("dwin_cols")))
    dh1 = _matmul(dproj, W["w_in"], "nt", F32, n("dh1"), tk=1152)
    dx, dsh1, dsc1, dn1 = _lnmod_bwd(dh1, sv["x"], P["norm1_g"][l:l + 1], sc1, dx1, n("dln1"))
    g["norm1_g"], g["norm2_g"] = dn1[0], dn2[0]
    dmod = jnp.concatenate([dsh1, dsc1, dgt1, dsh2, dsc2, dgt2], axis=1)[0]
    return dx, g, pay, dmod, received


SMALL = ("norm1_g", "norm2_g", "rel_table", "w_conv", "a_log", "dt_bias", "gdn_norm_g")
SMALL_PACK_C = 1024


def _as_rows(t):
    flat = t.reshape(-1)
    rows = -(-flat.shape[0] // SMALL_PACK_C)
    return jnp.pad(flat, (0, rows * SMALL_PACK_C - flat.shape[0])).reshape(rows, SMALL_PACK_C)


def _pack_rows(parts):
    blk = jnp.concatenate([_as_rows(p) for p in parts], axis=0)
    return jnp.pad(blk, ((0, -blk.shape[0] % 8), (0, 0)))


def _unpack_rows(blk, shapes):
    out, r = [], 0
    for shp in shapes:
        size = int(np.prod(shp))
        rows = -(-size // SMALL_PACK_C)
        out.append(blk[..., r:r + rows, :].reshape(blk.shape[:-2] + (rows * SMALL_PACK_C,))[..., :size]
                   .reshape(blk.shape[:-2] + tuple(shp)))
        r += rows
    return out


def kernel(x, c, w_ada, b_ada, norm1_g, norm2_g, w_in, rel_table, w_conv, a_log, dt_bias, gdn_norm_g, w_branch_a, w_branch_b, w_out, w_ffn_in, w_ffn_out, final_g, loss_target, m_w_ada, m_b_ada, m_norm1_g, m_norm2_g, m_w_in, m_rel_table, m_w_conv, m_a_log, m_dt_bias, m_gdn_norm_g, m_w_branch_a, m_w_branch_b, m_w_out, m_w_ffn_in, m_w_ffn_out, m_final_g, v_w_ada, v_b_ada, v_norm1_g, v_norm2_g, v_w_in, v_rel_table, v_w_conv, v_a_log, v_dt_bias, v_gdn_norm_g, v_w_branch_a, v_w_branch_b, v_w_out, v_w_ffn_in, v_w_ffn_out, v_final_g):
    weights = dict(w_ada=w_ada, b_ada=b_ada, norm1_g=norm1_g, norm2_g=norm2_g, w_in=w_in, rel_table=rel_table,
                   w_conv=w_conv, a_log=a_log, dt_bias=dt_bias, gdn_norm_g=gdn_norm_g, w_branch_a=w_branch_a,
                   w_branch_b=w_branch_b, w_out=w_out, w_ffn_in=w_ffn_in, w_ffn_out=w_ffn_out, final_g=final_g)
    mom_m = dict(w_ada=m_w_ada, b_ada=m_b_ada, norm1_g=m_norm1_g, norm2_g=m_norm2_g, w_in=m_w_in,
                 rel_table=m_rel_table, w_conv=m_w_conv, a_log=m_a_log, dt_bias=m_dt_bias, gdn_norm_g=m_gdn_norm_g,
                 w_branch_a=m_w_branch_a, w_branch_b=m_w_branch_b, w_out=m_w_out, w_ffn_in=m_w_ffn_in,
                 w_ffn_out=m_w_ffn_out, final_g=m_final_g)
    mom_v = dict(w_ada=v_w_ada, b_ada=v_b_ada, norm1_g=v_norm1_g, norm2_g=v_norm2_g, w_in=v_w_in,
                 rel_table=v_rel_table, w_conv=v_w_conv, a_log=v_a_log, dt_bias=v_dt_bias, gdn_norm_g=v_gdn_norm_g,
                 w_branch_a=v_w_branch_a, w_branch_b=v_w_branch_b, w_out=v_w_out, w_ffn_in=v_w_ffn_in,
                 w_ffn_out=v_w_ffn_out, final_g=v_final_g)
    xi, yi, ci = _position()
    chip = 2 * xi + yi
    dev = 2 * chip + ci
    L, D = norm1_g.shape
    NMOD = b_ada.shape[1] // D
    ns = w_ada.shape[2]
    cs = w_conv.shape[2]

    first_blk = _pack_rows([c, w_conv])
    first_all = _allgather8(first_blk, "gather_c").reshape(8, first_blk.shape[0], SMALL_PACK_C)
    c_all, w_conv_all = _unpack_rows(first_all, [(D,), w_conv.shape])
    w_conv_full = w_conv_all.reshape(4, 2, L, CONV_K, cs)[:, 0].transpose(1, 2, 0, 3).reshape(L, CONV_K, 4 * cs)
    b_shard = lax.dynamic_slice_in_dim(b_ada, chip * ns, ns, axis=1).reshape(L, 1, ns)
    mod_shard = _ada_mod(c_all, w_ada, b_shard, "ada_mod")
    mod_all = _allgather8(mod_shard.reshape(L * 8, ns), "gather_mod").reshape(4, 2, L, 8, ns)
    mod = lax.dynamic_index_in_dim(mod_all[:, 0], dev, axis=2, keepdims=False)
    mod = mod.transpose(1, 0, 2).reshape(L, NMOD, D)

    shards = [weights[k].astype(BF16) for k in BIG]
    shards = [s.reshape(s.shape[0], 2, s.shape[1] // 2, s.shape[2]) for s in shards]
    col_stacked = lambda t: t.reshape(4, 2 * t.shape[2], t.shape[3])
    row_joined = lambda t: t.reshape(8 * t.shape[2], t.shape[3])

    def layer_weights(l, gathered):
        gd = dict(zip(BIG, gathered))
        return dict(w_in=_reorder_w_in(col_stacked(gd["w_in"])[None], f"w_in_cols_l{l}")[0],
                    w_branch_a=col_stacked(gd["w_branch_a"]), w_branch_b=col_stacked(gd["w_branch_b"]),
                    w_ffn_in=col_stacked(gd["w_ffn_in"]), w_out=row_joined(gd["w_out"]),
                    w_ffn_out=row_joined(gd["w_ffn_out"]))

    P = dict(norm1_g=norm1_g, norm2_g=norm2_g, w_conv=w_conv_full, a_log=a_log, dt_bias=dt_bias,
             gdn_norm_g=gdn_norm_g)
    big = _band_bias(rel_table, "band_bias")

    W = [layer_weights(0, _gather_weights(shards, 0, "gather_weights_l0"))]
    xc = x[0]
    saved = []
    for l in range(L):
        xc, sv, gathered = _layer_fwd(l, xc, mod[l], W[l], P, big[l], (shards, l + 1) if l + 1 < L else None)
        saved.append(sv)
        if l + 1 < L:
            W.append(layer_weights(l + 1, gathered))
    dx, loss_dev, dfinal = _loss_head(xc, final_g.reshape(1, D), loss_target[0], "loss_head")

    where = jnp.stack([ci, chip]).astype(jnp.int32)
    grads = [None] * L
    dmods = [None] * L
    shard_grads = {k: [None] * L for k in BIG}

    def finish_reduce_scatter(l, sums, from_chips):
        halves = [_sum_own_and_received(s_[1], r_, where, f"rs_sum_{k}_l{l}")
                  for k, s_, r_ in zip(BIG, sums, from_chips)]
        for k, t in zip(BIG, _rs_pair(halves, f"rs_pair_l{l}")):
            shard_grads[k][l] = t.reshape(2 * t.shape[1], t.shape[2])

    pending = None
    for l in reversed(range(L)):
        exchange = [s_[0] for s_ in pending] if pending is not None else ()
        dx, grads[l], pay, dmods[l], received = _layer_bwd(l, dx, saved[l], mod[l], W[l], P, big[l], exchange)
        if pending is not None:
            finish_reduce_scatter(l + 1, pending, received)
        gs = [grads[l][k] for k in BIG]
        from_sibling = _rs_sibling([pay[k] for k in BIG], f"rs_sibling_l{l}")
        pending = [_pair_sums(g_, where, r_, f"rs_pair_sum_{k}_l{l}") for k, g_, r_ in zip(BIG, gs, from_sibling)]
    finish_reduce_scatter(0, pending, _rs_chips([s_[0] for s_ in pending], "rs_chips_l0"))
    dmod = jnp.stack(dmods)

    small = {k: jnp.stack([grads[l][k] for l in range(L)]) for k in SMALL}
    parts = [dmod] + [small[k] for k in SMALL] + [dfinal, loss_dev[0, :1]]
    small_blk = _pack_rows(parts)
    srows = small_blk.shape[0]
    small_all, small_sum = _allgather8(small_blk, "gather_small", reduce_rows=srows)
    shapes = [dmod.shape] + [small[k].shape for k in SMALL] + [(D,), (1,)]
    tot = _unpack_rows(small_sum, shapes)
    G = dict(zip(SMALL, tot[1:1 + len(SMALL)]))
    G["b_ada"] = tot[0].reshape(b_ada.shape)
    G["w_conv"] = lax.dynamic_slice_in_dim(G["w_conv"], chip * cs, cs, axis=2)
    G["final_g"] = tot[-2]
    loss = tot[-1][0]
    dmod_all = _unpack_rows(small_all.reshape(8, srows, SMALL_PACK_C), [dmod.shape])[0]
    dmod_cols = lax.dynamic_slice_in_dim(dmod_all, chip * ns, ns, axis=2).transpose(1, 0, 2)
    G["w_ada"] = _ada_wgrad(c_all, dmod_cols, "ada_wgrad")

    order = ["w_ada", "b_ada", "norm1_g", "norm2_g", "w_in", "rel_table", "w_conv", "a_log", "dt_bias", "gdn_norm_g",
             "w_branch_a", "w_branch_b", "w_out", "w_ffn_in", "w_ffn_out", "final_g"]
    deltas, new_m, new_v = {}, {}, {}
    for k in order:
        w = weights[k]
        if k == "w_in":
            to_cols = lambda t: jnp.transpose(t, (2, 0, 1))
            from_cols = lambda t: jnp.transpose(t, (1, 2, 0))
            gt = to_cols(jnp.stack(shard_grads[k]))
            d_, m_, v_ = _adamw_lead(to_cols(w), gt, to_cols(mom_m[k]), to_cols(mom_v[k]), f"adamw_{k}",
                                     W_IN_SHARD // 30)
            G[k], deltas[k], new_m[k], new_v[k] = from_cols(gt), from_cols(d_), from_cols(m_), from_cols(v_)
            continue
        if k in BIG:
            G[k], deltas[k], new_m[k], new_v[k] = _adamw_layers(w, shard_grads[k], mom_m[k], mom_v[k], f"adamw_{k}")
            continue
        as2d = (lambda t: t.reshape(1, -1)) if w.ndim == 1 else (lambda t: t)
        d_, m_, v_ = _adamw(as2d(w), as2d(G[k]), as2d(mom_m[k]), as2d(mom_v[k]), f"adamw_{k}")
        deltas[k], new_m[k], new_v[k] = d_.reshape(w.shape), m_.reshape(w.shape), v_.reshape(w.shape)
    return (loss, dx[None], *[G[k] for k in order], *[deltas[k] for k in order], *[new_m[k] for k in order],
            *[new_v[k] for k in order])
```

```python
import functools

import numpy as np
import jax
import jax.numpy as jnp
from jax import lax
from jax.experimental import pallas as pl
from jax.experimental.pallas import tpu as pltpu

F32 = jnp.float32
BF16 = jnp.bfloat16
HI = lax.Precision.HIGHEST
SOLVE_PREC = lax.Precision.HIGH
MESH = pl.DeviceIdType.MESH

EPS = 1e-6
CHUNK = 64
A_HEADS = 8
A_DH = 64
A_PAST = 8
A_MAX_REL = 128
B_HEADS = 4
B_DH = 128
CONV_K = 4
LANE = 128
QBLK = 4 * CHUNK
KSPAN = QBLK + A_PAST * CHUNK
NEG = -1e30

ADAM_LR = 0.001
ADAM_B1 = 0.9
ADAM_B2 = 0.999
ADAM_EPS = 1e-08
ADAM_WD = 0.01
ADAM_STEP = 10

P_QKVA, P_QKVB, P_GA, P_GB, P_Z, P_BA, P_END = 0, 1536, 3072, 4096, 5120, 5632, 5760
W_IN_SHARD = 1410


def _sigmoid(x):
    return 1.0 / (1.0 + jnp.exp(-x))


def _dot(a, b, ca, cb, prec):
    lead = a.ndim - 2
    batch = ((0,), (0,)) if lead else ((), ())
    return lax.dot_general(a, b, (((ca + lead,), (cb + lead,)), batch), precision=prec, preferred_element_type=F32)


def _nn(a, b, prec=None):
    return _dot(a, b, 1, 0, prec)


def _nt(a, b, prec=None):
    return _dot(a, b, 1, 1, prec)


def _tn(a, b, prec=None):
    return _dot(a, b, 0, 0, prec)


def _bnn(a, b):
    return _nn(a.astype(BF16), b.astype(BF16))


def _bnt(a, b):
    return _nt(a.astype(BF16), b.astype(BF16))


def _btn(a, b):
    return _tn(a.astype(BF16), b.astype(BF16))


def _pick(n, target, unit=LANE):
    best = None
    for t in range(unit, min(n, target) + 1, unit):
        if n % t == 0:
            best = t
    return best if best is not None else n


def _acc(ref, val, i):
    @pl.when(i == 0)
    def _():
        ref[...] = val

    @pl.when(i != 0)
    def _():
        ref[...] += val


def _arb(n):
    return pltpu.CompilerParams(dimension_semantics=("arbitrary",) * n)


def _par(n):
    return pltpu.CompilerParams(dimension_semantics=("parallel",) * n)


def _matmul(a, b, mode, out_dtype, name, tm=1024, tn=1024, tk=1024, layer=None, stacked=False, out_stacked=False,
            also_bf16=False, gather=None):
    bs = b.shape[1:] if layer is not None else b.shape
    if mode == "nn":
        M, K = a.shape
        N = 4 * bs[2] if stacked else bs[1]
        if stacked:
            tn = bs[2]
    elif mode == "nt":
        M, K = a.shape
        N = bs[1] if stacked else bs[0]
        if stacked:
            tk = bs[2]
    else:
        K, M = a.shape
        N = bs[1]
        if out_stacked:
            tn = N // 4
    tm, tn, tk = _pick(M, tm), _pick(N, tn), _pick(K, tk)
    nk = K // tk
    lead = () if layer is None else (layer,)
    lead_blk = () if layer is None else (None,)
    if mode == "nn":
        a_spec = pl.BlockSpec((tm, tk), lambda i, j, k: (i, k))
        if stacked:
            b_spec = pl.BlockSpec(lead_blk + (None, tk, tn), lambda i, j, k: lead + (j, k, 0))
        else:
            b_spec = pl.BlockSpec(lead_blk + (tk, tn), lambda i, j, k: lead + (k, j))
        dot = _nn
    elif mode == "nt":
        a_spec = pl.BlockSpec((tm, tk), lambda i, j, k: (i, k))
        if stacked:
            b_spec = pl.BlockSpec(lead_blk + (None, tn, tk), lambda i, j, k: lead + (k, j, 0))
        else:
            b_spec = pl.BlockSpec(lead_blk + (tn, tk), lambda i, j, k: lead + (j, k))
        dot = _nt
    else:
        a_spec = pl.BlockSpec((tk, tm), lambda i, j, k: (k, i))
        b_spec = pl.BlockSpec((tk, tn), lambda i, j, k: (k, j))
        dot = _tn
    if out_stacked:
        o_spec = pl.BlockSpec((None, tm, tn), lambda i, j, k: (j, i, 0))
        o_shape = jax.ShapeDtypeStruct((4, M, tn), out_dtype)
    else:
        o_spec = pl.BlockSpec((tm, tn), lambda i, j, k: (i, j))
        o_shape = jax.ShapeDtypeStruct((M, N), out_dtype)

    shards, glayer = gather if gather is not None else ((), None)
    ng = len(shards)
    o_shapes = [o_shape] + ([jax.ShapeDtypeStruct(o_shape.shape, BF16)] if also_bf16 else [])
    no = len(o_shapes)
    gi, gj = M // tm, N // tn

    def write(o_refs, val):
        for o_ref in o_refs:
            o_ref[...] = val.astype(o_ref.dtype)

    def body(a_ref, b_ref, *refs):
        srcs, o_refs, gouts, scratch = refs[:ng], refs[ng:ng + no], refs[ng + no:2 * ng + no], refs[2 * ng + no:]
        i, j, k = pl.program_id(0), pl.program_id(1), pl.program_id(2)
        if ng:
            start = (j == 0) & (k == 0)
            done = _carry(_gather_phases(glayer, srcs, gouts, *scratch[-3:]), (i == 0) & start, (i == gi // 2) & start,
                          (i == gi - 1) & (j == gj - 1) & (k == nk - 1))
        if nk == 1:
            write(o_refs, dot(a_ref[...], b_ref[...]))
        else:
            acc_ref = scratch[0]

            @pl.when(k == 0)
            def _():
                acc_ref[...] = jnp.zeros_like(acc_ref)

            acc_ref[...] += dot(a_ref[...], b_ref[...])

            @pl.when(k == nk - 1)
            def _():
                write(o_refs, acc_ref[...])
        if ng:
            done()

    sem = ("arbitrary",) * 3 if ng else ("parallel", "parallel", "arbitrary")
    res = pl.pallas_call(
        body, name=name, grid=(gi, gj, nk), in_specs=[a_spec, b_spec] + [HBM] * ng,
        out_specs=[o_spec] * no + [HBM] * ng, out_shape=o_shapes + _gather_out_shapes(shards),
        scratch_shapes=([] if nk == 1 else [pltpu.VMEM((tm, tn), F32)]) + (_gather_scratch(ng) if ng else []),
        compiler_params=pltpu.CompilerParams(dimension_semantics=sem),
    )(a, b, *shards)
    out = tuple(res[:no]) if also_bf16 else res[0]
    return (out, list(res[no:])) if ng else out


def _rows(tm, n, col=0):
    return pl.BlockSpec((tm, n), lambda i: (i, col))


def _vec(n):
    return pl.BlockSpec((1, n), lambda i: (0, 0))


def _lnmod_fwd(x, g, sc, sh, name):
    S, D = x.shape
    tm = _pick(S, 512, 8)

    def body(x_ref, g_ref, sc_ref, sh_ref, o_ref):
        xv = x_ref[...]
        r = lax.rsqrt(jnp.mean(xv * xv, axis=-1, keepdims=True) + EPS)
        o_ref[...] = ((xv * r * g_ref[...]) * (1.0 + sc_ref[...]) + sh_ref[...]).astype(BF16)

    return pl.pallas_call(
        body, name=name, grid=(S // tm,),
        in_specs=[_rows(tm, D), _vec(D), _vec(D), _vec(D)], out_specs=_rows(tm, D),
        out_shape=jax.ShapeDtypeStruct((S, D), BF16), compiler_params=_par(1),
    )(x, g, sc, sh)


def _lnmod_bwd(dh, x, g, sc, dres, name):
    S, D = x.shape
    tm = _pick(S, 512, 8)

    def body(dh_ref, x_ref, g_ref, sc_ref, dres_ref, dx_ref, dsh_ref, dsc_ref, dg_ref):
        i = pl.program_id(0)
        xv = x_ref[...]
        dh_ = dh_ref[...]
        r = lax.rsqrt(jnp.mean(xv * xv, axis=-1, keepdims=True) + EPS)
        xhat = xv * r
        gv = g_ref[...]
        dn = dh_ * (1.0 + sc_ref[...])
        dxhat = dn * gv
        dx_ref[...] = dres_ref[...] + r * (dxhat - xhat * jnp.mean(dxhat * xhat, axis=-1, keepdims=True))
        _acc(dsh_ref, jnp.sum(dh_, axis=0, keepdims=True), i)
        _acc(dsc_ref, jnp.sum(dh_ * (xhat * gv), axis=0, keepdims=True), i)
        _acc(dg_ref, jnp.sum(dn * xhat, axis=0, keepdims=True), i)

    return pl.pallas_call(
        body, name=name, grid=(S // tm,),
        in_specs=[_rows(tm, D), _rows(tm, D), _vec(D), _vec(D), _rows(tm, D)],
        out_specs=[_rows(tm, D), _vec(D), _vec(D), _vec(D)],
        out_shape=[jax.ShapeDtypeStruct((S, D), F32)] + [jax.ShapeDtypeStruct((1, D), F32)] * 3,
        compiler_params=_arb(1),
    )(dh, x, g, sc, dres)


def _gate_fwd(x, y, gt, name):
    S, D = x.shape
    tm = _pick(S, 512, 8)

    def body(x_ref, y_ref, gt_ref, o_ref):
        o_ref[...] = x_ref[...] + gt_ref[...] * y_ref[...]

    return pl.pallas_call(
        body, name=name, grid=(S // tm,), in_specs=[_rows(tm, D), _rows(tm, D), _vec(D)], out_specs=_rows(tm, D),
        out_shape=jax.ShapeDtypeStruct((S, D), F32), compiler_params=_par(1),
    )(x, y, gt)


def _gate_bwd(dx, y, gt, name):
    S, D = dx.shape
    tm = _pick(S, 512, 8)

    def body(dx_ref, y_ref, gt_ref, dz_ref, dgt_ref):
        i = pl.program_id(0)
        d = dx_ref[...]
        dz_ref[...] = (d * gt_ref[...]).astype(BF16)
        _acc(dgt_ref, jnp.sum(d * y_ref[...], axis=0, keepdims=True), i)

    return pl.pallas_call(
        body, name=name, grid=(S // tm,), in_specs=[_rows(tm, D), _rows(tm, D), _vec(D)],
        out_specs=[_rows(tm, D), _vec(D)],
        out_shape=[jax.ShapeDtypeStruct((S, D), BF16), jax.ShapeDtypeStruct((1, D), F32)],
        compiler_params=_arb(1),
    )(dx, y, gt)


def _ffn_act_fwd(gu, name):
    S, H2 = gu.shape
    H = H2 // 2
    tm = _pick(S, 256, 8)

    def body(g_ref, u_ref, o_ref):
        gv = g_ref[...].astype(F32)
        o_ref[...] = (gv * _sigmoid(gv) * u_ref[...].astype(F32)).astype(BF16)

    return pl.pallas_call(
        body, name=name, grid=(S // tm,), in_specs=[_rows(tm, H, 0), _rows(tm, H, 1)], out_specs=_rows(tm, H),
        out_shape=jax.ShapeDtypeStruct((S, H), BF16), compiler_params=_par(1),
    )(gu, gu)


def _ffn_act_bwd(gu, dact, name):
    S, H2 = gu.shape
    H = H2 // 2
    tm = _pick(S, 256, 8)

    def body(g_ref, u_ref, da_ref, o_ref):
        gv = g_ref[...].astype(F32)
        s = _sigmoid(gv)
        da = da_ref[...].astype(F32)
        o_ref[:, :H] = (da * u_ref[...].astype(F32) * (s * (1.0 + gv * (1.0 - s)))).astype(BF16)
        o_ref[:, H:] = (da * (gv * s)).astype(BF16)

    return pl.pallas_call(
        body, name=name, grid=(S // tm,), in_specs=[_rows(tm, H, 0), _rows(tm, H, 1), _rows(tm, H)],
        out_specs=_rows(tm, H2), out_shape=jax.ShapeDtypeStruct((S, H2), BF16), compiler_params=_par(1),
    )(gu, gu, dact)


def _merge_fwd(proj, pa, pb, name):
    S, D = pa.shape
    tm = _pick(S, 512, 8)

    def body(ga_ref, gb_ref, pa_ref, pb_ref, o_ref):
        o_ref[...] = (_sigmoid(ga_ref[...]) * pa_ref[...].astype(F32)
                      + _sigmoid(gb_ref[...]) * pb_ref[...].astype(F32)).astype(BF16)

    return pl.pallas_call(
        body, name=name, grid=(S // tm,),
        in_specs=[_rows(tm, D, P_GA // D), _rows(tm, D, P_GB // D), _rows(tm, D), _rows(tm, D)],
        out_specs=_rows(tm, D), out_shape=jax.ShapeDtypeStruct((S, D), BF16), compiler_params=_par(1),
    )(proj, proj, pa, pb)


def _merge_bwd(proj, pa, pb, dm, name):
    S, D = pa.shape
    tm = _pick(S, 512, 8)

    def body(ga_ref, gb_ref, pa_ref, pb_ref, dm_ref, dg_ref, dpa_ref, dpb_ref):
        d = dm_ref[...].astype(F32)
        sa = _sigmoid(ga_ref[...])
        sb = _sigmoid(gb_ref[...])
        dg_ref[:, :D] = (d * pa_ref[...].astype(F32) * sa * (1.0 - sa)).astype(BF16)
        dg_ref[:, D:] = (d * pb_ref[...].astype(F32) * sb * (1.0 - sb)).astype(BF16)
        dpa_ref[...] = (d * sa).astype(BF16)
        dpb_ref[...] = (d * sb).astype(BF16)

    return pl.pallas_call(
        body, name=name, grid=(S // tm,),
        in_specs=[_rows(tm, D, P_GA // D), _rows(tm, D, P_GB // D), _rows(tm, D), _rows(tm, D), _rows(tm, D)],
        out_specs=[_rows(tm, 2 * D), _rows(tm, D), _rows(tm, D)],
        out_shape=[jax.ShapeDtypeStruct((S, 2 * D), BF16), jax.ShapeDtypeStruct((S, D), BF16),
                   jax.ShapeDtypeStruct((S, D), BF16)],
        compiler_params=_par(1),
    )(proj, proj, pa, pb, dm)


def _loss_head(x, g, target, name):
    S, D = x.shape
    tm = _pick(S, 512, 8)

    def body(x_ref, g_ref, t_ref, dx_ref, loss_ref, dg_ref):
        i = pl.program_id(0)
        xv = x_ref[...]
        gv = g_ref[...]
        r = lax.rsqrt(jnp.mean(xv * xv, axis=-1, keepdims=True) + EPS)
        xhat = xv * r
        err = xhat * gv - t_ref[...]
        part = 0.5 * jnp.sum(jnp.mean(err * err, axis=-1, keepdims=True), axis=0, keepdims=True)
        _acc(loss_ref, jnp.broadcast_to(part, (1, LANE)), i)
        dy = err * (1.0 / D)
        _acc(dg_ref, jnp.sum(dy * xhat, axis=0, keepdims=True), i)
        dxhat = dy * gv
        dx_ref[...] = r * (dxhat - xhat * jnp.mean(dxhat * xhat, axis=-1, keepdims=True))

    return pl.pallas_call(
        body, name=name, grid=(S // tm,), in_specs=[_rows(tm, D), _vec(D), _rows(tm, D)],
        out_specs=[_rows(tm, D), _vec(LANE), _vec(D)],
        out_shape=[jax.ShapeDtypeStruct((S, D), F32), jax.ShapeDtypeStruct((1, LANE), F32),
                   jax.ShapeDtypeStruct((1, D), F32)],
        compiler_params=_arb(1),
    )(x, g, target)


HEADS_PER_SLAB = LANE // A_DH
N_SLABS = A_HEADS // HEADS_PER_SLAB
SPAN_BLOCKS = KSPAN // QBLK


def _attn_specs(seg):
    q_spec = pl.BlockSpec((QBLK, LANE), lambda p, m: (m, seg[0] * N_SLABS + p))
    k_specs = [pl.BlockSpec((QBLK, LANE), functools.partial(
        lambda j, p, m: (jnp.maximum(m - (SPAN_BLOCKS - 1) + j, 0), seg[1] * N_SLABS + p), j)) for j in range(SPAN_BLOCKS)]
    v_specs = [pl.BlockSpec((QBLK, LANE), functools.partial(
        lambda j, p, m: (jnp.maximum(m - (SPAN_BLOCKS - 1) + j, 0), seg[2] * N_SLABS + p), j)) for j in range(SPAN_BLOCKS)]
    b_spec = pl.BlockSpec((HEADS_PER_SLAB, QBLK, KSPAN), lambda p, m: (p, 0, 0))
    return q_spec, k_specs, v_specs, b_spec


def _head_lanes(t, hh):
    lane = lax.broadcasted_iota(jnp.int32, t.shape, 1)
    return jnp.where((lane // A_DH) == hh, t, jnp.zeros_like(t))


def _front_mask(m):
    col = lax.broadcasted_iota(jnp.int32, (QBLK, KSPAN), 1)
    return jnp.where(col < (SPAN_BLOCKS - 1 - m) * QBLK, NEG, 0.0)


def _attn_probs(qk, bias, front):
    s = qk * (A_DH ** -0.5) + (bias + front)
    p = jnp.exp(s - jnp.max(s, axis=-1, keepdims=True))
    return p * (1.0 / jnp.sum(p, axis=-1, keepdims=True))


def _grid_ends(nq):
    p, m = pl.program_id(0), pl.program_id(1)
    return (p == 0) & (m == 0), (p == N_SLABS // 2) & (m == 0), (p == N_SLABS - 1) & (m == nq - 1)


def _attn_fwd(proj, big, name, gather=None):
    S = proj.shape[0]
    q_spec, k_specs, v_specs, b_spec = _attn_specs((0, 1, 2))
    shards, layer = gather if gather is not None else ((), None)
    ng = len(shards)

    def body(q_ref, k0, k1, k2, v0, v1, v2, b_ref, *rest):
        srcs, o_ref, gouts, sems = rest[:ng], rest[ng], rest[ng + 1:2 * ng + 1], rest[2 * ng + 1:]
        done = _carry(_gather_phases(layer, srcs, gouts, *sems), *_grid_ends(S // QBLK)) if ng else None
        m = pl.program_id(1)
        q = q_ref[...].astype(BF16)
        k = jnp.concatenate([k0[...], k1[...], k2[...]], axis=0).astype(BF16)
        v = jnp.concatenate([v0[...], v1[...], v2[...]], axis=0).astype(BF16)
        front = _front_mask(m)
        heads = range(HEADS_PER_SLAB)
        scores = [_nt(_head_lanes(q, hh), k) for hh in heads]
        probs = [_attn_probs(scores[hh], b_ref[hh], front).astype(BF16) for hh in heads]
        outs = [_nn(probs[hh], v) for hh in heads]
        lane = lax.broadcasted_iota(jnp.int32, (QBLK, LANE), 1)
        o_ref[...] = jnp.where(lane < A_DH, outs[0], outs[1]).astype(BF16)
        if ng:
            done()

    res = pl.pallas_call(
        body, name=name, grid=(N_SLABS, S // QBLK), in_specs=[q_spec] + k_specs + v_specs + [b_spec] + [HBM] * ng,
        out_specs=[pl.BlockSpec((QBLK, LANE), lambda p, m: (m, p))] + [HBM] * ng,
        out_shape=[jax.ShapeDtypeStruct((S, A_HEADS * A_DH), BF16)] + _gather_out_shapes(shards),
        scratch_shapes=_gather_scratch(ng) if ng else [], compiler_params=_arb(2),
    )(proj, proj, proj, proj, proj, proj, proj, big, *shards)
    return res[0], list(res[1:])


def _attn_bwd(proj, big, dya, name, exchange=()):
    S = proj.shape[0]
    W = A_HEADS * A_DH
    q_spec, k_specs, v_specs, b_spec = _attn_specs((0, 1, 2))
    out_q = pl.BlockSpec((QBLK, LANE), lambda p, m: (m, p))
    out_kv = pl.BlockSpec((S, LANE), lambda p, m: (0, p))
    ne = len(exchange)

    def body(q_ref, k0, k1, k2, v0, v1, v2, b_ref, do_ref, *rest):
        srcs, (dq_ref, dk_ref, dv_ref, db_ref), eouts, sems = rest[:ne], rest[ne:ne + 4], rest[ne + 4:2 * ne + 4], rest[2 * ne + 4:]
        done = _carry(_chips_phases(srcs, eouts, *sems), *_grid_ends(S // QBLK)) if ne else None
        m = pl.program_id(1)

        @pl.when(m == 0)
        def _():
            dk_ref[...] = jnp.zeros_like(dk_ref)
            dv_ref[...] = jnp.zeros_like(dv_ref)
            db_ref[...] = jnp.zeros_like(db_ref)

        q = q_ref[...].astype(BF16)
        k = jnp.concatenate([k0[...], k1[...], k2[...]], axis=0).astype(BF16)
        v = jnp.concatenate([v0[...], v1[...], v2[...]], axis=0).astype(BF16)
        do = do_ref[...]
        front = _front_mask(m)
        heads = range(HEADS_PER_SLAB)
        qh = [_head_lanes(q, hh) for hh in heads]
        doh = [_head_lanes(do, hh) for hh in heads]
        scores = [_nt(qh[hh], k) for hh in heads]
        dps = [_nt(doh[hh], v) for hh in heads]
        ps = [_attn_probs(scores[hh], b_ref[hh], front) for hh in heads]
        dss = [ps[hh] * (dps[hh] - jnp.sum(ps[hh] * dps[hh], axis=-1, keepdims=True)) for hh in heads]
        for hh in heads:
            db_ref[hh] += dss[hh]
        dsb = [(dss[hh] * (A_DH ** -0.5)).astype(BF16) for hh in heads]
        dqs = [_nn(dsb[hh], k) for hh in heads]
        dk = sum(_tn(dsb[hh], qh[hh]) for hh in heads)
        dv = sum(_tn(ps[hh].astype(BF16), doh[hh]) for hh in heads)
        lane = lax.broadcasted_iota(jnp.int32, (QBLK, LANE), 1)
        dq_ref[...] = jnp.where(lane < A_DH, dqs[0], dqs[1])
        for j in range(SPAN_BLOCKS):
            blk = m - (SPAN_BLOCKS - 1) + j

            @pl.when(blk >= 0)
            def _():
                off = pl.multiple_of(blk * QBLK, QBLK)
                dk_ref[pl.ds(off, QBLK), :] += dk[j * QBLK:(j + 1) * QBLK]
                dv_ref[pl.ds(off, QBLK), :] += dv[j * QBLK:(j + 1) * QBLK]
        if ne:
            done()

    res = pl.pallas_call(
        body, name=name, grid=(N_SLABS, S // QBLK),
        in_specs=[q_spec] + k_specs + v_specs + [b_spec, pl.BlockSpec((QBLK, LANE), lambda p, m: (m, p))] + [HBM] * ne,
        out_specs=[out_q, out_kv, out_kv, b_spec] + [HBM] * ne,
        out_shape=[jax.ShapeDtypeStruct((S, W), F32)] * 3 + [jax.ShapeDtypeStruct((A_HEADS, QBLK, KSPAN), F32)]
        + _chips_out_shapes(exchange),
        scratch_shapes=_chips_scratch(ne) if ne else [], compiler_params=_arb(2),
    )(proj, proj, proj, proj, proj, proj, proj, big, dya, *exchange)
    return tuple(res[:4]) + (list(res[4:]),)


NREL_PAD = 3 * LANE
SKEW_W = 1024


def _rel_table_grad(dbig, name):
    H, R, C = dbig.shape

    def body(d_ref, o_ref):
        x = jnp.concatenate([d_ref[...], jnp.zeros((R, SKEW_W - C), F32)], axis=1)
        row = lax.broadcasted_iota(jnp.int32, (R, SKEW_W), 0)
        for b in range(R.bit_length() - 1):
            x = jnp.where(((row >> b) & 1) == 1, pltpu.roll(x, SKEW_W - (1 << b), 1), x)
        e = jnp.sum(x, axis=0, keepdims=True)
        xi = lax.broadcasted_iota(jnp.int32, (SKEW_W, NREL_PAD), 0)
        r = lax.broadcasted_iota(jnp.int32, (SKEW_W, NREL_PAD), 1)
        diag = jnp.where(xi < C, xi, xi - SKEW_W)
        rel = jnp.clip(A_PAST * CHUNK - diag, -A_MAX_REL, A_MAX_REL) + A_MAX_REL
        o_ref[...] = _nn(e, jnp.where(rel == r, 1.0, 0.0).astype(F32), HI)

    return pl.pallas_call(
        body, name=name, grid=(H,), in_specs=[pl.BlockSpec((None, R, C), lambda h: (h, 0, 0))],
        out_specs=pl.BlockSpec((None, 1, NREL_PAD), lambda h: (h, 0, 0)),
        out_shape=jax.ShapeDtypeStruct((H, 1, NREL_PAD), F32), compiler_params=_par(1),
    )(dbig)


def _chunk_cumsum_matrix(n, reverse):
    j = lax.broadcasted_iota(jnp.int32, (n, n), 0)
    i = lax.broadcasted_iota(jnp.int32, (n, n), 1)
    same = (j // CHUNK) == (i // CHUNK)
    return jnp.where(same & ((j >= i) if reverse else (j <= i)), 1.0, 0.0).astype(F32)


def _gdn_gates_fwd(b_t, a_t, alog, dtb, name):
    Hh, S = b_t.shape
    tl = _pick(S, 512)
    row = pl.BlockSpec((Hh, tl), lambda i: (0, i))
    col = pl.BlockSpec((Hh, 1), lambda i: (0, 0))

    def body(b_ref, a_ref, al_ref, dt_ref, beta_ref, gam_ref):
        z = a_ref[...] + dt_ref[...]
        sp = jnp.maximum(z, 0.0) + jnp.log(1.0 + jnp.exp(-jnp.abs(z)))
        g = -jnp.exp(al_ref[...]) * sp
        beta_ref[...] = _sigmoid(b_ref[...])
        gam_ref[...] = _nn(g, _chunk_cumsum_matrix(tl, False), HI)

    return pl.pallas_call(
        body, name=name, grid=(S // tl,), in_specs=[row, row, col, col], out_specs=[row, row],
        out_shape=[jax.ShapeDtypeStruct((Hh, S), F32)] * 2, compiler_params=_par(1),
    )(b_t, a_t, alog, dtb)


def _gdn_gates_bwd(dbeta, dgam_a, dgam_b, b_t, a_t, alog, dtb, name):
    Hh, S = b_t.shape
    tl = _pick(S, 512)
    row = pl.BlockSpec((Hh, tl), lambda i: (0, i))
    col = pl.BlockSpec((Hh, 1), lambda i: (0, 0))
    accs = pl.BlockSpec((Hh, LANE), lambda i: (0, 0))

    def body(dbeta_ref, dga_ref, dgb_ref, b_ref, a_ref, al_ref, dt_ref, db_ref, da_ref, dal_ref, ddt_ref):
        i = pl.program_id(0)
        z = a_ref[...] + dt_ref[...]
        sp = jnp.maximum(z, 0.0) + jnp.log(1.0 + jnp.exp(-jnp.abs(z)))
        ea = jnp.exp(al_ref[...])
        dg = _nn(dga_ref[...] + dgb_ref[...], _chunk_cumsum_matrix(tl, True), HI)
        da = dg * (-ea) * _sigmoid(z)
        beta = _sigmoid(b_ref[...])
        db_ref[...] = dbeta_ref[...] * beta * (1.0 - beta)
        da_ref[...] = da
        _acc(dal_ref, jnp.broadcast_to(jnp.sum(dg * (-ea * sp), axis=1, keepdims=True), (Hh, LANE)), i)
        _acc(ddt_ref, jnp.broadcast_to(jnp.sum(da, axis=1, keepdims=True), (Hh, LANE)), i)

    return pl.pallas_call(
        body, name=name, grid=(S // tl,), in_specs=[row] * 5 + [col, col], out_specs=[row, row, accs, accs],
        out_shape=[jax.ShapeDtypeStruct((Hh, S), F32)] * 2 + [jax.ShapeDtypeStruct((Hh, LANE), F32)] * 2,
        compiler_params=_arb(1),
    )(dbeta, dgam_a, dgam_b, b_t, a_t, alog, dtb)


HALO = 8


def _conv_silu(xx_ref, w_ref, tm):
    y = w_ref[0:1, :] * xx_ref[pl.ds(HALO - CONV_K + 1, tm), :]
    for j in range(1, CONV_K):
        y = y + w_ref[j:j + 1, :] * xx_ref[pl.ds(HALO - CONV_K + 1 + j, tm), :]
    return y, y * _sigmoid(y)


def _fill_prev_halo(xx_ref, x_ref, prev_ref, i, tm):
    xx_ref[pl.ds(HALO, tm), :] = x_ref[...]

    @pl.when(i == 0)
    def _():
        xx_ref[pl.ds(0, HALO), :] = jnp.zeros((HALO, xx_ref.shape[1]), F32)

    @pl.when(i != 0)
    def _():
        xx_ref[pl.ds(0, HALO), :] = prev_ref[...]


def _gdn_pre_specs(tm, C, colblk):
    cur = pl.BlockSpec((tm, C), lambda i: (i, colblk))
    prev = pl.BlockSpec((HALO, C), lambda i: (jnp.maximum(i * (tm // HALO) - 1, 0), colblk))
    return cur, prev


def _gdn_pre_fwd(proj, wconv, name):
    S = proj.shape[0]
    C = 3 * B_HEADS * B_DH
    W = B_HEADS * B_DH
    tm = _pick(S, 256, 8)
    cur, prev = _gdn_pre_specs(tm, C, P_QKVB // C)

    def body(x_ref, prev_ref, w_ref, q_ref, k_ref, v_ref, xx_ref):
        i = pl.program_id(0)
        _fill_prev_halo(xx_ref, x_ref, prev_ref, i, tm)
        _, sl = _conv_silu(xx_ref, w_ref, tm)
        for h in range(B_HEADS):
            hs = slice(h * B_DH, (h + 1) * B_DH)
            q = sl[:, h * B_DH:(h + 1) * B_DH]
            k = sl[:, W + h * B_DH:W + (h + 1) * B_DH]
            q_ref[:, hs] = q * (lax.rsqrt(jnp.sum(q * q, axis=-1, keepdims=True) + EPS) * (B_DH ** -0.5))
            k_ref[:, hs] = k * lax.rsqrt(jnp.sum(k * k, axis=-1, keepdims=True) + EPS)
        v_ref[...] = sl[:, 2 * W:]

    return pl.pallas_call(
        body, name=name, grid=(S // tm,), in_specs=[cur, prev, pl.BlockSpec((CONV_K, C), lambda i: (0, 0))],
        out_specs=[_rows(tm, W)] * 3, out_shape=[jax.ShapeDtypeStruct((S, W), F32)] * 3,
        scratch_shapes=[pltpu.VMEM((HALO + tm, C), F32)], compiler_params=_par(1),
    )(proj, proj, wconv)


def _gdn_pre_bwd_a(proj, wconv, dqn, dkn, dv, name):
    S = proj.shape[0]
    C = 3 * B_HEADS * B_DH
    W = B_HEADS * B_DH
    tm = _pick(S, 256, 8)
    cur, prev = _gdn_pre_specs(tm, C, P_QKVB // C)

    def body(x_ref, prev_ref, w_ref, dq_ref, dk_ref, dv_ref, dy_ref, xx_ref):
        i = pl.program_id(0)
        _fill_prev_halo(xx_ref, x_ref, prev_ref, i, tm)
        y, sl = _conv_silu(xx_ref, w_ref, tm)
        sg = _sigmoid(y)
        dsilu = sg * (1.0 + y * (1.0 - sg))
        for h in range(B_HEADS):
            for base, d_ref, c in ((0, dq_ref, B_DH ** -0.5), (W, dk_ref, 1.0)):
                lo = base + h * B_DH
                t = sl[:, lo:lo + B_DH]
                d = d_ref[:, h * B_DH:(h + 1) * B_DH]
                r = lax.rsqrt(jnp.sum(t * t, axis=-1, keepdims=True) + EPS)
                dt = (c * r) * (d - t * (r * r) * jnp.sum(d * t, axis=-1, keepdims=True))
                dy_ref[:, lo:lo + B_DH] = dt * dsilu[:, lo:lo + B_DH]
        dy_ref[:, 2 * W:] = dv_ref[...] * dsilu[:, 2 * W:]

    return pl.pallas_call(
        body, name=name, grid=(S // tm,),
        in_specs=[cur, prev, pl.BlockSpec((CONV_K, C), lambda i: (0, 0))] + [_rows(tm, W)] * 3,
        out_specs=_rows(tm, C), out_shape=jax.ShapeDtypeStruct((S, C), F32),
        scratch_shapes=[pltpu.VMEM((HALO + tm, C), F32)], compiler_params=_par(1),
    )(proj, proj, wconv, dqn, dkn, dv)


def _gdn_pre_bwd_b(proj, wconv, dy, name):
    S = proj.shape[0]
    C = 3 * B_HEADS * B_DH
    tm = _pick(S, 256, 8)
    nt_ = S // tm
    cur, prev = _gdn_pre_specs(tm, C, P_QKVB // C)
    nxt = pl.BlockSpec((HALO, C), lambda i: (jnp.minimum((i + 1) * (tm // HALO), S // HALO - 1), 0))

    def body(x_ref, prev_ref, w_ref, dy_ref, next_ref, dx_ref, dw_ref, xx_ref, dd_ref):
        i = pl.program_id(0)
        _fill_prev_halo(xx_ref, x_ref, prev_ref, i, tm)
        dyv = dy_ref[...]
        dd_ref[pl.ds(0, tm), :] = dyv

        @pl.when(i == nt_ - 1)
        def _():
            dd_ref[pl.ds(tm, HALO), :] = jnp.zeros((HALO, C), F32)

        @pl.when(i != nt_ - 1)
        def _():
            dd_ref[pl.ds(tm, HALO), :] = next_ref[...]

        dx = w_ref[0:1, :] * dd_ref[pl.ds(CONV_K - 1, tm), :]
        for j in range(1, CONV_K):
            dx = dx + w_ref[j:j + 1, :] * dd_ref[pl.ds(CONV_K - 1 - j, tm), :]
        dx_ref[...] = dx.astype(BF16)
        dw = jnp.concatenate(
            [jnp.sum(dyv * xx_ref[pl.ds(HALO - CONV_K + 1 + j, tm), :], axis=0, keepdims=True) for j in range(CONV_K)],
            axis=0)
        _acc(dw_ref, dw, i)

    return pl.pallas_call(
        body, name=name, grid=(nt_,),
        in_specs=[cur, prev, pl.BlockSpec((CONV_K, C), lambda i: (0, 0)), _rows(tm, C), nxt],
        out_specs=[_rows(tm, C), pl.BlockSpec((CONV_K, C), lambda i: (0, 0))],
        out_shape=[jax.ShapeDtypeStruct((S, C), BF16), jax.ShapeDtypeStruct((CONV_K, C), F32)],
        scratch_shapes=[pltpu.VMEM((HALO + tm, C), F32), pltpu.VMEM((tm + HALO, C), F32)],
        compiler_params=_arb(1),
    )(proj, proj, wconv, dy, dy)


def _chunk_masks():
    row = lax.broadcasted_iota(jnp.int32, (CHUNK, CHUNK), 0)
    col = lax.broadcasted_iota(jnp.int32, (CHUNK, CHUNK), 1)
    return row >= col, row > col


def _chunk_local(q, k, vv, bc, gc, gr, tri):
    dm = jnp.where(tri, jnp.exp(jnp.where(tri, gc - gr, 0.0)), 0.0)
    kk = _bnt(k, k)
    glast = gr[..., CHUNK - 1:CHUNK]
    ep = jnp.exp(gc)
    em = jnp.exp(glast - gc)
    el = jnp.exp(glast)
    return dm, kk, ep, em, el, vv * bc, k * (bc * ep)


def _unit_lower_inverse(low):
    row = lax.broadcasted_iota(jnp.int32, (CHUNK, CHUNK), 0)
    col = lax.broadcasted_iota(jnp.int32, (CHUNK, CHUNK), 1)
    p = -low
    t = jnp.where(row == col, 1.0, 0.0).astype(F32) + p
    steps = CHUNK.bit_length() - 2
    for _ in range(steps):
        p = _nn(p, p, SOLVE_PREC)
        t = t + _nn(t, p, SOLVE_PREC)
    return t


GROUP = 4


def _carry(phases, first, middle, last):
    if len(phases) == 3:
        pl.when(first)(phases[0])
        pl.when(middle)(phases[1])
        return lambda: pl.when(last)(phases[2])
    pl.when(first)(phases[0])
    return lambda: pl.when(last)(phases[1])


def _pairs(nchunks):
    return [(c, h) for c in range(nchunks) for h in range(B_HEADS)]


def _tok(c):
    return slice(c * CHUNK, (c + 1) * CHUNK)


def _head(h):
    return slice(h * B_DH, (h + 1) * B_DH)


def _stack_tokens(ref, nchunks):
    return jnp.stack([ref[_tok(c), _head(h)] for c, h in _pairs(nchunks)])


def _stack_cols(ref, nchunks):
    per_chunk = [ref[c] for c in range(nchunks)] if len(ref.shape) == 3 else [ref[...]]
    return jnp.stack([per_chunk[c][:, h:h + 1] for c, h in _pairs(nchunks)])


def _stack_rows(ref, nchunks):
    if len(ref.shape) == 3:
        return jnp.stack([ref[c, h:h + 1, :] for c, h in _pairs(nchunks)])
    return jnp.stack([ref[h:h + 1, :] for _, h in _pairs(1)])


def _gdn_group_specs(ng_steps, W):
    tok = pl.BlockSpec((GROUP * CHUNK, W), lambda i: (i, 0))
    colv = pl.BlockSpec((GROUP, CHUNK, B_HEADS), lambda i: (i, 0, 0))
    rowv = pl.BlockSpec((GROUP, B_HEADS, CHUNK), lambda i: (i, 0, 0))
    mat = pl.BlockSpec((GROUP, B_HEADS, CHUNK, CHUNK), lambda i: (i, 0, 0, 0))
    return tok, colv, rowv, mat


def _gdn_local_fwd(qn, kn, v, bcol, gcol, grow, name, gather=None):
    S, Wd = qn.shape
    nc = S // CHUNK
    steps = nc // GROUP
    tok, colv, rowv, mat = _gdn_group_specs(steps, Wd)
    shards, layer = gather if gather is not None else ((), None)
    ng = len(shards)

    def body(q_ref, k_ref, v_ref, bc_ref, gc_ref, gr_ref, *rest):
        srcs, (t_ref, a_ref, u_ref, w_ref), gouts, sems = rest[:ng], rest[ng:ng + 4], rest[ng + 4:2 * ng + 4], rest[2 * ng + 4:]
        i = pl.program_id(0)
        done = _carry(_gather_phases(layer, srcs, gouts, *sems), i == 0, i == steps // 2, i == steps - 1) if ng else None
        tri, strict = _chunk_masks()
        q, k, vv = (_stack_tokens(r, GROUP) for r in (q_ref, k_ref, v_ref))
        bc, gc, gr = _stack_cols(bc_ref, GROUP), _stack_cols(gc_ref, GROUP), _stack_rows(gr_ref, GROUP)
        dm, kk, ep, em, el, vb, kb = _chunk_local(q, k, vv, bc, gc, gr, tri)
        t = _unit_lower_inverse(jnp.where(strict, bc * kk * dm, 0.0))
        a = _bnt(q, k) * dm
        u = _nn(t, vb, SOLVE_PREC)
        w = _nn(t, kb, SOLVE_PREC)
        for n, (c, h) in enumerate(_pairs(GROUP)):
            t_ref[c, h] = t[n]
            a_ref[c, h] = a[n]
            u_ref[_tok(c), _head(h)] = u[n]
            w_ref[_tok(c), _head(h)] = w[n]
        if ng:
            done()

    res = pl.pallas_call(
        body, name=name, grid=(steps,), in_specs=[tok, tok, tok, colv, colv, rowv] + [HBM] * ng,
        out_specs=[mat, mat, tok, tok] + [HBM] * ng,
        out_shape=[jax.ShapeDtypeStruct((nc, B_HEADS, CHUNK, CHUNK), F32)] * 2 + [jax.ShapeDtypeStruct((S, Wd), F32)] * 2
        + _gather_out_shapes(shards),
        scratch_shapes=_gather_scratch(ng) if ng else [], compiler_params=_arb(1),
    )(qn, kn, v, bcol, gcol, grow, *shards)
    return res[0], res[1], res[2], res[3], list(res[4:])


def _scan_decays(gc, gr):
    glast = gr[..., CHUNK - 1:CHUNK]
    return jnp.exp(gc), jnp.exp(glast - gc), jnp.exp(glast)


def _gdn_scan_specs(nc, rev):
    idx = (lambda i: nc - 1 - i) if rev else (lambda i: i)
    W = B_HEADS * B_DH
    tok = pl.BlockSpec((CHUNK, W), lambda i: (idx(i), 0))
    colv = pl.BlockSpec((None, CHUNK, B_HEADS), lambda i: (idx(i), 0, 0))
    rowv = pl.BlockSpec((None, B_HEADS, CHUNK), lambda i: (idx(i), 0, 0))
    mat = pl.BlockSpec((None, B_HEADS, CHUNK, CHUNK), lambda i: (idx(i), 0, 0, 0))
    smat = pl.BlockSpec((None, B_HEADS, B_DH, B_DH), lambda i: (idx(i), 0, 0, 0))
    return tok, colv, rowv, mat, smat


def _gdn_scan_fwd(qn, kn, u, w, a, gcol, grow, name, gather=None):
    S, Wd = qn.shape
    nc = S // CHUNK
    tok, colv, rowv, mat, smat = _gdn_scan_specs(nc, False)
    shards, layer = gather if gather is not None else ((), None)
    ng = len(shards)

    def body(q_ref, k_ref, u_ref, w_ref, a_ref, gc_ref, gr_ref, *rest):
        srcs, (o_ref, sh_ref), gouts = rest[:ng], rest[ng:ng + 2], rest[ng + 2:2 * ng + 2]
        st_ref, sems = rest[2 * ng + 2], rest[2 * ng + 3:]
        i = pl.program_id(0)
        done = _carry(_gather_phases(layer, srcs, gouts, *sems), i == 0, i == nc // 2, i == nc - 1) if ng else None

        @pl.when(i == 0)
        def _():
            st_ref[...] = jnp.zeros_like(st_ref)

        ep, em, el = _scan_decays(_stack_cols(gc_ref, 1), _stack_rows(gr_ref, 1))
        q, k, u, w = (_stack_tokens(r, 1) for r in (q_ref, k_ref, u_ref, w_ref))
        s0 = st_ref[...]
        ut = u - _bnn(w, s0)
        o = _bnn(q * ep, s0) + _bnn(a_ref[...], ut)
        st_ref[...] = el * s0 + _btn(k * em, ut)
        sh_ref[...] = s0
        for h in range(B_HEADS):
            o_ref[:, _head(h)] = o[h]
        if ng:
            done()

    res = pl.pallas_call(
        body, name=name, grid=(nc,), in_specs=[tok, tok, tok, tok, mat, colv, rowv] + [HBM] * ng,
        out_specs=[tok, smat] + [HBM] * ng,
        out_shape=[jax.ShapeDtypeStruct((S, Wd), F32), jax.ShapeDtypeStruct((nc, B_HEADS, B_DH, B_DH), F32)]
        + _gather_out_shapes(shards),
        scratch_shapes=[pltpu.VMEM((B_HEADS, B_DH, B_DH), F32)] + (_gather_scratch(ng) if ng else []),
        compiler_params=_arb(1),
    )(qn, kn, u, w, a, gcol, grow, *shards)
    return res[0], res[1], list(res[2:])


def _gdn_scan_bwd(qn, kn, u, w, a, gcol, grow, ssave, do, name):
    S, Wd = qn.shape
    nc = S // CHUNK
    tok, colv, rowv, mat, smat = _gdn_scan_specs(nc, True)

    def body(q_ref, k_ref, u_ref, w_ref, a_ref, gc_ref, gr_ref, sh_ref, do_ref,
             du_ref, dw_ref, dqd_ref, dkd_ref, da_ref, dgl_ref, ds_ref):
        i = pl.program_id(0)

        @pl.when(i == 0)
        def _():
            ds_ref[...] = jnp.zeros_like(ds_ref)

        tri, _ = _chunk_masks()
        sub4 = lax.broadcasted_iota(jnp.int32, (B_HEADS, CHUNK), 0)
        lane_last = lax.broadcasted_iota(jnp.int32, (1, CHUNK), 1) == CHUNK - 1
        ep, em, el = _scan_decays(_stack_cols(gc_ref, 1), _stack_rows(gr_ref, 1))
        q, k, u, w, dout = (_stack_tokens(r, 1) for r in (q_ref, k_ref, u_ref, w_ref, do_ref))
        s0 = sh_ref[...]
        ds = ds_ref[...]
        ut = u - _bnn(w, s0)
        dut = _btn(a_ref[...], dout) + _bnn(k * em, ds)
        ds_ref[...] = el * ds + _btn(q * ep, dout) - _btn(w, dut)
        dw = -_bnt(dut, s0)
        dqd = _bnt(dout, s0)
        dkd = _bnt(ut, ds)
        da_ref[...] = jnp.where(tri, _bnt(dout, ut), 0.0)
        d_el = jnp.sum(jnp.sum(s0 * ds, axis=-1, keepdims=True), axis=-2, keepdims=True)
        last = d_el * el
        dgl_acc = jnp.zeros((B_HEADS, CHUNK), F32)
        for h in range(B_HEADS):
            du_ref[:, _head(h)] = dut[h]
            dw_ref[:, _head(h)] = dw[h]
            dqd_ref[:, _head(h)] = dqd[h]
            dkd_ref[:, _head(h)] = dkd[h]
            dgl_acc = jnp.where(sub4 == h, jnp.where(lane_last, last[h], 0.0), dgl_acc)
        dgl_ref[...] = dgl_acc

    return pl.pallas_call(
        body, name=name, grid=(nc,), in_specs=[tok, tok, tok, tok, mat, colv, rowv, smat, tok],
        out_specs=[tok, tok, tok, tok, mat, rowv],
        out_shape=[jax.ShapeDtypeStruct((S, Wd), F32)] * 4 + [jax.ShapeDtypeStruct((nc, B_HEADS, CHUNK, CHUNK), F32),
                                                             jax.ShapeDtypeStruct((nc, B_HEADS, CHUNK), F32)],
        scratch_shapes=[pltpu.VMEM((B_HEADS, B_DH, B_DH), F32)], compiler_params=_arb(1),
    )(qn, kn, u, w, a, gcol, grow, ssave, do)


def _gdn_local_bwd(qn, kn, v, bcol, gcol, grow, tsave, du, dw, dqd, dkd, da, dgl, name, exchange=()):
    S, Wd = qn.shape
    nc = S // CHUNK
    steps = nc // GROUP
    tok, colv, rowv, mat = _gdn_group_specs(steps, Wd)
    ne = len(exchange)

    def body(q_ref, k_ref, v_ref, bc_ref, gc_ref, gr_ref, t_ref, du_ref, dw_ref, dqd_ref, dkd_ref, da_ref, dgl_ref, *rest):
        srcs, (dq_ref, dk_ref, dv_ref, dbc_ref, dgc_ref, dgr_ref) = rest[:ne], rest[ne:ne + 6]
        eouts, sems = rest[ne + 6:2 * ne + 6], rest[2 * ne + 6:]
        i = pl.program_id(0)
        done = _carry(_chips_phases(srcs, eouts, *sems), i == 0, None, i == steps - 1) if ne else None
        tri, strict = _chunk_masks()
        lane4 = lax.broadcasted_iota(jnp.int32, (CHUNK, B_HEADS), 1)
        sub4 = lax.broadcasted_iota(jnp.int32, (B_HEADS, CHUNK), 0)
        lane_last = lax.broadcasted_iota(jnp.int32, (1, CHUNK), 1) == CHUNK - 1
        q, k, vv, dut, dwv, dqd, dkd = (_stack_tokens(r, GROUP)
                                        for r in (q_ref, k_ref, v_ref, du_ref, dw_ref, dqd_ref, dkd_ref))
        bc, gc, gr = _stack_cols(bc_ref, GROUP), _stack_cols(gc_ref, GROUP), _stack_rows(gr_ref, GROUP)
        dm, kk, ep, em, el, vb, kb = _chunk_local(q, k, vv, bc, gc, gr, tri)
        t = jnp.stack([t_ref[c, h] for c, h in _pairs(GROUP)])
        dav = jnp.stack([da_ref[c, h] for c, h in _pairs(GROUP)])
        qk = _bnt(q, k)
        dt = _nt(dut, vb, SOLVE_PREC) + _nt(dwv, kb, SOLVE_PREC)
        dvb = _tn(t, dut, SOLVE_PREC)
        dkb = _tn(t, dwv, SOLVE_PREC)
        dl = jnp.where(strict, -_tn(t, _nt(dt, t, SOLVE_PREC), SOLVE_PREC), 0.0)
        g1 = dl * dm
        dkb_k = jnp.sum(dkb * k, axis=-1, keepdims=True)
        dbeta = jnp.sum(g1 * kk, axis=-1, keepdims=True) + jnp.sum(dvb * vv, axis=-1, keepdims=True) + dkb_k * ep
        dkk = g1 * bc
        ddm = dl * (bc * kk) + dav * qk
        dqk = dav * dm
        dq = _bnn(dqk, k) + dqd * ep
        dk = _btn(dqk, q) + _bnn(dkk, k) + _btn(dkk, k) + dkb * (bc * ep) + dkd * em
        dv = dvb * bc
        dep = dkb_k * bc + jnp.sum(dqd * q, axis=-1, keepdims=True)
        dem = jnp.sum(dkd * k, axis=-1, keepdims=True)
        mm = ddm * dm
        dgam_c = jnp.sum(mm, axis=-1, keepdims=True) + dep * ep - dem * em
        dglast = jnp.sum(dem * em, axis=-2, keepdims=True)
        dgam_r = -jnp.sum(mm, axis=-2, keepdims=True) + jnp.where(lane_last, dglast, 0.0)
        for c in range(GROUP):
            dbc_acc = jnp.zeros((CHUNK, B_HEADS), F32)
            dgc_acc = jnp.zeros((CHUNK, B_HEADS), F32)
            dgr_acc = jnp.zeros((B_HEADS, CHUNK), F32)
            for h in range(B_HEADS):
                n = c * B_HEADS + h
                dq_ref[_tok(c), _head(h)] = dq[n]
                dk_ref[_tok(c), _head(h)] = dk[n]
                dv_ref[_tok(c), _head(h)] = dv[n]
                dbc_acc = jnp.where(lane4 == h, dbeta[n], dbc_acc)
                dgc_acc = jnp.where(lane4 == h, dgam_c[n], dgc_acc)
                dgr_acc = jnp.where(sub4 == h, dgam_r[n], dgr_acc)
            dbc_ref[c] = dbc_acc
            dgc_ref[c] = dgc_acc
            dgr_ref[c] = dgr_acc + dgl_ref[c]
        if ne:
            done()

    res = pl.pallas_call(
        body, name=name, grid=(steps,),
        in_specs=[tok, tok, tok, colv, colv, rowv, mat, tok, tok, tok, tok, mat, rowv] + [HBM] * ne,
        out_specs=[tok, tok, tok, colv, colv, rowv] + [HBM] * ne,
        out_shape=[jax.ShapeDtypeStruct((S, Wd), F32)] * 3
        + [jax.ShapeDtypeStruct((nc, CHUNK, B_HEADS), F32)] * 2 + [jax.ShapeDtypeStruct((nc, B_HEADS, CHUNK), F32)]
        + _chips_out_shapes(exchange),
        scratch_shapes=_chips_scratch(ne) if ne else [], compiler_params=_arb(1),
    )(qn, kn, v, bcol, gcol, grow, tsave, du, dw, dqd, dkd, da, dgl, *exchange)
    return tuple(res[:6]) + (list(res[6:]),)


def _gdn_post_fwd(o, proj, ng, name):
    S, W = o.shape
    tm = _pick(S, 512, 8)

    def body(o_ref, z_ref, g_ref, y_ref):
        gv = g_ref[...]
        for h in range(B_HEADS):
            hs = slice(h * B_DH, (h + 1) * B_DH)
            oh = o_ref[:, hs]
            z = z_ref[:, hs]
            r = lax.rsqrt(jnp.mean(oh * oh, axis=-1, keepdims=True) + EPS)
            y_ref[:, hs] = (oh * r * gv * (z * _sigmoid(z))).astype(BF16)

    return pl.pallas_call(
        body, name=name, grid=(S // tm,), in_specs=[_rows(tm, W), _rows(tm, W, P_Z // W), _vec(B_DH)],
        out_specs=_rows(tm, W), out_shape=jax.ShapeDtypeStruct((S, W), BF16), compiler_params=_par(1),
    )(o, proj, ng)


def _gdn_post_bwd(dy, o, proj, ng, name):
    S, W = o.shape
    tm = _pick(S, 512, 8)

    def body(dy_ref, o_ref, z_ref, g_ref, do_ref, dz_ref, dg_ref):
        i = pl.program_id(0)
        gv = g_ref[...]
        dg = jnp.zeros((1, B_DH), F32)
        for h in range(B_HEADS):
            hs = slice(h * B_DH, (h + 1) * B_DH)
            oh = o_ref[:, hs]
            z = z_ref[:, hs]
            d = dy_ref[:, hs]
            r = lax.rsqrt(jnp.mean(oh * oh, axis=-1, keepdims=True) + EPS)
            n = oh * r
            sg = _sigmoid(z)
            sz = z * sg
            dn = d * gv * sz
            dg = dg + jnp.sum(d * n * sz, axis=0, keepdims=True)
            dz_ref[:, hs] = (d * n * gv * (sg * (1.0 + z * (1.0 - sg)))).astype(BF16)
            do_ref[:, hs] = r * (dn - n * jnp.mean(dn * n, axis=-1, keepdims=True))
        _acc(dg_ref, dg, i)

    return pl.pallas_call(
        body, name=name, grid=(S // tm,), in_specs=[_rows(tm, W), _rows(tm, W), _rows(tm, W, P_Z // W), _vec(B_DH)],
        out_specs=[_rows(tm, W), _rows(tm, W), _vec(B_DH)],
        out_shape=[jax.ShapeDtypeStruct((S, W), F32), jax.ShapeDtypeStruct((S, W), BF16),
                   jax.ShapeDtypeStruct((1, B_DH), F32)],
        compiler_params=_arb(1),
    )(dy, o, proj, ng)


def _ada_mod(c_all, w_ada, b_shard, name):
    L, D, Ns = w_ada.shape
    B = c_all.shape[0]

    def body(c_ref, w_ref, b_ref, o_ref):
        cv = c_ref[...]
        cond = (cv * _sigmoid(cv)).astype(BF16)
        o_ref[...] = _nn(cond, w_ref[...].astype(BF16)) + b_ref[...]

    return pl.pallas_call(
        body, name=name, grid=(L,),
        in_specs=[pl.BlockSpec((B, D), lambda l: (0, 0)), pl.BlockSpec((None, D, Ns), lambda l: (l, 0, 0)),
                  pl.BlockSpec((None, 1, Ns), lambda l: (l, 0, 0))],
        out_specs=pl.BlockSpec((None, B, Ns), lambda l: (l, 0, 0)),
        out_shape=jax.ShapeDtypeStruct((L, B, Ns), F32), compiler_params=_par(1),
    )(c_all, w_ada, b_shard)


def _ada_wgrad(c_all, dmod, name):
    L, B, Ns = dmod.shape
    D = c_all.shape[1]

    def body(c_ref, d_ref, o_ref):
        cv = c_ref[...]
        cond = (cv * _sigmoid(cv)).astype(BF16)
        o_ref[...] = _tn(cond, d_ref[...].astype(BF16))

    return pl.pallas_call(
        body, name=name, grid=(L,),
        in_specs=[pl.BlockSpec((B, D), lambda l: (0, 0)), pl.BlockSpec((None, B, Ns), lambda l: (l, 0, 0))],
        out_specs=pl.BlockSpec((None, D, Ns), lambda l: (l, 0, 0)),
        out_shape=jax.ShapeDtypeStruct((L, D, Ns), F32), compiler_params=_par(1),
    )(c_all, dmod)


W_IN_PIECES = ((0, 0, 1410), (1, 0, 1410), (2, 0, 252), (2, 772, 638), (3, 0, 1410), (2, 252, 512), (2, 764, 8))


def _reorder_w_in(w4, name):
    L, _, D, Cs = w4.shape
    tm = _pick(D, 256, 16)
    used = sum(p[2] for p in W_IN_PIECES)

    def body(w_ref, o_ref):
        shard = [w_ref[s] for s in range(4)]
        parts = [shard[s][:, lo:lo + n] for s, lo, n in W_IN_PIECES]
        o_ref[...] = jnp.concatenate(parts + [jnp.zeros((tm, P_END - used), w4.dtype)], axis=1)

    return pl.pallas_call(
        body, name=name, grid=(L, D // tm), in_specs=[pl.BlockSpec((None, 4, tm, Cs), lambda l, i: (l, 0, i, 0))],
        out_specs=pl.BlockSpec((None, tm, P_END), lambda l, i: (l, i, 0)),
        out_shape=jax.ShapeDtypeStruct((L, D, P_END), w4.dtype), compiler_params=_par(2),
    )(w4)


def _restore_w_in(g, name):
    D = g.shape[0]
    tm = _pick(D, 256, 16)

    def body(g_ref, o_ref, ob_ref):
        gv = g_ref[...]
        off = 0
        pieces = {}
        for s, lo, n in W_IN_PIECES:
            pieces.setdefault(s, []).append((lo, gv[:, off:off + n]))
            off += n
        for s, lst in pieces.items():
            lst.sort(key=lambda t: t[0])
            shard = lst[0][1] if len(lst) == 1 else jnp.concatenate([t[1] for t in lst], axis=1)
            o_ref[s] = shard
            ob_ref[s] = shard.astype(BF16)

    spec = pl.BlockSpec((4, tm, W_IN_SHARD), lambda i: (0, i, 0))
    return pl.pallas_call(
        body, name=name, grid=(D // tm,), in_specs=[pl.BlockSpec((tm, P_END), lambda i: (i, 0))],
        out_specs=[spec, spec],
        out_shape=[jax.ShapeDtypeStruct((4, D, W_IN_SHARD), g.dtype), jax.ShapeDtypeStruct((4, D, W_IN_SHARD), BF16)],
        compiler_params=_par(1),
    )(g)


def _adam_update(w, g, m, v):
    mn = ADAM_B1 * m + (1.0 - ADAM_B1) * g
    vn = ADAM_B2 * v + (1.0 - ADAM_B2) * (g * g)
    m_hat = mn / (1.0 - ADAM_B1 ** ADAM_STEP)
    v_hat = vn / (1.0 - ADAM_B2 ** ADAM_STEP)
    return -ADAM_LR * (m_hat / (jnp.sqrt(v_hat) + ADAM_EPS) + ADAM_WD * w), mn, vn


def _adamw(w, g, m, v, name):
    shape = w.shape
    C = shape[-1]
    R = w.size // C
    tm = _pick(R, 512, 8)
    spec = pl.BlockSpec((tm, C), lambda i: (i, 0))

    def body(w_ref, g_ref, m_ref, v_ref, d_ref, mo_ref, vo_ref):
        d_ref[...], mo_ref[...], vo_ref[...] = _adam_update(w_ref[...], g_ref[...], m_ref[...], v_ref[...])

    outs = pl.pallas_call(
        body, name=name, grid=(R // tm,), in_specs=[spec] * 4, out_specs=[spec] * 3,
        out_shape=[jax.ShapeDtypeStruct((R, C), F32)] * 3, compiler_params=_par(1),
    )(*(t.reshape(R, C) for t in (w, g, m, v)))
    return tuple(o.reshape(shape) for o in outs)


def _adamw_lead(w, g, m, v, name, tl):
    A, B, C = w.shape
    spec = pl.BlockSpec((tl, B, C), lambda i: (i, 0, 0))

    def body(w_ref, g_ref, m_ref, v_ref, d_ref, mo_ref, vo_ref):
        d_ref[...], mo_ref[...], vo_ref[...] = _adam_update(w_ref[...], g_ref[...], m_ref[...], v_ref[...])

    return pl.pallas_call(
        body, name=name, grid=(A // tl,), in_specs=[spec] * 4, out_specs=[spec] * 3,
        out_shape=[jax.ShapeDtypeStruct((A, B, C), F32)] * 3, compiler_params=_par(1),
    )(w, g, m, v)


def _adamw_layers(w, gs, m, v, name):
    L, R, C = w.shape
    tm = _pick(R, 128, 8)
    spec = pl.BlockSpec((None, tm, C), lambda l, i: (l, i, 0))
    g_specs = [pl.BlockSpec((tm, C), functools.partial(lambda ll, l, i: (jnp.where(l == ll, i, 0), 0), ll))
               for ll in range(L)]

    def body(w_ref, m_ref, v_ref, *rest):
        g_refs, (go_ref, d_ref, mo_ref, vo_ref) = rest[:L], rest[L:]
        l = pl.program_id(0)
        for ll in range(L):
            @pl.when(l == ll)
            def _():
                g = g_refs[ll][...]
                go_ref[...] = g
                d_ref[...], mo_ref[...], vo_ref[...] = _adam_update(w_ref[...], g, m_ref[...], v_ref[...])

    return pl.pallas_call(
        body, name=name, grid=(L, R // tm), in_specs=[spec] * 3 + g_specs, out_specs=[spec] * 4,
        out_shape=[jax.ShapeDtypeStruct((L, R, C), F32)] * 4, compiler_params=_arb(2),
    )(w, m, v, *gs)


def _pair_sums(a, where, b, name):
    NB, _, R, C = a.shape

    def body(where_ref, a_ref, b_ref, p_ref, own_ref):
        s = a_ref[...] + b_ref[...].astype(F32)
        p_ref[...] = s.astype(BF16)

        @pl.when(pl.program_id(0) == where_ref[1])
        def _():
            own_ref[...] = s

    return pl.pallas_call(
        body, name=name,
        grid_spec=pltpu.PrefetchScalarGridSpec(
            num_scalar_prefetch=1, grid=(NB,),
            in_specs=[pl.BlockSpec((None, None, R, C), lambda k, w: (k, w[0], 0, 0)),
                      pl.BlockSpec((None, R, C), lambda k, w: (k, 0, 0))],
            out_specs=[pl.BlockSpec((None, R, C), lambda k, w: (k, 0, 0)), pl.BlockSpec((R, C), lambda k, w: (0, 0))]),
        out_shape=[jax.ShapeDtypeStruct((NB, R, C), BF16), jax.ShapeDtypeStruct((R, C), F32)],
        compiler_params=_arb(1),
    )(where, a, b)


def _sum_own_and_received(own, recv, where, name):
    R, C = own.shape
    tm = _pick(R, 256, 16)

    def body(where_ref, p_ref, r_ref, o_ref):
        o_ref[...] = ((p_ref[...] + r_ref[0].astype(F32)) + r_ref[1].astype(F32)) + r_ref[2].astype(F32)

    return pl.pallas_call(
        body, name=name,
        grid_spec=pltpu.PrefetchScalarGridSpec(
            num_scalar_prefetch=1, grid=(R // tm,),
            in_specs=[pl.BlockSpec((tm, C), lambda i, w: (i, 0)), pl.BlockSpec((3, tm, C), lambda i, w: (0, i, 0))],
            out_specs=pl.BlockSpec((None, tm, C), lambda i, w: (w[0], i, 0))),
        out_shape=jax.ShapeDtypeStruct((2, R, C), F32), compiler_params=_par(1),
    )(where, own, recv)


def _position():
    return lax.axis_index("x"), lax.axis_index("y"), lax.axis_index("c")


def _other_chips(x, y):
    return [(1 - x, y), (x, 1 - y), (1 - x, 1 - y)]


HBM = pl.BlockSpec(memory_space=pl.ANY)


def _allgather8(blk, name, reduce_rows=None):
    M, N = blk.shape

    def body(x_ref, out_ref, *rest):
        if reduce_rows is None:
            send_sems, recv_sems, local_sem = rest
        else:
            sum_ref, send_sems, recv_sems, local_sem = rest
        x, y, c = _position()
        me, sibling = (x, y, c), (x, y, 1 - c)
        chips = _other_chips(x, y)

        def rows(px, py, pc):
            return out_ref.at[pl.ds((4 * px + 2 * py + pc) * M, M), :]

        def copy(k, block, to, src=None):
            return pltpu.make_async_remote_copy(
                src_ref=rows(*block) if src is None else src, dst_ref=rows(*block),
                send_sem=send_sems.at[k], recv_sem=recv_sems.at[k], device_id=to, device_id_type=MESH)

        mine = pltpu.make_async_copy(x_ref, rows(*me), local_sem)
        mine.start()
        first = [copy(0, me, sibling, src=x_ref)]
        first += [copy(1 + j, me, (*chip, c), src=x_ref) for j, chip in enumerate(chips)]
        for cp in first:
            cp.start()
        passed = [copy(4 + j, (*chip, c), sibling) for j, chip in enumerate(chips)]
        for j, chip in enumerate(chips):
            copy(1 + j, (*chip, c), me).wait_recv()
            passed[j].start()
        copy(0, sibling, me).wait_recv()
        for j, chip in enumerate(chips):
            copy(4 + j, (*chip, 1 - c), me).wait_recv()
        for cp in first + passed:
            cp.wait_send()
        mine.wait()
        if reduce_rows is not None:
            tot = out_ref[pl.ds(0, reduce_rows), :]
            for d in range(1, 8):
                tot = tot + out_ref[pl.ds(d * M, reduce_rows), :]
            sum_ref[...] = tot

    vmem = pl.BlockSpec(memory_space=pltpu.VMEM)
    out_shape = [jax.ShapeDtypeStruct((8 * M, N), blk.dtype)]
    if reduce_rows is not None:
        out_shape.append(jax.ShapeDtypeStruct((reduce_rows, N), blk.dtype))
    res = pl.pallas_call(
        body, name=name, out_shape=out_shape, in_specs=[vmem], out_specs=[vmem] * len(out_shape),
        scratch_shapes=[pltpu.SemaphoreType.DMA((7,)), pltpu.SemaphoreType.DMA((7,)), pltpu.SemaphoreType.DMA],
    )(blk)
    return res[0] if reduce_rows is None else (res[0], res[1])


def _gather_phases(layer, srcs, outs, send_sems, recv_sems, local_sems):
    n = len(srcs)
    x, y, c = _position()
    me, sibling = (x, y, c), (x, y, 1 - c)
    chips = _other_chips(x, y)

    def region(t, px, py, pc):
        return outs[t].at[2 * px + py, pc]

    def copy(t, k, block, to, own=False):
        return pltpu.make_async_remote_copy(
            src_ref=srcs[t].at[layer, c] if own else region(t, *block), dst_ref=region(t, *block),
            send_sem=send_sems.at[7 * t + k], recv_sem=recv_sems.at[7 * t + k], device_id=to, device_id_type=MESH)

    def local(t):
        return pltpu.make_async_copy(srcs[t].at[layer, c], region(t, *me), local_sems.at[t])

    def first(t):
        return [copy(t, 0, me, sibling, own=True)] + [copy(t, 1 + j, me, (*chip, c), own=True)
                                                       for j, chip in enumerate(chips)]

    def start():
        for t in range(n):
            local(t).start()
        for t in range(n):
            for cp in first(t):
                cp.start()

    def forward():
        for j, chip in enumerate(chips):
            for t in range(n):
                copy(t, 1 + j, (*chip, c), me).wait_recv()
                copy(t, 4 + j, (*chip, c), sibling).start()

    def finish():
        for t in range(n):
            copy(t, 0, sibling, me).wait_recv()
        for j, chip in enumerate(chips):
            for t in range(n):
                copy(t, 4 + j, (*chip, 1 - c), me).wait_recv()
        for t in range(n):
            for cp in first(t) + [copy(t, 4 + j, (*chip, c), sibling) for j, chip in enumerate(chips)]:
                cp.wait_send()
            local(t).wait()

    return start, forward, finish


def _gather_scratch(n):
    return [pltpu.SemaphoreType.DMA((7 * n,)), pltpu.SemaphoreType.DMA((7 * n,)), pltpu.SemaphoreType.DMA((n,))]


def _gather_out_shapes(shards):
    return [jax.ShapeDtypeStruct((4,) + s.shape[1:], s.dtype) for s in shards]


def _gather_weights(shards, layer, name):
    n = len(shards)

    def body(*refs):
        start, forward, finish = _gather_phases(layer, refs[:n], refs[n:2 * n], *refs[2 * n:])
        start()
        forward()
        finish()

    return pl.pallas_call(
        body, name=name, out_shape=_gather_out_shapes(shards), in_specs=[HBM] * n, out_specs=[HBM] * n,
        scratch_shapes=_gather_scratch(n),
    )(*shards)


def _rs_sibling(gs, name):
    n = len(gs)

    def body(*refs):
        srcs, outs = refs[:n], refs[n:2 * n]
        send_sems, recv_sems = refs[2 * n:]
        x, y, c = _position()
        copies = [pltpu.make_async_remote_copy(
            src_ref=srcs[t].at[k, 1 - c], dst_ref=outs[t].at[k], send_sem=send_sems.at[4 * t + k],
            recv_sem=recv_sems.at[4 * t + k], device_id=(x, y, 1 - c), device_id_type=MESH)
            for t in range(n) for k in range(4)]
        for cp in copies:
            cp.start()
        for cp in copies:
            cp.wait()

    out_shape = [jax.ShapeDtypeStruct((4,) + g.shape[2:], g.dtype) for g in gs]
    return pl.pallas_call(
        body, name=name, out_shape=out_shape, in_specs=[HBM] * n, out_specs=[HBM] * n,
        scratch_shapes=[pltpu.SemaphoreType.DMA((4 * n,)), pltpu.SemaphoreType.DMA((4 * n,))],
    )(*gs)


def _rs_chips(ps, name):
    n = len(ps)

    def body(*refs):
        start, finish = _chips_phases(refs[:n], refs[n:2 * n], *refs[2 * n:])
        start()
        finish()

    return pl.pallas_call(
        body, name=name, out_shape=_chips_out_shapes(ps), in_specs=[HBM] * n, out_specs=[HBM] * n,
        scratch_shapes=_chips_scratch(n),
    )(*ps)


def _chips_phases(srcs, outs, send_sems, recv_sems):
    x, y, c = _position()
    copies = [pltpu.make_async_remote_copy(
        src_ref=srcs[t].at[2 * px + py], dst_ref=outs[t].at[j], send_sem=send_sems.at[3 * t + j],
        recv_sem=recv_sems.at[3 * t + j], device_id=(px, py, c), device_id_type=MESH)
        for t in range(len(srcs)) for j, (px, py) in enumerate(_other_chips(x, y))]

    def start():
        for cp in copies:
            cp.start()

    def finish():
        for cp in copies:
            cp.wait()

    return start, finish


def _chips_scratch(n):
    return [pltpu.SemaphoreType.DMA((3 * n,)), pltpu.SemaphoreType.DMA((3 * n,))]


def _chips_out_shapes(ps):
    return [jax.ShapeDtypeStruct((3,) + p.shape[1:], p.dtype) for p in ps]


def _rs_pair(hs, name):
    n = len(hs)

    def body(*refs):
        bufs = refs[n:2 * n]
        send_sems, recv_sems = refs[2 * n:]
        x, y, c = _position()

        def copy(t, half):
            return pltpu.make_async_remote_copy(
                src_ref=bufs[t].at[half], dst_ref=bufs[t].at[half], send_sem=send_sems.at[t], recv_sem=recv_sems.at[t],
                device_id=(x, y, 1 - c), device_id_type=MESH)

        for t in range(n):
            copy(t, c).start()
        for t in range(n):
            copy(t, 1 - c).wait_recv()
        for t in range(n):
            copy(t, c).wait_send()

    out_shape = [jax.ShapeDtypeStruct(h.shape, h.dtype) for h in hs]
    return pl.pallas_call(
        body, name=name, out_shape=out_shape, in_specs=[HBM] * n, out_specs=[HBM] * n,
        input_output_aliases={t: t for t in range(n)},
        scratch_shapes=[pltpu.SemaphoreType.DMA((n,)), pltpu.SemaphoreType.DMA((n,))],
    )(*hs)


BIG = ("w_in", "w_branch_a", "w_branch_b", "w_out", "w_ffn_in", "w_ffn_out")
CARRY_ATTN = ["w_in"]
CARRY_LOCAL = ["w_ffn_out"]
CARRY_SCAN = ["w_branch_a", "w_branch_b", "w_out"]
CARRY_GU = ["w_ffn_in"]
CARRY_DATTN = ["w_in", "w_ffn_in"]
CARRY_DLOCAL = ["w_branch_a", "w_branch_b", "w_out", "w_ffn_out"]


def _band_bias(rel_table, name, gather=None):
    L, H, n = rel_table.shape
    tab = jnp.pad(rel_table, ((0, 0), (0, 0), (0, NREL_PAD - n))).reshape(L * H, 1, NREL_PAD)
    band = (A_PAST + 1) * CHUNK

    shards, glayer = gather if gather is not None else ((), None)
    ng = len(shards)

    def body(t_ref, *rest):
        srcs, o_ref, gouts, sems = rest[:ng], rest[ng], rest[ng + 1:2 * ng + 1], rest[2 * ng + 1:]
        i = pl.program_id(0)
        done = (_carry(_gather_phases(glayer, srcs, gouts, *sems), i == 0, i == (L * H) // 2, i == L * H - 1)
                if ng else None)
        r = lax.broadcasted_iota(jnp.int32, (NREL_PAD, SKEW_W), 0)
        xi = lax.broadcasted_iota(jnp.int32, (NREL_PAD, SKEW_W), 1)
        diag = jnp.where(xi < KSPAN, xi, xi - SKEW_W)
        rel = jnp.clip(A_PAST * CHUNK - diag, -A_MAX_REL, A_MAX_REL) + A_MAX_REL
        e = _nn(t_ref[...], jnp.where(rel == r, 1.0, 0.0).astype(F32), HI)
        x = jnp.broadcast_to(e, (QBLK, SKEW_W))
        row = lax.broadcasted_iota(jnp.int32, (QBLK, SKEW_W), 0)
        for b in range(QBLK.bit_length() - 1):
            x = jnp.where(((row >> b) & 1) == 1, pltpu.roll(x, 1 << b, 1), x)
        x = x[:, :KSPAN]
        first = (lax.broadcasted_iota(jnp.int32, (QBLK, KSPAN), 0) // CHUNK) * CHUNK
        col = lax.broadcasted_iota(jnp.int32, (QBLK, KSPAN), 1)
        o_ref[...] = jnp.where((col >= first) & (col < first + band), x, NEG)
        if ng:
            done()

    res = pl.pallas_call(
        body, name=name, grid=(L * H,), in_specs=[pl.BlockSpec((None, 1, NREL_PAD), lambda i: (i, 0, 0))] + [HBM] * ng,
        out_specs=[pl.BlockSpec((None, QBLK, KSPAN), lambda i: (i, 0, 0))] + [HBM] * ng,
        out_shape=[jax.ShapeDtypeStruct((L * H, QBLK, KSPAN), F32)] + _gather_out_shapes(shards),
        scratch_shapes=_gather_scratch(ng) if ng else [], compiler_params=_arb(1),
    )(tab, *shards)
    return res[0].reshape(L, H, QBLK, KSPAN), list(res[1:])


def _col_row_forms(t, S):
    nc = S // CHUNK
    return t.T.reshape(nc, CHUNK, B_HEADS), t.reshape(B_HEADS, nc, CHUNK).transpose(1, 0, 2)


def _weight_view(name, gathered, tag):
    if name in ("w_out", "w_ffn_out"):
        return gathered.reshape(8 * gathered.shape[2], gathered.shape[3])
    stacked = gathered.reshape(4, 2 * gathered.shape[2], gathered.shape[3])
    return _reorder_w_in(stacked[None], f"w_in_cols_{tag}")[0] if name == "w_in" else stacked


def _layer_fwd(l, x, mod, W, P, big, gather=None, late=None):
    S, D = x.shape
    n = lambda s: f"{s}_l{l}"
    sh1, sc1, gt1, sh2, sc2, gt2 = (mod[i:i + 1] for i in range(6))
    h1 = _lnmod_fwd(x, P["norm1_g"][l:l + 1], sc1, sh1, n("ln1"))
    if late is None:
        proj = _matmul(h1, W["w_in"], "nn", F32, n("proj"), tn=1152)
    else:
        proj, got = _matmul(h1, W["w_in"], "nn", F32, n("proj"), tn=1152, gather=(late[1], l))
        W = {**W, **{k: _weight_view(k, t, f"l{l}") for k, t in zip(late[0], got)}}
    part = (lambda names: ([gather[0][BIG.index(k)] for k in names], gather[1])) if gather is not None else (lambda names: None)
    ya, got_a = _attn_fwd(proj, big, n("attn"), part(CARRY_ATTN))
    ba = proj[:, P_BA:P_BA + 2 * B_HEADS]
    b_t, a_t = ba[:, :B_HEADS].T, ba[:, B_HEADS:].T
    alog, dtb = P["a_log"][l].reshape(B_HEADS, 1), P["dt_bias"][l].reshape(B_HEADS, 1)
    beta, gam = _gdn_gates_fwd(b_t, a_t, alog, dtb, n("gates"))
    bcol, _ = _col_row_forms(beta, S)
    gcol, grow = _col_row_forms(gam, S)
    qn, kn, v = _gdn_pre_fwd(proj, P["w_conv"][l], n("gdnpre"))
    tsave, amat, u, w, got_l = _gdn_local_fwd(qn, kn, v, bcol, gcol, grow, n("gdnlocal"), part(CARRY_LOCAL))
    o, ssave, got_s = _gdn_scan_fwd(qn, kn, u, w, amat, gcol, grow, n("gdnscan"), part(CARRY_SCAN))
    yb = _gdn_post_fwd(o, proj, P["gdn_norm_g"][l:l + 1], n("gdnpost"))
    pa = _matmul(ya, W["w_branch_a"], "nn", BF16, n("pa"), tm=2048, stacked=True)
    pb = _matmul(yb, W["w_branch_b"], "nn", BF16, n("pb"), tm=2048, stacked=True)
    merged = _merge_fwd(proj, pa, pb, n("merge"))
    ao = _matmul(merged, W["w_out"], "nn", F32, n("ao"))
    x1 = _gate_fwd(x, ao, gt1, n("res1"))
    h2 = _lnmod_fwd(x1, P["norm2_g"][l:l + 1], sc2, sh2, n("ln2"))
    gu = _matmul(h2, W["w_ffn_in"], "nn", BF16, n("gu"), stacked=True, gather=part(CARRY_GU))
    gu, got_g = gu if gather is not None else (gu, [])
    got = dict(zip(CARRY_ATTN + CARRY_LOCAL + CARRY_SCAN + CARRY_GU, got_a + got_l + got_s + got_g))
    gathered = {k: got[k] for k in BIG} if gather is not None else None
    act = _ffn_act_fwd(gu, n("act"))
    fo = _matmul(act, W["w_ffn_out"], "nn", F32, n("fo"), tk=1408)
    x2 = _gate_fwd(x1, fo, gt2, n("res2"))
    saved = dict(x=x, h1=h1, proj=proj, ya=ya, b_t=b_t, a_t=a_t, bcol=bcol, gcol=gcol, grow=grow,
                 qn=qn, kn=kn, v=v, o=o, tsave=tsave, ssave=ssave, amat=amat, u=u, w=w, yb=yb, pa=pa, pb=pb,
                 merged=merged, ao=ao, x1=x1,
                 h2=h2, gu=gu, act=act, fo=fo)
    return x2, saved, gathered, W


def _layer_bwd(l, dx2, sv, mod, W, P, big, exchange=()):
    S, D = dx2.shape
    n = lambda s: f"{s}_l{l}"
    sh1, sc1, gt1, sh2, sc2, gt2 = (mod[i:i + 1] for i in range(6))
    g, pay = {}, {}
    view = lambda t: t.reshape((4, 2, t.shape[-2] // (2 if t.ndim == 3 else 8), t.shape[-1]))
    dz2, dgt2 = _gate_bwd(dx2, sv["fo"], gt2, n("dres2"))
    g["w_ffn_out"], pay["w_ffn_out"] = map(view, _matmul(sv["act"], dz2, "tn", F32, n("dwfo"), tm=1408, also_bf16=True))
    dact = _matmul(dz2, W["w_ffn_out"], "nt", BF16, n("dact"), tn=1408)
    dgu = _ffn_act_bwd(sv["gu"], dact, n("dgu"))
    g["w_ffn_in"], pay["w_ffn_in"] = map(view, _matmul(sv["h2"], dgu, "tn", F32, n("dwfi"), out_stacked=True,
                                                       also_bf16=True))
    dh2 = _matmul(dgu, W["w_ffn_in"], "nt", F32, n("dh2"), stacked=True)
    dx1, dsh2, dsc2, dn2 = _lnmod_bwd(dh2, sv["x1"], P["norm2_g"][l:l + 1], sc2, dx2, n("dln2"))
    dz1, dgt1 = _gate_bwd(dx1, sv["ao"], gt1, n("dres1"))
    g["w_out"], pay["w_out"] = map(view, _matmul(sv["merged"], dz1, "tn", F32, n("dwo"), also_bf16=True))
    dmerged = _matmul(dz1, W["w_out"], "nt", BF16, n("dmerged"))
    dgab, dpa, dpb = _merge_bwd(sv["proj"], sv["pa"], sv["pb"], dmerged, n("dmerge"))
    g["w_branch_a"], pay["w_branch_a"] = map(view, _matmul(sv["ya"], dpa, "tn", F32, n("dwa"), out_stacked=True,
                                                           also_bf16=True))
    g["w_branch_b"], pay["w_branch_b"] = map(view, _matmul(sv["yb"], dpb, "tn", F32, n("dwb"), out_stacked=True,
                                                           also_bf16=True))
    dya = _matmul(dpa, W["w_branch_a"], "nt", BF16, n("dya"), tm=2048, stacked=True)
    dyb = _matmul(dpb, W["w_branch_b"], "nt", F32, n("dyb"), tm=2048, stacked=True)
    ex = (lambda names: [exchange[BIG.index(k)] for k in names]) if len(exchange) else (lambda names: ())
    dq, dk, dv, dbig, rec_a = _attn_bwd(sv["proj"], big, dya, n("dattn"), ex(CARRY_DATTN))
    g["rel_table"] = _rel_table_grad(dbig, n("drel"))[:, 0, :2 * A_MAX_REL + 1]
    do, dzb, dng = _gdn_post_bwd(dyb, sv["o"], sv["proj"], P["gdn_norm_g"][l:l + 1], n("dgdnpost"))
    g["gdn_norm_g"] = dng[0]
    du, dw, dqd, dkd, da, dgl = _gdn_scan_bwd(sv["qn"], sv["kn"], sv["u"], sv["w"], sv["amat"], sv["gcol"], sv["grow"],
                                              sv["ssave"], do, n("dgdnscan"))
    dqn, dkn, dvv, dbc, dgc, dgr, rec_l = _gdn_local_bwd(
        sv["qn"], sv["kn"], sv["v"], sv["bcol"], sv["gcol"], sv["grow"], sv["tsave"], du, dw, dqd, dkd, da, dgl,
        n("dgdnlocal"), ex(CARRY_DLOCAL))
    rec = dict(zip(CARRY_DATTN + CARRY_DLOCAL, rec_a + rec_l))
    received = [rec[k] for k in BIG] if len(exchange) else None
    dbeta_t = dbc.reshape(S, B_HEADS).T
    dgam_a = dgc.reshape(S, B_HEADS).T
    dgam_b = dgr.transpose(1, 0, 2).reshape(B_HEADS, S)
    alog, dtb = P["a_log"][l].reshape(B_HEADS, 1), P["dt_bias"][l].reshape(B_HEADS, 1)
    db_t, da_t, dal, ddt = _gdn_gates_bwd(dbeta_t, dgam_a, dgam_b, sv["b_t"], sv["a_t"], alog, dtb, n("dgates"))
    g["a_log"], g["dt_bias"] = dal[:, 0], ddt[:, 0]
    dy = _gdn_pre_bwd_a(sv["proj"], P["w_conv"][l], dqn, dkn, dvv, n("dgdnpre_a"))
    dqkvb, g["w_conv"] = _gdn_pre_bwd_b(sv["proj"], P["w_conv"][l], dy, n("dgdnpre_b"))
    dba = jnp.concatenate([db_t.T, da_t.T, jnp.zeros((S, P_END - P_BA - 2 * B_HEADS), F32)], axis=1)
    dproj = jnp.concatenate([dq.astype(BF16), dk.astype(BF16), dv.astype(BF16), dqkvb, dgab, dzb, dba.astype(BF16)],
                            axis=1)
    g["w_in"], pay["w_in"] = map(view, _restore_w_in(_matmul(sv["h1"], dproj, "tn", F32, n("dwin"), tn=1152),
                                                     n("dwin_cols")))
    dh1 = _matmul(dproj, W["w_in"], "nt", F32, n("dh1"), tk=1152)
    dx, dsh1, dsc1, dn1 = _lnmod_bwd(dh1, sv["x"], P["norm1_g"][l:l + 1], sc1, dx1, n("dln1"))
    g["norm1_g"], g["norm2_g"] = dn1[0], dn2[0]
    dmod = jnp.concatenate([dsh1, dsc1, dgt1, dsh2, dsc2, dgt2], axis=1)[0]
    return dx, g, pay, dmod, received


SMALL = ("norm1_g", "norm2_g", "rel_table", "w_conv", "a_log", "dt_bias", "gdn_norm_g")
SMALL_PACK_C = 1024


def _as_rows(t):
    flat = t.reshape(-1)
    rows = -(-flat.shape[0] // SMALL_PACK_C)
    return jnp.pad(flat, (0, rows * SMALL_PACK_C - flat.shape[0])).reshape(rows, SMALL_PACK_C)


def _pack_rows(parts):
    blk = jnp.concatenate([_as_rows(p) for p in parts], axis=0)
    return jnp.pad(blk, ((0, -blk.shape[0] % 8), (0, 0)))


def _unpack_rows(blk, shapes):
    out, r = [], 0
    for shp in shapes:
        size = int(np.prod(shp))
        rows = -(-size // SMALL_PACK_C)
        out.append(blk[..., r:r + rows, :].reshape(blk.shape[:-2] + (rows * SMALL_PACK_C,))[..., :size]
                   .reshape(blk.shape[:-2] + tuple(shp)))
        r += rows
    return out


def kernel(x, c, w_ada, b_ada, norm1_g, norm2_g, w_in, rel_table, w_conv, a_log, dt_bias, gdn_norm_g, w_branch_a, w_branch_b, w_out, w_ffn_in, w_ffn_out, final_g, loss_target, m_w_ada, m_b_ada, m_norm1_g, m_norm2_g, m_w_in, m_rel_table, m_w_conv, m_a_log, m_dt_bias, m_gdn_norm_g, m_w_branch_a, m_w_branch_b, m_w_out, m_w_ffn_in, m_w_ffn_out, m_final_g, v_w_ada, v_b_ada, v_norm1_g, v_norm2_g, v_w_in, v_rel_table, v_w_conv, v_a_log, v_dt_bias, v_gdn_norm_g, v_w_branch_a, v_w_branch_b, v_w_out, v_w_ffn_in, v_w_ffn_out, v_final_g):
    weights = dict(w_ada=w_ada, b_ada=b_ada, norm1_g=norm1_g, norm2_g=norm2_g, w_in=w_in, rel_table=rel_table,
                   w_conv=w_conv, a_log=a_log, dt_bias=dt_bias, gdn_norm_g=gdn_norm_g, w_branch_a=w_branch_a,
                   w_branch_b=w_branch_b, w_out=w_out, w_ffn_in=w_ffn_in, w_ffn_out=w_ffn_out, final_g=final_g)
    mom_m = dict(w_ada=m_w_ada, b_ada=m_b_ada, norm1_g=m_norm1_g, norm2_g=m_norm2_g, w_in=m_w_in,
                 rel_table=m_rel_table, w_conv=m_w_conv, a_log=m_a_log, dt_bias=m_dt_bias, gdn_norm_g=m_gdn_norm_g,
                 w_branch_a=m_w_branch_a, w_branch_b=m_w_branch_b, w_out=m_w_out, w_ffn_in=m_w_ffn_in,
                 w_ffn_out=m_w_ffn_out, final_g=m_final_g)
    mom_v = dict(w_ada=v_w_ada, b_ada=v_b_ada, norm1_g=v_norm1_g, norm2_g=v_norm2_g, w_in=v_w_in,
                 rel_table=v_rel_table, w_conv=v_w_conv, a_log=v_a_log, dt_bias=v_dt_bias, gdn_norm_g=v_gdn_norm_g,
                 w_branch_a=v_w_branch_a, w_branch_b=v_w_branch_b, w_out=v_w_out, w_ffn_in=v_w_ffn_in,
                 w_ffn_out=v_w_ffn_out, final_g=v_final_g)
    xi, yi, ci = _position()
    chip = 2 * xi + yi
    dev = 2 * chip + ci
    L, D = norm1_g.shape
    NMOD = b_ada.shape[1] // D
    ns = w_ada.shape[2]
    cs = w_conv.shape[2]

    first_blk = _pack_rows([c, w_conv])
    first_all = _allgather8(first_blk, "gather_c").reshape(8, first_blk.shape[0], SMALL_PACK_C)
    c_all, w_conv_all = _unpack_rows(first_all, [(D,), w_conv.shape])
    w_conv_full = w_conv_all.reshape(4, 2, L, CONV_K, cs)[:, 0].transpose(1, 2, 0, 3).reshape(L, CONV_K, 4 * cs)
    b_shard = lax.dynamic_slice_in_dim(b_ada, chip * ns, ns, axis=1).reshape(L, 1, ns)
    mod_shard = _ada_mod(c_all, w_ada, b_shard, "ada_mod")
    mod_all = _allgather8(mod_shard.reshape(L * 8, ns), "gather_mod").reshape(4, 2, L, 8, ns)
    mod = lax.dynamic_index_in_dim(mod_all[:, 0], dev, axis=2, keepdims=False)
    mod = mod.transpose(1, 0, 2).reshape(L, NMOD, D)

    shards = [weights[k].astype(BF16) for k in BIG]
    shards = [s.reshape(s.shape[0], 2, s.shape[1] // 2, s.shape[2]) for s in shards]
    P = dict(norm1_g=norm1_g, norm2_g=norm2_g, w_conv=w_conv_full, a_log=a_log, dt_bias=dt_bias,
             gdn_norm_g=gdn_norm_g)
    shard_of = dict(zip(BIG, shards))

    big, got = _band_bias(rel_table, "band_bias", ([shard_of["w_in"]], 0))
    alone = ["w_branch_a", "w_branch_b", "w_out", "w_ffn_out"]
    got += _gather_weights([shard_of[k] for k in alone], 0, "gather_weights_l0")
    W = [{k: _weight_view(k, t, "l0") for k, t in zip(["w_in"] + alone, got)}]
    late = (["w_ffn_in"], [shard_of["w_ffn_in"]])
    xc = x[0]
    saved = []
    for l in range(L):
        xc, sv, gathered, W[l] = _layer_fwd(l, xc, mod[l], W[l], P, big[l], (shards, l + 1) if l + 1 < L else None,
                                           late if l == 0 else None)
        saved.append(sv)
        if l + 1 < L:
            W.append({k: _weight_view(k, gathered[k], f"l{l + 1}") for k in BIG})
    dx, loss_dev, dfinal = _loss_head(xc, final_g.reshape(1, D), loss_target[0], "loss_head")

    where = jnp.stack([ci, chip]).astype(jnp.int32)
    grads = [None] * L
    dmods = [None] * L
    shard_grads = {k: [None] * L for k in BIG}

    def finish_reduce_scatter(l, sums, from_chips):
        halves = [_sum_own_and_received(s_[1], r_, where, f"rs_sum_{k}_l{l}")
                  for k, s_, r_ in zip(BIG, sums, from_chips)]
        for k, t in zip(BIG, _rs_pair(halves, f"rs_pair_l{l}")):
            shard_grads[k][l] = t.reshape(2 * t.shape[1], t.shape[2])

    pending = None
    for l in reversed(range(L)):
        exchange = [s_[0] for s_ in pending] if pending is not None else ()
        dx, grads[l], pay, dmods[l], received = _layer_bwd(l, dx, saved[l], mod[l], W[l], P, big[l], exchange)
        if pending is not None:
            finish_reduce_scatter(l + 1, pending, received)
        gs = [grads[l][k] for k in BIG]
        from_sibling = _rs_sibling([pay[k] for k in BIG], f"rs_sibling_l{l}")
        pending = [_pair_sums(g_, where, r_, f"rs_pair_sum_{k}_l{l}") for k, g_, r_ in zip(BIG, gs, from_sibling)]
    finish_reduce_scatter(0, pending, _rs_chips([s_[0] for s_ in pending], "rs_chips_l0"))
    dmod = jnp.stack(dmods)

    small = {k: jnp.stack([grads[l][k] for l in range(L)]) for k in SMALL}
    parts = [dmod] + [small[k] for k in SMALL] + [dfinal, loss_dev[0, :1]]
    small_blk = _pack_rows(parts)
    srows = small_blk.shape[0]
    small_all, small_sum = _allgather8(small_blk, "gather_small", reduce_rows=srows)
    shapes = [dmod.shape] + [small[k].shape for k in SMALL] + [(D,), (1,)]
    tot = _unpack_rows(small_sum, shapes)
    G = dict(zip(SMALL, tot[1:1 + len(SMALL)]))
    G["b_ada"] = tot[0].reshape(b_ada.shape)
    G["w_conv"] = lax.dynamic_slice_in_dim(G["w_conv"], chip * cs, cs, axis=2)
    G["final_g"] = tot[-2]
    loss = tot[-1][0]
    dmod_all = _unpack_rows(small_all.reshape(8, srows, SMALL_PACK_C), [dmod.shape])[0]
    dmod_cols = lax.dynamic_slice_in_dim(dmod_all, chip * ns, ns, axis=2).transpose(1, 0, 2)
    G["w_ada"] = _ada_wgrad(c_all, dmod_cols, "ada_wgrad")

    order = ["w_ada", "b_ada", "norm1_g", "norm2_g", "w_in", "rel_table", "w_conv", "a_log", "dt_bias", "gdn_norm_g",
             "w_branch_a", "w_branch_b", "w_out", "w_ffn_in", "w_ffn_out", "final_g"]
    deltas, new_m, new_v = {}, {}, {}
    for k in order:
        w = weights[k]
        if k == "w_in":
            to_cols = lambda t: jnp.transpose(t, (2, 0, 1))
            from_cols = lambda t: jnp.transpose(t, (1, 2, 0))
            gt = to_cols(jnp.stack(shard_grads[k]))
            d_, m_, v_ = _adamw_lead(to_cols(w), gt, to_cols(mom_m[k]), to_cols(mom_v[k]), f"adamw_{k}",
                                     W_IN_SHARD // 30)
            G[k], deltas[k], new_m[k], new_v[k] = from_cols(gt), from_cols(d_), from_cols(m_), from_cols(v_)
            continue
        if k in BIG:
            G[k], deltas[k], new_m[k], new_v[k] = _adamw_layers(w, shard_grads[k], mom_m[k], mom_v[k], f"adamw_{k}")
            continue
        as2d = (lambda t: t.reshape(1, -1)) if w.ndim == 1 else (lambda t: t)
        d_, m_, v_ = _adamw(as2d(w), as2d(G[k]), as2d(mom_m[k]), as2d(mom_v[k]), f"adamw_{k}")
        deltas[k], new_m[k], new_v[k] = d_.reshape(w.shape), m_.reshape(w.shape), v_.reshape(w.shape)
    return (loss, dx[None], *[G[k] for k in order], *[deltas[k] for k in order], *[new_m[k] for k in order],
            *[new_v[k] for k in order])
```

```python
import functools

import numpy as np
import jax
import jax.numpy as jnp
from jax import lax
from jax.experimental import pallas as pl
from jax.experimental.pallas import tpu as pltpu

F32 = jnp.float32
BF16 = jnp.bfloat16
HI = lax.Precision.HIGHEST
SOLVE_PREC = lax.Precision.HIGH
MESH = pl.DeviceIdType.MESH

EPS = 1e-6
CHUNK = 64
A_HEADS = 8
A_DH = 64
A_PAST = 8
A_MAX_REL = 128
B_HEADS = 4
B_DH = 128
CONV_K = 4
LANE = 128
QBLK = 4 * CHUNK
KSPAN = QBLK + A_PAST * CHUNK
NEG = -1e30

ADAM_LR = 0.001
ADAM_B1 = 0.9
ADAM_B2 = 0.999
ADAM_EPS = 1e-08
ADAM_WD = 0.01
ADAM_STEP = 10

P_QKVA, P_QKVB, P_GA, P_GB, P_Z, P_BA, P_END = 0, 1536, 3072, 4096, 5120, 5632, 5760
W_IN_SHARD = 1410


def _sigmoid(x):
    return 1.0 / (1.0 + jnp.exp(-x))


def _dot(a, b, ca, cb, prec):
    lead = a.ndim - 2
    batch = ((0,), (0,)) if lead else ((), ())
    return lax.dot_general(a, b, (((ca + lead,), (cb + lead,)), batch), precision=prec, preferred_element_type=F32)


def _nn(a, b, prec=None):
    return _dot(a, b, 1, 0, prec)


def _nt(a, b, prec=None):
    return _dot(a, b, 1, 1, prec)


def _tn(a, b, prec=None):
    return _dot(a, b, 0, 0, prec)


def _bnn(a, b):
    return _nn(a.astype(BF16), b.astype(BF16))


def _bnt(a, b):
    return _nt(a.astype(BF16), b.astype(BF16))


def _btn(a, b):
    return _tn(a.astype(BF16), b.astype(BF16))


def _pick(n, target, unit=LANE):
    best = None
    for t in range(unit, min(n, target) + 1, unit):
        if n % t == 0:
            best = t
    return best if best is not None else n


def _acc(ref, val, i):
    @pl.when(i == 0)
    def _():
        ref[...] = val

    @pl.when(i != 0)
    def _():
        ref[...] += val


def _arb(n):
    return pltpu.CompilerParams(dimension_semantics=("arbitrary",) * n)


def _par(n):
    return pltpu.CompilerParams(dimension_semantics=("parallel",) * n)


def _matmul(a, b, mode, out_dtype, name, tm=1024, tn=1024, tk=1024, layer=None, stacked=False, out_stacked=False,
            also_bf16=False, gather=None):
    bs = b.shape[1:] if layer is not None else b.shape
    if mode == "nn":
        M, K = a.shape
        N = 4 * bs[2] if stacked else bs[1]
        if stacked:
            tn = bs[2]
    elif mode == "nt":
        M, K = a.shape
        N = bs[1] if stacked else bs[0]
        if stacked:
            tk = bs[2]
    else:
        K, M = a.shape
        N = bs[1]
        if out_stacked:
            tn = N // 4
    tm, tn, tk = _pick(M, tm), _pick(N, tn), _pick(K, tk)
    nk = K // tk
    lead = () if layer is None else (layer,)
    lead_blk = () if layer is None else (None,)
    if mode == "nn":
        a_spec = pl.BlockSpec((tm, tk), lambda i, j, k: (i, k))
        if stacked:
            b_spec = pl.BlockSpec(lead_blk + (None, tk, tn), lambda i, j, k: lead + (j, k, 0))
        else:
            b_spec = pl.BlockSpec(lead_blk + (tk, tn), lambda i, j, k: lead + (k, j))
        dot = _nn
    elif mode == "nt":
        a_spec = pl.BlockSpec((tm, tk), lambda i, j, k: (i, k))
        if stacked:
            b_spec = pl.BlockSpec(lead_blk + (None, tn, tk), lambda i, j, k: lead + (k, j, 0))
        else:
            b_spec = pl.BlockSpec(lead_blk + (tn, tk), lambda i, j, k: lead + (j, k))
        dot = _nt
    else:
        a_spec = pl.BlockSpec((tk, tm), lambda i, j, k: (k, i))
        b_spec = pl.BlockSpec((tk, tn), lambda i, j, k: (k, j))
        dot = _tn
    if out_stacked:
        o_spec = pl.BlockSpec((None, tm, tn), lambda i, j, k: (j, i, 0))
        o_shape = jax.ShapeDtypeStruct((4, M, tn), out_dtype)
    else:
        o_spec = pl.BlockSpec((tm, tn), lambda i, j, k: (i, j))
        o_shape = jax.ShapeDtypeStruct((M, N), out_dtype)

    shards, glayer = gather if gather is not None else ((), None)
    ng = len(shards)
    o_shapes = [o_shape] + ([jax.ShapeDtypeStruct(o_shape.shape, BF16)] if also_bf16 else [])
    no = len(o_shapes)
    gi, gj = M // tm, N // tn

    def write(o_refs, val):
        for o_ref in o_refs:
            o_ref[...] = val.astype(o_ref.dtype)

    def body(a_ref, b_ref, *refs):
        srcs, o_refs, gouts, scratch = refs[:ng], refs[ng:ng + no], refs[ng + no:2 * ng + no], refs[2 * ng + no:]
        i, j, k = pl.program_id(0), pl.program_id(1), pl.program_id(2)
        if ng:
            start = (j == 0) & (k == 0)
            done = _carry(_gather_phases(glayer, srcs, gouts, *scratch[-3:]), (i == 0) & start, (i == gi - 1) & start,
                          (i == gi - 1) & (j == gj - 1) & (k == nk - 1))
        if nk == 1:
            write(o_refs, dot(a_ref[...], b_ref[...]))
        else:
            acc_ref = scratch[0]

            @pl.when(k == 0)
            def _():
                acc_ref[...] = jnp.zeros_like(acc_ref)

            acc_ref[...] += dot(a_ref[...], b_ref[...])

            @pl.when(k == nk - 1)
            def _():
                write(o_refs, acc_ref[...])
        if ng:
            done()

    sem = ("arbitrary",) * 3 if ng else ("parallel", "parallel", "arbitrary")
    res = pl.pallas_call(
        body, name=name, grid=(gi, gj, nk), in_specs=[a_spec, b_spec] + [HBM] * ng,
        out_specs=[o_spec] * no + [HBM] * ng, out_shape=o_shapes + _gather_out_shapes(shards),
        scratch_shapes=([] if nk == 1 else [pltpu.VMEM((tm, tn), F32)]) + (_gather_scratch(ng) if ng else []),
        compiler_params=pltpu.CompilerParams(dimension_semantics=sem),
    )(a, b, *shards)
    out = tuple(res[:no]) if also_bf16 else res[0]
    return (out, list(res[no:])) if ng else out


def _rows(tm, n, col=0):
    return pl.BlockSpec((tm, n), lambda i: (i, col))


def _vec(n):
    return pl.BlockSpec((1, n), lambda i: (0, 0))


def _lnmod_fwd(x, g, sc, sh, name):
    S, D = x.shape
    tm = _pick(S, 512, 8)

    def body(x_ref, g_ref, sc_ref, sh_ref, o_ref):
        xv = x_ref[...]
        r = lax.rsqrt(jnp.mean(xv * xv, axis=-1, keepdims=True) + EPS)
        o_ref[...] = ((xv * r * g_ref[...]) * (1.0 + sc_ref[...]) + sh_ref[...]).astype(BF16)

    return pl.pallas_call(
        body, name=name, grid=(S // tm,),
        in_specs=[_rows(tm, D), _vec(D), _vec(D), _vec(D)], out_specs=_rows(tm, D),
        out_shape=jax.ShapeDtypeStruct((S, D), BF16), compiler_params=_par(1),
    )(x, g, sc, sh)


def _lnmod_bwd(dh, x, g, sc, dres, name):
    S, D = x.shape
    tm = _pick(S, 512, 8)

    def body(dh_ref, x_ref, g_ref, sc_ref, dres_ref, dx_ref, dsh_ref, dsc_ref, dg_ref):
        i = pl.program_id(0)
        xv = x_ref[...]
        dh_ = dh_ref[...]
        r = lax.rsqrt(jnp.mean(xv * xv, axis=-1, keepdims=True) + EPS)
        xhat = xv * r
        gv = g_ref[...]
        dn = dh_ * (1.0 + sc_ref[...])
        dxhat = dn * gv
        dx_ref[...] = dres_ref[...] + r * (dxhat - xhat * jnp.mean(dxhat * xhat, axis=-1, keepdims=True))
        _acc(dsh_ref, jnp.sum(dh_, axis=0, keepdims=True), i)
        _acc(dsc_ref, jnp.sum(dh_ * (xhat * gv), axis=0, keepdims=True), i)
        _acc(dg_ref, jnp.sum(dn * xhat, axis=0, keepdims=True), i)

    return pl.pallas_call(
        body, name=name, grid=(S // tm,),
        in_specs=[_rows(tm, D), _rows(tm, D), _vec(D), _vec(D), _rows(tm, D)],
        out_specs=[_rows(tm, D), _vec(D), _vec(D), _vec(D)],
        out_shape=[jax.ShapeDtypeStruct((S, D), F32)] + [jax.ShapeDtypeStruct((1, D), F32)] * 3,
        compiler_params=_arb(1),
    )(dh, x, g, sc, dres)


def _gate_fwd(x, y, gt, name):
    S, D = x.shape
    tm = _pick(S, 512, 8)

    def body(x_ref, y_ref, gt_ref, o_ref):
        o_ref[...] = x_ref[...] + gt_ref[...] * y_ref[...]

    return pl.pallas_call(
        body, name=name, grid=(S // tm,), in_specs=[_rows(tm, D), _rows(tm, D), _vec(D)], out_specs=_rows(tm, D),
        out_shape=jax.ShapeDtypeStruct((S, D), F32), compiler_params=_par(1),
    )(x, y, gt)


def _gate_bwd(dx, y, gt, name):
    S, D = dx.shape
    tm = _pick(S, 512, 8)

    def body(dx_ref, y_ref, gt_ref, dz_ref, dgt_ref):
        i = pl.program_id(0)
        d = dx_ref[...]
        dz_ref[...] = (d * gt_ref[...]).astype(BF16)
        _acc(dgt_ref, jnp.sum(d * y_ref[...], axis=0, keepdims=True), i)

    return pl.pallas_call(
        body, name=name, grid=(S // tm,), in_specs=[_rows(tm, D), _rows(tm, D), _vec(D)],
        out_specs=[_rows(tm, D), _vec(D)],
        out_shape=[jax.ShapeDtypeStruct((S, D), BF16), jax.ShapeDtypeStruct((1, D), F32)],
        compiler_params=_arb(1),
    )(dx, y, gt)


def _ffn_act_fwd(gu, name):
    S, H2 = gu.shape
    H = H2 // 2
    tm = _pick(S, 256, 8)

    def body(g_ref, u_ref, o_ref):
        gv = g_ref[...].astype(F32)
        o_ref[...] = (gv * _sigmoid(gv) * u_ref[...].astype(F32)).astype(BF16)

    return pl.pallas_call(
        body, name=name, grid=(S // tm,), in_specs=[_rows(tm, H, 0), _rows(tm, H, 1)], out_specs=_rows(tm, H),
        out_shape=jax.ShapeDtypeStruct((S, H), BF16), compiler_params=_par(1),
    )(gu, gu)


def _ffn_act_bwd(gu, dact, name):
    S, H2 = gu.shape
    H = H2 // 2
    tm = _pick(S, 256, 8)

    def body(g_ref, u_ref, da_ref, o_ref):
        gv = g_ref[...].astype(F32)
        s = _sigmoid(gv)
        da = da_ref[...].astype(F32)
        o_ref[:, :H] = (da * u_ref[...].astype(F32) * (s * (1.0 + gv * (1.0 - s)))).astype(BF16)
        o_ref[:, H:] = (da * (gv * s)).astype(BF16)

    return pl.pallas_call(
        body, name=name, grid=(S // tm,), in_specs=[_rows(tm, H, 0), _rows(tm, H, 1), _rows(tm, H)],
        out_specs=_rows(tm, H2), out_shape=jax.ShapeDtypeStruct((S, H2), BF16), compiler_params=_par(1),
    )(gu, gu, dact)


def _merge_fwd(proj, pa, pb, name):
    S, D = pa.shape
    tm = _pick(S, 512, 8)

    def body(ga_ref, gb_ref, pa_ref, pb_ref, o_ref):
        o_ref[...] = (_sigmoid(ga_ref[...]) * pa_ref[...].astype(F32)
                      + _sigmoid(gb_ref[...]) * pb_ref[...].astype(F32)).astype(BF16)

    return pl.pallas_call(
        body, name=name, grid=(S // tm,),
        in_specs=[_rows(tm, D, P_GA // D), _rows(tm, D, P_GB // D), _rows(tm, D), _rows(tm, D)],
        out_specs=_rows(tm, D), out_shape=jax.ShapeDtypeStruct((S, D), BF16), compiler_params=_par(1),
    )(proj, proj, pa, pb)


def _merge_bwd(proj, pa, pb, dm, name):
    S, D = pa.shape
    tm = _pick(S, 512, 8)

    def body(ga_ref, gb_ref, pa_ref, pb_ref, dm_ref, dg_ref, dpa_ref, dpb_ref):
        d = dm_ref[...].astype(F32)
        sa = _sigmoid(ga_ref[...])
        sb = _sigmoid(gb_ref[...])
        dg_ref[:, :D] = (d * pa_ref[...].astype(F32) * sa * (1.0 - sa)).astype(BF16)
        dg_ref[:, D:] = (d * pb_ref[...].astype(F32) * sb * (1.0 - sb)).astype(BF16)
        dpa_ref[...] = (d * sa).astype(BF16)
        dpb_ref[...] = (d * sb).astype(BF16)

    return pl.pallas_call(
        body, name=name, grid=(S // tm,),
        in_specs=[_rows(tm, D, P_GA // D), _rows(tm, D, P_GB // D), _rows(tm, D), _rows(tm, D), _rows(tm, D)],
        out_specs=[_rows(tm, 2 * D), _rows(tm, D), _rows(tm, D)],
        out_shape=[jax.ShapeDtypeStruct((S, 2 * D), BF16), jax.ShapeDtypeStruct((S, D), BF16),
                   jax.ShapeDtypeStruct((S, D), BF16)],
        compiler_params=_par(1),
    )(proj, proj, pa, pb, dm)


def _loss_head(x, g, target, name):
    S, D = x.shape
    tm = _pick(S, 512, 8)

    def body(x_ref, g_ref, t_ref, dx_ref, loss_ref, dg_ref):
        i = pl.program_id(0)
        xv = x_ref[...]
        gv = g_ref[...]
        r = lax.rsqrt(jnp.mean(xv * xv, axis=-1, keepdims=True) + EPS)
        xhat = xv * r
        err = xhat * gv - t_ref[...]
        part = 0.5 * jnp.sum(jnp.mean(err * err, axis=-1, keepdims=True), axis=0, keepdims=True)
        _acc(loss_ref, jnp.broadcast_to(part, (1, LANE)), i)
        dy = err * (1.0 / D)
        _acc(dg_ref, jnp.sum(dy * xhat, axis=0, keepdims=True), i)
        dxhat = dy * gv
        dx_ref[...] = r * (dxhat - xhat * jnp.mean(dxhat * xhat, axis=-1, keepdims=True))

    return pl.pallas_call(
        body, name=name, grid=(S // tm,), in_specs=[_rows(tm, D), _vec(D), _rows(tm, D)],
        out_specs=[_rows(tm, D), _vec(LANE), _vec(D)],
        out_shape=[jax.ShapeDtypeStruct((S, D), F32), jax.ShapeDtypeStruct((1, LANE), F32),
                   jax.ShapeDtypeStruct((1, D), F32)],
        compiler_params=_arb(1),
    )(x, g, target)


HEADS_PER_SLAB = LANE // A_DH
N_SLABS = A_HEADS // HEADS_PER_SLAB
SPAN_BLOCKS = KSPAN // QBLK


def _attn_specs(seg):
    q_spec = pl.BlockSpec((QBLK, LANE), lambda p, m: (m, seg[0] * N_SLABS + p))
    k_specs = [pl.BlockSpec((QBLK, LANE), functools.partial(
        lambda j, p, m: (jnp.maximum(m - (SPAN_BLOCKS - 1) + j, 0), seg[1] * N_SLABS + p), j)) for j in range(SPAN_BLOCKS)]
    v_specs = [pl.BlockSpec((QBLK, LANE), functools.partial(
        lambda j, p, m: (jnp.maximum(m - (SPAN_BLOCKS - 1) + j, 0), seg[2] * N_SLABS + p), j)) for j in range(SPAN_BLOCKS)]
    b_spec = pl.BlockSpec((HEADS_PER_SLAB, QBLK, KSPAN), lambda p, m: (p, 0, 0))
    return q_spec, k_specs, v_specs, b_spec


def _head_lanes(t, hh):
    lane = lax.broadcasted_iota(jnp.int32, t.shape, 1)
    return jnp.where((lane // A_DH) == hh, t, jnp.zeros_like(t))


def _front_mask(m):
    col = lax.broadcasted_iota(jnp.int32, (QBLK, KSPAN), 1)
    return jnp.where(col < (SPAN_BLOCKS - 1 - m) * QBLK, NEG, 0.0)


def _attn_probs(qk, bias, front):
    s = qk * (A_DH ** -0.5) + (bias + front)
    p = jnp.exp(s - jnp.max(s, axis=-1, keepdims=True))
    return p * (1.0 / jnp.sum(p, axis=-1, keepdims=True))


def _grid_ends(nq):
    p, m = pl.program_id(0), pl.program_id(1)
    return (p == 0) & (m == 0), (p == N_SLABS - 1) & (m == nq // 2), (p == N_SLABS - 1) & (m == nq - 1)


def _attn_fwd(proj, big, name, gather=None):
    S = proj.shape[0]
    q_spec, k_specs, v_specs, b_spec = _attn_specs((0, 1, 2))
    shards, layer = gather if gather is not None else ((), None)
    ng = len(shards)

    def body(q_ref, k0, k1, k2, v0, v1, v2, b_ref, *rest):
        srcs, o_ref, gouts, sems = rest[:ng], rest[ng], rest[ng + 1:2 * ng + 1], rest[2 * ng + 1:]
        done = _carry(_gather_phases(layer, srcs, gouts, *sems), *_grid_ends(S // QBLK)) if ng else None
        m = pl.program_id(1)
        q = q_ref[...].astype(BF16)
        k = jnp.concatenate([k0[...], k1[...], k2[...]], axis=0).astype(BF16)
        v = jnp.concatenate([v0[...], v1[...], v2[...]], axis=0).astype(BF16)
        front = _front_mask(m)
        heads = range(HEADS_PER_SLAB)
        scores = [_nt(_head_lanes(q, hh), k) for hh in heads]
        probs = [_attn_probs(scores[hh], b_ref[hh], front).astype(BF16) for hh in heads]
        outs = [_nn(probs[hh], v) for hh in heads]
        lane = lax.broadcasted_iota(jnp.int32, (QBLK, LANE), 1)
        o_ref[...] = jnp.where(lane < A_DH, outs[0], outs[1]).astype(BF16)
        if ng:
            done()

    res = pl.pallas_call(
        body, name=name, grid=(N_SLABS, S // QBLK), in_specs=[q_spec] + k_specs + v_specs + [b_spec] + [HBM] * ng,
        out_specs=[pl.BlockSpec((QBLK, LANE), lambda p, m: (m, p))] + [HBM] * ng,
        out_shape=[jax.ShapeDtypeStruct((S, A_HEADS * A_DH), BF16)] + _gather_out_shapes(shards),
        scratch_shapes=_gather_scratch(ng) if ng else [], compiler_params=_arb(2),
    )(proj, proj, proj, proj, proj, proj, proj, big, *shards)
    return res[0], list(res[1:])


def _attn_bwd(proj, big, dya, name, exchange=()):
    S = proj.shape[0]
    W = A_HEADS * A_DH
    q_spec, k_specs, v_specs, b_spec = _attn_specs((0, 1, 2))
    out_q = pl.BlockSpec((QBLK, LANE), lambda p, m: (m, p))
    out_kv = pl.BlockSpec((S, LANE), lambda p, m: (0, p))
    ne = len(exchange)

    def body(q_ref, k0, k1, k2, v0, v1, v2, b_ref, do_ref, *rest):
        srcs, (dq_ref, dk_ref, dv_ref, db_ref), eouts, sems = rest[:ne], rest[ne:ne + 4], rest[ne + 4:2 * ne + 4], rest[2 * ne + 4:]
        done = _carry(_chips_phases(srcs, eouts, *sems), *_grid_ends(S // QBLK)) if ne else None
        m = pl.program_id(1)

        @pl.when(m == 0)
        def _():
            dk_ref[...] = jnp.zeros_like(dk_ref)
            dv_ref[...] = jnp.zeros_like(dv_ref)
            db_ref[...] = jnp.zeros_like(db_ref)

        q = q_ref[...].astype(BF16)
        k = jnp.concatenate([k0[...], k1[...], k2[...]], axis=0).astype(BF16)
        v = jnp.concatenate([v0[...], v1[...], v2[...]], axis=0).astype(BF16)
        do = do_ref[...]
        front = _front_mask(m)
        heads = range(HEADS_PER_SLAB)
        qh = [_head_lanes(q, hh) for hh in heads]
        doh = [_head_lanes(do, hh) for hh in heads]
        scores = [_nt(qh[hh], k) for hh in heads]
        dps = [_nt(doh[hh], v) for hh in heads]
        ps = [_attn_probs(scores[hh], b_ref[hh], front) for hh in heads]
        dss = [ps[hh] * (dps[hh] - jnp.sum(ps[hh] * dps[hh], axis=-1, keepdims=True)) for hh in heads]
        for hh in heads:
            db_ref[hh] += dss[hh]
        dsb = [(dss[hh] * (A_DH ** -0.5)).astype(BF16) for hh in heads]
        dqs = [_nn(dsb[hh], k) for hh in heads]
        dk = sum(_tn(dsb[hh], qh[hh]) for hh in heads)
        dv = sum(_tn(ps[hh].astype(BF16), doh[hh]) for hh in heads)
        lane = lax.broadcasted_iota(jnp.int32, (QBLK, LANE), 1)
        dq_ref[...] = jnp.where(lane < A_DH, dqs[0], dqs[1])
        for j in range(SPAN_BLOCKS):
            blk = m - (SPAN_BLOCKS - 1) + j

            @pl.when(blk >= 0)
            def _():
                off = pl.multiple_of(blk * QBLK, QBLK)
                dk_ref[pl.ds(off, QBLK), :] += dk[j * QBLK:(j + 1) * QBLK]
                dv_ref[pl.ds(off, QBLK), :] += dv[j * QBLK:(j + 1) * QBLK]
        if ne:
            done()

    res = pl.pallas_call(
        body, name=name, grid=(N_SLABS, S // QBLK),
        in_specs=[q_spec] + k_specs + v_specs + [b_spec, pl.BlockSpec((QBLK, LANE), lambda p, m: (m, p))] + [HBM] * ne,
        out_specs=[out_q, out_kv, out_kv, b_spec] + [HBM] * ne,
        out_shape=[jax.ShapeDtypeStruct((S, W), F32)] * 3 + [jax.ShapeDtypeStruct((A_HEADS, QBLK, KSPAN), F32)]
        + _chips_out_shapes(exchange),
        scratch_shapes=_chips_scratch(ne) if ne else [], compiler_params=_arb(2),
    )(proj, proj, proj, proj, proj, proj, proj, big, dya, *exchange)
    return tuple(res[:4]) + (list(res[4:]),)


NREL_PAD = 3 * LANE
SKEW_W = 1024


def _rel_table_grad(dbig, name):
    H, R, C = dbig.shape

    def body(d_ref, o_ref):
        x = jnp.concatenate([d_ref[...], jnp.zeros((R, SKEW_W - C), F32)], axis=1)
        row = lax.broadcasted_iota(jnp.int32, (R, SKEW_W), 0)
        for b in range(R.bit_length() - 1):
            x = jnp.where(((row >> b) & 1) == 1, pltpu.roll(x, SKEW_W - (1 << b), 1), x)
        e = jnp.sum(x, axis=0, keepdims=True)
        xi = lax.broadcasted_iota(jnp.int32, (SKEW_W, NREL_PAD), 0)
        r = lax.broadcasted_iota(jnp.int32, (SKEW_W, NREL_PAD), 1)
        diag = jnp.where(xi < C, xi, xi - SKEW_W)
        rel = jnp.clip(A_PAST * CHUNK - diag, -A_MAX_REL, A_MAX_REL) + A_MAX_REL
        o_ref[...] = _nn(e, jnp.where(rel == r, 1.0, 0.0).astype(F32), HI)

    return pl.pallas_call(
        body, name=name, grid=(H,), in_specs=[pl.BlockSpec((None, R, C), lambda h: (h, 0, 0))],
        out_specs=pl.BlockSpec((None, 1, NREL_PAD), lambda h: (h, 0, 0)),
        out_shape=jax.ShapeDtypeStruct((H, 1, NREL_PAD), F32), compiler_params=_par(1),
    )(dbig)


def _chunk_cumsum_matrix(n, reverse):
    j = lax.broadcasted_iota(jnp.int32, (n, n), 0)
    i = lax.broadcasted_iota(jnp.int32, (n, n), 1)
    same = (j // CHUNK) == (i // CHUNK)
    return jnp.where(same & ((j >= i) if reverse else (j <= i)), 1.0, 0.0).astype(F32)


def _gdn_gates_fwd(b_t, a_t, alog, dtb, name):
    Hh, S = b_t.shape
    tl = _pick(S, 512)
    row = pl.BlockSpec((Hh, tl), lambda i: (0, i))
    col = pl.BlockSpec((Hh, 1), lambda i: (0, 0))

    def body(b_ref, a_ref, al_ref, dt_ref, beta_ref, gam_ref):
        z = a_ref[...] + dt_ref[...]
        sp = jnp.maximum(z, 0.0) + jnp.log(1.0 + jnp.exp(-jnp.abs(z)))
        g = -jnp.exp(al_ref[...]) * sp
        beta_ref[...] = _sigmoid(b_ref[...])
        gam_ref[...] = _nn(g, _chunk_cumsum_matrix(tl, False), HI)

    return pl.pallas_call(
        body, name=name, grid=(S // tl,), in_specs=[row, row, col, col], out_specs=[row, row],
        out_shape=[jax.ShapeDtypeStruct((Hh, S), F32)] * 2, compiler_params=_par(1),
    )(b_t, a_t, alog, dtb)


def _gdn_gates_bwd(dbeta, dgam_a, dgam_b, b_t, a_t, alog, dtb, name):
    Hh, S = b_t.shape
    tl = _pick(S, 512)
    row = pl.BlockSpec((Hh, tl), lambda i: (0, i))
    col = pl.BlockSpec((Hh, 1), lambda i: (0, 0))
    accs = pl.BlockSpec((Hh, LANE), lambda i: (0, 0))

    def body(dbeta_ref, dga_ref, dgb_ref, b_ref, a_ref, al_ref, dt_ref, db_ref, da_ref, dal_ref, ddt_ref):
        i = pl.program_id(0)
        z = a_ref[...] + dt_ref[...]
        sp = jnp.maximum(z, 0.0) + jnp.log(1.0 + jnp.exp(-jnp.abs(z)))
        ea = jnp.exp(al_ref[...])
        dg = _nn(dga_ref[...] + dgb_ref[...], _chunk_cumsum_matrix(tl, True), HI)
        da = dg * (-ea) * _sigmoid(z)
        beta = _sigmoid(b_ref[...])
        db_ref[...] = dbeta_ref[...] * beta * (1.0 - beta)
        da_ref[...] = da
        _acc(dal_ref, jnp.broadcast_to(jnp.sum(dg * (-ea * sp), axis=1, keepdims=True), (Hh, LANE)), i)
        _acc(ddt_ref, jnp.broadcast_to(jnp.sum(da, axis=1, keepdims=True), (Hh, LANE)), i)

    return pl.pallas_call(
        body, name=name, grid=(S // tl,), in_specs=[row] * 5 + [col, col], out_specs=[row, row, accs, accs],
        out_shape=[jax.ShapeDtypeStruct((Hh, S), F32)] * 2 + [jax.ShapeDtypeStruct((Hh, LANE), F32)] * 2,
        compiler_params=_arb(1),
    )(dbeta, dgam_a, dgam_b, b_t, a_t, alog, dtb)


HALO = 8


def _conv_silu(xx_ref, w_ref, tm):
    y = w_ref[0:1, :] * xx_ref[pl.ds(HALO - CONV_K + 1, tm), :]
    for j in range(1, CONV_K):
        y = y + w_ref[j:j + 1, :] * xx_ref[pl.ds(HALO - CONV_K + 1 + j, tm), :]
    return y, y * _sigmoid(y)


def _fill_prev_halo(xx_ref, x_ref, prev_ref, i, tm):
    xx_ref[pl.ds(HALO, tm), :] = x_ref[...]

    @pl.when(i == 0)
    def _():
        xx_ref[pl.ds(0, HALO), :] = jnp.zeros((HALO, xx_ref.shape[1]), F32)

    @pl.when(i != 0)
    def _():
        xx_ref[pl.ds(0, HALO), :] = prev_ref[...]


def _gdn_pre_specs(tm, C, colblk):
    cur = pl.BlockSpec((tm, C), lambda i: (i, colblk))
    prev = pl.BlockSpec((HALO, C), lambda i: (jnp.maximum(i * (tm // HALO) - 1, 0), colblk))
    return cur, prev


def _gdn_pre_fwd(proj, wconv, name):
    S = proj.shape[0]
    C = 3 * B_HEADS * B_DH
    W = B_HEADS * B_DH
    tm = _pick(S, 256, 8)
    cur, prev = _gdn_pre_specs(tm, C, P_QKVB // C)

    def body(x_ref, prev_ref, w_ref, q_ref, k_ref, v_ref, xx_ref):
        i = pl.program_id(0)
        _fill_prev_halo(xx_ref, x_ref, prev_ref, i, tm)
        _, sl = _conv_silu(xx_ref, w_ref, tm)
        for h in range(B_HEADS):
            hs = slice(h * B_DH, (h + 1) * B_DH)
            q = sl[:, h * B_DH:(h + 1) * B_DH]
            k = sl[:, W + h * B_DH:W + (h + 1) * B_DH]
            q_ref[:, hs] = q * (lax.rsqrt(jnp.sum(q * q, axis=-1, keepdims=True) + EPS) * (B_DH ** -0.5))
            k_ref[:, hs] = k * lax.rsqrt(jnp.sum(k * k, axis=-1, keepdims=True) + EPS)
        v_ref[...] = sl[:, 2 * W:]

    return pl.pallas_call(
        body, name=name, grid=(S // tm,), in_specs=[cur, prev, pl.BlockSpec((CONV_K, C), lambda i: (0, 0))],
        out_specs=[_rows(tm, W)] * 3, out_shape=[jax.ShapeDtypeStruct((S, W), F32)] * 3,
        scratch_shapes=[pltpu.VMEM((HALO + tm, C), F32)], compiler_params=_par(1),
    )(proj, proj, wconv)


def _gdn_pre_bwd_a(proj, wconv, dqn, dkn, dv, name):
    S = proj.shape[0]
    C = 3 * B_HEADS * B_DH
    W = B_HEADS * B_DH
    tm = _pick(S, 256, 8)
    cur, prev = _gdn_pre_specs(tm, C, P_QKVB // C)

    def body(x_ref, prev_ref, w_ref, dq_ref, dk_ref, dv_ref, dy_ref, xx_ref):
        i = pl.program_id(0)
        _fill_prev_halo(xx_ref, x_ref, prev_ref, i, tm)
        y, sl = _conv_silu(xx_ref, w_ref, tm)
        sg = _sigmoid(y)
        dsilu = sg * (1.0 + y * (1.0 - sg))
        for h in range(B_HEADS):
            for base, d_ref, c in ((0, dq_ref, B_DH ** -0.5), (W, dk_ref, 1.0)):
                lo = base + h * B_DH
                t = sl[:, lo:lo + B_DH]
                d = d_ref[:, h * B_DH:(h + 1) * B_DH]
                r = lax.rsqrt(jnp.sum(t * t, axis=-1, keepdims=True) + EPS)
                dt = (c * r) * (d - t * (r * r) * jnp.sum(d * t, axis=-1, keepdims=True))
                dy_ref[:, lo:lo + B_DH] = dt * dsilu[:, lo:lo + B_DH]
        dy_ref[:, 2 * W:] = dv_ref[...] * dsilu[:, 2 * W:]

    return pl.pallas_call(
        body, name=name, grid=(S // tm,),
        in_specs=[cur, prev, pl.BlockSpec((CONV_K, C), lambda i: (0, 0))] + [_rows(tm, W)] * 3,
        out_specs=_rows(tm, C), out_shape=jax.ShapeDtypeStruct((S, C), F32),
        scratch_shapes=[pltpu.VMEM((HALO + tm, C), F32)], compiler_params=_par(1),
    )(proj, proj, wconv, dqn, dkn, dv)


def _gdn_pre_bwd_b(proj, wconv, dy, name):
    S = proj.shape[0]
    C = 3 * B_HEADS * B_DH
    tm = _pick(S, 256, 8)
    nt_ = S // tm
    cur, prev = _gdn_pre_specs(tm, C, P_QKVB // C)
    nxt = pl.BlockSpec((HALO, C), lambda i: (jnp.minimum((i + 1) * (tm // HALO), S // HALO - 1), 0))

    def body(x_ref, prev_ref, w_ref, dy_ref, next_ref, dx_ref, dw_ref, xx_ref, dd_ref):
        i = pl.program_id(0)
        _fill_prev_halo(xx_ref, x_ref, prev_ref, i, tm)
        dyv = dy_ref[...]
        dd_ref[pl.ds(0, tm), :] = dyv

        @pl.when(i == nt_ - 1)
        def _():
            dd_ref[pl.ds(tm, HALO), :] = jnp.zeros((HALO, C), F32)

        @pl.when(i != nt_ - 1)
        def _():
            dd_ref[pl.ds(tm, HALO), :] = next_ref[...]

        dx = w_ref[0:1, :] * dd_ref[pl.ds(CONV_K - 1, tm), :]
        for j in range(1, CONV_K):
            dx = dx + w_ref[j:j + 1, :] * dd_ref[pl.ds(CONV_K - 1 - j, tm), :]
        dx_ref[...] = dx.astype(BF16)
        dw = jnp.concatenate(
            [jnp.sum(dyv * xx_ref[pl.ds(HALO - CONV_K + 1 + j, tm), :], axis=0, keepdims=True) for j in range(CONV_K)],
            axis=0)
        _acc(dw_ref, dw, i)

    return pl.pallas_call(
        body, name=name, grid=(nt_,),
        in_specs=[cur, prev, pl.BlockSpec((CONV_K, C), lambda i: (0, 0)), _rows(tm, C), nxt],
        out_specs=[_rows(tm, C), pl.BlockSpec((CONV_K, C), lambda i: (0, 0))],
        out_shape=[jax.ShapeDtypeStruct((S, C), BF16), jax.ShapeDtypeStruct((CONV_K, C), F32)],
        scratch_shapes=[pltpu.VMEM((HALO + tm, C), F32), pltpu.VMEM((tm + HALO, C), F32)],
        compiler_params=_arb(1),
    )(proj, proj, wconv, dy, dy)


def _chunk_masks():
    row = lax.broadcasted_iota(jnp.int32, (CHUNK, CHUNK), 0)
    col = lax.broadcasted_iota(jnp.int32, (CHUNK, CHUNK), 1)
    return row >= col, row > col


def _chunk_local(q, k, vv, bc, gc, gr, tri):
    dm = jnp.where(tri, jnp.exp(jnp.where(tri, gc - gr, 0.0)), 0.0)
    kk = _bnt(k, k)
    glast = gr[..., CHUNK - 1:CHUNK]
    ep = jnp.exp(gc)
    em = jnp.exp(glast - gc)
    el = jnp.exp(glast)
    return dm, kk, ep, em, el, vv * bc, k * (bc * ep)


def _unit_lower_inverse(low):
    row = lax.broadcasted_iota(jnp.int32, (CHUNK, CHUNK), 0)
    col = lax.broadcasted_iota(jnp.int32, (CHUNK, CHUNK), 1)
    p = -low
    t = jnp.where(row == col, 1.0, 0.0).astype(F32) + p
    steps = CHUNK.bit_length() - 2
    for _ in range(steps):
        p = _nn(p, p, SOLVE_PREC)
        t = t + _nn(t, p, SOLVE_PREC)
    return t


GROUP = 4


LATE = 7


def _carry(phases, first, middle, last):
    if len(phases) == 3:
        pl.when(first)(phases[0])
        pl.when(middle)(phases[1])
        return lambda: pl.when(last)(phases[2])
    pl.when(first)(phases[0])
    return lambda: pl.when(last)(phases[1])


def _pairs(nchunks):
    return [(c, h) for c in range(nchunks) for h in range(B_HEADS)]


def _tok(c):
    return slice(c * CHUNK, (c + 1) * CHUNK)


def _head(h):
    return slice(h * B_DH, (h + 1) * B_DH)


def _stack_tokens(ref, nchunks):
    return jnp.stack([ref[_tok(c), _head(h)] for c, h in _pairs(nchunks)])


def _stack_cols(ref, nchunks):
    per_chunk = [ref[c] for c in range(nchunks)] if len(ref.shape) == 3 else [ref[...]]
    return jnp.stack([per_chunk[c][:, h:h + 1] for c, h in _pairs(nchunks)])


def _stack_rows(ref, nchunks):
    if len(ref.shape) == 3:
        return jnp.stack([ref[c, h:h + 1, :] for c, h in _pairs(nchunks)])
    return jnp.stack([ref[h:h + 1, :] for _, h in _pairs(1)])


def _gdn_group_specs(ng_steps, W):
    tok = pl.BlockSpec((GROUP * CHUNK, W), lambda i: (i, 0))
    colv = pl.BlockSpec((GROUP, CHUNK, B_HEADS), lambda i: (i, 0, 0))
    rowv = pl.BlockSpec((GROUP, B_HEADS, CHUNK), lambda i: (i, 0, 0))
    mat = pl.BlockSpec((GROUP, B_HEADS, CHUNK, CHUNK), lambda i: (i, 0, 0, 0))
    return tok, colv, rowv, mat


def _gdn_local_fwd(qn, kn, v, bcol, gcol, grow, name, gather=None):
    S, Wd = qn.shape
    nc = S // CHUNK
    steps = nc // GROUP
    tok, colv, rowv, mat = _gdn_group_specs(steps, Wd)
    shards, layer = gather if gather is not None else ((), None)
    ng = len(shards)

    def body(q_ref, k_ref, v_ref, bc_ref, gc_ref, gr_ref, *rest):
        srcs, (t_ref, a_ref, u_ref, w_ref), gouts, sems = rest[:ng], rest[ng:ng + 4], rest[ng + 4:2 * ng + 4], rest[2 * ng + 4:]
        i = pl.program_id(0)
        done = _carry(_gather_phases(layer, srcs, gouts, *sems), i == 0, i == LATE * steps // 8, i == steps - 1) if ng else None
        tri, strict = _chunk_masks()
        q, k, vv = (_stack_tokens(r, GROUP) for r in (q_ref, k_ref, v_ref))
        bc, gc, gr = _stack_cols(bc_ref, GROUP), _stack_cols(gc_ref, GROUP), _stack_rows(gr_ref, GROUP)
        dm, kk, ep, em, el, vb, kb = _chunk_local(q, k, vv, bc, gc, gr, tri)
        t = _unit_lower_inverse(jnp.where(strict, bc * kk * dm, 0.0))
        a = _bnt(q, k) * dm
        u = _nn(t, vb, SOLVE_PREC)
        w = _nn(t, kb, SOLVE_PREC)
        for n, (c, h) in enumerate(_pairs(GROUP)):
            t_ref[c, h] = t[n]
            a_ref[c, h] = a[n]
            u_ref[_tok(c), _head(h)] = u[n]
            w_ref[_tok(c), _head(h)] = w[n]
        if ng:
            done()

    res = pl.pallas_call(
        body, name=name, grid=(steps,), in_specs=[tok, tok, tok, colv, colv, rowv] + [HBM] * ng,
        out_specs=[mat, mat, tok, tok] + [HBM] * ng,
        out_shape=[jax.ShapeDtypeStruct((nc, B_HEADS, CHUNK, CHUNK), F32)] * 2 + [jax.ShapeDtypeStruct((S, Wd), F32)] * 2
        + _gather_out_shapes(shards),
        scratch_shapes=_gather_scratch(ng) if ng else [], compiler_params=_arb(1),
    )(qn, kn, v, bcol, gcol, grow, *shards)
    return res[0], res[1], res[2], res[3], list(res[4:])


def _scan_decays(gc, gr):
    glast = gr[..., CHUNK - 1:CHUNK]
    return jnp.exp(gc), jnp.exp(glast - gc), jnp.exp(glast)


def _gdn_scan_specs(nc, rev):
    idx = (lambda i: nc - 1 - i) if rev else (lambda i: i)
    W = B_HEADS * B_DH
    tok = pl.BlockSpec((CHUNK, W), lambda i: (idx(i), 0))
    colv = pl.BlockSpec((None, CHUNK, B_HEADS), lambda i: (idx(i), 0, 0))
    rowv = pl.BlockSpec((None, B_HEADS, CHUNK), lambda i: (idx(i), 0, 0))
    mat = pl.BlockSpec((None, B_HEADS, CHUNK, CHUNK), lambda i: (idx(i), 0, 0, 0))
    smat = pl.BlockSpec((None, B_HEADS, B_DH, B_DH), lambda i: (idx(i), 0, 0, 0))
    return tok, colv, rowv, mat, smat


def _gdn_scan_fwd(qn, kn, u, w, a, gcol, grow, name, gather=None):
    S, Wd = qn.shape
    nc = S // CHUNK
    tok, colv, rowv, mat, smat = _gdn_scan_specs(nc, False)
    shards, layer = gather if gather is not None else ((), None)
    ng = len(shards)

    def body(q_ref, k_ref, u_ref, w_ref, a_ref, gc_ref, gr_ref, *rest):
        srcs, (o_ref, sh_ref), gouts = rest[:ng], rest[ng:ng + 2], rest[ng + 2:2 * ng + 2]
        st_ref, sems = rest[2 * ng + 2], rest[2 * ng + 3:]
        i = pl.program_id(0)
        done = _carry(_gather_phases(layer, srcs, gouts, *sems), i == 0, i == LATE * nc // 8, i == nc - 1) if ng else None

        @pl.when(i == 0)
        def _():
            st_ref[...] = jnp.zeros_like(st_ref)

        ep, em, el = _scan_decays(_stack_cols(gc_ref, 1), _stack_rows(gr_ref, 1))
        q, k, u, w = (_stack_tokens(r, 1) for r in (q_ref, k_ref, u_ref, w_ref))
        s0 = st_ref[...]
        ut = u - _bnn(w, s0)
        o = _bnn(q * ep, s0) + _bnn(a_ref[...], ut)
        st_ref[...] = el * s0 + _btn(k * em, ut)
        sh_ref[...] = s0
        for h in range(B_HEADS):
            o_ref[:, _head(h)] = o[h]
        if ng:
            done()

    res = pl.pallas_call(
        body, name=name, grid=(nc,), in_specs=[tok, tok, tok, tok, mat, colv, rowv] + [HBM] * ng,
        out_specs=[tok, smat] + [HBM] * ng,
        out_shape=[jax.ShapeDtypeStruct((S, Wd), F32), jax.ShapeDtypeStruct((nc, B_HEADS, B_DH, B_DH), F32)]
        + _gather_out_shapes(shards),
        scratch_shapes=[pltpu.VMEM((B_HEADS, B_DH, B_DH), F32)] + (_gather_scratch(ng) if ng else []),
        compiler_params=_arb(1),
    )(qn, kn, u, w, a, gcol, grow, *shards)
    return res[0], res[1], list(res[2:])


def _gdn_scan_bwd(qn, kn, u, w, a, gcol, grow, ssave, do, name):
    S, Wd = qn.shape
    nc = S // CHUNK
    tok, colv, rowv, mat, smat = _gdn_scan_specs(nc, True)

    def body(q_ref, k_ref, u_ref, w_ref, a_ref, gc_ref, gr_ref, sh_ref, do_ref,
             du_ref, dw_ref, dqd_ref, dkd_ref, da_ref, dgl_ref, ds_ref):
        i = pl.program_id(0)

        @pl.when(i == 0)
        def _():
            ds_ref[...] = jnp.zeros_like(ds_ref)

        tri, _ = _chunk_masks()
        sub4 = lax.broadcasted_iota(jnp.int32, (B_HEADS, CHUNK), 0)
        lane_last = lax.broadcasted_iota(jnp.int32, (1, CHUNK), 1) == CHUNK - 1
        ep, em, el = _scan_decays(_stack_cols(gc_ref, 1), _stack_rows(gr_ref, 1))
        q, k, u, w, dout = (_stack_tokens(r, 1) for r in (q_ref, k_ref, u_ref, w_ref, do_ref))
        s0 = sh_ref[...]
        ds = ds_ref[...]
        ut = u - _bnn(w, s0)
        dut = _btn(a_ref[...], dout) + _bnn(k * em, ds)
        ds_ref[...] = el * ds + _btn(q * ep, dout) - _btn(w, dut)
        dw = -_bnt(dut, s0)
        dqd = _bnt(dout, s0)
        dkd = _bnt(ut, ds)
        da_ref[...] = jnp.where(tri, _bnt(dout, ut), 0.0)
        d_el = jnp.sum(jnp.sum(s0 * ds, axis=-1, keepdims=True), axis=-2, keepdims=True)
        last = d_el * el
        dgl_acc = jnp.zeros((B_HEADS, CHUNK), F32)
        for h in range(B_HEADS):
            du_ref[:, _head(h)] = dut[h]
            dw_ref[:, _head(h)] = dw[h]
            dqd_ref[:, _head(h)] = dqd[h]
            dkd_ref[:, _head(h)] = dkd[h]
            dgl_acc = jnp.where(sub4 == h, jnp.where(lane_last, last[h], 0.0), dgl_acc)
        dgl_ref[...] = dgl_acc

    return pl.pallas_call(
        body, name=name, grid=(nc,), in_specs=[tok, tok, tok, tok, mat, colv, rowv, smat, tok],
        out_specs=[tok, tok, tok, tok, mat, rowv],
        out_shape=[jax.ShapeDtypeStruct((S, Wd), F32)] * 4 + [jax.ShapeDtypeStruct((nc, B_HEADS, CHUNK, CHUNK), F32),
                                                             jax.ShapeDtypeStruct((nc, B_HEADS, CHUNK), F32)],
        scratch_shapes=[pltpu.VMEM((B_HEADS, B_DH, B_DH), F32)], compiler_params=_arb(1),
    )(qn, kn, u, w, a, gcol, grow, ssave, do)


def _gdn_local_bwd(qn, kn, v, bcol, gcol, grow, tsave, du, dw, dqd, dkd, da, dgl, name, exchange=()):
    S, Wd = qn.shape
    nc = S // CHUNK
    steps = nc // GROUP
    tok, colv, rowv, mat = _gdn_group_specs(steps, Wd)
    ne = len(exchange)

    def body(q_ref, k_ref, v_ref, bc_ref, gc_ref, gr_ref, t_ref, du_ref, dw_ref, dqd_ref, dkd_ref, da_ref, dgl_ref, *rest):
        srcs, (dq_ref, dk_ref, dv_ref, dbc_ref, dgc_ref, dgr_ref) = rest[:ne], rest[ne:ne + 6]
        eouts, sems = rest[ne + 6:2 * ne + 6], rest[2 * ne + 6:]
        i = pl.program_id(0)
        done = _carry(_chips_phases(srcs, eouts, *sems), i == 0, None, i == steps - 1) if ne else None
        tri, strict = _chunk_masks()
        lane4 = lax.broadcasted_iota(jnp.int32, (CHUNK, B_HEADS), 1)
        sub4 = lax.broadcasted_iota(jnp.int32, (B_HEADS, CHUNK), 0)
        lane_last = lax.broadcasted_iota(jnp.int32, (1, CHUNK), 1) == CHUNK - 1
        q, k, vv, dut, dwv, dqd, dkd = (_stack_tokens(r, GROUP)
                                        for r in (q_ref, k_ref, v_ref, du_ref, dw_ref, dqd_ref, dkd_ref))
        bc, gc, gr = _stack_cols(bc_ref, GROUP), _stack_cols(gc_ref, GROUP), _stack_rows(gr_ref, GROUP)
        dm, kk, ep, em, el, vb, kb = _chunk_local(q, k, vv, bc, gc, gr, tri)
        t = jnp.stack([t_ref[c, h] for c, h in _pairs(GROUP)])
        dav = jnp.stack([da_ref[c, h] for c, h in _pairs(GROUP)])
        qk = _bnt(q, k)
        dt = _nt(dut, vb, SOLVE_PREC) + _nt(dwv, kb, SOLVE_PREC)
        dvb = _tn(t, dut, SOLVE_PREC)
        dkb = _tn(t, dwv, SOLVE_PREC)
        dl = jnp.where(strict, -_tn(t, _nt(dt, t, SOLVE_PREC), SOLVE_PREC), 0.0)
        g1 = dl * dm
        dkb_k = jnp.sum(dkb * k, axis=-1, keepdims=True)
        dbeta = jnp.sum(g1 * kk, axis=-1, keepdims=True) + jnp.sum(dvb * vv, axis=-1, keepdims=True) + dkb_k * ep
        dkk = g1 * bc
        ddm = dl * (bc * kk) + dav * qk
        dqk = dav * dm
        dq = _bnn(dqk, k) + dqd * ep
        dk = _btn(dqk, q) + _bnn(dkk, k) + _btn(dkk, k) + dkb * (bc * ep) + dkd * em
        dv = dvb * bc
        dep = dkb_k * bc + jnp.sum(dqd * q, axis=-1, keepdims=True)
        dem = jnp.sum(dkd * k, axis=-1, keepdims=True)
        mm = ddm * dm
        dgam_c = jnp.sum(mm, axis=-1, keepdims=True) + dep * ep - dem * em
        dglast = jnp.sum(dem * em, axis=-2, keepdims=True)
        dgam_r = -jnp.sum(mm, axis=-2, keepdims=True) + jnp.where(lane_last, dglast, 0.0)
        for c in range(GROUP):
            dbc_acc = jnp.zeros((CHUNK, B_HEADS), F32)
            dgc_acc = jnp.zeros((CHUNK, B_HEADS), F32)
            dgr_acc = jnp.zeros((B_HEADS, CHUNK), F32)
            for h in range(B_HEADS):
                n = c * B_HEADS + h
                dq_ref[_tok(c), _head(h)] = dq[n]
                dk_ref[_tok(c), _head(h)] = dk[n]
                dv_ref[_tok(c), _head(h)] = dv[n]
                dbc_acc = jnp.where(lane4 == h, dbeta[n], dbc_acc)
                dgc_acc = jnp.where(lane4 == h, dgam_c[n], dgc_acc)
                dgr_acc = jnp.where(sub4 == h, dgam_r[n], dgr_acc)
            dbc_ref[c] = dbc_acc
            dgc_ref[c] = dgc_acc
            dgr_ref[c] = dgr_acc + dgl_ref[c]
        if ne:
            done()

    res = pl.pallas_call(
        body, name=name, grid=(steps,),
        in_specs=[tok, tok, tok, colv, colv, rowv, mat, tok, tok, tok, tok, mat, rowv] + [HBM] * ne,
        out_specs=[tok, tok, tok, colv, colv, rowv] + [HBM] * ne,
        out_shape=[jax.ShapeDtypeStruct((S, Wd), F32)] * 3
        + [jax.ShapeDtypeStruct((nc, CHUNK, B_HEADS), F32)] * 2 + [jax.ShapeDtypeStruct((nc, B_HEADS, CHUNK), F32)]
        + _chips_out_shapes(exchange),
        scratch_shapes=_chips_scratch(ne) if ne else [], compiler_params=_arb(1),
    )(qn, kn, v, bcol, gcol, grow, tsave, du, dw, dqd, dkd, da, dgl, *exchange)
    return tuple(res[:6]) + (list(res[6:]),)


def _gdn_post_fwd(o, proj, ng, name):
    S, W = o.shape
    tm = _pick(S, 512, 8)

    def body(o_ref, z_ref, g_ref, y_ref):
        gv = g_ref[...]
        for h in range(B_HEADS):
            hs = slice(h * B_DH, (h + 1) * B_DH)
            oh = o_ref[:, hs]
            z = z_ref[:, hs]
            r = lax.rsqrt(jnp.mean(oh * oh, axis=-1, keepdims=True) + EPS)
            y_ref[:, hs] = (oh * r * gv * (z * _sigmoid(z))).astype(BF16)

    return pl.pallas_call(
        body, name=name, grid=(S // tm,), in_specs=[_rows(tm, W), _rows(tm, W, P_Z // W), _vec(B_DH)],
        out_specs=_rows(tm, W), out_shape=jax.ShapeDtypeStruct((S, W), BF16), compiler_params=_par(1),
    )(o, proj, ng)


def _gdn_post_bwd(dy, o, proj, ng, name):
    S, W = o.shape
    tm = _pick(S, 512, 8)

    def body(dy_ref, o_ref, z_ref, g_ref, do_ref, dz_ref, dg_ref):
        i = pl.program_id(0)
        gv = g_ref[...]
        dg = jnp.zeros((1, B_DH), F32)
        for h in range(B_HEADS):
            hs = slice(h * B_DH, (h + 1) * B_DH)
            oh = o_ref[:, hs]
            z = z_ref[:, hs]
            d = dy_ref[:, hs]
            r = lax.rsqrt(jnp.mean(oh * oh, axis=-1, keepdims=True) + EPS)
            n = oh * r
            sg = _sigmoid(z)
            sz = z * sg
            dn = d * gv * sz
            dg = dg + jnp.sum(d * n * sz, axis=0, keepdims=True)
            dz_ref[:, hs] = (d * n * gv * (sg * (1.0 + z * (1.0 - sg)))).astype(BF16)
            do_ref[:, hs] = r * (dn - n * jnp.mean(dn * n, axis=-1, keepdims=True))
        _acc(dg_ref, dg, i)

    return pl.pallas_call(
        body, name=name, grid=(S // tm,), in_specs=[_rows(tm, W), _rows(tm, W), _rows(tm, W, P_Z // W), _vec(B_DH)],
        out_specs=[_rows(tm, W), _rows(tm, W), _vec(B_DH)],
        out_shape=[jax.ShapeDtypeStruct((S, W), F32), jax.ShapeDtypeStruct((S, W), BF16),
                   jax.ShapeDtypeStruct((1, B_DH), F32)],
        compiler_params=_arb(1),
    )(dy, o, proj, ng)


def _ada_mod(c_all, w_ada, b_shard, name):
    L, D, Ns = w_ada.shape
    B = c_all.shape[0]

    def body(c_ref, w_ref, b_ref, o_ref):
        cv = c_ref[...]
        cond = (cv * _sigmoid(cv)).astype(BF16)
        o_ref[...] = _nn(cond, w_ref[...].astype(BF16)) + b_ref[...]

    return pl.pallas_call(
        body, name=name, grid=(L,),
        in_specs=[pl.BlockSpec((B, D), lambda l: (0, 0)), pl.BlockSpec((None, D, Ns), lambda l: (l, 0, 0)),
                  pl.BlockSpec((None, 1, Ns), lambda l: (l, 0, 0))],
        out_specs=pl.BlockSpec((None, B, Ns), lambda l: (l, 0, 0)),
        out_shape=jax.ShapeDtypeStruct((L, B, Ns), F32), compiler_params=_par(1),
    )(c_all, w_ada, b_shard)


def _ada_wgrad(c_all, dmod, name):
    L, B, Ns = dmod.shape
    D = c_all.shape[1]

    def body(c_ref, d_ref, o_ref):
        cv = c_ref[...]
        cond = (cv * _sigmoid(cv)).astype(BF16)
        o_ref[...] = _tn(cond, d_ref[...].astype(BF16))

    return pl.pallas_call(
        body, name=name, grid=(L,),
        in_specs=[pl.BlockSpec((B, D), lambda l: (0, 0)), pl.BlockSpec((None, B, Ns), lambda l: (l, 0, 0))],
        out_specs=pl.BlockSpec((None, D, Ns), lambda l: (l, 0, 0)),
        out_shape=jax.ShapeDtypeStruct((L, D, Ns), F32), compiler_params=_par(1),
    )(c_all, dmod)


W_IN_PIECES = ((0, 0, 1410), (1, 0, 1410), (2, 0, 252), (2, 772, 638), (3, 0, 1410), (2, 252, 512), (2, 764, 8))


def _reorder_w_in(w4, name):
    L, _, D, Cs = w4.shape
    tm = _pick(D, 256, 16)
    used = sum(p[2] for p in W_IN_PIECES)

    def body(w_ref, o_ref):
        shard = [w_ref[s] for s in range(4)]
        parts = [shard[s][:, lo:lo + n] for s, lo, n in W_IN_PIECES]
        o_ref[...] = jnp.concatenate(parts + [jnp.zeros((tm, P_END - used), w4.dtype)], axis=1)

    return pl.pallas_call(
        body, name=name, grid=(L, D // tm), in_specs=[pl.BlockSpec((None, 4, tm, Cs), lambda l, i: (l, 0, i, 0))],
        out_specs=pl.BlockSpec((None, tm, P_END), lambda l, i: (l, i, 0)),
        out_shape=jax.ShapeDtypeStruct((L, D, P_END), w4.dtype), compiler_params=_par(2),
    )(w4)


def _restore_w_in(g, name):
    D = g.shape[0]
    tm = _pick(D, 256, 16)

    def body(g_ref, o_ref, ob_ref):
        gv = g_ref[...]
        off = 0
        pieces = {}
        for s, lo, n in W_IN_PIECES:
            pieces.setdefault(s, []).append((lo, gv[:, off:off + n]))
            off += n
        for s, lst in pieces.items():
            lst.sort(key=lambda t: t[0])
            shard = lst[0][1] if len(lst) == 1 else jnp.concatenate([t[1] for t in lst], axis=1)
            o_ref[s] = shard
            ob_ref[s] = shard.astype(BF16)

    spec = pl.BlockSpec((4, tm, W_IN_SHARD), lambda i: (0, i, 0))
    return pl.pallas_call(
        body, name=name, grid=(D // tm,), in_specs=[pl.BlockSpec((tm, P_END), lambda i: (i, 0))],
        out_specs=[spec, spec],
        out_shape=[jax.ShapeDtypeStruct((4, D, W_IN_SHARD), g.dtype), jax.ShapeDtypeStruct((4, D, W_IN_SHARD), BF16)],
        compiler_params=_par(1),
    )(g)


def _adam_update(w, g, m, v):
    mn = ADAM_B1 * m + (1.0 - ADAM_B1) * g
    vn = ADAM_B2 * v + (1.0 - ADAM_B2) * (g * g)
    m_hat = mn / (1.0 - ADAM_B1 ** ADAM_STEP)
    v_hat = vn / (1.0 - ADAM_B2 ** ADAM_STEP)
    return -ADAM_LR * (m_hat / (jnp.sqrt(v_hat) + ADAM_EPS) + ADAM_WD * w), mn, vn


def _adamw(w, g, m, v, name):
    shape = w.shape
    C = shape[-1]
    R = w.size // C
    tm = _pick(R, 512, 8)
    spec = pl.BlockSpec((tm, C), lambda i: (i, 0))

    def body(w_ref, g_ref, m_ref, v_ref, d_ref, mo_ref, vo_ref):
        d_ref[...], mo_ref[...], vo_ref[...] = _adam_update(w_ref[...], g_ref[...], m_ref[...], v_ref[...])

    outs = pl.pallas_call(
        body, name=name, grid=(R // tm,), in_specs=[spec] * 4, out_specs=[spec] * 3,
        out_shape=[jax.ShapeDtypeStruct((R, C), F32)] * 3, compiler_params=_par(1),
    )(*(t.reshape(R, C) for t in (w, g, m, v)))
    return tuple(o.reshape(shape) for o in outs)


def _adamw_lead(w, g, m, v, name, tl):
    A, B, C = w.shape
    spec = pl.BlockSpec((tl, B, C), lambda i: (i, 0, 0))

    def body(w_ref, g_ref, m_ref, v_ref, d_ref, mo_ref, vo_ref):
        d_ref[...], mo_ref[...], vo_ref[...] = _adam_update(w_ref[...], g_ref[...], m_ref[...], v_ref[...])

    return pl.pallas_call(
        body, name=name, grid=(A // tl,), in_specs=[spec] * 4, out_specs=[spec] * 3,
        out_shape=[jax.ShapeDtypeStruct((A, B, C), F32)] * 3, compiler_params=_par(1),
    )(w, g, m, v)


def _adamw_layers(w, gs, m, v, name):
    L, R, C = w.shape
    tm = _pick(R, 128, 8)
    spec = pl.BlockSpec((None, tm, C), lambda l, i: (l, i, 0))
    g_specs = [pl.BlockSpec((tm, C), functools.partial(lambda ll, l, i: (jnp.where(l == ll, i, 0), 0), ll))
               for ll in range(L)]

    def body(w_ref, m_ref, v_ref, *rest):
        g_refs, (go_ref, d_ref, mo_ref, vo_ref) = rest[:L], rest[L:]
        l = pl.program_id(0)
        for ll in range(L):
            @pl.when(l == ll)
            def _():
                g = g_refs[ll][...]
                go_ref[...] = g
                d_ref[...], mo_ref[...], vo_ref[...] = _adam_update(w_ref[...], g, m_ref[...], v_ref[...])

    return pl.pallas_call(
        body, name=name, grid=(L, R // tm), in_specs=[spec] * 3 + g_specs, out_specs=[spec] * 4,
        out_shape=[jax.ShapeDtypeStruct((L, R, C), F32)] * 4, compiler_params=_arb(2),
    )(w, m, v, *gs)


def _pair_sums(a, where, b, name):
    NB, _, R, C = a.shape

    def body(where_ref, a_ref, b_ref, p_ref, own_ref):
        s = a_ref[...] + b_ref[...].astype(F32)
        p_ref[...] = s.astype(BF16)

        @pl.when(pl.program_id(0) == where_ref[1])
        def _():
            own_ref[...] = s

    return pl.pallas_call(
        body, name=name,
        grid_spec=pltpu.PrefetchScalarGridSpec(
            num_scalar_prefetch=1, grid=(NB,),
            in_specs=[pl.BlockSpec((None, None, R, C), lambda k, w: (k, w[0], 0, 0)),
                      pl.BlockSpec((None, R, C), lambda k, w: (k, 0, 0))],
            out_specs=[pl.BlockSpec((None, R, C), lambda k, w: (k, 0, 0)), pl.BlockSpec((R, C), lambda k, w: (0, 0))]),
        out_shape=[jax.ShapeDtypeStruct((NB, R, C), BF16), jax.ShapeDtypeStruct((R, C), F32)],
        compiler_params=_arb(1),
    )(where, a, b)


def _sum_own_and_received(own, recv, where, name):
    R, C = own.shape
    tm = _pick(R, 256, 16)

    def body(where_ref, p_ref, r_ref, o_ref):
        o_ref[...] = ((p_ref[...] + r_ref[0].astype(F32)) + r_ref[1].astype(F32)) + r_ref[2].astype(F32)

    return pl.pallas_call(
        body, name=name,
        grid_spec=pltpu.PrefetchScalarGridSpec(
            num_scalar_prefetch=1, grid=(R // tm,),
            in_specs=[pl.BlockSpec((tm, C), lambda i, w: (i, 0)), pl.BlockSpec((3, tm, C), lambda i, w: (0, i, 0))],
            out_specs=pl.BlockSpec((None, tm, C), lambda i, w: (w[0], i, 0))),
        out_shape=jax.ShapeDtypeStruct((2, R, C), F32), compiler_params=_par(1),
    )(where, own, recv)


def _position():
    return lax.axis_index("x"), lax.axis_index("y"), lax.axis_index("c")


def _other_chips(x, y):
    return [(1 - x, y), (x, 1 - y), (1 - x, 1 - y)]


HBM = pl.BlockSpec(memory_space=pl.ANY)


def _allgather8(blk, name, reduce_rows=None):
    M, N = blk.shape

    def body(x_ref, out_ref, *rest):
        if reduce_rows is None:
            send_sems, recv_sems, local_sem = rest
        else:
            sum_ref, send_sems, recv_sems, local_sem = rest
        x, y, c = _position()
        me, sibling = (x, y, c), (x, y, 1 - c)
        chips = _other_chips(x, y)

        def rows(px, py, pc):
            return out_ref.at[pl.ds((4 * px + 2 * py + pc) * M, M), :]

        def copy(k, block, to, src=None):
            return pltpu.make_async_remote_copy(
                src_ref=rows(*block) if src is None else src, dst_ref=rows(*block),
                send_sem=send_sems.at[k], recv_sem=recv_sems.at[k], device_id=to, device_id_type=MESH)

        mine = pltpu.make_async_copy(x_ref, rows(*me), local_sem)
        mine.start()
        first = [copy(0, me, sibling, src=x_ref)]
        first += [copy(1 + j, me, (*chip, c), src=x_ref) for j, chip in enumerate(chips)]
        for cp in first:
            cp.start()
        passed = [copy(4 + j, (*chip, c), sibling) for j, chip in enumerate(chips)]
        for j, chip in enumerate(chips):
            copy(1 + j, (*chip, c), me).wait_recv()
            passed[j].start()
        copy(0, sibling, me).wait_recv()
        for j, chip in enumerate(chips):
            copy(4 + j, (*chip, 1 - c), me).wait_recv()
        for cp in first + passed:
            cp.wait_send()
        mine.wait()
        if reduce_rows is not None:
            tot = out_ref[pl.ds(0, reduce_rows), :]
            for d in range(1, 8):
                tot = tot + out_ref[pl.ds(d * M, reduce_rows), :]
            sum_ref[...] = tot

    vmem = pl.BlockSpec(memory_space=pltpu.VMEM)
    out_shape = [jax.ShapeDtypeStruct((8 * M, N), blk.dtype)]
    if reduce_rows is not None:
        out_shape.append(jax.ShapeDtypeStruct((reduce_rows, N), blk.dtype))
    res = pl.pallas_call(
        body, name=name, out_shape=out_shape, in_specs=[vmem], out_specs=[vmem] * len(out_shape),
        scratch_shapes=[pltpu.SemaphoreType.DMA((7,)), pltpu.SemaphoreType.DMA((7,)), pltpu.SemaphoreType.DMA],
    )(blk)
    return res[0] if reduce_rows is None else (res[0], res[1])


def _gather_phases(layer, srcs, outs, send_sems, recv_sems, local_sems):
    n = len(srcs)
    x, y, c = _position()
    me, sibling = (x, y, c), (x, y, 1 - c)
    chips = _other_chips(x, y)

    def region(t, px, py, pc):
        return outs[t].at[2 * px + py, pc]

    def copy(t, k, block, to, own=False):
        return pltpu.make_async_remote_copy(
            src_ref=srcs[t].at[layer, c] if own else region(t, *block), dst_ref=region(t, *block),
            send_sem=send_sems.at[7 * t + k], recv_sem=recv_sems.at[7 * t + k], device_id=to, device_id_type=MESH)

    def local(t):
        return pltpu.make_async_copy(srcs[t].at[layer, c], region(t, *me), local_sems.at[t])

    def first(t):
        return [copy(t, 0, me, sibling, own=True)] + [copy(t, 1 + j, me, (*chip, c), own=True)
                                                       for j, chip in enumerate(chips)]

    def start():
        for t in range(n):
            local(t).start()
        for t in range(n):
            for cp in first(t):
                cp.start()

    def forward():
        for j, chip in enumerate(chips):
            for t in range(n):
                copy(t, 1 + j, (*chip, c), me).wait_recv()
                copy(t, 4 + j, (*chip, c), sibling).start()

    def finish():
        for t in range(n):
            copy(t, 0, sibling, me).wait_recv()
        for j, chip in enumerate(chips):
            for t in range(n):
                copy(t, 4 + j, (*chip, 1 - c), me).wait_recv()
        for t in range(n):
            for cp in first(t) + [copy(t, 4 + j, (*chip, c), sibling) for j, chip in enumerate(chips)]:
                cp.wait_send()
            local(t).wait()

    return start, forward, finish


def _gather_scratch(n):
    return [pltpu.SemaphoreType.DMA((7 * n,)), pltpu.SemaphoreType.DMA((7 * n,)), pltpu.SemaphoreType.DMA((n,))]


def _gather_out_shapes(shards):
    return [jax.ShapeDtypeStruct((4,) + s.shape[1:], s.dtype) for s in shards]


def _gather_weights(shards, layer, name):
    n = len(shards)

    def body(*refs):
        start, forward, finish = _gather_phases(layer, refs[:n], refs[n:2 * n], *refs[2 * n:])
        start()
        forward()
        finish()

    return pl.pallas_call(
        body, name=name, out_shape=_gather_out_shapes(shards), in_specs=[HBM] * n, out_specs=[HBM] * n,
        scratch_shapes=_gather_scratch(n),
    )(*shards)


def _rs_sibling(gs, name):
    n = len(gs)

    def body(*refs):
        srcs, outs = refs[:n], refs[n:2 * n]
        send_sems, recv_sems = refs[2 * n:]
        x, y, c = _position()
        copies = [pltpu.make_async_remote_copy(
            src_ref=srcs[t].at[k, 1 - c], dst_ref=outs[t].at[k], send_sem=send_sems.at[4 * t + k],
            recv_sem=recv_sems.at[4 * t + k], device_id=(x, y, 1 - c), device_id_type=MESH)
            for t in range(n) for k in range(4)]
        for cp in copies:
            cp.start()
        for cp in copies:
            cp.wait()

    out_shape = [jax.ShapeDtypeStruct((4,) + g.shape[2:], g.dtype) for g in gs]
    return pl.pallas_call(
        body, name=name, out_shape=out_shape, in_specs=[HBM] * n, out_specs=[HBM] * n,
        scratch_shapes=[pltpu.SemaphoreType.DMA((4 * n,)), pltpu.SemaphoreType.DMA((4 * n,))],
    )(*gs)


def _rs_chips(ps, name):
    n = len(ps)

    def body(*refs):
        start, finish = _chips_phases(refs[:n], refs[n:2 * n], *refs[2 * n:])
        start()
        finish()

    return pl.pallas_call(
        body, name=name, out_shape=_chips_out_shapes(ps), in_specs=[HBM] * n, out_specs=[HBM] * n,
        scratch_shapes=_chips_scratch(n),
    )(*ps)


def _chips_phases(srcs, outs, send_sems, recv_sems):
    x, y, c = _position()
    copies = [pltpu.make_async_remote_copy(
        src_ref=srcs[t].at[2 * px + py], dst_ref=outs[t].at[j], send_sem=send_sems.at[3 * t + j],
        recv_sem=recv_sems.at[3 * t + j], device_id=(px, py, c), device_id_type=MESH)
        for t in range(len(srcs)) for j, (px, py) in enumerate(_other_chips(x, y))]

    def start():
        for cp in copies:
            cp.start()

    def finish():
        for cp in copies:
            cp.wait()

    return start, finish


def _chips_scratch(n):
    return [pltpu.SemaphoreType.DMA((3 * n,)), pltpu.SemaphoreType.DMA((3 * n,))]


def _chips_out_shapes(ps):
    return [jax.ShapeDtypeStruct((3,) + p.shape[1:], p.dtype) for p in ps]


def _rs_pair(hs, name):
    n = len(hs)

    def body(*refs):
        bufs = refs[n:2 * n]
        send_sems, recv_sems = refs[2 * n:]
        x, y, c = _position()

        def copy(t, half):
            return pltpu.make_async_remote_copy(
                src_ref=bufs[t].at[half], dst_ref=bufs[t].at[half], send_sem=send_sems.at[t], recv_sem=recv_sems.at[t],
                device_id=(x, y, 1 - c), device_id_type=MESH)

        for t in range(n):
            copy(t, c).start()
        for t in range(n):
            copy(t, 1 - c).wait_recv()
        for t in range(n):
            copy(t, c).wait_send()

    out_shape = [jax.ShapeDtypeStruct(h.shape, h.dtype) for h in hs]
    return pl.pallas_call(
        body, name=name, out_shape=out_shape, in_specs=[HBM] * n, out_specs=[HBM] * n,
        input_output_aliases={t: t for t in range(n)},
        scratch_shapes=[pltpu.SemaphoreType.DMA((n,)), pltpu.SemaphoreType.DMA((n,))],
    )(*hs)


BIG = ("w_in", "w_branch_a", "w_branch_b", "w_out", "w_ffn_in", "w_ffn_out")
CARRY_ATTN = ["w_in"]
CARRY_LOCAL = ["w_ffn_out"]
CARRY_SCAN = ["w_branch_a", "w_branch_b", "w_out"]
CARRY_GU = ["w_ffn_in"]
CARRY_DATTN = ["w_in", "w_ffn_in"]
CARRY_DLOCAL = ["w_branch_a", "w_branch_b", "w_out", "w_ffn_out"]


def _band_bias(rel_table, name, gather=None):
    L, H, n = rel_table.shape
    tab = jnp.pad(rel_table, ((0, 0), (0, 0), (0, NREL_PAD - n))).reshape(L * H, 1, NREL_PAD)
    band = (A_PAST + 1) * CHUNK

    shards, glayer = gather if gather is not None else ((), None)
    ng = len(shards)

    def body(t_ref, *rest):
        srcs, o_ref, gouts, sems = rest[:ng], rest[ng], rest[ng + 1:2 * ng + 1], rest[2 * ng + 1:]
        i = pl.program_id(0)
        done = (_carry(_gather_phases(glayer, srcs, gouts, *sems), i == 0, i == LATE * (L * H) // 8, i == L * H - 1)
                if ng else None)
        r = lax.broadcasted_iota(jnp.int32, (NREL_PAD, SKEW_W), 0)
        xi = lax.broadcasted_iota(jnp.int32, (NREL_PAD, SKEW_W), 1)
        diag = jnp.where(xi < KSPAN, xi, xi - SKEW_W)
        rel = jnp.clip(A_PAST * CHUNK - diag, -A_MAX_REL, A_MAX_REL) + A_MAX_REL
        e = _nn(t_ref[...], jnp.where(rel == r, 1.0, 0.0).astype(F32), HI)
        x = jnp.broadcast_to(e, (QBLK, SKEW_W))
        row = lax.broadcasted_iota(jnp.int32, (QBLK, SKEW_W), 0)
        for b in range(QBLK.bit_length() - 1):
            x = jnp.where(((row >> b) & 1) == 1, pltpu.roll(x, 1 << b, 1), x)
        x = x[:, :KSPAN]
        first = (lax.broadcasted_iota(jnp.int32, (QBLK, KSPAN), 0) // CHUNK) * CHUNK
        col = lax.broadcasted_iota(jnp.int32, (QBLK, KSPAN), 1)
        o_ref[...] = jnp.where((col >= first) & (col < first + band), x, NEG)
        if ng:
            done()

    res = pl.pallas_call(
        body, name=name, grid=(L * H,), in_specs=[pl.BlockSpec((None, 1, NREL_PAD), lambda i: (i, 0, 0))] + [HBM] * ng,
        out_specs=[pl.BlockSpec((None, QBLK, KSPAN), lambda i: (i, 0, 0))] + [HBM] * ng,
        out_shape=[jax.ShapeDtypeStruct((L * H, QBLK, KSPAN), F32)] + _gather_out_shapes(shards),
        scratch_shapes=_gather_scratch(ng) if ng else [], compiler_params=_arb(1),
    )(tab, *shards)
    return res[0].reshape(L, H, QBLK, KSPAN), list(res[1:])


def _col_row_forms(t, S):
    nc = S // CHUNK
    return t.T.reshape(nc, CHUNK, B_HEADS), t.reshape(B_HEADS, nc, CHUNK).transpose(1, 0, 2)


def _weight_view(name, gathered, tag):
    if name in ("w_out", "w_ffn_out"):
        return gathered.reshape(8 * gathered.shape[2], gathered.shape[3])
    stacked = gathered.reshape(4, 2 * gathered.shape[2], gathered.shape[3])
    return _reorder_w_in(stacked[None], f"w_in_cols_{tag}")[0] if name == "w_in" else stacked


def _layer_fwd(l, x, mod, W, P, big, gather=None, late=None):
    S, D = x.shape
    n = lambda s: f"{s}_l{l}"
    sh1, sc1, gt1, sh2, sc2, gt2 = (mod[i:i + 1] for i in range(6))
    h1 = _lnmod_fwd(x, P["norm1_g"][l:l + 1], sc1, sh1, n("ln1"))
    if late is None:
        proj = _matmul(h1, W["w_in"], "nn", F32, n("proj"), tn=1152)
    else:
        proj, got = _matmul(h1, W["w_in"], "nn", F32, n("proj"), tn=1152, gather=(late[1], l))
        W = {**W, **{k: _weight_view(k, t, f"l{l}") for k, t in zip(late[0], got)}}
    part = (lambda names: ([gather[0][BIG.index(k)] for k in names], gather[1])) if gather is not None else (lambda names: None)
    ya, got_a = _attn_fwd(proj, big, n("attn"), part(CARRY_ATTN))
    ba = proj[:, P_BA:P_BA + 2 * B_HEADS]
    b_t, a_t = ba[:, :B_HEADS].T, ba[:, B_HEADS:].T
    alog, dtb = P["a_log"][l].reshape(B_HEADS, 1), P["dt_bias"][l].reshape(B_HEADS, 1)
    beta, gam = _gdn_gates_fwd(b_t, a_t, alog, dtb, n("gates"))
    bcol, _ = _col_row_forms(beta, S)
    gcol, grow = _col_row_forms(gam, S)
    qn, kn, v = _gdn_pre_fwd(proj, P["w_conv"][l], n("gdnpre"))
    tsave, amat, u, w, got_l = _gdn_local_fwd(qn, kn, v, bcol, gcol, grow, n("gdnlocal"), part(CARRY_LOCAL))
    o, ssave, got_s = _gdn_scan_fwd(qn, kn, u, w, amat, gcol, grow, n("gdnscan"), part(CARRY_SCAN))
    yb = _gdn_post_fwd(o, proj, P["gdn_norm_g"][l:l + 1], n("gdnpost"))
    pa = _matmul(ya, W["w_branch_a"], "nn", BF16, n("pa"), tm=2048, stacked=True)
    pb = _matmul(yb, W["w_branch_b"], "nn", BF16, n("pb"), tm=2048, stacked=True)
    merged = _merge_fwd(proj, pa, pb, n("merge"))
    ao = _matmul(merged, W["w_out"], "nn", F32, n("ao"))
    x1 = _gate_fwd(x, ao, gt1, n("res1"))
    h2 = _lnmod_fwd(x1, P["norm2_g"][l:l + 1], sc2, sh2, n("ln2"))
    gu = _matmul(h2, W["w_ffn_in"], "nn", BF16, n("gu"), stacked=True, gather=part(CARRY_GU))
    gu, got_g = gu if gather is not None else (gu, [])
    got = dict(zip(CARRY_ATTN + CARRY_LOCAL + CARRY_SCAN + CARRY_GU, got_a + got_l + got_s + got_g))
    gathered = {k: got[k] for k in BIG} if gather is not None else None
    act = _ffn_act_fwd(gu, n("act"))
    fo = _matmul(act, W["w_ffn_out"], "nn", F32, n("fo"), tk=1408)
    x2 = _gate_fwd(x1, fo, gt2, n("res2"))
    saved = dict(x=x, h1=h1, proj=proj, ya=ya, b_t=b_t, a_t=a_t, bcol=bcol, gcol=gcol, grow=grow,
                 qn=qn, kn=kn, v=v, o=o, tsave=tsave, ssave=ssave, amat=amat, u=u, w=w, yb=yb, pa=pa, pb=pb,
                 merged=merged, ao=ao, x1=x1,
                 h2=h2, gu=gu, act=act, fo=fo)
    return x2, saved, gathered, W


def _layer_bwd(l, dx2, sv, mod, W, P, big, exchange=()):
    S, D = dx2.shape
    n = lambda s: f"{s}_l{l}"
    sh1, sc1, gt1, sh2, sc2, gt2 = (mod[i:i + 1] for i in range(6))
    g, pay = {}, {}
    view = lambda t: t.reshape((4, 2, t.shape[-2] // (2 if t.ndim == 3 else 8), t.shape[-1]))
    dz2, dgt2 = _gate_bwd(dx2, sv["fo"], gt2, n("dres2"))
    g["w_ffn_out"], pay["w_ffn_out"] = map(view, _matmul(sv["act"], dz2, "tn", F32, n("dwfo"), tm=1408, also_bf16=True))
    dact = _matmul(dz2, W["w_ffn_out"], "nt", BF16, n("dact"), tn=1408)
    dgu = _ffn_act_bwd(sv["gu"], dact, n("dgu"))
    g["w_ffn_in"], pay["w_ffn_in"] = map(view, _matmul(sv["h2"], dgu, "tn", F32, n("dwfi"), out_stacked=True,
                                                       also_bf16=True))
    dh2 = _matmul(dgu, W["w_ffn_in"], "nt", F32, n("dh2"), stacked=True)
    dx1, dsh2, dsc2, dn2 = _lnmod_bwd(dh2, sv["x1"], P["norm2_g"][l:l + 1], sc2, dx2, n("dln2"))
    dz1, dgt1 = _gate_bwd(dx1, sv["ao"], gt1, n("dres1"))
    g["w_out"], pay["w_out"] = map(view, _matmul(sv["merged"], dz1, "tn", F32, n("dwo"), also_bf16=True))
    dmerged = _matmul(dz1, W["w_out"], "nt", BF16, n("dmerged"))
    dgab, dpa, dpb = _merge_bwd(sv["proj"], sv["pa"], sv["pb"], dmerged, n("dmerge"))
    g["w_branch_a"], pay["w_branch_a"] = map(view, _matmul(sv["ya"], dpa, "tn", F32, n("dwa"), out_stacked=True,
                                                           also_bf16=True))
    g["w_branch_b"], pay["w_branch_b"] = map(view, _matmul(sv["yb"], dpb, "tn", F32, n("dwb"), out_stacked=True,
                                                           also_bf16=True))
    dya = _matmul(dpa, W["w_branch_a"], "nt", BF16, n("dya"), tm=2048, stacked=True)
    dyb = _matmul(dpb, W["w_branch_b"], "nt", F32, n("dyb"), tm=2048, stacked=True)
    ex = (lambda names: [exchange[BIG.index(k)] for k in names]) if len(exchange) else (lambda names: ())
    dq, dk, dv, dbig, rec_a = _attn_bwd(sv["proj"], big, dya, n("dattn"), ex(CARRY_DATTN))
    g["rel_table"] = _rel_table_grad(dbig, n("drel"))[:, 0, :2 * A_MAX_REL + 1]
    do, dzb, dng = _gdn_post_bwd(dyb, sv["o"], sv["proj"], P["gdn_norm_g"][l:l + 1], n("dgdnpost"))
    g["gdn_norm_g"] = dng[0]
    du, dw, dqd, dkd, da, dgl = _gdn_scan_bwd(sv["qn"], sv["kn"], sv["u"], sv["w"], sv["amat"], sv["gcol"], sv["grow"],
                                              sv["ssave"], do, n("dgdnscan"))
    dqn, dkn, dvv, dbc, dgc, dgr, rec_l = _gdn_local_bwd(
        sv["qn"], sv["kn"], sv["v"], sv["bcol"], sv["gcol"], sv["grow"], sv["tsave"], du, dw, dqd, dkd, da, dgl,
        n("dgdnlocal"), ex(CARRY_DLOCAL))
    rec = dict(zip(CARRY_DATTN + CARRY_DLOCAL, rec_a + rec_l))
    received = [rec[k] for k in BIG] if len(exchange) else None
    dbeta_t = dbc.reshape(S, B_HEADS).T
    dgam_a = dgc.reshape(S, B_HEADS).T
    dgam_b = dgr.transpose(1, 0, 2).reshape(B_HEADS, S)
    alog, dtb = P["a_log"][l].reshape(B_HEADS, 1), P["dt_bias"][l].reshape(B_HEADS, 1)
    db_t, da_t, dal, ddt = _gdn_gates_bwd(dbeta_t, dgam_a, dgam_b, sv["b_t"], sv["a_t"], alog, dtb, n("dgates"))
    g["a_log"], g["dt_bias"] = dal[:, 0], ddt[:, 0]
    dy = _gdn_pre_bwd_a(sv["proj"], P["w_conv"][l], dqn, dkn, dvv, n("dgdnpre_a"))
    dqkvb, g["w_conv"] = _gdn_pre_bwd_b(sv["proj"], P["w_conv"][l], dy, n("dgdnpre_b"))
    dba = jnp.concatenate([db_t.T, da_t.T, jnp.zeros((S, P_END - P_BA - 2 * B_HEADS), F32)], axis=1)
    dproj = jnp.concatenate([dq.astype(BF16), dk.astype(BF16), dv.astype(BF16), dqkvb, dgab, dzb, dba.astype(BF16)],
                            axis=1)
    g["w_in"], pay["w_in"] = map(view, _restore_w_in(_matmul(sv["h1"], dproj, "tn", F32, n("dwin"), tn=1152),
                                                     n("dwin_cols")))
    dh1 = _matmul(dproj, W["w_in"], "nt", F32, n("dh1"), tk=1152)
    dx, dsh1, dsc1, dn1 = _lnmod_bwd(dh1, sv["x"], P["norm1_g"][l:l + 1], sc1, dx1, n("dln1"))
    g["norm1_g"], g["norm2_g"] = dn1[0], dn2[0]
    dmod = jnp.concatenate([dsh1, dsc1, dgt1, dsh2, dsc2, dgt2], axis=1)[0]
    return dx, g, pay, dmod, received


SMALL = ("norm1_g", "norm2_g", "rel_table", "w_conv", "a_log", "dt_bias", "gdn_norm_g")
SMALL_PACK_C = 1024


def _as_rows(t):
    flat = t.reshape(-1)
    rows = -(-flat.shape[0] // SMALL_PACK_C)
    return jnp.pad(flat, (0, rows * SMALL_PACK_C - flat.shape[0])).reshape(rows, SMALL_PACK_C)


def _pack_rows(parts):
    blk = jnp.concatenate([_as_rows(p) for p in parts], axis=0)
    return jnp.pad(blk, ((0, -blk.shape[0] % 8), (0, 0)))


def _unpack_rows(blk, shapes):
    out, r = [], 0
    for shp in shapes:
        size = int(np.prod(shp))
        rows = -(-size // SMALL_PACK_C)
        out.append(blk[..., r:r + rows, :].reshape(blk.shape[:-2] + (rows * SMALL_PACK_C,))[..., :size]
                   .reshape(blk.shape[:-2] + tuple(shp)))
        r += rows
    return out


def kernel(x, c, w_ada, b_ada, norm1_g, norm2_g, w_in, rel_table, w_conv, a_log, dt_bias, gdn_norm_g, w_branch_a, w_branch_b, w_out, w_ffn_in, w_ffn_out, final_g, loss_target, m_w_ada, m_b_ada, m_norm1_g, m_norm2_g, m_w_in, m_rel_table, m_w_conv, m_a_log, m_dt_bias, m_gdn_norm_g, m_w_branch_a, m_w_branch_b, m_w_out, m_w_ffn_in, m_w_ffn_out, m_final_g, v_w_ada, v_b_ada, v_norm1_g, v_norm2_g, v_w_in, v_rel_table, v_w_conv, v_a_log, v_dt_bias, v_gdn_norm_g, v_w_branch_a, v_w_branch_b, v_w_out, v_w_ffn_in, v_w_ffn_out, v_final_g):
    weights = dict(w_ada=w_ada, b_ada=b_ada, norm1_g=norm1_g, norm2_g=norm2_g, w_in=w_in, rel_table=rel_table,
                   w_conv=w_conv, a_log=a_log, dt_bias=dt_bias, gdn_norm_g=gdn_norm_g, w_branch_a=w_branch_a,
                   w_branch_b=w_branch_b, w_out=w_out, w_ffn_in=w_ffn_in, w_ffn_out=w_ffn_out, final_g=final_g)
    mom_m = dict(w_ada=m_w_ada, b_ada=m_b_ada, norm1_g=m_norm1_g, norm2_g=m_norm2_g, w_in=m_w_in,
                 rel_table=m_rel_table, w_conv=m_w_conv, a_log=m_a_log, dt_bias=m_dt_bias, gdn_norm_g=m_gdn_norm_g,
                 w_branch_a=m_w_branch_a, w_branch_b=m_w_branch_b, w_out=m_w_out, w_ffn_in=m_w_ffn_in,
                 w_ffn_out=m_w_ffn_out, final_g=m_final_g)
    mom_v = dict(w_ada=v_w_ada, b_ada=v_b_ada, norm1_g=v_norm1_g, norm2_g=v_norm2_g, w_in=v_w_in,
                 rel_table=v_rel_table, w_conv=v_w_conv, a_log=v_a_log, dt_bias=v_dt_bias, gdn_norm_g=v_gdn_norm_g,
                 w_branch_a=v_w_branch_a, w_branch_b=v_w_branch_b, w_out=v_w_out, w_ffn_in=v_w_ffn_in,
                 w_ffn_out=v_w_ffn_out, final_g=v_final_g)
    xi, yi, ci = _position()
    chip = 2 * xi + yi
    dev = 2 * chip + ci
    L, D = norm1_g.shape
    NMOD = b_ada.shape[1] // D
    ns = w_ada.shape[2]
    cs = w_conv.shape[2]

    first_blk = _pack_rows([c, w_conv])
    first_all = _allgather8(first_blk, "gather_c").reshape(8, first_blk.shape[0], SMALL_PACK_C)
    c_all, w_conv_all = _unpack_rows(first_all, [(D,), w_conv.shape])
    w_conv_full = w_conv_all.reshape(4, 2, L, CONV_K, cs)[:, 0].transpose(1, 2, 0, 3).reshape(L, CONV_K, 4 * cs)
    b_shard = lax.dynamic_slice_in_dim(b_ada, chip * ns, ns, axis=1).reshape(L, 1, ns)
    mod_shard = _ada_mod(c_all, w_ada, b_shard, "ada_mod")
    mod_all = _allgather8(mod_shard.reshape(L * 8, ns), "gather_mod").reshape(4, 2, L, 8, ns)
    mod = lax.dynamic_index_in_dim(mod_all[:, 0], dev, axis=2, keepdims=False)
    mod = mod.transpose(1, 0, 2).reshape(L, NMOD, D)

    shards = [weights[k].astype(BF16) for k in BIG]
    shards = [s.reshape(s.shape[0], 2, s.shape[1] // 2, s.shape[2]) for s in shards]
    P = dict(norm1_g=norm1_g, norm2_g=norm2_g, w_conv=w_conv_full, a_log=a_log, dt_bias=dt_bias,
             gdn_norm_g=gdn_norm_g)
    shard_of = dict(zip(BIG, shards))

    big, got = _band_bias(rel_table, "band_bias", ([shard_of["w_in"]], 0))
    alone = ["w_branch_a", "w_branch_b", "w_out", "w_ffn_out"]
    got += _gather_weights([shard_of[k] for k in alone], 0, "gather_weights_l0")
    W = [{k: _weight_view(k, t, "l0") for k, t in zip(["w_in"] + alone, got)}]
    late = (["w_ffn_in"], [shard_of["w_ffn_in"]])
    xc = x[0]
    saved = []
    for l in range(L):
        xc, sv, gathered, W[l] = _layer_fwd(l, xc, mod[l], W[l], P, big[l], (shards, l + 1) if l + 1 < L else None,
                                           late if l == 0 else None)
        saved.append(sv)
        if l + 1 < L:
            W.append({k: _weight_view(k, gathered[k], f"l{l + 1}") for k in BIG})
    dx, loss_dev, dfinal = _loss_head(xc, final_g.reshape(1, D), loss_target[0], "loss_head")

    where = jnp.stack([ci, chip]).astype(jnp.int32)
    grads = [None] * L
    dmods = [None] * L
    shard_grads = {k: [None] * L for k in BIG}

    def finish_reduce_scatter(l, sums, from_chips):
        halves = [_sum_own_and_received(s_[1], r_, where, f"rs_sum_{k}_l{l}")
                  for k, s_, r_ in zip(BIG, sums, from_chips)]
        for k, t in zip(BIG, _rs_pair(halves, f"rs_pair_l{l}")):
            shard_grads[k][l] = t.reshape(2 * t.shape[1], t.shape[2])

    pending = None
    for l in reversed(range(L)):
        exchange = [s_[0] for s_ in pending] if pending is not None else ()
        dx, grads[l], pay, dmods[l], received = _layer_bwd(l, dx, saved[l], mod[l], W[l], P, big[l], exchange)
        if pending is not None:
            finish_reduce_scatter(l + 1, pending, received)
        gs = [grads[l][k] for k in BIG]
        from_sibling = _rs_sibling([pay[k] for k in BIG], f"rs_sibling_l{l}")
        pending = [_pair_sums(g_, where, r_, f"rs_pair_sum_{k}_l{l}") for k, g_, r_ in zip(BIG, gs, from_sibling)]
    finish_reduce_scatter(0, pending, _rs_chips([s_[0] for s_ in pending], "rs_chips_l0"))
    dmod = jnp.stack(dmods)

    small = {k: jnp.stack([grads[l][k] for l in range(L)]) for k in SMALL}
    parts = [dmod] + [small[k] for k in SMALL] + [dfinal, loss_dev[0, :1]]
    small_blk = _pack_rows(parts)
    srows = small_blk.shape[0]
    small_all, small_sum = _allgather8(small_blk, "gather_small", reduce_rows=srows)
    shapes = [dmod.shape] + [small[k].shape for k in SMALL] + [(D,), (1,)]
    tot = _unpack_rows(small_sum, shapes)
    G = dict(zip(SMALL, tot[1:1 + len(SMALL)]))
    G["b_ada"] = tot[0].reshape(b_ada.shape)
    G["w_conv"] = lax.dynamic_slice_in_dim(G["w_conv"], chip * cs, cs, axis=2)
    G["final_g"] = tot[-2]
    loss = tot[-1][0]
    dmod_all = _unpack_rows(small_all.reshape(8, srows, SMALL_PACK_C), [dmod.shape])[0]
    dmod_cols = lax.dynamic_slice_in_dim(dmod_all, chip * ns, ns, axis=2).transpose(1, 0, 2)
    G["w_ada"] = _ada_wgrad(c_all, dmod_cols, "ada_wgrad")

    order = ["w_ada", "b_ada", "norm1_g", "norm2_g", "w_in", "rel_table", "w_conv", "a_log", "dt_bias", "gdn_norm_g",
             "w_branch_a", "w_branch_b", "w_out", "w_ffn_in", "w_ffn_out", "final_g"]
    deltas, new_m, new_v = {}, {}, {}
    for k in order:
        w = weights[k]
        if k == "w_in":
            to_cols = lambda t: jnp.transpose(t, (2, 0, 1))
            from_cols = lambda t: jnp.transpose(t, (1, 2, 0))
            gt = to_cols(jnp.stack(shard_grads[k]))
            d_, m_, v_ = _adamw_lead(to_cols(w), gt, to_cols(mom_m[k]), to_cols(mom_v[k]), f"adamw_{k}",
                                     W_IN_SHARD // 30)
            G[k], deltas[k], new_m[k], new_v[k] = from_cols(gt), from_cols(d_), from_cols(m_), from_cols(v_)
            continue
        if k in BIG:
            G[k], deltas[k], new_m[k], new_v[k] = _adamw_layers(w, shard_grads[k], mom_m[k], mom_v[k], f"adamw_{k}")
            continue
        as2d = (lambda t: t.reshape(1, -1)) if w.ndim == 1 else (lambda t: t)
        d_, m_, v_ = _adamw(as2d(w), as2d(G[k]), as2d(mom_m[k]), as2d(mom_v[k]), f"adamw_{k}")
        deltas[k], new_m[k], new_v[k] = d_.reshape(w.shape), m_.reshape(w.shape), v_.reshape(w.shape)
    return (loss, dx[None], *[G[k] for k in order], *[deltas[k] for k in order], *[new_m[k] for k in order],
            *[new_v[k] for k in order])
```

```python
import functools

import numpy as np
import jax
import jax.numpy as jnp
from jax import lax
from jax.experimental import pallas as pl
from jax.experimental.pallas import tpu as pltpu

F32 = jnp.float32
BF16 = jnp.bfloat16
HI = lax.Precision.HIGHEST
SOLVE_PREC = lax.Precision.HIGH
MESH = pl.DeviceIdType.MESH

EPS = 1e-6
CHUNK = 64
A_HEADS = 8
A_DH = 64
A_PAST = 8
A_MAX_REL = 128
B_HEADS = 4
B_DH = 128
CONV_K = 4
LANE = 128
QBLK = 4 * CHUNK
KSPAN = QBLK + A_PAST * CHUNK
NEG = -1e30

ADAM_LR = 0.001
ADAM_B1 = 0.9
ADAM_B2 = 0.999
ADAM_EPS = 1e-08
ADAM_WD = 0.01
ADAM_STEP = 10

P_QKVA, P_QKVB, P_GA, P_GB, P_Z, P_BA, P_END = 0, 1536, 3072, 4096, 5120, 5632, 5760
W_IN_SHARD = 1410


def _sigmoid(x):
    return 1.0 / (1.0 + jnp.exp(-x))


def _dot(a, b, ca, cb, prec):
    lead = a.ndim - 2
    batch = ((0,), (0,)) if lead else ((), ())
    return lax.dot_general(a, b, (((ca + lead,), (cb + lead,)), batch), precision=prec, preferred_element_type=F32)


def _nn(a, b, prec=None):
    return _dot(a, b, 1, 0, prec)


def _nt(a, b, prec=None):
    return _dot(a, b, 1, 1, prec)


def _tn(a, b, prec=None):
    return _dot(a, b, 0, 0, prec)


def _bnn(a, b):
    return _nn(a.astype(BF16), b.astype(BF16))


def _bnt(a, b):
    return _nt(a.astype(BF16), b.astype(BF16))


def _btn(a, b):
    return _tn(a.astype(BF16), b.astype(BF16))


def _pick(n, target, unit=LANE):
    best = None
    for t in range(unit, min(n, target) + 1, unit):
        if n % t == 0:
            best = t
    return best if best is not None else n


def _acc(ref, val, i):
    @pl.when(i == 0)
    def _():
        ref[...] = val

    @pl.when(i != 0)
    def _():
        ref[...] += val


def _arb(n):
    return pltpu.CompilerParams(dimension_semantics=("arbitrary",) * n)


def _par(n):
    return pltpu.CompilerParams(dimension_semantics=("parallel",) * n)


def _matmul(a, b, mode, out_dtype, name, tm=1024, tn=1024, tk=1024, layer=None, stacked=False, out_stacked=False,
            also_bf16=False, gather=None):
    bs = b.shape[1:] if layer is not None else b.shape
    if mode == "nn":
        M, K = a.shape
        N = 4 * bs[2] if stacked else bs[1]
        if stacked:
            tn = bs[2]
    elif mode == "nt":
        M, K = a.shape
        N = bs[1] if stacked else bs[0]
        if stacked:
            tk = bs[2]
    else:
        K, M = a.shape
        N = bs[1]
        if out_stacked:
            tn = N // 4
    tm, tn, tk = _pick(M, tm), _pick(N, tn), _pick(K, tk)
    nk = K // tk
    lead = () if layer is None else (layer,)
    lead_blk = () if layer is None else (None,)
    if mode == "nn":
        a_spec = pl.BlockSpec((tm, tk), lambda i, j, k: (i, k))
        if stacked:
            b_spec = pl.BlockSpec(lead_blk + (None, tk, tn), lambda i, j, k: lead + (j, k, 0))
        else:
            b_spec = pl.BlockSpec(lead_blk + (tk, tn), lambda i, j, k: lead + (k, j))
        dot = _nn
    elif mode == "nt":
        a_spec = pl.BlockSpec((tm, tk), lambda i, j, k: (i, k))
        if stacked:
            b_spec = pl.BlockSpec(lead_blk + (None, tn, tk), lambda i, j, k: lead + (k, j, 0))
        else:
            b_spec = pl.BlockSpec(lead_blk + (tn, tk), lambda i, j, k: lead + (j, k))
        dot = _nt
    else:
        a_spec = pl.BlockSpec((tk, tm), lambda i, j, k: (k, i))
        b_spec = pl.BlockSpec((tk, tn), lambda i, j, k: (k, j))
        dot = _tn
    if out_stacked:
        o_spec = pl.BlockSpec((None, tm, tn), lambda i, j, k: (j, i, 0))
        o_shape = jax.ShapeDtypeStruct((4, M, tn), out_dtype)
    else:
        o_spec = pl.BlockSpec((tm, tn), lambda i, j, k: (i, j))
        o_shape = jax.ShapeDtypeStruct((M, N), out_dtype)

    shards, glayer = gather if gather is not None else ((), None)
    ng = len(shards)
    o_shapes = [o_shape] + ([jax.ShapeDtypeStruct(o_shape.shape, BF16)] if also_bf16 else [])
    no = len(o_shapes)
    gi, gj = M // tm, N // tn

    def write(o_refs, val):
        for o_ref in o_refs:
            o_ref[...] = val.astype(o_ref.dtype)

    def body(a_ref, b_ref, *refs):
        srcs, o_refs, gouts, scratch = refs[:ng], refs[ng:ng + no], refs[ng + no:2 * ng + no], refs[2 * ng + no:]
        i, j, k = pl.program_id(0), pl.program_id(1), pl.program_id(2)
        if ng:
            start = (j == 0) & (k == 0)
            done = _carry(_gather_phases(glayer, srcs, gouts, *scratch[-3:]), (i == 0) & start, (i == gi - 1) & start,
                          (i == gi - 1) & (j == gj - 1) & (k == nk - 1))
        if nk == 1:
            write(o_refs, dot(a_ref[...], b_ref[...]))
        else:
            acc_ref = scratch[0]

            @pl.when(k == 0)
            def _():
                acc_ref[...] = jnp.zeros_like(acc_ref)

            acc_ref[...] += dot(a_ref[...], b_ref[...])

            @pl.when(k == nk - 1)
            def _():
                write(o_refs, acc_ref[...])
        if ng:
            done()

    sem = ("arbitrary",) * 3 if ng else ("parallel", "parallel", "arbitrary")
    res = pl.pallas_call(
        body, name=name, grid=(gi, gj, nk), in_specs=[a_spec, b_spec] + [HBM] * ng,
        out_specs=[o_spec] * no + [HBM] * ng, out_shape=o_shapes + _gather_out_shapes(shards),
        scratch_shapes=([] if nk == 1 else [pltpu.VMEM((tm, tn), F32)]) + (_gather_scratch(ng) if ng else []),
        compiler_params=pltpu.CompilerParams(dimension_semantics=sem),
    )(a, b, *shards)
    out = tuple(res[:no]) if also_bf16 else res[0]
    return (out, list(res[no:])) if ng else out


def _rows(tm, n, col=0):
    return pl.BlockSpec((tm, n), lambda i: (i, col))


def _vec(n):
    return pl.BlockSpec((1, n), lambda i: (0, 0))


def _lnmod_fwd(x, g, sc, sh, name):
    S, D = x.shape
    tm = _pick(S, 512, 8)

    def body(x_ref, g_ref, sc_ref, sh_ref, o_ref):
        xv = x_ref[...]
        r = lax.rsqrt(jnp.mean(xv * xv, axis=-1, keepdims=True) + EPS)
        o_ref[...] = ((xv * r * g_ref[...]) * (1.0 + sc_ref[...]) + sh_ref[...]).astype(BF16)

    return pl.pallas_call(
        body, name=name, grid=(S // tm,),
        in_specs=[_rows(tm, D), _vec(D), _vec(D), _vec(D)], out_specs=_rows(tm, D),
        out_shape=jax.ShapeDtypeStruct((S, D), BF16), compiler_params=_par(1),
    )(x, g, sc, sh)


def _lnmod_bwd(dh, x, g, sc, dres, name):
    S, D = x.shape
    tm = _pick(S, 512, 8)

    def body(dh_ref, x_ref, g_ref, sc_ref, dres_ref, dx_ref, dsh_ref, dsc_ref, dg_ref):
        i = pl.program_id(0)
        xv = x_ref[...]
        dh_ = dh_ref[...]
        r = lax.rsqrt(jnp.mean(xv * xv, axis=-1, keepdims=True) + EPS)
        xhat = xv * r
        gv = g_ref[...]
        dn = dh_ * (1.0 + sc_ref[...])
        dxhat = dn * gv
        dx_ref[...] = dres_ref[...] + r * (dxhat - xhat * jnp.mean(dxhat * xhat, axis=-1, keepdims=True))
        _acc(dsh_ref, jnp.sum(dh_, axis=0, keepdims=True), i)
        _acc(dsc_ref, jnp.sum(dh_ * (xhat * gv), axis=0, keepdims=True), i)
        _acc(dg_ref, jnp.sum(dn * xhat, axis=0, keepdims=True), i)

    return pl.pallas_call(
        body, name=name, grid=(S // tm,),
        in_specs=[_rows(tm, D), _rows(tm, D), _vec(D), _vec(D), _rows(tm, D)],
        out_specs=[_rows(tm, D), _vec(D), _vec(D), _vec(D)],
        out_shape=[jax.ShapeDtypeStruct((S, D), F32)] + [jax.ShapeDtypeStruct((1, D), F32)] * 3,
        compiler_params=_arb(1),
    )(dh, x, g, sc, dres)


def _gate_fwd(x, y, gt, name):
    S, D = x.shape
    tm = _pick(S, 512, 8)

    def body(x_ref, y_ref, gt_ref, o_ref):
        o_ref[...] = x_ref[...] + gt_ref[...] * y_ref[...]

    return pl.pallas_call(
        body, name=name, grid=(S // tm,), in_specs=[_rows(tm, D), _rows(tm, D), _vec(D)], out_specs=_rows(tm, D),
        out_shape=jax.ShapeDtypeStruct((S, D), F32), compiler_params=_par(1),
    )(x, y, gt)


def _gate_bwd(dx, y, gt, name):
    S, D = dx.shape
    tm = _pick(S, 512, 8)

    def body(dx_ref, y_ref, gt_ref, dz_ref, dgt_ref):
        i = pl.program_id(0)
        d = dx_ref[...]
        dz_ref[...] = (d * gt_ref[...]).astype(BF16)
        _acc(dgt_ref, jnp.sum(d * y_ref[...], axis=0, keepdims=True), i)

    return pl.pallas_call(
        body, name=name, grid=(S // tm,), in_specs=[_rows(tm, D), _rows(tm, D), _vec(D)],
        out_specs=[_rows(tm, D), _vec(D)],
        out_shape=[jax.ShapeDtypeStruct((S, D), BF16), jax.ShapeDtypeStruct((1, D), F32)],
        compiler_params=_arb(1),
    )(dx, y, gt)


def _ffn_act_fwd(gu, name):
    S, H2 = gu.shape
    H = H2 // 2
    tm = _pick(S, 256, 8)

    def body(g_ref, u_ref, o_ref):
        gv = g_ref[...].astype(F32)
        o_ref[...] = (gv * _sigmoid(gv) * u_ref[...].astype(F32)).astype(BF16)

    return pl.pallas_call(
        body, name=name, grid=(S // tm,), in_specs=[_rows(tm, H, 0), _rows(tm, H, 1)], out_specs=_rows(tm, H),
        out_shape=jax.ShapeDtypeStruct((S, H), BF16), compiler_params=_par(1),
    )(gu, gu)


def _ffn_act_bwd(gu, dact, name):
    S, H2 = gu.shape
    H = H2 // 2
    tm = _pick(S, 256, 8)

    def body(g_ref, u_ref, da_ref, o_ref):
        gv = g_ref[...].astype(F32)
        s = _sigmoid(gv)
        da = da_ref[...].astype(F32)
        o_ref[:, :H] = (da * u_ref[...].astype(F32) * (s * (1.0 + gv * (1.0 - s)))).astype(BF16)
        o_ref[:, H:] = (da * (gv * s)).astype(BF16)

    return pl.pallas_call(
        body, name=name, grid=(S // tm,), in_specs=[_rows(tm, H, 0), _rows(tm, H, 1), _rows(tm, H)],
        out_specs=_rows(tm, H2), out_shape=jax.ShapeDtypeStruct((S, H2), BF16), compiler_params=_par(1),
    )(gu, gu, dact)


def _merge_fwd(proj, pa, pb, name):
    S, D = pa.shape
    tm = _pick(S, 512, 8)

    def body(ga_ref, gb_ref, pa_ref, pb_ref, o_ref):
        o_ref[...] = (_sigmoid(ga_ref[...]) * pa_ref[...].astype(F32)
                      + _sigmoid(gb_ref[...]) * pb_ref[...].astype(F32)).astype(BF16)

    return pl.pallas_call(
        body, name=name, grid=(S // tm,),
        in_specs=[_rows(tm, D, P_GA // D), _rows(tm, D, P_GB // D), _rows(tm, D), _rows(tm, D)],
        out_specs=_rows(tm, D), out_shape=jax.ShapeDtypeStruct((S, D), BF16), compiler_params=_par(1),
    )(proj, proj, pa, pb)


def _merge_bwd(proj, pa, pb, dm, name):
    S, D = pa.shape
    tm = _pick(S, 512, 16)
    rows_j = pl.BlockSpec((tm, D), lambda i, j: (i, 0))

    def body(g_ref, pa_ref, pb_ref, dm_ref, dg_ref, dpa_ref, dpb_ref):
        d = dm_ref[...].astype(F32)
        s = _sigmoid(g_ref[...])
        for branch, p_ref, dp_ref in ((0, pa_ref, dpa_ref), (1, pb_ref, dpb_ref)):
            @pl.when(pl.program_id(1) == branch)
            def _():
                dg_ref[...] = (d * p_ref[...].astype(F32) * s * (1.0 - s)).astype(BF16)
                dp_ref[...] = (d * s).astype(BF16)

    return pl.pallas_call(
        body, name=name, grid=(S // tm, 2),
        in_specs=[pl.BlockSpec((tm, D), lambda i, j: (i, P_GA // D + j)), rows_j, rows_j, rows_j],
        out_specs=[pl.BlockSpec((tm, D), lambda i, j: (i, P_GA // D + j)), rows_j, rows_j],
        out_shape=[jax.ShapeDtypeStruct((S, P_END), BF16), jax.ShapeDtypeStruct((S, D), BF16),
                   jax.ShapeDtypeStruct((S, D), BF16)],
        compiler_params=_arb(2),
    )(proj, pa, pb, dm)


def _write_columns(buf, parts, width, colblk, name):
    S = buf.shape[0]
    tm = _pick(S, 512, 16)
    n = len(parts)

    def body(*refs):
        o_ref = refs[n + 1]
        off = 0
        for p_ref in refs[:n]:
            w = p_ref.shape[1]
            o_ref[:, off:off + w] = p_ref[...].astype(BF16)
            off += w

    return pl.pallas_call(
        body, name=name, grid=(S // tm,), in_specs=[_rows(tm, p.shape[1]) for p in parts] + [HBM],
        out_specs=_rows(tm, width, colblk), out_shape=jax.ShapeDtypeStruct(buf.shape, buf.dtype),
        input_output_aliases={n: 0}, compiler_params=_par(1),
    )(*parts, buf)


def _loss_head(x, g, target, name):
    S, D = x.shape
    tm = _pick(S, 512, 8)

    def body(x_ref, g_ref, t_ref, dx_ref, loss_ref, dg_ref):
        i = pl.program_id(0)
        xv = x_ref[...]
        gv = g_ref[...]
        r = lax.rsqrt(jnp.mean(xv * xv, axis=-1, keepdims=True) + EPS)
        xhat = xv * r
        err = xhat * gv - t_ref[...]
        part = 0.5 * jnp.sum(jnp.mean(err * err, axis=-1, keepdims=True), axis=0, keepdims=True)
        _acc(loss_ref, jnp.broadcast_to(part, (1, LANE)), i)
        dy = err * (1.0 / D)
        _acc(dg_ref, jnp.sum(dy * xhat, axis=0, keepdims=True), i)
        dxhat = dy * gv
        dx_ref[...] = r * (dxhat - xhat * jnp.mean(dxhat * xhat, axis=-1, keepdims=True))

    return pl.pallas_call(
        body, name=name, grid=(S // tm,), in_specs=[_rows(tm, D), _vec(D), _rows(tm, D)],
        out_specs=[_rows(tm, D), _vec(LANE), _vec(D)],
        out_shape=[jax.ShapeDtypeStruct((S, D), F32), jax.ShapeDtypeStruct((1, LANE), F32),
                   jax.ShapeDtypeStruct((1, D), F32)],
        compiler_params=_arb(1),
    )(x, g, target)


HEADS_PER_SLAB = LANE // A_DH
N_SLABS = A_HEADS // HEADS_PER_SLAB
SPAN_BLOCKS = KSPAN // QBLK


def _attn_specs(seg):
    q_spec = pl.BlockSpec((QBLK, LANE), lambda p, m: (m, seg[0] * N_SLABS + p))
    k_specs = [pl.BlockSpec((QBLK, LANE), functools.partial(
        lambda j, p, m: (jnp.maximum(m - (SPAN_BLOCKS - 1) + j, 0), seg[1] * N_SLABS + p), j)) for j in range(SPAN_BLOCKS)]
    v_specs = [pl.BlockSpec((QBLK, LANE), functools.partial(
        lambda j, p, m: (jnp.maximum(m - (SPAN_BLOCKS - 1) + j, 0), seg[2] * N_SLABS + p), j)) for j in range(SPAN_BLOCKS)]
    b_spec = pl.BlockSpec((HEADS_PER_SLAB, QBLK, KSPAN), lambda p, m: (p, 0, 0))
    return q_spec, k_specs, v_specs, b_spec


def _head_lanes(t, hh):
    lane = lax.broadcasted_iota(jnp.int32, t.shape, 1)
    return jnp.where((lane // A_DH) == hh, t, jnp.zeros_like(t))


def _front_mask(m):
    col = lax.broadcasted_iota(jnp.int32, (QBLK, KSPAN), 1)
    return jnp.where(col < (SPAN_BLOCKS - 1 - m) * QBLK, NEG, 0.0)


def _attn_probs(qk, bias, front):
    s = qk * (A_DH ** -0.5) + (bias + front)
    p = jnp.exp(s - jnp.max(s, axis=-1, keepdims=True))
    return p * (1.0 / jnp.sum(p, axis=-1, keepdims=True))


def _grid_ends(nq):
    p, m = pl.program_id(0), pl.program_id(1)
    return (p == 0) & (m == 0), (p == N_SLABS - 1) & (m == nq // 2), (p == N_SLABS - 1) & (m == nq - 1)


def _attn_fwd(proj, big, name, gather=None):
    S = proj.shape[0]
    q_spec, k_specs, v_specs, b_spec = _attn_specs((0, 1, 2))
    shards, layer = gather if gather is not None else ((), None)
    ng = len(shards)

    def body(q_ref, k0, k1, k2, v0, v1, v2, b_ref, *rest):
        srcs, o_ref, gouts, sems = rest[:ng], rest[ng], rest[ng + 1:2 * ng + 1], rest[2 * ng + 1:]
        done = _carry(_gather_phases(layer, srcs, gouts, *sems), *_grid_ends(S // QBLK)) if ng else None
        m = pl.program_id(1)
        q = q_ref[...].astype(BF16)
        k = jnp.concatenate([k0[...], k1[...], k2[...]], axis=0).astype(BF16)
        v = jnp.concatenate([v0[...], v1[...], v2[...]], axis=0).astype(BF16)
        front = _front_mask(m)
        heads = range(HEADS_PER_SLAB)
        scores = [_nt(_head_lanes(q, hh), k) for hh in heads]
        probs = [_attn_probs(scores[hh], b_ref[hh], front).astype(BF16) for hh in heads]
        outs = [_nn(probs[hh], v) for hh in heads]
        lane = lax.broadcasted_iota(jnp.int32, (QBLK, LANE), 1)
        o_ref[...] = jnp.where(lane < A_DH, outs[0], outs[1]).astype(BF16)
        if ng:
            done()

    res = pl.pallas_call(
        body, name=name, grid=(N_SLABS, S // QBLK), in_specs=[q_spec] + k_specs + v_specs + [b_spec] + [HBM] * ng,
        out_specs=[pl.BlockSpec((QBLK, LANE), lambda p, m: (m, p))] + [HBM] * ng,
        out_shape=[jax.ShapeDtypeStruct((S, A_HEADS * A_DH), BF16)] + _gather_out_shapes(shards),
        scratch_shapes=_gather_scratch(ng) if ng else [], compiler_params=_arb(2),
    )(proj, proj, proj, proj, proj, proj, proj, big, *shards)
    return res[0], list(res[1:])


def _attn_bwd(proj, big, dya, name, exchange=()):
    S = proj.shape[0]
    W = A_HEADS * A_DH
    q_spec, k_specs, v_specs, b_spec = _attn_specs((0, 1, 2))
    out_q = pl.BlockSpec((QBLK, LANE), lambda p, m: (m, p))
    out_kv = pl.BlockSpec((S, LANE), lambda p, m: (0, p))
    ne = len(exchange)

    def body(q_ref, k0, k1, k2, v0, v1, v2, b_ref, do_ref, *rest):
        srcs, (dq_ref, dk_ref, dv_ref, db_ref), eouts, sems = rest[:ne], rest[ne:ne + 4], rest[ne + 4:2 * ne + 4], rest[2 * ne + 4:]
        done = _carry(_chips_phases(srcs, eouts, *sems), *_grid_ends(S // QBLK)) if ne else None
        m = pl.program_id(1)

        @pl.when(m == 0)
        def _():
            dk_ref[...] = jnp.zeros_like(dk_ref)
            dv_ref[...] = jnp.zeros_like(dv_ref)
            db_ref[...] = jnp.zeros_like(db_ref)

        q = q_ref[...].astype(BF16)
        k = jnp.concatenate([k0[...], k1[...], k2[...]], axis=0).astype(BF16)
        v = jnp.concatenate([v0[...], v1[...], v2[...]], axis=0).astype(BF16)
        do = do_ref[...]
        front = _front_mask(m)
        heads = range(HEADS_PER_SLAB)
        qh = [_head_lanes(q, hh) for hh in heads]
        doh = [_head_lanes(do, hh) for hh in heads]
        scores = [_nt(qh[hh], k) for hh in heads]
        dps = [_nt(doh[hh], v) for hh in heads]
        ps = [_attn_probs(scores[hh], b_ref[hh], front) for hh in heads]
        dss = [ps[hh] * (dps[hh] - jnp.sum(ps[hh] * dps[hh], axis=-1, keepdims=True)) for hh in heads]
        for hh in heads:
            db_ref[hh] += dss[hh]
        dsb = [(dss[hh] * (A_DH ** -0.5)).astype(BF16) for hh in heads]
        dqs = [_nn(dsb[hh], k) for hh in heads]
        dk = sum(_tn(dsb[hh], qh[hh]) for hh in heads)
        dv = sum(_tn(ps[hh].astype(BF16), doh[hh]) for hh in heads)
        lane = lax.broadcasted_iota(jnp.int32, (QBLK, LANE), 1)
        dq_ref[...] = jnp.where(lane < A_DH, dqs[0], dqs[1])
        for j in range(SPAN_BLOCKS):
            blk = m - (SPAN_BLOCKS - 1) + j

            @pl.when(blk >= 0)
            def _():
                off = pl.multiple_of(blk * QBLK, QBLK)
                dk_ref[pl.ds(off, QBLK), :] += dk[j * QBLK:(j + 1) * QBLK]
                dv_ref[pl.ds(off, QBLK), :] += dv[j * QBLK:(j + 1) * QBLK]
        if ne:
            done()

    res = pl.pallas_call(
        body, name=name, grid=(N_SLABS, S // QBLK),
        in_specs=[q_spec] + k_specs + v_specs + [b_spec, pl.BlockSpec((QBLK, LANE), lambda p, m: (m, p))] + [HBM] * ne,
        out_specs=[out_q, out_kv, out_kv, b_spec] + [HBM] * ne,
        out_shape=[jax.ShapeDtypeStruct((S, W), F32)] * 3 + [jax.ShapeDtypeStruct((A_HEADS, QBLK, KSPAN), F32)]
        + _chips_out_shapes(exchange),
        scratch_shapes=_chips_scratch(ne) if ne else [], compiler_params=_arb(2),
    )(proj, proj, proj, proj, proj, proj, proj, big, dya, *exchange)
    return tuple(res[:4]) + (list(res[4:]),)


NREL_PAD = 3 * LANE
SKEW_W = 1024


def _rel_table_grad(dbig, name):
    H, R, C = dbig.shape

    def body(d_ref, o_ref):
        x = jnp.concatenate([d_ref[...], jnp.zeros((R, SKEW_W - C), F32)], axis=1)
        row = lax.broadcasted_iota(jnp.int32, (R, SKEW_W), 0)
        for b in range(R.bit_length() - 1):
            x = jnp.where(((row >> b) & 1) == 1, pltpu.roll(x, SKEW_W - (1 << b), 1), x)
        e = jnp.sum(x, axis=0, keepdims=True)
        xi = lax.broadcasted_iota(jnp.int32, (SKEW_W, NREL_PAD), 0)
        r = lax.broadcasted_iota(jnp.int32, (SKEW_W, NREL_PAD), 1)
        diag = jnp.where(xi < C, xi, xi - SKEW_W)
        rel = jnp.clip(A_PAST * CHUNK - diag, -A_MAX_REL, A_MAX_REL) + A_MAX_REL
        o_ref[...] = _nn(e, jnp.where(rel == r, 1.0, 0.0).astype(F32), HI)

    return pl.pallas_call(
        body, name=name, grid=(H,), in_specs=[pl.BlockSpec((None, R, C), lambda h: (h, 0, 0))],
        out_specs=pl.BlockSpec((None, 1, NREL_PAD), lambda h: (h, 0, 0)),
        out_shape=jax.ShapeDtypeStruct((H, 1, NREL_PAD), F32), compiler_params=_par(1),
    )(dbig)


def _chunk_cumsum_matrix(n, reverse):
    j = lax.broadcasted_iota(jnp.int32, (n, n), 0)
    i = lax.broadcasted_iota(jnp.int32, (n, n), 1)
    same = (j // CHUNK) == (i // CHUNK)
    return jnp.where(same & ((j >= i) if reverse else (j <= i)), 1.0, 0.0).astype(F32)


def _gdn_gates_fwd(b_t, a_t, alog, dtb, name):
    Hh, S = b_t.shape
    tl = _pick(S, 512)
    row = pl.BlockSpec((Hh, tl), lambda i: (0, i))
    col = pl.BlockSpec((Hh, 1), lambda i: (0, 0))

    def body(b_ref, a_ref, al_ref, dt_ref, beta_ref, gam_ref):
        z = a_ref[...] + dt_ref[...]
        sp = jnp.maximum(z, 0.0) + jnp.log(1.0 + jnp.exp(-jnp.abs(z)))
        g = -jnp.exp(al_ref[...]) * sp
        beta_ref[...] = _sigmoid(b_ref[...])
        gam_ref[...] = _nn(g, _chunk_cumsum_matrix(tl, False), HI)

    return pl.pallas_call(
        body, name=name, grid=(S // tl,), in_specs=[row, row, col, col], out_specs=[row, row],
        out_shape=[jax.ShapeDtypeStruct((Hh, S), F32)] * 2, compiler_params=_par(1),
    )(b_t, a_t, alog, dtb)


def _gdn_gates_bwd(dbeta, dgam_a, dgam_b, b_t, a_t, alog, dtb, name):
    Hh, S = b_t.shape
    tl = _pick(S, 512)
    row = pl.BlockSpec((Hh, tl), lambda i: (0, i))
    col = pl.BlockSpec((Hh, 1), lambda i: (0, 0))
    accs = pl.BlockSpec((Hh, LANE), lambda i: (0, 0))

    def body(dbeta_ref, dga_ref, dgb_ref, b_ref, a_ref, al_ref, dt_ref, db_ref, da_ref, dal_ref, ddt_ref):
        i = pl.program_id(0)
        z = a_ref[...] + dt_ref[...]
        sp = jnp.maximum(z, 0.0) + jnp.log(1.0 + jnp.exp(-jnp.abs(z)))
        ea = jnp.exp(al_ref[...])
        dg = _nn(dga_ref[...] + dgb_ref[...], _chunk_cumsum_matrix(tl, True), HI)
        da = dg * (-ea) * _sigmoid(z)
        beta = _sigmoid(b_ref[...])
        db_ref[...] = dbeta_ref[...] * beta * (1.0 - beta)
        da_ref[...] = da
        _acc(dal_ref, jnp.broadcast_to(jnp.sum(dg * (-ea * sp), axis=1, keepdims=True), (Hh, LANE)), i)
        _acc(ddt_ref, jnp.broadcast_to(jnp.sum(da, axis=1, keepdims=True), (Hh, LANE)), i)

    return pl.pallas_call(
        body, name=name, grid=(S // tl,), in_specs=[row] * 5 + [col, col], out_specs=[row, row, accs, accs],
        out_shape=[jax.ShapeDtypeStruct((Hh, S), F32)] * 2 + [jax.ShapeDtypeStruct((Hh, LANE), F32)] * 2,
        compiler_params=_arb(1),
    )(dbeta, dgam_a, dgam_b, b_t, a_t, alog, dtb)


HALO = 8


def _conv_silu(xx_ref, w_ref, tm):
    y = w_ref[0:1, :] * xx_ref[pl.ds(HALO - CONV_K + 1, tm), :]
    for j in range(1, CONV_K):
        y = y + w_ref[j:j + 1, :] * xx_ref[pl.ds(HALO - CONV_K + 1 + j, tm), :]
    return y, y * _sigmoid(y)


def _fill_prev_halo(xx_ref, x_ref, prev_ref, i, tm):
    xx_ref[pl.ds(HALO, tm), :] = x_ref[...]

    @pl.when(i == 0)
    def _():
        xx_ref[pl.ds(0, HALO), :] = jnp.zeros((HALO, xx_ref.shape[1]), F32)

    @pl.when(i != 0)
    def _():
        xx_ref[pl.ds(0, HALO), :] = prev_ref[...]


def _gdn_pre_specs(tm, C, colblk):
    cur = pl.BlockSpec((tm, C), lambda i: (i, colblk))
    prev = pl.BlockSpec((HALO, C), lambda i: (jnp.maximum(i * (tm // HALO) - 1, 0), colblk))
    return cur, prev


def _gdn_pre_fwd(proj, wconv, name):
    S = proj.shape[0]
    C = 3 * B_HEADS * B_DH
    W = B_HEADS * B_DH
    tm = _pick(S, 256, 8)
    cur, prev = _gdn_pre_specs(tm, C, P_QKVB // C)

    def body(x_ref, prev_ref, w_ref, q_ref, k_ref, v_ref, xx_ref):
        i = pl.program_id(0)
        _fill_prev_halo(xx_ref, x_ref, prev_ref, i, tm)
        _, sl = _conv_silu(xx_ref, w_ref, tm)
        for h in range(B_HEADS):
            hs = slice(h * B_DH, (h + 1) * B_DH)
            q = sl[:, h * B_DH:(h + 1) * B_DH]
            k = sl[:, W + h * B_DH:W + (h + 1) * B_DH]
            q_ref[:, hs] = q * (lax.rsqrt(jnp.sum(q * q, axis=-1, keepdims=True) + EPS) * (B_DH ** -0.5))
            k_ref[:, hs] = k * lax.rsqrt(jnp.sum(k * k, axis=-1, keepdims=True) + EPS)
        v_ref[...] = sl[:, 2 * W:]

    return pl.pallas_call(
        body, name=name, grid=(S // tm,), in_specs=[cur, prev, pl.BlockSpec((CONV_K, C), lambda i: (0, 0))],
        out_specs=[_rows(tm, W)] * 3, out_shape=[jax.ShapeDtypeStruct((S, W), F32)] * 3,
        scratch_shapes=[pltpu.VMEM((HALO + tm, C), F32)], compiler_params=_par(1),
    )(proj, proj, wconv)


def _gdn_pre_bwd_a(proj, wconv, dqn, dkn, dv, name):
    S = proj.shape[0]
    C = 3 * B_HEADS * B_DH
    W = B_HEADS * B_DH
    tm = _pick(S, 256, 8)
    cur, prev = _gdn_pre_specs(tm, C, P_QKVB // C)

    def body(x_ref, prev_ref, w_ref, dq_ref, dk_ref, dv_ref, dy_ref, xx_ref):
        i = pl.program_id(0)
        _fill_prev_halo(xx_ref, x_ref, prev_ref, i, tm)
        y, sl = _conv_silu(xx_ref, w_ref, tm)
        sg = _sigmoid(y)
        dsilu = sg * (1.0 + y * (1.0 - sg))
        for h in range(B_HEADS):
            for base, d_ref, c in ((0, dq_ref, B_DH ** -0.5), (W, dk_ref, 1.0)):
                lo = base + h * B_DH
                t = sl[:, lo:lo + B_DH]
                d = d_ref[:, h * B_DH:(h + 1) * B_DH]
                r = lax.rsqrt(jnp.sum(t * t, axis=-1, keepdims=True) + EPS)
                dt = (c * r) * (d - t * (r * r) * jnp.sum(d * t, axis=-1, keepdims=True))
                dy_ref[:, lo:lo + B_DH] = dt * dsilu[:, lo:lo + B_DH]
        dy_ref[:, 2 * W:] = dv_ref[...] * dsilu[:, 2 * W:]

    return pl.pallas_call(
        body, name=name, grid=(S // tm,),
        in_specs=[cur, prev, pl.BlockSpec((CONV_K, C), lambda i: (0, 0))] + [_rows(tm, W)] * 3,
        out_specs=_rows(tm, C), out_shape=jax.ShapeDtypeStruct((S, C), F32),
        scratch_shapes=[pltpu.VMEM((HALO + tm, C), F32)], compiler_params=_par(1),
    )(proj, proj, wconv, dqn, dkn, dv)


def _gdn_pre_bwd_b(proj, wconv, dy, dproj, name):
    S = proj.shape[0]
    C = 3 * B_HEADS * B_DH
    tm = _pick(S, 256, 16)
    nt_ = S // tm
    cur, prev = _gdn_pre_specs(tm, C, P_QKVB // C)
    nxt = pl.BlockSpec((HALO, C), lambda i: (jnp.minimum((i + 1) * (tm // HALO), S // HALO - 1), 0))

    def body(x_ref, prev_ref, w_ref, dy_ref, next_ref, _, dx_ref, dw_ref, xx_ref, dd_ref):
        i = pl.program_id(0)
        _fill_prev_halo(xx_ref, x_ref, prev_ref, i, tm)
        dyv = dy_ref[...]
        dd_ref[pl.ds(0, tm), :] = dyv

        @pl.when(i == nt_ - 1)
        def _():
            dd_ref[pl.ds(tm, HALO), :] = jnp.zeros((HALO, C), F32)

        @pl.when(i != nt_ - 1)
        def _():
            dd_ref[pl.ds(tm, HALO), :] = next_ref[...]

        dx = w_ref[0:1, :] * dd_ref[pl.ds(CONV_K - 1, tm), :]
        for j in range(1, CONV_K):
            dx = dx + w_ref[j:j + 1, :] * dd_ref[pl.ds(CONV_K - 1 - j, tm), :]
        dx_ref[...] = dx.astype(BF16)
        dw = jnp.concatenate(
            [jnp.sum(dyv * xx_ref[pl.ds(HALO - CONV_K + 1 + j, tm), :], axis=0, keepdims=True) for j in range(CONV_K)],
            axis=0)
        _acc(dw_ref, dw, i)

    return pl.pallas_call(
        body, name=name, grid=(nt_,),
        in_specs=[cur, prev, pl.BlockSpec((CONV_K, C), lambda i: (0, 0)), _rows(tm, C), nxt, HBM],
        out_specs=[_rows(tm, C, P_QKVB // C), pl.BlockSpec((CONV_K, C), lambda i: (0, 0))],
        out_shape=[jax.ShapeDtypeStruct(dproj.shape, BF16), jax.ShapeDtypeStruct((CONV_K, C), F32)],
        input_output_aliases={5: 0},
        scratch_shapes=[pltpu.VMEM((HALO + tm, C), F32), pltpu.VMEM((tm + HALO, C), F32)],
        compiler_params=_arb(1),
    )(proj, proj, wconv, dy, dy, dproj)


def _chunk_masks():
    row = lax.broadcasted_iota(jnp.int32, (CHUNK, CHUNK), 0)
    col = lax.broadcasted_iota(jnp.int32, (CHUNK, CHUNK), 1)
    return row >= col, row > col


def _chunk_local(q, k, vv, bc, gc, gr, tri):
    dm = jnp.where(tri, jnp.exp(jnp.where(tri, gc - gr, 0.0)), 0.0)
    kk = _bnt(k, k)
    glast = gr[..., CHUNK - 1:CHUNK]
    ep = jnp.exp(gc)
    em = jnp.exp(glast - gc)
    el = jnp.exp(glast)
    return dm, kk, ep, em, el, vv * bc, k * (bc * ep)


def _unit_lower_inverse(low):
    row = lax.broadcasted_iota(jnp.int32, (CHUNK, CHUNK), 0)
    col = lax.broadcasted_iota(jnp.int32, (CHUNK, CHUNK), 1)
    p = -low
    t = jnp.where(row == col, 1.0, 0.0).astype(F32) + p
    steps = CHUNK.bit_length() - 2
    for _ in range(steps):
        p = _nn(p, p, SOLVE_PREC)
        t = t + _nn(t, p, SOLVE_PREC)
    return t


GROUP = 4


LATE = 7


def _carry(phases, first, middle, last):
    if len(phases) == 3:
        pl.when(first)(phases[0])
        pl.when(middle)(phases[1])
        return lambda: pl.when(last)(phases[2])
    pl.when(first)(phases[0])
    return lambda: pl.when(last)(phases[1])


def _pairs(nchunks):
    return [(c, h) for c in range(nchunks) for h in range(B_HEADS)]


def _tok(c):
    return slice(c * CHUNK, (c + 1) * CHUNK)


def _head(h):
    return slice(h * B_DH, (h + 1) * B_DH)


def _stack_tokens(ref, nchunks):
    return jnp.stack([ref[_tok(c), _head(h)] for c, h in _pairs(nchunks)])


def _stack_cols(ref, nchunks):
    per_chunk = [ref[c] for c in range(nchunks)] if len(ref.shape) == 3 else [ref[...]]
    return jnp.stack([per_chunk[c][:, h:h + 1] for c, h in _pairs(nchunks)])


def _stack_rows(ref, nchunks):
    if len(ref.shape) == 3:
        return jnp.stack([ref[c, h:h + 1, :] for c, h in _pairs(nchunks)])
    return jnp.stack([ref[h:h + 1, :] for _, h in _pairs(1)])


def _gdn_group_specs(ng_steps, W):
    tok = pl.BlockSpec((GROUP * CHUNK, W), lambda i: (i, 0))
    colv = pl.BlockSpec((GROUP, CHUNK, B_HEADS), lambda i: (i, 0, 0))
    rowv = pl.BlockSpec((GROUP, B_HEADS, CHUNK), lambda i: (i, 0, 0))
    mat = pl.BlockSpec((GROUP, B_HEADS, CHUNK, CHUNK), lambda i: (i, 0, 0, 0))
    return tok, colv, rowv, mat


def _gdn_local_fwd(qn, kn, v, bcol, gcol, grow, name, gather=None):
    S, Wd = qn.shape
    nc = S // CHUNK
    steps = nc // GROUP
    tok, colv, rowv, mat = _gdn_group_specs(steps, Wd)
    shards, layer = gather if gather is not None else ((), None)
    ng = len(shards)

    def body(q_ref, k_ref, v_ref, bc_ref, gc_ref, gr_ref, *rest):
        srcs, (t_ref, a_ref, u_ref, w_ref), gouts, sems = rest[:ng], rest[ng:ng + 4], rest[ng + 4:2 * ng + 4], rest[2 * ng + 4:]
        i = pl.program_id(0)
        done = _carry(_gather_phases(layer, srcs, gouts, *sems), i == 0, i == LATE * steps // 8, i == steps - 1) if ng else None
        tri, strict = _chunk_masks()
        q, k, vv = (_stack_tokens(r, GROUP) for r in (q_ref, k_ref, v_ref))
        bc, gc, gr = _stack_cols(bc_ref, GROUP), _stack_cols(gc_ref, GROUP), _stack_rows(gr_ref, GROUP)
        dm, kk, ep, em, el, vb, kb = _chunk_local(q, k, vv, bc, gc, gr, tri)
        t = _unit_lower_inverse(jnp.where(strict, bc * kk * dm, 0.0))
        a = _bnt(q, k) * dm
        u = _nn(t, vb, SOLVE_PREC)
        w = _nn(t, kb, SOLVE_PREC)
        for n, (c, h) in enumerate(_pairs(GROUP)):
            t_ref[c, h] = t[n]
            a_ref[c, h] = a[n]
            u_ref[_tok(c), _head(h)] = u[n]
            w_ref[_tok(c), _head(h)] = w[n]
        if ng:
            done()

    res = pl.pallas_call(
        body, name=name, grid=(steps,), in_specs=[tok, tok, tok, colv, colv, rowv] + [HBM] * ng,
        out_specs=[mat, mat, tok, tok] + [HBM] * ng,
        out_shape=[jax.ShapeDtypeStruct((nc, B_HEADS, CHUNK, CHUNK), F32)] * 2 + [jax.ShapeDtypeStruct((S, Wd), F32)] * 2
        + _gather_out_shapes(shards),
        scratch_shapes=_gather_scratch(ng) if ng else [], compiler_params=_arb(1),
    )(qn, kn, v, bcol, gcol, grow, *shards)
    return res[0], res[1], res[2], res[3], list(res[4:])


def _scan_decays(gc, gr):
    glast = gr[..., CHUNK - 1:CHUNK]
    return jnp.exp(gc), jnp.exp(glast - gc), jnp.exp(glast)


def _gdn_scan_specs(nc, rev):
    idx = (lambda i: nc - 1 - i) if rev else (lambda i: i)
    W = B_HEADS * B_DH
    tok = pl.BlockSpec((CHUNK, W), lambda i: (idx(i), 0))
    colv = pl.BlockSpec((None, CHUNK, B_HEADS), lambda i: (idx(i), 0, 0))
    rowv = pl.BlockSpec((None, B_HEADS, CHUNK), lambda i: (idx(i), 0, 0))
    mat = pl.BlockSpec((None, B_HEADS, CHUNK, CHUNK), lambda i: (idx(i), 0, 0, 0))
    smat = pl.BlockSpec((None, B_HEADS, B_DH, B_DH), lambda i: (idx(i), 0, 0, 0))
    return tok, colv, rowv, mat, smat


def _gdn_scan_fwd(qn, kn, u, w, a, gcol, grow, name, gather=None):
    S, Wd = qn.shape
    nc = S // CHUNK
    tok, colv, rowv, mat, smat = _gdn_scan_specs(nc, False)
    shards, layer = gather if gather is not None else ((), None)
    ng = len(shards)

    def body(q_ref, k_ref, u_ref, w_ref, a_ref, gc_ref, gr_ref, *rest):
        srcs, (o_ref, sh_ref), gouts = rest[:ng], rest[ng:ng + 2], rest[ng + 2:2 * ng + 2]
        st_ref, sems = rest[2 * ng + 2], rest[2 * ng + 3:]
        i = pl.program_id(0)
        done = _carry(_gather_phases(layer, srcs, gouts, *sems), i == 0, i == LATE * nc // 8, i == nc - 1) if ng else None

        @pl.when(i == 0)
        def _():
            st_ref[...] = jnp.zeros_like(st_ref)

        ep, em, el = _scan_decays(_stack_cols(gc_ref, 1), _stack_rows(gr_ref, 1))
        q, k, u, w = (_stack_tokens(r, 1) for r in (q_ref, k_ref, u_ref, w_ref))
        s0 = st_ref[...]
        ut = u - _bnn(w, s0)
        o = _bnn(q * ep, s0) + _bnn(a_ref[...], ut)
        st_ref[...] = el * s0 + _btn(k * em, ut)
        sh_ref[...] = s0
        for h in range(B_HEADS):
            o_ref[:, _head(h)] = o[h]
        if ng:
            done()

    res = pl.pallas_call(
        body, name=name, grid=(nc,), in_specs=[tok, tok, tok, tok, mat, colv, rowv] + [HBM] * ng,
        out_specs=[tok, smat] + [HBM] * ng,
        out_shape=[jax.ShapeDtypeStruct((S, Wd), F32), jax.ShapeDtypeStruct((nc, B_HEADS, B_DH, B_DH), F32)]
        + _gather_out_shapes(shards),
        scratch_shapes=[pltpu.VMEM((B_HEADS, B_DH, B_DH), F32)] + (_gather_scratch(ng) if ng else []),
        compiler_params=_arb(1),
    )(qn, kn, u, w, a, gcol, grow, *shards)
    return res[0], res[1], list(res[2:])


def _gdn_scan_bwd(qn, kn, u, w, a, gcol, grow, ssave, do, name):
    S, Wd = qn.shape
    nc = S // CHUNK
    tok, colv, rowv, mat, smat = _gdn_scan_specs(nc, True)

    def body(q_ref, k_ref, u_ref, w_ref, a_ref, gc_ref, gr_ref, sh_ref, do_ref,
             du_ref, dw_ref, dqd_ref, dkd_ref, da_ref, dgl_ref, ds_ref):
        i = pl.program_id(0)

        @pl.when(i == 0)
        def _():
            ds_ref[...] = jnp.zeros_like(ds_ref)

        tri, _ = _chunk_masks()
        sub4 = lax.broadcasted_iota(jnp.int32, (B_HEADS, CHUNK), 0)
        lane_last = lax.broadcasted_iota(jnp.int32, (1, CHUNK), 1) == CHUNK - 1
        ep, em, el = _scan_decays(_stack_cols(gc_ref, 1), _stack_rows(gr_ref, 1))
        q, k, u, w, dout = (_stack_tokens(r, 1) for r in (q_ref, k_ref, u_ref, w_ref, do_ref))
        s0 = sh_ref[...]
        ds = ds_ref[...]
        ut = u - _bnn(w, s0)
        dut = _btn(a_ref[...], dout) + _bnn(k * em, ds)
        ds_ref[...] = el * ds + _btn(q * ep, dout) - _btn(w, dut)
        dw = -_bnt(dut, s0)
        dqd = _bnt(dout, s0)
        dkd = _bnt(ut, ds)
        da_ref[...] = jnp.where(tri, _bnt(dout, ut), 0.0)
        d_el = jnp.sum(jnp.sum(s0 * ds, axis=-1, keepdims=True), axis=-2, keepdims=True)
        last = d_el * el
        dgl_acc = jnp.zeros((B_HEADS, CHUNK), F32)
        for h in range(B_HEADS):
            du_ref[:, _head(h)] = dut[h]
            dw_ref[:, _head(h)] = dw[h]
            dqd_ref[:, _head(h)] = dqd[h]
            dkd_ref[:, _head(h)] = dkd[h]
            dgl_acc = jnp.where(sub4 == h, jnp.where(lane_last, last[h], 0.0), dgl_acc)
        dgl_ref[...] = dgl_acc

    return pl.pallas_call(
        body, name=name, grid=(nc,), in_specs=[tok, tok, tok, tok, mat, colv, rowv, smat, tok],
        out_specs=[tok, tok, tok, tok, mat, rowv],
        out_shape=[jax.ShapeDtypeStruct((S, Wd), F32)] * 4 + [jax.ShapeDtypeStruct((nc, B_HEADS, CHUNK, CHUNK), F32),
                                                             jax.ShapeDtypeStruct((nc, B_HEADS, CHUNK), F32)],
        scratch_shapes=[pltpu.VMEM((B_HEADS, B_DH, B_DH), F32)], compiler_params=_arb(1),
    )(qn, kn, u, w, a, gcol, grow, ssave, do)


def _gdn_local_bwd(qn, kn, v, bcol, gcol, grow, tsave, du, dw, dqd, dkd, da, dgl, name, exchange=()):
    S, Wd = qn.shape
    nc = S // CHUNK
    steps = nc // GROUP
    tok, colv, rowv, mat = _gdn_group_specs(steps, Wd)
    ne = len(exchange)

    def body(q_ref, k_ref, v_ref, bc_ref, gc_ref, gr_ref, t_ref, du_ref, dw_ref, dqd_ref, dkd_ref, da_ref, dgl_ref, *rest):
        srcs, (dq_ref, dk_ref, dv_ref, dbc_ref, dgc_ref, dgr_ref) = rest[:ne], rest[ne:ne + 6]
        eouts, sems = rest[ne + 6:2 * ne + 6], rest[2 * ne + 6:]
        i = pl.program_id(0)
        done = _carry(_chips_phases(srcs, eouts, *sems), i == 0, None, i == steps - 1) if ne else None
        tri, strict = _chunk_masks()
        lane4 = lax.broadcasted_iota(jnp.int32, (CHUNK, B_HEADS), 1)
        sub4 = lax.broadcasted_iota(jnp.int32, (B_HEADS, CHUNK), 0)
        lane_last = lax.broadcasted_iota(jnp.int32, (1, CHUNK), 1) == CHUNK - 1
        q, k, vv, dut, dwv, dqd, dkd = (_stack_tokens(r, GROUP)
                                        for r in (q_ref, k_ref, v_ref, du_ref, dw_ref, dqd_ref, dkd_ref))
        bc, gc, gr = _stack_cols(bc_ref, GROUP), _stack_cols(gc_ref, GROUP), _stack_rows(gr_ref, GROUP)
        dm, kk, ep, em, el, vb, kb = _chunk_local(q, k, vv, bc, gc, gr, tri)
        t = jnp.stack([t_ref[c, h] for c, h in _pairs(GROUP)])
        dav = jnp.stack([da_ref[c, h] for c, h in _pairs(GROUP)])
        qk = _bnt(q, k)
        dt = _nt(dut, vb, SOLVE_PREC) + _nt(dwv, kb, SOLVE_PREC)
        dvb = _tn(t, dut, SOLVE_PREC)
        dkb = _tn(t, dwv, SOLVE_PREC)
        dl = jnp.where(strict, -_tn(t, _nt(dt, t, SOLVE_PREC), SOLVE_PREC), 0.0)
        g1 = dl * dm
        dkb_k = jnp.sum(dkb * k, axis=-1, keepdims=True)
        dbeta = jnp.sum(g1 * kk, axis=-1, keepdims=True) + jnp.sum(dvb * vv, axis=-1, keepdims=True) + dkb_k * ep
        dkk = g1 * bc
        ddm = dl * (bc * kk) + dav * qk
        dqk = dav * dm
        dq = _bnn(dqk, k) + dqd * ep
        dk = _btn(dqk, q) + _bnn(dkk, k) + _btn(dkk, k) + dkb * (bc * ep) + dkd * em
        dv = dvb * bc
        dep = dkb_k * bc + jnp.sum(dqd * q, axis=-1, keepdims=True)
        dem = jnp.sum(dkd * k, axis=-1, keepdims=True)
        mm = ddm * dm
        dgam_c = jnp.sum(mm, axis=-1, keepdims=True) + dep * ep - dem * em
        dglast = jnp.sum(dem * em, axis=-2, keepdims=True)
        dgam_r = -jnp.sum(mm, axis=-2, keepdims=True) + jnp.where(lane_last, dglast, 0.0)
        for c in range(GROUP):
            dbc_acc = jnp.zeros((CHUNK, B_HEADS), F32)
            dgc_acc = jnp.zeros((CHUNK, B_HEADS), F32)
            dgr_acc = jnp.zeros((B_HEADS, CHUNK), F32)
            for h in range(B_HEADS):
                n = c * B_HEADS + h
                dq_ref[_tok(c), _head(h)] = dq[n]
                dk_ref[_tok(c), _head(h)] = dk[n]
                dv_ref[_tok(c), _head(h)] = dv[n]
                dbc_acc = jnp.where(lane4 == h, dbeta[n], dbc_acc)
                dgc_acc = jnp.where(lane4 == h, dgam_c[n], dgc_acc)
                dgr_acc = jnp.where(sub4 == h, dgam_r[n], dgr_acc)
            dbc_ref[c] = dbc_acc
            dgc_ref[c] = dgc_acc
            dgr_ref[c] = dgr_acc + dgl_ref[c]
        if ne:
            done()

    res = pl.pallas_call(
        body, name=name, grid=(steps,),
        in_specs=[tok, tok, tok, colv, colv, rowv, mat, tok, tok, tok, tok, mat, rowv] + [HBM] * ne,
        out_specs=[tok, tok, tok, colv, colv, rowv] + [HBM] * ne,
        out_shape=[jax.ShapeDtypeStruct((S, Wd), F32)] * 3
        + [jax.ShapeDtypeStruct((nc, CHUNK, B_HEADS), F32)] * 2 + [jax.ShapeDtypeStruct((nc, B_HEADS, CHUNK), F32)]
        + _chips_out_shapes(exchange),
        scratch_shapes=_chips_scratch(ne) if ne else [], compiler_params=_arb(1),
    )(qn, kn, v, bcol, gcol, grow, tsave, du, dw, dqd, dkd, da, dgl, *exchange)
    return tuple(res[:6]) + (list(res[6:]),)


def _gdn_post_fwd(o, proj, ng, name):
    S, W = o.shape
    tm = _pick(S, 512, 8)

    def body(o_ref, z_ref, g_ref, y_ref):
        gv = g_ref[...]
        for h in range(B_HEADS):
            hs = slice(h * B_DH, (h + 1) * B_DH)
            oh = o_ref[:, hs]
            z = z_ref[:, hs]
            r = lax.rsqrt(jnp.mean(oh * oh, axis=-1, keepdims=True) + EPS)
            y_ref[:, hs] = (oh * r * gv * (z * _sigmoid(z))).astype(BF16)

    return pl.pallas_call(
        body, name=name, grid=(S // tm,), in_specs=[_rows(tm, W), _rows(tm, W, P_Z // W), _vec(B_DH)],
        out_specs=_rows(tm, W), out_shape=jax.ShapeDtypeStruct((S, W), BF16), compiler_params=_par(1),
    )(o, proj, ng)


def _gdn_post_bwd(dy, o, proj, ng, dproj, name):
    S, W = o.shape
    tm = _pick(S, 512, 16)

    def body(dy_ref, o_ref, z_ref, g_ref, _, do_ref, dz_ref, dg_ref):
        i = pl.program_id(0)
        gv = g_ref[...]
        dg = jnp.zeros((1, B_DH), F32)
        for h in range(B_HEADS):
            hs = slice(h * B_DH, (h + 1) * B_DH)
            oh = o_ref[:, hs]
            z = z_ref[:, hs]
            d = dy_ref[:, hs]
            r = lax.rsqrt(jnp.mean(oh * oh, axis=-1, keepdims=True) + EPS)
            n = oh * r
            sg = _sigmoid(z)
            sz = z * sg
            dn = d * gv * sz
            dg = dg + jnp.sum(d * n * sz, axis=0, keepdims=True)
            dz_ref[:, hs] = (d * n * gv * (sg * (1.0 + z * (1.0 - sg)))).astype(BF16)
            do_ref[:, hs] = r * (dn - n * jnp.mean(dn * n, axis=-1, keepdims=True))
        _acc(dg_ref, dg, i)

    return pl.pallas_call(
        body, name=name, grid=(S // tm,),
        in_specs=[_rows(tm, W), _rows(tm, W), _rows(tm, W, P_Z // W), _vec(B_DH), HBM],
        out_specs=[_rows(tm, W), _rows(tm, W, P_Z // W), _vec(B_DH)],
        out_shape=[jax.ShapeDtypeStruct((S, W), F32), jax.ShapeDtypeStruct(dproj.shape, BF16),
                   jax.ShapeDtypeStruct((1, B_DH), F32)],
        input_output_aliases={4: 1}, compiler_params=_arb(1),
    )(dy, o, proj, ng, dproj)


def _ada_mod(c_all, w_ada, b_shard, name):
    L, D, Ns = w_ada.shape
    B = c_all.shape[0]

    def body(c_ref, w_ref, b_ref, o_ref):
        cv = c_ref[...]
        cond = (cv * _sigmoid(cv)).astype(BF16)
        o_ref[...] = _nn(cond, w_ref[...].astype(BF16)) + b_ref[...]

    return pl.pallas_call(
        body, name=name, grid=(L,),
        in_specs=[pl.BlockSpec((B, D), lambda l: (0, 0)), pl.BlockSpec((None, D, Ns), lambda l: (l, 0, 0)),
                  pl.BlockSpec((None, 1, Ns), lambda l: (l, 0, 0))],
        out_specs=pl.BlockSpec((None, B, Ns), lambda l: (l, 0, 0)),
        out_shape=jax.ShapeDtypeStruct((L, B, Ns), F32), compiler_params=_par(1),
    )(c_all, w_ada, b_shard)


def _ada_wgrad(c_all, dmod, name):
    L, B, Ns = dmod.shape
    D = c_all.shape[1]

    def body(c_ref, d_ref, o_ref):
        cv = c_ref[...]
        cond = (cv * _sigmoid(cv)).astype(BF16)
        o_ref[...] = _tn(cond, d_ref[...].astype(BF16))

    return pl.pallas_call(
        body, name=name, grid=(L,),
        in_specs=[pl.BlockSpec((B, D), lambda l: (0, 0)), pl.BlockSpec((None, B, Ns), lambda l: (l, 0, 0))],
        out_specs=pl.BlockSpec((None, D, Ns), lambda l: (l, 0, 0)),
        out_shape=jax.ShapeDtypeStruct((L, D, Ns), F32), compiler_params=_par(1),
    )(c_all, dmod)


W_IN_PIECES = ((0, 0, 1410), (1, 0, 1410), (2, 0, 252), (2, 772, 638), (3, 0, 1410), (2, 252, 512), (2, 764, 8))


def _reorder_w_in(w4, name):
    L, _, D, Cs = w4.shape
    tm = _pick(D, 256, 16)
    used = sum(p[2] for p in W_IN_PIECES)

    def body(w_ref, o_ref):
        shard = [w_ref[s] for s in range(4)]
        parts = [shard[s][:, lo:lo + n] for s, lo, n in W_IN_PIECES]
        o_ref[...] = jnp.concatenate(parts + [jnp.zeros((tm, P_END - used), w4.dtype)], axis=1)

    return pl.pallas_call(
        body, name=name, grid=(L, D // tm), in_specs=[pl.BlockSpec((None, 4, tm, Cs), lambda l, i: (l, 0, i, 0))],
        out_specs=pl.BlockSpec((None, tm, P_END), lambda l, i: (l, i, 0)),
        out_shape=jax.ShapeDtypeStruct((L, D, P_END), w4.dtype), compiler_params=_par(2),
    )(w4)


def _restore_w_in(g, name):
    D = g.shape[0]
    tm = _pick(D, 256, 16)

    def body(g_ref, o_ref, ob_ref):
        gv = g_ref[...]
        off = 0
        pieces = {}
        for s, lo, n in W_IN_PIECES:
            pieces.setdefault(s, []).append((lo, gv[:, off:off + n]))
            off += n
        for s, lst in pieces.items():
            lst.sort(key=lambda t: t[0])
            shard = lst[0][1] if len(lst) == 1 else jnp.concatenate([t[1] for t in lst], axis=1)
            o_ref[s] = shard
            ob_ref[s] = shard.astype(BF16)

    spec = pl.BlockSpec((4, tm, W_IN_SHARD), lambda i: (0, i, 0))
    return pl.pallas_call(
        body, name=name, grid=(D // tm,), in_specs=[pl.BlockSpec((tm, P_END), lambda i: (i, 0))],
        out_specs=[spec, spec],
        out_shape=[jax.ShapeDtypeStruct((4, D, W_IN_SHARD), g.dtype), jax.ShapeDtypeStruct((4, D, W_IN_SHARD), BF16)],
        compiler_params=_par(1),
    )(g)


def _adam_update(w, g, m, v):
    mn = ADAM_B1 * m + (1.0 - ADAM_B1) * g
    vn = ADAM_B2 * v + (1.0 - ADAM_B2) * (g * g)
    m_hat = mn / (1.0 - ADAM_B1 ** ADAM_STEP)
    v_hat = vn / (1.0 - ADAM_B2 ** ADAM_STEP)
    return -ADAM_LR * (m_hat / (jnp.sqrt(v_hat) + ADAM_EPS) + ADAM_WD * w), mn, vn


def _adamw(w, g, m, v, name):
    shape = w.shape
    C = shape[-1]
    R = w.size // C
    tm = _pick(R, 512, 8)
    spec = pl.BlockSpec((tm, C), lambda i: (i, 0))

    def body(w_ref, g_ref, m_ref, v_ref, d_ref, mo_ref, vo_ref):
        d_ref[...], mo_ref[...], vo_ref[...] = _adam_update(w_ref[...], g_ref[...], m_ref[...], v_ref[...])

    outs = pl.pallas_call(
        body, name=name, grid=(R // tm,), in_specs=[spec] * 4, out_specs=[spec] * 3,
        out_shape=[jax.ShapeDtypeStruct((R, C), F32)] * 3, compiler_params=_par(1),
    )(*(t.reshape(R, C) for t in (w, g, m, v)))
    return tuple(o.reshape(shape) for o in outs)


def _adamw_lead(w, g, m, v, name, tl):
    A, B, C = w.shape
    spec = pl.BlockSpec((tl, B, C), lambda i: (i, 0, 0))

    def body(w_ref, g_ref, m_ref, v_ref, d_ref, mo_ref, vo_ref):
        d_ref[...], mo_ref[...], vo_ref[...] = _adam_update(w_ref[...], g_ref[...], m_ref[...], v_ref[...])

    return pl.pallas_call(
        body, name=name, grid=(A // tl,), in_specs=[spec] * 4, out_specs=[spec] * 3,
        out_shape=[jax.ShapeDtypeStruct((A, B, C), F32)] * 3, compiler_params=_par(1),
    )(w, g, m, v)


def _adamw_layers(w, gs, m, v, name):
    L, R, C = w.shape
    tm = _pick(R, 128, 8)
    spec = pl.BlockSpec((None, tm, C), lambda l, i: (l, i, 0))
    g_specs = [pl.BlockSpec((tm, C), functools.partial(lambda ll, l, i: (jnp.where(l == ll, i, 0), 0), ll))
               for ll in range(L)]

    def body(w_ref, m_ref, v_ref, *rest):
        g_refs, (go_ref, d_ref, mo_ref, vo_ref) = rest[:L], rest[L:]
        l = pl.program_id(0)
        for ll in range(L):
            @pl.when(l == ll)
            def _():
                g = g_refs[ll][...]
                go_ref[...] = g
                d_ref[...], mo_ref[...], vo_ref[...] = _adam_update(w_ref[...], g, m_ref[...], v_ref[...])

    return pl.pallas_call(
        body, name=name, grid=(L, R // tm), in_specs=[spec] * 3 + g_specs, out_specs=[spec] * 4,
        out_shape=[jax.ShapeDtypeStruct((L, R, C), F32)] * 4, compiler_params=_arb(2),
    )(w, m, v, *gs)


def _pair_sums(a, where, b, name):
    NB, _, R, C = a.shape

    def body(where_ref, a_ref, b_ref, p_ref, own_ref):
        s = a_ref[...] + b_ref[...].astype(F32)
        p_ref[...] = s.astype(BF16)

        @pl.when(pl.program_id(0) == where_ref[1])
        def _():
            own_ref[...] = s

    return pl.pallas_call(
        body, name=name,
        grid_spec=pltpu.PrefetchScalarGridSpec(
            num_scalar_prefetch=1, grid=(NB,),
            in_specs=[pl.BlockSpec((None, None, R, C), lambda k, w: (k, w[0], 0, 0)),
                      pl.BlockSpec((None, R, C), lambda k, w: (k, 0, 0))],
            out_specs=[pl.BlockSpec((None, R, C), lambda k, w: (k, 0, 0)), pl.BlockSpec((R, C), lambda k, w: (0, 0))]),
        out_shape=[jax.ShapeDtypeStruct((NB, R, C), BF16), jax.ShapeDtypeStruct((R, C), F32)],
        compiler_params=_arb(1),
    )(where, a, b)


def _sum_own_and_received(own, recv, where, name):
    R, C = own.shape
    tm = _pick(R, 256, 16)

    def body(where_ref, p_ref, r_ref, o_ref):
        o_ref[...] = ((p_ref[...] + r_ref[0].astype(F32)) + r_ref[1].astype(F32)) + r_ref[2].astype(F32)

    return pl.pallas_call(
        body, name=name,
        grid_spec=pltpu.PrefetchScalarGridSpec(
            num_scalar_prefetch=1, grid=(R // tm,),
            in_specs=[pl.BlockSpec((tm, C), lambda i, w: (i, 0)), pl.BlockSpec((3, tm, C), lambda i, w: (0, i, 0))],
            out_specs=pl.BlockSpec((None, tm, C), lambda i, w: (w[0], i, 0))),
        out_shape=jax.ShapeDtypeStruct((2, R, C), F32), compiler_params=_par(1),
    )(where, own, recv)


def _position():
    return lax.axis_index("x"), lax.axis_index("y"), lax.axis_index("c")


def _other_chips(x, y):
    return [(1 - x, y), (x, 1 - y), (1 - x, 1 - y)]


HBM = pl.BlockSpec(memory_space=pl.ANY)


def _allgather8(blk, name, reduce_rows=None):
    M, N = blk.shape

    def body(x_ref, out_ref, *rest):
        if reduce_rows is None:
            send_sems, recv_sems, local_sem = rest
        else:
            sum_ref, send_sems, recv_sems, local_sem = rest
        x, y, c = _position()
        me, sibling = (x, y, c), (x, y, 1 - c)
        chips = _other_chips(x, y)

        def rows(px, py, pc):
            return out_ref.at[pl.ds((4 * px + 2 * py + pc) * M, M), :]

        def copy(k, block, to, src=None):
            return pltpu.make_async_remote_copy(
                src_ref=rows(*block) if src is None else src, dst_ref=rows(*block),
                send_sem=send_sems.at[k], recv_sem=recv_sems.at[k], device_id=to, device_id_type=MESH)

        mine = pltpu.make_async_copy(x_ref, rows(*me), local_sem)
        mine.start()
        first = [copy(0, me, sibling, src=x_ref)]
        first += [copy(1 + j, me, (*chip, c), src=x_ref) for j, chip in enumerate(chips)]
        for cp in first:
            cp.start()
        passed = [copy(4 + j, (*chip, c), sibling) for j, chip in enumerate(chips)]
        for j, chip in enumerate(chips):
            copy(1 + j, (*chip, c), me).wait_recv()
            passed[j].start()
        copy(0, sibling, me).wait_recv()
        for j, chip in enumerate(chips):
            copy(4 + j, (*chip, 1 - c), me).wait_recv()
        for cp in first + passed:
            cp.wait_send()
        mine.wait()
        if reduce_rows is not None:
            tot = out_ref[pl.ds(0, reduce_rows), :]
            for d in range(1, 8):
                tot = tot + out_ref[pl.ds(d * M, reduce_rows), :]
            sum_ref[...] = tot

    vmem = pl.BlockSpec(memory_space=pltpu.VMEM)
    out_shape = [jax.ShapeDtypeStruct((8 * M, N), blk.dtype)]
    if reduce_rows is not None:
        out_shape.append(jax.ShapeDtypeStruct((reduce_rows, N), blk.dtype))
    res = pl.pallas_call(
        body, name=name, out_shape=out_shape, in_specs=[vmem], out_specs=[vmem] * len(out_shape),
        scratch_shapes=[pltpu.SemaphoreType.DMA((7,)), pltpu.SemaphoreType.DMA((7,)), pltpu.SemaphoreType.DMA],
    )(blk)
    return res[0] if reduce_rows is None else (res[0], res[1])


def _gather_phases(layer, srcs, outs, send_sems, recv_sems, local_sems):
    n = len(srcs)
    x, y, c = _position()
    me, sibling = (x, y, c), (x, y, 1 - c)
    chips = _other_chips(x, y)

    def region(t, px, py, pc):
        return outs[t].at[2 * px + py, pc]

    def copy(t, k, block, to, own=False):
        return pltpu.make_async_remote_copy(
            src_ref=srcs[t].at[layer, c] if own else region(t, *block), dst_ref=region(t, *block),
            send_sem=send_sems.at[7 * t + k], recv_sem=recv_sems.at[7 * t + k], device_id=to, device_id_type=MESH)

    def local(t):
        return pltpu.make_async_copy(srcs[t].at[layer, c], region(t, *me), local_sems.at[t])

    def first(t):
        return [copy(t, 0, me, sibling, own=True)] + [copy(t, 1 + j, me, (*chip, c), own=True)
                                                       for j, chip in enumerate(chips)]

    def start():
        for t in range(n):
            local(t).start()
        for t in range(n):
            for cp in first(t):
                cp.start()

    def forward():
        for j, chip in enumerate(chips):
            for t in range(n):
                copy(t, 1 + j, (*chip, c), me).wait_recv()
                copy(t, 4 + j, (*chip, c), sibling).start()

    def finish():
        for t in range(n):
            copy(t, 0, sibling, me).wait_recv()
        for j, chip in enumerate(chips):
            for t in range(n):
                copy(t, 4 + j, (*chip, 1 - c), me).wait_recv()
        for t in range(n):
            for cp in first(t) + [copy(t, 4 + j, (*chip, c), sibling) for j, chip in enumerate(chips)]:
                cp.wait_send()
            local(t).wait()

    return start, forward, finish


def _gather_scratch(n):
    return [pltpu.SemaphoreType.DMA((7 * n,)), pltpu.SemaphoreType.DMA((7 * n,)), pltpu.SemaphoreType.DMA((n,))]


def _gather_out_shapes(shards):
    return [jax.ShapeDtypeStruct((4,) + s.shape[1:], s.dtype) for s in shards]


def _gather_weights(shards, layer, name):
    n = len(shards)

    def body(*refs):
        start, forward, finish = _gather_phases(layer, refs[:n], refs[n:2 * n], *refs[2 * n:])
        start()
        forward()
        finish()

    return pl.pallas_call(
        body, name=name, out_shape=_gather_out_shapes(shards), in_specs=[HBM] * n, out_specs=[HBM] * n,
        scratch_shapes=_gather_scratch(n),
    )(*shards)


def _rs_sibling(gs, name):
    n = len(gs)

    def body(*refs):
        srcs, outs = refs[:n], refs[n:2 * n]
        send_sems, recv_sems = refs[2 * n:]
        x, y, c = _position()
        copies = [pltpu.make_async_remote_copy(
            src_ref=srcs[t].at[k, 1 - c], dst_ref=outs[t].at[k], send_sem=send_sems.at[4 * t + k],
            recv_sem=recv_sems.at[4 * t + k], device_id=(x, y, 1 - c), device_id_type=MESH)
            for t in range(n) for k in range(4)]
        for cp in copies:
            cp.start()
        for cp in copies:
            cp.wait()

    out_shape = [jax.ShapeDtypeStruct((4,) + g.shape[2:], g.dtype) for g in gs]
    return pl.pallas_call(
        body, name=name, out_shape=out_shape, in_specs=[HBM] * n, out_specs=[HBM] * n,
        scratch_shapes=[pltpu.SemaphoreType.DMA((4 * n,)), pltpu.SemaphoreType.DMA((4 * n,))],
    )(*gs)


def _rs_chips(ps, name):
    n = len(ps)

    def body(*refs):
        start, finish = _chips_phases(refs[:n], refs[n:2 * n], *refs[2 * n:])
        start()
        finish()

    return pl.pallas_call(
        body, name=name, out_shape=_chips_out_shapes(ps), in_specs=[HBM] * n, out_specs=[HBM] * n,
        scratch_shapes=_chips_scratch(n),
    )(*ps)


def _chips_phases(srcs, outs, send_sems, recv_sems):
    x, y, c = _position()
    copies = [pltpu.make_async_remote_copy(
        src_ref=srcs[t].at[2 * px + py], dst_ref=outs[t].at[j], send_sem=send_sems.at[3 * t + j],
        recv_sem=recv_sems.at[3 * t + j], device_id=(px, py, c), device_id_type=MESH)
        for t in range(len(srcs)) for j, (px, py) in enumerate(_other_chips(x, y))]

    def start():
        for cp in copies:
            cp.start()

    def finish():
        for cp in copies:
            cp.wait()

    return start, finish


def _chips_scratch(n):
    return [pltpu.SemaphoreType.DMA((3 * n,)), pltpu.SemaphoreType.DMA((3 * n,))]


def _chips_out_shapes(ps):
    return [jax.ShapeDtypeStruct((3,) + p.shape[1:], p.dtype) for p in ps]


def _rs_pair(hs, name):
    n = len(hs)

    def body(*refs):
        bufs = refs[n:2 * n]
        send_sems, recv_sems = refs[2 * n:]
        x, y, c = _position()

        def copy(t, half):
            return pltpu.make_async_remote_copy(
                src_ref=bufs[t].at[half], dst_ref=bufs[t].at[half], send_sem=send_sems.at[t], recv_sem=recv_sems.at[t],
                device_id=(x, y, 1 - c), device_id_type=MESH)

        for t in range(n):
            copy(t, c).start()
        for t in range(n):
            copy(t, 1 - c).wait_recv()
        for t in range(n):
            copy(t, c).wait_send()

    out_shape = [jax.ShapeDtypeStruct(h.shape, h.dtype) for h in hs]
    return pl.pallas_call(
        body, name=name, out_shape=out_shape, in_specs=[HBM] * n, out_specs=[HBM] * n,
        input_output_aliases={t: t for t in range(n)},
        scratch_shapes=[pltpu.SemaphoreType.DMA((n,)), pltpu.SemaphoreType.DMA((n,))],
    )(*hs)


BIG = ("w_in", "w_branch_a", "w_branch_b", "w_out", "w_ffn_in", "w_ffn_out")
CARRY_ATTN = ["w_in"]
CARRY_LOCAL = ["w_ffn_out"]
CARRY_SCAN = ["w_branch_a", "w_branch_b", "w_out"]
CARRY_GU = ["w_ffn_in"]
CARRY_DATTN = ["w_in", "w_ffn_in"]
CARRY_DLOCAL = ["w_branch_a", "w_branch_b", "w_out", "w_ffn_out"]


def _band_bias(rel_table, name, gather=None):
    L, H, n = rel_table.shape
    tab = jnp.pad(rel_table, ((0, 0), (0, 0), (0, NREL_PAD - n))).reshape(L * H, 1, NREL_PAD)
    band = (A_PAST + 1) * CHUNK

    shards, glayer = gather if gather is not None else ((), None)
    ng = len(shards)

    def body(t_ref, *rest):
        srcs, o_ref, gouts, sems = rest[:ng], rest[ng], rest[ng + 1:2 * ng + 1], rest[2 * ng + 1:]
        i = pl.program_id(0)
        done = (_carry(_gather_phases(glayer, srcs, gouts, *sems), i == 0, i == LATE * (L * H) // 8, i == L * H - 1)
                if ng else None)
        r = lax.broadcasted_iota(jnp.int32, (NREL_PAD, SKEW_W), 0)
        xi = lax.broadcasted_iota(jnp.int32, (NREL_PAD, SKEW_W), 1)
        diag = jnp.where(xi < KSPAN, xi, xi - SKEW_W)
        rel = jnp.clip(A_PAST * CHUNK - diag, -A_MAX_REL, A_MAX_REL) + A_MAX_REL
        e = _nn(t_ref[...], jnp.where(rel == r, 1.0, 0.0).astype(F32), HI)
        x = jnp.broadcast_to(e, (QBLK, SKEW_W))
        row = lax.broadcasted_iota(jnp.int32, (QBLK, SKEW_W), 0)
        for b in range(QBLK.bit_length() - 1):
            x = jnp.where(((row >> b) & 1) == 1, pltpu.roll(x, 1 << b, 1), x)
        x = x[:, :KSPAN]
        first = (lax.broadcasted_iota(jnp.int32, (QBLK, KSPAN), 0) // CHUNK) * CHUNK
        col = lax.broadcasted_iota(jnp.int32, (QBLK, KSPAN), 1)
        o_ref[...] = jnp.where((col >= first) & (col < first + band), x, NEG)
        if ng:
            done()

    res = pl.pallas_call(
        body, name=name, grid=(L * H,), in_specs=[pl.BlockSpec((None, 1, NREL_PAD), lambda i: (i, 0, 0))] + [HBM] * ng,
        out_specs=[pl.BlockSpec((None, QBLK, KSPAN), lambda i: (i, 0, 0))] + [HBM] * ng,
        out_shape=[jax.ShapeDtypeStruct((L * H, QBLK, KSPAN), F32)] + _gather_out_shapes(shards),
        scratch_shapes=_gather_scratch(ng) if ng else [], compiler_params=_arb(1),
    )(tab, *shards)
    return res[0].reshape(L, H, QBLK, KSPAN), list(res[1:])


def _col_row_forms(t, S):
    nc = S // CHUNK
    return t.T.reshape(nc, CHUNK, B_HEADS), t.reshape(B_HEADS, nc, CHUNK).transpose(1, 0, 2)


def _weight_view(name, gathered, tag):
    if name in ("w_out", "w_ffn_out"):
        return gathered.reshape(8 * gathered.shape[2], gathered.shape[3])
    stacked = gathered.reshape(4, 2 * gathered.shape[2], gathered.shape[3])
    return _reorder_w_in(stacked[None], f"w_in_cols_{tag}")[0] if name == "w_in" else stacked


def _layer_fwd(l, x, mod, W, P, big, gather=None, late=None):
    S, D = x.shape
    n = lambda s: f"{s}_l{l}"
    sh1, sc1, gt1, sh2, sc2, gt2 = (mod[i:i + 1] for i in range(6))
    h1 = _lnmod_fwd(x, P["norm1_g"][l:l + 1], sc1, sh1, n("ln1"))
    if late is None:
        proj = _matmul(h1, W["w_in"], "nn", F32, n("proj"), tn=1152)
    else:
        proj, got = _matmul(h1, W["w_in"], "nn", F32, n("proj"), tn=1152, gather=(late[1], l))
        W = {**W, **{k: _weight_view(k, t, f"l{l}") for k, t in zip(late[0], got)}}
    part = (lambda names: ([gather[0][BIG.index(k)] for k in names], gather[1])) if gather is not None else (lambda names: None)
    ya, got_a = _attn_fwd(proj, big, n("attn"), part(CARRY_ATTN))
    ba = proj[:, P_BA:P_BA + 2 * B_HEADS]
    b_t, a_t = ba[:, :B_HEADS].T, ba[:, B_HEADS:].T
    alog, dtb = P["a_log"][l].reshape(B_HEADS, 1), P["dt_bias"][l].reshape(B_HEADS, 1)
    beta, gam = _gdn_gates_fwd(b_t, a_t, alog, dtb, n("gates"))
    bcol, _ = _col_row_forms(beta, S)
    gcol, grow = _col_row_forms(gam, S)
    qn, kn, v = _gdn_pre_fwd(proj, P["w_conv"][l], n("gdnpre"))
    tsave, amat, u, w, got_l = _gdn_local_fwd(qn, kn, v, bcol, gcol, grow, n("gdnlocal"), part(CARRY_LOCAL))
    o, ssave, got_s = _gdn_scan_fwd(qn, kn, u, w, amat, gcol, grow, n("gdnscan"), part(CARRY_SCAN))
    yb = _gdn_post_fwd(o, proj, P["gdn_norm_g"][l:l + 1], n("gdnpost"))
    pa = _matmul(ya, W["w_branch_a"], "nn", BF16, n("pa"), tm=2048, stacked=True)
    pb = _matmul(yb, W["w_branch_b"], "nn", BF16, n("pb"), tm=2048, stacked=True)
    merged = _merge_fwd(proj, pa, pb, n("merge"))
    ao = _matmul(merged, W["w_out"], "nn", F32, n("ao"))
    x1 = _gate_fwd(x, ao, gt1, n("res1"))
    h2 = _lnmod_fwd(x1, P["norm2_g"][l:l + 1], sc2, sh2, n("ln2"))
    gu = _matmul(h2, W["w_ffn_in"], "nn", BF16, n("gu"), stacked=True, gather=part(CARRY_GU))
    gu, got_g = gu if gather is not None else (gu, [])
    got = dict(zip(CARRY_ATTN + CARRY_LOCAL + CARRY_SCAN + CARRY_GU, got_a + got_l + got_s + got_g))
    gathered = {k: got[k] for k in BIG} if gather is not None else None
    act = _ffn_act_fwd(gu, n("act"))
    fo = _matmul(act, W["w_ffn_out"], "nn", F32, n("fo"), tk=1408)
    x2 = _gate_fwd(x1, fo, gt2, n("res2"))
    saved = dict(x=x, h1=h1, proj=proj, ya=ya, b_t=b_t, a_t=a_t, bcol=bcol, gcol=gcol, grow=grow,
                 qn=qn, kn=kn, v=v, o=o, tsave=tsave, ssave=ssave, amat=amat, u=u, w=w, yb=yb, pa=pa, pb=pb,
                 merged=merged, ao=ao, x1=x1,
                 h2=h2, gu=gu, act=act, fo=fo)
    return x2, saved, gathered, W


def _layer_bwd(l, dx2, sv, mod, W, P, big, exchange=()):
    S, D = dx2.shape
    n = lambda s: f"{s}_l{l}"
    sh1, sc1, gt1, sh2, sc2, gt2 = (mod[i:i + 1] for i in range(6))
    g, pay = {}, {}
    view = lambda t: t.reshape((4, 2, t.shape[-2] // (2 if t.ndim == 3 else 8), t.shape[-1]))
    dz2, dgt2 = _gate_bwd(dx2, sv["fo"], gt2, n("dres2"))
    g["w_ffn_out"], pay["w_ffn_out"] = map(view, _matmul(sv["act"], dz2, "tn", F32, n("dwfo"), tm=1408, also_bf16=True))
    dact = _matmul(dz2, W["w_ffn_out"], "nt", BF16, n("dact"), tn=1408)
    dgu = _ffn_act_bwd(sv["gu"], dact, n("dgu"))
    g["w_ffn_in"], pay["w_ffn_in"] = map(view, _matmul(sv["h2"], dgu, "tn", F32, n("dwfi"), out_stacked=True,
                                                       also_bf16=True))
    dh2 = _matmul(dgu, W["w_ffn_in"], "nt", F32, n("dh2"), stacked=True)
    dx1, dsh2, dsc2, dn2 = _lnmod_bwd(dh2, sv["x1"], P["norm2_g"][l:l + 1], sc2, dx2, n("dln2"))
    dz1, dgt1 = _gate_bwd(dx1, sv["ao"], gt1, n("dres1"))
    g["w_out"], pay["w_out"] = map(view, _matmul(sv["merged"], dz1, "tn", F32, n("dwo"), also_bf16=True))
    dmerged = _matmul(dz1, W["w_out"], "nt", BF16, n("dmerged"))
    dproj, dpa, dpb = _merge_bwd(sv["proj"], sv["pa"], sv["pb"], dmerged, n("dmerge"))
    g["w_branch_a"], pay["w_branch_a"] = map(view, _matmul(sv["ya"], dpa, "tn", F32, n("dwa"), out_stacked=True,
                                                           also_bf16=True))
    g["w_branch_b"], pay["w_branch_b"] = map(view, _matmul(sv["yb"], dpb, "tn", F32, n("dwb"), out_stacked=True,
                                                           also_bf16=True))
    dya = _matmul(dpa, W["w_branch_a"], "nt", BF16, n("dya"), tm=2048, stacked=True)
    dyb = _matmul(dpb, W["w_branch_b"], "nt", F32, n("dyb"), tm=2048, stacked=True)
    ex = (lambda names: [exchange[BIG.index(k)] for k in names]) if len(exchange) else (lambda names: ())
    dq, dk, dv, dbig, rec_a = _attn_bwd(sv["proj"], big, dya, n("dattn"), ex(CARRY_DATTN))
    g["rel_table"] = _rel_table_grad(dbig, n("drel"))[:, 0, :2 * A_MAX_REL + 1]
    dproj = _write_columns(dproj, [dq, dk, dv], 3 * dq.shape[1], P_QKVA // (3 * dq.shape[1]), n("dqkva"))
    do, dproj, dng = _gdn_post_bwd(dyb, sv["o"], sv["proj"], P["gdn_norm_g"][l:l + 1], dproj, n("dgdnpost"))
    g["gdn_norm_g"] = dng[0]
    du, dw, dqd, dkd, da, dgl = _gdn_scan_bwd(sv["qn"], sv["kn"], sv["u"], sv["w"], sv["amat"], sv["gcol"], sv["grow"],
                                              sv["ssave"], do, n("dgdnscan"))
    dqn, dkn, dvv, dbc, dgc, dgr, rec_l = _gdn_local_bwd(
        sv["qn"], sv["kn"], sv["v"], sv["bcol"], sv["gcol"], sv["grow"], sv["tsave"], du, dw, dqd, dkd, da, dgl,
        n("dgdnlocal"), ex(CARRY_DLOCAL))
    rec = dict(zip(CARRY_DATTN + CARRY_DLOCAL, rec_a + rec_l))
    received = [rec[k] for k in BIG] if len(exchange) else None
    dbeta_t = dbc.reshape(S, B_HEADS).T
    dgam_a = dgc.reshape(S, B_HEADS).T
    dgam_b = dgr.transpose(1, 0, 2).reshape(B_HEADS, S)
    alog, dtb = P["a_log"][l].reshape(B_HEADS, 1), P["dt_bias"][l].reshape(B_HEADS, 1)
    db_t, da_t, dal, ddt = _gdn_gates_bwd(dbeta_t, dgam_a, dgam_b, sv["b_t"], sv["a_t"], alog, dtb, n("dgates"))
    g["a_log"], g["dt_bias"] = dal[:, 0], ddt[:, 0]
    dy = _gdn_pre_bwd_a(sv["proj"], P["w_conv"][l], dqn, dkn, dvv, n("dgdnpre_a"))
    dproj, g["w_conv"] = _gdn_pre_bwd_b(sv["proj"], P["w_conv"][l], dy, dproj, n("dgdnpre_b"))
    dba = jnp.concatenate([db_t.T, da_t.T, jnp.zeros((S, P_END - P_BA - 2 * B_HEADS), F32)], axis=1)
    dproj = _write_columns(dproj, [dba], dba.shape[1], P_BA // dba.shape[1], n("dba"))
    g["w_in"], pay["w_in"] = map(view, _restore_w_in(_matmul(sv["h1"], dproj, "tn", F32, n("dwin"), tn=1152),
                                                     n("dwin_cols")))
    dh1 = _matmul(dproj, W["w_in"], "nt", F32, n("dh1"), tk=1152)
    dx, dsh1, dsc1, dn1 = _lnmod_bwd(dh1, sv["x"], P["norm1_g"][l:l + 1], sc1, dx1, n("dln1"))
    g["norm1_g"], g["norm2_g"] = dn1[0], dn2[0]
    dmod = jnp.concatenate([dsh1, dsc1, dgt1, dsh2, dsc2, dgt2], axis=1)[0]
    return dx, g, pay, dmod, received


SMALL = ("norm1_g", "norm2_g", "rel_table", "w_conv", "a_log", "dt_bias", "gdn_norm_g")
SMALL_PACK_C = 1024


def _as_rows(t):
    flat = t.reshape(-1)
    rows = -(-flat.shape[0] // SMALL_PACK_C)
    return jnp.pad(flat, (0, rows * SMALL_PACK_C - flat.shape[0])).reshape(rows, SMALL_PACK_C)


def _pack_rows(parts):
    blk = jnp.concatenate([_as_rows(p) for p in parts], axis=0)
    return jnp.pad(blk, ((0, -blk.shape[0] % 8), (0, 0)))


def _unpack_rows(blk, shapes):
    out, r = [], 0
    for shp in shapes:
        size = int(np.prod(shp))
        rows = -(-size // SMALL_PACK_C)
        out.append(blk[..., r:r + rows, :].reshape(blk.shape[:-2] + (rows * SMALL_PACK_C,))[..., :size]
                   .reshape(blk.shape[:-2] + tuple(shp)))
        r += rows
    return out


def kernel(x, c, w_ada, b_ada, norm1_g, norm2_g, w_in, rel_table, w_conv, a_log, dt_bias, gdn_norm_g, w_branch_a, w_branch_b, w_out, w_ffn_in, w_ffn_out, final_g, loss_target, m_w_ada, m_b_ada, m_norm1_g, m_norm2_g, m_w_in, m_rel_table, m_w_conv, m_a_log, m_dt_bias, m_gdn_norm_g, m_w_branch_a, m_w_branch_b, m_w_out, m_w_ffn_in, m_w_ffn_out, m_final_g, v_w_ada, v_b_ada, v_norm1_g, v_norm2_g, v_w_in, v_rel_table, v_w_conv, v_a_log, v_dt_bias, v_gdn_norm_g, v_w_branch_a, v_w_branch_b, v_w_out, v_w_ffn_in, v_w_ffn_out, v_final_g):
    weights = dict(w_ada=w_ada, b_ada=b_ada, norm1_g=norm1_g, norm2_g=norm2_g, w_in=w_in, rel_table=rel_table,
                   w_conv=w_conv, a_log=a_log, dt_bias=dt_bias, gdn_norm_g=gdn_norm_g, w_branch_a=w_branch_a,
                   w_branch_b=w_branch_b, w_out=w_out, w_ffn_in=w_ffn_in, w_ffn_out=w_ffn_out, final_g=final_g)
    mom_m = dict(w_ada=m_w_ada, b_ada=m_b_ada, norm1_g=m_norm1_g, norm2_g=m_norm2_g, w_in=m_w_in,
                 rel_table=m_rel_table, w_conv=m_w_conv, a_log=m_a_log, dt_bias=m_dt_bias, gdn_norm_g=m_gdn_norm_g,
                 w_branch_a=m_w_branch_a, w_branch_b=m_w_branch_b, w_out=m_w_out, w_ffn_in=m_w_ffn_in,
                 w_ffn_out=m_w_ffn_out, final_g=m_final_g)
    mom_v = dict(w_ada=v_w_ada, b_ada=v_b_ada, norm1_g=v_norm1_g, norm2_g=v_norm2_g, w_in=v_w_in,
                 rel_table=v_rel_table, w_conv=v_w_conv, a_log=v_a_log, dt_bias=v_dt_bias, gdn_norm_g=v_gdn_norm_g,
                 w_branch_a=v_w_branch_a, w_branch_b=v_w_branch_b, w_out=v_w_out, w_ffn_in=v_w_ffn_in,
                 w_ffn_out=v_w_ffn_out, final_g=v_final_g)
    xi, yi, ci = _position()
    chip = 2 * xi + yi
    dev = 2 * chip + ci
    L, D = norm1_g.shape
    NMOD = b_ada.shape[1] // D
    ns = w_ada.shape[2]
    cs = w_conv.shape[2]

    first_blk = _pack_rows([c, w_conv])
    first_all = _allgather8(first_blk, "gather_c").reshape(8, first_blk.shape[0], SMALL_PACK_C)
    c_all, w_conv_all = _unpack_rows(first_all, [(D,), w_conv.shape])
    w_conv_full = w_conv_all.reshape(4, 2, L, CONV_K, cs)[:, 0].transpose(1, 2, 0, 3).reshape(L, CONV_K, 4 * cs)
    b_shard = lax.dynamic_slice_in_dim(b_ada, chip * ns, ns, axis=1).reshape(L, 1, ns)
    mod_shard = _ada_mod(c_all, w_ada, b_shard, "ada_mod")
    mod_all = _allgather8(mod_shard.reshape(L * 8, ns), "gather_mod").reshape(4, 2, L, 8, ns)
    mod = lax.dynamic_index_in_dim(mod_all[:, 0], dev, axis=2, keepdims=False)
    mod = mod.transpose(1, 0, 2).reshape(L, NMOD, D)

    shards = [weights[k].astype(BF16) for k in BIG]
    shards = [s.reshape(s.shape[0], 2, s.shape[1] // 2, s.shape[2]) for s in shards]
    P = dict(norm1_g=norm1_g, norm2_g=norm2_g, w_conv=w_conv_full, a_log=a_log, dt_bias=dt_bias,
             gdn_norm_g=gdn_norm_g)
    shard_of = dict(zip(BIG, shards))

    big, got = _band_bias(rel_table, "band_bias", ([shard_of["w_in"]], 0))
    alone = ["w_branch_a", "w_branch_b", "w_out", "w_ffn_out"]
    got += _gather_weights([shard_of[k] for k in alone], 0, "gather_weights_l0")
    W = [{k: _weight_view(k, t, "l0") for k, t in zip(["w_in"] + alone, got)}]
    late = (["w_ffn_in"], [shard_of["w_ffn_in"]])
    xc = x[0]
    saved = []
    for l in range(L):
        xc, sv, gathered, W[l] = _layer_fwd(l, xc, mod[l], W[l], P, big[l], (shards, l + 1) if l + 1 < L else None,
                                           late if l == 0 else None)
        saved.append(sv)
        if l + 1 < L:
            W.append({k: _weight_view(k, gathered[k], f"l{l + 1}") for k in BIG})
    dx, loss_dev, dfinal = _loss_head(xc, final_g.reshape(1, D), loss_target[0], "loss_head")

    where = jnp.stack([ci, chip]).astype(jnp.int32)
    grads = [None] * L
    dmods = [None] * L
    shard_grads = {k: [None] * L for k in BIG}

    def finish_reduce_scatter(l, sums, from_chips):
        halves = [_sum_own_and_received(s_[1], r_, where, f"rs_sum_{k}_l{l}")
                  for k, s_, r_ in zip(BIG, sums, from_chips)]
        for k, t in zip(BIG, _rs_pair(halves, f"rs_pair_l{l}")):
            shard_grads[k][l] = t.reshape(2 * t.shape[1], t.shape[2])

    pending = None
    for l in reversed(range(L)):
        exchange = [s_[0] for s_ in pending] if pending is not None else ()
        dx, grads[l], pay, dmods[l], received = _layer_bwd(l, dx, saved[l], mod[l], W[l], P, big[l], exchange)
        if pending is not None:
            finish_reduce_scatter(l + 1, pending, received)
        gs = [grads[l][k] for k in BIG]
        from_sibling = _rs_sibling([pay[k] for k in BIG], f"rs_sibling_l{l}")
        pending = [_pair_sums(g_, where, r_, f"rs_pair_sum_{k}_l{l}") for k, g_, r_ in zip(BIG, gs, from_sibling)]
    finish_reduce_scatter(0, pending, _rs_chips([s_[0] for s_ in pending], "rs_chips_l0"))
    dmod = jnp.stack(dmods)

    small = {k: jnp.stack([grads[l][k] for l in range(L)]) for k in SMALL}
    parts = [dmod] + [small[k] for k in SMALL] + [dfinal, loss_dev[0, :1]]
    small_blk = _pack_rows(parts)
    srows = small_blk.shape[0]
    small_all, small_sum = _allgather8(small_blk, "gather_small", reduce_rows=srows)
    shapes = [dmod.shape] + [small[k].shape for k in SMALL] + [(D,), (1,)]
    tot = _unpack_rows(small_sum, shapes)
    G = dict(zip(SMALL, tot[1:1 + len(SMALL)]))
    G["b_ada"] = tot[0].reshape(b_ada.shape)
    G["w_conv"] = lax.dynamic_slice_in_dim(G["w_conv"], chip * cs, cs, axis=2)
    G["final_g"] = tot[-2]
    loss = tot[-1][0]
    dmod_all = _unpack_rows(small_all.reshape(8, srows, SMALL_PACK_C), [dmod.shape])[0]
    dmod_cols = lax.dynamic_slice_in_dim(dmod_all, chip * ns, ns, axis=2).transpose(1, 0, 2)
    G["w_ada"] = _ada_wgrad(c_all, dmod_cols, "ada_wgrad")

    order = ["w_ada", "b_ada", "norm1_g", "norm2_g", "w_in", "rel_table", "w_conv", "a_log", "dt_bias", "gdn_norm_g",
             "w_branch_a", "w_branch_b", "w_out", "w_ffn_in", "w_ffn_out", "final_g"]
    deltas, new_m, new_v = {}, {}, {}
    for k in order:
        w = weights[k]
        if k == "w_in":
            to_cols = lambda t: jnp.transpose(t, (2, 0, 1))
            from_cols = lambda t: jnp.transpose(t, (1, 2, 0))
            gt = to_cols(jnp.stack(shard_grads[k]))
            d_, m_, v_ = _adamw_lead(to_cols(w), gt, to_cols(mom_m[k]), to_cols(mom_v[k]), f"adamw_{k}",
                                     W_IN_SHARD // 30)
            G[k], deltas[k], new_m[k], new_v[k] = from_cols(gt), from_cols(d_), from_cols(m_), from_cols(v_)
            continue
        if k in BIG:
            G[k], deltas[k], new_m[k], new_v[k] = _adamw_layers(w, shard_grads[k], mom_m[k], mom_v[k], f"adamw_{k}")
            continue
        as2d = (lambda t: t.reshape(1, -1)) if w.ndim == 1 else (lambda t: t)
        d_, m_, v_ = _adamw(as2d(w), as2d(G[k]), as2d(mom_m[k]), as2d(mom_v[k]), f"adamw_{k}")
        deltas[k], new_m[k], new_v[k] = d_.reshape(w.shape), m_.reshape(w.shape), v_.reshape(w.shape)
    return (loss, dx[None], *[G[k] for k in order], *[deltas[k] for k in order], *[new_m[k] for k in order],
            *[new_v[k] for k in order])
```

```python
import functools

import numpy as np
import jax
import jax.numpy as jnp
from jax import lax
from jax.experimental import pallas as pl
from jax.experimental.pallas import tpu as pltpu

F32 = jnp.float32
BF16 = jnp.bfloat16
HI = lax.Precision.HIGHEST
SOLVE_PREC = lax.Precision.HIGH
MESH = pl.DeviceIdType.MESH

EPS = 1e-6
CHUNK = 64
A_HEADS = 8
A_DH = 64
A_PAST = 8
A_MAX_REL = 128
B_HEADS = 4
B_DH = 128
CONV_K = 4
LANE = 128
QBLK = 4 * CHUNK
KSPAN = QBLK + A_PAST * CHUNK
NEG = -1e30

ADAM_LR = 0.001
ADAM_B1 = 0.9
ADAM_B2 = 0.999
ADAM_EPS = 1e-08
ADAM_WD = 0.01
ADAM_STEP = 10

P_QKVA, P_QKVB, P_GA, P_GB, P_Z, P_BA, P_END = 0, 1536, 3072, 4096, 5120, 5632, 5760
W_IN_SHARD = 1410


def _sigmoid(x):
    return 1.0 / (1.0 + jnp.exp(-x))


def _dot(a, b, ca, cb, prec):
    lead = a.ndim - 2
    batch = ((0,), (0,)) if lead else ((), ())
    return lax.dot_general(a, b, (((ca + lead,), (cb + lead,)), batch), precision=prec, preferred_element_type=F32)


def _nn(a, b, prec=None):
    return _dot(a, b, 1, 0, prec)


def _nt(a, b, prec=None):
    return _dot(a, b, 1, 1, prec)


def _tn(a, b, prec=None):
    return _dot(a, b, 0, 0, prec)


def _bnn(a, b):
    return _nn(a.astype(BF16), b.astype(BF16))


def _bnt(a, b):
    return _nt(a.astype(BF16), b.astype(BF16))


def _btn(a, b):
    return _tn(a.astype(BF16), b.astype(BF16))


def _pick(n, target, unit=LANE):
    best = None
    for t in range(unit, min(n, target) + 1, unit):
        if n % t == 0:
            best = t
    return best if best is not None else n


def _acc(ref, val, i):
    @pl.when(i == 0)
    def _():
        ref[...] = val

    @pl.when(i != 0)
    def _():
        ref[...] += val


def _arb(n):
    return pltpu.CompilerParams(dimension_semantics=("arbitrary",) * n)


def _par(n):
    return pltpu.CompilerParams(dimension_semantics=("parallel",) * n)


def _matmul(a, b, mode, out_dtype, name, tm=1024, tn=1024, tk=1024, layer=None, stacked=False, out_stacked=False,
            also_bf16=False, gather=None, sibling=None):
    bs = b.shape[1:] if layer is not None else b.shape
    if mode == "nn":
        M, K = a.shape
        N = 4 * bs[2] if stacked else bs[1]
        if stacked:
            tn = bs[2]
    elif mode == "nt":
        M, K = a.shape
        N = bs[1] if stacked else bs[0]
        if stacked:
            tk = bs[2]
    else:
        K, M = a.shape
        N = bs[1]
        if out_stacked:
            tn = N // 4
    tm, tn, tk = _pick(M, tm), _pick(N, tn), _pick(K, tk)
    nk = K // tk
    lead = () if layer is None else (layer,)
    lead_blk = () if layer is None else (None,)
    if mode == "nn":
        a_spec = pl.BlockSpec((tm, tk), lambda i, j, k: (i, k))
        if stacked:
            b_spec = pl.BlockSpec(lead_blk + (None, tk, tn), lambda i, j, k: lead + (j, k, 0))
        else:
            b_spec = pl.BlockSpec(lead_blk + (tk, tn), lambda i, j, k: lead + (k, j))
        dot = _nn
    elif mode == "nt":
        a_spec = pl.BlockSpec((tm, tk), lambda i, j, k: (i, k))
        if stacked:
            b_spec = pl.BlockSpec(lead_blk + (None, tn, tk), lambda i, j, k: lead + (k, j, 0))
        else:
            b_spec = pl.BlockSpec(lead_blk + (tn, tk), lambda i, j, k: lead + (j, k))
        dot = _nt
    else:
        a_spec = pl.BlockSpec((tk, tm), lambda i, j, k: (k, i))
        b_spec = pl.BlockSpec((tk, tn), lambda i, j, k: (k, j))
        dot = _tn
    if out_stacked:
        o_spec = pl.BlockSpec((None, tm, tn), lambda i, j, k: (j, i, 0))
        o_shape = jax.ShapeDtypeStruct((4, M, tn), out_dtype)
    else:
        o_spec = pl.BlockSpec((tm, tn), lambda i, j, k: (i, j))
        o_shape = jax.ShapeDtypeStruct((M, N), out_dtype)

    if gather is not None:
        shards, glayer = gather
        carried_shapes, carried_scratch = _gather_out_shapes(shards), _gather_scratch(len(shards))
    elif sibling is not None:
        shards = sibling
        carried_shapes, carried_scratch = _sibling_out_shapes(shards), _sibling_scratch(len(shards))
    else:
        shards, carried_shapes, carried_scratch = (), [], []
    ng = len(shards)
    o_shapes = [o_shape] + ([jax.ShapeDtypeStruct(o_shape.shape, BF16)] if also_bf16 else [])
    no = len(o_shapes)
    gi, gj = M // tm, N // tn

    def write(o_refs, val):
        for o_ref in o_refs:
            o_ref[...] = val.astype(o_ref.dtype)

    def body(a_ref, b_ref, *refs):
        srcs, o_refs, gouts, scratch = refs[:ng], refs[ng:ng + no], refs[ng + no:2 * ng + no], refs[2 * ng + no:]
        i, j, k = pl.program_id(0), pl.program_id(1), pl.program_id(2)
        if ng:
            start = (j == 0) & (k == 0)
            sems = scratch[len(scratch) - len(carried_scratch):]
            phases = (_gather_phases(glayer, srcs, gouts, *sems) if gather is not None
                      else _sibling_phases(srcs, gouts, *sems))
            done = _carry(phases, (i == 0) & start, (i == gi - 1) & start,
                          (i == gi - 1) & (j == gj - 1) & (k == nk - 1))
        if nk == 1:
            write(o_refs, dot(a_ref[...], b_ref[...]))
        else:
            acc_ref = scratch[0]

            @pl.when(k == 0)
            def _():
                acc_ref[...] = jnp.zeros_like(acc_ref)

            acc_ref[...] += dot(a_ref[...], b_ref[...])

            @pl.when(k == nk - 1)
            def _():
                write(o_refs, acc_ref[...])
        if ng:
            done()

    sem = ("arbitrary",) * 3 if ng else ("parallel", "parallel", "arbitrary")
    res = pl.pallas_call(
        body, name=name, grid=(gi, gj, nk), in_specs=[a_spec, b_spec] + [HBM] * ng,
        out_specs=[o_spec] * no + [HBM] * ng, out_shape=o_shapes + carried_shapes,
        scratch_shapes=([] if nk == 1 else [pltpu.VMEM((tm, tn), F32)]) + carried_scratch,
        compiler_params=pltpu.CompilerParams(dimension_semantics=sem),
    )(a, b, *shards)
    out = tuple(res[:no]) if also_bf16 else res[0]
    return (out, list(res[no:])) if ng else out


def _rows(tm, n, col=0):
    return pl.BlockSpec((tm, n), lambda i: (i, col))


def _vec(n):
    return pl.BlockSpec((1, n), lambda i: (0, 0))


def _lnmod_fwd(x, g, sc, sh, name):
    S, D = x.shape
    tm = _pick(S, 512, 8)

    def body(x_ref, g_ref, sc_ref, sh_ref, o_ref):
        xv = x_ref[...]
        r = lax.rsqrt(jnp.mean(xv * xv, axis=-1, keepdims=True) + EPS)
        o_ref[...] = ((xv * r * g_ref[...]) * (1.0 + sc_ref[...]) + sh_ref[...]).astype(BF16)

    return pl.pallas_call(
        body, name=name, grid=(S // tm,),
        in_specs=[_rows(tm, D), _vec(D), _vec(D), _vec(D)], out_specs=_rows(tm, D),
        out_shape=jax.ShapeDtypeStruct((S, D), BF16), compiler_params=_par(1),
    )(x, g, sc, sh)


def _lnmod_bwd(dh, x, g, sc, dres, name):
    S, D = x.shape
    tm = _pick(S, 512, 8)

    def body(dh_ref, x_ref, g_ref, sc_ref, dres_ref, dx_ref, dsh_ref, dsc_ref, dg_ref):
        i = pl.program_id(0)
        xv = x_ref[...]
        dh_ = dh_ref[...]
        r = lax.rsqrt(jnp.mean(xv * xv, axis=-1, keepdims=True) + EPS)
        xhat = xv * r
        gv = g_ref[...]
        dn = dh_ * (1.0 + sc_ref[...])
        dxhat = dn * gv
        dx_ref[...] = dres_ref[...] + r * (dxhat - xhat * jnp.mean(dxhat * xhat, axis=-1, keepdims=True))
        _acc(dsh_ref, jnp.sum(dh_, axis=0, keepdims=True), i)
        _acc(dsc_ref, jnp.sum(dh_ * (xhat * gv), axis=0, keepdims=True), i)
        _acc(dg_ref, jnp.sum(dn * xhat, axis=0, keepdims=True), i)

    return pl.pallas_call(
        body, name=name, grid=(S // tm,),
        in_specs=[_rows(tm, D), _rows(tm, D), _vec(D), _vec(D), _rows(tm, D)],
        out_specs=[_rows(tm, D), _vec(D), _vec(D), _vec(D)],
        out_shape=[jax.ShapeDtypeStruct((S, D), F32)] + [jax.ShapeDtypeStruct((1, D), F32)] * 3,
        compiler_params=_arb(1),
    )(dh, x, g, sc, dres)


def _gate_fwd(x, y, gt, name):
    S, D = x.shape
    tm = _pick(S, 512, 8)

    def body(x_ref, y_ref, gt_ref, o_ref):
        o_ref[...] = x_ref[...] + gt_ref[...] * y_ref[...]

    return pl.pallas_call(
        body, name=name, grid=(S // tm,), in_specs=[_rows(tm, D), _rows(tm, D), _vec(D)], out_specs=_rows(tm, D),
        out_shape=jax.ShapeDtypeStruct((S, D), F32), compiler_params=_par(1),
    )(x, y, gt)


def _gate_bwd(dx, y, gt, name):
    S, D = dx.shape
    tm = _pick(S, 512, 8)

    def body(dx_ref, y_ref, gt_ref, dz_ref, dgt_ref):
        i = pl.program_id(0)
        d = dx_ref[...]
        dz_ref[...] = (d * gt_ref[...]).astype(BF16)
        _acc(dgt_ref, jnp.sum(d * y_ref[...], axis=0, keepdims=True), i)

    return pl.pallas_call(
        body, name=name, grid=(S // tm,), in_specs=[_rows(tm, D), _rows(tm, D), _vec(D)],
        out_specs=[_rows(tm, D), _vec(D)],
        out_shape=[jax.ShapeDtypeStruct((S, D), BF16), jax.ShapeDtypeStruct((1, D), F32)],
        compiler_params=_arb(1),
    )(dx, y, gt)


def _ffn_act_fwd(gu, name):
    S, H2 = gu.shape
    H = H2 // 2
    tm = _pick(S, 256, 8)

    def body(g_ref, u_ref, o_ref):
        gv = g_ref[...].astype(F32)
        o_ref[...] = (gv * _sigmoid(gv) * u_ref[...].astype(F32)).astype(BF16)

    return pl.pallas_call(
        body, name=name, grid=(S // tm,), in_specs=[_rows(tm, H, 0), _rows(tm, H, 1)], out_specs=_rows(tm, H),
        out_shape=jax.ShapeDtypeStruct((S, H), BF16), compiler_params=_par(1),
    )(gu, gu)


def _ffn_act_bwd(gu, dact, name):
    S, H2 = gu.shape
    H = H2 // 2
    tm = _pick(S, 256, 8)

    def body(g_ref, u_ref, da_ref, o_ref):
        gv = g_ref[...].astype(F32)
        s = _sigmoid(gv)
        da = da_ref[...].astype(F32)
        o_ref[:, :H] = (da * u_ref[...].astype(F32) * (s * (1.0 + gv * (1.0 - s)))).astype(BF16)
        o_ref[:, H:] = (da * (gv * s)).astype(BF16)

    return pl.pallas_call(
        body, name=name, grid=(S // tm,), in_specs=[_rows(tm, H, 0), _rows(tm, H, 1), _rows(tm, H)],
        out_specs=_rows(tm, H2), out_shape=jax.ShapeDtypeStruct((S, H2), BF16), compiler_params=_par(1),
    )(gu, gu, dact)


def _merge_fwd(proj, pa, pb, name):
    S, D = pa.shape
    tm = _pick(S, 512, 8)

    def body(ga_ref, gb_ref, pa_ref, pb_ref, o_ref):
        o_ref[...] = (_sigmoid(ga_ref[...]) * pa_ref[...].astype(F32)
                      + _sigmoid(gb_ref[...]) * pb_ref[...].astype(F32)).astype(BF16)

    return pl.pallas_call(
        body, name=name, grid=(S // tm,),
        in_specs=[_rows(tm, D, P_GA // D), _rows(tm, D, P_GB // D), _rows(tm, D), _rows(tm, D)],
        out_specs=_rows(tm, D), out_shape=jax.ShapeDtypeStruct((S, D), BF16), compiler_params=_par(1),
    )(proj, proj, pa, pb)


def _merge_bwd(proj, pa, pb, dm, name):
    S, D = pa.shape
    tm = _pick(S, 512, 16)
    rows_j = pl.BlockSpec((tm, D), lambda i, j: (i, 0))

    def body(g_ref, pa_ref, pb_ref, dm_ref, dg_ref, dpa_ref, dpb_ref):
        d = dm_ref[...].astype(F32)
        s = _sigmoid(g_ref[...])
        for branch, p_ref, dp_ref in ((0, pa_ref, dpa_ref), (1, pb_ref, dpb_ref)):
            @pl.when(pl.program_id(1) == branch)
            def _():
                dg_ref[...] = (d * p_ref[...].astype(F32) * s * (1.0 - s)).astype(BF16)
                dp_ref[...] = (d * s).astype(BF16)

    return pl.pallas_call(
        body, name=name, grid=(S // tm, 2),
        in_specs=[pl.BlockSpec((tm, D), lambda i, j: (i, P_GA // D + j)), rows_j, rows_j, rows_j],
        out_specs=[pl.BlockSpec((tm, D), lambda i, j: (i, P_GA // D + j)), rows_j, rows_j],
        out_shape=[jax.ShapeDtypeStruct((S, P_END), BF16), jax.ShapeDtypeStruct((S, D), BF16),
                   jax.ShapeDtypeStruct((S, D), BF16)],
        compiler_params=_arb(2),
    )(proj, pa, pb, dm)


def _write_columns(buf, parts, width, colblk, name):
    S = buf.shape[0]
    tm = _pick(S, 512, 16)
    n = len(parts)

    def body(*refs):
        o_ref = refs[n + 1]
        off = 0
        for p_ref in refs[:n]:
            w = p_ref.shape[1]
            o_ref[:, off:off + w] = p_ref[...].astype(BF16)
            off += w

    return pl.pallas_call(
        body, name=name, grid=(S // tm,), in_specs=[_rows(tm, p.shape[1]) for p in parts] + [HBM],
        out_specs=_rows(tm, width, colblk), out_shape=jax.ShapeDtypeStruct(buf.shape, buf.dtype),
        input_output_aliases={n: 0}, compiler_params=_par(1),
    )(*parts, buf)


def _loss_head(x, g, target, name):
    S, D = x.shape
    tm = _pick(S, 512, 8)

    def body(x_ref, g_ref, t_ref, dx_ref, loss_ref, dg_ref):
        i = pl.program_id(0)
        xv = x_ref[...]
        gv = g_ref[...]
        r = lax.rsqrt(jnp.mean(xv * xv, axis=-1, keepdims=True) + EPS)
        xhat = xv * r
        err = xhat * gv - t_ref[...]
        part = 0.5 * jnp.sum(jnp.mean(err * err, axis=-1, keepdims=True), axis=0, keepdims=True)
        _acc(loss_ref, jnp.broadcast_to(part, (1, LANE)), i)
        dy = err * (1.0 / D)
        _acc(dg_ref, jnp.sum(dy * xhat, axis=0, keepdims=True), i)
        dxhat = dy * gv
        dx_ref[...] = r * (dxhat - xhat * jnp.mean(dxhat * xhat, axis=-1, keepdims=True))

    return pl.pallas_call(
        body, name=name, grid=(S // tm,), in_specs=[_rows(tm, D), _vec(D), _rows(tm, D)],
        out_specs=[_rows(tm, D), _vec(LANE), _vec(D)],
        out_shape=[jax.ShapeDtypeStruct((S, D), F32), jax.ShapeDtypeStruct((1, LANE), F32),
                   jax.ShapeDtypeStruct((1, D), F32)],
        compiler_params=_arb(1),
    )(x, g, target)


HEADS_PER_SLAB = LANE // A_DH
N_SLABS = A_HEADS // HEADS_PER_SLAB
SPAN_BLOCKS = KSPAN // QBLK


def _attn_specs(seg):
    q_spec = pl.BlockSpec((QBLK, LANE), lambda p, m: (m, seg[0] * N_SLABS + p))
    k_specs = [pl.BlockSpec((QBLK, LANE), functools.partial(
        lambda j, p, m: (jnp.maximum(m - (SPAN_BLOCKS - 1) + j, 0), seg[1] * N_SLABS + p), j)) for j in range(SPAN_BLOCKS)]
    v_specs = [pl.BlockSpec((QBLK, LANE), functools.partial(
        lambda j, p, m: (jnp.maximum(m - (SPAN_BLOCKS - 1) + j, 0), seg[2] * N_SLABS + p), j)) for j in range(SPAN_BLOCKS)]
    b_spec = pl.BlockSpec((HEADS_PER_SLAB, QBLK, KSPAN), lambda p, m: (p, 0, 0))
    return q_spec, k_specs, v_specs, b_spec


def _head_lanes(t, hh):
    lane = lax.broadcasted_iota(jnp.int32, t.shape, 1)
    return jnp.where((lane // A_DH) == hh, t, jnp.zeros_like(t))


def _front_mask(m):
    col = lax.broadcasted_iota(jnp.int32, (QBLK, KSPAN), 1)
    return jnp.where(col < (SPAN_BLOCKS - 1 - m) * QBLK, NEG, 0.0)


def _attn_probs(qk, bias, front):
    s = qk * (A_DH ** -0.5) + (bias + front)
    p = jnp.exp(s - jnp.max(s, axis=-1, keepdims=True))
    return p * (1.0 / jnp.sum(p, axis=-1, keepdims=True))


def _grid_ends(nq):
    p, m = pl.program_id(0), pl.program_id(1)
    return (p == 0) & (m == 0), (p == N_SLABS - 1) & (m == nq // 2), (p == N_SLABS - 1) & (m == nq - 1)


def _attn_fwd(proj, big, name, gather=None):
    S = proj.shape[0]
    q_spec, k_specs, v_specs, b_spec = _attn_specs((0, 1, 2))
    shards, layer = gather if gather is not None else ((), None)
    ng = len(shards)

    def body(q_ref, k0, k1, k2, v0, v1, v2, b_ref, *rest):
        srcs, o_ref, gouts, sems = rest[:ng], rest[ng], rest[ng + 1:2 * ng + 1], rest[2 * ng + 1:]
        done = _carry(_gather_phases(layer, srcs, gouts, *sems), *_grid_ends(S // QBLK)) if ng else None
        m = pl.program_id(1)
        q = q_ref[...].astype(BF16)
        k = jnp.concatenate([k0[...], k1[...], k2[...]], axis=0).astype(BF16)
        v = jnp.concatenate([v0[...], v1[...], v2[...]], axis=0).astype(BF16)
        front = _front_mask(m)
        heads = range(HEADS_PER_SLAB)
        scores = [_nt(_head_lanes(q, hh), k) for hh in heads]
        probs = [_attn_probs(scores[hh], b_ref[hh], front).astype(BF16) for hh in heads]
        outs = [_nn(probs[hh], v) for hh in heads]
        lane = lax.broadcasted_iota(jnp.int32, (QBLK, LANE), 1)
        o_ref[...] = jnp.where(lane < A_DH, outs[0], outs[1]).astype(BF16)
        if ng:
            done()

    res = pl.pallas_call(
        body, name=name, grid=(N_SLABS, S // QBLK), in_specs=[q_spec] + k_specs + v_specs + [b_spec] + [HBM] * ng,
        out_specs=[pl.BlockSpec((QBLK, LANE), lambda p, m: (m, p))] + [HBM] * ng,
        out_shape=[jax.ShapeDtypeStruct((S, A_HEADS * A_DH), BF16)] + _gather_out_shapes(shards),
        scratch_shapes=_gather_scratch(ng) if ng else [], compiler_params=_arb(2),
    )(proj, proj, proj, proj, proj, proj, proj, big, *shards)
    return res[0], list(res[1:])


def _attn_bwd(proj, big, dya, name, exchange=()):
    S = proj.shape[0]
    W = A_HEADS * A_DH
    q_spec, k_specs, v_specs, b_spec = _attn_specs((0, 1, 2))
    out_q = pl.BlockSpec((QBLK, LANE), lambda p, m: (m, p))
    out_kv = pl.BlockSpec((S, LANE), lambda p, m: (0, p))
    ne = len(exchange)

    def body(q_ref, k0, k1, k2, v0, v1, v2, b_ref, do_ref, *rest):
        srcs, (dq_ref, dk_ref, dv_ref, db_ref), eouts, sems = rest[:ne], rest[ne:ne + 4], rest[ne + 4:2 * ne + 4], rest[2 * ne + 4:]
        done = _carry(_chips_phases(srcs, eouts, *sems), *_grid_ends(S // QBLK)) if ne else None
        m = pl.program_id(1)

        @pl.when(m == 0)
        def _():
            dk_ref[...] = jnp.zeros_like(dk_ref)
            dv_ref[...] = jnp.zeros_like(dv_ref)
            db_ref[...] = jnp.zeros_like(db_ref)

        q = q_ref[...].astype(BF16)
        k = jnp.concatenate([k0[...], k1[...], k2[...]], axis=0).astype(BF16)
        v = jnp.concatenate([v0[...], v1[...], v2[...]], axis=0).astype(BF16)
        do = do_ref[...]
        front = _front_mask(m)
        heads = range(HEADS_PER_SLAB)
        qh = [_head_lanes(q, hh) for hh in heads]
        doh = [_head_lanes(do, hh) for hh in heads]
        scores = [_nt(qh[hh], k) for hh in heads]
        dps = [_nt(doh[hh], v) for hh in heads]
        ps = [_attn_probs(scores[hh], b_ref[hh], front) for hh in heads]
        dss = [ps[hh] * (dps[hh] - jnp.sum(ps[hh] * dps[hh], axis=-1, keepdims=True)) for hh in heads]
        for hh in heads:
            db_ref[hh] += dss[hh]
        dsb = [(dss[hh] * (A_DH ** -0.5)).astype(BF16) for hh in heads]
        dqs = [_nn(dsb[hh], k) for hh in heads]
        dk = sum(_tn(dsb[hh], qh[hh]) for hh in heads)
        dv = sum(_tn(ps[hh].astype(BF16), doh[hh]) for hh in heads)
        lane = lax.broadcasted_iota(jnp.int32, (QBLK, LANE), 1)
        dq_ref[...] = jnp.where(lane < A_DH, dqs[0], dqs[1])
        for j in range(SPAN_BLOCKS):
            blk = m - (SPAN_BLOCKS - 1) + j

            @pl.when(blk >= 0)
            def _():
                off = pl.multiple_of(blk * QBLK, QBLK)
                dk_ref[pl.ds(off, QBLK), :] += dk[j * QBLK:(j + 1) * QBLK]
                dv_ref[pl.ds(off, QBLK), :] += dv[j * QBLK:(j + 1) * QBLK]
        if ne:
            done()

    res = pl.pallas_call(
        body, name=name, grid=(N_SLABS, S // QBLK),
        in_specs=[q_spec] + k_specs + v_specs + [b_spec, pl.BlockSpec((QBLK, LANE), lambda p, m: (m, p))] + [HBM] * ne,
        out_specs=[out_q, out_kv, out_kv, b_spec] + [HBM] * ne,
        out_shape=[jax.ShapeDtypeStruct((S, W), F32)] * 3 + [jax.ShapeDtypeStruct((A_HEADS, QBLK, KSPAN), F32)]
        + _chips_out_shapes(exchange),
        scratch_shapes=_chips_scratch(ne) if ne else [], compiler_params=_arb(2),
    )(proj, proj, proj, proj, proj, proj, proj, big, dya, *exchange)
    return tuple(res[:4]) + (list(res[4:]),)


NREL_PAD = 3 * LANE
SKEW_W = 1024


def _rel_table_grad(dbig, name):
    H, R, C = dbig.shape

    def body(d_ref, o_ref):
        x = jnp.concatenate([d_ref[...], jnp.zeros((R, SKEW_W - C), F32)], axis=1)
        row = lax.broadcasted_iota(jnp.int32, (R, SKEW_W), 0)
        for b in range(R.bit_length() - 1):
            x = jnp.where(((row >> b) & 1) == 1, pltpu.roll(x, SKEW_W - (1 << b), 1), x)
        e = jnp.sum(x, axis=0, keepdims=True)
        xi = lax.broadcasted_iota(jnp.int32, (SKEW_W, NREL_PAD), 0)
        r = lax.broadcasted_iota(jnp.int32, (SKEW_W, NREL_PAD), 1)
        diag = jnp.where(xi < C, xi, xi - SKEW_W)
        rel = jnp.clip(A_PAST * CHUNK - diag, -A_MAX_REL, A_MAX_REL) + A_MAX_REL
        o_ref[...] = _nn(e, jnp.where(rel == r, 1.0, 0.0).astype(F32), HI)

    return pl.pallas_call(
        body, name=name, grid=(H,), in_specs=[pl.BlockSpec((None, R, C), lambda h: (h, 0, 0))],
        out_specs=pl.BlockSpec((None, 1, NREL_PAD), lambda h: (h, 0, 0)),
        out_shape=jax.ShapeDtypeStruct((H, 1, NREL_PAD), F32), compiler_params=_par(1),
    )(dbig)


def _chunk_cumsum_matrix(n, reverse):
    j = lax.broadcasted_iota(jnp.int32, (n, n), 0)
    i = lax.broadcasted_iota(jnp.int32, (n, n), 1)
    same = (j // CHUNK) == (i // CHUNK)
    return jnp.where(same & ((j >= i) if reverse else (j <= i)), 1.0, 0.0).astype(F32)


def _gdn_gates_fwd(b_t, a_t, alog, dtb, name):
    Hh, S = b_t.shape
    tl = _pick(S, 512)
    row = pl.BlockSpec((Hh, tl), lambda i: (0, i))
    col = pl.BlockSpec((Hh, 1), lambda i: (0, 0))

    def body(b_ref, a_ref, al_ref, dt_ref, beta_ref, gam_ref):
        z = a_ref[...] + dt_ref[...]
        sp = jnp.maximum(z, 0.0) + jnp.log(1.0 + jnp.exp(-jnp.abs(z)))
        g = -jnp.exp(al_ref[...]) * sp
        beta_ref[...] = _sigmoid(b_ref[...])
        gam_ref[...] = _nn(g, _chunk_cumsum_matrix(tl, False), HI)

    return pl.pallas_call(
        body, name=name, grid=(S // tl,), in_specs=[row, row, col, col], out_specs=[row, row],
        out_shape=[jax.ShapeDtypeStruct((Hh, S), F32)] * 2, compiler_params=_par(1),
    )(b_t, a_t, alog, dtb)


def _gdn_gates_bwd(dbeta, dgam_a, dgam_b, b_t, a_t, alog, dtb, name):
    Hh, S = b_t.shape
    tl = _pick(S, 512)
    row = pl.BlockSpec((Hh, tl), lambda i: (0, i))
    col = pl.BlockSpec((Hh, 1), lambda i: (0, 0))
    accs = pl.BlockSpec((Hh, LANE), lambda i: (0, 0))

    def body(dbeta_ref, dga_ref, dgb_ref, b_ref, a_ref, al_ref, dt_ref, db_ref, da_ref, dal_ref, ddt_ref):
        i = pl.program_id(0)
        z = a_ref[...] + dt_ref[...]
        sp = jnp.maximum(z, 0.0) + jnp.log(1.0 + jnp.exp(-jnp.abs(z)))
        ea = jnp.exp(al_ref[...])
        dg = _nn(dga_ref[...] + dgb_ref[...], _chunk_cumsum_matrix(tl, True), HI)
        da = dg * (-ea) * _sigmoid(z)
        beta = _sigmoid(b_ref[...])
        db_ref[...] = dbeta_ref[...] * beta * (1.0 - beta)
        da_ref[...] = da
        _acc(dal_ref, jnp.broadcast_to(jnp.sum(dg * (-ea * sp), axis=1, keepdims=True), (Hh, LANE)), i)
        _acc(ddt_ref, jnp.broadcast_to(jnp.sum(da, axis=1, keepdims=True), (Hh, LANE)), i)

    return pl.pallas_call(
        body, name=name, grid=(S // tl,), in_specs=[row] * 5 + [col, col], out_specs=[row, row, accs, accs],
        out_shape=[jax.ShapeDtypeStruct((Hh, S), F32)] * 2 + [jax.ShapeDtypeStruct((Hh, LANE), F32)] * 2,
        compiler_params=_arb(1),
    )(dbeta, dgam_a, dgam_b, b_t, a_t, alog, dtb)


HALO = 8


def _conv_silu(xx_ref, w_ref, tm):
    y = w_ref[0:1, :] * xx_ref[pl.ds(HALO - CONV_K + 1, tm), :]
    for j in range(1, CONV_K):
        y = y + w_ref[j:j + 1, :] * xx_ref[pl.ds(HALO - CONV_K + 1 + j, tm), :]
    return y, y * _sigmoid(y)


def _fill_prev_halo(xx_ref, x_ref, prev_ref, i, tm):
    xx_ref[pl.ds(HALO, tm), :] = x_ref[...]

    @pl.when(i == 0)
    def _():
        xx_ref[pl.ds(0, HALO), :] = jnp.zeros((HALO, xx_ref.shape[1]), F32)

    @pl.when(i != 0)
    def _():
        xx_ref[pl.ds(0, HALO), :] = prev_ref[...]


def _gdn_pre_specs(tm, C, colblk):
    cur = pl.BlockSpec((tm, C), lambda i: (i, colblk))
    prev = pl.BlockSpec((HALO, C), lambda i: (jnp.maximum(i * (tm // HALO) - 1, 0), colblk))
    return cur, prev


def _gdn_pre_fwd(proj, wconv, name):
    S = proj.shape[0]
    C = 3 * B_HEADS * B_DH
    W = B_HEADS * B_DH
    tm = _pick(S, 256, 8)
    cur, prev = _gdn_pre_specs(tm, C, P_QKVB // C)

    def body(x_ref, prev_ref, w_ref, q_ref, k_ref, v_ref, xx_ref):
        i = pl.program_id(0)
        _fill_prev_halo(xx_ref, x_ref, prev_ref, i, tm)
        _, sl = _conv_silu(xx_ref, w_ref, tm)
        for h in range(B_HEADS):
            hs = slice(h * B_DH, (h + 1) * B_DH)
            q = sl[:, h * B_DH:(h + 1) * B_DH]
            k = sl[:, W + h * B_DH:W + (h + 1) * B_DH]
            q_ref[:, hs] = q * (lax.rsqrt(jnp.sum(q * q, axis=-1, keepdims=True) + EPS) * (B_DH ** -0.5))
            k_ref[:, hs] = k * lax.rsqrt(jnp.sum(k * k, axis=-1, keepdims=True) + EPS)
        v_ref[...] = sl[:, 2 * W:]

    return pl.pallas_call(
        body, name=name, grid=(S // tm,), in_specs=[cur, prev, pl.BlockSpec((CONV_K, C), lambda i: (0, 0))],
        out_specs=[_rows(tm, W)] * 3, out_shape=[jax.ShapeDtypeStruct((S, W), F32)] * 3,
        scratch_shapes=[pltpu.VMEM((HALO + tm, C), F32)], compiler_params=_par(1),
    )(proj, proj, wconv)


def _gdn_pre_bwd_a(proj, wconv, dqn, dkn, dv, name):
    S = proj.shape[0]
    C = 3 * B_HEADS * B_DH
    W = B_HEADS * B_DH
    tm = _pick(S, 256, 8)
    cur, prev = _gdn_pre_specs(tm, C, P_QKVB // C)

    def body(x_ref, prev_ref, w_ref, dq_ref, dk_ref, dv_ref, dy_ref, xx_ref):
        i = pl.program_id(0)
        _fill_prev_halo(xx_ref, x_ref, prev_ref, i, tm)
        y, sl = _conv_silu(xx_ref, w_ref, tm)
        sg = _sigmoid(y)
        dsilu = sg * (1.0 + y * (1.0 - sg))
        for h in range(B_HEADS):
            for base, d_ref, c in ((0, dq_ref, B_DH ** -0.5), (W, dk_ref, 1.0)):
                lo = base + h * B_DH
                t = sl[:, lo:lo + B_DH]
                d = d_ref[:, h * B_DH:(h + 1) * B_DH]
                r = lax.rsqrt(jnp.sum(t * t, axis=-1, keepdims=True) + EPS)
                dt = (c * r) * (d - t * (r * r) * jnp.sum(d * t, axis=-1, keepdims=True))
                dy_ref[:, lo:lo + B_DH] = dt * dsilu[:, lo:lo + B_DH]
        dy_ref[:, 2 * W:] = dv_ref[...] * dsilu[:, 2 * W:]

    return pl.pallas_call(
        body, name=name, grid=(S // tm,),
        in_specs=[cur, prev, pl.BlockSpec((CONV_K, C), lambda i: (0, 0))] + [_rows(tm, W)] * 3,
        out_specs=_rows(tm, C), out_shape=jax.ShapeDtypeStruct((S, C), F32),
        scratch_shapes=[pltpu.VMEM((HALO + tm, C), F32)], compiler_params=_par(1),
    )(proj, proj, wconv, dqn, dkn, dv)


def _gdn_pre_bwd_b(proj, wconv, dy, dproj, name):
    S = proj.shape[0]
    C = 3 * B_HEADS * B_DH
    tm = _pick(S, 256, 16)
    nt_ = S // tm
    cur, prev = _gdn_pre_specs(tm, C, P_QKVB // C)
    nxt = pl.BlockSpec((HALO, C), lambda i: (jnp.minimum((i + 1) * (tm // HALO), S // HALO - 1), 0))

    def body(x_ref, prev_ref, w_ref, dy_ref, next_ref, _, dx_ref, dw_ref, xx_ref, dd_ref):
        i = pl.program_id(0)
        _fill_prev_halo(xx_ref, x_ref, prev_ref, i, tm)
        dyv = dy_ref[...]
        dd_ref[pl.ds(0, tm), :] = dyv

        @pl.when(i == nt_ - 1)
        def _():
            dd_ref[pl.ds(tm, HALO), :] = jnp.zeros((HALO, C), F32)

        @pl.when(i != nt_ - 1)
        def _():
            dd_ref[pl.ds(tm, HALO), :] = next_ref[...]

        dx = w_ref[0:1, :] * dd_ref[pl.ds(CONV_K - 1, tm), :]
        for j in range(1, CONV_K):
            dx = dx + w_ref[j:j + 1, :] * dd_ref[pl.ds(CONV_K - 1 - j, tm), :]
        dx_ref[...] = dx.astype(BF16)
        dw = jnp.concatenate(
            [jnp.sum(dyv * xx_ref[pl.ds(HALO - CONV_K + 1 + j, tm), :], axis=0, keepdims=True) for j in range(CONV_K)],
            axis=0)
        _acc(dw_ref, dw, i)

    return pl.pallas_call(
        body, name=name, grid=(nt_,),
        in_specs=[cur, prev, pl.BlockSpec((CONV_K, C), lambda i: (0, 0)), _rows(tm, C), nxt, HBM],
        out_specs=[_rows(tm, C, P_QKVB // C), pl.BlockSpec((CONV_K, C), lambda i: (0, 0))],
        out_shape=[jax.ShapeDtypeStruct(dproj.shape, BF16), jax.ShapeDtypeStruct((CONV_K, C), F32)],
        input_output_aliases={5: 0},
        scratch_shapes=[pltpu.VMEM((HALO + tm, C), F32), pltpu.VMEM((tm + HALO, C), F32)],
        compiler_params=_arb(1),
    )(proj, proj, wconv, dy, dy, dproj)


def _chunk_masks():
    row = lax.broadcasted_iota(jnp.int32, (CHUNK, CHUNK), 0)
    col = lax.broadcasted_iota(jnp.int32, (CHUNK, CHUNK), 1)
    return row >= col, row > col


def _chunk_local(q, k, vv, bc, gc, gr, tri):
    dm = jnp.where(tri, jnp.exp(jnp.where(tri, gc - gr, 0.0)), 0.0)
    kk = _bnt(k, k)
    glast = gr[..., CHUNK - 1:CHUNK]
    ep = jnp.exp(gc)
    em = jnp.exp(glast - gc)
    el = jnp.exp(glast)
    return dm, kk, ep, em, el, vv * bc, k * (bc * ep)


def _unit_lower_inverse(low):
    row = lax.broadcasted_iota(jnp.int32, (CHUNK, CHUNK), 0)
    col = lax.broadcasted_iota(jnp.int32, (CHUNK, CHUNK), 1)
    p = -low
    t = jnp.where(row == col, 1.0, 0.0).astype(F32) + p
    steps = CHUNK.bit_length() - 2
    for _ in range(steps):
        p = _nn(p, p, SOLVE_PREC)
        t = t + _nn(t, p, SOLVE_PREC)
    return t


GROUP = 4


LATE = 7


def _carry(phases, first, middle, last):
    if len(phases) == 3:
        pl.when(first)(phases[0])
        pl.when(middle)(phases[1])
        return lambda: pl.when(last)(phases[2])
    pl.when(first)(phases[0])
    return lambda: pl.when(last)(phases[1])


def _pairs(nchunks):
    return [(c, h) for c in range(nchunks) for h in range(B_HEADS)]


def _tok(c):
    return slice(c * CHUNK, (c + 1) * CHUNK)


def _head(h):
    return slice(h * B_DH, (h + 1) * B_DH)


def _stack_tokens(ref, nchunks):
    return jnp.stack([ref[_tok(c), _head(h)] for c, h in _pairs(nchunks)])


def _stack_cols(ref, nchunks):
    per_chunk = [ref[c] for c in range(nchunks)] if len(ref.shape) == 3 else [ref[...]]
    return jnp.stack([per_chunk[c][:, h:h + 1] for c, h in _pairs(nchunks)])


def _stack_rows(ref, nchunks):
    if len(ref.shape) == 3:
        return jnp.stack([ref[c, h:h + 1, :] for c, h in _pairs(nchunks)])
    return jnp.stack([ref[h:h + 1, :] for _, h in _pairs(1)])


def _gdn_group_specs(ng_steps, W):
    tok = pl.BlockSpec((GROUP * CHUNK, W), lambda i: (i, 0))
    colv = pl.BlockSpec((GROUP, CHUNK, B_HEADS), lambda i: (i, 0, 0))
    rowv = pl.BlockSpec((GROUP, B_HEADS, CHUNK), lambda i: (i, 0, 0))
    mat = pl.BlockSpec((GROUP, B_HEADS, CHUNK, CHUNK), lambda i: (i, 0, 0, 0))
    return tok, colv, rowv, mat


def _gdn_local_fwd(qn, kn, v, bcol, gcol, grow, name, gather=None):
    S, Wd = qn.shape
    nc = S // CHUNK
    steps = nc // GROUP
    tok, colv, rowv, mat = _gdn_group_specs(steps, Wd)
    shards, layer = gather if gather is not None else ((), None)
    ng = len(shards)

    def body(q_ref, k_ref, v_ref, bc_ref, gc_ref, gr_ref, *rest):
        srcs, (t_ref, a_ref, u_ref, w_ref), gouts, sems = rest[:ng], rest[ng:ng + 4], rest[ng + 4:2 * ng + 4], rest[2 * ng + 4:]
        i = pl.program_id(0)
        done = _carry(_gather_phases(layer, srcs, gouts, *sems), i == 0, i == LATE * steps // 8, i == steps - 1) if ng else None
        tri, strict = _chunk_masks()
        q, k, vv = (_stack_tokens(r, GROUP) for r in (q_ref, k_ref, v_ref))
        bc, gc, gr = _stack_cols(bc_ref, GROUP), _stack_cols(gc_ref, GROUP), _stack_rows(gr_ref, GROUP)
        dm, kk, ep, em, el, vb, kb = _chunk_local(q, k, vv, bc, gc, gr, tri)
        t = _unit_lower_inverse(jnp.where(strict, bc * kk * dm, 0.0))
        a = _bnt(q, k) * dm
        u = _nn(t, vb, SOLVE_PREC)
        w = _nn(t, kb, SOLVE_PREC)
        for n, (c, h) in enumerate(_pairs(GROUP)):
            t_ref[c, h] = t[n]
            a_ref[c, h] = a[n]
            u_ref[_tok(c), _head(h)] = u[n]
            w_ref[_tok(c), _head(h)] = w[n]
        if ng:
            done()

    res = pl.pallas_call(
        body, name=name, grid=(steps,), in_specs=[tok, tok, tok, colv, colv, rowv] + [HBM] * ng,
        out_specs=[mat, mat, tok, tok] + [HBM] * ng,
        out_shape=[jax.ShapeDtypeStruct((nc, B_HEADS, CHUNK, CHUNK), F32)] * 2 + [jax.ShapeDtypeStruct((S, Wd), F32)] * 2
        + _gather_out_shapes(shards),
        scratch_shapes=_gather_scratch(ng) if ng else [], compiler_params=_arb(1),
    )(qn, kn, v, bcol, gcol, grow, *shards)
    return res[0], res[1], res[2], res[3], list(res[4:])


def _scan_decays(gc, gr):
    glast = gr[..., CHUNK - 1:CHUNK]
    return jnp.exp(gc), jnp.exp(glast - gc), jnp.exp(glast)


def _gdn_scan_specs(nc, rev):
    idx = (lambda i: nc - 1 - i) if rev else (lambda i: i)
    W = B_HEADS * B_DH
    tok = pl.BlockSpec((CHUNK, W), lambda i: (idx(i), 0))
    colv = pl.BlockSpec((None, CHUNK, B_HEADS), lambda i: (idx(i), 0, 0))
    rowv = pl.BlockSpec((None, B_HEADS, CHUNK), lambda i: (idx(i), 0, 0))
    mat = pl.BlockSpec((None, B_HEADS, CHUNK, CHUNK), lambda i: (idx(i), 0, 0, 0))
    smat = pl.BlockSpec((None, B_HEADS, B_DH, B_DH), lambda i: (idx(i), 0, 0, 0))
    return tok, colv, rowv, mat, smat


def _gdn_scan_fwd(qn, kn, u, w, a, gcol, grow, name, gather=None):
    S, Wd = qn.shape
    nc = S // CHUNK
    tok, colv, rowv, mat, smat = _gdn_scan_specs(nc, False)
    shards, layer = gather if gather is not None else ((), None)
    ng = len(shards)

    def body(q_ref, k_ref, u_ref, w_ref, a_ref, gc_ref, gr_ref, *rest):
        srcs, (o_ref, sh_ref), gouts = rest[:ng], rest[ng:ng + 2], rest[ng + 2:2 * ng + 2]
        st_ref, sems = rest[2 * ng + 2], rest[2 * ng + 3:]
        i = pl.program_id(0)
        done = _carry(_gather_phases(layer, srcs, gouts, *sems), i == 0, i == LATE * nc // 8, i == nc - 1) if ng else None

        @pl.when(i == 0)
        def _():
            st_ref[...] = jnp.zeros_like(st_ref)

        ep, em, el = _scan_decays(_stack_cols(gc_ref, 1), _stack_rows(gr_ref, 1))
        q, k, u, w = (_stack_tokens(r, 1) for r in (q_ref, k_ref, u_ref, w_ref))
        s0 = st_ref[...]
        ut = u - _bnn(w, s0)
        o = _bnn(q * ep, s0) + _bnn(a_ref[...], ut)
        st_ref[...] = el * s0 + _btn(k * em, ut)
        sh_ref[...] = s0
        for h in range(B_HEADS):
            o_ref[:, _head(h)] = o[h]
        if ng:
            done()

    res = pl.pallas_call(
        body, name=name, grid=(nc,), in_specs=[tok, tok, tok, tok, mat, colv, rowv] + [HBM] * ng,
        out_specs=[tok, smat] + [HBM] * ng,
        out_shape=[jax.ShapeDtypeStruct((S, Wd), F32), jax.ShapeDtypeStruct((nc, B_HEADS, B_DH, B_DH), F32)]
        + _gather_out_shapes(shards),
        scratch_shapes=[pltpu.VMEM((B_HEADS, B_DH, B_DH), F32)] + (_gather_scratch(ng) if ng else []),
        compiler_params=_arb(1),
    )(qn, kn, u, w, a, gcol, grow, *shards)
    return res[0], res[1], list(res[2:])


def _gdn_scan_bwd(qn, kn, u, w, a, gcol, grow, ssave, do, name):
    S, Wd = qn.shape
    nc = S // CHUNK
    tok, colv, rowv, mat, smat = _gdn_scan_specs(nc, True)

    def body(q_ref, k_ref, u_ref, w_ref, a_ref, gc_ref, gr_ref, sh_ref, do_ref,
             du_ref, dw_ref, dqd_ref, dkd_ref, da_ref, dgl_ref, ds_ref):
        i = pl.program_id(0)

        @pl.when(i == 0)
        def _():
            ds_ref[...] = jnp.zeros_like(ds_ref)

        tri, _ = _chunk_masks()
        sub4 = lax.broadcasted_iota(jnp.int32, (B_HEADS, CHUNK), 0)
        lane_last = lax.broadcasted_iota(jnp.int32, (1, CHUNK), 1) == CHUNK - 1
        ep, em, el = _scan_decays(_stack_cols(gc_ref, 1), _stack_rows(gr_ref, 1))
        q, k, u, w, dout = (_stack_tokens(r, 1) for r in (q_ref, k_ref, u_ref, w_ref, do_ref))
        s0 = sh_ref[...]
        ds = ds_ref[...]
        ut = u - _bnn(w, s0)
        dut = _btn(a_ref[...], dout) + _bnn(k * em, ds)
        ds_ref[...] = el * ds + _btn(q * ep, dout) - _btn(w, dut)
        dw = -_bnt(dut, s0)
        dqd = _bnt(dout, s0)
        dkd = _bnt(ut, ds)
        da_ref[...] = jnp.where(tri, _bnt(dout, ut), 0.0)
        d_el = jnp.sum(jnp.sum(s0 * ds, axis=-1, keepdims=True), axis=-2, keepdims=True)
        last = d_el * el
        dgl_acc = jnp.zeros((B_HEADS, CHUNK), F32)
        for h in range(B_HEADS):
            du_ref[:, _head(h)] = dut[h]
            dw_ref[:, _head(h)] = dw[h]
            dqd_ref[:, _head(h)] = dqd[h]
            dkd_ref[:, _head(h)] = dkd[h]
            dgl_acc = jnp.where(sub4 == h, jnp.where(lane_last, last[h], 0.0), dgl_acc)
        dgl_ref[...] = dgl_acc

    return pl.pallas_call(
        body, name=name, grid=(nc,), in_specs=[tok, tok, tok, tok, mat, colv, rowv, smat, tok],
        out_specs=[tok, tok, tok, tok, mat, rowv],
        out_shape=[jax.ShapeDtypeStruct((S, Wd), F32)] * 4 + [jax.ShapeDtypeStruct((nc, B_HEADS, CHUNK, CHUNK), F32),
                                                             jax.ShapeDtypeStruct((nc, B_HEADS, CHUNK), F32)],
        scratch_shapes=[pltpu.VMEM((B_HEADS, B_DH, B_DH), F32)], compiler_params=_arb(1),
    )(qn, kn, u, w, a, gcol, grow, ssave, do)


def _gdn_local_bwd(qn, kn, v, bcol, gcol, grow, tsave, du, dw, dqd, dkd, da, dgl, name, exchange=()):
    S, Wd = qn.shape
    nc = S // CHUNK
    steps = nc // GROUP
    tok, colv, rowv, mat = _gdn_group_specs(steps, Wd)
    ne = len(exchange)

    def body(q_ref, k_ref, v_ref, bc_ref, gc_ref, gr_ref, t_ref, du_ref, dw_ref, dqd_ref, dkd_ref, da_ref, dgl_ref, *rest):
        srcs, (dq_ref, dk_ref, dv_ref, dbc_ref, dgc_ref, dgr_ref) = rest[:ne], rest[ne:ne + 6]
        eouts, sems = rest[ne + 6:2 * ne + 6], rest[2 * ne + 6:]
        i = pl.program_id(0)
        done = _carry(_chips_phases(srcs, eouts, *sems), i == 0, None, i == steps - 1) if ne else None
        tri, strict = _chunk_masks()
        lane4 = lax.broadcasted_iota(jnp.int32, (CHUNK, B_HEADS), 1)
        sub4 = lax.broadcasted_iota(jnp.int32, (B_HEADS, CHUNK), 0)
        lane_last = lax.broadcasted_iota(jnp.int32, (1, CHUNK), 1) == CHUNK - 1
        q, k, vv, dut, dwv, dqd, dkd = (_stack_tokens(r, GROUP)
                                        for r in (q_ref, k_ref, v_ref, du_ref, dw_ref, dqd_ref, dkd_ref))
        bc, gc, gr = _stack_cols(bc_ref, GROUP), _stack_cols(gc_ref, GROUP), _stack_rows(gr_ref, GROUP)
        dm, kk, ep, em, el, vb, kb = _chunk_local(q, k, vv, bc, gc, gr, tri)
        t = jnp.stack([t_ref[c, h] for c, h in _pairs(GROUP)])
        dav = jnp.stack([da_ref[c, h] for c, h in _pairs(GROUP)])
        qk = _bnt(q, k)
        dt = _nt(dut, vb, SOLVE_PREC) + _nt(dwv, kb, SOLVE_PREC)
        dvb = _tn(t, dut, SOLVE_PREC)
        dkb = _tn(t, dwv, SOLVE_PREC)
        dl = jnp.where(strict, -_tn(t, _nt(dt, t, SOLVE_PREC), SOLVE_PREC), 0.0)
        g1 = dl * dm
        dkb_k = jnp.sum(dkb * k, axis=-1, keepdims=True)
        dbeta = jnp.sum(g1 * kk, axis=-1, keepdims=True) + jnp.sum(dvb * vv, axis=-1, keepdims=True) + dkb_k * ep
        dkk = g1 * bc
        ddm = dl * (bc * kk) + dav * qk
        dqk = dav * dm
        dq = _bnn(dqk, k) + dqd * ep
        dk = _btn(dqk, q) + _bnn(dkk, k) + _btn(dkk, k) + dkb * (bc * ep) + dkd * em
        dv = dvb * bc
        dep = dkb_k * bc + jnp.sum(dqd * q, axis=-1, keepdims=True)
        dem = jnp.sum(dkd * k, axis=-1, keepdims=True)
        mm = ddm * dm
        dgam_c = jnp.sum(mm, axis=-1, keepdims=True) + dep * ep - dem * em
        dglast = jnp.sum(dem * em, axis=-2, keepdims=True)
        dgam_r = -jnp.sum(mm, axis=-2, keepdims=True) + jnp.where(lane_last, dglast, 0.0)
        for c in range(GROUP):
            dbc_acc = jnp.zeros((CHUNK, B_HEADS), F32)
            dgc_acc = jnp.zeros((CHUNK, B_HEADS), F32)
            dgr_acc = jnp.zeros((B_HEADS, CHUNK), F32)
            for h in range(B_HEADS):
                n = c * B_HEADS + h
                dq_ref[_tok(c), _head(h)] = dq[n]
                dk_ref[_tok(c), _head(h)] = dk[n]
                dv_ref[_tok(c), _head(h)] = dv[n]
                dbc_acc = jnp.where(lane4 == h, dbeta[n], dbc_acc)
                dgc_acc = jnp.where(lane4 == h, dgam_c[n], dgc_acc)
                dgr_acc = jnp.where(sub4 == h, dgam_r[n], dgr_acc)
            dbc_ref[c] = dbc_acc
            dgc_ref[c] = dgc_acc
            dgr_ref[c] = dgr_acc + dgl_ref[c]
        if ne:
            done()

    res = pl.pallas_call(
        body, name=name, grid=(steps,),
        in_specs=[tok, tok, tok, colv, colv, rowv, mat, tok, tok, tok, tok, mat, rowv] + [HBM] * ne,
        out_specs=[tok, tok, tok, colv, colv, rowv] + [HBM] * ne,
        out_shape=[jax.ShapeDtypeStruct((S, Wd), F32)] * 3
        + [jax.ShapeDtypeStruct((nc, CHUNK, B_HEADS), F32)] * 2 + [jax.ShapeDtypeStruct((nc, B_HEADS, CHUNK), F32)]
        + _chips_out_shapes(exchange),
        scratch_shapes=_chips_scratch(ne) if ne else [], compiler_params=_arb(1),
    )(qn, kn, v, bcol, gcol, grow, tsave, du, dw, dqd, dkd, da, dgl, *exchange)
    return tuple(res[:6]) + (list(res[6:]),)


def _gdn_post_fwd(o, proj, ng, name):
    S, W = o.shape
    tm = _pick(S, 512, 8)

    def body(o_ref, z_ref, g_ref, y_ref):
        gv = g_ref[...]
        for h in range(B_HEADS):
            hs = slice(h * B_DH, (h + 1) * B_DH)
            oh = o_ref[:, hs]
            z = z_ref[:, hs]
            r = lax.rsqrt(jnp.mean(oh * oh, axis=-1, keepdims=True) + EPS)
            y_ref[:, hs] = (oh * r * gv * (z * _sigmoid(z))).astype(BF16)

    return pl.pallas_call(
        body, name=name, grid=(S // tm,), in_specs=[_rows(tm, W), _rows(tm, W, P_Z // W), _vec(B_DH)],
        out_specs=_rows(tm, W), out_shape=jax.ShapeDtypeStruct((S, W), BF16), compiler_params=_par(1),
    )(o, proj, ng)


def _gdn_post_bwd(dy, o, proj, ng, dproj, name):
    S, W = o.shape
    tm = _pick(S, 512, 16)

    def body(dy_ref, o_ref, z_ref, g_ref, _, do_ref, dz_ref, dg_ref):
        i = pl.program_id(0)
        gv = g_ref[...]
        dg = jnp.zeros((1, B_DH), F32)
        for h in range(B_HEADS):
            hs = slice(h * B_DH, (h + 1) * B_DH)
            oh = o_ref[:, hs]
            z = z_ref[:, hs]
            d = dy_ref[:, hs]
            r = lax.rsqrt(jnp.mean(oh * oh, axis=-1, keepdims=True) + EPS)
            n = oh * r
            sg = _sigmoid(z)
            sz = z * sg
            dn = d * gv * sz
            dg = dg + jnp.sum(d * n * sz, axis=0, keepdims=True)
            dz_ref[:, hs] = (d * n * gv * (sg * (1.0 + z * (1.0 - sg)))).astype(BF16)
            do_ref[:, hs] = r * (dn - n * jnp.mean(dn * n, axis=-1, keepdims=True))
        _acc(dg_ref, dg, i)

    return pl.pallas_call(
        body, name=name, grid=(S // tm,),
        in_specs=[_rows(tm, W), _rows(tm, W), _rows(tm, W, P_Z // W), _vec(B_DH), HBM],
        out_specs=[_rows(tm, W), _rows(tm, W, P_Z // W), _vec(B_DH)],
        out_shape=[jax.ShapeDtypeStruct((S, W), F32), jax.ShapeDtypeStruct(dproj.shape, BF16),
                   jax.ShapeDtypeStruct((1, B_DH), F32)],
        input_output_aliases={4: 1}, compiler_params=_arb(1),
    )(dy, o, proj, ng, dproj)


def _ada_mod(c_all, w_ada, b_shard, name):
    L, D, Ns = w_ada.shape
    B = c_all.shape[0]

    def body(c_ref, w_ref, b_ref, o_ref):
        cv = c_ref[...]
        cond = (cv * _sigmoid(cv)).astype(BF16)
        o_ref[...] = _nn(cond, w_ref[...].astype(BF16)) + b_ref[...]

    return pl.pallas_call(
        body, name=name, grid=(L,),
        in_specs=[pl.BlockSpec((B, D), lambda l: (0, 0)), pl.BlockSpec((None, D, Ns), lambda l: (l, 0, 0)),
                  pl.BlockSpec((None, 1, Ns), lambda l: (l, 0, 0))],
        out_specs=pl.BlockSpec((None, B, Ns), lambda l: (l, 0, 0)),
        out_shape=jax.ShapeDtypeStruct((L, B, Ns), F32), compiler_params=_par(1),
    )(c_all, w_ada, b_shard)


def _ada_wgrad(c_all, dmod, name):
    L, B, Ns = dmod.shape
    D = c_all.shape[1]

    def body(c_ref, d_ref, o_ref):
        cv = c_ref[...]
        cond = (cv * _sigmoid(cv)).astype(BF16)
        o_ref[...] = _tn(cond, d_ref[...].astype(BF16))

    return pl.pallas_call(
        body, name=name, grid=(L,),
        in_specs=[pl.BlockSpec((B, D), lambda l: (0, 0)), pl.BlockSpec((None, B, Ns), lambda l: (l, 0, 0))],
        out_specs=pl.BlockSpec((None, D, Ns), lambda l: (l, 0, 0)),
        out_shape=jax.ShapeDtypeStruct((L, D, Ns), F32), compiler_params=_par(1),
    )(c_all, dmod)


W_IN_PIECES = ((0, 0, 1410), (1, 0, 1410), (2, 0, 252), (2, 772, 638), (3, 0, 1410), (2, 252, 512), (2, 764, 8))


def _reorder_w_in(w4, name):
    L, _, D, Cs = w4.shape
    tm = _pick(D, 256, 16)
    used = sum(p[2] for p in W_IN_PIECES)

    def body(w_ref, o_ref):
        shard = [w_ref[s] for s in range(4)]
        parts = [shard[s][:, lo:lo + n] for s, lo, n in W_IN_PIECES]
        o_ref[...] = jnp.concatenate(parts + [jnp.zeros((tm, P_END - used), w4.dtype)], axis=1)

    return pl.pallas_call(
        body, name=name, grid=(L, D // tm), in_specs=[pl.BlockSpec((None, 4, tm, Cs), lambda l, i: (l, 0, i, 0))],
        out_specs=pl.BlockSpec((None, tm, P_END), lambda l, i: (l, i, 0)),
        out_shape=jax.ShapeDtypeStruct((L, D, P_END), w4.dtype), compiler_params=_par(2),
    )(w4)


def _restore_w_in(g, name):
    D = g.shape[0]
    tm = _pick(D, 256, 16)

    def body(g_ref, o_ref, ob_ref):
        gv = g_ref[...]
        off = 0
        pieces = {}
        for s, lo, n in W_IN_PIECES:
            pieces.setdefault(s, []).append((lo, gv[:, off:off + n]))
            off += n
        for s, lst in pieces.items():
            lst.sort(key=lambda t: t[0])
            shard = lst[0][1] if len(lst) == 1 else jnp.concatenate([t[1] for t in lst], axis=1)
            o_ref[s] = shard
            ob_ref[s] = shard.astype(BF16)

    spec = pl.BlockSpec((4, tm, W_IN_SHARD), lambda i: (0, i, 0))
    return pl.pallas_call(
        body, name=name, grid=(D // tm,), in_specs=[pl.BlockSpec((tm, P_END), lambda i: (i, 0))],
        out_specs=[spec, spec],
        out_shape=[jax.ShapeDtypeStruct((4, D, W_IN_SHARD), g.dtype), jax.ShapeDtypeStruct((4, D, W_IN_SHARD), BF16)],
        compiler_params=_par(1),
    )(g)


def _adam_update(w, g, m, v):
    mn = ADAM_B1 * m + (1.0 - ADAM_B1) * g
    vn = ADAM_B2 * v + (1.0 - ADAM_B2) * (g * g)
    m_hat = mn / (1.0 - ADAM_B1 ** ADAM_STEP)
    v_hat = vn / (1.0 - ADAM_B2 ** ADAM_STEP)
    return -ADAM_LR * (m_hat / (jnp.sqrt(v_hat) + ADAM_EPS) + ADAM_WD * w), mn, vn


def _adamw(w, g, m, v, name):
    shape = w.shape
    C = shape[-1]
    R = w.size // C
    tm = _pick(R, 512, 8)
    spec = pl.BlockSpec((tm, C), lambda i: (i, 0))

    def body(w_ref, g_ref, m_ref, v_ref, d_ref, mo_ref, vo_ref):
        d_ref[...], mo_ref[...], vo_ref[...] = _adam_update(w_ref[...], g_ref[...], m_ref[...], v_ref[...])

    outs = pl.pallas_call(
        body, name=name, grid=(R // tm,), in_specs=[spec] * 4, out_specs=[spec] * 3,
        out_shape=[jax.ShapeDtypeStruct((R, C), F32)] * 3, compiler_params=_par(1),
    )(*(t.reshape(R, C) for t in (w, g, m, v)))
    return tuple(o.reshape(shape) for o in outs)


def _adamw_lead(w, g, m, v, name, tl):
    A, B, C = w.shape
    spec = pl.BlockSpec((tl, B, C), lambda i: (i, 0, 0))

    def body(w_ref, g_ref, m_ref, v_ref, d_ref, mo_ref, vo_ref):
        d_ref[...], mo_ref[...], vo_ref[...] = _adam_update(w_ref[...], g_ref[...], m_ref[...], v_ref[...])

    return pl.pallas_call(
        body, name=name, grid=(A // tl,), in_specs=[spec] * 4, out_specs=[spec] * 3,
        out_shape=[jax.ShapeDtypeStruct((A, B, C), F32)] * 3, compiler_params=_par(1),
    )(w, g, m, v)


def _adamw_layers(w, gs, m, v, name):
    L, R, C = w.shape
    tm = _pick(R, 256, 8)
    spec = pl.BlockSpec((None, tm, C), lambda l, i: (l, i, 0))
    g_specs = [pl.BlockSpec((tm, C), functools.partial(lambda ll, l, i: (jnp.where(l == ll, i, 0), 0), ll))
               for ll in range(L)]

    def body(w_ref, m_ref, v_ref, *rest):
        g_refs, (go_ref, d_ref, mo_ref, vo_ref) = rest[:L], rest[L:]
        l = pl.program_id(0)
        for ll in range(L):
            @pl.when(l == ll)
            def _():
                g = g_refs[ll][...]
                go_ref[...] = g
                d_ref[...], mo_ref[...], vo_ref[...] = _adam_update(w_ref[...], g, m_ref[...], v_ref[...])

    return pl.pallas_call(
        body, name=name, grid=(L, R // tm), in_specs=[spec] * 3 + g_specs, out_specs=[spec] * 4,
        out_shape=[jax.ShapeDtypeStruct((L, R, C), F32)] * 4, compiler_params=_arb(2),
    )(w, m, v, *gs)


def _pair_sums(a, where, b, name):
    NB, _, R, C = a.shape

    def body(where_ref, a_ref, b_ref, p_ref, own_ref):
        s = a_ref[...] + b_ref[...].astype(F32)
        p_ref[...] = s.astype(BF16)

        @pl.when(pl.program_id(0) == where_ref[1])
        def _():
            own_ref[...] = s

    return pl.pallas_call(
        body, name=name,
        grid_spec=pltpu.PrefetchScalarGridSpec(
            num_scalar_prefetch=1, grid=(NB,),
            in_specs=[pl.BlockSpec((None, None, R, C), lambda k, w: (k, w[0], 0, 0)),
                      pl.BlockSpec((None, R, C), lambda k, w: (k, 0, 0))],
            out_specs=[pl.BlockSpec((None, R, C), lambda k, w: (k, 0, 0)), pl.BlockSpec((R, C), lambda k, w: (0, 0))]),
        out_shape=[jax.ShapeDtypeStruct((NB, R, C), BF16), jax.ShapeDtypeStruct((R, C), F32)],
        compiler_params=_arb(1),
    )(where, a, b)


def _sum_own_and_received(own, recv, where, name):
    R, C = own.shape
    tm = _pick(R, 256, 16)

    def body(where_ref, p_ref, r_ref, o_ref):
        o_ref[...] = ((p_ref[...] + r_ref[0].astype(F32)) + r_ref[1].astype(F32)) + r_ref[2].astype(F32)

    return pl.pallas_call(
        body, name=name,
        grid_spec=pltpu.PrefetchScalarGridSpec(
            num_scalar_prefetch=1, grid=(R // tm,),
            in_specs=[pl.BlockSpec((tm, C), lambda i, w: (i, 0)), pl.BlockSpec((3, tm, C), lambda i, w: (0, i, 0))],
            out_specs=pl.BlockSpec((None, tm, C), lambda i, w: (w[0], i, 0))),
        out_shape=jax.ShapeDtypeStruct((2, R, C), F32), compiler_params=_par(1),
    )(where, own, recv)


def _position():
    return lax.axis_index("x"), lax.axis_index("y"), lax.axis_index("c")


def _other_chips(x, y):
    return [(1 - x, y), (x, 1 - y), (1 - x, 1 - y)]


HBM = pl.BlockSpec(memory_space=pl.ANY)


def _allgather8(blk, name, reduce_rows=None):
    M, N = blk.shape

    def body(x_ref, out_ref, *rest):
        if reduce_rows is None:
            send_sems, recv_sems, local_sem = rest
        else:
            sum_ref, send_sems, recv_sems, local_sem = rest
        x, y, c = _position()
        me, sibling = (x, y, c), (x, y, 1 - c)
        chips = _other_chips(x, y)

        def rows(px, py, pc):
            return out_ref.at[pl.ds((4 * px + 2 * py + pc) * M, M), :]

        def copy(k, block, to, src=None):
            return pltpu.make_async_remote_copy(
                src_ref=rows(*block) if src is None else src, dst_ref=rows(*block),
                send_sem=send_sems.at[k], recv_sem=recv_sems.at[k], device_id=to, device_id_type=MESH)

        mine = pltpu.make_async_copy(x_ref, rows(*me), local_sem)
        mine.start()
        first = [copy(0, me, sibling, src=x_ref)]
        first += [copy(1 + j, me, (*chip, c), src=x_ref) for j, chip in enumerate(chips)]
        for cp in first:
            cp.start()
        passed = [copy(4 + j, (*chip, c), sibling) for j, chip in enumerate(chips)]
        for j, chip in enumerate(chips):
            copy(1 + j, (*chip, c), me).wait_recv()
            passed[j].start()
        copy(0, sibling, me).wait_recv()
        for j, chip in enumerate(chips):
            copy(4 + j, (*chip, 1 - c), me).wait_recv()
        for cp in first + passed:
            cp.wait_send()
        mine.wait()
        if reduce_rows is not None:
            tot = out_ref[pl.ds(0, reduce_rows), :]
            for d in range(1, 8):
                tot = tot + out_ref[pl.ds(d * M, reduce_rows), :]
            sum_ref[...] = tot

    vmem = pl.BlockSpec(memory_space=pltpu.VMEM)
    out_shape = [jax.ShapeDtypeStruct((8 * M, N), blk.dtype)]
    if reduce_rows is not None:
        out_shape.append(jax.ShapeDtypeStruct((reduce_rows, N), blk.dtype))
    res = pl.pallas_call(
        body, name=name, out_shape=out_shape, in_specs=[vmem], out_specs=[vmem] * len(out_shape),
        scratch_shapes=[pltpu.SemaphoreType.DMA((7,)), pltpu.SemaphoreType.DMA((7,)), pltpu.SemaphoreType.DMA],
    )(blk)
    return res[0] if reduce_rows is None else (res[0], res[1])


def _gather_phases(layer, srcs, outs, send_sems, recv_sems, local_sems):
    n = len(srcs)
    x, y, c = _position()
    me, sibling = (x, y, c), (x, y, 1 - c)
    chips = _other_chips(x, y)

    def region(t, px, py, pc):
        return outs[t].at[2 * px + py, pc]

    def copy(t, k, block, to, own=False):
        return pltpu.make_async_remote_copy(
            src_ref=srcs[t].at[layer, c] if own else region(t, *block), dst_ref=region(t, *block),
            send_sem=send_sems.at[7 * t + k], recv_sem=recv_sems.at[7 * t + k], device_id=to, device_id_type=MESH)

    def local(t):
        return pltpu.make_async_copy(srcs[t].at[layer, c], region(t, *me), local_sems.at[t])

    def first(t):
        return [copy(t, 0, me, sibling, own=True)] + [copy(t, 1 + j, me, (*chip, c), own=True)
                                                       for j, chip in enumerate(chips)]

    def start():
        for t in range(n):
            local(t).start()
        for t in range(n):
            for cp in first(t):
                cp.start()

    def forward():
        for j, chip in enumerate(chips):
            for t in range(n):
                copy(t, 1 + j, (*chip, c), me).wait_recv()
                copy(t, 4 + j, (*chip, c), sibling).start()

    def finish():
        for t in range(n):
            copy(t, 0, sibling, me).wait_recv()
        for j, chip in enumerate(chips):
            for t in range(n):
                copy(t, 4 + j, (*chip, 1 - c), me).wait_recv()
        for t in range(n):
            for cp in first(t) + [copy(t, 4 + j, (*chip, c), sibling) for j, chip in enumerate(chips)]:
                cp.wait_send()
            local(t).wait()

    return start, forward, finish


def _gather_scratch(n):
    return [pltpu.SemaphoreType.DMA((7 * n,)), pltpu.SemaphoreType.DMA((7 * n,)), pltpu.SemaphoreType.DMA((n,))]


def _gather_out_shapes(shards):
    return [jax.ShapeDtypeStruct((4,) + s.shape[1:], s.dtype) for s in shards]


def _gather_weights(shards, layer, name):
    n = len(shards)

    def body(*refs):
        start, forward, finish = _gather_phases(layer, refs[:n], refs[n:2 * n], *refs[2 * n:])
        start()
        forward()
        finish()

    return pl.pallas_call(
        body, name=name, out_shape=_gather_out_shapes(shards), in_specs=[HBM] * n, out_specs=[HBM] * n,
        scratch_shapes=_gather_scratch(n),
    )(*shards)


def _sibling_phases(srcs, outs, send_sems, recv_sems):
    x, y, c = _position()
    copies = [pltpu.make_async_remote_copy(
        src_ref=srcs[t].at[k, 1 - c], dst_ref=outs[t].at[k], send_sem=send_sems.at[4 * t + k],
        recv_sem=recv_sems.at[4 * t + k], device_id=(x, y, 1 - c), device_id_type=MESH)
        for t in range(len(srcs)) for k in range(4)]

    def start():
        for cp in copies:
            cp.start()

    def finish():
        for cp in copies:
            cp.wait()

    return start, finish


def _sibling_scratch(n):
    return [pltpu.SemaphoreType.DMA((4 * n,)), pltpu.SemaphoreType.DMA((4 * n,))]


def _sibling_out_shapes(gs):
    return [jax.ShapeDtypeStruct((4,) + g.shape[2:], g.dtype) for g in gs]


def _rs_chips(ps, name):
    n = len(ps)

    def body(*refs):
        start, finish = _chips_phases(refs[:n], refs[n:2 * n], *refs[2 * n:])
        start()
        finish()

    return pl.pallas_call(
        body, name=name, out_shape=_chips_out_shapes(ps), in_specs=[HBM] * n, out_specs=[HBM] * n,
        scratch_shapes=_chips_scratch(n),
    )(*ps)


def _chips_phases(srcs, outs, send_sems, recv_sems):
    x, y, c = _position()
    copies = [pltpu.make_async_remote_copy(
        src_ref=srcs[t].at[2 * px + py], dst_ref=outs[t].at[j], send_sem=send_sems.at[3 * t + j],
        recv_sem=recv_sems.at[3 * t + j], device_id=(px, py, c), device_id_type=MESH)
        for t in range(len(srcs)) for j, (px, py) in enumerate(_other_chips(x, y))]

    def start():
        for cp in copies:
            cp.start()

    def finish():
        for cp in copies:
            cp.wait()

    return start, finish


def _chips_scratch(n):
    return [pltpu.SemaphoreType.DMA((3 * n,)), pltpu.SemaphoreType.DMA((3 * n,))]


def _chips_out_shapes(ps):
    return [jax.ShapeDtypeStruct((3,) + p.shape[1:], p.dtype) for p in ps]


def _rs_pair(hs, name):
    n = len(hs)

    def body(*refs):
        bufs = refs[n:2 * n]
        send_sems, recv_sems = refs[2 * n:]
        x, y, c = _position()

        def copy(t, half):
            return pltpu.make_async_remote_copy(
                src_ref=bufs[t].at[half], dst_ref=bufs[t].at[half], send_sem=send_sems.at[t], recv_sem=recv_sems.at[t],
                device_id=(x, y, 1 - c), device_id_type=MESH)

        for t in range(n):
            copy(t, c).start()
        for t in range(n):
            copy(t, 1 - c).wait_recv()
        for t in range(n):
            copy(t, c).wait_send()

    out_shape = [jax.ShapeDtypeStruct(h.shape, h.dtype) for h in hs]
    return pl.pallas_call(
        body, name=name, out_shape=out_shape, in_specs=[HBM] * n, out_specs=[HBM] * n,
        input_output_aliases={t: t for t in range(n)},
        scratch_shapes=[pltpu.SemaphoreType.DMA((n,)), pltpu.SemaphoreType.DMA((n,))],
    )(*hs)


BIG = ("w_in", "w_branch_a", "w_branch_b", "w_out", "w_ffn_in", "w_ffn_out")
CARRY_ATTN = ["w_in"]
CARRY_LOCAL = ["w_ffn_out"]
CARRY_SCAN = ["w_branch_a", "w_branch_b", "w_out"]
CARRY_GU = ["w_ffn_in"]
CARRY_DATTN = ["w_in", "w_ffn_in"]
CARRY_DLOCAL = ["w_branch_a", "w_branch_b", "w_out", "w_ffn_out"]


def _band_bias(rel_table, name, gather=None):
    L, H, n = rel_table.shape
    tab = jnp.pad(rel_table, ((0, 0), (0, 0), (0, NREL_PAD - n))).reshape(L * H, 1, NREL_PAD)
    band = (A_PAST + 1) * CHUNK

    shards, glayer = gather if gather is not None else ((), None)
    ng = len(shards)

    def body(t_ref, *rest):
        srcs, o_ref, gouts, sems = rest[:ng], rest[ng], rest[ng + 1:2 * ng + 1], rest[2 * ng + 1:]
        i = pl.program_id(0)
        done = (_carry(_gather_phases(glayer, srcs, gouts, *sems), i == 0, i == LATE * (L * H) // 8, i == L * H - 1)
                if ng else None)
        r = lax.broadcasted_iota(jnp.int32, (NREL_PAD, SKEW_W), 0)
        xi = lax.broadcasted_iota(jnp.int32, (NREL_PAD, SKEW_W), 1)
        diag = jnp.where(xi < KSPAN, xi, xi - SKEW_W)
        rel = jnp.clip(A_PAST * CHUNK - diag, -A_MAX_REL, A_MAX_REL) + A_MAX_REL
        e = _nn(t_ref[...], jnp.where(rel == r, 1.0, 0.0).astype(F32), HI)
        x = jnp.broadcast_to(e, (QBLK, SKEW_W))
        row = lax.broadcasted_iota(jnp.int32, (QBLK, SKEW_W), 0)
        for b in range(QBLK.bit_length() - 1):
            x = jnp.where(((row >> b) & 1) == 1, pltpu.roll(x, 1 << b, 1), x)
        x = x[:, :KSPAN]
        first = (lax.broadcasted_iota(jnp.int32, (QBLK, KSPAN), 0) // CHUNK) * CHUNK
        col = lax.broadcasted_iota(jnp.int32, (QBLK, KSPAN), 1)
        o_ref[...] = jnp.where((col >= first) & (col < first + band), x, NEG)
        if ng:
            done()

    res = pl.pallas_call(
        body, name=name, grid=(L * H,), in_specs=[pl.BlockSpec((None, 1, NREL_PAD), lambda i: (i, 0, 0))] + [HBM] * ng,
        out_specs=[pl.BlockSpec((None, QBLK, KSPAN), lambda i: (i, 0, 0))] + [HBM] * ng,
        out_shape=[jax.ShapeDtypeStruct((L * H, QBLK, KSPAN), F32)] + _gather_out_shapes(shards),
        scratch_shapes=_gather_scratch(ng) if ng else [], compiler_params=_arb(1),
    )(tab, *shards)
    return res[0].reshape(L, H, QBLK, KSPAN), list(res[1:])


def _col_row_forms(t, S):
    nc = S // CHUNK
    return t.T.reshape(nc, CHUNK, B_HEADS), t.reshape(B_HEADS, nc, CHUNK).transpose(1, 0, 2)


def _weight_view(name, gathered, tag):
    if name in ("w_out", "w_ffn_out"):
        return gathered.reshape(8 * gathered.shape[2], gathered.shape[3])
    stacked = gathered.reshape(4, 2 * gathered.shape[2], gathered.shape[3])
    return _reorder_w_in(stacked[None], f"w_in_cols_{tag}")[0] if name == "w_in" else stacked


def _layer_fwd(l, x, mod, W, P, big, gather=None, late=None):
    S, D = x.shape
    n = lambda s: f"{s}_l{l}"
    sh1, sc1, gt1, sh2, sc2, gt2 = (mod[i:i + 1] for i in range(6))
    h1 = _lnmod_fwd(x, P["norm1_g"][l:l + 1], sc1, sh1, n("ln1"))
    if late is None:
        proj = _matmul(h1, W["w_in"], "nn", F32, n("proj"), tn=1152)
    else:
        proj, got = _matmul(h1, W["w_in"], "nn", F32, n("proj"), tn=1152, gather=(late[1], l))
        W = {**W, **{k: _weight_view(k, t, f"l{l}") for k, t in zip(late[0], got)}}
    part = (lambda names: ([gather[0][BIG.index(k)] for k in names], gather[1])) if gather is not None else (lambda names: None)
    ya, got_a = _attn_fwd(proj, big, n("attn"), part(CARRY_ATTN))
    ba = proj[:, P_BA:P_BA + 2 * B_HEADS]
    b_t, a_t = ba[:, :B_HEADS].T, ba[:, B_HEADS:].T
    alog, dtb = P["a_log"][l].reshape(B_HEADS, 1), P["dt_bias"][l].reshape(B_HEADS, 1)
    beta, gam = _gdn_gates_fwd(b_t, a_t, alog, dtb, n("gates"))
    bcol, _ = _col_row_forms(beta, S)
    gcol, grow = _col_row_forms(gam, S)
    qn, kn, v = _gdn_pre_fwd(proj, P["w_conv"][l], n("gdnpre"))
    tsave, amat, u, w, got_l = _gdn_local_fwd(qn, kn, v, bcol, gcol, grow, n("gdnlocal"), part(CARRY_LOCAL))
    o, ssave, got_s = _gdn_scan_fwd(qn, kn, u, w, amat, gcol, grow, n("gdnscan"), part(CARRY_SCAN))
    yb = _gdn_post_fwd(o, proj, P["gdn_norm_g"][l:l + 1], n("gdnpost"))
    pa = _matmul(ya, W["w_branch_a"], "nn", BF16, n("pa"), tm=2048, stacked=True)
    pb = _matmul(yb, W["w_branch_b"], "nn", BF16, n("pb"), tm=2048, stacked=True)
    merged = _merge_fwd(proj, pa, pb, n("merge"))
    ao = _matmul(merged, W["w_out"], "nn", F32, n("ao"))
    x1 = _gate_fwd(x, ao, gt1, n("res1"))
    h2 = _lnmod_fwd(x1, P["norm2_g"][l:l + 1], sc2, sh2, n("ln2"))
    gu = _matmul(h2, W["w_ffn_in"], "nn", BF16, n("gu"), stacked=True, gather=part(CARRY_GU))
    gu, got_g = gu if gather is not None else (gu, [])
    got = dict(zip(CARRY_ATTN + CARRY_LOCAL + CARRY_SCAN + CARRY_GU, got_a + got_l + got_s + got_g))
    gathered = {k: got[k] for k in BIG} if gather is not None else None
    act = _ffn_act_fwd(gu, n("act"))
    fo = _matmul(act, W["w_ffn_out"], "nn", F32, n("fo"), tk=1408)
    x2 = _gate_fwd(x1, fo, gt2, n("res2"))
    saved = dict(x=x, h1=h1, proj=proj, ya=ya, b_t=b_t, a_t=a_t, bcol=bcol, gcol=gcol, grow=grow,
                 qn=qn, kn=kn, v=v, o=o, tsave=tsave, ssave=ssave, amat=amat, u=u, w=w, yb=yb, pa=pa, pb=pb,
                 merged=merged, ao=ao, x1=x1,
                 h2=h2, gu=gu, act=act, fo=fo)
    return x2, saved, gathered, W


def _layer_bwd(l, dx2, sv, mod, W, P, big, exchange=()):
    S, D = dx2.shape
    n = lambda s: f"{s}_l{l}"
    sh1, sc1, gt1, sh2, sc2, gt2 = (mod[i:i + 1] for i in range(6))
    g, pay = {}, {}
    view = lambda t: t.reshape((4, 2, t.shape[-2] // (2 if t.ndim == 3 else 8), t.shape[-1]))
    dz2, dgt2 = _gate_bwd(dx2, sv["fo"], gt2, n("dres2"))
    g["w_ffn_out"], pay["w_ffn_out"] = map(view, _matmul(sv["act"], dz2, "tn", F32, n("dwfo"), tm=1408, also_bf16=True))
    dact = _matmul(dz2, W["w_ffn_out"], "nt", BF16, n("dact"), tn=1408)
    dgu = _ffn_act_bwd(sv["gu"], dact, n("dgu"))
    g["w_ffn_in"], pay["w_ffn_in"] = map(view, _matmul(sv["h2"], dgu, "tn", F32, n("dwfi"), out_stacked=True,
                                                       also_bf16=True))
    dh2 = _matmul(dgu, W["w_ffn_in"], "nt", F32, n("dh2"), stacked=True)
    dx1, dsh2, dsc2, dn2 = _lnmod_bwd(dh2, sv["x1"], P["norm2_g"][l:l + 1], sc2, dx2, n("dln2"))
    dz1, dgt1 = _gate_bwd(dx1, sv["ao"], gt1, n("dres1"))
    g["w_out"], pay["w_out"] = map(view, _matmul(sv["merged"], dz1, "tn", F32, n("dwo"), also_bf16=True))
    dmerged = _matmul(dz1, W["w_out"], "nt", BF16, n("dmerged"))
    dproj, dpa, dpb = _merge_bwd(sv["proj"], sv["pa"], sv["pb"], dmerged, n("dmerge"))
    g["w_branch_a"], pay["w_branch_a"] = map(view, _matmul(sv["ya"], dpa, "tn", F32, n("dwa"), out_stacked=True,
                                                           also_bf16=True))
    g["w_branch_b"], pay["w_branch_b"] = map(view, _matmul(sv["yb"], dpb, "tn", F32, n("dwb"), out_stacked=True,
                                                           also_bf16=True))
    dya = _matmul(dpa, W["w_branch_a"], "nt", BF16, n("dya"), tm=2048, stacked=True)
    dyb = _matmul(dpb, W["w_branch_b"], "nt", F32, n("dyb"), tm=2048, stacked=True)
    ex = (lambda names: [exchange[BIG.index(k)] for k in names]) if len(exchange) else (lambda names: ())
    dq, dk, dv, dbig, rec_a = _attn_bwd(sv["proj"], big, dya, n("dattn"), ex(CARRY_DATTN))
    g["rel_table"] = _rel_table_grad(dbig, n("drel"))[:, 0, :2 * A_MAX_REL + 1]
    dproj = _write_columns(dproj, [dq, dk, dv], 3 * dq.shape[1], P_QKVA // (3 * dq.shape[1]), n("dqkva"))
    do, dproj, dng = _gdn_post_bwd(dyb, sv["o"], sv["proj"], P["gdn_norm_g"][l:l + 1], dproj, n("dgdnpost"))
    g["gdn_norm_g"] = dng[0]
    du, dw, dqd, dkd, da, dgl = _gdn_scan_bwd(sv["qn"], sv["kn"], sv["u"], sv["w"], sv["amat"], sv["gcol"], sv["grow"],
                                              sv["ssave"], do, n("dgdnscan"))
    dqn, dkn, dvv, dbc, dgc, dgr, rec_l = _gdn_local_bwd(
        sv["qn"], sv["kn"], sv["v"], sv["bcol"], sv["gcol"], sv["grow"], sv["tsave"], du, dw, dqd, dkd, da, dgl,
        n("dgdnlocal"), ex(CARRY_DLOCAL))
    rec = dict(zip(CARRY_DATTN + CARRY_DLOCAL, rec_a + rec_l))
    received = [rec[k] for k in BIG] if len(exchange) else None
    dbeta_t = dbc.reshape(S, B_HEADS).T
    dgam_a = dgc.reshape(S, B_HEADS).T
    dgam_b = dgr.transpose(1, 0, 2).reshape(B_HEADS, S)
    alog, dtb = P["a_log"][l].reshape(B_HEADS, 1), P["dt_bias"][l].reshape(B_HEADS, 1)
    db_t, da_t, dal, ddt = _gdn_gates_bwd(dbeta_t, dgam_a, dgam_b, sv["b_t"], sv["a_t"], alog, dtb, n("dgates"))
    g["a_log"], g["dt_bias"] = dal[:, 0], ddt[:, 0]
    dy = _gdn_pre_bwd_a(sv["proj"], P["w_conv"][l], dqn, dkn, dvv, n("dgdnpre_a"))
    dproj, g["w_conv"] = _gdn_pre_bwd_b(sv["proj"], P["w_conv"][l], dy, dproj, n("dgdnpre_b"))
    dba = jnp.concatenate([db_t.T, da_t.T, jnp.zeros((S, P_END - P_BA - 2 * B_HEADS), F32)], axis=1)
    dproj = _write_columns(dproj, [dba], dba.shape[1], P_BA // dba.shape[1], n("dba"))
    g["w_in"], pay["w_in"] = map(view, _restore_w_in(_matmul(sv["h1"], dproj, "tn", F32, n("dwin"), tn=1152),
                                                     n("dwin_cols")))
    dh1, from_sibling = _matmul(dproj, W["w_in"], "nt", F32, n("dh1"), tk=1152, sibling=[pay[k] for k in BIG])
    dx, dsh1, dsc1, dn1 = _lnmod_bwd(dh1, sv["x"], P["norm1_g"][l:l + 1], sc1, dx1, n("dln1"))
    g["norm1_g"], g["norm2_g"] = dn1[0], dn2[0]
    dmod = jnp.concatenate([dsh1, dsc1, dgt1, dsh2, dsc2, dgt2], axis=1)[0]
    return dx, g, from_sibling, dmod, received


SMALL = ("norm1_g", "norm2_g", "rel_table", "w_conv", "a_log", "dt_bias", "gdn_norm_g")
SMALL_PACK_C = 1024


def _as_rows(t):
    flat = t.reshape(-1)
    rows = -(-flat.shape[0] // SMALL_PACK_C)
    return jnp.pad(flat, (0, rows * SMALL_PACK_C - flat.shape[0])).reshape(rows, SMALL_PACK_C)


def _pack_rows(parts):
    blk = jnp.concatenate([_as_rows(p) for p in parts], axis=0)
    return jnp.pad(blk, ((0, -blk.shape[0] % 8), (0, 0)))


def _unpack_rows(blk, shapes):
    out, r = [], 0
    for shp in shapes:
        size = int(np.prod(shp))
        rows = -(-size // SMALL_PACK_C)
        out.append(blk[..., r:r + rows, :].reshape(blk.shape[:-2] + (rows * SMALL_PACK_C,))[..., :size]
                   .reshape(blk.shape[:-2] + tuple(shp)))
        r += rows
    return out


def kernel(x, c, w_ada, b_ada, norm1_g, norm2_g, w_in, rel_table, w_conv, a_log, dt_bias, gdn_norm_g, w_branch_a, w_branch_b, w_out, w_ffn_in, w_ffn_out, final_g, loss_target, m_w_ada, m_b_ada, m_norm1_g, m_norm2_g, m_w_in, m_rel_table, m_w_conv, m_a_log, m_dt_bias, m_gdn_norm_g, m_w_branch_a, m_w_branch_b, m_w_out, m_w_ffn_in, m_w_ffn_out, m_final_g, v_w_ada, v_b_ada, v_norm1_g, v_norm2_g, v_w_in, v_rel_table, v_w_conv, v_a_log, v_dt_bias, v_gdn_norm_g, v_w_branch_a, v_w_branch_b, v_w_out, v_w_ffn_in, v_w_ffn_out, v_final_g):
    weights = dict(w_ada=w_ada, b_ada=b_ada, norm1_g=norm1_g, norm2_g=norm2_g, w_in=w_in, rel_table=rel_table,
                   w_conv=w_conv, a_log=a_log, dt_bias=dt_bias, gdn_norm_g=gdn_norm_g, w_branch_a=w_branch_a,
                   w_branch_b=w_branch_b, w_out=w_out, w_ffn_in=w_ffn_in, w_ffn_out=w_ffn_out, final_g=final_g)
    mom_m = dict(w_ada=m_w_ada, b_ada=m_b_ada, norm1_g=m_norm1_g, norm2_g=m_norm2_g, w_in=m_w_in,
                 rel_table=m_rel_table, w_conv=m_w_conv, a_log=m_a_log, dt_bias=m_dt_bias, gdn_norm_g=m_gdn_norm_g,
                 w_branch_a=m_w_branch_a, w_branch_b=m_w_branch_b, w_out=m_w_out, w_ffn_in=m_w_ffn_in,
                 w_ffn_out=m_w_ffn_out, final_g=m_final_g)
    mom_v = dict(w_ada=v_w_ada, b_ada=v_b_ada, norm1_g=v_norm1_g, norm2_g=v_norm2_g, w_in=v_w_in,
                 rel_table=v_rel_table, w_conv=v_w_conv, a_log=v_a_log, dt_bias=v_dt_bias, gdn_norm_g=v_gdn_norm_g,
                 w_branch_a=v_w_branch_a, w_branch_b=v_w_branch_b, w_out=v_w_out, w_ffn_in=v_w_ffn_in,
                 w_ffn_out=v_w_ffn_out, final_g=v_final_g)
    xi, yi, ci = _position()
    chip = 2 * xi + yi
    dev = 2 * chip + ci
    L, D = norm1_g.shape
    NMOD = b_ada.shape[1] // D
    ns = w_ada.shape[2]
    cs = w_conv.shape[2]

    first_blk = _pack_rows([c, w_conv])
    first_all = _allgather8(first_blk, "gather_c").reshape(8, first_blk.shape[0], SMALL_PACK_C)
    c_all, w_conv_all = _unpack_rows(first_all, [(D,), w_conv.shape])
    w_conv_full = w_conv_all.reshape(4, 2, L, CONV_K, cs)[:, 0].transpose(1, 2, 0, 3).reshape(L, CONV_K, 4 * cs)
    b_shard = lax.dynamic_slice_in_dim(b_ada, chip * ns, ns, axis=1).reshape(L, 1, ns)
    mod_shard = _ada_mod(c_all, w_ada, b_shard, "ada_mod")
    mod_all = _allgather8(mod_shard.reshape(L * 8, ns), "gather_mod").reshape(4, 2, L, 8, ns)
    mod = lax.dynamic_index_in_dim(mod_all[:, 0], dev, axis=2, keepdims=False)
    mod = mod.transpose(1, 0, 2).reshape(L, NMOD, D)

    shards = [weights[k].astype(BF16) for k in BIG]
    shards = [s.reshape(s.shape[0], 2, s.shape[1] // 2, s.shape[2]) for s in shards]
    P = dict(norm1_g=norm1_g, norm2_g=norm2_g, w_conv=w_conv_full, a_log=a_log, dt_bias=dt_bias,
             gdn_norm_g=gdn_norm_g)
    shard_of = dict(zip(BIG, shards))

    big, got = _band_bias(rel_table, "band_bias", ([shard_of["w_in"]], 0))
    alone = ["w_branch_a", "w_branch_b", "w_out", "w_ffn_out"]
    got += _gather_weights([shard_of[k] for k in alone], 0, "gather_weights_l0")
    W = [{k: _weight_view(k, t, "l0") for k, t in zip(["w_in"] + alone, got)}]
    late = (["w_ffn_in"], [shard_of["w_ffn_in"]])
    xc = x[0]
    saved = []
    for l in range(L):
        xc, sv, gathered, W[l] = _layer_fwd(l, xc, mod[l], W[l], P, big[l], (shards, l + 1) if l + 1 < L else None,
                                           late if l == 0 else None)
        saved.append(sv)
        if l + 1 < L:
            W.append({k: _weight_view(k, gathered[k], f"l{l + 1}") for k in BIG})
    dx, loss_dev, dfinal = _loss_head(xc, final_g.reshape(1, D), loss_target[0], "loss_head")

    where = jnp.stack([ci, chip]).astype(jnp.int32)
    grads = [None] * L
    dmods = [None] * L
    shard_grads = {k: [None] * L for k in BIG}

    def finish_reduce_scatter(l, sums, from_chips):
        halves = [_sum_own_and_received(s_[1], r_, where, f"rs_sum_{k}_l{l}")
                  for k, s_, r_ in zip(BIG, sums, from_chips)]
        for k, t in zip(BIG, _rs_pair(halves, f"rs_pair_l{l}")):
            shard_grads[k][l] = t.reshape(2 * t.shape[1], t.shape[2])

    pending = None
    for l in reversed(range(L)):
        exchange = [s_[0] for s_ in pending] if pending is not None else ()
        dx, grads[l], from_sibling, dmods[l], received = _layer_bwd(l, dx, saved[l], mod[l], W[l], P, big[l], exchange)
        if pending is not None:
            finish_reduce_scatter(l + 1, pending, received)
        gs = [grads[l][k] for k in BIG]
        pending = [_pair_sums(g_, where, r_, f"rs_pair_sum_{k}_l{l}") for k, g_, r_ in zip(BIG, gs, from_sibling)]
    finish_reduce_scatter(0, pending, _rs_chips([s_[0] for s_ in pending], "rs_chips_l0"))
    dmod = jnp.stack(dmods)

    small = {k: jnp.stack([grads[l][k] for l in range(L)]) for k in SMALL}
    parts = [dmod] + [small[k] for k in SMALL] + [dfinal, loss_dev[0, :1]]
    small_blk = _pack_rows(parts)
    srows = small_blk.shape[0]
    small_all, small_sum = _allgather8(small_blk, "gather_small", reduce_rows=srows)
    shapes = [dmod.shape] + [small[k].shape for k in SMALL] + [(D,), (1,)]
    tot = _unpack_rows(small_sum, shapes)
    G = dict(zip(SMALL, tot[1:1 + len(SMALL)]))
    G["b_ada"] = tot[0].reshape(b_ada.shape)
    G["w_conv"] = lax.dynamic_slice_in_dim(G["w_conv"], chip * cs, cs, axis=2)
    G["final_g"] = tot[-2]
    loss = tot[-1][0]
    dmod_all = _unpack_rows(small_all.reshape(8, srows, SMALL_PACK_C), [dmod.shape])[0]
    dmod_cols = lax.dynamic_slice_in_dim(dmod_all, chip * ns, ns, axis=2).transpose(1, 0, 2)
    G["w_ada"] = _ada_wgrad(c_all, dmod_cols, "ada_wgrad")

    order = ["w_ada", "b_ada", "norm1_g", "norm2_g", "w_in", "rel_table", "w_conv", "a_log", "dt_bias", "gdn_norm_g",
             "w_branch_a", "w_branch_b", "w_out", "w_ffn_in", "w_ffn_out", "final_g"]
    deltas, new_m, new_v = {}, {}, {}
    for k in order:
        w = weights[k]
        if k == "w_in":
            to_cols = lambda t: jnp.transpose(t, (2, 0, 1))
            from_cols = lambda t: jnp.transpose(t, (1, 2, 0))
            gt = to_cols(jnp.stack(shard_grads[k]))
            d_, m_, v_ = _adamw_lead(to_cols(w), gt, to_cols(mom_m[k]), to_cols(mom_v[k]), f"adamw_{k}",
                                     W_IN_SHARD // 30)
            G[k], deltas[k], new_m[k], new_v[k] = from_cols(gt), from_cols(d_), from_cols(m_), from_cols(v_)
            continue
        if k in BIG:
            G[k], deltas[k], new_m[k], new_v[k] = _adamw_layers(w, shard_grads[k], mom_m[k], mom_v[k], f"adamw_{k}")
            continue
        as2d = (lambda t: t.reshape(1, -1)) if w.ndim == 1 else (lambda t: t)
        d_, m_, v_ = _adamw(as2d(w), as2d(G[k]), as2d(mom_m[k]), as2d(mom_v[k]), f"adamw_{k}")
        deltas[k], new_m[k], new_v[k] = d_.reshape(w.shape), m_.reshape(w.shape), v_.reshape(w.shape)
    return (loss, dx[None], *[G[k] for k in order], *[deltas[k] for k in order], *[new_m[k] for k in order],
            *[new_v[k] for k in order])
```

```python
import functools

import numpy as np
import jax
import jax.numpy as jnp
from jax import lax
from jax.experimental import pallas as pl
from jax.experimental.pallas import tpu as pltpu

F32 = jnp.float32
BF16 = jnp.bfloat16
HI = lax.Precision.HIGHEST
SOLVE_PREC = lax.Precision.HIGH
MESH = pl.DeviceIdType.MESH

EPS = 1e-6
CHUNK = 64
A_HEADS = 8
A_DH = 64
A_PAST = 8
A_MAX_REL = 128
B_HEADS = 4
B_DH = 128
CONV_K = 4
LANE = 128
QBLK = 4 * CHUNK
KSPAN = QBLK + A_PAST * CHUNK
NEG = -1e30

ADAM_LR = 0.001
ADAM_B1 = 0.9
ADAM_B2 = 0.999
ADAM_EPS = 1e-08
ADAM_WD = 0.01
ADAM_STEP = 10

P_QKVA, P_QKVB, P_GA, P_GB, P_Z, P_BA, P_END = 0, 1536, 3072, 4096, 5120, 5632, 5760
W_IN_SHARD = 1410


def _sigmoid(x):
    return 1.0 / (1.0 + jnp.exp(-x))


def _dot(a, b, ca, cb, prec):
    lead = a.ndim - 2
    batch = ((0,), (0,)) if lead else ((), ())
    return lax.dot_general(a, b, (((ca + lead,), (cb + lead,)), batch), precision=prec, preferred_element_type=F32)


def _nn(a, b, prec=None):
    return _dot(a, b, 1, 0, prec)


def _nt(a, b, prec=None):
    return _dot(a, b, 1, 1, prec)


def _tn(a, b, prec=None):
    return _dot(a, b, 0, 0, prec)


def _bnn(a, b):
    return _nn(a.astype(BF16), b.astype(BF16))


def _bnt(a, b):
    return _nt(a.astype(BF16), b.astype(BF16))


def _btn(a, b):
    return _tn(a.astype(BF16), b.astype(BF16))


def _pick(n, target, unit=LANE):
    best = None
    for t in range(unit, min(n, target) + 1, unit):
        if n % t == 0:
            best = t
    return best if best is not None else n


def _acc(ref, val, i):
    @pl.when(i == 0)
    def _():
        ref[...] = val

    @pl.when(i != 0)
    def _():
        ref[...] += val


def _arb(n):
    return pltpu.CompilerParams(dimension_semantics=("arbitrary",) * n)


def _par(n):
    return pltpu.CompilerParams(dimension_semantics=("parallel",) * n)


def _matmul(a, b, mode, out_dtype, name, tm=1024, tn=1024, tk=1024, layer=None, stacked=False, out_stacked=False,
            also_bf16=False, gather=None, sibling=None):
    bs = b.shape[1:] if layer is not None else b.shape
    if mode == "nn":
        M, K = a.shape
        N = 4 * bs[2] if stacked else bs[1]
        if stacked:
            tn = bs[2]
    elif mode == "nt":
        M, K = a.shape
        N = bs[1] if stacked else bs[0]
        if stacked:
            tk = bs[2]
    else:
        K, M = a.shape
        N = bs[1]
        if out_stacked:
            tn = N // 4
    tm, tn, tk = _pick(M, tm), _pick(N, tn), _pick(K, tk)
    nk = K // tk
    lead = () if layer is None else (layer,)
    lead_blk = () if layer is None else (None,)
    if mode == "nn":
        a_spec = pl.BlockSpec((tm, tk), lambda i, j, k: (i, k))
        if stacked:
            b_spec = pl.BlockSpec(lead_blk + (None, tk, tn), lambda i, j, k: lead + (j, k, 0))
        else:
            b_spec = pl.BlockSpec(lead_blk + (tk, tn), lambda i, j, k: lead + (k, j))
        dot = _nn
    elif mode == "nt":
        a_spec = pl.BlockSpec((tm, tk), lambda i, j, k: (i, k))
        if stacked:
            b_spec = pl.BlockSpec(lead_blk + (None, tn, tk), lambda i, j, k: lead + (k, j, 0))
        else:
            b_spec = pl.BlockSpec(lead_blk + (tn, tk), lambda i, j, k: lead + (j, k))
        dot = _nt
    else:
        a_spec = pl.BlockSpec((tk, tm), lambda i, j, k: (k, i))
        b_spec = pl.BlockSpec((tk, tn), lambda i, j, k: (k, j))
        dot = _tn
    if out_stacked:
        o_spec = pl.BlockSpec((None, tm, tn), lambda i, j, k: (j, i, 0))
        o_shape = jax.ShapeDtypeStruct((4, M, tn), out_dtype)
    else:
        o_spec = pl.BlockSpec((tm, tn), lambda i, j, k: (i, j))
        o_shape = jax.ShapeDtypeStruct((M, N), out_dtype)

    if gather is not None:
        shards, glayer = gather
        carried_shapes, carried_scratch = _gather_out_shapes(shards), _gather_scratch(len(shards))
    elif sibling is not None:
        shards = sibling
        carried_shapes, carried_scratch = _sibling_out_shapes(shards), _sibling_scratch(len(shards))
    else:
        shards, carried_shapes, carried_scratch = (), [], []
    ng = len(shards)
    o_shapes = [o_shape] + ([jax.ShapeDtypeStruct(o_shape.shape, BF16)] if also_bf16 else [])
    no = len(o_shapes)
    gi, gj = M // tm, N // tn

    def write(o_refs, val):
        for o_ref in o_refs:
            o_ref[...] = val.astype(o_ref.dtype)

    def body(a_ref, b_ref, *refs):
        srcs, o_refs, gouts, scratch = refs[:ng], refs[ng:ng + no], refs[ng + no:2 * ng + no], refs[2 * ng + no:]
        i, j, k = pl.program_id(0), pl.program_id(1), pl.program_id(2)
        if ng:
            start = (j == 0) & (k == 0)
            sems = scratch[len(scratch) - len(carried_scratch):]
            phases = (_gather_phases(glayer, srcs, gouts, *sems) if gather is not None
                      else _sibling_phases(srcs, gouts, *sems))
            done = _carry(phases, (i == 0) & start, (i == gi - 1) & start,
                          (i == gi - 1) & (j == gj - 1) & (k == nk - 1))
        if nk == 1:
            write(o_refs, dot(a_ref[...], b_ref[...]))
        else:
            acc_ref = scratch[0]

            @pl.when(k == 0)
            def _():
                acc_ref[...] = jnp.zeros_like(acc_ref)

            acc_ref[...] += dot(a_ref[...], b_ref[...])

            @pl.when(k == nk - 1)
            def _():
                write(o_refs, acc_ref[...])
        if ng:
            done()

    sem = ("arbitrary",) * 3 if ng else ("parallel", "parallel", "arbitrary")
    res = pl.pallas_call(
        body, name=name, grid=(gi, gj, nk), in_specs=[a_spec, b_spec] + [HBM] * ng,
        out_specs=[o_spec] * no + [HBM] * ng, out_shape=o_shapes + carried_shapes,
        scratch_shapes=([] if nk == 1 else [pltpu.VMEM((tm, tn), F32)]) + carried_scratch,
        compiler_params=pltpu.CompilerParams(dimension_semantics=sem),
    )(a, b, *shards)
    out = tuple(res[:no]) if also_bf16 else res[0]
    return (out, list(res[no:])) if ng else out


def _rows(tm, n, col=0):
    return pl.BlockSpec((tm, n), lambda i: (i, col))


def _vec(n):
    return pl.BlockSpec((1, n), lambda i: (0, 0))


def _lnmod_fwd(x, g, sc, sh, name):
    S, D = x.shape
    tm = _pick(S, 512, 8)

    def body(x_ref, g_ref, sc_ref, sh_ref, o_ref):
        xv = x_ref[...]
        r = lax.rsqrt(jnp.mean(xv * xv, axis=-1, keepdims=True) + EPS)
        o_ref[...] = ((xv * r * g_ref[...]) * (1.0 + sc_ref[...]) + sh_ref[...]).astype(BF16)

    return pl.pallas_call(
        body, name=name, grid=(S // tm,),
        in_specs=[_rows(tm, D), _vec(D), _vec(D), _vec(D)], out_specs=_rows(tm, D),
        out_shape=jax.ShapeDtypeStruct((S, D), BF16), compiler_params=_par(1),
    )(x, g, sc, sh)


def _lnmod_bwd(dh, x, g, sc, dres, name):
    S, D = x.shape
    tm = _pick(S, 512, 8)

    def body(dh_ref, x_ref, g_ref, sc_ref, dres_ref, dx_ref, dsh_ref, dsc_ref, dg_ref):
        i = pl.program_id(0)
        xv = x_ref[...]
        dh_ = dh_ref[...]
        r = lax.rsqrt(jnp.mean(xv * xv, axis=-1, keepdims=True) + EPS)
        xhat = xv * r
        gv = g_ref[...]
        dn = dh_ * (1.0 + sc_ref[...])
        dxhat = dn * gv
        dx_ref[...] = dres_ref[...] + r * (dxhat - xhat * jnp.mean(dxhat * xhat, axis=-1, keepdims=True))
        _acc(dsh_ref, jnp.sum(dh_, axis=0, keepdims=True), i)
        _acc(dsc_ref, jnp.sum(dh_ * (xhat * gv), axis=0, keepdims=True), i)
        _acc(dg_ref, jnp.sum(dn * xhat, axis=0, keepdims=True), i)

    return pl.pallas_call(
        body, name=name, grid=(S // tm,),
        in_specs=[_rows(tm, D), _rows(tm, D), _vec(D), _vec(D), _rows(tm, D)],
        out_specs=[_rows(tm, D), _vec(D), _vec(D), _vec(D)],
        out_shape=[jax.ShapeDtypeStruct((S, D), F32)] + [jax.ShapeDtypeStruct((1, D), F32)] * 3,
        compiler_params=_arb(1),
    )(dh, x, g, sc, dres)


def _gate_fwd(x, y, gt, name):
    S, D = x.shape
    tm = _pick(S, 512, 8)

    def body(x_ref, y_ref, gt_ref, o_ref):
        o_ref[...] = x_ref[...] + gt_ref[...] * y_ref[...]

    return pl.pallas_call(
        body, name=name, grid=(S // tm,), in_specs=[_rows(tm, D), _rows(tm, D), _vec(D)], out_specs=_rows(tm, D),
        out_shape=jax.ShapeDtypeStruct((S, D), F32), compiler_params=_par(1),
    )(x, y, gt)


def _gate_bwd(dx, y, gt, name):
    S, D = dx.shape
    tm = _pick(S, 512, 8)

    def body(dx_ref, y_ref, gt_ref, dz_ref, dgt_ref):
        i = pl.program_id(0)
        d = dx_ref[...]
        dz_ref[...] = (d * gt_ref[...]).astype(BF16)
        _acc(dgt_ref, jnp.sum(d * y_ref[...], axis=0, keepdims=True), i)

    return pl.pallas_call(
        body, name=name, grid=(S // tm,), in_specs=[_rows(tm, D), _rows(tm, D), _vec(D)],
        out_specs=[_rows(tm, D), _vec(D)],
        out_shape=[jax.ShapeDtypeStruct((S, D), BF16), jax.ShapeDtypeStruct((1, D), F32)],
        compiler_params=_arb(1),
    )(dx, y, gt)


def _ffn_act_fwd(gu, name):
    S, H2 = gu.shape
    H = H2 // 2
    tm = _pick(S, 256, 8)

    def body(g_ref, u_ref, o_ref):
        gv = g_ref[...].astype(F32)
        o_ref[...] = (gv * _sigmoid(gv) * u_ref[...].astype(F32)).astype(BF16)

    return pl.pallas_call(
        body, name=name, grid=(S // tm,), in_specs=[_rows(tm, H, 0), _rows(tm, H, 1)], out_specs=_rows(tm, H),
        out_shape=jax.ShapeDtypeStruct((S, H), BF16), compiler_params=_par(1),
    )(gu, gu)


def _ffn_act_bwd(gu, dact, name):
    S, H2 = gu.shape
    H = H2 // 2
    tm = _pick(S, 256, 8)

    def body(g_ref, u_ref, da_ref, o_ref):
        gv = g_ref[...].astype(F32)
        s = _sigmoid(gv)
        da = da_ref[...].astype(F32)
        o_ref[:, :H] = (da * u_ref[...].astype(F32) * (s * (1.0 + gv * (1.0 - s)))).astype(BF16)
        o_ref[:, H:] = (da * (gv * s)).astype(BF16)

    return pl.pallas_call(
        body, name=name, grid=(S // tm,), in_specs=[_rows(tm, H, 0), _rows(tm, H, 1), _rows(tm, H)],
        out_specs=_rows(tm, H2), out_shape=jax.ShapeDtypeStruct((S, H2), BF16), compiler_params=_par(1),
    )(gu, gu, dact)


def _merge_fwd(proj, pa, pb, name):
    S, D = pa.shape
    tm = _pick(S, 512, 8)

    def body(ga_ref, gb_ref, pa_ref, pb_ref, o_ref):
        o_ref[...] = (_sigmoid(ga_ref[...]) * pa_ref[...].astype(F32)
                      + _sigmoid(gb_ref[...]) * pb_ref[...].astype(F32)).astype(BF16)

    return pl.pallas_call(
        body, name=name, grid=(S // tm,),
        in_specs=[_rows(tm, D, P_GA // D), _rows(tm, D, P_GB // D), _rows(tm, D), _rows(tm, D)],
        out_specs=_rows(tm, D), out_shape=jax.ShapeDtypeStruct((S, D), BF16), compiler_params=_par(1),
    )(proj, proj, pa, pb)


def _merge_bwd(proj, pa, pb, dm, name):
    S, D = pa.shape
    tm = _pick(S, 512, 16)
    rows_j = pl.BlockSpec((tm, D), lambda i, j: (i, 0))

    def body(g_ref, pa_ref, pb_ref, dm_ref, dg_ref, dpa_ref, dpb_ref):
        d = dm_ref[...].astype(F32)
        s = _sigmoid(g_ref[...])
        for branch, p_ref, dp_ref in ((0, pa_ref, dpa_ref), (1, pb_ref, dpb_ref)):
            @pl.when(pl.program_id(1) == branch)
            def _():
                dg_ref[...] = (d * p_ref[...].astype(F32) * s * (1.0 - s)).astype(BF16)
                dp_ref[...] = (d * s).astype(BF16)

    return pl.pallas_call(
        body, name=name, grid=(S // tm, 2),
        in_specs=[pl.BlockSpec((tm, D), lambda i, j: (i, P_GA // D + j)), rows_j, rows_j, rows_j],
        out_specs=[pl.BlockSpec((tm, D), lambda i, j: (i, P_GA // D + j)), rows_j, rows_j],
        out_shape=[jax.ShapeDtypeStruct((S, P_END), BF16), jax.ShapeDtypeStruct((S, D), BF16),
                   jax.ShapeDtypeStruct((S, D), BF16)],
        compiler_params=_arb(2),
    )(proj, pa, pb, dm)


def _write_columns(buf, parts, width, colblk, name):
    S = buf.shape[0]
    tm = _pick(S, 512, 16)
    n = len(parts)

    def body(*refs):
        o_ref = refs[n + 1]
        off = 0
        for p_ref in refs[:n]:
            w = p_ref.shape[1]
            o_ref[:, off:off + w] = p_ref[...].astype(BF16)
            off += w

    return pl.pallas_call(
        body, name=name, grid=(S // tm,), in_specs=[_rows(tm, p.shape[1]) for p in parts] + [HBM],
        out_specs=_rows(tm, width, colblk), out_shape=jax.ShapeDtypeStruct(buf.shape, buf.dtype),
        input_output_aliases={n: 0}, compiler_params=_par(1),
    )(*parts, buf)


def _loss_head(x, g, target, name):
    S, D = x.shape
    tm = _pick(S, 512, 8)

    def body(x_ref, g_ref, t_ref, dx_ref, loss_ref, dg_ref):
        i = pl.program_id(0)
        xv = x_ref[...]
        gv = g_ref[...]
        r = lax.rsqrt(jnp.mean(xv * xv, axis=-1, keepdims=True) + EPS)
        xhat = xv * r
        err = xhat * gv - t_ref[...]
        part = 0.5 * jnp.sum(jnp.mean(err * err, axis=-1, keepdims=True), axis=0, keepdims=True)
        _acc(loss_ref, jnp.broadcast_to(part, (1, LANE)), i)
        dy = err * (1.0 / D)
        _acc(dg_ref, jnp.sum(dy * xhat, axis=0, keepdims=True), i)
        dxhat = dy * gv
        dx_ref[...] = r * (dxhat - xhat * jnp.mean(dxhat * xhat, axis=-1, keepdims=True))

    return pl.pallas_call(
        body, name=name, grid=(S // tm,), in_specs=[_rows(tm, D), _vec(D), _rows(tm, D)],
        out_specs=[_rows(tm, D), _vec(LANE), _vec(D)],
        out_shape=[jax.ShapeDtypeStruct((S, D), F32), jax.ShapeDtypeStruct((1, LANE), F32),
                   jax.ShapeDtypeStruct((1, D), F32)],
        compiler_params=_arb(1),
    )(x, g, target)


HEADS_PER_SLAB = LANE // A_DH
N_SLABS = A_HEADS // HEADS_PER_SLAB
SPAN_BLOCKS = KSPAN // QBLK


def _attn_specs(seg):
    q_spec = pl.BlockSpec((QBLK, LANE), lambda p, m: (m, seg[0] * N_SLABS + p))
    k_specs = [pl.BlockSpec((QBLK, LANE), functools.partial(
        lambda j, p, m: (jnp.maximum(m - (SPAN_BLOCKS - 1) + j, 0), seg[1] * N_SLABS + p), j)) for j in range(SPAN_BLOCKS)]
    v_specs = [pl.BlockSpec((QBLK, LANE), functools.partial(
        lambda j, p, m: (jnp.maximum(m - (SPAN_BLOCKS - 1) + j, 0), seg[2] * N_SLABS + p), j)) for j in range(SPAN_BLOCKS)]
    b_spec = pl.BlockSpec((HEADS_PER_SLAB, QBLK, KSPAN), lambda p, m: (p, 0, 0))
    return q_spec, k_specs, v_specs, b_spec


def _head_lanes(t, hh):
    lane = lax.broadcasted_iota(jnp.int32, t.shape, 1)
    return jnp.where((lane // A_DH) == hh, t, jnp.zeros_like(t))


def _front_mask(m):
    col = lax.broadcasted_iota(jnp.int32, (QBLK, KSPAN), 1)
    return jnp.where(col < (SPAN_BLOCKS - 1 - m) * QBLK, NEG, 0.0)


def _attn_probs(qk, bias, front):
    s = qk * (A_DH ** -0.5) + (bias + front)
    p = jnp.exp(s - jnp.max(s, axis=-1, keepdims=True))
    return p * (1.0 / jnp.sum(p, axis=-1, keepdims=True))


def _grid_ends(nq):
    p, m = pl.program_id(0), pl.program_id(1)
    return (p == 0) & (m == 0), (p == N_SLABS - 1) & (m == nq // 2), (p == N_SLABS - 1) & (m == nq - 1)


def _attn_fwd(proj, big, name, gather=None):
    S = proj.shape[0]
    q_spec, k_specs, v_specs, b_spec = _attn_specs((0, 1, 2))
    shards, layer = gather if gather is not None else ((), None)
    ng = len(shards)

    def body(q_ref, k0, k1, k2, v0, v1, v2, b_ref, *rest):
        srcs, o_ref, gouts, sems = rest[:ng], rest[ng], rest[ng + 1:2 * ng + 1], rest[2 * ng + 1:]
        done = _carry(_gather_phases(layer, srcs, gouts, *sems), *_grid_ends(S // QBLK)) if ng else None
        m = pl.program_id(1)
        q = q_ref[...].astype(BF16)
        k = jnp.concatenate([k0[...], k1[...], k2[...]], axis=0).astype(BF16)
        v = jnp.concatenate([v0[...], v1[...], v2[...]], axis=0).astype(BF16)
        front = _front_mask(m)
        heads = range(HEADS_PER_SLAB)
        scores = [_nt(_head_lanes(q, hh), k) for hh in heads]
        probs = [_attn_probs(scores[hh], b_ref[hh], front).astype(BF16) for hh in heads]
        outs = [_nn(probs[hh], v) for hh in heads]
        lane = lax.broadcasted_iota(jnp.int32, (QBLK, LANE), 1)
        o_ref[...] = jnp.where(lane < A_DH, outs[0], outs[1]).astype(BF16)
        if ng:
            done()

    res = pl.pallas_call(
        body, name=name, grid=(N_SLABS, S // QBLK), in_specs=[q_spec] + k_specs + v_specs + [b_spec] + [HBM] * ng,
        out_specs=[pl.BlockSpec((QBLK, LANE), lambda p, m: (m, p))] + [HBM] * ng,
        out_shape=[jax.ShapeDtypeStruct((S, A_HEADS * A_DH), BF16)] + _gather_out_shapes(shards),
        scratch_shapes=_gather_scratch(ng) if ng else [], compiler_params=_arb(2),
    )(proj, proj, proj, proj, proj, proj, proj, big, *shards)
    return res[0], list(res[1:])


def _attn_bwd(proj, big, dya, name, exchange=()):
    S = proj.shape[0]
    W = A_HEADS * A_DH
    q_spec, k_specs, v_specs, b_spec = _attn_specs((0, 1, 2))
    out_q = pl.BlockSpec((QBLK, LANE), lambda p, m: (m, p))
    out_kv = pl.BlockSpec((S, LANE), lambda p, m: (0, p))
    ne = len(exchange)

    def body(q_ref, k0, k1, k2, v0, v1, v2, b_ref, do_ref, *rest):
        srcs, (dq_ref, dk_ref, dv_ref, db_ref), eouts, sems = rest[:ne], rest[ne:ne + 4], rest[ne + 4:2 * ne + 4], rest[2 * ne + 4:]
        done = _carry(_chips_phases(srcs, eouts, *sems), *_grid_ends(S // QBLK)) if ne else None
        m = pl.program_id(1)

        @pl.when(m == 0)
        def _():
            dk_ref[...] = jnp.zeros_like(dk_ref)
            dv_ref[...] = jnp.zeros_like(dv_ref)
            db_ref[...] = jnp.zeros_like(db_ref)

        q = q_ref[...].astype(BF16)
        k = jnp.concatenate([k0[...], k1[...], k2[...]], axis=0).astype(BF16)
        v = jnp.concatenate([v0[...], v1[...], v2[...]], axis=0).astype(BF16)
        do = do_ref[...]
        front = _front_mask(m)
        heads = range(HEADS_PER_SLAB)
        qh = [_head_lanes(q, hh) for hh in heads]
        doh = [_head_lanes(do, hh) for hh in heads]
        scores = [_nt(qh[hh], k) for hh in heads]
        dps = [_nt(doh[hh], v) for hh in heads]
        ps = [_attn_probs(scores[hh], b_ref[hh], front) for hh in heads]
        dss = [ps[hh] * (dps[hh] - jnp.sum(ps[hh] * dps[hh], axis=-1, keepdims=True)) for hh in heads]
        for hh in heads:
            db_ref[hh] += dss[hh]
        dsb = [(dss[hh] * (A_DH ** -0.5)).astype(BF16) for hh in heads]
        dqs = [_nn(dsb[hh], k) for hh in heads]
        dk = sum(_tn(dsb[hh], qh[hh]) for hh in heads)
        dv = sum(_tn(ps[hh].astype(BF16), doh[hh]) for hh in heads)
        lane = lax.broadcasted_iota(jnp.int32, (QBLK, LANE), 1)
        dq_ref[...] = jnp.where(lane < A_DH, dqs[0], dqs[1])
        for j in range(SPAN_BLOCKS):
            blk = m - (SPAN_BLOCKS - 1) + j

            @pl.when(blk >= 0)
            def _():
                off = pl.multiple_of(blk * QBLK, QBLK)
                dk_ref[pl.ds(off, QBLK), :] += dk[j * QBLK:(j + 1) * QBLK]
                dv_ref[pl.ds(off, QBLK), :] += dv[j * QBLK:(j + 1) * QBLK]
        if ne:
            done()

    res = pl.pallas_call(
        body, name=name, grid=(N_SLABS, S // QBLK),
        in_specs=[q_spec] + k_specs + v_specs + [b_spec, pl.BlockSpec((QBLK, LANE), lambda p, m: (m, p))] + [HBM] * ne,
        out_specs=[out_q, out_kv, out_kv, b_spec] + [HBM] * ne,
        out_shape=[jax.ShapeDtypeStruct((S, W), F32)] * 3 + [jax.ShapeDtypeStruct((A_HEADS, QBLK, KSPAN), F32)]
        + _chips_out_shapes(exchange),
        scratch_shapes=_chips_scratch(ne) if ne else [], compiler_params=_arb(2),
    )(proj, proj, proj, proj, proj, proj, proj, big, dya, *exchange)
    return tuple(res[:4]) + (list(res[4:]),)


NREL_PAD = 3 * LANE
SKEW_W = 1024


def _rel_table_grad(dbig, name):
    H, R, C = dbig.shape

    def body(d_ref, o_ref):
        x = jnp.concatenate([d_ref[...], jnp.zeros((R, SKEW_W - C), F32)], axis=1)
        row = lax.broadcasted_iota(jnp.int32, (R, SKEW_W), 0)
        for b in range(R.bit_length() - 1):
            x = jnp.where(((row >> b) & 1) == 1, pltpu.roll(x, SKEW_W - (1 << b), 1), x)
        e = jnp.sum(x, axis=0, keepdims=True)
        xi = lax.broadcasted_iota(jnp.int32, (SKEW_W, NREL_PAD), 0)
        r = lax.broadcasted_iota(jnp.int32, (SKEW_W, NREL_PAD), 1)
        diag = jnp.where(xi < C, xi, xi - SKEW_W)
        rel = jnp.clip(A_PAST * CHUNK - diag, -A_MAX_REL, A_MAX_REL) + A_MAX_REL
        o_ref[...] = _nn(e, jnp.where(rel == r, 1.0, 0.0).astype(F32), HI)

    return pl.pallas_call(
        body, name=name, grid=(H,), in_specs=[pl.BlockSpec((None, R, C), lambda h: (h, 0, 0))],
        out_specs=pl.BlockSpec((None, 1, NREL_PAD), lambda h: (h, 0, 0)),
        out_shape=jax.ShapeDtypeStruct((H, 1, NREL_PAD), F32), compiler_params=_par(1),
    )(dbig)


def _chunk_cumsum_matrix(n, reverse):
    j = lax.broadcasted_iota(jnp.int32, (n, n), 0)
    i = lax.broadcasted_iota(jnp.int32, (n, n), 1)
    same = (j // CHUNK) == (i // CHUNK)
    return jnp.where(same & ((j >= i) if reverse else (j <= i)), 1.0, 0.0).astype(F32)


def _gdn_gates_fwd(proj, alog, dtb, name):
    Hh, S = B_HEADS, proj.shape[0]
    tl = _pick(S, 512)
    row = pl.BlockSpec((Hh, tl), lambda i: (0, i))
    col = pl.BlockSpec((Hh, 1), lambda i: (0, 0))

    def body(ba_ref, al_ref, dt_ref, beta_ref, gam_ref, b_ref, a_ref, t_ref):
        t_ref[...] = ba_ref[...].T
        b = t_ref[0:Hh, :]
        a = t_ref[Hh:2 * Hh, :]
        z = a + dt_ref[...]
        sp = jnp.maximum(z, 0.0) + jnp.log(1.0 + jnp.exp(-jnp.abs(z)))
        g = -jnp.exp(al_ref[...]) * sp
        beta_ref[...] = _sigmoid(b)
        gam_ref[...] = _nn(g, _chunk_cumsum_matrix(tl, False), HI)
        b_ref[...] = b
        a_ref[...] = a

    return pl.pallas_call(
        body, name=name, grid=(S // tl,), in_specs=[pl.BlockSpec((tl, LANE), lambda i: (i, P_BA // LANE)), col, col],
        out_specs=[row] * 4, out_shape=[jax.ShapeDtypeStruct((Hh, S), F32)] * 4,
        scratch_shapes=[pltpu.VMEM((LANE, tl), F32)], compiler_params=_par(1),
    )(proj, alog, dtb)


def _gdn_gates_bwd(dbeta, dgam_a, dgam_b, b_t, a_t, alog, dtb, name):
    Hh, S = b_t.shape
    tl = _pick(S, 512)
    row = pl.BlockSpec((Hh, tl), lambda i: (0, i))
    col = pl.BlockSpec((Hh, 1), lambda i: (0, 0))
    accs = pl.BlockSpec((Hh, LANE), lambda i: (0, 0))

    def body(dbeta_ref, dga_ref, dgb_ref, b_ref, a_ref, al_ref, dt_ref, db_ref, da_ref, dal_ref, ddt_ref):
        i = pl.program_id(0)
        z = a_ref[...] + dt_ref[...]
        sp = jnp.maximum(z, 0.0) + jnp.log(1.0 + jnp.exp(-jnp.abs(z)))
        ea = jnp.exp(al_ref[...])
        dg = _nn(dga_ref[...] + dgb_ref[...], _chunk_cumsum_matrix(tl, True), HI)
        da = dg * (-ea) * _sigmoid(z)
        beta = _sigmoid(b_ref[...])
        db_ref[...] = dbeta_ref[...] * beta * (1.0 - beta)
        da_ref[...] = da
        _acc(dal_ref, jnp.broadcast_to(jnp.sum(dg * (-ea * sp), axis=1, keepdims=True), (Hh, LANE)), i)
        _acc(ddt_ref, jnp.broadcast_to(jnp.sum(da, axis=1, keepdims=True), (Hh, LANE)), i)

    return pl.pallas_call(
        body, name=name, grid=(S // tl,), in_specs=[row] * 5 + [col, col], out_specs=[row, row, accs, accs],
        out_shape=[jax.ShapeDtypeStruct((Hh, S), F32)] * 2 + [jax.ShapeDtypeStruct((Hh, LANE), F32)] * 2,
        compiler_params=_arb(1),
    )(dbeta, dgam_a, dgam_b, b_t, a_t, alog, dtb)


HALO = 8


def _conv_silu(xx_ref, w_ref, tm):
    y = w_ref[0:1, :] * xx_ref[pl.ds(HALO - CONV_K + 1, tm), :]
    for j in range(1, CONV_K):
        y = y + w_ref[j:j + 1, :] * xx_ref[pl.ds(HALO - CONV_K + 1 + j, tm), :]
    return y, y * _sigmoid(y)


def _fill_prev_halo(xx_ref, x_ref, prev_ref, i, tm):
    xx_ref[pl.ds(HALO, tm), :] = x_ref[...]

    @pl.when(i == 0)
    def _():
        xx_ref[pl.ds(0, HALO), :] = jnp.zeros((HALO, xx_ref.shape[1]), F32)

    @pl.when(i != 0)
    def _():
        xx_ref[pl.ds(0, HALO), :] = prev_ref[...]


def _gdn_pre_specs(tm, C, colblk):
    cur = pl.BlockSpec((tm, C), lambda i: (i, colblk))
    prev = pl.BlockSpec((HALO, C), lambda i: (jnp.maximum(i * (tm // HALO) - 1, 0), colblk))
    return cur, prev


def _gdn_pre_fwd(proj, wconv, name):
    S = proj.shape[0]
    C = 3 * B_HEADS * B_DH
    W = B_HEADS * B_DH
    tm = _pick(S, 256, 8)
    cur, prev = _gdn_pre_specs(tm, C, P_QKVB // C)

    def body(x_ref, prev_ref, w_ref, q_ref, k_ref, v_ref, xx_ref):
        i = pl.program_id(0)
        _fill_prev_halo(xx_ref, x_ref, prev_ref, i, tm)
        _, sl = _conv_silu(xx_ref, w_ref, tm)
        for h in range(B_HEADS):
            hs = slice(h * B_DH, (h + 1) * B_DH)
            q = sl[:, h * B_DH:(h + 1) * B_DH]
            k = sl[:, W + h * B_DH:W + (h + 1) * B_DH]
            q_ref[:, hs] = q * (lax.rsqrt(jnp.sum(q * q, axis=-1, keepdims=True) + EPS) * (B_DH ** -0.5))
            k_ref[:, hs] = k * lax.rsqrt(jnp.sum(k * k, axis=-1, keepdims=True) + EPS)
        v_ref[...] = sl[:, 2 * W:]

    return pl.pallas_call(
        body, name=name, grid=(S // tm,), in_specs=[cur, prev, pl.BlockSpec((CONV_K, C), lambda i: (0, 0))],
        out_specs=[_rows(tm, W)] * 3, out_shape=[jax.ShapeDtypeStruct((S, W), F32)] * 3,
        scratch_shapes=[pltpu.VMEM((HALO + tm, C), F32)], compiler_params=_par(1),
    )(proj, proj, wconv)


def _gdn_pre_bwd_a(proj, wconv, dqn, dkn, dv, name):
    S = proj.shape[0]
    C = 3 * B_HEADS * B_DH
    W = B_HEADS * B_DH
    tm = _pick(S, 256, 8)
    cur, prev = _gdn_pre_specs(tm, C, P_QKVB // C)

    def body(x_ref, prev_ref, w_ref, dq_ref, dk_ref, dv_ref, dy_ref, xx_ref):
        i = pl.program_id(0)
        _fill_prev_halo(xx_ref, x_ref, prev_ref, i, tm)
        y, sl = _conv_silu(xx_ref, w_ref, tm)
        sg = _sigmoid(y)
        dsilu = sg * (1.0 + y * (1.0 - sg))
        for h in range(B_HEADS):
            for base, d_ref, c in ((0, dq_ref, B_DH ** -0.5), (W, dk_ref, 1.0)):
                lo = base + h * B_DH
                t = sl[:, lo:lo + B_DH]
                d = d_ref[:, h * B_DH:(h + 1) * B_DH]
                r = lax.rsqrt(jnp.sum(t * t, axis=-1, keepdims=True) + EPS)
                dt = (c * r) * (d - t * (r * r) * jnp.sum(d * t, axis=-1, keepdims=True))
                dy_ref[:, lo:lo + B_DH] = dt * dsilu[:, lo:lo + B_DH]
        dy_ref[:, 2 * W:] = dv_ref[...] * dsilu[:, 2 * W:]

    return pl.pallas_call(
        body, name=name, grid=(S // tm,),
        in_specs=[cur, prev, pl.BlockSpec((CONV_K, C), lambda i: (0, 0))] + [_rows(tm, W)] * 3,
        out_specs=_rows(tm, C), out_shape=jax.ShapeDtypeStruct((S, C), F32),
        scratch_shapes=[pltpu.VMEM((HALO + tm, C), F32)], compiler_params=_par(1),
    )(proj, proj, wconv, dqn, dkn, dv)


def _gdn_pre_bwd_b(proj, wconv, dy, dproj, name):
    S = proj.shape[0]
    C = 3 * B_HEADS * B_DH
    tm = _pick(S, 256, 16)
    nt_ = S // tm
    cur, prev = _gdn_pre_specs(tm, C, P_QKVB // C)
    nxt = pl.BlockSpec((HALO, C), lambda i: (jnp.minimum((i + 1) * (tm // HALO), S // HALO - 1), 0))

    def body(x_ref, prev_ref, w_ref, dy_ref, next_ref, _, dx_ref, dw_ref, xx_ref, dd_ref):
        i = pl.program_id(0)
        _fill_prev_halo(xx_ref, x_ref, prev_ref, i, tm)
        dyv = dy_ref[...]
        dd_ref[pl.ds(0, tm), :] = dyv

        @pl.when(i == nt_ - 1)
        def _():
            dd_ref[pl.ds(tm, HALO), :] = jnp.zeros((HALO, C), F32)

        @pl.when(i != nt_ - 1)
        def _():
            dd_ref[pl.ds(tm, HALO), :] = next_ref[...]

        dx = w_ref[0:1, :] * dd_ref[pl.ds(CONV_K - 1, tm), :]
        for j in range(1, CONV_K):
            dx = dx + w_ref[j:j + 1, :] * dd_ref[pl.ds(CONV_K - 1 - j, tm), :]
        dx_ref[...] = dx.astype(BF16)
        dw = jnp.concatenate(
            [jnp.sum(dyv * xx_ref[pl.ds(HALO - CONV_K + 1 + j, tm), :], axis=0, keepdims=True) for j in range(CONV_K)],
            axis=0)
        _acc(dw_ref, dw, i)

    return pl.pallas_call(
        body, name=name, grid=(nt_,),
        in_specs=[cur, prev, pl.BlockSpec((CONV_K, C), lambda i: (0, 0)), _rows(tm, C), nxt, HBM],
        out_specs=[_rows(tm, C, P_QKVB // C), pl.BlockSpec((CONV_K, C), lambda i: (0, 0))],
        out_shape=[jax.ShapeDtypeStruct(dproj.shape, BF16), jax.ShapeDtypeStruct((CONV_K, C), F32)],
        input_output_aliases={5: 0},
        scratch_shapes=[pltpu.VMEM((HALO + tm, C), F32), pltpu.VMEM((tm + HALO, C), F32)],
        compiler_params=_arb(1),
    )(proj, proj, wconv, dy, dy, dproj)


def _chunk_masks():
    row = lax.broadcasted_iota(jnp.int32, (CHUNK, CHUNK), 0)
    col = lax.broadcasted_iota(jnp.int32, (CHUNK, CHUNK), 1)
    return row >= col, row > col


def _chunk_local(q, k, vv, bc, gc, gr, tri):
    dm = jnp.where(tri, jnp.exp(jnp.where(tri, gc - gr, 0.0)), 0.0)
    kk = _bnt(k, k)
    glast = gr[..., CHUNK - 1:CHUNK]
    ep = jnp.exp(gc)
    em = jnp.exp(glast - gc)
    el = jnp.exp(glast)
    return dm, kk, ep, em, el, vv * bc, k * (bc * ep)


def _unit_lower_inverse(low):
    row = lax.broadcasted_iota(jnp.int32, (CHUNK, CHUNK), 0)
    col = lax.broadcasted_iota(jnp.int32, (CHUNK, CHUNK), 1)
    p = -low
    t = jnp.where(row == col, 1.0, 0.0).astype(F32) + p
    steps = CHUNK.bit_length() - 2
    for _ in range(steps):
        p = _nn(p, p, SOLVE_PREC)
        t = t + _nn(t, p, SOLVE_PREC)
    return t


GROUP = 4


LATE = 7


def _carry(phases, first, middle, last):
    if len(phases) == 3:
        pl.when(first)(phases[0])
        pl.when(middle)(phases[1])
        return lambda: pl.when(last)(phases[2])
    pl.when(first)(phases[0])
    return lambda: pl.when(last)(phases[1])


def _pairs(nchunks):
    return [(c, h) for c in range(nchunks) for h in range(B_HEADS)]


def _tok(c):
    return slice(c * CHUNK, (c + 1) * CHUNK)


def _head(h):
    return slice(h * B_DH, (h + 1) * B_DH)


def _stack_tokens(ref, nchunks):
    return jnp.stack([ref[_tok(c), _head(h)] for c, h in _pairs(nchunks)])


def _stack_cols(ref, nchunks):
    per_chunk = [ref[c] for c in range(nchunks)] if len(ref.shape) == 3 else [ref[...]]
    return jnp.stack([per_chunk[c][:, h:h + 1] for c, h in _pairs(nchunks)])


def _stack_rows(ref, nchunks):
    if len(ref.shape) == 3:
        return jnp.stack([ref[c, h:h + 1, :] for c, h in _pairs(nchunks)])
    return jnp.stack([ref[h:h + 1, :] for _, h in _pairs(1)])


def _gdn_group_specs(ng_steps, W):
    tok = pl.BlockSpec((GROUP * CHUNK, W), lambda i: (i, 0))
    colv = pl.BlockSpec((GROUP, CHUNK, B_HEADS), lambda i: (i, 0, 0))
    rowv = pl.BlockSpec((GROUP, B_HEADS, CHUNK), lambda i: (i, 0, 0))
    mat = pl.BlockSpec((GROUP, B_HEADS, CHUNK, CHUNK), lambda i: (i, 0, 0, 0))
    return tok, colv, rowv, mat


def _gdn_local_fwd(qn, kn, v, bcol, gcol, grow, name, gather=None):
    S, Wd = qn.shape
    nc = S // CHUNK
    steps = nc // GROUP
    tok, colv, rowv, mat = _gdn_group_specs(steps, Wd)
    shards, layer = gather if gather is not None else ((), None)
    ng = len(shards)

    def body(q_ref, k_ref, v_ref, bc_ref, gc_ref, gr_ref, *rest):
        srcs, (t_ref, a_ref, u_ref, w_ref), gouts, sems = rest[:ng], rest[ng:ng + 4], rest[ng + 4:2 * ng + 4], rest[2 * ng + 4:]
        i = pl.program_id(0)
        done = _carry(_gather_phases(layer, srcs, gouts, *sems), i == 0, i == LATE * steps // 8, i == steps - 1) if ng else None
        tri, strict = _chunk_masks()
        q, k, vv = (_stack_tokens(r, GROUP) for r in (q_ref, k_ref, v_ref))
        bc, gc, gr = _stack_cols(bc_ref, GROUP), _stack_cols(gc_ref, GROUP), _stack_rows(gr_ref, GROUP)
        dm, kk, ep, em, el, vb, kb = _chunk_local(q, k, vv, bc, gc, gr, tri)
        t = _unit_lower_inverse(jnp.where(strict, bc * kk * dm, 0.0))
        a = _bnt(q, k) * dm
        u = _nn(t, vb, SOLVE_PREC)
        w = _nn(t, kb, SOLVE_PREC)
        for n, (c, h) in enumerate(_pairs(GROUP)):
            t_ref[c, h] = t[n]
            a_ref[c, h] = a[n]
            u_ref[_tok(c), _head(h)] = u[n]
            w_ref[_tok(c), _head(h)] = w[n]
        if ng:
            done()

    res = pl.pallas_call(
        body, name=name, grid=(steps,), in_specs=[tok, tok, tok, colv, colv, rowv] + [HBM] * ng,
        out_specs=[mat, mat, tok, tok] + [HBM] * ng,
        out_shape=[jax.ShapeDtypeStruct((nc, B_HEADS, CHUNK, CHUNK), F32)] * 2 + [jax.ShapeDtypeStruct((S, Wd), F32)] * 2
        + _gather_out_shapes(shards),
        scratch_shapes=_gather_scratch(ng) if ng else [], compiler_params=_arb(1),
    )(qn, kn, v, bcol, gcol, grow, *shards)
    return res[0], res[1], res[2], res[3], list(res[4:])


def _scan_decays(gc, gr):
    glast = gr[..., CHUNK - 1:CHUNK]
    return jnp.exp(gc), jnp.exp(glast - gc), jnp.exp(glast)


SCAN = 2


def _gdn_scan_specs(steps, rev):
    idx = (lambda i: steps - 1 - i) if rev else (lambda i: i)
    W = B_HEADS * B_DH
    tok = pl.BlockSpec((SCAN * CHUNK, W), lambda i: (idx(i), 0))
    colv = pl.BlockSpec((SCAN, CHUNK, B_HEADS), lambda i: (idx(i), 0, 0))
    rowv = pl.BlockSpec((SCAN, B_HEADS, CHUNK), lambda i: (idx(i), 0, 0))
    mat = pl.BlockSpec((SCAN, B_HEADS, CHUNK, CHUNK), lambda i: (idx(i), 0, 0, 0))
    smat = pl.BlockSpec((SCAN, B_HEADS, B_DH, B_DH), lambda i: (idx(i), 0, 0, 0))
    return tok, colv, rowv, mat, smat


def _chunk_rows(ref, c):
    return ref.at[pl.ds(c * CHUNK, CHUNK)]


def _gdn_scan_fwd(qn, kn, u, w, a, gcol, grow, name, gather=None):
    S, Wd = qn.shape
    nc = S // CHUNK
    steps = nc // SCAN
    tok, colv, rowv, mat, smat = _gdn_scan_specs(steps, False)
    shards, layer = gather if gather is not None else ((), None)
    ng = len(shards)

    def body(q_ref, k_ref, u_ref, w_ref, a_ref, gc_ref, gr_ref, *rest):
        srcs, (o_ref, sh_ref), gouts = rest[:ng], rest[ng:ng + 2], rest[ng + 2:2 * ng + 2]
        st_ref, sems = rest[2 * ng + 2], rest[2 * ng + 3:]
        i = pl.program_id(0)
        done = (_carry(_gather_phases(layer, srcs, gouts, *sems), i == 0, i == LATE * steps // 8, i == steps - 1)
                if ng else None)

        @pl.when(i == 0)
        def _():
            st_ref[...] = jnp.zeros_like(st_ref)

        for c in range(SCAN):
            ep, em, el = _scan_decays(_stack_cols(gc_ref.at[c], 1), _stack_rows(gr_ref.at[c], 1))
            q, k, u, w = (_stack_tokens(_chunk_rows(r, c), 1) for r in (q_ref, k_ref, u_ref, w_ref))
            s0 = st_ref[...]
            ut = u - _bnn(w, s0)
            o = _bnn(q * ep, s0) + _bnn(a_ref[c], ut)
            st_ref[...] = el * s0 + _btn(k * em, ut)
            sh_ref[c] = s0
            for h in range(B_HEADS):
                o_ref[_tok(c), _head(h)] = o[h]
        if ng:
            done()

    res = pl.pallas_call(
        body, name=name, grid=(steps,), in_specs=[tok, tok, tok, tok, mat, colv, rowv] + [HBM] * ng,
        out_specs=[tok, smat] + [HBM] * ng,
        out_shape=[jax.ShapeDtypeStruct((S, Wd), F32), jax.ShapeDtypeStruct((nc, B_HEADS, B_DH, B_DH), F32)]
        + _gather_out_shapes(shards),
        scratch_shapes=[pltpu.VMEM((B_HEADS, B_DH, B_DH), F32)] + (_gather_scratch(ng) if ng else []),
        compiler_params=_arb(1),
    )(qn, kn, u, w, a, gcol, grow, *shards)
    return res[0], res[1], list(res[2:])


def _gdn_scan_bwd(qn, kn, u, w, a, gcol, grow, ssave, do, name):
    S, Wd = qn.shape
    nc = S // CHUNK
    steps = nc // SCAN
    tok, colv, rowv, mat, smat = _gdn_scan_specs(steps, True)

    def body(q_ref, k_ref, u_ref, w_ref, a_ref, gc_ref, gr_ref, sh_ref, do_ref,
             du_ref, dw_ref, dqd_ref, dkd_ref, da_ref, dgl_ref, ds_ref):
        i = pl.program_id(0)

        @pl.when(i == 0)
        def _():
            ds_ref[...] = jnp.zeros_like(ds_ref)

        tri, _ = _chunk_masks()
        sub4 = lax.broadcasted_iota(jnp.int32, (B_HEADS, CHUNK), 0)
        lane_last = lax.broadcasted_iota(jnp.int32, (1, CHUNK), 1) == CHUNK - 1
        for c in reversed(range(SCAN)):
            ep, em, el = _scan_decays(_stack_cols(gc_ref.at[c], 1), _stack_rows(gr_ref.at[c], 1))
            q, k, u, w, dout = (_stack_tokens(_chunk_rows(r, c), 1) for r in (q_ref, k_ref, u_ref, w_ref, do_ref))
            s0 = sh_ref[c]
            ds = ds_ref[...]
            ut = u - _bnn(w, s0)
            dut = _btn(a_ref[c], dout) + _bnn(k * em, ds)
            ds_ref[...] = el * ds + _btn(q * ep, dout) - _btn(w, dut)
            dw = -_bnt(dut, s0)
            dqd = _bnt(dout, s0)
            dkd = _bnt(ut, ds)
            da_ref[c] = jnp.where(tri, _bnt(dout, ut), 0.0)
            d_el = jnp.sum(jnp.sum(s0 * ds, axis=-1, keepdims=True), axis=-2, keepdims=True)
            last = d_el * el
            dgl_acc = jnp.zeros((B_HEADS, CHUNK), F32)
            for h in range(B_HEADS):
                du_ref[_tok(c), _head(h)] = dut[h]
                dw_ref[_tok(c), _head(h)] = dw[h]
                dqd_ref[_tok(c), _head(h)] = dqd[h]
                dkd_ref[_tok(c), _head(h)] = dkd[h]
                dgl_acc = jnp.where(sub4 == h, jnp.where(lane_last, last[h], 0.0), dgl_acc)
            dgl_ref[c] = dgl_acc

    return pl.pallas_call(
        body, name=name, grid=(steps,), in_specs=[tok, tok, tok, tok, mat, colv, rowv, smat, tok],
        out_specs=[tok, tok, tok, tok, mat, rowv],
        out_shape=[jax.ShapeDtypeStruct((S, Wd), F32)] * 4 + [jax.ShapeDtypeStruct((nc, B_HEADS, CHUNK, CHUNK), F32),
                                                             jax.ShapeDtypeStruct((nc, B_HEADS, CHUNK), F32)],
        scratch_shapes=[pltpu.VMEM((B_HEADS, B_DH, B_DH), F32)], compiler_params=_arb(1),
    )(qn, kn, u, w, a, gcol, grow, ssave, do)


def _gdn_local_bwd(qn, kn, v, bcol, gcol, grow, tsave, du, dw, dqd, dkd, da, dgl, name, exchange=()):
    S, Wd = qn.shape
    nc = S // CHUNK
    steps = nc // GROUP
    tok, colv, rowv, mat = _gdn_group_specs(steps, Wd)
    ne = len(exchange)

    def body(q_ref, k_ref, v_ref, bc_ref, gc_ref, gr_ref, t_ref, du_ref, dw_ref, dqd_ref, dkd_ref, da_ref, dgl_ref, *rest):
        srcs, (dq_ref, dk_ref, dv_ref, dbc_ref, dgc_ref, dgr_ref) = rest[:ne], rest[ne:ne + 6]
        eouts, sems = rest[ne + 6:2 * ne + 6], rest[2 * ne + 6:]
        i = pl.program_id(0)
        done = _carry(_chips_phases(srcs, eouts, *sems), i == 0, None, i == steps - 1) if ne else None
        tri, strict = _chunk_masks()
        lane4 = lax.broadcasted_iota(jnp.int32, (CHUNK, B_HEADS), 1)
        sub4 = lax.broadcasted_iota(jnp.int32, (B_HEADS, CHUNK), 0)
        lane_last = lax.broadcasted_iota(jnp.int32, (1, CHUNK), 1) == CHUNK - 1
        q, k, vv, dut, dwv, dqd, dkd = (_stack_tokens(r, GROUP)
                                        for r in (q_ref, k_ref, v_ref, du_ref, dw_ref, dqd_ref, dkd_ref))
        bc, gc, gr = _stack_cols(bc_ref, GROUP), _stack_cols(gc_ref, GROUP), _stack_rows(gr_ref, GROUP)
        dm, kk, ep, em, el, vb, kb = _chunk_local(q, k, vv, bc, gc, gr, tri)
        t = jnp.stack([t_ref[c, h] for c, h in _pairs(GROUP)])
        dav = jnp.stack([da_ref[c, h] for c, h in _pairs(GROUP)])
        qk = _bnt(q, k)
        dt = _nt(dut, vb, SOLVE_PREC) + _nt(dwv, kb, SOLVE_PREC)
        dvb = _tn(t, dut, SOLVE_PREC)
        dkb = _tn(t, dwv, SOLVE_PREC)
        dl = jnp.where(strict, -_tn(t, _nt(dt, t, SOLVE_PREC), SOLVE_PREC), 0.0)
        g1 = dl * dm
        dkb_k = jnp.sum(dkb * k, axis=-1, keepdims=True)
        dbeta = jnp.sum(g1 * kk, axis=-1, keepdims=True) + jnp.sum(dvb * vv, axis=-1, keepdims=True) + dkb_k * ep
        dkk = g1 * bc
        ddm = dl * (bc * kk) + dav * qk
        dqk = dav * dm
        dq = _bnn(dqk, k) + dqd * ep
        dk = _btn(dqk, q) + _bnn(dkk, k) + _btn(dkk, k) + dkb * (bc * ep) + dkd * em
        dv = dvb * bc
        dep = dkb_k * bc + jnp.sum(dqd * q, axis=-1, keepdims=True)
        dem = jnp.sum(dkd * k, axis=-1, keepdims=True)
        mm = ddm * dm
        dgam_c = jnp.sum(mm, axis=-1, keepdims=True) + dep * ep - dem * em
        dglast = jnp.sum(dem * em, axis=-2, keepdims=True)
        dgam_r = -jnp.sum(mm, axis=-2, keepdims=True) + jnp.where(lane_last, dglast, 0.0)
        for c in range(GROUP):
            dbc_acc = jnp.zeros((CHUNK, B_HEADS), F32)
            dgc_acc = jnp.zeros((CHUNK, B_HEADS), F32)
            dgr_acc = jnp.zeros((B_HEADS, CHUNK), F32)
            for h in range(B_HEADS):
                n = c * B_HEADS + h
                dq_ref[_tok(c), _head(h)] = dq[n]
                dk_ref[_tok(c), _head(h)] = dk[n]
                dv_ref[_tok(c), _head(h)] = dv[n]
                dbc_acc = jnp.where(lane4 == h, dbeta[n], dbc_acc)
                dgc_acc = jnp.where(lane4 == h, dgam_c[n], dgc_acc)
                dgr_acc = jnp.where(sub4 == h, dgam_r[n], dgr_acc)
            dbc_ref[c] = dbc_acc
            dgc_ref[c] = dgc_acc
            dgr_ref[c] = dgr_acc + dgl_ref[c]
        if ne:
            done()

    res = pl.pallas_call(
        body, name=name, grid=(steps,),
        in_specs=[tok, tok, tok, colv, colv, rowv, mat, tok, tok, tok, tok, mat, rowv] + [HBM] * ne,
        out_specs=[tok, tok, tok, colv, colv, rowv] + [HBM] * ne,
        out_shape=[jax.ShapeDtypeStruct((S, Wd), F32)] * 3
        + [jax.ShapeDtypeStruct((nc, CHUNK, B_HEADS), F32)] * 2 + [jax.ShapeDtypeStruct((nc, B_HEADS, CHUNK), F32)]
        + _chips_out_shapes(exchange),
        scratch_shapes=_chips_scratch(ne) if ne else [], compiler_params=_arb(1),
    )(qn, kn, v, bcol, gcol, grow, tsave, du, dw, dqd, dkd, da, dgl, *exchange)
    return tuple(res[:6]) + (list(res[6:]),)


def _gdn_post_fwd(o, proj, ng, name):
    S, W = o.shape
    tm = _pick(S, 512, 8)

    def body(o_ref, z_ref, g_ref, y_ref):
        gv = g_ref[...]
        for h in range(B_HEADS):
            hs = slice(h * B_DH, (h + 1) * B_DH)
            oh = o_ref[:, hs]
            z = z_ref[:, hs]
            r = lax.rsqrt(jnp.mean(oh * oh, axis=-1, keepdims=True) + EPS)
            y_ref[:, hs] = (oh * r * gv * (z * _sigmoid(z))).astype(BF16)

    return pl.pallas_call(
        body, name=name, grid=(S // tm,), in_specs=[_rows(tm, W), _rows(tm, W, P_Z // W), _vec(B_DH)],
        out_specs=_rows(tm, W), out_shape=jax.ShapeDtypeStruct((S, W), BF16), compiler_params=_par(1),
    )(o, proj, ng)


def _gdn_post_bwd(dy, o, proj, ng, dproj, name):
    S, W = o.shape
    tm = _pick(S, 512, 16)

    def body(dy_ref, o_ref, z_ref, g_ref, _, do_ref, dz_ref, dg_ref):
        i = pl.program_id(0)
        gv = g_ref[...]
        dg = jnp.zeros((1, B_DH), F32)
        for h in range(B_HEADS):
            hs = slice(h * B_DH, (h + 1) * B_DH)
            oh = o_ref[:, hs]
            z = z_ref[:, hs]
            d = dy_ref[:, hs]
            r = lax.rsqrt(jnp.mean(oh * oh, axis=-1, keepdims=True) + EPS)
            n = oh * r
            sg = _sigmoid(z)
            sz = z * sg
            dn = d * gv * sz
            dg = dg + jnp.sum(d * n * sz, axis=0, keepdims=True)
            dz_ref[:, hs] = (d * n * gv * (sg * (1.0 + z * (1.0 - sg)))).astype(BF16)
            do_ref[:, hs] = r * (dn - n * jnp.mean(dn * n, axis=-1, keepdims=True))
        _acc(dg_ref, dg, i)

    return pl.pallas_call(
        body, name=name, grid=(S // tm,),
        in_specs=[_rows(tm, W), _rows(tm, W), _rows(tm, W, P_Z // W), _vec(B_DH), HBM],
        out_specs=[_rows(tm, W), _rows(tm, W, P_Z // W), _vec(B_DH)],
        out_shape=[jax.ShapeDtypeStruct((S, W), F32), jax.ShapeDtypeStruct(dproj.shape, BF16),
                   jax.ShapeDtypeStruct((1, B_DH), F32)],
        input_output_aliases={4: 1}, compiler_params=_arb(1),
    )(dy, o, proj, ng, dproj)


def _ada_mod(c_all, w_ada, b_shard, name):
    L, D, Ns = w_ada.shape
    B = c_all.shape[0]

    def body(c_ref, w_ref, b_ref, o_ref):
        cv = c_ref[...]
        cond = (cv * _sigmoid(cv)).astype(BF16)
        o_ref[...] = _nn(cond, w_ref[...].astype(BF16)) + b_ref[...]

    return pl.pallas_call(
        body, name=name, grid=(L,),
        in_specs=[pl.BlockSpec((B, D), lambda l: (0, 0)), pl.BlockSpec((None, D, Ns), lambda l: (l, 0, 0)),
                  pl.BlockSpec((None, 1, Ns), lambda l: (l, 0, 0))],
        out_specs=pl.BlockSpec((None, B, Ns), lambda l: (l, 0, 0)),
        out_shape=jax.ShapeDtypeStruct((L, B, Ns), F32), compiler_params=_par(1),
    )(c_all, w_ada, b_shard)


def _ada_wgrad(c_all, dmod, name):
    L, B, Ns = dmod.shape
    D = c_all.shape[1]

    def body(c_ref, d_ref, o_ref):
        cv = c_ref[...]
        cond = (cv * _sigmoid(cv)).astype(BF16)
        o_ref[...] = _tn(cond, d_ref[...].astype(BF16))

    return pl.pallas_call(
        body, name=name, grid=(L,),
        in_specs=[pl.BlockSpec((B, D), lambda l: (0, 0)), pl.BlockSpec((None, B, Ns), lambda l: (l, 0, 0))],
        out_specs=pl.BlockSpec((None, D, Ns), lambda l: (l, 0, 0)),
        out_shape=jax.ShapeDtypeStruct((L, D, Ns), F32), compiler_params=_par(1),
    )(c_all, dmod)


W_IN_PIECES = ((0, 0, 1410), (1, 0, 1410), (2, 0, 252), (2, 772, 638), (3, 0, 1410), (2, 252, 512), (2, 764, 8))


def _reorder_w_in(w4, name):
    L, _, D, Cs = w4.shape
    tm = _pick(D, 256, 16)
    used = sum(p[2] for p in W_IN_PIECES)

    def body(w_ref, o_ref):
        shard = [w_ref[s] for s in range(4)]
        parts = [shard[s][:, lo:lo + n] for s, lo, n in W_IN_PIECES]
        o_ref[...] = jnp.concatenate(parts + [jnp.zeros((tm, P_END - used), w4.dtype)], axis=1)

    return pl.pallas_call(
        body, name=name, grid=(L, D // tm), in_specs=[pl.BlockSpec((None, 4, tm, Cs), lambda l, i: (l, 0, i, 0))],
        out_specs=pl.BlockSpec((None, tm, P_END), lambda l, i: (l, i, 0)),
        out_shape=jax.ShapeDtypeStruct((L, D, P_END), w4.dtype), compiler_params=_par(2),
    )(w4)


def _restore_w_in(g, name):
    D = g.shape[0]
    tm = _pick(D, 256, 16)

    def body(g_ref, o_ref, ob_ref):
        gv = g_ref[...]
        off = 0
        pieces = {}
        for s, lo, n in W_IN_PIECES:
            pieces.setdefault(s, []).append((lo, gv[:, off:off + n]))
            off += n
        for s, lst in pieces.items():
            lst.sort(key=lambda t: t[0])
            shard = lst[0][1] if len(lst) == 1 else jnp.concatenate([t[1] for t in lst], axis=1)
            o_ref[s] = shard
            ob_ref[s] = shard.astype(BF16)

    spec = pl.BlockSpec((4, tm, W_IN_SHARD), lambda i: (0, i, 0))
    return pl.pallas_call(
        body, name=name, grid=(D // tm,), in_specs=[pl.BlockSpec((tm, P_END), lambda i: (i, 0))],
        out_specs=[spec, spec],
        out_shape=[jax.ShapeDtypeStruct((4, D, W_IN_SHARD), g.dtype), jax.ShapeDtypeStruct((4, D, W_IN_SHARD), BF16)],
        compiler_params=_par(1),
    )(g)


def _adam_update(w, g, m, v):
    mn = ADAM_B1 * m + (1.0 - ADAM_B1) * g
    vn = ADAM_B2 * v + (1.0 - ADAM_B2) * (g * g)
    m_hat = mn / (1.0 - ADAM_B1 ** ADAM_STEP)
    v_hat = vn / (1.0 - ADAM_B2 ** ADAM_STEP)
    return -ADAM_LR * (m_hat / (jnp.sqrt(v_hat) + ADAM_EPS) + ADAM_WD * w), mn, vn


def _adamw(w, g, m, v, name):
    shape = w.shape
    C = shape[-1]
    R = w.size // C
    tm = _pick(R, 512, 8)
    spec = pl.BlockSpec((tm, C), lambda i: (i, 0))

    def body(w_ref, g_ref, m_ref, v_ref, d_ref, mo_ref, vo_ref):
        d_ref[...], mo_ref[...], vo_ref[...] = _adam_update(w_ref[...], g_ref[...], m_ref[...], v_ref[...])

    outs = pl.pallas_call(
        body, name=name, grid=(R // tm,), in_specs=[spec] * 4, out_specs=[spec] * 3,
        out_shape=[jax.ShapeDtypeStruct((R, C), F32)] * 3, compiler_params=_par(1),
    )(*(t.reshape(R, C) for t in (w, g, m, v)))
    return tuple(o.reshape(shape) for o in outs)


def _adamw_lead(w, g, m, v, name, tl):
    A, B, C = w.shape
    spec = pl.BlockSpec((tl, B, C), lambda i: (i, 0, 0))

    def body(w_ref, g_ref, m_ref, v_ref, d_ref, mo_ref, vo_ref):
        d_ref[...], mo_ref[...], vo_ref[...] = _adam_update(w_ref[...], g_ref[...], m_ref[...], v_ref[...])

    return pl.pallas_call(
        body, name=name, grid=(A // tl,), in_specs=[spec] * 4, out_specs=[spec] * 3,
        out_shape=[jax.ShapeDtypeStruct((A, B, C), F32)] * 3, compiler_params=_par(1),
    )(w, g, m, v)


def _adamw_layers(w, gs, m, v, name):
    L, R, C = w.shape
    tm = _pick(R, 256, 8)
    spec = pl.BlockSpec((None, tm, C), lambda l, i: (l, i, 0))
    g_specs = [pl.BlockSpec((tm, C), functools.partial(lambda ll, l, i: (jnp.where(l == ll, i, 0), 0), ll))
               for ll in range(L)]

    def body(w_ref, m_ref, v_ref, *rest):
        g_refs, (go_ref, d_ref, mo_ref, vo_ref) = rest[:L], rest[L:]
        l = pl.program_id(0)
        for ll in range(L):
            @pl.when(l == ll)
            def _():
                g = g_refs[ll][...]
                go_ref[...] = g
                d_ref[...], mo_ref[...], vo_ref[...] = _adam_update(w_ref[...], g, m_ref[...], v_ref[...])

    return pl.pallas_call(
        body, name=name, grid=(L, R // tm), in_specs=[spec] * 3 + g_specs, out_specs=[spec] * 4,
        out_shape=[jax.ShapeDtypeStruct((L, R, C), F32)] * 4, compiler_params=_arb(2),
    )(w, m, v, *gs)


def _pair_sums(a, where, b, name):
    NB, _, R, C = a.shape

    def body(where_ref, a_ref, b_ref, p_ref, own_ref):
        s = a_ref[...] + b_ref[...].astype(F32)
        p_ref[...] = s.astype(BF16)

        @pl.when(pl.program_id(0) == where_ref[1])
        def _():
            own_ref[...] = s

    return pl.pallas_call(
        body, name=name,
        grid_spec=pltpu.PrefetchScalarGridSpec(
            num_scalar_prefetch=1, grid=(NB,),
            in_specs=[pl.BlockSpec((None, None, R, C), lambda k, w: (k, w[0], 0, 0)),
                      pl.BlockSpec((None, R, C), lambda k, w: (k, 0, 0))],
            out_specs=[pl.BlockSpec((None, R, C), lambda k, w: (k, 0, 0)), pl.BlockSpec((R, C), lambda k, w: (0, 0))]),
        out_shape=[jax.ShapeDtypeStruct((NB, R, C), BF16), jax.ShapeDtypeStruct((R, C), F32)],
        compiler_params=_arb(1),
    )(where, a, b)


def _sum_own_and_received(own, recv, where, name):
    R, C = own.shape
    tm = _pick(R, 256, 16)

    def body(where_ref, p_ref, r_ref, o_ref):
        o_ref[...] = ((p_ref[...] + r_ref[0].astype(F32)) + r_ref[1].astype(F32)) + r_ref[2].astype(F32)

    return pl.pallas_call(
        body, name=name,
        grid_spec=pltpu.PrefetchScalarGridSpec(
            num_scalar_prefetch=1, grid=(R // tm,),
            in_specs=[pl.BlockSpec((tm, C), lambda i, w: (i, 0)), pl.BlockSpec((3, tm, C), lambda i, w: (0, i, 0))],
            out_specs=pl.BlockSpec((None, tm, C), lambda i, w: (w[0], i, 0))),
        out_shape=jax.ShapeDtypeStruct((2, R, C), F32), compiler_params=_par(1),
    )(where, own, recv)


def _position():
    return lax.axis_index("x"), lax.axis_index("y"), lax.axis_index("c")


def _other_chips(x, y):
    return [(1 - x, y), (x, 1 - y), (1 - x, 1 - y)]


HBM = pl.BlockSpec(memory_space=pl.ANY)


def _allgather8(blk, name, reduce_rows=None):
    M, N = blk.shape

    def body(x_ref, out_ref, *rest):
        if reduce_rows is None:
            send_sems, recv_sems, local_sem = rest
        else:
            sum_ref, send_sems, recv_sems, local_sem = rest
        x, y, c = _position()
        me, sibling = (x, y, c), (x, y, 1 - c)
        chips = _other_chips(x, y)

        def rows(px, py, pc):
            return out_ref.at[pl.ds((4 * px + 2 * py + pc) * M, M), :]

        def copy(k, block, to, src=None):
            return pltpu.make_async_remote_copy(
                src_ref=rows(*block) if src is None else src, dst_ref=rows(*block),
                send_sem=send_sems.at[k], recv_sem=recv_sems.at[k], device_id=to, device_id_type=MESH)

        mine = pltpu.make_async_copy(x_ref, rows(*me), local_sem)
        mine.start()
        first = [copy(0, me, sibling, src=x_ref)]
        first += [copy(1 + j, me, (*chip, c), src=x_ref) for j, chip in enumerate(chips)]
        for cp in first:
            cp.start()
        passed = [copy(4 + j, (*chip, c), sibling) for j, chip in enumerate(chips)]
        for j, chip in enumerate(chips):
            copy(1 + j, (*chip, c), me).wait_recv()
            passed[j].start()
        copy(0, sibling, me).wait_recv()
        for j, chip in enumerate(chips):
            copy(4 + j, (*chip, 1 - c), me).wait_recv()
        for cp in first + passed:
            cp.wait_send()
        mine.wait()
        if reduce_rows is not None:
            tot = out_ref[pl.ds(0, reduce_rows), :]
            for d in range(1, 8):
                tot = tot + out_ref[pl.ds(d * M, reduce_rows), :]
            sum_ref[...] = tot

    vmem = pl.BlockSpec(memory_space=pltpu.VMEM)
    out_shape = [jax.ShapeDtypeStruct((8 * M, N), blk.dtype)]
    if reduce_rows is not None:
        out_shape.append(jax.ShapeDtypeStruct((reduce_rows, N), blk.dtype))
    res = pl.pallas_call(
        body, name=name, out_shape=out_shape, in_specs=[vmem], out_specs=[vmem] * len(out_shape),
        scratch_shapes=[pltpu.SemaphoreType.DMA((7,)), pltpu.SemaphoreType.DMA((7,)), pltpu.SemaphoreType.DMA],
    )(blk)
    return res[0] if reduce_rows is None else (res[0], res[1])


def _gather_phases(layer, srcs, outs, send_sems, recv_sems, local_sems):
    n = len(srcs)
    x, y, c = _position()
    me, sibling = (x, y, c), (x, y, 1 - c)
    chips = _other_chips(x, y)

    def region(t, px, py, pc):
        return outs[t].at[2 * px + py, pc]

    def copy(t, k, block, to, own=False):
        return pltpu.make_async_remote_copy(
            src_ref=srcs[t].at[layer, c] if own else region(t, *block), dst_ref=region(t, *block),
            send_sem=send_sems.at[7 * t + k], recv_sem=recv_sems.at[7 * t + k], device_id=to, device_id_type=MESH)

    def local(t):
        return pltpu.make_async_copy(srcs[t].at[layer, c], region(t, *me), local_sems.at[t])

    def first(t):
        return [copy(t, 0, me, sibling, own=True)] + [copy(t, 1 + j, me, (*chip, c), own=True)
                                                       for j, chip in enumerate(chips)]

    def start():
        for t in range(n):
            local(t).start()
        for t in range(n):
            for cp in first(t):
                cp.start()

    def forward():
        for j, chip in enumerate(chips):
            for t in range(n):
                copy(t, 1 + j, (*chip, c), me).wait_recv()
                copy(t, 4 + j, (*chip, c), sibling).start()

    def finish():
        for t in range(n):
            copy(t, 0, sibling, me).wait_recv()
        for j, chip in enumerate(chips):
            for t in range(n):
                copy(t, 4 + j, (*chip, 1 - c), me).wait_recv()
        for t in range(n):
            for cp in first(t) + [copy(t, 4 + j, (*chip, c), sibling) for j, chip in enumerate(chips)]:
                cp.wait_send()
            local(t).wait()

    return start, forward, finish


def _gather_scratch(n):
    return [pltpu.SemaphoreType.DMA((7 * n,)), pltpu.SemaphoreType.DMA((7 * n,)), pltpu.SemaphoreType.DMA((n,))]


def _gather_out_shapes(shards):
    return [jax.ShapeDtypeStruct((4,) + s.shape[1:], s.dtype) for s in shards]


def _gather_weights(shards, layer, name):
    n = len(shards)

    def body(*refs):
        start, forward, finish = _gather_phases(layer, refs[:n], refs[n:2 * n], *refs[2 * n:])
        start()
        forward()
        finish()

    return pl.pallas_call(
        body, name=name, out_shape=_gather_out_shapes(shards), in_specs=[HBM] * n, out_specs=[HBM] * n,
        scratch_shapes=_gather_scratch(n),
    )(*shards)


def _sibling_phases(srcs, outs, send_sems, recv_sems):
    x, y, c = _position()
    copies = [pltpu.make_async_remote_copy(
        src_ref=srcs[t].at[k, 1 - c], dst_ref=outs[t].at[k], send_sem=send_sems.at[4 * t + k],
        recv_sem=recv_sems.at[4 * t + k], device_id=(x, y, 1 - c), device_id_type=MESH)
        for t in range(len(srcs)) for k in range(4)]

    def start():
        for cp in copies:
            cp.start()

    def finish():
        for cp in copies:
            cp.wait()

    return start, finish


def _sibling_scratch(n):
    return [pltpu.SemaphoreType.DMA((4 * n,)), pltpu.SemaphoreType.DMA((4 * n,))]


def _sibling_out_shapes(gs):
    return [jax.ShapeDtypeStruct((4,) + g.shape[2:], g.dtype) for g in gs]


def _rs_chips(ps, name):
    n = len(ps)

    def body(*refs):
        start, finish = _chips_phases(refs[:n], refs[n:2 * n], *refs[2 * n:])
        start()
        finish()

    return pl.pallas_call(
        body, name=name, out_shape=_chips_out_shapes(ps), in_specs=[HBM] * n, out_specs=[HBM] * n,
        scratch_shapes=_chips_scratch(n),
    )(*ps)


def _chips_phases(srcs, outs, send_sems, recv_sems):
    x, y, c = _position()
    copies = [pltpu.make_async_remote_copy(
        src_ref=srcs[t].at[2 * px + py], dst_ref=outs[t].at[j], send_sem=send_sems.at[3 * t + j],
        recv_sem=recv_sems.at[3 * t + j], device_id=(px, py, c), device_id_type=MESH)
        for t in range(len(srcs)) for j, (px, py) in enumerate(_other_chips(x, y))]

    def start():
        for cp in copies:
            cp.start()

    def finish():
        for cp in copies:
            cp.wait()

    return start, finish


def _chips_scratch(n):
    return [pltpu.SemaphoreType.DMA((3 * n,)), pltpu.SemaphoreType.DMA((3 * n,))]


def _chips_out_shapes(ps):
    return [jax.ShapeDtypeStruct((3,) + p.shape[1:], p.dtype) for p in ps]


def _rs_pair(hs, name):
    n = len(hs)

    def body(*refs):
        bufs = refs[n:2 * n]
        send_sems, recv_sems = refs[2 * n:]
        x, y, c = _position()

        def copy(t, half):
            return pltpu.make_async_remote_copy(
                src_ref=bufs[t].at[half], dst_ref=bufs[t].at[half], send_sem=send_sems.at[t], recv_sem=recv_sems.at[t],
                device_id=(x, y, 1 - c), device_id_type=MESH)

        for t in range(n):
            copy(t, c).start()
        for t in range(n):
            copy(t, 1 - c).wait_recv()
        for t in range(n):
            copy(t, c).wait_send()

    out_shape = [jax.ShapeDtypeStruct(h.shape, h.dtype) for h in hs]
    return pl.pallas_call(
        body, name=name, out_shape=out_shape, in_specs=[HBM] * n, out_specs=[HBM] * n,
        input_output_aliases={t: t for t in range(n)},
        scratch_shapes=[pltpu.SemaphoreType.DMA((n,)), pltpu.SemaphoreType.DMA((n,))],
    )(*hs)


BIG = ("w_in", "w_branch_a", "w_branch_b", "w_out", "w_ffn_in", "w_ffn_out")
CARRY_ATTN = ["w_in"]
CARRY_LOCAL = ["w_ffn_out"]
CARRY_SCAN = ["w_branch_a", "w_branch_b", "w_out"]
CARRY_GU = ["w_ffn_in"]
CARRY_DATTN = ["w_in", "w_ffn_in"]
CARRY_DLOCAL = ["w_branch_a", "w_branch_b", "w_out", "w_ffn_out"]


def _band_bias(rel_table, name, gather=None):
    L, H, n = rel_table.shape
    tab = jnp.pad(rel_table, ((0, 0), (0, 0), (0, NREL_PAD - n))).reshape(L * H, 1, NREL_PAD)
    band = (A_PAST + 1) * CHUNK

    shards, glayer = gather if gather is not None else ((), None)
    ng = len(shards)

    def body(t_ref, *rest):
        srcs, o_ref, gouts, sems = rest[:ng], rest[ng], rest[ng + 1:2 * ng + 1], rest[2 * ng + 1:]
        i = pl.program_id(0)
        done = (_carry(_gather_phases(glayer, srcs, gouts, *sems), i == 0, i == LATE * (L * H) // 8, i == L * H - 1)
                if ng else None)
        r = lax.broadcasted_iota(jnp.int32, (NREL_PAD, SKEW_W), 0)
        xi = lax.broadcasted_iota(jnp.int32, (NREL_PAD, SKEW_W), 1)
        diag = jnp.where(xi < KSPAN, xi, xi - SKEW_W)
        rel = jnp.clip(A_PAST * CHUNK - diag, -A_MAX_REL, A_MAX_REL) + A_MAX_REL
        e = _nn(t_ref[...], jnp.where(rel == r, 1.0, 0.0).astype(F32), HI)
        x = jnp.broadcast_to(e, (QBLK, SKEW_W))
        row = lax.broadcasted_iota(jnp.int32, (QBLK, SKEW_W), 0)
        for b in range(QBLK.bit_length() - 1):
            x = jnp.where(((row >> b) & 1) == 1, pltpu.roll(x, 1 << b, 1), x)
        x = x[:, :KSPAN]
        first = (lax.broadcasted_iota(jnp.int32, (QBLK, KSPAN), 0) // CHUNK) * CHUNK
        col = lax.broadcasted_iota(jnp.int32, (QBLK, KSPAN), 1)
        o_ref[...] = jnp.where((col >= first) & (col < first + band), x, NEG)
        if ng:
            done()

    res = pl.pallas_call(
        body, name=name, grid=(L * H,), in_specs=[pl.BlockSpec((None, 1, NREL_PAD), lambda i: (i, 0, 0))] + [HBM] * ng,
        out_specs=[pl.BlockSpec((None, QBLK, KSPAN), lambda i: (i, 0, 0))] + [HBM] * ng,
        out_shape=[jax.ShapeDtypeStruct((L * H, QBLK, KSPAN), F32)] + _gather_out_shapes(shards),
        scratch_shapes=_gather_scratch(ng) if ng else [], compiler_params=_arb(1),
    )(tab, *shards)
    return res[0].reshape(L, H, QBLK, KSPAN), list(res[1:])


def _col_row_forms(t, S):
    nc = S // CHUNK
    return t.T.reshape(nc, CHUNK, B_HEADS), t.reshape(B_HEADS, nc, CHUNK).transpose(1, 0, 2)


def _weight_view(name, gathered, tag):
    if name in ("w_out", "w_ffn_out"):
        return gathered.reshape(8 * gathered.shape[2], gathered.shape[3])
    stacked = gathered.reshape(4, 2 * gathered.shape[2], gathered.shape[3])
    return _reorder_w_in(stacked[None], f"w_in_cols_{tag}")[0] if name == "w_in" else stacked


def _layer_fwd(l, x, mod, W, P, big, gather=None, late=None):
    S, D = x.shape
    n = lambda s: f"{s}_l{l}"
    sh1, sc1, gt1, sh2, sc2, gt2 = (mod[i:i + 1] for i in range(6))
    h1 = _lnmod_fwd(x, P["norm1_g"][l:l + 1], sc1, sh1, n("ln1"))
    if late is None:
        proj = _matmul(h1, W["w_in"], "nn", F32, n("proj"), tn=1152)
    else:
        proj, got = _matmul(h1, W["w_in"], "nn", F32, n("proj"), tn=1152, gather=(late[1], l))
        W = {**W, **{k: _weight_view(k, t, f"l{l}") for k, t in zip(late[0], got)}}
    part = (lambda names: ([gather[0][BIG.index(k)] for k in names], gather[1])) if gather is not None else (lambda names: None)
    ya, got_a = _attn_fwd(proj, big, n("attn"), part(CARRY_ATTN))
    alog, dtb = P["a_log"][l].reshape(B_HEADS, 1), P["dt_bias"][l].reshape(B_HEADS, 1)
    beta, gam, b_t, a_t = _gdn_gates_fwd(proj, alog, dtb, n("gates"))
    bcol, _ = _col_row_forms(beta, S)
    gcol, grow = _col_row_forms(gam, S)
    qn, kn, v = _gdn_pre_fwd(proj, P["w_conv"][l], n("gdnpre"))
    tsave, amat, u, w, got_l = _gdn_local_fwd(qn, kn, v, bcol, gcol, grow, n("gdnlocal"), part(CARRY_LOCAL))
    o, ssave, got_s = _gdn_scan_fwd(qn, kn, u, w, amat, gcol, grow, n("gdnscan"), part(CARRY_SCAN))
    yb = _gdn_post_fwd(o, proj, P["gdn_norm_g"][l:l + 1], n("gdnpost"))
    pa = _matmul(ya, W["w_branch_a"], "nn", BF16, n("pa"), tm=2048, stacked=True)
    pb = _matmul(yb, W["w_branch_b"], "nn", BF16, n("pb"), tm=2048, stacked=True)
    merged = _merge_fwd(proj, pa, pb, n("merge"))
    ao = _matmul(merged, W["w_out"], "nn", F32, n("ao"))
    x1 = _gate_fwd(x, ao, gt1, n("res1"))
    h2 = _lnmod_fwd(x1, P["norm2_g"][l:l + 1], sc2, sh2, n("ln2"))
    gu = _matmul(h2, W["w_ffn_in"], "nn", BF16, n("gu"), stacked=True, gather=part(CARRY_GU))
    gu, got_g = gu if gather is not None else (gu, [])
    got = dict(zip(CARRY_ATTN + CARRY_LOCAL + CARRY_SCAN + CARRY_GU, got_a + got_l + got_s + got_g))
    gathered = {k: got[k] for k in BIG} if gather is not None else None
    act = _ffn_act_fwd(gu, n("act"))
    fo = _matmul(act, W["w_ffn_out"], "nn", F32, n("fo"), tk=1408)
    x2 = _gate_fwd(x1, fo, gt2, n("res2"))
    saved = dict(x=x, h1=h1, proj=proj, ya=ya, b_t=b_t, a_t=a_t, bcol=bcol, gcol=gcol, grow=grow,
                 qn=qn, kn=kn, v=v, o=o, tsave=tsave, ssave=ssave, amat=amat, u=u, w=w, yb=yb, pa=pa, pb=pb,
                 merged=merged, ao=ao, x1=x1,
                 h2=h2, gu=gu, act=act, fo=fo)
    return x2, saved, gathered, W


def _layer_bwd(l, dx2, sv, mod, W, P, big, exchange=()):
    S, D = dx2.shape
    n = lambda s: f"{s}_l{l}"
    sh1, sc1, gt1, sh2, sc2, gt2 = (mod[i:i + 1] for i in range(6))
    g, pay = {}, {}
    view = lambda t: t.reshape((4, 2, t.shape[-2] // (2 if t.ndim == 3 else 8), t.shape[-1]))
    dz2, dgt2 = _gate_bwd(dx2, sv["fo"], gt2, n("dres2"))
    g["w_ffn_out"], pay["w_ffn_out"] = map(view, _matmul(sv["act"], dz2, "tn", F32, n("dwfo"), tm=1408, also_bf16=True))
    dact = _matmul(dz2, W["w_ffn_out"], "nt", BF16, n("dact"), tn=1408)
    dgu = _ffn_act_bwd(sv["gu"], dact, n("dgu"))
    g["w_ffn_in"], pay["w_ffn_in"] = map(view, _matmul(sv["h2"], dgu, "tn", F32, n("dwfi"), out_stacked=True,
                                                       also_bf16=True))
    dh2 = _matmul(dgu, W["w_ffn_in"], "nt", F32, n("dh2"), stacked=True)
    dx1, dsh2, dsc2, dn2 = _lnmod_bwd(dh2, sv["x1"], P["norm2_g"][l:l + 1], sc2, dx2, n("dln2"))
    dz1, dgt1 = _gate_bwd(dx1, sv["ao"], gt1, n("dres1"))
    g["w_out"], pay["w_out"] = map(view, _matmul(sv["merged"], dz1, "tn", F32, n("dwo"), also_bf16=True))
    dmerged = _matmul(dz1, W["w_out"], "nt", BF16, n("dmerged"))
    dproj, dpa, dpb = _merge_bwd(sv["proj"], sv["pa"], sv["pb"], dmerged, n("dmerge"))
    g["w_branch_a"], pay["w_branch_a"] = map(view, _matmul(sv["ya"], dpa, "tn", F32, n("dwa"), out_stacked=True,
                                                           also_bf16=True))
    g["w_branch_b"], pay["w_branch_b"] = map(view, _matmul(sv["yb"], dpb, "tn", F32, n("dwb"), out_stacked=True,
                                                           also_bf16=True))
    dya = _matmul(dpa, W["w_branch_a"], "nt", BF16, n("dya"), tm=2048, stacked=True)
    dyb = _matmul(dpb, W["w_branch_b"], "nt", F32, n("dyb"), tm=2048, stacked=True)
    ex = (lambda names: [exchange[BIG.index(k)] for k in names]) if len(exchange) else (lambda names: ())
    dq, dk, dv, dbig, rec_a = _attn_bwd(sv["proj"], big, dya, n("dattn"), ex(CARRY_DATTN))
    g["rel_table"] = _rel_table_grad(dbig, n("drel"))[:, 0, :2 * A_MAX_REL + 1]
    dproj = _write_columns(dproj, [dq, dk, dv], 3 * dq.shape[1], P_QKVA // (3 * dq.shape[1]), n("dqkva"))
    do, dproj, dng = _gdn_post_bwd(dyb, sv["o"], sv["proj"], P["gdn_norm_g"][l:l + 1], dproj, n("dgdnpost"))
    g["gdn_norm_g"] = dng[0]
    du, dw, dqd, dkd, da, dgl = _gdn_scan_bwd(sv["qn"], sv["kn"], sv["u"], sv["w"], sv["amat"], sv["gcol"], sv["grow"],
                                              sv["ssave"], do, n("dgdnscan"))
    dqn, dkn, dvv, dbc, dgc, dgr, rec_l = _gdn_local_bwd(
        sv["qn"], sv["kn"], sv["v"], sv["bcol"], sv["gcol"], sv["grow"], sv["tsave"], du, dw, dqd, dkd, da, dgl,
        n("dgdnlocal"), ex(CARRY_DLOCAL))
    rec = dict(zip(CARRY_DATTN + CARRY_DLOCAL, rec_a + rec_l))
    received = [rec[k] for k in BIG] if len(exchange) else None
    dbeta_t = dbc.reshape(S, B_HEADS).T
    dgam_a = dgc.reshape(S, B_HEADS).T
    dgam_b = dgr.transpose(1, 0, 2).reshape(B_HEADS, S)
    alog, dtb = P["a_log"][l].reshape(B_HEADS, 1), P["dt_bias"][l].reshape(B_HEADS, 1)
    db_t, da_t, dal, ddt = _gdn_gates_bwd(dbeta_t, dgam_a, dgam_b, sv["b_t"], sv["a_t"], alog, dtb, n("dgates"))
    g["a_log"], g["dt_bias"] = dal[:, 0], ddt[:, 0]
    dy = _gdn_pre_bwd_a(sv["proj"], P["w_conv"][l], dqn, dkn, dvv, n("dgdnpre_a"))
    dproj, g["w_conv"] = _gdn_pre_bwd_b(sv["proj"], P["w_conv"][l], dy, dproj, n("dgdnpre_b"))
    dba = jnp.concatenate([db_t.T, da_t.T, jnp.zeros((S, P_END - P_BA - 2 * B_HEADS), F32)], axis=1)
    dproj = _write_columns(dproj, [dba], dba.shape[1], P_BA // dba.shape[1], n("dba"))
    g["w_in"], pay["w_in"] = map(view, _restore_w_in(_matmul(sv["h1"], dproj, "tn", F32, n("dwin"), tn=1152),
                                                     n("dwin_cols")))
    dh1, from_sibling = _matmul(dproj, W["w_in"], "nt", F32, n("dh1"), tk=1152, sibling=[pay[k] for k in BIG])
    dx, dsh1, dsc1, dn1 = _lnmod_bwd(dh1, sv["x"], P["norm1_g"][l:l + 1], sc1, dx1, n("dln1"))
    g["norm1_g"], g["norm2_g"] = dn1[0], dn2[0]
    dmod = jnp.concatenate([dsh1, dsc1, dgt1, dsh2, dsc2, dgt2], axis=1)[0]
    return dx, g, from_sibling, dmod, received


SMALL = ("norm1_g", "norm2_g", "rel_table", "w_conv", "a_log", "dt_bias", "gdn_norm_g")
SMALL_PACK_C = 1024


def _as_rows(t):
    flat = t.reshape(-1)
    rows = -(-flat.shape[0] // SMALL_PACK_C)
    return jnp.pad(flat, (0, rows * SMALL_PACK_C - flat.shape[0])).reshape(rows, SMALL_PACK_C)


def _pack_rows(parts):
    blk = jnp.concatenate([_as_rows(p) for p in parts], axis=0)
    return jnp.pad(blk, ((0, -blk.shape[0] % 8), (0, 0)))


def _unpack_rows(blk, shapes):
    out, r = [], 0
    for shp in shapes:
        size = int(np.prod(shp))
        rows = -(-size // SMALL_PACK_C)
        out.append(blk[..., r:r + rows, :].reshape(blk.shape[:-2] + (rows * SMALL_PACK_C,))[..., :size]
                   .reshape(blk.shape[:-2] + tuple(shp)))
        r += rows
    return out


def kernel(x, c, w_ada, b_ada, norm1_g, norm2_g, w_in, rel_table, w_conv, a_log, dt_bias, gdn_norm_g, w_branch_a, w_branch_b, w_out, w_ffn_in, w_ffn_out, final_g, loss_target, m_w_ada, m_b_ada, m_norm1_g, m_norm2_g, m_w_in, m_rel_table, m_w_conv, m_a_log, m_dt_bias, m_gdn_norm_g, m_w_branch_a, m_w_branch_b, m_w_out, m_w_ffn_in, m_w_ffn_out, m_final_g, v_w_ada, v_b_ada, v_norm1_g, v_norm2_g, v_w_in, v_rel_table, v_w_conv, v_a_log, v_dt_bias, v_gdn_norm_g, v_w_branch_a, v_w_branch_b, v_w_out, v_w_ffn_in, v_w_ffn_out, v_final_g):
    weights = dict(w_ada=w_ada, b_ada=b_ada, norm1_g=norm1_g, norm2_g=norm2_g, w_in=w_in, rel_table=rel_table,
                   w_conv=w_conv, a_log=a_log, dt_bias=dt_bias, gdn_norm_g=gdn_norm_g, w_branch_a=w_branch_a,
                   w_branch_b=w_branch_b, w_out=w_out, w_ffn_in=w_ffn_in, w_ffn_out=w_ffn_out, final_g=final_g)
    mom_m = dict(w_ada=m_w_ada, b_ada=m_b_ada, norm1_g=m_norm1_g, norm2_g=m_norm2_g, w_in=m_w_in,
                 rel_table=m_rel_table, w_conv=m_w_conv, a_log=m_a_log, dt_bias=m_dt_bias, gdn_norm_g=m_gdn_norm_g,
                 w_branch_a=m_w_branch_a, w_branch_b=m_w_branch_b, w_out=m_w_out, w_ffn_in=m_w_ffn_in,
                 w_ffn_out=m_w_ffn_out, final_g=m_final_g)
    mom_v = dict(w_ada=v_w_ada, b_ada=v_b_ada, norm1_g=v_norm1_g, norm2_g=v_norm2_g, w_in=v_w_in,
                 rel_table=v_rel_table, w_conv=v_w_conv, a_log=v_a_log, dt_bias=v_dt_bias, gdn_norm_g=v_gdn_norm_g,
                 w_branch_a=v_w_branch_a, w_branch_b=v_w_branch_b, w_out=v_w_out, w_ffn_in=v_w_ffn_in,
                 w_ffn_out=v_w_ffn_out, final_g=v_final_g)
    xi, yi, ci = _position()
    chip = 2 * xi + yi
    dev = 2 * chip + ci
    L, D = norm1_g.shape
    NMOD = b_ada.shape[1] // D
    ns = w_ada.shape[2]
    cs = w_conv.shape[2]

    first_blk = _pack_rows([c, w_conv])
    first_all = _allgather8(first_blk, "gather_c").reshape(8, first_blk.shape[0], SMALL_PACK_C)
    c_all, w_conv_all = _unpack_rows(first_all, [(D,), w_conv.shape])
    w_conv_full = w_conv_all.reshape(4, 2, L, CONV_K, cs)[:, 0].transpose(1, 2, 0, 3).reshape(L, CONV_K, 4 * cs)
    b_shard = lax.dynamic_slice_in_dim(b_ada, chip * ns, ns, axis=1).reshape(L, 1, ns)
    mod_shard = _ada_mod(c_all, w_ada, b_shard, "ada_mod")
    mod_all = _allgather8(mod_shard.reshape(L * 8, ns), "gather_mod").reshape(4, 2, L, 8, ns)
    mod = lax.dynamic_index_in_dim(mod_all[:, 0], dev, axis=2, keepdims=False)
    mod = mod.transpose(1, 0, 2).reshape(L, NMOD, D)

    shards = [weights[k].astype(BF16) for k in BIG]
    shards = [s.reshape(s.shape[0], 2, s.shape[1] // 2, s.shape[2]) for s in shards]
    P = dict(norm1_g=norm1_g, norm2_g=norm2_g, w_conv=w_conv_full, a_log=a_log, dt_bias=dt_bias,
             gdn_norm_g=gdn_norm_g)
    shard_of = dict(zip(BIG, shards))

    big, got = _band_bias(rel_table, "band_bias", ([shard_of["w_in"]], 0))
    alone = ["w_branch_a", "w_branch_b", "w_out", "w_ffn_out"]
    got += _gather_weights([shard_of[k] for k in alone], 0, "gather_weights_l0")
    W = [{k: _weight_view(k, t, "l0") for k, t in zip(["w_in"] + alone, got)}]
    late = (["w_ffn_in"], [shard_of["w_ffn_in"]])
    xc = x[0]
    saved = []
    for l in range(L):
        xc, sv, gathered, W[l] = _layer_fwd(l, xc, mod[l], W[l], P, big[l], (shards, l + 1) if l + 1 < L else None,
                                           late if l == 0 else None)
        saved.append(sv)
        if l + 1 < L:
            W.append({k: _weight_view(k, gathered[k], f"l{l + 1}") for k in BIG})
    dx, loss_dev, dfinal = _loss_head(xc, final_g.reshape(1, D), loss_target[0], "loss_head")

    where = jnp.stack([ci, chip]).astype(jnp.int32)
    grads = [None] * L
    dmods = [None] * L
    shard_grads = {k: [None] * L for k in BIG}

    def finish_reduce_scatter(l, sums, from_chips):
        halves = [_sum_own_and_received(s_[1], r_, where, f"rs_sum_{k}_l{l}")
                  for k, s_, r_ in zip(BIG, sums, from_chips)]
        for k, t in zip(BIG, _rs_pair(halves, f"rs_pair_l{l}")):
            shard_grads[k][l] = t.reshape(2 * t.shape[1], t.shape[2])

    pending = None
    for l in reversed(range(L)):
        exchange = [s_[0] for s_ in pending] if pending is not None else ()
        dx, grads[l], from_sibling, dmods[l], received = _layer_bwd(l, dx, saved[l], mod[l], W[l], P, big[l], exchange)
        if pending is not None:
            finish_reduce_scatter(l + 1, pending, received)
        gs = [grads[l][k] for k in BIG]
        pending = [_pair_sums(g_, where, r_, f"rs_pair_sum_{k}_l{l}") for k, g_, r_ in zip(BIG, gs, from_sibling)]
    finish_reduce_scatter(0, pending, _rs_chips([s_[0] for s_ in pending], "rs_chips_l0"))
    dmod = jnp.stack(dmods)

    small = {k: jnp.stack([grads[l][k] for l in range(L)]) for k in SMALL}
    parts = [dmod] + [small[k] for k in SMALL] + [dfinal, loss_dev[0, :1]]
    small_blk = _pack_rows(parts)
    srows = small_blk.shape[0]
    small_all, small_sum = _allgather8(small_blk, "gather_small", reduce_rows=srows)
    shapes = [dmod.shape] + [small[k].shape for k in SMALL] + [(D,), (1,)]
    tot = _unpack_rows(small_sum, shapes)
    G = dict(zip(SMALL, tot[1:1 + len(SMALL)]))
    G["b_ada"] = tot[0].reshape(b_ada.shape)
    G["w_conv"] = lax.dynamic_slice_in_dim(G["w_conv"], chip * cs, cs, axis=2)
    G["final_g"] = tot[-2]
    loss = tot[-1][0]
    dmod_all = _unpack_rows(small_all.reshape(8, srows, SMALL_PACK_C), [dmod.shape])[0]
    dmod_cols = lax.dynamic_slice_in_dim(dmod_all, chip * ns, ns, axis=2).transpose(1, 0, 2)
    G["w_ada"] = _ada_wgrad(c_all, dmod_cols, "ada_wgrad")

    order = ["w_ada", "b_ada", "norm1_g", "norm2_g", "w_in", "rel_table", "w_conv", "a_log", "dt_bias", "gdn_norm_g",
             "w_branch_a", "w_branch_b", "w_out", "w_ffn_in", "w_ffn_out", "final_g"]
    deltas, new_m, new_v = {}, {}, {}
    for k in order:
        w = weights[k]
        if k == "w_in":
            to_cols = lambda t: jnp.transpose(t, (2, 0, 1))
            from_cols = lambda t: jnp.transpose(t, (1, 2, 0))
            gt = to_cols(jnp.stack(shard_grads[k]))
            d_, m_, v_ = _adamw_lead(to_cols(w), gt, to_cols(mom_m[k]), to_cols(mom_v[k]), f"adamw_{k}",
                                     W_IN_SHARD // 30)
            G[k], deltas[k], new_m[k], new_v[k] = from_cols(gt), from_cols(d_), from_cols(m_), from_cols(v_)
            continue
        if k in BIG:
            G[k], deltas[k], new_m[k], new_v[k] = _adamw_layers(w, shard_grads[k], mom_m[k], mom_v[k], f"adamw_{k}")
            continue
        as2d = (lambda t: t.reshape(1, -1)) if w.ndim == 1 else (lambda t: t)
        d_, m_, v_ = _adamw(as2d(w), as2d(G[k]), as2d(mom_m[k]), as2d(mom_v[k]), f"adamw_{k}")
        deltas[k], new_m[k], new_v[k] = d_.reshape(w.shape), m_.reshape(w.shape), v_.reshape(w.shape)
    return (loss, dx[None], *[G[k] for k in order], *[deltas[k] for k in order], *[new_m[k] for k in order],
            *[new_v[k] for k in order])
```

```python
import functools

import numpy as np
import jax
import jax.numpy as jnp
from jax import lax
from jax.experimental import pallas as pl
from jax.experimental.pallas import tpu as pltpu

F32 = jnp.float32
BF16 = jnp.bfloat16
HI = lax.Precision.HIGHEST
SOLVE_PREC = lax.Precision.HIGH
MESH = pl.DeviceIdType.MESH

EPS = 1e-6
CHUNK = 64
A_HEADS = 8
A_DH = 64
A_PAST = 8
A_MAX_REL = 128
B_HEADS = 4
B_DH = 128
CONV_K = 4
LANE = 128
QBLK = 4 * CHUNK
KSPAN = QBLK + A_PAST * CHUNK
NEG = -1e30

ADAM_LR = 0.001
ADAM_B1 = 0.9
ADAM_B2 = 0.999
ADAM_EPS = 1e-08
ADAM_WD = 0.01
ADAM_STEP = 10

P_QKVA, P_QKVB, P_GA, P_GB, P_Z, P_BA, P_END = 0, 1536, 3072, 4096, 5120, 5632, 5760
W_IN_SHARD = 1410


def _sigmoid(x):
    return 1.0 / (1.0 + jnp.exp(-x))


def _dot(a, b, ca, cb, prec):
    lead = a.ndim - 2
    batch = ((0,), (0,)) if lead else ((), ())
    return lax.dot_general(a, b, (((ca + lead,), (cb + lead,)), batch), precision=prec, preferred_element_type=F32)


def _nn(a, b, prec=None):
    return _dot(a, b, 1, 0, prec)


def _nt(a, b, prec=None):
    return _dot(a, b, 1, 1, prec)


def _tn(a, b, prec=None):
    return _dot(a, b, 0, 0, prec)


def _bnn(a, b):
    return _nn(a.astype(BF16), b.astype(BF16))


def _bnt(a, b):
    return _nt(a.astype(BF16), b.astype(BF16))


def _btn(a, b):
    return _tn(a.astype(BF16), b.astype(BF16))


def _pick(n, target, unit=LANE):
    best = None
    for t in range(unit, min(n, target) + 1, unit):
        if n % t == 0:
            best = t
    return best if best is not None else n


def _acc(ref, val, i):
    @pl.when(i == 0)
    def _():
        ref[...] = val

    @pl.when(i != 0)
    def _():
        ref[...] += val


def _arb(n):
    return pltpu.CompilerParams(dimension_semantics=("arbitrary",) * n)


def _par(n):
    return pltpu.CompilerParams(dimension_semantics=("parallel",) * n)


def _matmul(a, b, mode, out_dtype, name, tm=1024, tn=1024, tk=1024, layer=None, stacked=False, out_stacked=False,
            also_bf16=False, gather=None, sibling=None):
    bs = b.shape[1:] if layer is not None else b.shape
    if mode == "nn":
        M, K = a.shape
        N = 4 * bs[2] if stacked else bs[1]
        if stacked:
            tn = bs[2]
    elif mode == "nt":
        M, K = a.shape
        N = bs[1] if stacked else bs[0]
        if stacked:
            tk = bs[2]
    else:
        K, M = a.shape
        N = bs[1]
        if out_stacked:
            tn = N // 4
    tm, tn, tk = _pick(M, tm), _pick(N, tn), _pick(K, tk)
    nk = K // tk
    lead = () if layer is None else (layer,)
    lead_blk = () if layer is None else (None,)
    if mode == "nn":
        a_spec = pl.BlockSpec((tm, tk), lambda i, j, k: (i, k))
        if stacked:
            b_spec = pl.BlockSpec(lead_blk + (None, tk, tn), lambda i, j, k: lead + (j, k, 0))
        else:
            b_spec = pl.BlockSpec(lead_blk + (tk, tn), lambda i, j, k: lead + (k, j))
        dot = _nn
    elif mode == "nt":
        a_spec = pl.BlockSpec((tm, tk), lambda i, j, k: (i, k))
        if stacked:
            b_spec = pl.BlockSpec(lead_blk + (None, tn, tk), lambda i, j, k: lead + (k, j, 0))
        else:
            b_spec = pl.BlockSpec(lead_blk + (tn, tk), lambda i, j, k: lead + (j, k))
        dot = _nt
    else:
        a_spec = pl.BlockSpec((tk, tm), lambda i, j, k: (k, i))
        b_spec = pl.BlockSpec((tk, tn), lambda i, j, k: (k, j))
        dot = _tn
    if out_stacked:
        o_spec = pl.BlockSpec((None, tm, tn), lambda i, j, k: (j, i, 0))
        o_shape = jax.ShapeDtypeStruct((4, M, tn), out_dtype)
    else:
        o_spec = pl.BlockSpec((tm, tn), lambda i, j, k: (i, j))
        o_shape = jax.ShapeDtypeStruct((M, N), out_dtype)

    if gather is not None:
        shards, glayer = gather
        carried_shapes, carried_scratch = _gather_out_shapes(shards), _gather_scratch(len(shards))
    elif sibling is not None:
        shards = sibling
        carried_shapes, carried_scratch = _sibling_out_shapes(shards), _sibling_scratch(len(shards))
    else:
        shards, carried_shapes, carried_scratch = (), [], []
    ng = len(shards)
    o_shapes = [o_shape] + ([jax.ShapeDtypeStruct(o_shape.shape, BF16)] if also_bf16 else [])
    no = len(o_shapes)
    gi, gj = M // tm, N // tn

    def write(o_refs, val):
        for o_ref in o_refs:
            o_ref[...] = val.astype(o_ref.dtype)

    def body(a_ref, b_ref, *refs):
        srcs, o_refs, gouts, scratch = refs[:ng], refs[ng:ng + no], refs[ng + no:2 * ng + no], refs[2 * ng + no:]
        i, j, k = pl.program_id(0), pl.program_id(1), pl.program_id(2)
        if ng:
            start = (j == 0) & (k == 0)
            sems = scratch[len(scratch) - len(carried_scratch):]
            phases = (_gather_phases(glayer, srcs, gouts, *sems) if gather is not None
                      else _sibling_phases(srcs, gouts, *sems))
            done = _carry(phases, (i == 0) & start, (i == gi - 1) & start,
                          (i == gi - 1) & (j == gj - 1) & (k == nk - 1))
        if nk == 1:
            write(o_refs, dot(a_ref[...], b_ref[...]))
        else:
            acc_ref = scratch[0]

            @pl.when(k == 0)
            def _():
                acc_ref[...] = jnp.zeros_like(acc_ref)

            acc_ref[...] += dot(a_ref[...], b_ref[...])

            @pl.when(k == nk - 1)
            def _():
                write(o_refs, acc_ref[...])
        if ng:
            done()

    sem = ("arbitrary",) * 3 if ng else ("parallel", "parallel", "arbitrary")
    res = pl.pallas_call(
        body, name=name, grid=(gi, gj, nk), in_specs=[a_spec, b_spec] + [HBM] * ng,
        out_specs=[o_spec] * no + [HBM] * ng, out_shape=o_shapes + carried_shapes,
        scratch_shapes=([] if nk == 1 else [pltpu.VMEM((tm, tn), F32)]) + carried_scratch,
        compiler_params=pltpu.CompilerParams(dimension_semantics=sem),
    )(a, b, *shards)
    out = tuple(res[:no]) if also_bf16 else res[0]
    return (out, list(res[no:])) if ng else out


def _rows(tm, n, col=0):
    return pl.BlockSpec((tm, n), lambda i: (i, col))


def _vec(n):
    return pl.BlockSpec((1, n), lambda i: (0, 0))


def _lnmod_fwd(x, g, sc, sh, name):
    S, D = x.shape
    tm = _pick(S, 512, 8)

    def body(x_ref, g_ref, sc_ref, sh_ref, o_ref):
        xv = x_ref[...]
        r = lax.rsqrt(jnp.mean(xv * xv, axis=-1, keepdims=True) + EPS)
        o_ref[...] = ((xv * r * g_ref[...]) * (1.0 + sc_ref[...]) + sh_ref[...]).astype(BF16)

    return pl.pallas_call(
        body, name=name, grid=(S // tm,),
        in_specs=[_rows(tm, D), _vec(D), _vec(D), _vec(D)], out_specs=_rows(tm, D),
        out_shape=jax.ShapeDtypeStruct((S, D), BF16), compiler_params=_par(1),
    )(x, g, sc, sh)


def _lnmod_bwd(dh, x, g, sc, dres, name):
    S, D = x.shape
    tm = _pick(S, 512, 8)

    def body(dh_ref, x_ref, g_ref, sc_ref, dres_ref, dx_ref, dsh_ref, dsc_ref, dg_ref):
        i = pl.program_id(0)
        xv = x_ref[...]
        dh_ = dh_ref[...]
        r = lax.rsqrt(jnp.mean(xv * xv, axis=-1, keepdims=True) + EPS)
        xhat = xv * r
        gv = g_ref[...]
        dn = dh_ * (1.0 + sc_ref[...])
        dxhat = dn * gv
        dx_ref[...] = dres_ref[...] + r * (dxhat - xhat * jnp.mean(dxhat * xhat, axis=-1, keepdims=True))
        _acc(dsh_ref, jnp.sum(dh_, axis=0, keepdims=True), i)
        _acc(dsc_ref, jnp.sum(dh_ * (xhat * gv), axis=0, keepdims=True), i)
        _acc(dg_ref, jnp.sum(dn * xhat, axis=0, keepdims=True), i)

    return pl.pallas_call(
        body, name=name, grid=(S // tm,),
        in_specs=[_rows(tm, D), _rows(tm, D), _vec(D), _vec(D), _rows(tm, D)],
        out_specs=[_rows(tm, D), _vec(D), _vec(D), _vec(D)],
        out_shape=[jax.ShapeDtypeStruct((S, D), F32)] + [jax.ShapeDtypeStruct((1, D), F32)] * 3,
        compiler_params=_arb(1),
    )(dh, x, g, sc, dres)


def _gate_fwd(x, y, gt, name):
    S, D = x.shape
    tm = _pick(S, 512, 8)

    def body(x_ref, y_ref, gt_ref, o_ref):
        o_ref[...] = x_ref[...] + gt_ref[...] * y_ref[...]

    return pl.pallas_call(
        body, name=name, grid=(S // tm,), in_specs=[_rows(tm, D), _rows(tm, D), _vec(D)], out_specs=_rows(tm, D),
        out_shape=jax.ShapeDtypeStruct((S, D), F32), compiler_params=_par(1),
    )(x, y, gt)


def _gate_bwd(dx, y, gt, name):
    S, D = dx.shape
    tm = _pick(S, 512, 8)

    def body(dx_ref, y_ref, gt_ref, dz_ref, dgt_ref):
        i = pl.program_id(0)
        d = dx_ref[...]
        dz_ref[...] = (d * gt_ref[...]).astype(BF16)
        _acc(dgt_ref, jnp.sum(d * y_ref[...], axis=0, keepdims=True), i)

    return pl.pallas_call(
        body, name=name, grid=(S // tm,), in_specs=[_rows(tm, D), _rows(tm, D), _vec(D)],
        out_specs=[_rows(tm, D), _vec(D)],
        out_shape=[jax.ShapeDtypeStruct((S, D), BF16), jax.ShapeDtypeStruct((1, D), F32)],
        compiler_params=_arb(1),
    )(dx, y, gt)


def _ffn_act_fwd(gu, name):
    S, H2 = gu.shape
    H = H2 // 2
    tm = _pick(S, 256, 8)

    def body(g_ref, u_ref, o_ref):
        gv = g_ref[...].astype(F32)
        o_ref[...] = (gv * _sigmoid(gv) * u_ref[...].astype(F32)).astype(BF16)

    return pl.pallas_call(
        body, name=name, grid=(S // tm,), in_specs=[_rows(tm, H, 0), _rows(tm, H, 1)], out_specs=_rows(tm, H),
        out_shape=jax.ShapeDtypeStruct((S, H), BF16), compiler_params=_par(1),
    )(gu, gu)


def _ffn_act_bwd(gu, dact, name):
    S, H2 = gu.shape
    H = H2 // 2
    tm = _pick(S, 256, 8)

    def body(g_ref, u_ref, da_ref, o_ref):
        gv = g_ref[...].astype(F32)
        s = _sigmoid(gv)
        da = da_ref[...].astype(F32)
        o_ref[:, :H] = (da * u_ref[...].astype(F32) * (s * (1.0 + gv * (1.0 - s)))).astype(BF16)
        o_ref[:, H:] = (da * (gv * s)).astype(BF16)

    return pl.pallas_call(
        body, name=name, grid=(S // tm,), in_specs=[_rows(tm, H, 0), _rows(tm, H, 1), _rows(tm, H)],
        out_specs=_rows(tm, H2), out_shape=jax.ShapeDtypeStruct((S, H2), BF16), compiler_params=_par(1),
    )(gu, gu, dact)


def _merge_fwd(proj, pa, pb, name):
    S, D = pa.shape
    tm = _pick(S, 512, 8)

    def body(ga_ref, gb_ref, pa_ref, pb_ref, o_ref):
        o_ref[...] = (_sigmoid(ga_ref[...]) * pa_ref[...].astype(F32)
                      + _sigmoid(gb_ref[...]) * pb_ref[...].astype(F32)).astype(BF16)

    return pl.pallas_call(
        body, name=name, grid=(S // tm,),
        in_specs=[_rows(tm, D, P_GA // D), _rows(tm, D, P_GB // D), _rows(tm, D), _rows(tm, D)],
        out_specs=_rows(tm, D), out_shape=jax.ShapeDtypeStruct((S, D), BF16), compiler_params=_par(1),
    )(proj, proj, pa, pb)


def _merge_bwd(proj, pa, pb, dm, name):
    S, D = pa.shape
    tm = _pick(S, 512, 16)
    rows_j = pl.BlockSpec((tm, D), lambda i, j: (i, 0))

    def body(g_ref, pa_ref, pb_ref, dm_ref, dg_ref, dpa_ref, dpb_ref):
        d = dm_ref[...].astype(F32)
        s = _sigmoid(g_ref[...])
        for branch, p_ref, dp_ref in ((0, pa_ref, dpa_ref), (1, pb_ref, dpb_ref)):
            @pl.when(pl.program_id(1) == branch)
            def _():
                dg_ref[...] = (d * p_ref[...].astype(F32) * s * (1.0 - s)).astype(BF16)
                dp_ref[...] = (d * s).astype(BF16)

    return pl.pallas_call(
        body, name=name, grid=(S // tm, 2),
        in_specs=[pl.BlockSpec((tm, D), lambda i, j: (i, P_GA // D + j)), rows_j, rows_j, rows_j],
        out_specs=[pl.BlockSpec((tm, D), lambda i, j: (i, P_GA // D + j)), rows_j, rows_j],
        out_shape=[jax.ShapeDtypeStruct((S, P_END), BF16), jax.ShapeDtypeStruct((S, D), BF16),
                   jax.ShapeDtypeStruct((S, D), BF16)],
        compiler_params=_arb(2),
    )(proj, pa, pb, dm)


def _write_columns(buf, parts, width, colblk, name):
    S = buf.shape[0]
    tm = _pick(S, 512, 16)
    n = len(parts)

    def body(*refs):
        o_ref = refs[n + 1]
        off = 0
        for p_ref in refs[:n]:
            w = p_ref.shape[1]
            o_ref[:, off:off + w] = p_ref[...].astype(BF16)
            off += w

    return pl.pallas_call(
        body, name=name, grid=(S // tm,), in_specs=[_rows(tm, p.shape[1]) for p in parts] + [HBM],
        out_specs=_rows(tm, width, colblk), out_shape=jax.ShapeDtypeStruct(buf.shape, buf.dtype),
        input_output_aliases={n: 0}, compiler_params=_par(1),
    )(*parts, buf)


def _loss_head(x, g, target, name):
    S, D = x.shape
    tm = _pick(S, 512, 8)

    def body(x_ref, g_ref, t_ref, dx_ref, loss_ref, dg_ref):
        i = pl.program_id(0)
        xv = x_ref[...]
        gv = g_ref[...]
        r = lax.rsqrt(jnp.mean(xv * xv, axis=-1, keepdims=True) + EPS)
        xhat = xv * r
        err = xhat * gv - t_ref[...]
        part = 0.5 * jnp.sum(jnp.mean(err * err, axis=-1, keepdims=True), axis=0, keepdims=True)
        _acc(loss_ref, jnp.broadcast_to(part, (1, LANE)), i)
        dy = err * (1.0 / D)
        _acc(dg_ref, jnp.sum(dy * xhat, axis=0, keepdims=True), i)
        dxhat = dy * gv
        dx_ref[...] = r * (dxhat - xhat * jnp.mean(dxhat * xhat, axis=-1, keepdims=True))

    return pl.pallas_call(
        body, name=name, grid=(S // tm,), in_specs=[_rows(tm, D), _vec(D), _rows(tm, D)],
        out_specs=[_rows(tm, D), _vec(LANE), _vec(D)],
        out_shape=[jax.ShapeDtypeStruct((S, D), F32), jax.ShapeDtypeStruct((1, LANE), F32),
                   jax.ShapeDtypeStruct((1, D), F32)],
        compiler_params=_arb(1),
    )(x, g, target)


HEADS_PER_SLAB = LANE // A_DH
N_SLABS = A_HEADS // HEADS_PER_SLAB
SPAN_BLOCKS = KSPAN // QBLK


def _attn_specs(seg):
    q_spec = pl.BlockSpec((QBLK, LANE), lambda p, m: (m, seg[0] * N_SLABS + p))
    k_specs = [pl.BlockSpec((QBLK, LANE), functools.partial(
        lambda j, p, m: (jnp.maximum(m - (SPAN_BLOCKS - 1) + j, 0), seg[1] * N_SLABS + p), j)) for j in range(SPAN_BLOCKS)]
    v_specs = [pl.BlockSpec((QBLK, LANE), functools.partial(
        lambda j, p, m: (jnp.maximum(m - (SPAN_BLOCKS - 1) + j, 0), seg[2] * N_SLABS + p), j)) for j in range(SPAN_BLOCKS)]
    b_spec = pl.BlockSpec((HEADS_PER_SLAB, QBLK, KSPAN), lambda p, m: (p, 0, 0))
    return q_spec, k_specs, v_specs, b_spec


def _head_lanes(t, hh):
    lane = lax.broadcasted_iota(jnp.int32, t.shape, 1)
    return jnp.where((lane // A_DH) == hh, t, jnp.zeros_like(t))


def _front_mask(m):
    col = lax.broadcasted_iota(jnp.int32, (QBLK, KSPAN), 1)
    return jnp.where(col < (SPAN_BLOCKS - 1 - m) * QBLK, NEG, 0.0)


def _attn_probs(qk, bias, front):
    s = qk * (A_DH ** -0.5) + (bias + front)
    p = jnp.exp(s - jnp.max(s, axis=-1, keepdims=True))
    return p * (1.0 / jnp.sum(p, axis=-1, keepdims=True))


def _grid_ends(nq):
    p, m = pl.program_id(0), pl.program_id(1)
    return (p == 0) & (m == 0), (p == N_SLABS - 1) & (m == nq // 2), (p == N_SLABS - 1) & (m == nq - 1)


def _attn_fwd(proj, big, name, gather=None):
    S = proj.shape[0]
    q_spec, k_specs, v_specs, b_spec = _attn_specs((0, 1, 2))
    shards, layer = gather if gather is not None else ((), None)
    ng = len(shards)

    def body(q_ref, k0, k1, k2, v0, v1, v2, b_ref, *rest):
        srcs, o_ref, gouts, sems = rest[:ng], rest[ng], rest[ng + 1:2 * ng + 1], rest[2 * ng + 1:]
        done = _carry(_gather_phases(layer, srcs, gouts, *sems), *_grid_ends(S // QBLK)) if ng else None
        m = pl.program_id(1)
        q = q_ref[...].astype(BF16)
        k = jnp.concatenate([k0[...], k1[...], k2[...]], axis=0).astype(BF16)
        v = jnp.concatenate([v0[...], v1[...], v2[...]], axis=0).astype(BF16)
        front = _front_mask(m)
        heads = range(HEADS_PER_SLAB)
        scores = [_nt(_head_lanes(q, hh), k) for hh in heads]
        probs = [_attn_probs(scores[hh], b_ref[hh], front).astype(BF16) for hh in heads]
        outs = [_nn(probs[hh], v) for hh in heads]
        lane = lax.broadcasted_iota(jnp.int32, (QBLK, LANE), 1)
        o_ref[...] = jnp.where(lane < A_DH, outs[0], outs[1]).astype(BF16)
        if ng:
            done()

    res = pl.pallas_call(
        body, name=name, grid=(N_SLABS, S // QBLK), in_specs=[q_spec] + k_specs + v_specs + [b_spec] + [HBM] * ng,
        out_specs=[pl.BlockSpec((QBLK, LANE), lambda p, m: (m, p))] + [HBM] * ng,
        out_shape=[jax.ShapeDtypeStruct((S, A_HEADS * A_DH), BF16)] + _gather_out_shapes(shards),
        scratch_shapes=_gather_scratch(ng) if ng else [], compiler_params=_arb(2),
    )(proj, proj, proj, proj, proj, proj, proj, big, *shards)
    return res[0], list(res[1:])


def _attn_bwd(proj, big, dya, name, exchange=()):
    S = proj.shape[0]
    W = A_HEADS * A_DH
    q_spec, k_specs, v_specs, b_spec = _attn_specs((0, 1, 2))
    out_q = pl.BlockSpec((QBLK, LANE), lambda p, m: (m, p))
    out_kv = pl.BlockSpec((S, LANE), lambda p, m: (0, p))
    ne = len(exchange)

    def body(q_ref, k0, k1, k2, v0, v1, v2, b_ref, do_ref, *rest):
        srcs, (dq_ref, dk_ref, dv_ref, db_ref), eouts, sems = rest[:ne], rest[ne:ne + 4], rest[ne + 4:2 * ne + 4], rest[2 * ne + 4:]
        done = _carry(_chips_phases(srcs, eouts, *sems), *_grid_ends(S // QBLK)) if ne else None
        m = pl.program_id(1)

        @pl.when(m == 0)
        def _():
            dk_ref[...] = jnp.zeros_like(dk_ref)
            dv_ref[...] = jnp.zeros_like(dv_ref)
            db_ref[...] = jnp.zeros_like(db_ref)

        q = q_ref[...].astype(BF16)
        k = jnp.concatenate([k0[...], k1[...], k2[...]], axis=0).astype(BF16)
        v = jnp.concatenate([v0[...], v1[...], v2[...]], axis=0).astype(BF16)
        do = do_ref[...]
        front = _front_mask(m)
        heads = range(HEADS_PER_SLAB)
        qh = [_head_lanes(q, hh) for hh in heads]
        doh = [_head_lanes(do, hh) for hh in heads]
        scores = [_nt(qh[hh], k) for hh in heads]
        dps = [_nt(doh[hh], v) for hh in heads]
        ps = [_attn_probs(scores[hh], b_ref[hh], front) for hh in heads]
        dss = [ps[hh] * (dps[hh] - jnp.sum(ps[hh] * dps[hh], axis=-1, keepdims=True)) for hh in heads]
        for hh in heads:
            db_ref[hh] += dss[hh]
        dsb = [(dss[hh] * (A_DH ** -0.5)).astype(BF16) for hh in heads]
        dqs = [_nn(dsb[hh], k) for hh in heads]
        dk = sum(_tn(dsb[hh], qh[hh]) for hh in heads)
        dv = sum(_tn(ps[hh].astype(BF16), doh[hh]) for hh in heads)
        lane = lax.broadcasted_iota(jnp.int32, (QBLK, LANE), 1)
        dq_ref[...] = jnp.where(lane < A_DH, dqs[0], dqs[1])
        for j in range(SPAN_BLOCKS):
            blk = m - (SPAN_BLOCKS - 1) + j

            @pl.when(blk >= 0)
            def _():
                off = pl.multiple_of(blk * QBLK, QBLK)
                dk_ref[pl.ds(off, QBLK), :] += dk[j * QBLK:(j + 1) * QBLK]
                dv_ref[pl.ds(off, QBLK), :] += dv[j * QBLK:(j + 1) * QBLK]
        if ne:
            done()

    res = pl.pallas_call(
        body, name=name, grid=(N_SLABS, S // QBLK),
        in_specs=[q_spec] + k_specs + v_specs + [b_spec, pl.BlockSpec((QBLK, LANE), lambda p, m: (m, p))] + [HBM] * ne,
        out_specs=[out_q, out_kv, out_kv, b_spec] + [HBM] * ne,
        out_shape=[jax.ShapeDtypeStruct((S, W), F32)] * 3 + [jax.ShapeDtypeStruct((A_HEADS, QBLK, KSPAN), F32)]
        + _chips_out_shapes(exchange),
        scratch_shapes=_chips_scratch(ne) if ne else [], compiler_params=_arb(2),
    )(proj, proj, proj, proj, proj, proj, proj, big, dya, *exchange)
    return tuple(res[:4]) + (list(res[4:]),)


NREL_PAD = 3 * LANE
SKEW_W = 1024


def _rel_table_grad(dbig, name):
    H, R, C = dbig.shape

    def body(d_ref, o_ref):
        x = jnp.concatenate([d_ref[...], jnp.zeros((R, SKEW_W - C), F32)], axis=1)
        row = lax.broadcasted_iota(jnp.int32, (R, SKEW_W), 0)
        for b in range(R.bit_length() - 1):
            x = jnp.where(((row >> b) & 1) == 1, pltpu.roll(x, SKEW_W - (1 << b), 1), x)
        e = jnp.sum(x, axis=0, keepdims=True)
        xi = lax.broadcasted_iota(jnp.int32, (SKEW_W, NREL_PAD), 0)
        r = lax.broadcasted_iota(jnp.int32, (SKEW_W, NREL_PAD), 1)
        diag = jnp.where(xi < C, xi, xi - SKEW_W)
        rel = jnp.clip(A_PAST * CHUNK - diag, -A_MAX_REL, A_MAX_REL) + A_MAX_REL
        o_ref[...] = _nn(e, jnp.where(rel == r, 1.0, 0.0).astype(F32), HI)

    return pl.pallas_call(
        body, name=name, grid=(H,), in_specs=[pl.BlockSpec((None, R, C), lambda h: (h, 0, 0))],
        out_specs=pl.BlockSpec((None, 1, NREL_PAD), lambda h: (h, 0, 0)),
        out_shape=jax.ShapeDtypeStruct((H, 1, NREL_PAD), F32), compiler_params=_par(1),
    )(dbig)


def _chunk_cumsum_matrix(n, reverse):
    j = lax.broadcasted_iota(jnp.int32, (n, n), 0)
    i = lax.broadcasted_iota(jnp.int32, (n, n), 1)
    same = (j // CHUNK) == (i // CHUNK)
    return jnp.where(same & ((j >= i) if reverse else (j <= i)), 1.0, 0.0).astype(F32)


def _gdn_gates_fwd(proj, alog, dtb, name):
    Hh, S = B_HEADS, proj.shape[0]
    tl = _pick(S, 512)
    row = pl.BlockSpec((Hh, tl), lambda i: (0, i))
    col = pl.BlockSpec((Hh, 1), lambda i: (0, 0))

    def body(ba_ref, al_ref, dt_ref, beta_ref, gam_ref, b_ref, a_ref, t_ref):
        t_ref[...] = ba_ref[...].T
        b = t_ref[0:Hh, :]
        a = t_ref[Hh:2 * Hh, :]
        z = a + dt_ref[...]
        sp = jnp.maximum(z, 0.0) + jnp.log(1.0 + jnp.exp(-jnp.abs(z)))
        g = -jnp.exp(al_ref[...]) * sp
        beta_ref[...] = _sigmoid(b)
        gam_ref[...] = _nn(g, _chunk_cumsum_matrix(tl, False), HI)
        b_ref[...] = b
        a_ref[...] = a

    return pl.pallas_call(
        body, name=name, grid=(S // tl,), in_specs=[pl.BlockSpec((tl, LANE), lambda i: (i, P_BA // LANE)), col, col],
        out_specs=[row] * 4, out_shape=[jax.ShapeDtypeStruct((Hh, S), F32)] * 4,
        scratch_shapes=[pltpu.VMEM((LANE, tl), F32)], compiler_params=_par(1),
    )(proj, alog, dtb)


def _gdn_gates_bwd(dbeta, dgam_a, dgam_b, b_t, a_t, alog, dtb, name):
    Hh, S = b_t.shape
    tl = _pick(S, 512)
    row = pl.BlockSpec((Hh, tl), lambda i: (0, i))
    col = pl.BlockSpec((Hh, 1), lambda i: (0, 0))
    accs = pl.BlockSpec((Hh, LANE), lambda i: (0, 0))

    def body(dbeta_ref, dga_ref, dgb_ref, b_ref, a_ref, al_ref, dt_ref, db_ref, da_ref, dal_ref, ddt_ref):
        i = pl.program_id(0)
        z = a_ref[...] + dt_ref[...]
        sp = jnp.maximum(z, 0.0) + jnp.log(1.0 + jnp.exp(-jnp.abs(z)))
        ea = jnp.exp(al_ref[...])
        dg = _nn(dga_ref[...] + dgb_ref[...], _chunk_cumsum_matrix(tl, True), HI)
        da = dg * (-ea) * _sigmoid(z)
        beta = _sigmoid(b_ref[...])
        db_ref[...] = dbeta_ref[...] * beta * (1.0 - beta)
        da_ref[...] = da
        _acc(dal_ref, jnp.broadcast_to(jnp.sum(dg * (-ea * sp), axis=1, keepdims=True), (Hh, LANE)), i)
        _acc(ddt_ref, jnp.broadcast_to(jnp.sum(da, axis=1, keepdims=True), (Hh, LANE)), i)

    return pl.pallas_call(
        body, name=name, grid=(S // tl,), in_specs=[row] * 5 + [col, col], out_specs=[row, row, accs, accs],
        out_shape=[jax.ShapeDtypeStruct((Hh, S), F32)] * 2 + [jax.ShapeDtypeStruct((Hh, LANE), F32)] * 2,
        compiler_params=_arb(1),
    )(dbeta, dgam_a, dgam_b, b_t, a_t, alog, dtb)


HALO = 8


def _conv_silu(xx_ref, w_ref, tm):
    y = w_ref[0:1, :] * xx_ref[pl.ds(HALO - CONV_K + 1, tm), :]
    for j in range(1, CONV_K):
        y = y + w_ref[j:j + 1, :] * xx_ref[pl.ds(HALO - CONV_K + 1 + j, tm), :]
    return y, y * _sigmoid(y)


def _fill_prev_halo(xx_ref, x_ref, prev_ref, i, tm):
    xx_ref[pl.ds(HALO, tm), :] = x_ref[...]

    @pl.when(i == 0)
    def _():
        xx_ref[pl.ds(0, HALO), :] = jnp.zeros((HALO, xx_ref.shape[1]), F32)

    @pl.when(i != 0)
    def _():
        xx_ref[pl.ds(0, HALO), :] = prev_ref[...]


def _gdn_pre_specs(tm, C, colblk):
    cur = pl.BlockSpec((tm, C), lambda i: (i, colblk))
    prev = pl.BlockSpec((HALO, C), lambda i: (jnp.maximum(i * (tm // HALO) - 1, 0), colblk))
    return cur, prev


def _gdn_pre_fwd(proj, wconv, name):
    S = proj.shape[0]
    C = 3 * B_HEADS * B_DH
    W = B_HEADS * B_DH
    tm = _pick(S, 256, 8)
    cur, prev = _gdn_pre_specs(tm, C, P_QKVB // C)

    def body(x_ref, prev_ref, w_ref, q_ref, k_ref, v_ref, xx_ref):
        i = pl.program_id(0)
        _fill_prev_halo(xx_ref, x_ref, prev_ref, i, tm)
        _, sl = _conv_silu(xx_ref, w_ref, tm)
        for h in range(B_HEADS):
            hs = slice(h * B_DH, (h + 1) * B_DH)
            q = sl[:, h * B_DH:(h + 1) * B_DH]
            k = sl[:, W + h * B_DH:W + (h + 1) * B_DH]
            q_ref[:, hs] = q * (lax.rsqrt(jnp.sum(q * q, axis=-1, keepdims=True) + EPS) * (B_DH ** -0.5))
            k_ref[:, hs] = k * lax.rsqrt(jnp.sum(k * k, axis=-1, keepdims=True) + EPS)
        v_ref[...] = sl[:, 2 * W:]

    return pl.pallas_call(
        body, name=name, grid=(S // tm,), in_specs=[cur, prev, pl.BlockSpec((CONV_K, C), lambda i: (0, 0))],
        out_specs=[_rows(tm, W)] * 3, out_shape=[jax.ShapeDtypeStruct((S, W), F32)] * 3,
        scratch_shapes=[pltpu.VMEM((HALO + tm, C), F32)], compiler_params=_par(1),
    )(proj, proj, wconv)


def _conv_rows(xx_ref, w_ref, start, n):
    base = HALO - CONV_K + 1 + start
    y = w_ref[0:1, :] * xx_ref[pl.ds(base, n), :]
    for j in range(1, CONV_K):
        y = y + w_ref[j:j + 1, :] * xx_ref[pl.ds(base + j, n), :]
    return y


def _pre_dy(y, dq, dk, dv):
    W = B_HEADS * B_DH
    sg = _sigmoid(y)
    sl = y * sg
    dsilu = sg * (1.0 + y * (1.0 - sg))
    parts = []
    for base, d_all, c in ((0, dq, B_DH ** -0.5), (W, dk, 1.0)):
        for h in range(B_HEADS):
            lo = base + h * B_DH
            t = sl[:, lo:lo + B_DH]
            d = d_all[:, h * B_DH:(h + 1) * B_DH]
            r = lax.rsqrt(jnp.sum(t * t, axis=-1, keepdims=True) + EPS)
            parts.append((c * r) * (d - t * (r * r) * jnp.sum(d * t, axis=-1, keepdims=True)) * dsilu[:, lo:lo + B_DH])
    parts.append(dv * dsilu[:, 2 * W:])
    return jnp.concatenate(parts, axis=1)


def _gdn_pre_bwd(proj, wconv, dqn, dkn, dv, dproj, name):
    S = proj.shape[0]
    C = 3 * B_HEADS * B_DH
    W = B_HEADS * B_DH
    tm = _pick(S, 256, 16)
    nt_ = S // tm
    cur, prev = _gdn_pre_specs(tm, C, P_QKVB // C)
    after = lambda i: jnp.minimum((i + 1) * (tm // HALO), S // HALO - 1)
    next_x = pl.BlockSpec((HALO, C), lambda i: (after(i), P_QKVB // C))
    next_d = pl.BlockSpec((HALO, W), lambda i: (after(i), 0))

    def body(x_ref, prev_ref, nx_ref, w_ref, dq_ref, dk_ref, dv_ref, ndq_ref, ndk_ref, ndv_ref, _, dx_ref, dw_ref,
             xx_ref, dd_ref):
        i = pl.program_id(0)
        _fill_prev_halo(xx_ref, x_ref, prev_ref, i, tm)
        xx_ref[pl.ds(HALO + tm, HALO), :] = nx_ref[...]
        dyv = _pre_dy(_conv_rows(xx_ref, w_ref, 0, tm), dq_ref[...], dk_ref[...], dv_ref[...])
        dd_ref[pl.ds(0, tm), :] = dyv

        @pl.when(i == nt_ - 1)
        def _():
            dd_ref[pl.ds(tm, HALO), :] = jnp.zeros((HALO, C), F32)

        @pl.when(i != nt_ - 1)
        def _():
            dd_ref[pl.ds(tm, HALO), :] = _pre_dy(_conv_rows(xx_ref, w_ref, tm, HALO), ndq_ref[...], ndk_ref[...],
                                                 ndv_ref[...])

        dx = w_ref[0:1, :] * dd_ref[pl.ds(CONV_K - 1, tm), :]
        for j in range(1, CONV_K):
            dx = dx + w_ref[j:j + 1, :] * dd_ref[pl.ds(CONV_K - 1 - j, tm), :]
        dx_ref[...] = dx.astype(BF16)
        dw = jnp.concatenate(
            [jnp.sum(dyv * xx_ref[pl.ds(HALO - CONV_K + 1 + j, tm), :], axis=0, keepdims=True) for j in range(CONV_K)],
            axis=0)
        _acc(dw_ref, dw, i)

    return pl.pallas_call(
        body, name=name, grid=(nt_,),
        in_specs=[cur, prev, next_x, pl.BlockSpec((CONV_K, C), lambda i: (0, 0))] + [_rows(tm, W)] * 3 + [next_d] * 3
        + [HBM],
        out_specs=[_rows(tm, C, P_QKVB // C), pl.BlockSpec((CONV_K, C), lambda i: (0, 0))],
        out_shape=[jax.ShapeDtypeStruct(dproj.shape, BF16), jax.ShapeDtypeStruct((CONV_K, C), F32)],
        input_output_aliases={10: 0},
        scratch_shapes=[pltpu.VMEM((HALO + tm + HALO, C), F32), pltpu.VMEM((tm + HALO, C), F32)],
        compiler_params=_arb(1),
    )(proj, proj, proj, wconv, dqn, dkn, dv, dqn, dkn, dv, dproj)


def _chunk_masks():
    row = lax.broadcasted_iota(jnp.int32, (CHUNK, CHUNK), 0)
    col = lax.broadcasted_iota(jnp.int32, (CHUNK, CHUNK), 1)
    return row >= col, row > col


def _chunk_local(q, k, vv, bc, gc, gr, tri):
    dm = jnp.where(tri, jnp.exp(jnp.where(tri, gc - gr, 0.0)), 0.0)
    kk = _bnt(k, k)
    glast = gr[..., CHUNK - 1:CHUNK]
    ep = jnp.exp(gc)
    em = jnp.exp(glast - gc)
    el = jnp.exp(glast)
    return dm, kk, ep, em, el, vv * bc, k * (bc * ep)


def _unit_lower_inverse(low):
    row = lax.broadcasted_iota(jnp.int32, (CHUNK, CHUNK), 0)
    col = lax.broadcasted_iota(jnp.int32, (CHUNK, CHUNK), 1)
    p = -low
    t = jnp.where(row == col, 1.0, 0.0).astype(F32) + p
    steps = CHUNK.bit_length() - 2
    for _ in range(steps):
        p = _nn(p, p, SOLVE_PREC)
        t = t + _nn(t, p, SOLVE_PREC)
    return t


GROUP = 4


LATE = 7


def _carry(phases, first, middle, last):
    if len(phases) == 3:
        pl.when(first)(phases[0])
        pl.when(middle)(phases[1])
        return lambda: pl.when(last)(phases[2])
    pl.when(first)(phases[0])
    return lambda: pl.when(last)(phases[1])


def _pairs(nchunks):
    return [(c, h) for c in range(nchunks) for h in range(B_HEADS)]


def _tok(c):
    return slice(c * CHUNK, (c + 1) * CHUNK)


def _head(h):
    return slice(h * B_DH, (h + 1) * B_DH)


def _stack_tokens(ref, nchunks):
    return jnp.stack([ref[_tok(c), _head(h)] for c, h in _pairs(nchunks)])


def _stack_cols(ref, nchunks):
    per_chunk = [ref[c] for c in range(nchunks)] if len(ref.shape) == 3 else [ref[...]]
    return jnp.stack([per_chunk[c][:, h:h + 1] for c, h in _pairs(nchunks)])


def _stack_rows(ref, nchunks):
    if len(ref.shape) == 3:
        return jnp.stack([ref[c, h:h + 1, :] for c, h in _pairs(nchunks)])
    return jnp.stack([ref[h:h + 1, :] for _, h in _pairs(1)])


def _gdn_group_specs(ng_steps, W):
    tok = pl.BlockSpec((GROUP * CHUNK, W), lambda i: (i, 0))
    colv = pl.BlockSpec((GROUP, CHUNK, B_HEADS), lambda i: (i, 0, 0))
    rowv = pl.BlockSpec((GROUP, B_HEADS, CHUNK), lambda i: (i, 0, 0))
    mat = pl.BlockSpec((GROUP, B_HEADS, CHUNK, CHUNK), lambda i: (i, 0, 0, 0))
    return tok, colv, rowv, mat


def _gdn_local_fwd(qn, kn, v, bcol, gcol, grow, name, gather=None):
    S, Wd = qn.shape
    nc = S // CHUNK
    steps = nc // GROUP
    tok, colv, rowv, mat = _gdn_group_specs(steps, Wd)
    shards, layer = gather if gather is not None else ((), None)
    ng = len(shards)

    def body(q_ref, k_ref, v_ref, bc_ref, gc_ref, gr_ref, *rest):
        srcs, (t_ref, a_ref, u_ref, w_ref), gouts, sems = rest[:ng], rest[ng:ng + 4], rest[ng + 4:2 * ng + 4], rest[2 * ng + 4:]
        i = pl.program_id(0)
        done = _carry(_gather_phases(layer, srcs, gouts, *sems), i == 0, i == LATE * steps // 8, i == steps - 1) if ng else None
        tri, strict = _chunk_masks()
        q, k, vv = (_stack_tokens(r, GROUP) for r in (q_ref, k_ref, v_ref))
        bc, gc, gr = _stack_cols(bc_ref, GROUP), _stack_cols(gc_ref, GROUP), _stack_rows(gr_ref, GROUP)
        dm, kk, ep, em, el, vb, kb = _chunk_local(q, k, vv, bc, gc, gr, tri)
        t = _unit_lower_inverse(jnp.where(strict, bc * kk * dm, 0.0))
        a = _bnt(q, k) * dm
        u = _nn(t, vb, SOLVE_PREC)
        w = _nn(t, kb, SOLVE_PREC)
        for n, (c, h) in enumerate(_pairs(GROUP)):
            t_ref[c, h] = t[n]
            a_ref[c, h] = a[n]
            u_ref[_tok(c), _head(h)] = u[n]
            w_ref[_tok(c), _head(h)] = w[n]
        if ng:
            done()

    res = pl.pallas_call(
        body, name=name, grid=(steps,), in_specs=[tok, tok, tok, colv, colv, rowv] + [HBM] * ng,
        out_specs=[mat, mat, tok, tok] + [HBM] * ng,
        out_shape=[jax.ShapeDtypeStruct((nc, B_HEADS, CHUNK, CHUNK), F32)] * 2 + [jax.ShapeDtypeStruct((S, Wd), F32)] * 2
        + _gather_out_shapes(shards),
        scratch_shapes=_gather_scratch(ng) if ng else [], compiler_params=_arb(1),
    )(qn, kn, v, bcol, gcol, grow, *shards)
    return res[0], res[1], res[2], res[3], list(res[4:])


def _scan_decays(gc, gr):
    glast = gr[..., CHUNK - 1:CHUNK]
    return jnp.exp(gc), jnp.exp(glast - gc), jnp.exp(glast)


SCAN = 4


def _gdn_scan_specs(steps, rev):
    idx = (lambda i: steps - 1 - i) if rev else (lambda i: i)
    W = B_HEADS * B_DH
    tok = pl.BlockSpec((SCAN * CHUNK, W), lambda i: (idx(i), 0))
    colv = pl.BlockSpec((SCAN, CHUNK, B_HEADS), lambda i: (idx(i), 0, 0))
    rowv = pl.BlockSpec((SCAN, B_HEADS, CHUNK), lambda i: (idx(i), 0, 0))
    mat = pl.BlockSpec((SCAN, B_HEADS, CHUNK, CHUNK), lambda i: (idx(i), 0, 0, 0))
    smat = pl.BlockSpec((SCAN, B_HEADS, B_DH, B_DH), lambda i: (idx(i), 0, 0, 0))
    return tok, colv, rowv, mat, smat


def _chunk_rows(ref, c):
    return ref.at[pl.ds(c * CHUNK, CHUNK)]


def _gdn_scan_fwd(qn, kn, u, w, a, gcol, grow, name, gather=None):
    S, Wd = qn.shape
    nc = S // CHUNK
    steps = nc // SCAN
    tok, colv, rowv, mat, smat = _gdn_scan_specs(steps, False)
    shards, layer = gather if gather is not None else ((), None)
    ng = len(shards)

    def body(q_ref, k_ref, u_ref, w_ref, a_ref, gc_ref, gr_ref, *rest):
        srcs, (o_ref, sh_ref), gouts = rest[:ng], rest[ng:ng + 2], rest[ng + 2:2 * ng + 2]
        st_ref, sems = rest[2 * ng + 2], rest[2 * ng + 3:]
        i = pl.program_id(0)
        done = (_carry(_gather_phases(layer, srcs, gouts, *sems), i == 0, i == LATE * steps // 8, i == steps - 1)
                if ng else None)

        @pl.when(i == 0)
        def _():
            st_ref[...] = jnp.zeros_like(st_ref)

        for c in range(SCAN):
            ep, em, el = _scan_decays(_stack_cols(gc_ref.at[c], 1), _stack_rows(gr_ref.at[c], 1))
            q, k, u, w = (_stack_tokens(_chunk_rows(r, c), 1) for r in (q_ref, k_ref, u_ref, w_ref))
            s0 = st_ref[...]
            ut = u - _bnn(w, s0)
            o = _bnn(q * ep, s0) + _bnn(a_ref[c], ut)
            st_ref[...] = el * s0 + _btn(k * em, ut)
            sh_ref[c] = s0
            for h in range(B_HEADS):
                o_ref[_tok(c), _head(h)] = o[h]
        if ng:
            done()

    res = pl.pallas_call(
        body, name=name, grid=(steps,), in_specs=[tok, tok, tok, tok, mat, colv, rowv] + [HBM] * ng,
        out_specs=[tok, smat] + [HBM] * ng,
        out_shape=[jax.ShapeDtypeStruct((S, Wd), F32), jax.ShapeDtypeStruct((nc, B_HEADS, B_DH, B_DH), F32)]
        + _gather_out_shapes(shards),
        scratch_shapes=[pltpu.VMEM((B_HEADS, B_DH, B_DH), F32)] + (_gather_scratch(ng) if ng else []),
        compiler_params=_arb(1),
    )(qn, kn, u, w, a, gcol, grow, *shards)
    return res[0], res[1], list(res[2:])


def _gdn_scan_bwd(qn, kn, u, w, a, gcol, grow, ssave, do, name):
    S, Wd = qn.shape
    nc = S // CHUNK
    steps = nc // SCAN
    tok, colv, rowv, mat, smat = _gdn_scan_specs(steps, True)

    def body(q_ref, k_ref, u_ref, w_ref, a_ref, gc_ref, gr_ref, sh_ref, do_ref,
             du_ref, dw_ref, dqd_ref, dkd_ref, da_ref, dgl_ref, ds_ref):
        i = pl.program_id(0)

        @pl.when(i == 0)
        def _():
            ds_ref[...] = jnp.zeros_like(ds_ref)

        tri, _ = _chunk_masks()
        sub4 = lax.broadcasted_iota(jnp.int32, (B_HEADS, CHUNK), 0)
        lane_last = lax.broadcasted_iota(jnp.int32, (1, CHUNK), 1) == CHUNK - 1
        for c in reversed(range(SCAN)):
            ep, em, el = _scan_decays(_stack_cols(gc_ref.at[c], 1), _stack_rows(gr_ref.at[c], 1))
            q, k, u, w, dout = (_stack_tokens(_chunk_rows(r, c), 1) for r in (q_ref, k_ref, u_ref, w_ref, do_ref))
            s0 = sh_ref[c]
            ds = ds_ref[...]
            ut = u - _bnn(w, s0)
            dut = _btn(a_ref[c], dout) + _bnn(k * em, ds)
            ds_ref[...] = el * ds + _btn(q * ep, dout) - _btn(w, dut)
            dw = -_bnt(dut, s0)
            dqd = _bnt(dout, s0)
            dkd = _bnt(ut, ds)
            da_ref[c] = jnp.where(tri, _bnt(dout, ut), 0.0)
            d_el = jnp.sum(jnp.sum(s0 * ds, axis=-1, keepdims=True), axis=-2, keepdims=True)
            last = d_el * el
            dgl_acc = jnp.zeros((B_HEADS, CHUNK), F32)
            for h in range(B_HEADS):
                du_ref[_tok(c), _head(h)] = dut[h]
                dw_ref[_tok(c), _head(h)] = dw[h]
                dqd_ref[_tok(c), _head(h)] = dqd[h]
                dkd_ref[_tok(c), _head(h)] = dkd[h]
                dgl_acc = jnp.where(sub4 == h, jnp.where(lane_last, last[h], 0.0), dgl_acc)
            dgl_ref[c] = dgl_acc

    return pl.pallas_call(
        body, name=name, grid=(steps,), in_specs=[tok, tok, tok, tok, mat, colv, rowv, smat, tok],
        out_specs=[tok, tok, tok, tok, mat, rowv],
        out_shape=[jax.ShapeDtypeStruct((S, Wd), F32)] * 4 + [jax.ShapeDtypeStruct((nc, B_HEADS, CHUNK, CHUNK), F32),
                                                             jax.ShapeDtypeStruct((nc, B_HEADS, CHUNK), F32)],
        scratch_shapes=[pltpu.VMEM((B_HEADS, B_DH, B_DH), F32)], compiler_params=_arb(1),
    )(qn, kn, u, w, a, gcol, grow, ssave, do)


def _gdn_local_bwd(qn, kn, v, bcol, gcol, grow, tsave, du, dw, dqd, dkd, da, dgl, name, exchange=()):
    S, Wd = qn.shape
    nc = S // CHUNK
    steps = nc // GROUP
    tok, colv, rowv, mat = _gdn_group_specs(steps, Wd)
    ne = len(exchange)

    def body(q_ref, k_ref, v_ref, bc_ref, gc_ref, gr_ref, t_ref, du_ref, dw_ref, dqd_ref, dkd_ref, da_ref, dgl_ref, *rest):
        srcs, (dq_ref, dk_ref, dv_ref, dbc_ref, dgc_ref, dgr_ref) = rest[:ne], rest[ne:ne + 6]
        eouts, sems = rest[ne + 6:2 * ne + 6], rest[2 * ne + 6:]
        i = pl.program_id(0)
        done = _carry(_chips_phases(srcs, eouts, *sems), i == 0, None, i == steps - 1) if ne else None
        tri, strict = _chunk_masks()
        lane4 = lax.broadcasted_iota(jnp.int32, (CHUNK, B_HEADS), 1)
        sub4 = lax.broadcasted_iota(jnp.int32, (B_HEADS, CHUNK), 0)
        lane_last = lax.broadcasted_iota(jnp.int32, (1, CHUNK), 1) == CHUNK - 1
        q, k, vv, dut, dwv, dqd, dkd = (_stack_tokens(r, GROUP)
                                        for r in (q_ref, k_ref, v_ref, du_ref, dw_ref, dqd_ref, dkd_ref))
        bc, gc, gr = _stack_cols(bc_ref, GROUP), _stack_cols(gc_ref, GROUP), _stack_rows(gr_ref, GROUP)
        dm, kk, ep, em, el, vb, kb = _chunk_local(q, k, vv, bc, gc, gr, tri)
        t = jnp.stack([t_ref[c, h] for c, h in _pairs(GROUP)])
        dav = jnp.stack([da_ref[c, h] for c, h in _pairs(GROUP)])
        qk = _bnt(q, k)
        dt = _nt(dut, vb, SOLVE_PREC) + _nt(dwv, kb, SOLVE_PREC)
        dvb = _tn(t, dut, SOLVE_PREC)
        dkb = _tn(t, dwv, SOLVE_PREC)
        dl = jnp.where(strict, -_tn(t, _nt(dt, t, SOLVE_PREC), SOLVE_PREC), 0.0)
        g1 = dl * dm
        dkb_k = jnp.sum(dkb * k, axis=-1, keepdims=True)
        dbeta = jnp.sum(g1 * kk, axis=-1, keepdims=True) + jnp.sum(dvb * vv, axis=-1, keepdims=True) + dkb_k * ep
        dkk = g1 * bc
        ddm = dl * (bc * kk) + dav * qk
        dqk = dav * dm
        dq = _bnn(dqk, k) + dqd * ep
        dk = _btn(dqk, q) + _bnn(dkk, k) + _btn(dkk, k) + dkb * (bc * ep) + dkd * em
        dv = dvb * bc
        dep = dkb_k * bc + jnp.sum(dqd * q, axis=-1, keepdims=True)
        dem = jnp.sum(dkd * k, axis=-1, keepdims=True)
        mm = ddm * dm
        dgam_c = jnp.sum(mm, axis=-1, keepdims=True) + dep * ep - dem * em
        dglast = jnp.sum(dem * em, axis=-2, keepdims=True)
        dgam_r = -jnp.sum(mm, axis=-2, keepdims=True) + jnp.where(lane_last, dglast, 0.0)
        for c in range(GROUP):
            dbc_acc = jnp.zeros((CHUNK, B_HEADS), F32)
            dgc_acc = jnp.zeros((CHUNK, B_HEADS), F32)
            dgr_acc = jnp.zeros((B_HEADS, CHUNK), F32)
            for h in range(B_HEADS):
                n = c * B_HEADS + h
                dq_ref[_tok(c), _head(h)] = dq[n]
                dk_ref[_tok(c), _head(h)] = dk[n]
                dv_ref[_tok(c), _head(h)] = dv[n]
                dbc_acc = jnp.where(lane4 == h, dbeta[n], dbc_acc)
                dgc_acc = jnp.where(lane4 == h, dgam_c[n], dgc_acc)
                dgr_acc = jnp.where(sub4 == h, dgam_r[n], dgr_acc)
            dbc_ref[c] = dbc_acc
            dgc_ref[c] = dgc_acc
            dgr_ref[c] = dgr_acc + dgl_ref[c]
        if ne:
            done()

    res = pl.pallas_call(
        body, name=name, grid=(steps,),
        in_specs=[tok, tok, tok, colv, colv, rowv, mat, tok, tok, tok, tok, mat, rowv] + [HBM] * ne,
        out_specs=[tok, tok, tok, colv, colv, rowv] + [HBM] * ne,
        out_shape=[jax.ShapeDtypeStruct((S, Wd), F32)] * 3
        + [jax.ShapeDtypeStruct((nc, CHUNK, B_HEADS), F32)] * 2 + [jax.ShapeDtypeStruct((nc, B_HEADS, CHUNK), F32)]
        + _chips_out_shapes(exchange),
        scratch_shapes=_chips_scratch(ne) if ne else [], compiler_params=_arb(1),
    )(qn, kn, v, bcol, gcol, grow, tsave, du, dw, dqd, dkd, da, dgl, *exchange)
    return tuple(res[:6]) + (list(res[6:]),)


def _gdn_post_fwd(o, proj, ng, name):
    S, W = o.shape
    tm = _pick(S, 512, 8)

    def body(o_ref, z_ref, g_ref, y_ref):
        gv = g_ref[...]
        for h in range(B_HEADS):
            hs = slice(h * B_DH, (h + 1) * B_DH)
            oh = o_ref[:, hs]
            z = z_ref[:, hs]
            r = lax.rsqrt(jnp.mean(oh * oh, axis=-1, keepdims=True) + EPS)
            y_ref[:, hs] = (oh * r * gv * (z * _sigmoid(z))).astype(BF16)

    return pl.pallas_call(
        body, name=name, grid=(S // tm,), in_specs=[_rows(tm, W), _rows(tm, W, P_Z // W), _vec(B_DH)],
        out_specs=_rows(tm, W), out_shape=jax.ShapeDtypeStruct((S, W), BF16), compiler_params=_par(1),
    )(o, proj, ng)


def _gdn_post_bwd(dy, o, proj, ng, dproj, name):
    S, W = o.shape
    tm = _pick(S, 512, 16)

    def body(dy_ref, o_ref, z_ref, g_ref, _, do_ref, dz_ref, dg_ref):
        i = pl.program_id(0)
        gv = g_ref[...]
        dg = jnp.zeros((1, B_DH), F32)
        for h in range(B_HEADS):
            hs = slice(h * B_DH, (h + 1) * B_DH)
            oh = o_ref[:, hs]
            z = z_ref[:, hs]
            d = dy_ref[:, hs]
            r = lax.rsqrt(jnp.mean(oh * oh, axis=-1, keepdims=True) + EPS)
            n = oh * r
            sg = _sigmoid(z)
            sz = z * sg
            dn = d * gv * sz
            dg = dg + jnp.sum(d * n * sz, axis=0, keepdims=True)
            dz_ref[:, hs] = (d * n * gv * (sg * (1.0 + z * (1.0 - sg)))).astype(BF16)
            do_ref[:, hs] = r * (dn - n * jnp.mean(dn * n, axis=-1, keepdims=True))
        _acc(dg_ref, dg, i)

    return pl.pallas_call(
        body, name=name, grid=(S // tm,),
        in_specs=[_rows(tm, W), _rows(tm, W), _rows(tm, W, P_Z // W), _vec(B_DH), HBM],
        out_specs=[_rows(tm, W), _rows(tm, W, P_Z // W), _vec(B_DH)],
        out_shape=[jax.ShapeDtypeStruct((S, W), F32), jax.ShapeDtypeStruct(dproj.shape, BF16),
                   jax.ShapeDtypeStruct((1, B_DH), F32)],
        input_output_aliases={4: 1}, compiler_params=_arb(1),
    )(dy, o, proj, ng, dproj)


def _ada_mod(c_all, w_ada, b_shard, name):
    L, D, Ns = w_ada.shape
    B = c_all.shape[0]

    def body(c_ref, w_ref, b_ref, o_ref):
        cv = c_ref[...]
        cond = (cv * _sigmoid(cv)).astype(BF16)
        o_ref[...] = _nn(cond, w_ref[...].astype(BF16)) + b_ref[...]

    return pl.pallas_call(
        body, name=name, grid=(L,),
        in_specs=[pl.BlockSpec((B, D), lambda l: (0, 0)), pl.BlockSpec((None, D, Ns), lambda l: (l, 0, 0)),
                  pl.BlockSpec((None, 1, Ns), lambda l: (l, 0, 0))],
        out_specs=pl.BlockSpec((None, B, Ns), lambda l: (l, 0, 0)),
        out_shape=jax.ShapeDtypeStruct((L, B, Ns), F32), compiler_params=_par(1),
    )(c_all, w_ada, b_shard)


def _ada_wgrad(c_all, dmod, name):
    L, B, Ns = dmod.shape
    D = c_all.shape[1]

    def body(c_ref, d_ref, o_ref):
        cv = c_ref[...]
        cond = (cv * _sigmoid(cv)).astype(BF16)
        o_ref[...] = _tn(cond, d_ref[...].astype(BF16))

    return pl.pallas_call(
        body, name=name, grid=(L,),
        in_specs=[pl.BlockSpec((B, D), lambda l: (0, 0)), pl.BlockSpec((None, B, Ns), lambda l: (l, 0, 0))],
        out_specs=pl.BlockSpec((None, D, Ns), lambda l: (l, 0, 0)),
        out_shape=jax.ShapeDtypeStruct((L, D, Ns), F32), compiler_params=_par(1),
    )(c_all, dmod)


W_IN_PIECES = ((0, 0, 1410), (1, 0, 1410), (2, 0, 252), (2, 772, 638), (3, 0, 1410), (2, 252, 512), (2, 764, 8))


def _reorder_w_in(w4, name):
    L, _, D, Cs = w4.shape
    tm = _pick(D, 256, 16)
    used = sum(p[2] for p in W_IN_PIECES)

    def body(w_ref, o_ref):
        shard = [w_ref[s] for s in range(4)]
        parts = [shard[s][:, lo:lo + n] for s, lo, n in W_IN_PIECES]
        o_ref[...] = jnp.concatenate(parts + [jnp.zeros((tm, P_END - used), w4.dtype)], axis=1)

    return pl.pallas_call(
        body, name=name, grid=(L, D // tm), in_specs=[pl.BlockSpec((None, 4, tm, Cs), lambda l, i: (l, 0, i, 0))],
        out_specs=pl.BlockSpec((None, tm, P_END), lambda l, i: (l, i, 0)),
        out_shape=jax.ShapeDtypeStruct((L, D, P_END), w4.dtype), compiler_params=_par(2),
    )(w4)


def _restore_w_in(g, name):
    D = g.shape[0]
    tm = _pick(D, 256, 16)

    def body(g_ref, o_ref, ob_ref):
        gv = g_ref[...]
        off = 0
        pieces = {}
        for s, lo, n in W_IN_PIECES:
            pieces.setdefault(s, []).append((lo, gv[:, off:off + n]))
            off += n
        for s, lst in pieces.items():
            lst.sort(key=lambda t: t[0])
            shard = lst[0][1] if len(lst) == 1 else jnp.concatenate([t[1] for t in lst], axis=1)
            o_ref[s] = shard
            ob_ref[s] = shard.astype(BF16)

    spec = pl.BlockSpec((4, tm, W_IN_SHARD), lambda i: (0, i, 0))
    return pl.pallas_call(
        body, name=name, grid=(D // tm,), in_specs=[pl.BlockSpec((tm, P_END), lambda i: (i, 0))],
        out_specs=[spec, spec],
        out_shape=[jax.ShapeDtypeStruct((4, D, W_IN_SHARD), g.dtype), jax.ShapeDtypeStruct((4, D, W_IN_SHARD), BF16)],
        compiler_params=_par(1),
    )(g)


def _adam_update(w, g, m, v):
    mn = ADAM_B1 * m + (1.0 - ADAM_B1) * g
    vn = ADAM_B2 * v + (1.0 - ADAM_B2) * (g * g)
    m_hat = mn / (1.0 - ADAM_B1 ** ADAM_STEP)
    v_hat = vn / (1.0 - ADAM_B2 ** ADAM_STEP)
    return -ADAM_LR * (m_hat / (jnp.sqrt(v_hat) + ADAM_EPS) + ADAM_WD * w), mn, vn


def _adamw(w, g, m, v, name):
    shape = w.shape
    C = shape[-1]
    R = w.size // C
    tm = _pick(R, 512, 8)
    spec = pl.BlockSpec((tm, C), lambda i: (i, 0))

    def body(w_ref, g_ref, m_ref, v_ref, d_ref, mo_ref, vo_ref):
        d_ref[...], mo_ref[...], vo_ref[...] = _adam_update(w_ref[...], g_ref[...], m_ref[...], v_ref[...])

    outs = pl.pallas_call(
        body, name=name, grid=(R // tm,), in_specs=[spec] * 4, out_specs=[spec] * 3,
        out_shape=[jax.ShapeDtypeStruct((R, C), F32)] * 3, compiler_params=_par(1),
    )(*(t.reshape(R, C) for t in (w, g, m, v)))
    return tuple(o.reshape(shape) for o in outs)


def _adamw_lead(w, g, m, v, name, tl):
    A, B, C = w.shape
    spec = pl.BlockSpec((tl, B, C), lambda i: (i, 0, 0))

    def body(w_ref, g_ref, m_ref, v_ref, d_ref, mo_ref, vo_ref):
        d_ref[...], mo_ref[...], vo_ref[...] = _adam_update(w_ref[...], g_ref[...], m_ref[...], v_ref[...])

    return pl.pallas_call(
        body, name=name, grid=(A // tl,), in_specs=[spec] * 4, out_specs=[spec] * 3,
        out_shape=[jax.ShapeDtypeStruct((A, B, C), F32)] * 3, compiler_params=_par(1),
    )(w, g, m, v)


def _adamw_layers(w, gs, m, v, name):
    L, R, C = w.shape
    tm = _pick(R, 256, 8)
    spec = pl.BlockSpec((None, tm, C), lambda l, i: (l, i, 0))
    g_specs = [pl.BlockSpec((tm, C), functools.partial(lambda ll, l, i: (jnp.where(l == ll, i, 0), 0), ll))
               for ll in range(L)]

    def body(w_ref, m_ref, v_ref, *rest):
        g_refs, (go_ref, d_ref, mo_ref, vo_ref) = rest[:L], rest[L:]
        l = pl.program_id(0)
        for ll in range(L):
            @pl.when(l == ll)
            def _():
                g = g_refs[ll][...]
                go_ref[...] = g
                d_ref[...], mo_ref[...], vo_ref[...] = _adam_update(w_ref[...], g, m_ref[...], v_ref[...])

    return pl.pallas_call(
        body, name=name, grid=(L, R // tm), in_specs=[spec] * 3 + g_specs, out_specs=[spec] * 4,
        out_shape=[jax.ShapeDtypeStruct((L, R, C), F32)] * 4, compiler_params=_arb(2),
    )(w, m, v, *gs)


def _pair_sums(a, where, b, name):
    NB, _, R, C = a.shape

    def body(where_ref, a_ref, b_ref, p_ref, own_ref):
        s = a_ref[...] + b_ref[...].astype(F32)
        p_ref[...] = s.astype(BF16)

        @pl.when(pl.program_id(0) == where_ref[1])
        def _():
            own_ref[...] = s

    return pl.pallas_call(
        body, name=name,
        grid_spec=pltpu.PrefetchScalarGridSpec(
            num_scalar_prefetch=1, grid=(NB,),
            in_specs=[pl.BlockSpec((None, None, R, C), lambda k, w: (k, w[0], 0, 0)),
                      pl.BlockSpec((None, R, C), lambda k, w: (k, 0, 0))],
            out_specs=[pl.BlockSpec((None, R, C), lambda k, w: (k, 0, 0)), pl.BlockSpec((R, C), lambda k, w: (0, 0))]),
        out_shape=[jax.ShapeDtypeStruct((NB, R, C), BF16), jax.ShapeDtypeStruct((R, C), F32)],
        compiler_params=_arb(1),
    )(where, a, b)


def _sum_own_and_received(own, recv, where, name):
    R, C = own.shape
    tm = _pick(R, 256, 16)

    def body(where_ref, p_ref, r_ref, o_ref):
        o_ref[...] = ((p_ref[...] + r_ref[0].astype(F32)) + r_ref[1].astype(F32)) + r_ref[2].astype(F32)

    return pl.pallas_call(
        body, name=name,
        grid_spec=pltpu.PrefetchScalarGridSpec(
            num_scalar_prefetch=1, grid=(R // tm,),
            in_specs=[pl.BlockSpec((tm, C), lambda i, w: (i, 0)), pl.BlockSpec((3, tm, C), lambda i, w: (0, i, 0))],
            out_specs=pl.BlockSpec((None, tm, C), lambda i, w: (w[0], i, 0))),
        out_shape=jax.ShapeDtypeStruct((2, R, C), F32), compiler_params=_par(1),
    )(where, own, recv)


def _position():
    return lax.axis_index("x"), lax.axis_index("y"), lax.axis_index("c")


def _other_chips(x, y):
    return [(1 - x, y), (x, 1 - y), (1 - x, 1 - y)]


HBM = pl.BlockSpec(memory_space=pl.ANY)


def _allgather8(blk, name, reduce_rows=None):
    M, N = blk.shape

    def body(x_ref, out_ref, *rest):
        if reduce_rows is None:
            send_sems, recv_sems, local_sem = rest
        else:
            sum_ref, send_sems, recv_sems, local_sem = rest
        x, y, c = _position()
        me, sibling = (x, y, c), (x, y, 1 - c)
        chips = _other_chips(x, y)

        def rows(px, py, pc):
            return out_ref.at[pl.ds((4 * px + 2 * py + pc) * M, M), :]

        def copy(k, block, to, src=None):
            return pltpu.make_async_remote_copy(
                src_ref=rows(*block) if src is None else src, dst_ref=rows(*block),
                send_sem=send_sems.at[k], recv_sem=recv_sems.at[k], device_id=to, device_id_type=MESH)

        mine = pltpu.make_async_copy(x_ref, rows(*me), local_sem)
        mine.start()
        first = [copy(0, me, sibling, src=x_ref)]
        first += [copy(1 + j, me, (*chip, c), src=x_ref) for j, chip in enumerate(chips)]
        for cp in first:
            cp.start()
        passed = [copy(4 + j, (*chip, c), sibling) for j, chip in enumerate(chips)]
        for j, chip in enumerate(chips):
            copy(1 + j, (*chip, c), me).wait_recv()
            passed[j].start()
        copy(0, sibling, me).wait_recv()
        for j, chip in enumerate(chips):
            copy(4 + j, (*chip, 1 - c), me).wait_recv()
        for cp in first + passed:
            cp.wait_send()
        mine.wait()
        if reduce_rows is not None:
            tot = out_ref[pl.ds(0, reduce_rows), :]
            for d in range(1, 8):
                tot = tot + out_ref[pl.ds(d * M, reduce_rows), :]
            sum_ref[...] = tot

    vmem = pl.BlockSpec(memory_space=pltpu.VMEM)
    out_shape = [jax.ShapeDtypeStruct((8 * M, N), blk.dtype)]
    if reduce_rows is not None:
        out_shape.append(jax.ShapeDtypeStruct((reduce_rows, N), blk.dtype))
    res = pl.pallas_call(
        body, name=name, out_shape=out_shape, in_specs=[vmem], out_specs=[vmem] * len(out_shape),
        scratch_shapes=[pltpu.SemaphoreType.DMA((7,)), pltpu.SemaphoreType.DMA((7,)), pltpu.SemaphoreType.DMA],
    )(blk)
    return res[0] if reduce_rows is None else (res[0], res[1])


def _gather_phases(layer, srcs, outs, send_sems, recv_sems, local_sems):
    n = len(srcs)
    x, y, c = _position()
    me, sibling = (x, y, c), (x, y, 1 - c)
    chips = _other_chips(x, y)

    def region(t, px, py, pc):
        return outs[t].at[2 * px + py, pc]

    def copy(t, k, block, to, own=False):
        return pltpu.make_async_remote_copy(
            src_ref=srcs[t].at[layer, c] if own else region(t, *block), dst_ref=region(t, *block),
            send_sem=send_sems.at[7 * t + k], recv_sem=recv_sems.at[7 * t + k], device_id=to, device_id_type=MESH)

    def local(t):
        return pltpu.make_async_copy(srcs[t].at[layer, c], region(t, *me), local_sems.at[t])

    def first(t):
        return [copy(t, 0, me, sibling, own=True)] + [copy(t, 1 + j, me, (*chip, c), own=True)
                                                       for j, chip in enumerate(chips)]

    def start():
        for t in range(n):
            local(t).start()
        for t in range(n):
            for cp in first(t):
                cp.start()

    def forward():
        for j, chip in enumerate(chips):
            for t in range(n):
                copy(t, 1 + j, (*chip, c), me).wait_recv()
                copy(t, 4 + j, (*chip, c), sibling).start()

    def finish():
        for t in range(n):
            copy(t, 0, sibling, me).wait_recv()
        for j, chip in enumerate(chips):
            for t in range(n):
                copy(t, 4 + j, (*chip, 1 - c), me).wait_recv()
        for t in range(n):
            for cp in first(t) + [copy(t, 4 + j, (*chip, c), sibling) for j, chip in enumerate(chips)]:
                cp.wait_send()
            local(t).wait()

    return start, forward, finish


def _gather_scratch(n):
    return [pltpu.SemaphoreType.DMA((7 * n,)), pltpu.SemaphoreType.DMA((7 * n,)), pltpu.SemaphoreType.DMA((n,))]


def _gather_out_shapes(shards):
    return [jax.ShapeDtypeStruct((4,) + s.shape[1:], s.dtype) for s in shards]


def _gather_weights(shards, layer, name):
    n = len(shards)

    def body(*refs):
        start, forward, finish = _gather_phases(layer, refs[:n], refs[n:2 * n], *refs[2 * n:])
        start()
        forward()
        finish()

    return pl.pallas_call(
        body, name=name, out_shape=_gather_out_shapes(shards), in_specs=[HBM] * n, out_specs=[HBM] * n,
        scratch_shapes=_gather_scratch(n),
    )(*shards)


def _sibling_phases(srcs, outs, send_sems, recv_sems):
    x, y, c = _position()
    copies = [pltpu.make_async_remote_copy(
        src_ref=srcs[t].at[k, 1 - c], dst_ref=outs[t].at[k], send_sem=send_sems.at[4 * t + k],
        recv_sem=recv_sems.at[4 * t + k], device_id=(x, y, 1 - c), device_id_type=MESH)
        for t in range(len(srcs)) for k in range(4)]

    def start():
        for cp in copies:
            cp.start()

    def finish():
        for cp in copies:
            cp.wait()

    return start, finish


def _sibling_scratch(n):
    return [pltpu.SemaphoreType.DMA((4 * n,)), pltpu.SemaphoreType.DMA((4 * n,))]


def _sibling_out_shapes(gs):
    return [jax.ShapeDtypeStruct((4,) + g.shape[2:], g.dtype) for g in gs]


def _rs_chips(ps, name):
    n = len(ps)

    def body(*refs):
        start, finish = _chips_phases(refs[:n], refs[n:2 * n], *refs[2 * n:])
        start()
        finish()

    return pl.pallas_call(
        body, name=name, out_shape=_chips_out_shapes(ps), in_specs=[HBM] * n, out_specs=[HBM] * n,
        scratch_shapes=_chips_scratch(n),
    )(*ps)


def _chips_phases(srcs, outs, send_sems, recv_sems):
    x, y, c = _position()
    copies = [pltpu.make_async_remote_copy(
        src_ref=srcs[t].at[2 * px + py], dst_ref=outs[t].at[j], send_sem=send_sems.at[3 * t + j],
        recv_sem=recv_sems.at[3 * t + j], device_id=(px, py, c), device_id_type=MESH)
        for t in range(len(srcs)) for j, (px, py) in enumerate(_other_chips(x, y))]

    def start():
        for cp in copies:
            cp.start()

    def finish():
        for cp in copies:
            cp.wait()

    return start, finish


def _chips_scratch(n):
    return [pltpu.SemaphoreType.DMA((3 * n,)), pltpu.SemaphoreType.DMA((3 * n,))]


def _chips_out_shapes(ps):
    return [jax.ShapeDtypeStruct((3,) + p.shape[1:], p.dtype) for p in ps]


def _rs_pair(hs, name):
    n = len(hs)

    def body(*refs):
        bufs = refs[n:2 * n]
        send_sems, recv_sems = refs[2 * n:]
        x, y, c = _position()

        def copy(t, half):
            return pltpu.make_async_remote_copy(
                src_ref=bufs[t].at[half], dst_ref=bufs[t].at[half], send_sem=send_sems.at[t], recv_sem=recv_sems.at[t],
                device_id=(x, y, 1 - c), device_id_type=MESH)

        for t in range(n):
            copy(t, c).start()
        for t in range(n):
            copy(t, 1 - c).wait_recv()
        for t in range(n):
            copy(t, c).wait_send()

    out_shape = [jax.ShapeDtypeStruct(h.shape, h.dtype) for h in hs]
    return pl.pallas_call(
        body, name=name, out_shape=out_shape, in_specs=[HBM] * n, out_specs=[HBM] * n,
        input_output_aliases={t: t for t in range(n)},
        scratch_shapes=[pltpu.SemaphoreType.DMA((n,)), pltpu.SemaphoreType.DMA((n,))],
    )(*hs)


BIG = ("w_in", "w_branch_a", "w_branch_b", "w_out", "w_ffn_in", "w_ffn_out")
CARRY_ATTN = ["w_in"]
CARRY_LOCAL = ["w_ffn_out"]
CARRY_SCAN = ["w_branch_a", "w_branch_b", "w_out"]
CARRY_GU = ["w_ffn_in"]
CARRY_DATTN = ["w_in", "w_ffn_in"]
CARRY_DLOCAL = ["w_branch_a", "w_branch_b", "w_out", "w_ffn_out"]


def _band_bias(rel_table, name, gather=None):
    L, H, n = rel_table.shape
    tab = jnp.pad(rel_table, ((0, 0), (0, 0), (0, NREL_PAD - n))).reshape(L * H, 1, NREL_PAD)
    band = (A_PAST + 1) * CHUNK

    shards, glayer = gather if gather is not None else ((), None)
    ng = len(shards)

    def body(t_ref, *rest):
        srcs, o_ref, gouts, sems = rest[:ng], rest[ng], rest[ng + 1:2 * ng + 1], rest[2 * ng + 1:]
        i = pl.program_id(0)
        done = (_carry(_gather_phases(glayer, srcs, gouts, *sems), i == 0, i == LATE * (L * H) // 8, i == L * H - 1)
                if ng else None)
        r = lax.broadcasted_iota(jnp.int32, (NREL_PAD, SKEW_W), 0)
        xi = lax.broadcasted_iota(jnp.int32, (NREL_PAD, SKEW_W), 1)
        diag = jnp.where(xi < KSPAN, xi, xi - SKEW_W)
        rel = jnp.clip(A_PAST * CHUNK - diag, -A_MAX_REL, A_MAX_REL) + A_MAX_REL
        e = _nn(t_ref[...], jnp.where(rel == r, 1.0, 0.0).astype(F32), HI)
        x = jnp.broadcast_to(e, (QBLK, SKEW_W))
        row = lax.broadcasted_iota(jnp.int32, (QBLK, SKEW_W), 0)
        for b in range(QBLK.bit_length() - 1):
            x = jnp.where(((row >> b) & 1) == 1, pltpu.roll(x, 1 << b, 1), x)
        x = x[:, :KSPAN]
        first = (lax.broadcasted_iota(jnp.int32, (QBLK, KSPAN), 0) // CHUNK) * CHUNK
        col = lax.broadcasted_iota(jnp.int32, (QBLK, KSPAN), 1)
        o_ref[...] = jnp.where((col >= first) & (col < first + band), x, NEG)
        if ng:
            done()

    res = pl.pallas_call(
        body, name=name, grid=(L * H,), in_specs=[pl.BlockSpec((None, 1, NREL_PAD), lambda i: (i, 0, 0))] + [HBM] * ng,
        out_specs=[pl.BlockSpec((None, QBLK, KSPAN), lambda i: (i, 0, 0))] + [HBM] * ng,
        out_shape=[jax.ShapeDtypeStruct((L * H, QBLK, KSPAN), F32)] + _gather_out_shapes(shards),
        scratch_shapes=_gather_scratch(ng) if ng else [], compiler_params=_arb(1),
    )(tab, *shards)
    return res[0].reshape(L, H, QBLK, KSPAN), list(res[1:])


def _col_row_forms(t, S):
    nc = S // CHUNK
    return t.T.reshape(nc, CHUNK, B_HEADS), t.reshape(B_HEADS, nc, CHUNK).transpose(1, 0, 2)


def _weight_view(name, gathered, tag):
    if name in ("w_out", "w_ffn_out"):
        return gathered.reshape(8 * gathered.shape[2], gathered.shape[3])
    stacked = gathered.reshape(4, 2 * gathered.shape[2], gathered.shape[3])
    return _reorder_w_in(stacked[None], f"w_in_cols_{tag}")[0] if name == "w_in" else stacked


def _layer_fwd(l, x, mod, W, P, big, gather=None, late=None):
    S, D = x.shape
    n = lambda s: f"{s}_l{l}"
    sh1, sc1, gt1, sh2, sc2, gt2 = (mod[i:i + 1] for i in range(6))
    h1 = _lnmod_fwd(x, P["norm1_g"][l:l + 1], sc1, sh1, n("ln1"))
    if late is None:
        proj = _matmul(h1, W["w_in"], "nn", F32, n("proj"), tn=1152)
    else:
        proj, got = _matmul(h1, W["w_in"], "nn", F32, n("proj"), tn=1152, gather=(late[1], l))
        W = {**W, **{k: _weight_view(k, t, f"l{l}") for k, t in zip(late[0], got)}}
    part = (lambda names: ([gather[0][BIG.index(k)] for k in names], gather[1])) if gather is not None else (lambda names: None)
    ya, got_a = _attn_fwd(proj, big, n("attn"), part(CARRY_ATTN))
    alog, dtb = P["a_log"][l].reshape(B_HEADS, 1), P["dt_bias"][l].reshape(B_HEADS, 1)
    beta, gam, b_t, a_t = _gdn_gates_fwd(proj, alog, dtb, n("gates"))
    bcol, _ = _col_row_forms(beta, S)
    gcol, grow = _col_row_forms(gam, S)
    qn, kn, v = _gdn_pre_fwd(proj, P["w_conv"][l], n("gdnpre"))
    tsave, amat, u, w, got_l = _gdn_local_fwd(qn, kn, v, bcol, gcol, grow, n("gdnlocal"), part(CARRY_LOCAL))
    o, ssave, got_s = _gdn_scan_fwd(qn, kn, u, w, amat, gcol, grow, n("gdnscan"), part(CARRY_SCAN))
    yb = _gdn_post_fwd(o, proj, P["gdn_norm_g"][l:l + 1], n("gdnpost"))
    pa = _matmul(ya, W["w_branch_a"], "nn", BF16, n("pa"), tm=2048, stacked=True)
    pb = _matmul(yb, W["w_branch_b"], "nn", BF16, n("pb"), tm=2048, stacked=True)
    merged = _merge_fwd(proj, pa, pb, n("merge"))
    ao = _matmul(merged, W["w_out"], "nn", F32, n("ao"))
    x1 = _gate_fwd(x, ao, gt1, n("res1"))
    h2 = _lnmod_fwd(x1, P["norm2_g"][l:l + 1], sc2, sh2, n("ln2"))
    gu = _matmul(h2, W["w_ffn_in"], "nn", BF16, n("gu"), stacked=True, gather=part(CARRY_GU))
    gu, got_g = gu if gather is not None else (gu, [])
    got = dict(zip(CARRY_ATTN + CARRY_LOCAL + CARRY_SCAN + CARRY_GU, got_a + got_l + got_s + got_g))
    gathered = {k: got[k] for k in BIG} if gather is not None else None
    act = _ffn_act_fwd(gu, n("act"))
    fo = _matmul(act, W["w_ffn_out"], "nn", F32, n("fo"), tk=1408)
    x2 = _gate_fwd(x1, fo, gt2, n("res2"))
    saved = dict(x=x, h1=h1, proj=proj, ya=ya, b_t=b_t, a_t=a_t, bcol=bcol, gcol=gcol, grow=grow,
                 qn=qn, kn=kn, v=v, o=o, tsave=tsave, ssave=ssave, amat=amat, u=u, w=w, yb=yb, pa=pa, pb=pb,
                 merged=merged, ao=ao, x1=x1,
                 h2=h2, gu=gu, act=act, fo=fo)
    return x2, saved, gathered, W


def _layer_bwd(l, dx2, sv, mod, W, P, big, exchange=()):
    S, D = dx2.shape
    n = lambda s: f"{s}_l{l}"
    sh1, sc1, gt1, sh2, sc2, gt2 = (mod[i:i + 1] for i in range(6))
    g, pay = {}, {}
    view = lambda t: t.reshape((4, 2, t.shape[-2] // (2 if t.ndim == 3 else 8), t.shape[-1]))
    dz2, dgt2 = _gate_bwd(dx2, sv["fo"], gt2, n("dres2"))
    g["w_ffn_out"], pay["w_ffn_out"] = map(view, _matmul(sv["act"], dz2, "tn", F32, n("dwfo"), tm=1408, also_bf16=True))
    dact = _matmul(dz2, W["w_ffn_out"], "nt", BF16, n("dact"), tn=1408)
    dgu = _ffn_act_bwd(sv["gu"], dact, n("dgu"))
    g["w_ffn_in"], pay["w_ffn_in"] = map(view, _matmul(sv["h2"], dgu, "tn", F32, n("dwfi"), out_stacked=True,
                                                       also_bf16=True))
    dh2 = _matmul(dgu, W["w_ffn_in"], "nt", F32, n("dh2"), stacked=True)
    dx1, dsh2, dsc2, dn2 = _lnmod_bwd(dh2, sv["x1"], P["norm2_g"][l:l + 1], sc2, dx2, n("dln2"))
    dz1, dgt1 = _gate_bwd(dx1, sv["ao"], gt1, n("dres1"))
    g["w_out"], pay["w_out"] = map(view, _matmul(sv["merged"], dz1, "tn", F32, n("dwo"), also_bf16=True))
    dmerged = _matmul(dz1, W["w_out"], "nt", BF16, n("dmerged"))
    dproj, dpa, dpb = _merge_bwd(sv["proj"], sv["pa"], sv["pb"], dmerged, n("dmerge"))
    g["w_branch_a"], pay["w_branch_a"] = map(view, _matmul(sv["ya"], dpa, "tn", F32, n("dwa"), out_stacked=True,
                                                           also_bf16=True))
    g["w_branch_b"], pay["w_branch_b"] = map(view, _matmul(sv["yb"], dpb, "tn", F32, n("dwb"), out_stacked=True,
                                                           also_bf16=True))
    dya = _matmul(dpa, W["w_branch_a"], "nt", BF16, n("dya"), tm=2048, stacked=True)
    dyb = _matmul(dpb, W["w_branch_b"], "nt", F32, n("dyb"), tm=2048, stacked=True)
    ex = (lambda names: [exchange[BIG.index(k)] for k in names]) if len(exchange) else (lambda names: ())
    dq, dk, dv, dbig, rec_a = _attn_bwd(sv["proj"], big, dya, n("dattn"), ex(CARRY_DATTN))
    g["rel_table"] = _rel_table_grad(dbig, n("drel"))[:, 0, :2 * A_MAX_REL + 1]
    dproj = _write_columns(dproj, [dq, dk, dv], 3 * dq.shape[1], P_QKVA // (3 * dq.shape[1]), n("dqkva"))
    do, dproj, dng = _gdn_post_bwd(dyb, sv["o"], sv["proj"], P["gdn_norm_g"][l:l + 1], dproj, n("dgdnpost"))
    g["gdn_norm_g"] = dng[0]
    du, dw, dqd, dkd, da, dgl = _gdn_scan_bwd(sv["qn"], sv["kn"], sv["u"], sv["w"], sv["amat"], sv["gcol"], sv["grow"],
                                              sv["ssave"], do, n("dgdnscan"))
    dqn, dkn, dvv, dbc, dgc, dgr, rec_l = _gdn_local_bwd(
        sv["qn"], sv["kn"], sv["v"], sv["bcol"], sv["gcol"], sv["grow"], sv["tsave"], du, dw, dqd, dkd, da, dgl,
        n("dgdnlocal"), ex(CARRY_DLOCAL))
    rec = dict(zip(CARRY_DATTN + CARRY_DLOCAL, rec_a + rec_l))
    received = [rec[k] for k in BIG] if len(exchange) else None
    dbeta_t = dbc.reshape(S, B_HEADS).T
    dgam_a = dgc.reshape(S, B_HEADS).T
    dgam_b = dgr.transpose(1, 0, 2).reshape(B_HEADS, S)
    alog, dtb = P["a_log"][l].reshape(B_HEADS, 1), P["dt_bias"][l].reshape(B_HEADS, 1)
    db_t, da_t, dal, ddt = _gdn_gates_bwd(dbeta_t, dgam_a, dgam_b, sv["b_t"], sv["a_t"], alog, dtb, n("dgates"))
    g["a_log"], g["dt_bias"] = dal[:, 0], ddt[:, 0]
    dproj, g["w_conv"] = _gdn_pre_bwd(sv["proj"], P["w_conv"][l], dqn, dkn, dvv, dproj, n("dgdnpre"))
    dba = jnp.concatenate([db_t.T, da_t.T, jnp.zeros((S, P_END - P_BA - 2 * B_HEADS), F32)], axis=1)
    dproj = _write_columns(dproj, [dba], dba.shape[1], P_BA // dba.shape[1], n("dba"))
    g["w_in"], pay["w_in"] = map(view, _restore_w_in(_matmul(sv["h1"], dproj, "tn", F32, n("dwin"), tn=1152),
                                                     n("dwin_cols")))
    dh1, from_sibling = _matmul(dproj, W["w_in"], "nt", F32, n("dh1"), tk=1152, sibling=[pay[k] for k in BIG])
    dx, dsh1, dsc1, dn1 = _lnmod_bwd(dh1, sv["x"], P["norm1_g"][l:l + 1], sc1, dx1, n("dln1"))
    g["norm1_g"], g["norm2_g"] = dn1[0], dn2[0]
    dmod = jnp.concatenate([dsh1, dsc1, dgt1, dsh2, dsc2, dgt2], axis=1)[0]
    return dx, g, from_sibling, dmod, received


SMALL = ("norm1_g", "norm2_g", "rel_table", "w_conv", "a_log", "dt_bias", "gdn_norm_g")
SMALL_PACK_C = 1024


def _as_rows(t):
    flat = t.reshape(-1)
    rows = -(-flat.shape[0] // SMALL_PACK_C)
    return jnp.pad(flat, (0, rows * SMALL_PACK_C - flat.shape[0])).reshape(rows, SMALL_PACK_C)


def _pack_rows(parts):
    blk = jnp.concatenate([_as_rows(p) for p in parts], axis=0)
    return jnp.pad(blk, ((0, -blk.shape[0] % 8), (0, 0)))


def _unpack_rows(blk, shapes):
    out, r = [], 0
    for shp in shapes:
        size = int(np.prod(shp))
        rows = -(-size // SMALL_PACK_C)
        out.append(blk[..., r:r + rows, :].reshape(blk.shape[:-2] + (rows * SMALL_PACK_C,))[..., :size]
                   .reshape(blk.shape[:-2] + tuple(shp)))
        r += rows
    return out


def kernel(x, c, w_ada, b_ada, norm1_g, norm2_g, w_in, rel_table, w_conv, a_log, dt_bias, gdn_norm_g, w_branch_a, w_branch_b, w_out, w_ffn_in, w_ffn_out, final_g, loss_target, m_w_ada, m_b_ada, m_norm1_g, m_norm2_g, m_w_in, m_rel_table, m_w_conv, m_a_log, m_dt_bias, m_gdn_norm_g, m_w_branch_a, m_w_branch_b, m_w_out, m_w_ffn_in, m_w_ffn_out, m_final_g, v_w_ada, v_b_ada, v_norm1_g, v_norm2_g, v_w_in, v_rel_table, v_w_conv, v_a_log, v_dt_bias, v_gdn_norm_g, v_w_branch_a, v_w_branch_b, v_w_out, v_w_ffn_in, v_w_ffn_out, v_final_g):
    weights = dict(w_ada=w_ada, b_ada=b_ada, norm1_g=norm1_g, norm2_g=norm2_g, w_in=w_in, rel_table=rel_table,
                   w_conv=w_conv, a_log=a_log, dt_bias=dt_bias, gdn_norm_g=gdn_norm_g, w_branch_a=w_branch_a,
                   w_branch_b=w_branch_b, w_out=w_out, w_ffn_in=w_ffn_in, w_ffn_out=w_ffn_out, final_g=final_g)
    mom_m = dict(w_ada=m_w_ada, b_ada=m_b_ada, norm1_g=m_norm1_g, norm2_g=m_norm2_g, w_in=m_w_in,
                 rel_table=m_rel_table, w_conv=m_w_conv, a_log=m_a_log, dt_bias=m_dt_bias, gdn_norm_g=m_gdn_norm_g,
                 w_branch_a=m_w_branch_a, w_branch_b=m_w_branch_b, w_out=m_w_out, w_ffn_in=m_w_ffn_in,
                 w_ffn_out=m_w_ffn_out, final_g=m_final_g)
    mom_v = dict(w_ada=v_w_ada, b_ada=v_b_ada, norm1_g=v_norm1_g, norm2_g=v_norm2_g, w_in=v_w_in,
                 rel_table=v_rel_table, w_conv=v_w_conv, a_log=v_a_log, dt_bias=v_dt_bias, gdn_norm_g=v_gdn_norm_g,
                 w_branch_a=v_w_branch_a, w_branch_b=v_w_branch_b, w_out=v_w_out, w_ffn_in=v_w_ffn_in,
                 w_ffn_out=v_w_ffn_out, final_g=v_final_g)
    xi, yi, ci = _position()
    chip = 2 * xi + yi
    dev = 2 * chip + ci
    L, D = norm1_g.shape
    NMOD = b_ada.shape[1] // D
    ns = w_ada.shape[2]
    cs = w_conv.shape[2]

    first_blk = _pack_rows([c, w_conv])
    first_all = _allgather8(first_blk, "gather_c").reshape(8, first_blk.shape[0], SMALL_PACK_C)
    c_all, w_conv_all = _unpack_rows(first_all, [(D,), w_conv.shape])
    w_conv_full = w_conv_all.reshape(4, 2, L, CONV_K, cs)[:, 0].transpose(1, 2, 0, 3).reshape(L, CONV_K, 4 * cs)
    b_shard = lax.dynamic_slice_in_dim(b_ada, chip * ns, ns, axis=1).reshape(L, 1, ns)
    mod_shard = _ada_mod(c_all, w_ada, b_shard, "ada_mod")
    mod_all = _allgather8(mod_shard.reshape(L * 8, ns), "gather_mod").reshape(4, 2, L, 8, ns)
    mod = lax.dynamic_index_in_dim(mod_all[:, 0], dev, axis=2, keepdims=False)
    mod = mod.transpose(1, 0, 2).reshape(L, NMOD, D)

    shards = [weights[k].astype(BF16) for k in BIG]
    shards = [s.reshape(s.shape[0], 2, s.shape[1] // 2, s.shape[2]) for s in shards]
    P = dict(norm1_g=norm1_g, norm2_g=norm2_g, w_conv=w_conv_full, a_log=a_log, dt_bias=dt_bias,
             gdn_norm_g=gdn_norm_g)
    shard_of = dict(zip(BIG, shards))

    big, got = _band_bias(rel_table, "band_bias", ([shard_of["w_in"]], 0))
    alone = ["w_branch_a", "w_branch_b", "w_out", "w_ffn_out"]
    got += _gather_weights([shard_of[k] for k in alone], 0, "gather_weights_l0")
    W = [{k: _weight_view(k, t, "l0") for k, t in zip(["w_in"] + alone, got)}]
    late = (["w_ffn_in"], [shard_of["w_ffn_in"]])
    xc = x[0]
    saved = []
    for l in range(L):
        xc, sv, gathered, W[l] = _layer_fwd(l, xc, mod[l], W[l], P, big[l], (shards, l + 1) if l + 1 < L else None,
                                           late if l == 0 else None)
        saved.append(sv)
        if l + 1 < L:
            W.append({k: _weight_view(k, gathered[k], f"l{l + 1}") for k in BIG})
    dx, loss_dev, dfinal = _loss_head(xc, final_g.reshape(1, D), loss_target[0], "loss_head")

    where = jnp.stack([ci, chip]).astype(jnp.int32)
    grads = [None] * L
    dmods = [None] * L
    shard_grads = {k: [None] * L for k in BIG}

    def finish_reduce_scatter(l, sums, from_chips):
        halves = [_sum_own_and_received(s_[1], r_, where, f"rs_sum_{k}_l{l}")
                  for k, s_, r_ in zip(BIG, sums, from_chips)]
        for k, t in zip(BIG, _rs_pair(halves, f"rs_pair_l{l}")):
            shard_grads[k][l] = t.reshape(2 * t.shape[1], t.shape[2])

    pending = None
    for l in reversed(range(L)):
        exchange = [s_[0] for s_ in pending] if pending is not None else ()
        dx, grads[l], from_sibling, dmods[l], received = _layer_bwd(l, dx, saved[l], mod[l], W[l], P, big[l], exchange)
        if pending is not None:
            finish_reduce_scatter(l + 1, pending, received)
        gs = [grads[l][k] for k in BIG]
        pending = [_pair_sums(g_, where, r_, f"rs_pair_sum_{k}_l{l}") for k, g_, r_ in zip(BIG, gs, from_sibling)]
    finish_reduce_scatter(0, pending, _rs_chips([s_[0] for s_ in pending], "rs_chips_l0"))
    dmod = jnp.stack(dmods)

    small = {k: jnp.stack([grads[l][k] for l in range(L)]) for k in SMALL}
    parts = [dmod] + [small[k] for k in SMALL] + [dfinal, loss_dev[0, :1]]
    small_blk = _pack_rows(parts)
    srows = small_blk.shape[0]
    small_all, small_sum = _allgather8(small_blk, "gather_small", reduce_rows=srows)
    shapes = [dmod.shape] + [small[k].shape for k in SMALL] + [(D,), (1,)]
    tot = _unpack_rows(small_sum, shapes)
    G = dict(zip(SMALL, tot[1:1 + len(SMALL)]))
    G["b_ada"] = tot[0].reshape(b_ada.shape)
    G["w_conv"] = lax.dynamic_slice_in_dim(G["w_conv"], chip * cs, cs, axis=2)
    G["final_g"] = tot[-2]
    loss = tot[-1][0]
    dmod_all = _unpack_rows(small_all.reshape(8, srows, SMALL_PACK_C), [dmod.shape])[0]
    dmod_cols = lax.dynamic_slice_in_dim(dmod_all, chip * ns, ns, axis=2).transpose(1, 0, 2)
    G["w_ada"] = _ada_wgrad(c_all, dmod_cols, "ada_wgrad")

    order = ["w_ada", "b_ada", "norm1_g", "norm2_g", "w_in", "rel_table", "w_conv", "a_log", "dt_bias", "gdn_norm_g",
             "w_branch_a", "w_branch_b", "w_out", "w_ffn_in", "w_ffn_out", "final_g"]
    deltas, new_m, new_v = {}, {}, {}
    for k in order:
        w = weights[k]
        if k == "w_in":
            to_cols = lambda t: jnp.transpose(t, (2, 0, 1))
            from_cols = lambda t: jnp.transpose(t, (1, 2, 0))
            gt = to_cols(jnp.stack(shard_grads[k]))
            d_, m_, v_ = _adamw_lead(to_cols(w), gt, to_cols(mom_m[k]), to_cols(mom_v[k]), f"adamw_{k}",
                                     W_IN_SHARD // 30)
            G[k], deltas[k], new_m[k], new_v[k] = from_cols(gt), from_cols(d_), from_cols(m_), from_cols(v_)
            continue
        if k in BIG:
            G[k], deltas[k], new_m[k], new_v[k] = _adamw_layers(w, shard_grads[k], mom_m[k], mom_v[k], f"adamw_{k}")
            continue
        as2d = (lambda t: t.reshape(1, -1)) if w.ndim == 1 else (lambda t: t)
        d_, m_, v_ = _adamw(as2d(w), as2d(G[k]), as2d(mom_m[k]), as2d(mom_v[k]), f"adamw_{k}")
        deltas[k], new_m[k], new_v[k] = d_.reshape(w.shape), m_.reshape(w.shape), v_.reshape(w.shape)
    return (loss, dx[None], *[G[k] for k in order], *[deltas[k] for k in order], *[new_m[k] for k in order],
            *[new_v[k] for k in order])
```

```python
import functools

import numpy as np
import jax
import jax.numpy as jnp
from jax import lax
from jax.experimental import pallas as pl
from jax.experimental.pallas import tpu as pltpu

F32 = jnp.float32
BF16 = jnp.bfloat16
HI = lax.Precision.HIGHEST
SOLVE_PREC = lax.Precision.HIGH
MESH = pl.DeviceIdType.MESH

EPS = 1e-6
CHUNK = 64
A_HEADS = 8
A_DH = 64
A_PAST = 8
A_MAX_REL = 128
B_HEADS = 4
B_DH = 128
CONV_K = 4
LANE = 128
QBLK = 4 * CHUNK
KSPAN = QBLK + A_PAST * CHUNK
NEG = -1e30

ADAM_LR = 0.001
ADAM_B1 = 0.9
ADAM_B2 = 0.999
ADAM_EPS = 1e-08
ADAM_WD = 0.01
ADAM_STEP = 10

P_QKVA, P_QKVB, P_GA, P_GB, P_Z, P_BA, P_END = 0, 1536, 3072, 4096, 5120, 5632, 5760
W_IN_SHARD = 1410


def _sigmoid(x):
    return 1.0 / (1.0 + jnp.exp(-x))


def _dot(a, b, ca, cb, prec):
    lead = a.ndim - 2
    batch = ((0,), (0,)) if lead else ((), ())
    return lax.dot_general(a, b, (((ca + lead,), (cb + lead,)), batch), precision=prec, preferred_element_type=F32)


def _nn(a, b, prec=None):
    return _dot(a, b, 1, 0, prec)


def _nt(a, b, prec=None):
    return _dot(a, b, 1, 1, prec)


def _tn(a, b, prec=None):
    return _dot(a, b, 0, 0, prec)


def _bnn(a, b):
    return _nn(a.astype(BF16), b.astype(BF16))


def _bnt(a, b):
    return _nt(a.astype(BF16), b.astype(BF16))


def _btn(a, b):
    return _tn(a.astype(BF16), b.astype(BF16))


def _pick(n, target, unit=LANE):
    best = None
    for t in range(unit, min(n, target) + 1, unit):
        if n % t == 0:
            best = t
    return best if best is not None else n


def _acc(ref, val, i):
    @pl.when(i == 0)
    def _():
        ref[...] = val

    @pl.when(i != 0)
    def _():
        ref[...] += val


def _arb(n):
    return pltpu.CompilerParams(dimension_semantics=("arbitrary",) * n)


def _par(n):
    return pltpu.CompilerParams(dimension_semantics=("parallel",) * n)


def _matmul(a, b, mode, out_dtype, name, tm=1024, tn=1024, tk=1024, layer=None, stacked=False, out_stacked=False,
            also_bf16=False, gather=None, sibling=None):
    bs = b.shape[1:] if layer is not None else b.shape
    if mode == "nn":
        M, K = a.shape
        N = 4 * bs[2] if stacked else bs[1]
        if stacked:
            tn = bs[2]
    elif mode == "nt":
        M, K = a.shape
        N = bs[1] if stacked else bs[0]
        if stacked:
            tk = bs[2]
    else:
        K, M = a.shape
        N = bs[1]
        if out_stacked:
            tn = N // 4
    tm, tn, tk = _pick(M, tm), _pick(N, tn), _pick(K, tk)
    nk = K // tk
    lead = () if layer is None else (layer,)
    lead_blk = () if layer is None else (None,)
    if mode == "nn":
        a_spec = pl.BlockSpec((tm, tk), lambda i, j, k: (i, k))
        if stacked:
            b_spec = pl.BlockSpec(lead_blk + (None, tk, tn), lambda i, j, k: lead + (j, k, 0))
        else:
            b_spec = pl.BlockSpec(lead_blk + (tk, tn), lambda i, j, k: lead + (k, j))
        dot = _nn
    elif mode == "nt":
        a_spec = pl.BlockSpec((tm, tk), lambda i, j, k: (i, k))
        if stacked:
            b_spec = pl.BlockSpec(lead_blk + (None, tn, tk), lambda i, j, k: lead + (k, j, 0))
        else:
            b_spec = pl.BlockSpec(lead_blk + (tn, tk), lambda i, j, k: lead + (j, k))
        dot = _nt
    else:
        a_spec = pl.BlockSpec((tk, tm), lambda i, j, k: (k, i))
        b_spec = pl.BlockSpec((tk, tn), lambda i, j, k: (k, j))
        dot = _tn
    if out_stacked:
        o_spec = pl.BlockSpec((None, tm, tn), lambda i, j, k: (j, i, 0))
        o_shape = jax.ShapeDtypeStruct((4, M, tn), out_dtype)
    else:
        o_spec = pl.BlockSpec((tm, tn), lambda i, j, k: (i, j))
        o_shape = jax.ShapeDtypeStruct((M, N), out_dtype)

    if gather is not None:
        shards, glayer = gather
        carried_shapes, carried_scratch = _gather_out_shapes(shards), _gather_scratch(len(shards))
    elif sibling is not None:
        shards = sibling
        carried_shapes, carried_scratch = _sibling_out_shapes(shards), _sibling_scratch(len(shards))
    else:
        shards, carried_shapes, carried_scratch = (), [], []
    ng = len(shards)
    o_shapes = [o_shape] + ([jax.ShapeDtypeStruct(o_shape.shape, BF16)] if also_bf16 else [])
    no = len(o_shapes)
    gi, gj = M // tm, N // tn

    def write(o_refs, val):
        for o_ref in o_refs:
            o_ref[...] = val.astype(o_ref.dtype)

    def body(a_ref, b_ref, *refs):
        srcs, o_refs, gouts, scratch = refs[:ng], refs[ng:ng + no], refs[ng + no:2 * ng + no], refs[2 * ng + no:]
        i, j, k = pl.program_id(0), pl.program_id(1), pl.program_id(2)
        if ng:
            start = (j == 0) & (k == 0)
            sems = scratch[len(scratch) - len(carried_scratch):]
            phases = (_gather_phases(glayer, srcs, gouts, *sems) if gather is not None
                      else _sibling_phases(srcs, gouts, *sems))
            done = _carry(phases, (i == 0) & start, (i == gi - 1) & start,
                          (i == gi - 1) & (j == gj - 1) & (k == nk - 1))
        if nk == 1:
            write(o_refs, dot(a_ref[...], b_ref[...]))
        else:
            acc_ref = scratch[0]

            @pl.when(k == 0)
            def _():
                acc_ref[...] = jnp.zeros_like(acc_ref)

            acc_ref[...] += dot(a_ref[...], b_ref[...])

            @pl.when(k == nk - 1)
            def _():
                write(o_refs, acc_ref[...])
        if ng:
            done()

    sem = ("arbitrary",) * 3 if ng else ("parallel", "parallel", "arbitrary")
    res = pl.pallas_call(
        body, name=name, grid=(gi, gj, nk), in_specs=[a_spec, b_spec] + [HBM] * ng,
        out_specs=[o_spec] * no + [HBM] * ng, out_shape=o_shapes + carried_shapes,
        scratch_shapes=([] if nk == 1 else [pltpu.VMEM((tm, tn), F32)]) + carried_scratch,
        compiler_params=pltpu.CompilerParams(dimension_semantics=sem),
    )(a, b, *shards)
    out = tuple(res[:no]) if also_bf16 else res[0]
    return (out, list(res[no:])) if ng else out


def _rows(tm, n, col=0):
    return pl.BlockSpec((tm, n), lambda i: (i, col))


def _vec(n):
    return pl.BlockSpec((1, n), lambda i: (0, 0))


def _lnmod_fwd(x, g, sc, sh, name):
    S, D = x.shape
    tm = _pick(S, 512, 8)

    def body(x_ref, g_ref, sc_ref, sh_ref, o_ref):
        xv = x_ref[...]
        r = lax.rsqrt(jnp.mean(xv * xv, axis=-1, keepdims=True) + EPS)
        o_ref[...] = ((xv * r * g_ref[...]) * (1.0 + sc_ref[...]) + sh_ref[...]).astype(BF16)

    return pl.pallas_call(
        body, name=name, grid=(S // tm,),
        in_specs=[_rows(tm, D), _vec(D), _vec(D), _vec(D)], out_specs=_rows(tm, D),
        out_shape=jax.ShapeDtypeStruct((S, D), BF16), compiler_params=_par(1),
    )(x, g, sc, sh)


def _lnmod_bwd(dh, x, g, sc, dres, name):
    S, D = x.shape
    tm = _pick(S, 512, 8)

    def body(dh_ref, x_ref, g_ref, sc_ref, dres_ref, dx_ref, dsh_ref, dsc_ref, dg_ref):
        i = pl.program_id(0)
        xv = x_ref[...]
        dh_ = dh_ref[...]
        r = lax.rsqrt(jnp.mean(xv * xv, axis=-1, keepdims=True) + EPS)
        xhat = xv * r
        gv = g_ref[...]
        dn = dh_ * (1.0 + sc_ref[...])
        dxhat = dn * gv
        dx_ref[...] = dres_ref[...] + r * (dxhat - xhat * jnp.mean(dxhat * xhat, axis=-1, keepdims=True))
        _acc(dsh_ref, jnp.sum(dh_, axis=0, keepdims=True), i)
        _acc(dsc_ref, jnp.sum(dh_ * (xhat * gv), axis=0, keepdims=True), i)
        _acc(dg_ref, jnp.sum(dn * xhat, axis=0, keepdims=True), i)

    return pl.pallas_call(
        body, name=name, grid=(S // tm,),
        in_specs=[_rows(tm, D), _rows(tm, D), _vec(D), _vec(D), _rows(tm, D)],
        out_specs=[_rows(tm, D), _vec(D), _vec(D), _vec(D)],
        out_shape=[jax.ShapeDtypeStruct((S, D), F32)] + [jax.ShapeDtypeStruct((1, D), F32)] * 3,
        compiler_params=_arb(1),
    )(dh, x, g, sc, dres)


def _gate_fwd(x, y, gt, name):
    S, D = x.shape
    tm = _pick(S, 512, 8)

    def body(x_ref, y_ref, gt_ref, o_ref):
        o_ref[...] = x_ref[...] + gt_ref[...] * y_ref[...]

    return pl.pallas_call(
        body, name=name, grid=(S // tm,), in_specs=[_rows(tm, D), _rows(tm, D), _vec(D)], out_specs=_rows(tm, D),
        out_shape=jax.ShapeDtypeStruct((S, D), F32), compiler_params=_par(1),
    )(x, y, gt)


def _gate_bwd(dx, y, gt, name):
    S, D = dx.shape
    tm = _pick(S, 512, 8)

    def body(dx_ref, y_ref, gt_ref, dz_ref, dgt_ref):
        i = pl.program_id(0)
        d = dx_ref[...]
        dz_ref[...] = (d * gt_ref[...]).astype(BF16)
        _acc(dgt_ref, jnp.sum(d * y_ref[...], axis=0, keepdims=True), i)

    return pl.pallas_call(
        body, name=name, grid=(S // tm,), in_specs=[_rows(tm, D), _rows(tm, D), _vec(D)],
        out_specs=[_rows(tm, D), _vec(D)],
        out_shape=[jax.ShapeDtypeStruct((S, D), BF16), jax.ShapeDtypeStruct((1, D), F32)],
        compiler_params=_arb(1),
    )(dx, y, gt)


def _ffn_act_fwd(gu, name):
    S, H2 = gu.shape
    H = H2 // 2
    tm = _pick(S, 256, 8)

    def body(g_ref, u_ref, o_ref):
        gv = g_ref[...].astype(F32)
        o_ref[...] = (gv * _sigmoid(gv) * u_ref[...].astype(F32)).astype(BF16)

    return pl.pallas_call(
        body, name=name, grid=(S // tm,), in_specs=[_rows(tm, H, 0), _rows(tm, H, 1)], out_specs=_rows(tm, H),
        out_shape=jax.ShapeDtypeStruct((S, H), BF16), compiler_params=_par(1),
    )(gu, gu)


def _ffn_act_bwd(gu, dact, name):
    S, H2 = gu.shape
    H = H2 // 2
    tm = _pick(S, 256, 8)

    def body(g_ref, u_ref, da_ref, o_ref):
        gv = g_ref[...].astype(F32)
        s = _sigmoid(gv)
        da = da_ref[...].astype(F32)
        o_ref[:, :H] = (da * u_ref[...].astype(F32) * (s * (1.0 + gv * (1.0 - s)))).astype(BF16)
        o_ref[:, H:] = (da * (gv * s)).astype(BF16)

    return pl.pallas_call(
        body, name=name, grid=(S // tm,), in_specs=[_rows(tm, H, 0), _rows(tm, H, 1), _rows(tm, H)],
        out_specs=_rows(tm, H2), out_shape=jax.ShapeDtypeStruct((S, H2), BF16), compiler_params=_par(1),
    )(gu, gu, dact)


def _merge_fwd(proj, pa, pb, name):
    S, D = pa.shape
    tm = _pick(S, 512, 8)

    def body(ga_ref, gb_ref, pa_ref, pb_ref, o_ref):
        o_ref[...] = (_sigmoid(ga_ref[...]) * pa_ref[...].astype(F32)
                      + _sigmoid(gb_ref[...]) * pb_ref[...].astype(F32)).astype(BF16)

    return pl.pallas_call(
        body, name=name, grid=(S // tm,),
        in_specs=[_rows(tm, D, P_GA // D), _rows(tm, D, P_GB // D), _rows(tm, D), _rows(tm, D)],
        out_specs=_rows(tm, D), out_shape=jax.ShapeDtypeStruct((S, D), BF16), compiler_params=_par(1),
    )(proj, proj, pa, pb)


def _merge_bwd(proj, pa, pb, dm, name):
    S, D = pa.shape
    tm = _pick(S, 512, 16)
    rows_j = pl.BlockSpec((tm, D), lambda i, j: (i, 0))

    def body(g_ref, pa_ref, pb_ref, dm_ref, dg_ref, dpa_ref, dpb_ref):
        d = dm_ref[...].astype(F32)
        s = _sigmoid(g_ref[...])
        for branch, p_ref, dp_ref in ((0, pa_ref, dpa_ref), (1, pb_ref, dpb_ref)):
            @pl.when(pl.program_id(1) == branch)
            def _():
                dg_ref[...] = (d * p_ref[...].astype(F32) * s * (1.0 - s)).astype(BF16)
                dp_ref[...] = (d * s).astype(BF16)

    return pl.pallas_call(
        body, name=name, grid=(S // tm, 2),
        in_specs=[pl.BlockSpec((tm, D), lambda i, j: (i, P_GA // D + j)), rows_j, rows_j, rows_j],
        out_specs=[pl.BlockSpec((tm, D), lambda i, j: (i, P_GA // D + j)), rows_j, rows_j],
        out_shape=[jax.ShapeDtypeStruct((S, P_END), BF16), jax.ShapeDtypeStruct((S, D), BF16),
                   jax.ShapeDtypeStruct((S, D), BF16)],
        compiler_params=_arb(2),
    )(proj, pa, pb, dm)


def _write_columns(buf, parts, width, colblk, name):
    S = buf.shape[0]
    tm = _pick(S, 512, 16)
    n = len(parts)

    def body(*refs):
        o_ref = refs[n + 1]
        off = 0
        for p_ref in refs[:n]:
            w = p_ref.shape[1]
            o_ref[:, off:off + w] = p_ref[...].astype(BF16)
            off += w

    return pl.pallas_call(
        body, name=name, grid=(S // tm,), in_specs=[_rows(tm, p.shape[1]) for p in parts] + [HBM],
        out_specs=_rows(tm, width, colblk), out_shape=jax.ShapeDtypeStruct(buf.shape, buf.dtype),
        input_output_aliases={n: 0}, compiler_params=_par(1),
    )(*parts, buf)


def _loss_head(x, g, target, name):
    S, D = x.shape
    tm = _pick(S, 512, 8)

    def body(x_ref, g_ref, t_ref, dx_ref, loss_ref, dg_ref):
        i = pl.program_id(0)
        xv = x_ref[...]
        gv = g_ref[...]
        r = lax.rsqrt(jnp.mean(xv * xv, axis=-1, keepdims=True) + EPS)
        xhat = xv * r
        err = xhat * gv - t_ref[...]
        part = 0.5 * jnp.sum(jnp.mean(err * err, axis=-1, keepdims=True), axis=0, keepdims=True)
        _acc(loss_ref, jnp.broadcast_to(part, (1, LANE)), i)
        dy = err * (1.0 / D)
        _acc(dg_ref, jnp.sum(dy * xhat, axis=0, keepdims=True), i)
        dxhat = dy * gv
        dx_ref[...] = r * (dxhat - xhat * jnp.mean(dxhat * xhat, axis=-1, keepdims=True))

    return pl.pallas_call(
        body, name=name, grid=(S // tm,), in_specs=[_rows(tm, D), _vec(D), _rows(tm, D)],
        out_specs=[_rows(tm, D), _vec(LANE), _vec(D)],
        out_shape=[jax.ShapeDtypeStruct((S, D), F32), jax.ShapeDtypeStruct((1, LANE), F32),
                   jax.ShapeDtypeStruct((1, D), F32)],
        compiler_params=_arb(1),
    )(x, g, target)


HEADS_PER_SLAB = LANE // A_DH
N_SLABS = A_HEADS // HEADS_PER_SLAB
SPAN_BLOCKS = KSPAN // QBLK


def _attn_specs(seg):
    q_spec = pl.BlockSpec((QBLK, LANE), lambda p, m: (m, seg[0] * N_SLABS + p))
    k_specs = [pl.BlockSpec((QBLK, LANE), functools.partial(
        lambda j, p, m: (jnp.maximum(m - (SPAN_BLOCKS - 1) + j, 0), seg[1] * N_SLABS + p), j)) for j in range(SPAN_BLOCKS)]
    v_specs = [pl.BlockSpec((QBLK, LANE), functools.partial(
        lambda j, p, m: (jnp.maximum(m - (SPAN_BLOCKS - 1) + j, 0), seg[2] * N_SLABS + p), j)) for j in range(SPAN_BLOCKS)]
    b_spec = pl.BlockSpec((HEADS_PER_SLAB, QBLK, KSPAN), lambda p, m: (p, 0, 0))
    return q_spec, k_specs, v_specs, b_spec


def _head_lanes(t, hh):
    lane = lax.broadcasted_iota(jnp.int32, t.shape, 1)
    return jnp.where((lane // A_DH) == hh, t, jnp.zeros_like(t))


def _front_mask(m):
    col = lax.broadcasted_iota(jnp.int32, (QBLK, KSPAN), 1)
    return jnp.where(col < (SPAN_BLOCKS - 1 - m) * QBLK, NEG, 0.0)


def _attn_probs(qk, bias, front):
    s = qk * (A_DH ** -0.5) + (bias + front)
    p = jnp.exp(s - jnp.max(s, axis=-1, keepdims=True))
    return p * (1.0 / jnp.sum(p, axis=-1, keepdims=True))


def _grid_ends(nq):
    p, m = pl.program_id(0), pl.program_id(1)
    return (p == 0) & (m == 0), (p == N_SLABS - 1) & (m == nq // 2), (p == N_SLABS - 1) & (m == nq - 1)


def _attn_fwd(proj, big, name, gather=None):
    S = proj.shape[0]
    q_spec, k_specs, v_specs, b_spec = _attn_specs((0, 1, 2))
    shards, layer = gather if gather is not None else ((), None)
    ng = len(shards)

    def body(q_ref, k0, k1, k2, v0, v1, v2, b_ref, *rest):
        srcs, o_ref, gouts, sems = rest[:ng], rest[ng], rest[ng + 1:2 * ng + 1], rest[2 * ng + 1:]
        done = _carry(_gather_phases(layer, srcs, gouts, *sems), *_grid_ends(S // QBLK)) if ng else None
        m = pl.program_id(1)
        q = q_ref[...].astype(BF16)
        k = jnp.concatenate([k0[...], k1[...], k2[...]], axis=0).astype(BF16)
        v = jnp.concatenate([v0[...], v1[...], v2[...]], axis=0).astype(BF16)
        front = _front_mask(m)
        heads = range(HEADS_PER_SLAB)
        scores = [_nt(_head_lanes(q, hh), k) for hh in heads]
        probs = [_attn_probs(scores[hh], b_ref[hh], front).astype(BF16) for hh in heads]
        outs = [_nn(probs[hh], v) for hh in heads]
        lane = lax.broadcasted_iota(jnp.int32, (QBLK, LANE), 1)
        o_ref[...] = jnp.where(lane < A_DH, outs[0], outs[1]).astype(BF16)
        if ng:
            done()

    res = pl.pallas_call(
        body, name=name, grid=(N_SLABS, S // QBLK), in_specs=[q_spec] + k_specs + v_specs + [b_spec] + [HBM] * ng,
        out_specs=[pl.BlockSpec((QBLK, LANE), lambda p, m: (m, p))] + [HBM] * ng,
        out_shape=[jax.ShapeDtypeStruct((S, A_HEADS * A_DH), BF16)] + _gather_out_shapes(shards),
        scratch_shapes=_gather_scratch(ng) if ng else [], compiler_params=_arb(2),
    )(proj, proj, proj, proj, proj, proj, proj, big, *shards)
    return res[0], list(res[1:])


def _attn_bwd(proj, big, dya, name, exchange=()):
    S = proj.shape[0]
    W = A_HEADS * A_DH
    q_spec, k_specs, v_specs, b_spec = _attn_specs((0, 1, 2))
    out_q = pl.BlockSpec((QBLK, LANE), lambda p, m: (m, p))
    out_kv = pl.BlockSpec((S, LANE), lambda p, m: (0, p))
    ne = len(exchange)

    def body(q_ref, k0, k1, k2, v0, v1, v2, b_ref, do_ref, *rest):
        srcs, (dq_ref, dk_ref, dv_ref, db_ref), eouts, sems = rest[:ne], rest[ne:ne + 4], rest[ne + 4:2 * ne + 4], rest[2 * ne + 4:]
        done = _carry(_chips_phases(srcs, eouts, *sems), *_grid_ends(S // QBLK)) if ne else None
        m = pl.program_id(1)

        @pl.when(m == 0)
        def _():
            dk_ref[...] = jnp.zeros_like(dk_ref)
            dv_ref[...] = jnp.zeros_like(dv_ref)
            db_ref[...] = jnp.zeros_like(db_ref)

        q = q_ref[...].astype(BF16)
        k = jnp.concatenate([k0[...], k1[...], k2[...]], axis=0).astype(BF16)
        v = jnp.concatenate([v0[...], v1[...], v2[...]], axis=0).astype(BF16)
        do = do_ref[...]
        front = _front_mask(m)
        heads = range(HEADS_PER_SLAB)
        qh = [_head_lanes(q, hh) for hh in heads]
        doh = [_head_lanes(do, hh) for hh in heads]
        scores = [_nt(qh[hh], k) for hh in heads]
        dps = [_nt(doh[hh], v) for hh in heads]
        ps = [_attn_probs(scores[hh], b_ref[hh], front) for hh in heads]
        dss = [ps[hh] * (dps[hh] - jnp.sum(ps[hh] * dps[hh], axis=-1, keepdims=True)) for hh in heads]
        for hh in heads:
            db_ref[hh] += dss[hh]
        dsb = [(dss[hh] * (A_DH ** -0.5)).astype(BF16) for hh in heads]
        dqs = [_nn(dsb[hh], k) for hh in heads]
        dk = sum(_tn(dsb[hh], qh[hh]) for hh in heads)
        dv = sum(_tn(ps[hh].astype(BF16), doh[hh]) for hh in heads)
        lane = lax.broadcasted_iota(jnp.int32, (QBLK, LANE), 1)
        dq_ref[...] = jnp.where(lane < A_DH, dqs[0], dqs[1])
        for j in range(SPAN_BLOCKS):
            blk = m - (SPAN_BLOCKS - 1) + j

            @pl.when(blk >= 0)
            def _():
                off = pl.multiple_of(blk * QBLK, QBLK)
                dk_ref[pl.ds(off, QBLK), :] += dk[j * QBLK:(j + 1) * QBLK]
                dv_ref[pl.ds(off, QBLK), :] += dv[j * QBLK:(j + 1) * QBLK]
        if ne:
            done()

    res = pl.pallas_call(
        body, name=name, grid=(N_SLABS, S // QBLK),
        in_specs=[q_spec] + k_specs + v_specs + [b_spec, pl.BlockSpec((QBLK, LANE), lambda p, m: (m, p))] + [HBM] * ne,
        out_specs=[out_q, out_kv, out_kv, b_spec] + [HBM] * ne,
        out_shape=[jax.ShapeDtypeStruct((S, W), F32)] * 3 + [jax.ShapeDtypeStruct((A_HEADS, QBLK, KSPAN), F32)]
        + _chips_out_shapes(exchange),
        scratch_shapes=_chips_scratch(ne) if ne else [], compiler_params=_arb(2),
    )(proj, proj, proj, proj, proj, proj, proj, big, dya, *exchange)
    return tuple(res[:4]) + (list(res[4:]),)


NREL_PAD = 3 * LANE
SKEW_W = 1024


def _rel_table_grad(dbig, name):
    H, R, C = dbig.shape

    def body(d_ref, o_ref):
        x = jnp.concatenate([d_ref[...], jnp.zeros((R, SKEW_W - C), F32)], axis=1)
        row = lax.broadcasted_iota(jnp.int32, (R, SKEW_W), 0)
        for b in range(R.bit_length() - 1):
            x = jnp.where(((row >> b) & 1) == 1, pltpu.roll(x, SKEW_W - (1 << b), 1), x)
        e = jnp.sum(x, axis=0, keepdims=True)
        xi = lax.broadcasted_iota(jnp.int32, (SKEW_W, NREL_PAD), 0)
        r = lax.broadcasted_iota(jnp.int32, (SKEW_W, NREL_PAD), 1)
        diag = jnp.where(xi < C, xi, xi - SKEW_W)
        rel = jnp.clip(A_PAST * CHUNK - diag, -A_MAX_REL, A_MAX_REL) + A_MAX_REL
        o_ref[...] = _nn(e, jnp.where(rel == r, 1.0, 0.0).astype(F32), HI)

    return pl.pallas_call(
        body, name=name, grid=(H,), in_specs=[pl.BlockSpec((None, R, C), lambda h: (h, 0, 0))],
        out_specs=pl.BlockSpec((None, 1, NREL_PAD), lambda h: (h, 0, 0)),
        out_shape=jax.ShapeDtypeStruct((H, 1, NREL_PAD), F32), compiler_params=_par(1),
    )(dbig)


def _chunk_cumsum_matrix(n, reverse):
    j = lax.broadcasted_iota(jnp.int32, (n, n), 0)
    i = lax.broadcasted_iota(jnp.int32, (n, n), 1)
    same = (j // CHUNK) == (i // CHUNK)
    return jnp.where(same & ((j >= i) if reverse else (j <= i)), 1.0, 0.0).astype(F32)


def _gdn_gates_fwd(proj, alog, dtb, name):
    Hh, S = B_HEADS, proj.shape[0]
    tl = _pick(S, 512)
    row = pl.BlockSpec((Hh, tl), lambda i: (0, i))
    col = pl.BlockSpec((Hh, 1), lambda i: (0, 0))

    def body(ba_ref, al_ref, dt_ref, beta_ref, gam_ref, b_ref, a_ref, t_ref):
        t_ref[...] = ba_ref[...].T
        b = t_ref[0:Hh, :]
        a = t_ref[Hh:2 * Hh, :]
        z = a + dt_ref[...]
        sp = jnp.maximum(z, 0.0) + jnp.log(1.0 + jnp.exp(-jnp.abs(z)))
        g = -jnp.exp(al_ref[...]) * sp
        beta_ref[...] = _sigmoid(b)
        gam_ref[...] = _nn(g, _chunk_cumsum_matrix(tl, False), HI)
        b_ref[...] = b
        a_ref[...] = a

    return pl.pallas_call(
        body, name=name, grid=(S // tl,), in_specs=[pl.BlockSpec((tl, LANE), lambda i: (i, P_BA // LANE)), col, col],
        out_specs=[row] * 4, out_shape=[jax.ShapeDtypeStruct((Hh, S), F32)] * 4,
        scratch_shapes=[pltpu.VMEM((LANE, tl), F32)], compiler_params=_par(1),
    )(proj, alog, dtb)


def _gdn_gates_bwd(dbeta, dgam_a, dgam_b, b_t, a_t, alog, dtb, name):
    Hh, S = b_t.shape
    tl = _pick(S, 512)
    row = pl.BlockSpec((Hh, tl), lambda i: (0, i))
    col = pl.BlockSpec((Hh, 1), lambda i: (0, 0))
    accs = pl.BlockSpec((Hh, LANE), lambda i: (0, 0))

    def body(dbeta_ref, dga_ref, dgb_ref, b_ref, a_ref, al_ref, dt_ref, db_ref, da_ref, dal_ref, ddt_ref):
        i = pl.program_id(0)
        z = a_ref[...] + dt_ref[...]
        sp = jnp.maximum(z, 0.0) + jnp.log(1.0 + jnp.exp(-jnp.abs(z)))
        ea = jnp.exp(al_ref[...])
        dg = _nn(dga_ref[...] + dgb_ref[...], _chunk_cumsum_matrix(tl, True), HI)
        da = dg * (-ea) * _sigmoid(z)
        beta = _sigmoid(b_ref[...])
        db_ref[...] = dbeta_ref[...] * beta * (1.0 - beta)
        da_ref[...] = da
        _acc(dal_ref, jnp.broadcast_to(jnp.sum(dg * (-ea * sp), axis=1, keepdims=True), (Hh, LANE)), i)
        _acc(ddt_ref, jnp.broadcast_to(jnp.sum(da, axis=1, keepdims=True), (Hh, LANE)), i)

    return pl.pallas_call(
        body, name=name, grid=(S // tl,), in_specs=[row] * 5 + [col, col], out_specs=[row, row, accs, accs],
        out_shape=[jax.ShapeDtypeStruct((Hh, S), F32)] * 2 + [jax.ShapeDtypeStruct((Hh, LANE), F32)] * 2,
        compiler_params=_arb(1),
    )(dbeta, dgam_a, dgam_b, b_t, a_t, alog, dtb)


HALO = 8


def _conv_silu(xx_ref, w_ref, tm):
    y = w_ref[0:1, :] * xx_ref[pl.ds(HALO - CONV_K + 1, tm), :]
    for j in range(1, CONV_K):
        y = y + w_ref[j:j + 1, :] * xx_ref[pl.ds(HALO - CONV_K + 1 + j, tm), :]
    return y, y * _sigmoid(y)


def _fill_prev_halo(xx_ref, x_ref, prev_ref, i, tm):
    xx_ref[pl.ds(HALO, tm), :] = x_ref[...]

    @pl.when(i == 0)
    def _():
        xx_ref[pl.ds(0, HALO), :] = jnp.zeros((HALO, xx_ref.shape[1]), F32)

    @pl.when(i != 0)
    def _():
        xx_ref[pl.ds(0, HALO), :] = prev_ref[...]


def _gdn_pre_specs(tm, C, colblk):
    cur = pl.BlockSpec((tm, C), lambda i: (i, colblk))
    prev = pl.BlockSpec((HALO, C), lambda i: (jnp.maximum(i * (tm // HALO) - 1, 0), colblk))
    return cur, prev


def _gdn_pre_fwd(proj, wconv, name):
    S = proj.shape[0]
    C = 3 * B_HEADS * B_DH
    W = B_HEADS * B_DH
    tm = _pick(S, 256, 8)
    cur, prev = _gdn_pre_specs(tm, C, P_QKVB // C)

    def body(x_ref, prev_ref, w_ref, q_ref, k_ref, v_ref, xx_ref):
        i = pl.program_id(0)
        _fill_prev_halo(xx_ref, x_ref, prev_ref, i, tm)
        _, sl = _conv_silu(xx_ref, w_ref, tm)
        for h in range(B_HEADS):
            hs = slice(h * B_DH, (h + 1) * B_DH)
            q = sl[:, h * B_DH:(h + 1) * B_DH]
            k = sl[:, W + h * B_DH:W + (h + 1) * B_DH]
            q_ref[:, hs] = q * (lax.rsqrt(jnp.sum(q * q, axis=-1, keepdims=True) + EPS) * (B_DH ** -0.5))
            k_ref[:, hs] = k * lax.rsqrt(jnp.sum(k * k, axis=-1, keepdims=True) + EPS)
        v_ref[...] = sl[:, 2 * W:]

    return pl.pallas_call(
        body, name=name, grid=(S // tm,), in_specs=[cur, prev, pl.BlockSpec((CONV_K, C), lambda i: (0, 0))],
        out_specs=[_rows(tm, W)] * 3, out_shape=[jax.ShapeDtypeStruct((S, W), F32)] * 3,
        scratch_shapes=[pltpu.VMEM((HALO + tm, C), F32)], compiler_params=_par(1),
    )(proj, proj, wconv)


def _conv_rows(xx_ref, w_ref, start, n):
    base = HALO - CONV_K + 1 + start
    y = w_ref[0:1, :] * xx_ref[pl.ds(base, n), :]
    for j in range(1, CONV_K):
        y = y + w_ref[j:j + 1, :] * xx_ref[pl.ds(base + j, n), :]
    return y


def _pre_dy(y, dq, dk, dv):
    W = B_HEADS * B_DH
    sg = _sigmoid(y)
    sl = y * sg
    dsilu = sg * (1.0 + y * (1.0 - sg))
    parts = []
    for base, d_all, c in ((0, dq, B_DH ** -0.5), (W, dk, 1.0)):
        for h in range(B_HEADS):
            lo = base + h * B_DH
            t = sl[:, lo:lo + B_DH]
            d = d_all[:, h * B_DH:(h + 1) * B_DH]
            r = lax.rsqrt(jnp.sum(t * t, axis=-1, keepdims=True) + EPS)
            parts.append((c * r) * (d - t * (r * r) * jnp.sum(d * t, axis=-1, keepdims=True)) * dsilu[:, lo:lo + B_DH])
    parts.append(dv * dsilu[:, 2 * W:])
    return jnp.concatenate(parts, axis=1)


def _gdn_pre_bwd(proj, wconv, dqn, dkn, dv, dproj, name):
    S = proj.shape[0]
    C = 3 * B_HEADS * B_DH
    W = B_HEADS * B_DH
    tm = _pick(S, 256, 16)
    nt_ = S // tm
    cur, prev = _gdn_pre_specs(tm, C, P_QKVB // C)
    after = lambda i: jnp.minimum((i + 1) * (tm // HALO), S // HALO - 1)
    next_x = pl.BlockSpec((HALO, C), lambda i: (after(i), P_QKVB // C))
    next_d = pl.BlockSpec((HALO, W), lambda i: (after(i), 0))

    def body(x_ref, prev_ref, nx_ref, w_ref, dq_ref, dk_ref, dv_ref, ndq_ref, ndk_ref, ndv_ref, _, dx_ref, dw_ref,
             xx_ref, dd_ref):
        i = pl.program_id(0)
        _fill_prev_halo(xx_ref, x_ref, prev_ref, i, tm)
        xx_ref[pl.ds(HALO + tm, HALO), :] = nx_ref[...]
        dyv = _pre_dy(_conv_rows(xx_ref, w_ref, 0, tm), dq_ref[...], dk_ref[...], dv_ref[...])
        dd_ref[pl.ds(0, tm), :] = dyv

        @pl.when(i == nt_ - 1)
        def _():
            dd_ref[pl.ds(tm, HALO), :] = jnp.zeros((HALO, C), F32)

        @pl.when(i != nt_ - 1)
        def _():
            dd_ref[pl.ds(tm, HALO), :] = _pre_dy(_conv_rows(xx_ref, w_ref, tm, HALO), ndq_ref[...], ndk_ref[...],
                                                 ndv_ref[...])

        dx = w_ref[0:1, :] * dd_ref[pl.ds(CONV_K - 1, tm), :]
        for j in range(1, CONV_K):
            dx = dx + w_ref[j:j + 1, :] * dd_ref[pl.ds(CONV_K - 1 - j, tm), :]
        dx_ref[...] = dx.astype(BF16)
        dw = jnp.concatenate(
            [jnp.sum(dyv * xx_ref[pl.ds(HALO - CONV_K + 1 + j, tm), :], axis=0, keepdims=True) for j in range(CONV_K)],
            axis=0)
        _acc(dw_ref, dw, i)

    return pl.pallas_call(
        body, name=name, grid=(nt_,),
        in_specs=[cur, prev, next_x, pl.BlockSpec((CONV_K, C), lambda i: (0, 0))] + [_rows(tm, W)] * 3 + [next_d] * 3
        + [HBM],
        out_specs=[_rows(tm, C, P_QKVB // C), pl.BlockSpec((CONV_K, C), lambda i: (0, 0))],
        out_shape=[jax.ShapeDtypeStruct(dproj.shape, BF16), jax.ShapeDtypeStruct((CONV_K, C), F32)],
        input_output_aliases={10: 0},
        scratch_shapes=[pltpu.VMEM((HALO + tm + HALO, C), F32), pltpu.VMEM((tm + HALO, C), F32)],
        compiler_params=_arb(1),
    )(proj, proj, proj, wconv, dqn, dkn, dv, dqn, dkn, dv, dproj)


def _chunk_masks():
    row = lax.broadcasted_iota(jnp.int32, (CHUNK, CHUNK), 0)
    col = lax.broadcasted_iota(jnp.int32, (CHUNK, CHUNK), 1)
    return row >= col, row > col


def _chunk_local(q, k, vv, bc, gc, gr, tri):
    dm = jnp.where(tri, jnp.exp(jnp.where(tri, gc - gr, 0.0)), 0.0)
    kk = _bnt(k, k)
    glast = gr[..., CHUNK - 1:CHUNK]
    ep = jnp.exp(gc)
    em = jnp.exp(glast - gc)
    el = jnp.exp(glast)
    return dm, kk, ep, em, el, vv * bc, k * (bc * ep)


def _unit_lower_inverse(low):
    row = lax.broadcasted_iota(jnp.int32, (CHUNK, CHUNK), 0)
    col = lax.broadcasted_iota(jnp.int32, (CHUNK, CHUNK), 1)
    p = -low
    t = jnp.where(row == col, 1.0, 0.0).astype(F32) + p
    steps = CHUNK.bit_length() - 2
    for _ in range(steps):
        p = _nn(p, p, SOLVE_PREC)
        t = t + _nn(t, p, SOLVE_PREC)
    return t


GROUP = 8


LATE = 7


def _carry(phases, first, middle, last):
    if len(phases) == 3:
        pl.when(first)(phases[0])
        pl.when(middle)(phases[1])
        return lambda: pl.when(last)(phases[2])
    pl.when(first)(phases[0])
    return lambda: pl.when(last)(phases[1])


def _pairs(nchunks):
    return [(c, h) for c in range(nchunks) for h in range(B_HEADS)]


def _tok(c):
    return slice(c * CHUNK, (c + 1) * CHUNK)


def _head(h):
    return slice(h * B_DH, (h + 1) * B_DH)


def _stack_tokens(ref, nchunks):
    return jnp.stack([ref[_tok(c), _head(h)] for c, h in _pairs(nchunks)])


def _stack_cols(ref, nchunks):
    per_chunk = [ref[c] for c in range(nchunks)] if len(ref.shape) == 3 else [ref[...]]
    return jnp.stack([per_chunk[c][:, h:h + 1] for c, h in _pairs(nchunks)])


def _stack_rows(ref, nchunks):
    if len(ref.shape) == 3:
        return jnp.stack([ref[c, h:h + 1, :] for c, h in _pairs(nchunks)])
    return jnp.stack([ref[h:h + 1, :] for _, h in _pairs(1)])


def _gdn_group_specs(ng_steps, W):
    tok = pl.BlockSpec((GROUP * CHUNK, W), lambda i: (i, 0))
    colv = pl.BlockSpec((GROUP, CHUNK, B_HEADS), lambda i: (i, 0, 0))
    rowv = pl.BlockSpec((GROUP, B_HEADS, CHUNK), lambda i: (i, 0, 0))
    mat = pl.BlockSpec((GROUP, B_HEADS, CHUNK, CHUNK), lambda i: (i, 0, 0, 0))
    return tok, colv, rowv, mat


def _gdn_local_fwd(qn, kn, v, bcol, gcol, grow, name, gather=None):
    S, Wd = qn.shape
    nc = S // CHUNK
    steps = nc // GROUP
    tok, colv, rowv, mat = _gdn_group_specs(steps, Wd)
    shards, layer = gather if gather is not None else ((), None)
    ng = len(shards)

    def body(q_ref, k_ref, v_ref, bc_ref, gc_ref, gr_ref, *rest):
        srcs, (t_ref, a_ref, u_ref, w_ref), gouts, sems = rest[:ng], rest[ng:ng + 4], rest[ng + 4:2 * ng + 4], rest[2 * ng + 4:]
        i = pl.program_id(0)
        done = _carry(_gather_phases(layer, srcs, gouts, *sems), i == 0, i == LATE * steps // 8, i == steps - 1) if ng else None
        tri, strict = _chunk_masks()
        q, k, vv = (_stack_tokens(r, GROUP) for r in (q_ref, k_ref, v_ref))
        bc, gc, gr = _stack_cols(bc_ref, GROUP), _stack_cols(gc_ref, GROUP), _stack_rows(gr_ref, GROUP)
        dm, kk, ep, em, el, vb, kb = _chunk_local(q, k, vv, bc, gc, gr, tri)
        t = _unit_lower_inverse(jnp.where(strict, bc * kk * dm, 0.0))
        a = _bnt(q, k) * dm
        u = _nn(t, vb, SOLVE_PREC)
        w = _nn(t, kb, SOLVE_PREC)
        for n, (c, h) in enumerate(_pairs(GROUP)):
            t_ref[c, h] = t[n]
            a_ref[c, h] = a[n]
            u_ref[_tok(c), _head(h)] = u[n]
            w_ref[_tok(c), _head(h)] = w[n]
        if ng:
            done()

    res = pl.pallas_call(
        body, name=name, grid=(steps,), in_specs=[tok, tok, tok, colv, colv, rowv] + [HBM] * ng,
        out_specs=[mat, mat, tok, tok] + [HBM] * ng,
        out_shape=[jax.ShapeDtypeStruct((nc, B_HEADS, CHUNK, CHUNK), F32)] * 2 + [jax.ShapeDtypeStruct((S, Wd), F32)] * 2
        + _gather_out_shapes(shards),
        scratch_shapes=_gather_scratch(ng) if ng else [], compiler_params=_arb(1),
    )(qn, kn, v, bcol, gcol, grow, *shards)
    return res[0], res[1], res[2], res[3], list(res[4:])


def _scan_decays(gc, gr):
    glast = gr[..., CHUNK - 1:CHUNK]
    return jnp.exp(gc), jnp.exp(glast - gc), jnp.exp(glast)


SCAN = 8


def _gdn_scan_specs(steps, rev):
    idx = (lambda i: steps - 1 - i) if rev else (lambda i: i)
    W = B_HEADS * B_DH
    tok = pl.BlockSpec((SCAN * CHUNK, W), lambda i: (idx(i), 0))
    colv = pl.BlockSpec((SCAN, CHUNK, B_HEADS), lambda i: (idx(i), 0, 0))
    rowv = pl.BlockSpec((SCAN, B_HEADS, CHUNK), lambda i: (idx(i), 0, 0))
    mat = pl.BlockSpec((SCAN, B_HEADS, CHUNK, CHUNK), lambda i: (idx(i), 0, 0, 0))
    smat = pl.BlockSpec((SCAN, B_HEADS, B_DH, B_DH), lambda i: (idx(i), 0, 0, 0))
    return tok, colv, rowv, mat, smat


def _chunk_rows(ref, c):
    return ref.at[pl.ds(c * CHUNK, CHUNK)]


def _gdn_scan_fwd(qn, kn, u, w, a, gcol, grow, name, gather=None):
    S, Wd = qn.shape
    nc = S // CHUNK
    steps = nc // SCAN
    tok, colv, rowv, mat, smat = _gdn_scan_specs(steps, False)
    shards, layer = gather if gather is not None else ((), None)
    ng = len(shards)

    def body(q_ref, k_ref, u_ref, w_ref, a_ref, gc_ref, gr_ref, *rest):
        srcs, (o_ref, sh_ref), gouts = rest[:ng], rest[ng:ng + 2], rest[ng + 2:2 * ng + 2]
        st_ref, sems = rest[2 * ng + 2], rest[2 * ng + 3:]
        i = pl.program_id(0)
        done = (_carry(_gather_phases(layer, srcs, gouts, *sems), i == 0, i == LATE * steps // 8, i == steps - 1)
                if ng else None)

        @pl.when(i == 0)
        def _():
            st_ref[...] = jnp.zeros_like(st_ref)

        for c in range(SCAN):
            ep, em, el = _scan_decays(_stack_cols(gc_ref.at[c], 1), _stack_rows(gr_ref.at[c], 1))
            q, k, u, w = (_stack_tokens(_chunk_rows(r, c), 1) for r in (q_ref, k_ref, u_ref, w_ref))
            s0 = st_ref[...]
            ut = u - _bnn(w, s0)
            o = _bnn(q * ep, s0) + _bnn(a_ref[c], ut)
            st_ref[...] = el * s0 + _btn(k * em, ut)
            sh_ref[c] = s0
            for h in range(B_HEADS):
                o_ref[_tok(c), _head(h)] = o[h]
        if ng:
            done()

    res = pl.pallas_call(
        body, name=name, grid=(steps,), in_specs=[tok, tok, tok, tok, mat, colv, rowv] + [HBM] * ng,
        out_specs=[tok, smat] + [HBM] * ng,
        out_shape=[jax.ShapeDtypeStruct((S, Wd), F32), jax.ShapeDtypeStruct((nc, B_HEADS, B_DH, B_DH), F32)]
        + _gather_out_shapes(shards),
        scratch_shapes=[pltpu.VMEM((B_HEADS, B_DH, B_DH), F32)] + (_gather_scratch(ng) if ng else []),
        compiler_params=_arb(1),
    )(qn, kn, u, w, a, gcol, grow, *shards)
    return res[0], res[1], list(res[2:])


def _gdn_scan_bwd(qn, kn, u, w, a, gcol, grow, ssave, do, name):
    S, Wd = qn.shape
    nc = S // CHUNK
    steps = nc // SCAN
    tok, colv, rowv, mat, smat = _gdn_scan_specs(steps, True)

    def body(q_ref, k_ref, u_ref, w_ref, a_ref, gc_ref, gr_ref, sh_ref, do_ref,
             du_ref, dw_ref, dqd_ref, dkd_ref, da_ref, dgl_ref, ds_ref):
        i = pl.program_id(0)

        @pl.when(i == 0)
        def _():
            ds_ref[...] = jnp.zeros_like(ds_ref)

        tri, _ = _chunk_masks()
        sub4 = lax.broadcasted_iota(jnp.int32, (B_HEADS, CHUNK), 0)
        lane_last = lax.broadcasted_iota(jnp.int32, (1, CHUNK), 1) == CHUNK - 1
        for c in reversed(range(SCAN)):
            ep, em, el = _scan_decays(_stack_cols(gc_ref.at[c], 1), _stack_rows(gr_ref.at[c], 1))
            q, k, u, w, dout = (_stack_tokens(_chunk_rows(r, c), 1) for r in (q_ref, k_ref, u_ref, w_ref, do_ref))
            s0 = sh_ref[c]
            ds = ds_ref[...]
            ut = u - _bnn(w, s0)
            dut = _btn(a_ref[c], dout) + _bnn(k * em, ds)
            ds_ref[...] = el * ds + _btn(q * ep, dout) - _btn(w, dut)
            dw = -_bnt(dut, s0)
            dqd = _bnt(dout, s0)
            dkd = _bnt(ut, ds)
            da_ref[c] = jnp.where(tri, _bnt(dout, ut), 0.0)
            d_el = jnp.sum(jnp.sum(s0 * ds, axis=-1, keepdims=True), axis=-2, keepdims=True)
            last = d_el * el
            dgl_acc = jnp.zeros((B_HEADS, CHUNK), F32)
            for h in range(B_HEADS):
                du_ref[_tok(c), _head(h)] = dut[h]
                dw_ref[_tok(c), _head(h)] = dw[h]
                dqd_ref[_tok(c), _head(h)] = dqd[h]
                dkd_ref[_tok(c), _head(h)] = dkd[h]
                dgl_acc = jnp.where(sub4 == h, jnp.where(lane_last, last[h], 0.0), dgl_acc)
            dgl_ref[c] = dgl_acc

    return pl.pallas_call(
        body, name=name, grid=(steps,), in_specs=[tok, tok, tok, tok, mat, colv, rowv, smat, tok],
        out_specs=[tok, tok, tok, tok, mat, rowv],
        out_shape=[jax.ShapeDtypeStruct((S, Wd), F32)] * 4 + [jax.ShapeDtypeStruct((nc, B_HEADS, CHUNK, CHUNK), F32),
                                                             jax.ShapeDtypeStruct((nc, B_HEADS, CHUNK), F32)],
        scratch_shapes=[pltpu.VMEM((B_HEADS, B_DH, B_DH), F32)], compiler_params=_arb(1),
    )(qn, kn, u, w, a, gcol, grow, ssave, do)


def _gdn_local_bwd(qn, kn, v, bcol, gcol, grow, tsave, du, dw, dqd, dkd, da, dgl, name, exchange=()):
    S, Wd = qn.shape
    nc = S // CHUNK
    steps = nc // GROUP
    tok, colv, rowv, mat = _gdn_group_specs(steps, Wd)
    ne = len(exchange)

    def body(q_ref, k_ref, v_ref, bc_ref, gc_ref, gr_ref, t_ref, du_ref, dw_ref, dqd_ref, dkd_ref, da_ref, dgl_ref, *rest):
        srcs, (dq_ref, dk_ref, dv_ref, dbc_ref, dgc_ref, dgr_ref) = rest[:ne], rest[ne:ne + 6]
        eouts, sems = rest[ne + 6:2 * ne + 6], rest[2 * ne + 6:]
        i = pl.program_id(0)
        done = _carry(_chips_phases(srcs, eouts, *sems), i == 0, None, i == steps - 1) if ne else None
        tri, strict = _chunk_masks()
        lane4 = lax.broadcasted_iota(jnp.int32, (CHUNK, B_HEADS), 1)
        sub4 = lax.broadcasted_iota(jnp.int32, (B_HEADS, CHUNK), 0)
        lane_last = lax.broadcasted_iota(jnp.int32, (1, CHUNK), 1) == CHUNK - 1
        q, k, vv, dut, dwv, dqd, dkd = (_stack_tokens(r, GROUP)
                                        for r in (q_ref, k_ref, v_ref, du_ref, dw_ref, dqd_ref, dkd_ref))
        bc, gc, gr = _stack_cols(bc_ref, GROUP), _stack_cols(gc_ref, GROUP), _stack_rows(gr_ref, GROUP)
        dm, kk, ep, em, el, vb, kb = _chunk_local(q, k, vv, bc, gc, gr, tri)
        t = jnp.stack([t_ref[c, h] for c, h in _pairs(GROUP)])
        dav = jnp.stack([da_ref[c, h] for c, h in _pairs(GROUP)])
        qk = _bnt(q, k)
        dt = _nt(dut, vb, SOLVE_PREC) + _nt(dwv, kb, SOLVE_PREC)
        dvb = _tn(t, dut, SOLVE_PREC)
        dkb = _tn(t, dwv, SOLVE_PREC)
        dl = jnp.where(strict, -_tn(t, _nt(dt, t, SOLVE_PREC), SOLVE_PREC), 0.0)
        g1 = dl * dm
        dkb_k = jnp.sum(dkb * k, axis=-1, keepdims=True)
        dbeta = jnp.sum(g1 * kk, axis=-1, keepdims=True) + jnp.sum(dvb * vv, axis=-1, keepdims=True) + dkb_k * ep
        dkk = g1 * bc
        ddm = dl * (bc * kk) + dav * qk
        dqk = dav * dm
        dq = _bnn(dqk, k) + dqd * ep
        dk = _btn(dqk, q) + _bnn(dkk, k) + _btn(dkk, k) + dkb * (bc * ep) + dkd * em
        dv = dvb * bc
        dep = dkb_k * bc + jnp.sum(dqd * q, axis=-1, keepdims=True)
        dem = jnp.sum(dkd * k, axis=-1, keepdims=True)
        mm = ddm * dm
        dgam_c = jnp.sum(mm, axis=-1, keepdims=True) + dep * ep - dem * em
        dglast = jnp.sum(dem * em, axis=-2, keepdims=True)
        dgam_r = -jnp.sum(mm, axis=-2, keepdims=True) + jnp.where(lane_last, dglast, 0.0)
        for c in range(GROUP):
            dbc_acc = jnp.zeros((CHUNK, B_HEADS), F32)
            dgc_acc = jnp.zeros((CHUNK, B_HEADS), F32)
            dgr_acc = jnp.zeros((B_HEADS, CHUNK), F32)
            for h in range(B_HEADS):
                n = c * B_HEADS + h
                dq_ref[_tok(c), _head(h)] = dq[n]
                dk_ref[_tok(c), _head(h)] = dk[n]
                dv_ref[_tok(c), _head(h)] = dv[n]
                dbc_acc = jnp.where(lane4 == h, dbeta[n], dbc_acc)
                dgc_acc = jnp.where(lane4 == h, dgam_c[n], dgc_acc)
                dgr_acc = jnp.where(sub4 == h, dgam_r[n], dgr_acc)
            dbc_ref[c] = dbc_acc
            dgc_ref[c] = dgc_acc
            dgr_ref[c] = dgr_acc + dgl_ref[c]
        if ne:
            done()

    res = pl.pallas_call(
        body, name=name, grid=(steps,),
        in_specs=[tok, tok, tok, colv, colv, rowv, mat, tok, tok, tok, tok, mat, rowv] + [HBM] * ne,
        out_specs=[tok, tok, tok, colv, colv, rowv] + [HBM] * ne,
        out_shape=[jax.ShapeDtypeStruct((S, Wd), F32)] * 3
        + [jax.ShapeDtypeStruct((nc, CHUNK, B_HEADS), F32)] * 2 + [jax.ShapeDtypeStruct((nc, B_HEADS, CHUNK), F32)]
        + _chips_out_shapes(exchange),
        scratch_shapes=_chips_scratch(ne) if ne else [], compiler_params=_arb(1),
    )(qn, kn, v, bcol, gcol, grow, tsave, du, dw, dqd, dkd, da, dgl, *exchange)
    return tuple(res[:6]) + (list(res[6:]),)


def _gdn_post_fwd(o, proj, ng, name):
    S, W = o.shape
    tm = _pick(S, 512, 8)

    def body(o_ref, z_ref, g_ref, y_ref):
        gv = g_ref[...]
        for h in range(B_HEADS):
            hs = slice(h * B_DH, (h + 1) * B_DH)
            oh = o_ref[:, hs]
            z = z_ref[:, hs]
            r = lax.rsqrt(jnp.mean(oh * oh, axis=-1, keepdims=True) + EPS)
            y_ref[:, hs] = (oh * r * gv * (z * _sigmoid(z))).astype(BF16)

    return pl.pallas_call(
        body, name=name, grid=(S // tm,), in_specs=[_rows(tm, W), _rows(tm, W, P_Z // W), _vec(B_DH)],
        out_specs=_rows(tm, W), out_shape=jax.ShapeDtypeStruct((S, W), BF16), compiler_params=_par(1),
    )(o, proj, ng)


def _gdn_post_bwd(dy, o, proj, ng, dproj, name):
    S, W = o.shape
    tm = _pick(S, 512, 16)

    def body(dy_ref, o_ref, z_ref, g_ref, _, do_ref, dz_ref, dg_ref):
        i = pl.program_id(0)
        gv = g_ref[...]
        dg = jnp.zeros((1, B_DH), F32)
        for h in range(B_HEADS):
            hs = slice(h * B_DH, (h + 1) * B_DH)
            oh = o_ref[:, hs]
            z = z_ref[:, hs]
            d = dy_ref[:, hs]
            r = lax.rsqrt(jnp.mean(oh * oh, axis=-1, keepdims=True) + EPS)
            n = oh * r
            sg = _sigmoid(z)
            sz = z * sg
            dn = d * gv * sz
            dg = dg + jnp.sum(d * n * sz, axis=0, keepdims=True)
            dz_ref[:, hs] = (d * n * gv * (sg * (1.0 + z * (1.0 - sg)))).astype(BF16)
            do_ref[:, hs] = r * (dn - n * jnp.mean(dn * n, axis=-1, keepdims=True))
        _acc(dg_ref, dg, i)

    return pl.pallas_call(
        body, name=name, grid=(S // tm,),
        in_specs=[_rows(tm, W), _rows(tm, W), _rows(tm, W, P_Z // W), _vec(B_DH), HBM],
        out_specs=[_rows(tm, W), _rows(tm, W, P_Z // W), _vec(B_DH)],
        out_shape=[jax.ShapeDtypeStruct((S, W), F32), jax.ShapeDtypeStruct(dproj.shape, BF16),
                   jax.ShapeDtypeStruct((1, B_DH), F32)],
        input_output_aliases={4: 1}, compiler_params=_arb(1),
    )(dy, o, proj, ng, dproj)


def _ada_mod(c_all, w_ada, b_shard, name):
    L, D, Ns = w_ada.shape
    B = c_all.shape[0]

    def body(c_ref, w_ref, b_ref, o_ref):
        cv = c_ref[...]
        cond = (cv * _sigmoid(cv)).astype(BF16)
        o_ref[...] = _nn(cond, w_ref[...].astype(BF16)) + b_ref[...]

    return pl.pallas_call(
        body, name=name, grid=(L,),
        in_specs=[pl.BlockSpec((B, D), lambda l: (0, 0)), pl.BlockSpec((None, D, Ns), lambda l: (l, 0, 0)),
                  pl.BlockSpec((None, 1, Ns), lambda l: (l, 0, 0))],
        out_specs=pl.BlockSpec((None, B, Ns), lambda l: (l, 0, 0)),
        out_shape=jax.ShapeDtypeStruct((L, B, Ns), F32), compiler_params=_par(1),
    )(c_all, w_ada, b_shard)


def _ada_wgrad(c_all, dmod, name):
    L, B, Ns = dmod.shape
    D = c_all.shape[1]

    def body(c_ref, d_ref, o_ref):
        cv = c_ref[...]
        cond = (cv * _sigmoid(cv)).astype(BF16)
        o_ref[...] = _tn(cond, d_ref[...].astype(BF16))

    return pl.pallas_call(
        body, name=name, grid=(L,),
        in_specs=[pl.BlockSpec((B, D), lambda l: (0, 0)), pl.BlockSpec((None, B, Ns), lambda l: (l, 0, 0))],
        out_specs=pl.BlockSpec((None, D, Ns), lambda l: (l, 0, 0)),
        out_shape=jax.ShapeDtypeStruct((L, D, Ns), F32), compiler_params=_par(1),
    )(c_all, dmod)


W_IN_PIECES = ((0, 0, 1410), (1, 0, 1410), (2, 0, 252), (2, 772, 638), (3, 0, 1410), (2, 252, 512), (2, 764, 8))


def _reorder_w_in(w4, name):
    L, _, D, Cs = w4.shape
    tm = _pick(D, 256, 16)
    used = sum(p[2] for p in W_IN_PIECES)

    def body(w_ref, o_ref):
        shard = [w_ref[s] for s in range(4)]
        parts = [shard[s][:, lo:lo + n] for s, lo, n in W_IN_PIECES]
        o_ref[...] = jnp.concatenate(parts + [jnp.zeros((tm, P_END - used), w4.dtype)], axis=1)

    return pl.pallas_call(
        body, name=name, grid=(L, D // tm), in_specs=[pl.BlockSpec((None, 4, tm, Cs), lambda l, i: (l, 0, i, 0))],
        out_specs=pl.BlockSpec((None, tm, P_END), lambda l, i: (l, i, 0)),
        out_shape=jax.ShapeDtypeStruct((L, D, P_END), w4.dtype), compiler_params=_par(2),
    )(w4)


def _restore_w_in(g, name):
    D = g.shape[0]
    tm = _pick(D, 256, 16)

    def body(g_ref, o_ref, ob_ref):
        gv = g_ref[...]
        off = 0
        pieces = {}
        for s, lo, n in W_IN_PIECES:
            pieces.setdefault(s, []).append((lo, gv[:, off:off + n]))
            off += n
        for s, lst in pieces.items():
            lst.sort(key=lambda t: t[0])
            shard = lst[0][1] if len(lst) == 1 else jnp.concatenate([t[1] for t in lst], axis=1)
            o_ref[s] = shard
            ob_ref[s] = shard.astype(BF16)

    spec = pl.BlockSpec((4, tm, W_IN_SHARD), lambda i: (0, i, 0))
    return pl.pallas_call(
        body, name=name, grid=(D // tm,), in_specs=[pl.BlockSpec((tm, P_END), lambda i: (i, 0))],
        out_specs=[spec, spec],
        out_shape=[jax.ShapeDtypeStruct((4, D, W_IN_SHARD), g.dtype), jax.ShapeDtypeStruct((4, D, W_IN_SHARD), BF16)],
        compiler_params=_par(1),
    )(g)


def _adam_update(w, g, m, v):
    mn = ADAM_B1 * m + (1.0 - ADAM_B1) * g
    vn = ADAM_B2 * v + (1.0 - ADAM_B2) * (g * g)
    m_hat = mn / (1.0 - ADAM_B1 ** ADAM_STEP)
    v_hat = vn / (1.0 - ADAM_B2 ** ADAM_STEP)
    return -ADAM_LR * (m_hat / (jnp.sqrt(v_hat) + ADAM_EPS) + ADAM_WD * w), mn, vn


def _adamw(w, g, m, v, name):
    shape = w.shape
    C = shape[-1]
    R = w.size // C
    tm = _pick(R, 512, 8)
    spec = pl.BlockSpec((tm, C), lambda i: (i, 0))

    def body(w_ref, g_ref, m_ref, v_ref, d_ref, mo_ref, vo_ref):
        d_ref[...], mo_ref[...], vo_ref[...] = _adam_update(w_ref[...], g_ref[...], m_ref[...], v_ref[...])

    outs = pl.pallas_call(
        body, name=name, grid=(R // tm,), in_specs=[spec] * 4, out_specs=[spec] * 3,
        out_shape=[jax.ShapeDtypeStruct((R, C), F32)] * 3, compiler_params=_par(1),
    )(*(t.reshape(R, C) for t in (w, g, m, v)))
    return tuple(o.reshape(shape) for o in outs)


def _adamw_lead(w, g, m, v, name, tl):
    A, B, C = w.shape
    spec = pl.BlockSpec((tl, B, C), lambda i: (i, 0, 0))

    def body(w_ref, g_ref, m_ref, v_ref, d_ref, mo_ref, vo_ref):
        d_ref[...], mo_ref[...], vo_ref[...] = _adam_update(w_ref[...], g_ref[...], m_ref[...], v_ref[...])

    return pl.pallas_call(
        body, name=name, grid=(A // tl,), in_specs=[spec] * 4, out_specs=[spec] * 3,
        out_shape=[jax.ShapeDtypeStruct((A, B, C), F32)] * 3, compiler_params=_par(1),
    )(w, g, m, v)


def _adamw_layers(w, gs, m, v, name):
    L, R, C = w.shape
    tm = _pick(R, 256, 8)
    spec = pl.BlockSpec((None, tm, C), lambda l, i: (l, i, 0))
    g_specs = [pl.BlockSpec((tm, C), functools.partial(lambda ll, l, i: (jnp.where(l == ll, i, 0), 0), ll))
               for ll in range(L)]

    def body(w_ref, m_ref, v_ref, *rest):
        g_refs, (go_ref, d_ref, mo_ref, vo_ref) = rest[:L], rest[L:]
        l = pl.program_id(0)
        for ll in range(L):
            @pl.when(l == ll)
            def _():
                g = g_refs[ll][...]
                go_ref[...] = g
                d_ref[...], mo_ref[...], vo_ref[...] = _adam_update(w_ref[...], g, m_ref[...], v_ref[...])

    return pl.pallas_call(
        body, name=name, grid=(L, R // tm), in_specs=[spec] * 3 + g_specs, out_specs=[spec] * 4,
        out_shape=[jax.ShapeDtypeStruct((L, R, C), F32)] * 4, compiler_params=_arb(2),
    )(w, m, v, *gs)


def _pair_sums(a, where, b, name):
    NB, _, R, C = a.shape

    def body(where_ref, a_ref, b_ref, p_ref, own_ref):
        s = a_ref[...] + b_ref[...].astype(F32)
        p_ref[...] = s.astype(BF16)

        @pl.when(pl.program_id(0) == where_ref[1])
        def _():
            own_ref[...] = s

    return pl.pallas_call(
        body, name=name,
        grid_spec=pltpu.PrefetchScalarGridSpec(
            num_scalar_prefetch=1, grid=(NB,),
            in_specs=[pl.BlockSpec((None, None, R, C), lambda k, w: (k, w[0], 0, 0)),
                      pl.BlockSpec((None, R, C), lambda k, w: (k, 0, 0))],
            out_specs=[pl.BlockSpec((None, R, C), lambda k, w: (k, 0, 0)), pl.BlockSpec((R, C), lambda k, w: (0, 0))]),
        out_shape=[jax.ShapeDtypeStruct((NB, R, C), BF16), jax.ShapeDtypeStruct((R, C), F32)],
        compiler_params=_arb(1),
    )(where, a, b)


def _sum_own_and_received(own, recv, where, name):
    R, C = own.shape
    tm = _pick(R, 256, 16)

    def body(where_ref, p_ref, r_ref, o_ref):
        o_ref[...] = ((p_ref[...] + r_ref[0].astype(F32)) + r_ref[1].astype(F32)) + r_ref[2].astype(F32)

    return pl.pallas_call(
        body, name=name,
        grid_spec=pltpu.PrefetchScalarGridSpec(
            num_scalar_prefetch=1, grid=(R // tm,),
            in_specs=[pl.BlockSpec((tm, C), lambda i, w: (i, 0)), pl.BlockSpec((3, tm, C), lambda i, w: (0, i, 0))],
            out_specs=pl.BlockSpec((None, tm, C), lambda i, w: (w[0], i, 0))),
        out_shape=jax.ShapeDtypeStruct((2, R, C), F32), compiler_params=_par(1),
    )(where, own, recv)


def _position():
    return lax.axis_index("x"), lax.axis_index("y"), lax.axis_index("c")


def _other_chips(x, y):
    return [(1 - x, y), (x, 1 - y), (1 - x, 1 - y)]


HBM = pl.BlockSpec(memory_space=pl.ANY)


def _allgather8(blk, name, reduce_rows=None):
    M, N = blk.shape

    def body(x_ref, out_ref, *rest):
        if reduce_rows is None:
            send_sems, recv_sems, local_sem = rest
        else:
            sum_ref, send_sems, recv_sems, local_sem = rest
        x, y, c = _position()
        me, sibling = (x, y, c), (x, y, 1 - c)
        chips = _other_chips(x, y)

        def rows(px, py, pc):
            return out_ref.at[pl.ds((4 * px + 2 * py + pc) * M, M), :]

        def copy(k, block, to, src=None):
            return pltpu.make_async_remote_copy(
                src_ref=rows(*block) if src is None else src, dst_ref=rows(*block),
                send_sem=send_sems.at[k], recv_sem=recv_sems.at[k], device_id=to, device_id_type=MESH)

        mine = pltpu.make_async_copy(x_ref, rows(*me), local_sem)
        mine.start()
        first = [copy(0, me, sibling, src=x_ref)]
        first += [copy(1 + j, me, (*chip, c), src=x_ref) for j, chip in enumerate(chips)]
        for cp in first:
            cp.start()
        passed = [copy(4 + j, (*chip, c), sibling) for j, chip in enumerate(chips)]
        for j, chip in enumerate(chips):
            copy(1 + j, (*chip, c), me).wait_recv()
            passed[j].start()
        copy(0, sibling, me).wait_recv()
        for j, chip in enumerate(chips):
            copy(4 + j, (*chip, 1 - c), me).wait_recv()
        for cp in first + passed:
            cp.wait_send()
        mine.wait()
        if reduce_rows is not None:
            tot = out_ref[pl.ds(0, reduce_rows), :]
            for d in range(1, 8):
                tot = tot + out_ref[pl.ds(d * M, reduce_rows), :]
            sum_ref[...] = tot

    vmem = pl.BlockSpec(memory_space=pltpu.VMEM)
    out_shape = [jax.ShapeDtypeStruct((8 * M, N), blk.dtype)]
    if reduce_rows is not None:
        out_shape.append(jax.ShapeDtypeStruct((reduce_rows, N), blk.dtype))
    res = pl.pallas_call(
        body, name=name, out_shape=out_shape, in_specs=[vmem], out_specs=[vmem] * len(out_shape),
        scratch_shapes=[pltpu.SemaphoreType.DMA((7,)), pltpu.SemaphoreType.DMA((7,)), pltpu.SemaphoreType.DMA],
    )(blk)
    return res[0] if reduce_rows is None else (res[0], res[1])


def _gather_phases(layer, srcs, outs, send_sems, recv_sems, local_sems):
    n = len(srcs)
    x, y, c = _position()
    me, sibling = (x, y, c), (x, y, 1 - c)
    chips = _other_chips(x, y)

    def region(t, px, py, pc):
        return outs[t].at[2 * px + py, pc]

    def copy(t, k, block, to, own=False):
        return pltpu.make_async_remote_copy(
            src_ref=srcs[t].at[layer, c] if own else region(t, *block), dst_ref=region(t, *block),
            send_sem=send_sems.at[7 * t + k], recv_sem=recv_sems.at[7 * t + k], device_id=to, device_id_type=MESH)

    def local(t):
        return pltpu.make_async_copy(srcs[t].at[layer, c], region(t, *me), local_sems.at[t])

    def first(t):
        return [copy(t, 0, me, sibling, own=True)] + [copy(t, 1 + j, me, (*chip, c), own=True)
                                                       for j, chip in enumerate(chips)]

    def start():
        for t in range(n):
            local(t).start()
        for t in range(n):
            for cp in first(t):
                cp.start()

    def forward():
        for j, chip in enumerate(chips):
            for t in range(n):
                copy(t, 1 + j, (*chip, c), me).wait_recv()
                copy(t, 4 + j, (*chip, c), sibling).start()

    def finish():
        for t in range(n):
            copy(t, 0, sibling, me).wait_recv()
        for j, chip in enumerate(chips):
            for t in range(n):
                copy(t, 4 + j, (*chip, 1 - c), me).wait_recv()
        for t in range(n):
            for cp in first(t) + [copy(t, 4 + j, (*chip, c), sibling) for j, chip in enumerate(chips)]:
                cp.wait_send()
            local(t).wait()

    return start, forward, finish


def _gather_scratch(n):
    return [pltpu.SemaphoreType.DMA((7 * n,)), pltpu.SemaphoreType.DMA((7 * n,)), pltpu.SemaphoreType.DMA((n,))]


def _gather_out_shapes(shards):
    return [jax.ShapeDtypeStruct((4,) + s.shape[1:], s.dtype) for s in shards]


def _gather_weights(shards, layer, name):
    n = len(shards)

    def body(*refs):
        start, forward, finish = _gather_phases(layer, refs[:n], refs[n:2 * n], *refs[2 * n:])
        start()
        forward()
        finish()

    return pl.pallas_call(
        body, name=name, out_shape=_gather_out_shapes(shards), in_specs=[HBM] * n, out_specs=[HBM] * n,
        scratch_shapes=_gather_scratch(n),
    )(*shards)


def _sibling_phases(srcs, outs, send_sems, recv_sems):
    x, y, c = _position()
    copies = [pltpu.make_async_remote_copy(
        src_ref=srcs[t].at[k, 1 - c], dst_ref=outs[t].at[k], send_sem=send_sems.at[4 * t + k],
        recv_sem=recv_sems.at[4 * t + k], device_id=(x, y, 1 - c), device_id_type=MESH)
        for t in range(len(srcs)) for k in range(4)]

    def start():
        for cp in copies:
            cp.start()

    def finish():
        for cp in copies:
            cp.wait()

    return start, finish


def _sibling_scratch(n):
    return [pltpu.SemaphoreType.DMA((4 * n,)), pltpu.SemaphoreType.DMA((4 * n,))]


def _sibling_out_shapes(gs):
    return [jax.ShapeDtypeStruct((4,) + g.shape[2:], g.dtype) for g in gs]


def _rs_chips(ps, name):
    n = len(ps)

    def body(*refs):
        start, finish = _chips_phases(refs[:n], refs[n:2 * n], *refs[2 * n:])
        start()
        finish()

    return pl.pallas_call(
        body, name=name, out_shape=_chips_out_shapes(ps), in_specs=[HBM] * n, out_specs=[HBM] * n,
        scratch_shapes=_chips_scratch(n),
    )(*ps)


def _chips_phases(srcs, outs, send_sems, recv_sems):
    x, y, c = _position()
    copies = [pltpu.make_async_remote_copy(
        src_ref=srcs[t].at[2 * px + py], dst_ref=outs[t].at[j], send_sem=send_sems.at[3 * t + j],
        recv_sem=recv_sems.at[3 * t + j], device_id=(px, py, c), device_id_type=MESH)
        for t in range(len(srcs)) for j, (px, py) in enumerate(_other_chips(x, y))]

    def start():
        for cp in copies:
            cp.start()

    def finish():
        for cp in copies:
            cp.wait()

    return start, finish


def _chips_scratch(n):
    return [pltpu.SemaphoreType.DMA((3 * n,)), pltpu.SemaphoreType.DMA((3 * n,))]


def _chips_out_shapes(ps):
    return [jax.ShapeDtypeStruct((3,) + p.shape[1:], p.dtype) for p in ps]


def _rs_pair(hs, name):
    n = len(hs)

    def body(*refs):
        bufs = refs[n:2 * n]
        send_sems, recv_sems = refs[2 * n:]
        x, y, c = _position()

        def copy(t, half):
            return pltpu.make_async_remote_copy(
                src_ref=bufs[t].at[half], dst_ref=bufs[t].at[half], send_sem=send_sems.at[t], recv_sem=recv_sems.at[t],
                device_id=(x, y, 1 - c), device_id_type=MESH)

        for t in range(n):
            copy(t, c).start()
        for t in range(n):
            copy(t, 1 - c).wait_recv()
        for t in range(n):
            copy(t, c).wait_send()

    out_shape = [jax.ShapeDtypeStruct(h.shape, h.dtype) for h in hs]
    return pl.pallas_call(
        body, name=name, out_shape=out_shape, in_specs=[HBM] * n, out_specs=[HBM] * n,
        input_output_aliases={t: t for t in range(n)},
        scratch_shapes=[pltpu.SemaphoreType.DMA((n,)), pltpu.SemaphoreType.DMA((n,))],
    )(*hs)


BIG = ("w_in", "w_branch_a", "w_branch_b", "w_out", "w_ffn_in", "w_ffn_out")
CARRY_ATTN = ["w_in"]
CARRY_LOCAL = ["w_ffn_out"]
CARRY_SCAN = ["w_branch_a", "w_branch_b", "w_out"]
CARRY_GU = ["w_ffn_in"]
CARRY_DATTN = ["w_in", "w_ffn_in"]
CARRY_DLOCAL = ["w_branch_a", "w_branch_b", "w_out", "w_ffn_out"]


def _band_bias(rel_table, name, gather=None):
    L, H, n = rel_table.shape
    tab = jnp.pad(rel_table, ((0, 0), (0, 0), (0, NREL_PAD - n))).reshape(L * H, 1, NREL_PAD)
    band = (A_PAST + 1) * CHUNK

    shards, glayer = gather if gather is not None else ((), None)
    ng = len(shards)

    def body(t_ref, *rest):
        srcs, o_ref, gouts, sems = rest[:ng], rest[ng], rest[ng + 1:2 * ng + 1], rest[2 * ng + 1:]
        i = pl.program_id(0)
        done = (_carry(_gather_phases(glayer, srcs, gouts, *sems), i == 0, i == LATE * (L * H) // 8, i == L * H - 1)
                if ng else None)
        r = lax.broadcasted_iota(jnp.int32, (NREL_PAD, SKEW_W), 0)
        xi = lax.broadcasted_iota(jnp.int32, (NREL_PAD, SKEW_W), 1)
        diag = jnp.where(xi < KSPAN, xi, xi - SKEW_W)
        rel = jnp.clip(A_PAST * CHUNK - diag, -A_MAX_REL, A_MAX_REL) + A_MAX_REL
        e = _nn(t_ref[...], jnp.where(rel == r, 1.0, 0.0).astype(F32), HI)
        x = jnp.broadcast_to(e, (QBLK, SKEW_W))
        row = lax.broadcasted_iota(jnp.int32, (QBLK, SKEW_W), 0)
        for b in range(QBLK.bit_length() - 1):
            x = jnp.where(((row >> b) & 1) == 1, pltpu.roll(x, 1 << b, 1), x)
        x = x[:, :KSPAN]
        first = (lax.broadcasted_iota(jnp.int32, (QBLK, KSPAN), 0) // CHUNK) * CHUNK
        col = lax.broadcasted_iota(jnp.int32, (QBLK, KSPAN), 1)
        o_ref[...] = jnp.where((col >= first) & (col < first + band), x, NEG)
        if ng:
            done()

    res = pl.pallas_call(
        body, name=name, grid=(L * H,), in_specs=[pl.BlockSpec((None, 1, NREL_PAD), lambda i: (i, 0, 0))] + [HBM] * ng,
        out_specs=[pl.BlockSpec((None, QBLK, KSPAN), lambda i: (i, 0, 0))] + [HBM] * ng,
        out_shape=[jax.ShapeDtypeStruct((L * H, QBLK, KSPAN), F32)] + _gather_out_shapes(shards),
        scratch_shapes=_gather_scratch(ng) if ng else [], compiler_params=_arb(1),
    )(tab, *shards)
    return res[0].reshape(L, H, QBLK, KSPAN), list(res[1:])


def _col_row_forms(t, S):
    nc = S // CHUNK
    return t.T.reshape(nc, CHUNK, B_HEADS), t.reshape(B_HEADS, nc, CHUNK).transpose(1, 0, 2)


def _weight_view(name, gathered, tag):
    if name in ("w_out", "w_ffn_out"):
        return gathered.reshape(8 * gathered.shape[2], gathered.shape[3])
    stacked = gathered.reshape(4, 2 * gathered.shape[2], gathered.shape[3])
    return _reorder_w_in(stacked[None], f"w_in_cols_{tag}")[0] if name == "w_in" else stacked


def _layer_fwd(l, x, mod, W, P, big, gather=None, late=None):
    S, D = x.shape
    n = lambda s: f"{s}_l{l}"
    sh1, sc1, gt1, sh2, sc2, gt2 = (mod[i:i + 1] for i in range(6))
    h1 = _lnmod_fwd(x, P["norm1_g"][l:l + 1], sc1, sh1, n("ln1"))
    if late is None:
        proj = _matmul(h1, W["w_in"], "nn", F32, n("proj"), tn=1152)
    else:
        proj, got = _matmul(h1, W["w_in"], "nn", F32, n("proj"), tn=1152, gather=(late[1], l))
        W = {**W, **{k: _weight_view(k, t, f"l{l}") for k, t in zip(late[0], got)}}
    part = (lambda names: ([gather[0][BIG.index(k)] for k in names], gather[1])) if gather is not None else (lambda names: None)
    ya, got_a = _attn_fwd(proj, big, n("attn"), part(CARRY_ATTN))
    alog, dtb = P["a_log"][l].reshape(B_HEADS, 1), P["dt_bias"][l].reshape(B_HEADS, 1)
    beta, gam, b_t, a_t = _gdn_gates_fwd(proj, alog, dtb, n("gates"))
    bcol, _ = _col_row_forms(beta, S)
    gcol, grow = _col_row_forms(gam, S)
    qn, kn, v = _gdn_pre_fwd(proj, P["w_conv"][l], n("gdnpre"))
    tsave, amat, u, w, got_l = _gdn_local_fwd(qn, kn, v, bcol, gcol, grow, n("gdnlocal"), part(CARRY_LOCAL))
    o, ssave, got_s = _gdn_scan_fwd(qn, kn, u, w, amat, gcol, grow, n("gdnscan"), part(CARRY_SCAN))
    yb = _gdn_post_fwd(o, proj, P["gdn_norm_g"][l:l + 1], n("gdnpost"))
    pa = _matmul(ya, W["w_branch_a"], "nn", BF16, n("pa"), tm=2048, stacked=True)
    pb = _matmul(yb, W["w_branch_b"], "nn", BF16, n("pb"), tm=2048, stacked=True)
    merged = _merge_fwd(proj, pa, pb, n("merge"))
    ao = _matmul(merged, W["w_out"], "nn", F32, n("ao"))
    x1 = _gate_fwd(x, ao, gt1, n("res1"))
    h2 = _lnmod_fwd(x1, P["norm2_g"][l:l + 1], sc2, sh2, n("ln2"))
    gu = _matmul(h2, W["w_ffn_in"], "nn", BF16, n("gu"), stacked=True, gather=part(CARRY_GU))
    gu, got_g = gu if gather is not None else (gu, [])
    got = dict(zip(CARRY_ATTN + CARRY_LOCAL + CARRY_SCAN + CARRY_GU, got_a + got_l + got_s + got_g))
    gathered = {k: got[k] for k in BIG} if gather is not None else None
    act = _ffn_act_fwd(gu, n("act"))
    fo = _matmul(act, W["w_ffn_out"], "nn", F32, n("fo"), tk=1408)
    x2 = _gate_fwd(x1, fo, gt2, n("res2"))
    saved = dict(x=x, h1=h1, proj=proj, ya=ya, b_t=b_t, a_t=a_t, bcol=bcol, gcol=gcol, grow=grow,
                 qn=qn, kn=kn, v=v, o=o, tsave=tsave, ssave=ssave, amat=amat, u=u, w=w, yb=yb, pa=pa, pb=pb,
                 merged=merged, ao=ao, x1=x1,
                 h2=h2, gu=gu, act=act, fo=fo)
    return x2, saved, gathered, W


def _layer_bwd(l, dx2, sv, mod, W, P, big, exchange=()):
    S, D = dx2.shape
    n = lambda s: f"{s}_l{l}"
    sh1, sc1, gt1, sh2, sc2, gt2 = (mod[i:i + 1] for i in range(6))
    g, pay = {}, {}
    view = lambda t: t.reshape((4, 2, t.shape[-2] // (2 if t.ndim == 3 else 8), t.shape[-1]))
    dz2, dgt2 = _gate_bwd(dx2, sv["fo"], gt2, n("dres2"))
    g["w_ffn_out"], pay["w_ffn_out"] = map(view, _matmul(sv["act"], dz2, "tn", F32, n("dwfo"), tm=1408, also_bf16=True))
    dact = _matmul(dz2, W["w_ffn_out"], "nt", BF16, n("dact"), tn=1408)
    dgu = _ffn_act_bwd(sv["gu"], dact, n("dgu"))
    g["w_ffn_in"], pay["w_ffn_in"] = map(view, _matmul(sv["h2"], dgu, "tn", F32, n("dwfi"), out_stacked=True,
                                                       also_bf16=True))
    dh2 = _matmul(dgu, W["w_ffn_in"], "nt", F32, n("dh2"), stacked=True)
    dx1, dsh2, dsc2, dn2 = _lnmod_bwd(dh2, sv["x1"], P["norm2_g"][l:l + 1], sc2, dx2, n("dln2"))
    dz1, dgt1 = _gate_bwd(dx1, sv["ao"], gt1, n("dres1"))
    g["w_out"], pay["w_out"] = map(view, _matmul(sv["merged"], dz1, "tn", F32, n("dwo"), also_bf16=True))
    dmerged = _matmul(dz1, W["w_out"], "nt", BF16, n("dmerged"))
    dproj, dpa, dpb = _merge_bwd(sv["proj"], sv["pa"], sv["pb"], dmerged, n("dmerge"))
    g["w_branch_a"], pay["w_branch_a"] = map(view, _matmul(sv["ya"], dpa, "tn", F32, n("dwa"), out_stacked=True,
                                                           also_bf16=True))
    g["w_branch_b"], pay["w_branch_b"] = map(view, _matmul(sv["yb"], dpb, "tn", F32, n("dwb"), out_stacked=True,
                                                           also_bf16=True))
    dya = _matmul(dpa, W["w_branch_a"], "nt", BF16, n("dya"), tm=2048, stacked=True)
    dyb = _matmul(dpb, W["w_branch_b"], "nt", F32, n("dyb"), tm=2048, stacked=True)
    ex = (lambda names: [exchange[BIG.index(k)] for k in names]) if len(exchange) else (lambda names: ())
    dq, dk, dv, dbig, rec_a = _attn_bwd(sv["proj"], big, dya, n("dattn"), ex(CARRY_DATTN))
    g["rel_table"] = _rel_table_grad(dbig, n("drel"))[:, 0, :2 * A_MAX_REL + 1]
    dproj = _write_columns(dproj, [dq, dk, dv], 3 * dq.shape[1], P_QKVA // (3 * dq.shape[1]), n("dqkva"))
    do, dproj, dng = _gdn_post_bwd(dyb, sv["o"], sv["proj"], P["gdn_norm_g"][l:l + 1], dproj, n("dgdnpost"))
    g["gdn_norm_g"] = dng[0]
    du, dw, dqd, dkd, da, dgl = _gdn_scan_bwd(sv["qn"], sv["kn"], sv["u"], sv["w"], sv["amat"], sv["gcol"], sv["grow"],
                                              sv["ssave"], do, n("dgdnscan"))
    dqn, dkn, dvv, dbc, dgc, dgr, rec_l = _gdn_local_bwd(
        sv["qn"], sv["kn"], sv["v"], sv["bcol"], sv["gcol"], sv["grow"], sv["tsave"], du, dw, dqd, dkd, da, dgl,
        n("dgdnlocal"), ex(CARRY_DLOCAL))
    rec = dict(zip(CARRY_DATTN + CARRY_DLOCAL, rec_a + rec_l))
    received = [rec[k] for k in BIG] if len(exchange) else None
    dbeta_t = dbc.reshape(S, B_HEADS).T
    dgam_a = dgc.reshape(S, B_HEADS).T
    dgam_b = dgr.transpose(1, 0, 2).reshape(B_HEADS, S)
    alog, dtb = P["a_log"][l].reshape(B_HEADS, 1), P["dt_bias"][l].reshape(B_HEADS, 1)
    db_t, da_t, dal, ddt = _gdn_gates_bwd(dbeta_t, dgam_a, dgam_b, sv["b_t"], sv["a_t"], alog, dtb, n("dgates"))
    g["a_log"], g["dt_bias"] = dal[:, 0], ddt[:, 0]
    dproj, g["w_conv"] = _gdn_pre_bwd(sv["proj"], P["w_conv"][l], dqn, dkn, dvv, dproj, n("dgdnpre"))
    dba = jnp.concatenate([db_t.T, da_t.T, jnp.zeros((S, P_END - P_BA - 2 * B_HEADS), F32)], axis=1)
    dproj = _write_columns(dproj, [dba], dba.shape[1], P_BA // dba.shape[1], n("dba"))
    g["w_in"], pay["w_in"] = map(view, _restore_w_in(_matmul(sv["h1"], dproj, "tn", F32, n("dwin"), tn=1152),
                                                     n("dwin_cols")))
    dh1, from_sibling = _matmul(dproj, W["w_in"], "nt", F32, n("dh1"), tk=1152, sibling=[pay[k] for k in BIG])
    dx, dsh1, dsc1, dn1 = _lnmod_bwd(dh1, sv["x"], P["norm1_g"][l:l + 1], sc1, dx1, n("dln1"))
    g["norm1_g"], g["norm2_g"] = dn1[0], dn2[0]
    dmod = jnp.concatenate([dsh1, dsc1, dgt1, dsh2, dsc2, dgt2], axis=1)[0]
    return dx, g, from_sibling, dmod, received


SMALL = ("norm1_g", "norm2_g", "rel_table", "w_conv", "a_log", "dt_bias", "gdn_norm_g")
SMALL_PACK_C = 1024


def _as_rows(t):
    flat = t.reshape(-1)
    rows = -(-flat.shape[0] // SMALL_PACK_C)
    return jnp.pad(flat, (0, rows * SMALL_PACK_C - flat.shape[0])).reshape(rows, SMALL_PACK_C)


def _pack_rows(parts):
    blk = jnp.concatenate([_as_rows(p) for p in parts], axis=0)
    return jnp.pad(blk, ((0, -blk.shape[0] % 8), (0, 0)))


def _unpack_rows(blk, shapes):
    out, r = [], 0
    for shp in shapes:
        size = int(np.prod(shp))
        rows = -(-size // SMALL_PACK_C)
        out.append(blk[..., r:r + rows, :].reshape(blk.shape[:-2] + (rows * SMALL_PACK_C,))[..., :size]
                   .reshape(blk.shape[:-2] + tuple(shp)))
        r += rows
    return out


def kernel(x, c, w_ada, b_ada, norm1_g, norm2_g, w_in, rel_table, w_conv, a_log, dt_bias, gdn_norm_g, w_branch_a, w_branch_b, w_out, w_ffn_in, w_ffn_out, final_g, loss_target, m_w_ada, m_b_ada, m_norm1_g, m_norm2_g, m_w_in, m_rel_table, m_w_conv, m_a_log, m_dt_bias, m_gdn_norm_g, m_w_branch_a, m_w_branch_b, m_w_out, m_w_ffn_in, m_w_ffn_out, m_final_g, v_w_ada, v_b_ada, v_norm1_g, v_norm2_g, v_w_in, v_rel_table, v_w_conv, v_a_log, v_dt_bias, v_gdn_norm_g, v_w_branch_a, v_w_branch_b, v_w_out, v_w_ffn_in, v_w_ffn_out, v_final_g):
    weights = dict(w_ada=w_ada, b_ada=b_ada, norm1_g=norm1_g, norm2_g=norm2_g, w_in=w_in, rel_table=rel_table,
                   w_conv=w_conv, a_log=a_log, dt_bias=dt_bias, gdn_norm_g=gdn_norm_g, w_branch_a=w_branch_a,
                   w_branch_b=w_branch_b, w_out=w_out, w_ffn_in=w_ffn_in, w_ffn_out=w_ffn_out, final_g=final_g)
    mom_m = dict(w_ada=m_w_ada, b_ada=m_b_ada, norm1_g=m_norm1_g, norm2_g=m_norm2_g, w_in=m_w_in,
                 rel_table=m_rel_table, w_conv=m_w_conv, a_log=m_a_log, dt_bias=m_dt_bias, gdn_norm_g=m_gdn_norm_g,
                 w_branch_a=m_w_branch_a, w_branch_b=m_w_branch_b, w_out=m_w_out, w_ffn_in=m_w_ffn_in,
                 w_ffn_out=m_w_ffn_out, final_g=m_final_g)
    mom_v = dict(w_ada=v_w_ada, b_ada=v_b_ada, norm1_g=v_norm1_g, norm2_g=v_norm2_g, w_in=v_w_in,
                 rel_table=v_rel_table, w_conv=v_w_conv, a_log=v_a_log, dt_bias=v_dt_bias, gdn_norm_g=v_gdn_norm_g,
                 w_branch_a=v_w_branch_a, w_branch_b=v_w_branch_b, w_out=v_w_out, w_ffn_in=v_w_ffn_in,
                 w_ffn_out=v_w_ffn_out, final_g=v_final_g)
    xi, yi, ci = _position()
    chip = 2 * xi + yi
    dev = 2 * chip + ci
    L, D = norm1_g.shape
    NMOD = b_ada.shape[1] // D
    ns = w_ada.shape[2]
    cs = w_conv.shape[2]

    first_blk = _pack_rows([c, w_conv])
    first_all = _allgather8(first_blk, "gather_c").reshape(8, first_blk.shape[0], SMALL_PACK_C)
    c_all, w_conv_all = _unpack_rows(first_all, [(D,), w_conv.shape])
    w_conv_full = w_conv_all.reshape(4, 2, L, CONV_K, cs)[:, 0].transpose(1, 2, 0, 3).reshape(L, CONV_K, 4 * cs)
    b_shard = lax.dynamic_slice_in_dim(b_ada, chip * ns, ns, axis=1).reshape(L, 1, ns)
    mod_shard = _ada_mod(c_all, w_ada, b_shard, "ada_mod")
    mod_all = _allgather8(mod_shard.reshape(L * 8, ns), "gather_mod").reshape(4, 2, L, 8, ns)
    mod = lax.dynamic_index_in_dim(mod_all[:, 0], dev, axis=2, keepdims=False)
    mod = mod.transpose(1, 0, 2).reshape(L, NMOD, D)

    shards = [weights[k].astype(BF16) for k in BIG]
    shards = [s.reshape(s.shape[0], 2, s.shape[1] // 2, s.shape[2]) for s in shards]
    P = dict(norm1_g=norm1_g, norm2_g=norm2_g, w_conv=w_conv_full, a_log=a_log, dt_bias=dt_bias,
             gdn_norm_g=gdn_norm_g)
    shard_of = dict(zip(BIG, shards))

    big, got = _band_bias(rel_table, "band_bias", ([shard_of["w_in"]], 0))
    alone = ["w_branch_a", "w_branch_b", "w_out", "w_ffn_out"]
    got += _gather_weights([shard_of[k] for k in alone], 0, "gather_weights_l0")
    W = [{k: _weight_view(k, t, "l0") for k, t in zip(["w_in"] + alone, got)}]
    late = (["w_ffn_in"], [shard_of["w_ffn_in"]])
    xc = x[0]
    saved = []
    for l in range(L):
        xc, sv, gathered, W[l] = _layer_fwd(l, xc, mod[l], W[l], P, big[l], (shards, l + 1) if l + 1 < L else None,
                                           late if l == 0 else None)
        saved.append(sv)
        if l + 1 < L:
            W.append({k: _weight_view(k, gathered[k], f"l{l + 1}") for k in BIG})
    dx, loss_dev, dfinal = _loss_head(xc, final_g.reshape(1, D), loss_target[0], "loss_head")

    where = jnp.stack([ci, chip]).astype(jnp.int32)
    grads = [None] * L
    dmods = [None] * L
    shard_grads = {k: [None] * L for k in BIG}

    def finish_reduce_scatter(l, sums, from_chips):
        halves = [_sum_own_and_received(s_[1], r_, where, f"rs_sum_{k}_l{l}")
                  for k, s_, r_ in zip(BIG, sums, from_chips)]
        for k, t in zip(BIG, _rs_pair(halves, f"rs_pair_l{l}")):
            shard_grads[k][l] = t.reshape(2 * t.shape[1], t.shape[2])

    pending = None
    for l in reversed(range(L)):
        exchange = [s_[0] for s_ in pending] if pending is not None else ()
        dx, grads[l], from_sibling, dmods[l], received = _layer_bwd(l, dx, saved[l], mod[l], W[l], P, big[l], exchange)
        if pending is not None:
            finish_reduce_scatter(l + 1, pending, received)
        gs = [grads[l][k] for k in BIG]
        pending = [_pair_sums(g_, where, r_, f"rs_pair_sum_{k}_l{l}") for k, g_, r_ in zip(BIG, gs, from_sibling)]
    finish_reduce_scatter(0, pending, _rs_chips([s_[0] for s_ in pending], "rs_chips_l0"))
    dmod = jnp.stack(dmods)

    small = {k: jnp.stack([grads[l][k] for l in range(L)]) for k in SMALL}
    parts = [dmod] + [small[k] for k in SMALL] + [dfinal, loss_dev[0, :1]]
    small_blk = _pack_rows(parts)
    srows = small_blk.shape[0]
    small_all, small_sum = _allgather8(small_blk, "gather_small", reduce_rows=srows)
    shapes = [dmod.shape] + [small[k].shape for k in SMALL] + [(D,), (1,)]
    tot = _unpack_rows(small_sum, shapes)
    G = dict(zip(SMALL, tot[1:1 + len(SMALL)]))
    G["b_ada"] = tot[0].reshape(b_ada.shape)
    G["w_conv"] = lax.dynamic_slice_in_dim(G["w_conv"], chip * cs, cs, axis=2)
    G["final_g"] = tot[-2]
    loss = tot[-1][0]
    dmod_all = _unpack_rows(small_all.reshape(8, srows, SMALL_PACK_C), [dmod.shape])[0]
    dmod_cols = lax.dynamic_slice_in_dim(dmod_all, chip * ns, ns, axis=2).transpose(1, 0, 2)
    G["w_ada"] = _ada_wgrad(c_all, dmod_cols, "ada_wgrad")

    order = ["w_ada", "b_ada", "norm1_g", "norm2_g", "w_in", "rel_table", "w_conv", "a_log", "dt_bias", "gdn_norm_g",
             "w_branch_a", "w_branch_b", "w_out", "w_ffn_in", "w_ffn_out", "final_g"]
    deltas, new_m, new_v = {}, {}, {}
    for k in order:
        w = weights[k]
        if k == "w_in":
            to_cols = lambda t: jnp.transpose(t, (2, 0, 1))
            from_cols = lambda t: jnp.transpose(t, (1, 2, 0))
            gt = to_cols(jnp.stack(shard_grads[k]))
            d_, m_, v_ = _adamw_lead(to_cols(w), gt, to_cols(mom_m[k]), to_cols(mom_v[k]), f"adamw_{k}",
                                     W_IN_SHARD // 30)
            G[k], deltas[k], new_m[k], new_v[k] = from_cols(gt), from_cols(d_), from_cols(m_), from_cols(v_)
            continue
        if k in BIG:
            G[k], deltas[k], new_m[k], new_v[k] = _adamw_layers(w, shard_grads[k], mom_m[k], mom_v[k], f"adamw_{k}")
            continue
        as2d = (lambda t: t.reshape(1, -1)) if w.ndim == 1 else (lambda t: t)
        d_, m_, v_ = _adamw(as2d(w), as2d(G[k]), as2d(mom_m[k]), as2d(mom_v[k]), f"adamw_{k}")
        deltas[k], new_m[k], new_v[k] = d_.reshape(w.shape), m_.reshape(w.shape), v_.reshape(w.shape)
    return (loss, dx[None], *[G[k] for k in order], *[deltas[k] for k in order], *[new_m[k] for k in order],
            *[new_v[k] for k in order])
```

```python
import functools

import numpy as np
import jax
import jax.numpy as jnp
from jax import lax
from jax.experimental import pallas as pl
from jax.experimental.pallas import tpu as pltpu

F32 = jnp.float32
BF16 = jnp.bfloat16
HI = lax.Precision.HIGHEST
SOLVE_PREC = lax.Precision.HIGH
MESH = pl.DeviceIdType.MESH

EPS = 1e-6
CHUNK = 64
A_HEADS = 8
A_DH = 64
A_PAST = 8
A_MAX_REL = 128
B_HEADS = 4
B_DH = 128
CONV_K = 4
LANE = 128
QBLK = 4 * CHUNK
KSPAN = QBLK + A_PAST * CHUNK
NEG = -1e30

ADAM_LR = 0.001
ADAM_B1 = 0.9
ADAM_B2 = 0.999
ADAM_EPS = 1e-08
ADAM_WD = 0.01
ADAM_STEP = 10

P_QKVA, P_QKVB, P_GA, P_GB, P_Z, P_BA, P_END = 0, 1536, 3072, 4096, 5120, 5632, 5760
W_IN_SHARD = 1410


def _sigmoid(x):
    return 0.5 * jnp.tanh(0.5 * x) + 0.5


def _dot(a, b, ca, cb, prec):
    lead = a.ndim - 2
    batch = ((0,), (0,)) if lead else ((), ())
    return lax.dot_general(a, b, (((ca + lead,), (cb + lead,)), batch), precision=prec, preferred_element_type=F32)


def _nn(a, b, prec=None):
    return _dot(a, b, 1, 0, prec)


def _nt(a, b, prec=None):
    return _dot(a, b, 1, 1, prec)


def _tn(a, b, prec=None):
    return _dot(a, b, 0, 0, prec)


def _bnn(a, b):
    return _nn(a.astype(BF16), b.astype(BF16))


def _bnt(a, b):
    return _nt(a.astype(BF16), b.astype(BF16))


def _btn(a, b):
    return _tn(a.astype(BF16), b.astype(BF16))


def _pick(n, target, unit=LANE):
    best = None
    for t in range(unit, min(n, target) + 1, unit):
        if n % t == 0:
            best = t
    return best if best is not None else n


def _acc(ref, val, i):
    @pl.when(i == 0)
    def _():
        ref[...] = val

    @pl.when(i != 0)
    def _():
        ref[...] += val


def _arb(n):
    return pltpu.CompilerParams(dimension_semantics=("arbitrary",) * n)


def _par(n):
    return pltpu.CompilerParams(dimension_semantics=("parallel",) * n)


def _matmul(a, b, mode, out_dtype, name, tm=1024, tn=1024, tk=1024, layer=None, stacked=False, out_stacked=False,
            also_bf16=False, gather=None, sibling=None):
    bs = b.shape[1:] if layer is not None else b.shape
    if mode == "nn":
        M, K = a.shape
        N = 4 * bs[2] if stacked else bs[1]
        if stacked:
            tn = bs[2]
    elif mode == "nt":
        M, K = a.shape
        N = bs[1] if stacked else bs[0]
        if stacked:
            tk = bs[2]
    else:
        K, M = a.shape
        N = bs[1]
        if out_stacked:
            tn = N // 4
    tm, tn, tk = _pick(M, tm), _pick(N, tn), _pick(K, tk)
    nk = K // tk
    lead = () if layer is None else (layer,)
    lead_blk = () if layer is None else (None,)
    if mode == "nn":
        a_spec = pl.BlockSpec((tm, tk), lambda i, j, k: (i, k))
        if stacked:
            b_spec = pl.BlockSpec(lead_blk + (None, tk, tn), lambda i, j, k: lead + (j, k, 0))
        else:
            b_spec = pl.BlockSpec(lead_blk + (tk, tn), lambda i, j, k: lead + (k, j))
        dot = _nn
    elif mode == "nt":
        a_spec = pl.BlockSpec((tm, tk), lambda i, j, k: (i, k))
        if stacked:
            b_spec = pl.BlockSpec(lead_blk + (None, tn, tk), lambda i, j, k: lead + (k, j, 0))
        else:
            b_spec = pl.BlockSpec(lead_blk + (tn, tk), lambda i, j, k: lead + (j, k))
        dot = _nt
    else:
        a_spec = pl.BlockSpec((tk, tm), lambda i, j, k: (k, i))
        b_spec = pl.BlockSpec((tk, tn), lambda i, j, k: (k, j))
        dot = _tn
    if out_stacked:
        o_spec = pl.BlockSpec((None, tm, tn), lambda i, j, k: (j, i, 0))
        o_shape = jax.ShapeDtypeStruct((4, M, tn), out_dtype)
    else:
        o_spec = pl.BlockSpec((tm, tn), lambda i, j, k: (i, j))
        o_shape = jax.ShapeDtypeStruct((M, N), out_dtype)

    if gather is not None:
        shards, glayer = gather
        carried_shapes, carried_scratch = _gather_out_shapes(shards), _gather_scratch(len(shards))
    elif sibling is not None:
        shards = sibling
        carried_shapes, carried_scratch = _sibling_out_shapes(shards), _sibling_scratch(len(shards))
    else:
        shards, carried_shapes, carried_scratch = (), [], []
    ng = len(shards)
    o_shapes = [o_shape] + ([jax.ShapeDtypeStruct(o_shape.shape, BF16)] if also_bf16 else [])
    no = len(o_shapes)
    gi, gj = M // tm, N // tn

    def write(o_refs, val):
        for o_ref in o_refs:
            o_ref[...] = val.astype(o_ref.dtype)

    def body(a_ref, b_ref, *refs):
        srcs, o_refs, gouts, scratch = refs[:ng], refs[ng:ng + no], refs[ng + no:2 * ng + no], refs[2 * ng + no:]
        i, j, k = pl.program_id(0), pl.program_id(1), pl.program_id(2)
        if ng:
            start = (j == 0) & (k == 0)
            sems = scratch[len(scratch) - len(carried_scratch):]
            phases = (_gather_phases(glayer, srcs, gouts, *sems) if gather is not None
                      else _sibling_phases(srcs, gouts, *sems))
            done = _carry(phases, (i == 0) & start, (i == gi - 1) & start,
                          (i == gi - 1) & (j == gj - 1) & (k == nk - 1))
        if nk == 1:
            write(o_refs, dot(a_ref[...], b_ref[...]))
        else:
            acc_ref = scratch[0]

            @pl.when(k == 0)
            def _():
                acc_ref[...] = jnp.zeros_like(acc_ref)

            acc_ref[...] += dot(a_ref[...], b_ref[...])

            @pl.when(k == nk - 1)
            def _():
                write(o_refs, acc_ref[...])
        if ng:
            done()

    sem = ("arbitrary",) * 3 if ng else ("parallel", "parallel", "arbitrary")
    res = pl.pallas_call(
        body, name=name, grid=(gi, gj, nk), in_specs=[a_spec, b_spec] + [HBM] * ng,
        out_specs=[o_spec] * no + [HBM] * ng, out_shape=o_shapes + carried_shapes,
        scratch_shapes=([] if nk == 1 else [pltpu.VMEM((tm, tn), F32)]) + carried_scratch,
        compiler_params=pltpu.CompilerParams(dimension_semantics=sem),
    )(a, b, *shards)
    out = tuple(res[:no]) if also_bf16 else res[0]
    return (out, list(res[no:])) if ng else out


def _rows(tm, n, col=0):
    return pl.BlockSpec((tm, n), lambda i: (i, col))


def _vec(n):
    return pl.BlockSpec((1, n), lambda i: (0, 0))


def _lnmod_fwd(x, g, sc, sh, name):
    S, D = x.shape
    tm = _pick(S, 512, 8)

    def body(x_ref, g_ref, sc_ref, sh_ref, o_ref):
        xv = x_ref[...]
        r = lax.rsqrt(jnp.mean(xv * xv, axis=-1, keepdims=True) + EPS)
        o_ref[...] = ((xv * r * g_ref[...]) * (1.0 + sc_ref[...]) + sh_ref[...]).astype(BF16)

    return pl.pallas_call(
        body, name=name, grid=(S // tm,),
        in_specs=[_rows(tm, D), _vec(D), _vec(D), _vec(D)], out_specs=_rows(tm, D),
        out_shape=jax.ShapeDtypeStruct((S, D), BF16), compiler_params=_par(1),
    )(x, g, sc, sh)


def _lnmod_bwd(dh, x, g, sc, dres, name):
    S, D = x.shape
    tm = _pick(S, 512, 8)

    def body(dh_ref, x_ref, g_ref, sc_ref, dres_ref, dx_ref, dsh_ref, dsc_ref, dg_ref):
        i = pl.program_id(0)
        xv = x_ref[...]
        dh_ = dh_ref[...]
        r = lax.rsqrt(jnp.mean(xv * xv, axis=-1, keepdims=True) + EPS)
        xhat = xv * r
        gv = g_ref[...]
        dn = dh_ * (1.0 + sc_ref[...])
        dxhat = dn * gv
        dx_ref[...] = dres_ref[...] + r * (dxhat - xhat * jnp.mean(dxhat * xhat, axis=-1, keepdims=True))
        _acc(dsh_ref, jnp.sum(dh_, axis=0, keepdims=True), i)
        _acc(dsc_ref, jnp.sum(dh_ * (xhat * gv), axis=0, keepdims=True), i)
        _acc(dg_ref, jnp.sum(dn * xhat, axis=0, keepdims=True), i)

    return pl.pallas_call(
        body, name=name, grid=(S // tm,),
        in_specs=[_rows(tm, D), _rows(tm, D), _vec(D), _vec(D), _rows(tm, D)],
        out_specs=[_rows(tm, D), _vec(D), _vec(D), _vec(D)],
        out_shape=[jax.ShapeDtypeStruct((S, D), F32)] + [jax.ShapeDtypeStruct((1, D), F32)] * 3,
        compiler_params=_arb(1),
    )(dh, x, g, sc, dres)


def _gate_fwd(x, y, gt, name):
    S, D = x.shape
    tm = _pick(S, 512, 8)

    def body(x_ref, y_ref, gt_ref, o_ref):
        o_ref[...] = x_ref[...] + gt_ref[...] * y_ref[...]

    return pl.pallas_call(
        body, name=name, grid=(S // tm,), in_specs=[_rows(tm, D), _rows(tm, D), _vec(D)], out_specs=_rows(tm, D),
        out_shape=jax.ShapeDtypeStruct((S, D), F32), compiler_params=_par(1),
    )(x, y, gt)


def _gate_bwd(dx, y, gt, name):
    S, D = dx.shape
    tm = _pick(S, 512, 8)

    def body(dx_ref, y_ref, gt_ref, dz_ref, dgt_ref):
        i = pl.program_id(0)
        d = dx_ref[...]
        dz_ref[...] = (d * gt_ref[...]).astype(BF16)
        _acc(dgt_ref, jnp.sum(d * y_ref[...], axis=0, keepdims=True), i)

    return pl.pallas_call(
        body, name=name, grid=(S // tm,), in_specs=[_rows(tm, D), _rows(tm, D), _vec(D)],
        out_specs=[_rows(tm, D), _vec(D)],
        out_shape=[jax.ShapeDtypeStruct((S, D), BF16), jax.ShapeDtypeStruct((1, D), F32)],
        compiler_params=_arb(1),
    )(dx, y, gt)


def _ffn_act_fwd(gu, name):
    S, H2 = gu.shape
    H = H2 // 2
    tm = _pick(S, 256, 8)

    def body(g_ref, u_ref, o_ref):
        gv = g_ref[...].astype(F32)
        o_ref[...] = (gv * _sigmoid(gv) * u_ref[...].astype(F32)).astype(BF16)

    return pl.pallas_call(
        body, name=name, grid=(S // tm,), in_specs=[_rows(tm, H, 0), _rows(tm, H, 1)], out_specs=_rows(tm, H),
        out_shape=jax.ShapeDtypeStruct((S, H), BF16), compiler_params=_par(1),
    )(gu, gu)


def _ffn_act_bwd(gu, dact, name):
    S, H2 = gu.shape
    H = H2 // 2
    tm = _pick(S, 256, 8)

    def body(g_ref, u_ref, da_ref, o_ref):
        gv = g_ref[...].astype(F32)
        s = _sigmoid(gv)
        da = da_ref[...].astype(F32)
        o_ref[:, :H] = (da * u_ref[...].astype(F32) * (s * (1.0 + gv * (1.0 - s)))).astype(BF16)
        o_ref[:, H:] = (da * (gv * s)).astype(BF16)

    return pl.pallas_call(
        body, name=name, grid=(S // tm,), in_specs=[_rows(tm, H, 0), _rows(tm, H, 1), _rows(tm, H)],
        out_specs=_rows(tm, H2), out_shape=jax.ShapeDtypeStruct((S, H2), BF16), compiler_params=_par(1),
    )(gu, gu, dact)


def _merge_fwd(proj, pa, pb, name):
    S, D = pa.shape
    tm = _pick(S, 512, 8)

    def body(ga_ref, gb_ref, pa_ref, pb_ref, o_ref):
        o_ref[...] = (_sigmoid(ga_ref[...]) * pa_ref[...].astype(F32)
                      + _sigmoid(gb_ref[...]) * pb_ref[...].astype(F32)).astype(BF16)

    return pl.pallas_call(
        body, name=name, grid=(S // tm,),
        in_specs=[_rows(tm, D, P_GA // D), _rows(tm, D, P_GB // D), _rows(tm, D), _rows(tm, D)],
        out_specs=_rows(tm, D), out_shape=jax.ShapeDtypeStruct((S, D), BF16), compiler_params=_par(1),
    )(proj, proj, pa, pb)


def _merge_bwd(proj, pa, pb, dm, name):
    S, D = pa.shape
    tm = _pick(S, 512, 16)
    rows_j = pl.BlockSpec((tm, D), lambda i, j: (i, 0))

    def body(g_ref, pa_ref, pb_ref, dm_ref, dg_ref, dpa_ref, dpb_ref):
        d = dm_ref[...].astype(F32)
        s = _sigmoid(g_ref[...])
        for branch, p_ref, dp_ref in ((0, pa_ref, dpa_ref), (1, pb_ref, dpb_ref)):
            @pl.when(pl.program_id(1) == branch)
            def _():
                dg_ref[...] = (d * p_ref[...].astype(F32) * s * (1.0 - s)).astype(BF16)
                dp_ref[...] = (d * s).astype(BF16)

    return pl.pallas_call(
        body, name=name, grid=(S // tm, 2),
        in_specs=[pl.BlockSpec((tm, D), lambda i, j: (i, P_GA // D + j)), rows_j, rows_j, rows_j],
        out_specs=[pl.BlockSpec((tm, D), lambda i, j: (i, P_GA // D + j)), rows_j, rows_j],
        out_shape=[jax.ShapeDtypeStruct((S, P_END), BF16), jax.ShapeDtypeStruct((S, D), BF16),
                   jax.ShapeDtypeStruct((S, D), BF16)],
        compiler_params=_arb(2),
    )(proj, pa, pb, dm)


def _write_columns(buf, parts, width, colblk, name):
    S = buf.shape[0]
    tm = _pick(S, 512, 16)
    n = len(parts)

    def body(*refs):
        o_ref = refs[n + 1]
        off = 0
        for p_ref in refs[:n]:
            w = p_ref.shape[1]
            o_ref[:, off:off + w] = p_ref[...].astype(BF16)
            off += w

    return pl.pallas_call(
        body, name=name, grid=(S // tm,), in_specs=[_rows(tm, p.shape[1]) for p in parts] + [HBM],
        out_specs=_rows(tm, width, colblk), out_shape=jax.ShapeDtypeStruct(buf.shape, buf.dtype),
        input_output_aliases={n: 0}, compiler_params=_par(1),
    )(*parts, buf)


def _loss_head(x, g, target, name):
    S, D = x.shape
    tm = _pick(S, 512, 8)

    def body(x_ref, g_ref, t_ref, dx_ref, loss_ref, dg_ref):
        i = pl.program_id(0)
        xv = x_ref[...]
        gv = g_ref[...]
        r = lax.rsqrt(jnp.mean(xv * xv, axis=-1, keepdims=True) + EPS)
        xhat = xv * r
        err = xhat * gv - t_ref[...]
        part = 0.5 * jnp.sum(jnp.mean(err * err, axis=-1, keepdims=True), axis=0, keepdims=True)
        _acc(loss_ref, jnp.broadcast_to(part, (1, LANE)), i)
        dy = err * (1.0 / D)
        _acc(dg_ref, jnp.sum(dy * xhat, axis=0, keepdims=True), i)
        dxhat = dy * gv
        dx_ref[...] = r * (dxhat - xhat * jnp.mean(dxhat * xhat, axis=-1, keepdims=True))

    return pl.pallas_call(
        body, name=name, grid=(S // tm,), in_specs=[_rows(tm, D), _vec(D), _rows(tm, D)],
        out_specs=[_rows(tm, D), _vec(LANE), _vec(D)],
        out_shape=[jax.ShapeDtypeStruct((S, D), F32), jax.ShapeDtypeStruct((1, LANE), F32),
                   jax.ShapeDtypeStruct((1, D), F32)],
        compiler_params=_arb(1),
    )(x, g, target)


HEADS_PER_SLAB = LANE // A_DH
N_SLABS = A_HEADS // HEADS_PER_SLAB
SPAN_BLOCKS = KSPAN // QBLK


def _attn_specs(seg):
    q_spec = pl.BlockSpec((QBLK, LANE), lambda p, m: (m, seg[0] * N_SLABS + p))
    k_specs = [pl.BlockSpec((QBLK, LANE), functools.partial(
        lambda j, p, m: (jnp.maximum(m - (SPAN_BLOCKS - 1) + j, 0), seg[1] * N_SLABS + p), j)) for j in range(SPAN_BLOCKS)]
    v_specs = [pl.BlockSpec((QBLK, LANE), functools.partial(
        lambda j, p, m: (jnp.maximum(m - (SPAN_BLOCKS - 1) + j, 0), seg[2] * N_SLABS + p), j)) for j in range(SPAN_BLOCKS)]
    b_spec = pl.BlockSpec((HEADS_PER_SLAB, QBLK, KSPAN), lambda p, m: (p, 0, 0))
    return q_spec, k_specs, v_specs, b_spec


def _head_lanes(t, hh):
    lane = lax.broadcasted_iota(jnp.int32, t.shape, 1)
    return jnp.where((lane // A_DH) == hh, t, jnp.zeros_like(t))


def _front_mask(m):
    col = lax.broadcasted_iota(jnp.int32, (QBLK, KSPAN), 1)
    return jnp.where(col < (SPAN_BLOCKS - 1 - m) * QBLK, NEG, 0.0)


def _attn_probs(qk, bias, front):
    s = qk * (A_DH ** -0.5) + (bias + front)
    p = jnp.exp(s - jnp.max(s, axis=-1, keepdims=True))
    return p * (1.0 / jnp.sum(p, axis=-1, keepdims=True))


def _grid_ends(nq):
    p, m = pl.program_id(0), pl.program_id(1)
    return (p == 0) & (m == 0), (p == N_SLABS - 1) & (m == nq // 2), (p == N_SLABS - 1) & (m == nq - 1)


def _attn_fwd(proj, big, name, gather=None):
    S = proj.shape[0]
    q_spec, k_specs, v_specs, b_spec = _attn_specs((0, 1, 2))
    shards, layer = gather if gather is not None else ((), None)
    ng = len(shards)

    def body(q_ref, k0, k1, k2, v0, v1, v2, b_ref, *rest):
        srcs, o_ref, gouts, sems = rest[:ng], rest[ng], rest[ng + 1:2 * ng + 1], rest[2 * ng + 1:]
        done = _carry(_gather_phases(layer, srcs, gouts, *sems), *_grid_ends(S // QBLK)) if ng else None
        m = pl.program_id(1)
        q = q_ref[...].astype(BF16)
        k = jnp.concatenate([k0[...], k1[...], k2[...]], axis=0).astype(BF16)
        v = jnp.concatenate([v0[...], v1[...], v2[...]], axis=0).astype(BF16)
        front = _front_mask(m)
        heads = range(HEADS_PER_SLAB)
        scores = [_nt(_head_lanes(q, hh), k) for hh in heads]
        probs = [_attn_probs(scores[hh], b_ref[hh], front).astype(BF16) for hh in heads]
        outs = [_nn(probs[hh], v) for hh in heads]
        lane = lax.broadcasted_iota(jnp.int32, (QBLK, LANE), 1)
        o_ref[...] = jnp.where(lane < A_DH, outs[0], outs[1]).astype(BF16)
        if ng:
            done()

    res = pl.pallas_call(
        body, name=name, grid=(N_SLABS, S // QBLK), in_specs=[q_spec] + k_specs + v_specs + [b_spec] + [HBM] * ng,
        out_specs=[pl.BlockSpec((QBLK, LANE), lambda p, m: (m, p))] + [HBM] * ng,
        out_shape=[jax.ShapeDtypeStruct((S, A_HEADS * A_DH), BF16)] + _gather_out_shapes(shards),
        scratch_shapes=_gather_scratch(ng) if ng else [], compiler_params=_arb(2),
    )(proj, proj, proj, proj, proj, proj, proj, big, *shards)
    return res[0], list(res[1:])


def _attn_bwd(proj, big, dya, name, exchange=()):
    S = proj.shape[0]
    W = A_HEADS * A_DH
    q_spec, k_specs, v_specs, b_spec = _attn_specs((0, 1, 2))
    out_q = pl.BlockSpec((QBLK, LANE), lambda p, m: (m, p))
    out_kv = pl.BlockSpec((S, LANE), lambda p, m: (0, p))
    ne = len(exchange)

    def body(q_ref, k0, k1, k2, v0, v1, v2, b_ref, do_ref, *rest):
        srcs, (dq_ref, dk_ref, dv_ref, db_ref), eouts, sems = rest[:ne], rest[ne:ne + 4], rest[ne + 4:2 * ne + 4], rest[2 * ne + 4:]
        done = _carry(_chips_phases(srcs, eouts, *sems), *_grid_ends(S // QBLK)) if ne else None
        m = pl.program_id(1)

        @pl.when(m == 0)
        def _():
            dk_ref[...] = jnp.zeros_like(dk_ref)
            dv_ref[...] = jnp.zeros_like(dv_ref)
            db_ref[...] = jnp.zeros_like(db_ref)

        q = q_ref[...].astype(BF16)
        k = jnp.concatenate([k0[...], k1[...], k2[...]], axis=0).astype(BF16)
        v = jnp.concatenate([v0[...], v1[...], v2[...]], axis=0).astype(BF16)
        do = do_ref[...]
        front = _front_mask(m)
        heads = range(HEADS_PER_SLAB)
        qh = [_head_lanes(q, hh) for hh in heads]
        doh = [_head_lanes(do, hh) for hh in heads]
        scores = [_nt(qh[hh], k) for hh in heads]
        dps = [_nt(doh[hh], v) for hh in heads]
        ps = [_attn_probs(scores[hh], b_ref[hh], front) for hh in heads]
        dss = [ps[hh] * (dps[hh] - jnp.sum(ps[hh] * dps[hh], axis=-1, keepdims=True)) for hh in heads]
        for hh in heads:
            db_ref[hh] += dss[hh]
        dsb = [(dss[hh] * (A_DH ** -0.5)).astype(BF16) for hh in heads]
        dqs = [_nn(dsb[hh], k) for hh in heads]
        dk = sum(_tn(dsb[hh], qh[hh]) for hh in heads)
        dv = sum(_tn(ps[hh].astype(BF16), doh[hh]) for hh in heads)
        lane = lax.broadcasted_iota(jnp.int32, (QBLK, LANE), 1)
        dq_ref[...] = jnp.where(lane < A_DH, dqs[0], dqs[1])
        for j in range(SPAN_BLOCKS):
            blk = m - (SPAN_BLOCKS - 1) + j

            @pl.when(blk >= 0)
            def _():
                off = pl.multiple_of(blk * QBLK, QBLK)
                dk_ref[pl.ds(off, QBLK), :] += dk[j * QBLK:(j + 1) * QBLK]
                dv_ref[pl.ds(off, QBLK), :] += dv[j * QBLK:(j + 1) * QBLK]
        if ne:
            done()

    res = pl.pallas_call(
        body, name=name, grid=(N_SLABS, S // QBLK),
        in_specs=[q_spec] + k_specs + v_specs + [b_spec, pl.BlockSpec((QBLK, LANE), lambda p, m: (m, p))] + [HBM] * ne,
        out_specs=[out_q, out_kv, out_kv, b_spec] + [HBM] * ne,
        out_shape=[jax.ShapeDtypeStruct((S, W), F32)] * 3 + [jax.ShapeDtypeStruct((A_HEADS, QBLK, KSPAN), F32)]
        + _chips_out_shapes(exchange),
        scratch_shapes=_chips_scratch(ne) if ne else [], compiler_params=_arb(2),
    )(proj, proj, proj, proj, proj, proj, proj, big, dya, *exchange)
    return tuple(res[:4]) + (list(res[4:]),)


NREL_PAD = 3 * LANE
SKEW_W = 1024


def _rel_table_grad(dbig, name):
    H, R, C = dbig.shape

    def body(d_ref, o_ref):
        x = jnp.concatenate([d_ref[...], jnp.zeros((R, SKEW_W - C), F32)], axis=1)
        row = lax.broadcasted_iota(jnp.int32, (R, SKEW_W), 0)
        for b in range(R.bit_length() - 1):
            x = jnp.where(((row >> b) & 1) == 1, pltpu.roll(x, SKEW_W - (1 << b), 1), x)
        e = jnp.sum(x, axis=0, keepdims=True)
        xi = lax.broadcasted_iota(jnp.int32, (SKEW_W, NREL_PAD), 0)
        r = lax.broadcasted_iota(jnp.int32, (SKEW_W, NREL_PAD), 1)
        diag = jnp.where(xi < C, xi, xi - SKEW_W)
        rel = jnp.clip(A_PAST * CHUNK - diag, -A_MAX_REL, A_MAX_REL) + A_MAX_REL
        o_ref[...] = _nn(e, jnp.where(rel == r, 1.0, 0.0).astype(F32), HI)

    return pl.pallas_call(
        body, name=name, grid=(H,), in_specs=[pl.BlockSpec((None, R, C), lambda h: (h, 0, 0))],
        out_specs=pl.BlockSpec((None, 1, NREL_PAD), lambda h: (h, 0, 0)),
        out_shape=jax.ShapeDtypeStruct((H, 1, NREL_PAD), F32), compiler_params=_par(1),
    )(dbig)


def _chunk_cumsum_matrix(n, reverse):
    j = lax.broadcasted_iota(jnp.int32, (n, n), 0)
    i = lax.broadcasted_iota(jnp.int32, (n, n), 1)
    same = (j // CHUNK) == (i // CHUNK)
    return jnp.where(same & ((j >= i) if reverse else (j <= i)), 1.0, 0.0).astype(F32)


def _gdn_gates_fwd(proj, alog, dtb, name):
    Hh, S = B_HEADS, proj.shape[0]
    tl = _pick(S, 512)
    row = pl.BlockSpec((Hh, tl), lambda i: (0, i))
    col = pl.BlockSpec((Hh, 1), lambda i: (0, 0))

    def body(ba_ref, al_ref, dt_ref, beta_ref, gam_ref, b_ref, a_ref, t_ref):
        t_ref[...] = ba_ref[...].T
        b = t_ref[0:Hh, :]
        a = t_ref[Hh:2 * Hh, :]
        z = a + dt_ref[...]
        sp = jnp.maximum(z, 0.0) + jnp.log(1.0 + jnp.exp(-jnp.abs(z)))
        g = -jnp.exp(al_ref[...]) * sp
        beta_ref[...] = _sigmoid(b)
        gam_ref[...] = _nn(g, _chunk_cumsum_matrix(tl, False), HI)
        b_ref[...] = b
        a_ref[...] = a

    return pl.pallas_call(
        body, name=name, grid=(S // tl,), in_specs=[pl.BlockSpec((tl, LANE), lambda i: (i, P_BA // LANE)), col, col],
        out_specs=[row] * 4, out_shape=[jax.ShapeDtypeStruct((Hh, S), F32)] * 4,
        scratch_shapes=[pltpu.VMEM((LANE, tl), F32)], compiler_params=_par(1),
    )(proj, alog, dtb)


def _gdn_gates_bwd(dbeta, dgam_a, dgam_b, b_t, a_t, alog, dtb, name):
    Hh, S = b_t.shape
    tl = _pick(S, 512)
    row = pl.BlockSpec((Hh, tl), lambda i: (0, i))
    col = pl.BlockSpec((Hh, 1), lambda i: (0, 0))
    accs = pl.BlockSpec((Hh, LANE), lambda i: (0, 0))

    def body(dbeta_ref, dga_ref, dgb_ref, b_ref, a_ref, al_ref, dt_ref, db_ref, da_ref, dal_ref, ddt_ref):
        i = pl.program_id(0)
        z = a_ref[...] + dt_ref[...]
        sp = jnp.maximum(z, 0.0) + jnp.log(1.0 + jnp.exp(-jnp.abs(z)))
        ea = jnp.exp(al_ref[...])
        dg = _nn(dga_ref[...] + dgb_ref[...], _chunk_cumsum_matrix(tl, True), HI)
        da = dg * (-ea) * _sigmoid(z)
        beta = _sigmoid(b_ref[...])
        db_ref[...] = dbeta_ref[...] * beta * (1.0 - beta)
        da_ref[...] = da
        _acc(dal_ref, jnp.broadcast_to(jnp.sum(dg * (-ea * sp), axis=1, keepdims=True), (Hh, LANE)), i)
        _acc(ddt_ref, jnp.broadcast_to(jnp.sum(da, axis=1, keepdims=True), (Hh, LANE)), i)

    return pl.pallas_call(
        body, name=name, grid=(S // tl,), in_specs=[row] * 5 + [col, col], out_specs=[row, row, accs, accs],
        out_shape=[jax.ShapeDtypeStruct((Hh, S), F32)] * 2 + [jax.ShapeDtypeStruct((Hh, LANE), F32)] * 2,
        compiler_params=_arb(1),
    )(dbeta, dgam_a, dgam_b, b_t, a_t, alog, dtb)


HALO = 8


def _conv_silu(xx_ref, w_ref, tm):
    y = w_ref[0:1, :] * xx_ref[pl.ds(HALO - CONV_K + 1, tm), :]
    for j in range(1, CONV_K):
        y = y + w_ref[j:j + 1, :] * xx_ref[pl.ds(HALO - CONV_K + 1 + j, tm), :]
    return y, y * _sigmoid(y)


def _fill_prev_halo(xx_ref, x_ref, prev_ref, i, tm):
    xx_ref[pl.ds(HALO, tm), :] = x_ref[...]

    @pl.when(i == 0)
    def _():
        xx_ref[pl.ds(0, HALO), :] = jnp.zeros((HALO, xx_ref.shape[1]), F32)

    @pl.when(i != 0)
    def _():
        xx_ref[pl.ds(0, HALO), :] = prev_ref[...]


def _gdn_pre_specs(tm, C, colblk):
    cur = pl.BlockSpec((tm, C), lambda i: (i, colblk))
    prev = pl.BlockSpec((HALO, C), lambda i: (jnp.maximum(i * (tm // HALO) - 1, 0), colblk))
    return cur, prev


def _gdn_pre_fwd(proj, wconv, name):
    S = proj.shape[0]
    C = 3 * B_HEADS * B_DH
    W = B_HEADS * B_DH
    tm = _pick(S, 256, 8)
    cur, prev = _gdn_pre_specs(tm, C, P_QKVB // C)

    def body(x_ref, prev_ref, w_ref, q_ref, k_ref, v_ref, xx_ref):
        i = pl.program_id(0)
        _fill_prev_halo(xx_ref, x_ref, prev_ref, i, tm)
        _, sl = _conv_silu(xx_ref, w_ref, tm)
        for h in range(B_HEADS):
            hs = slice(h * B_DH, (h + 1) * B_DH)
            q = sl[:, h * B_DH:(h + 1) * B_DH]
            k = sl[:, W + h * B_DH:W + (h + 1) * B_DH]
            q_ref[:, hs] = q * (lax.rsqrt(jnp.sum(q * q, axis=-1, keepdims=True) + EPS) * (B_DH ** -0.5))
            k_ref[:, hs] = k * lax.rsqrt(jnp.sum(k * k, axis=-1, keepdims=True) + EPS)
        v_ref[...] = sl[:, 2 * W:]

    return pl.pallas_call(
        body, name=name, grid=(S // tm,), in_specs=[cur, prev, pl.BlockSpec((CONV_K, C), lambda i: (0, 0))],
        out_specs=[_rows(tm, W)] * 3, out_shape=[jax.ShapeDtypeStruct((S, W), F32)] * 3,
        scratch_shapes=[pltpu.VMEM((HALO + tm, C), F32)], compiler_params=_par(1),
    )(proj, proj, wconv)


def _conv_rows(xx_ref, w_ref, start, n):
    base = HALO - CONV_K + 1 + start
    y = w_ref[0:1, :] * xx_ref[pl.ds(base, n), :]
    for j in range(1, CONV_K):
        y = y + w_ref[j:j + 1, :] * xx_ref[pl.ds(base + j, n), :]
    return y


def _pre_dy(y, dq, dk, dv):
    W = B_HEADS * B_DH
    sg = _sigmoid(y)
    sl = y * sg
    dsilu = sg * (1.0 + y * (1.0 - sg))
    parts = []
    for base, d_all, c in ((0, dq, B_DH ** -0.5), (W, dk, 1.0)):
        for h in range(B_HEADS):
            lo = base + h * B_DH
            t = sl[:, lo:lo + B_DH]
            d = d_all[:, h * B_DH:(h + 1) * B_DH]
            r = lax.rsqrt(jnp.sum(t * t, axis=-1, keepdims=True) + EPS)
            parts.append((c * r) * (d - t * (r * r) * jnp.sum(d * t, axis=-1, keepdims=True)) * dsilu[:, lo:lo + B_DH])
    parts.append(dv * dsilu[:, 2 * W:])
    return jnp.concatenate(parts, axis=1)


def _gdn_pre_bwd(proj, wconv, dqn, dkn, dv, dproj, name):
    S = proj.shape[0]
    C = 3 * B_HEADS * B_DH
    W = B_HEADS * B_DH
    tm = _pick(S, 256, 16)
    nt_ = S // tm
    cur, prev = _gdn_pre_specs(tm, C, P_QKVB // C)
    after = lambda i: jnp.minimum((i + 1) * (tm // HALO), S // HALO - 1)
    next_x = pl.BlockSpec((HALO, C), lambda i: (after(i), P_QKVB // C))
    next_d = pl.BlockSpec((HALO, W), lambda i: (after(i), 0))

    def body(x_ref, prev_ref, nx_ref, w_ref, dq_ref, dk_ref, dv_ref, ndq_ref, ndk_ref, ndv_ref, _, dx_ref, dw_ref,
             xx_ref, dd_ref):
        i = pl.program_id(0)
        _fill_prev_halo(xx_ref, x_ref, prev_ref, i, tm)
        xx_ref[pl.ds(HALO + tm, HALO), :] = nx_ref[...]
        dyv = _pre_dy(_conv_rows(xx_ref, w_ref, 0, tm), dq_ref[...], dk_ref[...], dv_ref[...])
        dd_ref[pl.ds(0, tm), :] = dyv

        @pl.when(i == nt_ - 1)
        def _():
            dd_ref[pl.ds(tm, HALO), :] = jnp.zeros((HALO, C), F32)

        @pl.when(i != nt_ - 1)
        def _():
            dd_ref[pl.ds(tm, HALO), :] = _pre_dy(_conv_rows(xx_ref, w_ref, tm, HALO), ndq_ref[...], ndk_ref[...],
                                                 ndv_ref[...])

        dx = w_ref[0:1, :] * dd_ref[pl.ds(CONV_K - 1, tm), :]
        for j in range(1, CONV_K):
            dx = dx + w_ref[j:j + 1, :] * dd_ref[pl.ds(CONV_K - 1 - j, tm), :]
        dx_ref[...] = dx.astype(BF16)
        dw = jnp.concatenate(
            [jnp.sum(dyv * xx_ref[pl.ds(HALO - CONV_K + 1 + j, tm), :], axis=0, keepdims=True) for j in range(CONV_K)],
            axis=0)
        _acc(dw_ref, dw, i)

    return pl.pallas_call(
        body, name=name, grid=(nt_,),
        in_specs=[cur, prev, next_x, pl.BlockSpec((CONV_K, C), lambda i: (0, 0))] + [_rows(tm, W)] * 3 + [next_d] * 3
        + [HBM],
        out_specs=[_rows(tm, C, P_QKVB // C), pl.BlockSpec((CONV_K, C), lambda i: (0, 0))],
        out_shape=[jax.ShapeDtypeStruct(dproj.shape, BF16), jax.ShapeDtypeStruct((CONV_K, C), F32)],
        input_output_aliases={10: 0},
        scratch_shapes=[pltpu.VMEM((HALO + tm + HALO, C), F32), pltpu.VMEM((tm + HALO, C), F32)],
        compiler_params=_arb(1),
    )(proj, proj, proj, wconv, dqn, dkn, dv, dqn, dkn, dv, dproj)


def _chunk_masks():
    row = lax.broadcasted_iota(jnp.int32, (CHUNK, CHUNK), 0)
    col = lax.broadcasted_iota(jnp.int32, (CHUNK, CHUNK), 1)
    return row >= col, row > col


def _chunk_local(q, k, vv, bc, gc, gr, tri):
    dm = jnp.where(tri, jnp.exp(jnp.where(tri, gc - gr, 0.0)), 0.0)
    kk = _bnt(k, k)
    glast = gr[..., CHUNK - 1:CHUNK]
    ep = jnp.exp(gc)
    em = jnp.exp(glast - gc)
    el = jnp.exp(glast)
    return dm, kk, ep, em, el, vv * bc, k * (bc * ep)


def _unit_lower_inverse(low):
    row = lax.broadcasted_iota(jnp.int32, (CHUNK, CHUNK), 0)
    col = lax.broadcasted_iota(jnp.int32, (CHUNK, CHUNK), 1)
    p = -low
    t = jnp.where(row == col, 1.0, 0.0).astype(F32) + p
    steps = CHUNK.bit_length() - 2
    for _ in range(steps):
        p = _nn(p, p, SOLVE_PREC)
        t = t + _nn(t, p, SOLVE_PREC)
    return t


GROUP = 8


LATE = 7


def _carry(phases, first, middle, last):
    if len(phases) == 3:
        pl.when(first)(phases[0])
        pl.when(middle)(phases[1])
        return lambda: pl.when(last)(phases[2])
    pl.when(first)(phases[0])
    return lambda: pl.when(last)(phases[1])


def _pairs(nchunks):
    return [(c, h) for c in range(nchunks) for h in range(B_HEADS)]


def _tok(c):
    return slice(c * CHUNK, (c + 1) * CHUNK)


def _head(h):
    return slice(h * B_DH, (h + 1) * B_DH)


def _stack_tokens(ref, nchunks):
    return jnp.stack([ref[_tok(c), _head(h)] for c, h in _pairs(nchunks)])


def _stack_cols(ref, nchunks):
    per_chunk = [ref[c] for c in range(nchunks)] if len(ref.shape) == 3 else [ref[...]]
    return jnp.stack([per_chunk[c][:, h:h + 1] for c, h in _pairs(nchunks)])


def _stack_rows(ref, nchunks):
    if len(ref.shape) == 3:
        return jnp.stack([ref[c, h:h + 1, :] for c, h in _pairs(nchunks)])
    return jnp.stack([ref[h:h + 1, :] for _, h in _pairs(1)])


def _gdn_group_specs(ng_steps, W):
    tok = pl.BlockSpec((GROUP * CHUNK, W), lambda i: (i, 0))
    colv = pl.BlockSpec((GROUP, CHUNK, B_HEADS), lambda i: (i, 0, 0))
    rowv = pl.BlockSpec((GROUP, B_HEADS, CHUNK), lambda i: (i, 0, 0))
    mat = pl.BlockSpec((GROUP, B_HEADS, CHUNK, CHUNK), lambda i: (i, 0, 0, 0))
    return tok, colv, rowv, mat


def _gdn_local_fwd(qn, kn, v, bcol, gcol, grow, name, gather=None):
    S, Wd = qn.shape
    nc = S // CHUNK
    steps = nc // GROUP
    tok, colv, rowv, mat = _gdn_group_specs(steps, Wd)
    shards, layer = gather if gather is not None else ((), None)
    ng = len(shards)

    def body(q_ref, k_ref, v_ref, bc_ref, gc_ref, gr_ref, *rest):
        srcs, (t_ref, a_ref, u_ref, w_ref), gouts, sems = rest[:ng], rest[ng:ng + 4], rest[ng + 4:2 * ng + 4], rest[2 * ng + 4:]
        i = pl.program_id(0)
        done = _carry(_gather_phases(layer, srcs, gouts, *sems), i == 0, i == LATE * steps // 8, i == steps - 1) if ng else None
        tri, strict = _chunk_masks()
        q, k, vv = (_stack_tokens(r, GROUP) for r in (q_ref, k_ref, v_ref))
        bc, gc, gr = _stack_cols(bc_ref, GROUP), _stack_cols(gc_ref, GROUP), _stack_rows(gr_ref, GROUP)
        dm, kk, ep, em, el, vb, kb = _chunk_local(q, k, vv, bc, gc, gr, tri)
        t = _unit_lower_inverse(jnp.where(strict, bc * kk * dm, 0.0))
        a = _bnt(q, k) * dm
        u = _nn(t, vb, SOLVE_PREC)
        w = _nn(t, kb, SOLVE_PREC)
        for n, (c, h) in enumerate(_pairs(GROUP)):
            t_ref[c, h] = t[n]
            a_ref[c, h] = a[n]
            u_ref[_tok(c), _head(h)] = u[n]
            w_ref[_tok(c), _head(h)] = w[n]
        if ng:
            done()

    res = pl.pallas_call(
        body, name=name, grid=(steps,), in_specs=[tok, tok, tok, colv, colv, rowv] + [HBM] * ng,
        out_specs=[mat, mat, tok, tok] + [HBM] * ng,
        out_shape=[jax.ShapeDtypeStruct((nc, B_HEADS, CHUNK, CHUNK), F32)] * 2 + [jax.ShapeDtypeStruct((S, Wd), F32)] * 2
        + _gather_out_shapes(shards),
        scratch_shapes=_gather_scratch(ng) if ng else [], compiler_params=_arb(1),
    )(qn, kn, v, bcol, gcol, grow, *shards)
    return res[0], res[1], res[2], res[3], list(res[4:])


def _scan_decays(gc, gr):
    glast = gr[..., CHUNK - 1:CHUNK]
    return jnp.exp(gc), jnp.exp(glast - gc), jnp.exp(glast)


SCAN = 8


def _gdn_scan_specs(steps, rev):
    idx = (lambda i: steps - 1 - i) if rev else (lambda i: i)
    W = B_HEADS * B_DH
    tok = pl.BlockSpec((SCAN * CHUNK, W), lambda i: (idx(i), 0))
    colv = pl.BlockSpec((SCAN, CHUNK, B_HEADS), lambda i: (idx(i), 0, 0))
    rowv = pl.BlockSpec((SCAN, B_HEADS, CHUNK), lambda i: (idx(i), 0, 0))
    mat = pl.BlockSpec((SCAN, B_HEADS, CHUNK, CHUNK), lambda i: (idx(i), 0, 0, 0))
    smat = pl.BlockSpec((SCAN, B_HEADS, B_DH, B_DH), lambda i: (idx(i), 0, 0, 0))
    return tok, colv, rowv, mat, smat


def _chunk_rows(ref, c):
    return ref.at[pl.ds(c * CHUNK, CHUNK)]


def _gdn_scan_fwd(qn, kn, u, w, a, gcol, grow, name, gather=None):
    S, Wd = qn.shape
    nc = S // CHUNK
    steps = nc // SCAN
    tok, colv, rowv, mat, smat = _gdn_scan_specs(steps, False)
    shards, layer = gather if gather is not None else ((), None)
    ng = len(shards)

    def body(q_ref, k_ref, u_ref, w_ref, a_ref, gc_ref, gr_ref, *rest):
        srcs, (o_ref, sh_ref), gouts = rest[:ng], rest[ng:ng + 2], rest[ng + 2:2 * ng + 2]
        st_ref, sems = rest[2 * ng + 2], rest[2 * ng + 3:]
        i = pl.program_id(0)
        done = (_carry(_gather_phases(layer, srcs, gouts, *sems), i == 0, i == LATE * steps // 8, i == steps - 1)
                if ng else None)

        @pl.when(i == 0)
        def _():
            st_ref[...] = jnp.zeros_like(st_ref)

        for c in range(SCAN):
            ep, em, el = _scan_decays(_stack_cols(gc_ref.at[c], 1), _stack_rows(gr_ref.at[c], 1))
            q, k, u, w = (_stack_tokens(_chunk_rows(r, c), 1) for r in (q_ref, k_ref, u_ref, w_ref))
            s0 = st_ref[...]
            ut = u - _bnn(w, s0)
            o = _bnn(q * ep, s0) + _bnn(a_ref[c], ut)
            st_ref[...] = el * s0 + _btn(k * em, ut)
            sh_ref[c] = s0
            for h in range(B_HEADS):
                o_ref[_tok(c), _head(h)] = o[h]
        if ng:
            done()

    res = pl.pallas_call(
        body, name=name, grid=(steps,), in_specs=[tok, tok, tok, tok, mat, colv, rowv] + [HBM] * ng,
        out_specs=[tok, smat] + [HBM] * ng,
        out_shape=[jax.ShapeDtypeStruct((S, Wd), F32), jax.ShapeDtypeStruct((nc, B_HEADS, B_DH, B_DH), F32)]
        + _gather_out_shapes(shards),
        scratch_shapes=[pltpu.VMEM((B_HEADS, B_DH, B_DH), F32)] + (_gather_scratch(ng) if ng else []),
        compiler_params=_arb(1),
    )(qn, kn, u, w, a, gcol, grow, *shards)
    return res[0], res[1], list(res[2:])


def _gdn_scan_bwd(qn, kn, u, w, a, gcol, grow, ssave, do, name):
    S, Wd = qn.shape
    nc = S // CHUNK
    steps = nc // SCAN
    tok, colv, rowv, mat, smat = _gdn_scan_specs(steps, True)

    def body(q_ref, k_ref, u_ref, w_ref, a_ref, gc_ref, gr_ref, sh_ref, do_ref,
             du_ref, dw_ref, dqd_ref, dkd_ref, da_ref, dgl_ref, ds_ref):
        i = pl.program_id(0)

        @pl.when(i == 0)
        def _():
            ds_ref[...] = jnp.zeros_like(ds_ref)

        tri, _ = _chunk_masks()
        sub4 = lax.broadcasted_iota(jnp.int32, (B_HEADS, CHUNK), 0)
        lane_last = lax.broadcasted_iota(jnp.int32, (1, CHUNK), 1) == CHUNK - 1
        for c in reversed(range(SCAN)):
            ep, em, el = _scan_decays(_stack_cols(gc_ref.at[c], 1), _stack_rows(gr_ref.at[c], 1))
            q, k, u, w, dout = (_stack_tokens(_chunk_rows(r, c), 1) for r in (q_ref, k_ref, u_ref, w_ref, do_ref))
            s0 = sh_ref[c]
            ds = ds_ref[...]
            ut = u - _bnn(w, s0)
            dut = _btn(a_ref[c], dout) + _bnn(k * em, ds)
            ds_ref[...] = el * ds + _btn(q * ep, dout) - _btn(w, dut)
            dw = -_bnt(dut, s0)
            dqd = _bnt(dout, s0)
            dkd = _bnt(ut, ds)
            da_ref[c] = jnp.where(tri, _bnt(dout, ut), 0.0)
            d_el = jnp.sum(jnp.sum(s0 * ds, axis=-1, keepdims=True), axis=-2, keepdims=True)
            last = d_el * el
            dgl_acc = jnp.zeros((B_HEADS, CHUNK), F32)
            for h in range(B_HEADS):
                du_ref[_tok(c), _head(h)] = dut[h]
                dw_ref[_tok(c), _head(h)] = dw[h]
                dqd_ref[_tok(c), _head(h)] = dqd[h]
                dkd_ref[_tok(c), _head(h)] = dkd[h]
                dgl_acc = jnp.where(sub4 == h, jnp.where(lane_last, last[h], 0.0), dgl_acc)
            dgl_ref[c] = dgl_acc

    return pl.pallas_call(
        body, name=name, grid=(steps,), in_specs=[tok, tok, tok, tok, mat, colv, rowv, smat, tok],
        out_specs=[tok, tok, tok, tok, mat, rowv],
        out_shape=[jax.ShapeDtypeStruct((S, Wd), F32)] * 4 + [jax.ShapeDtypeStruct((nc, B_HEADS, CHUNK, CHUNK), F32),
                                                             jax.ShapeDtypeStruct((nc, B_HEADS, CHUNK), F32)],
        scratch_shapes=[pltpu.VMEM((B_HEADS, B_DH, B_DH), F32)], compiler_params=_arb(1),
    )(qn, kn, u, w, a, gcol, grow, ssave, do)


def _gdn_local_bwd(qn, kn, v, bcol, gcol, grow, tsave, du, dw, dqd, dkd, da, dgl, name, exchange=()):
    S, Wd = qn.shape
    nc = S // CHUNK
    steps = nc // GROUP
    tok, colv, rowv, mat = _gdn_group_specs(steps, Wd)
    ne = len(exchange)

    def body(q_ref, k_ref, v_ref, bc_ref, gc_ref, gr_ref, t_ref, du_ref, dw_ref, dqd_ref, dkd_ref, da_ref, dgl_ref, *rest):
        srcs, (dq_ref, dk_ref, dv_ref, dbc_ref, dgc_ref, dgr_ref) = rest[:ne], rest[ne:ne + 6]
        eouts, sems = rest[ne + 6:2 * ne + 6], rest[2 * ne + 6:]
        i = pl.program_id(0)
        done = _carry(_chips_phases(srcs, eouts, *sems), i == 0, None, i == steps - 1) if ne else None
        tri, strict = _chunk_masks()
        lane4 = lax.broadcasted_iota(jnp.int32, (CHUNK, B_HEADS), 1)
        sub4 = lax.broadcasted_iota(jnp.int32, (B_HEADS, CHUNK), 0)
        lane_last = lax.broadcasted_iota(jnp.int32, (1, CHUNK), 1) == CHUNK - 1
        q, k, vv, dut, dwv, dqd, dkd = (_stack_tokens(r, GROUP)
                                        for r in (q_ref, k_ref, v_ref, du_ref, dw_ref, dqd_ref, dkd_ref))
        bc, gc, gr = _stack_cols(bc_ref, GROUP), _stack_cols(gc_ref, GROUP), _stack_rows(gr_ref, GROUP)
        dm, kk, ep, em, el, vb, kb = _chunk_local(q, k, vv, bc, gc, gr, tri)
        t = jnp.stack([t_ref[c, h] for c, h in _pairs(GROUP)])
        dav = jnp.stack([da_ref[c, h] for c, h in _pairs(GROUP)])
        qk = _bnt(q, k)
        dt = _nt(dut, vb, SOLVE_PREC) + _nt(dwv, kb, SOLVE_PREC)
        dvb = _tn(t, dut, SOLVE_PREC)
        dkb = _tn(t, dwv, SOLVE_PREC)
        dl = jnp.where(strict, -_tn(t, _nt(dt, t, SOLVE_PREC), SOLVE_PREC), 0.0)
        g1 = dl * dm
        dkb_k = jnp.sum(dkb * k, axis=-1, keepdims=True)
        dbeta = jnp.sum(g1 * kk, axis=-1, keepdims=True) + jnp.sum(dvb * vv, axis=-1, keepdims=True) + dkb_k * ep
        dkk = g1 * bc
        ddm = dl * (bc * kk) + dav * qk
        dqk = dav * dm
        dq = _bnn(dqk, k) + dqd * ep
        dk = _btn(dqk, q) + _bnn(dkk, k) + _btn(dkk, k) + dkb * (bc * ep) + dkd * em
        dv = dvb * bc
        dep = dkb_k * bc + jnp.sum(dqd * q, axis=-1, keepdims=True)
        dem = jnp.sum(dkd * k, axis=-1, keepdims=True)
        mm = ddm * dm
        dgam_c = jnp.sum(mm, axis=-1, keepdims=True) + dep * ep - dem * em
        dglast = jnp.sum(dem * em, axis=-2, keepdims=True)
        dgam_r = -jnp.sum(mm, axis=-2, keepdims=True) + jnp.where(lane_last, dglast, 0.0)
        for c in range(GROUP):
            dbc_acc = jnp.zeros((CHUNK, B_HEADS), F32)
            dgc_acc = jnp.zeros((CHUNK, B_HEADS), F32)
            dgr_acc = jnp.zeros((B_HEADS, CHUNK), F32)
            for h in range(B_HEADS):
                n = c * B_HEADS + h
                dq_ref[_tok(c), _head(h)] = dq[n]
                dk_ref[_tok(c), _head(h)] = dk[n]
                dv_ref[_tok(c), _head(h)] = dv[n]
                dbc_acc = jnp.where(lane4 == h, dbeta[n], dbc_acc)
                dgc_acc = jnp.where(lane4 == h, dgam_c[n], dgc_acc)
                dgr_acc = jnp.where(sub4 == h, dgam_r[n], dgr_acc)
            dbc_ref[c] = dbc_acc
            dgc_ref[c] = dgc_acc
            dgr_ref[c] = dgr_acc + dgl_ref[c]
        if ne:
            done()

    res = pl.pallas_call(
        body, name=name, grid=(steps,),
        in_specs=[tok, tok, tok, colv, colv, rowv, mat, tok, tok, tok, tok, mat, rowv] + [HBM] * ne,
        out_specs=[tok, tok, tok, colv, colv, rowv] + [HBM] * ne,
        out_shape=[jax.ShapeDtypeStruct((S, Wd), F32)] * 3
        + [jax.ShapeDtypeStruct((nc, CHUNK, B_HEADS), F32)] * 2 + [jax.ShapeDtypeStruct((nc, B_HEADS, CHUNK), F32)]
        + _chips_out_shapes(exchange),
        scratch_shapes=_chips_scratch(ne) if ne else [], compiler_params=_arb(1),
    )(qn, kn, v, bcol, gcol, grow, tsave, du, dw, dqd, dkd, da, dgl, *exchange)
    return tuple(res[:6]) + (list(res[6:]),)


def _gdn_post_fwd(o, proj, ng, name):
    S, W = o.shape
    tm = _pick(S, 512, 8)

    def body(o_ref, z_ref, g_ref, y_ref):
        gv = g_ref[...]
        for h in range(B_HEADS):
            hs = slice(h * B_DH, (h + 1) * B_DH)
            oh = o_ref[:, hs]
            z = z_ref[:, hs]
            r = lax.rsqrt(jnp.mean(oh * oh, axis=-1, keepdims=True) + EPS)
            y_ref[:, hs] = (oh * r * gv * (z * _sigmoid(z))).astype(BF16)

    return pl.pallas_call(
        body, name=name, grid=(S // tm,), in_specs=[_rows(tm, W), _rows(tm, W, P_Z // W), _vec(B_DH)],
        out_specs=_rows(tm, W), out_shape=jax.ShapeDtypeStruct((S, W), BF16), compiler_params=_par(1),
    )(o, proj, ng)


def _gdn_post_bwd(dy, o, proj, ng, dproj, name):
    S, W = o.shape
    tm = _pick(S, 512, 16)

    def body(dy_ref, o_ref, z_ref, g_ref, _, do_ref, dz_ref, dg_ref):
        i = pl.program_id(0)
        gv = g_ref[...]
        dg = jnp.zeros((1, B_DH), F32)
        for h in range(B_HEADS):
            hs = slice(h * B_DH, (h + 1) * B_DH)
            oh = o_ref[:, hs]
            z = z_ref[:, hs]
            d = dy_ref[:, hs]
            r = lax.rsqrt(jnp.mean(oh * oh, axis=-1, keepdims=True) + EPS)
            n = oh * r
            sg = _sigmoid(z)
            sz = z * sg
            dn = d * gv * sz
            dg = dg + jnp.sum(d * n * sz, axis=0, keepdims=True)
            dz_ref[:, hs] = (d * n * gv * (sg * (1.0 + z * (1.0 - sg)))).astype(BF16)
            do_ref[:, hs] = r * (dn - n * jnp.mean(dn * n, axis=-1, keepdims=True))
        _acc(dg_ref, dg, i)

    return pl.pallas_call(
        body, name=name, grid=(S // tm,),
        in_specs=[_rows(tm, W), _rows(tm, W), _rows(tm, W, P_Z // W), _vec(B_DH), HBM],
        out_specs=[_rows(tm, W), _rows(tm, W, P_Z // W), _vec(B_DH)],
        out_shape=[jax.ShapeDtypeStruct((S, W), F32), jax.ShapeDtypeStruct(dproj.shape, BF16),
                   jax.ShapeDtypeStruct((1, B_DH), F32)],
        input_output_aliases={4: 1}, compiler_params=_arb(1),
    )(dy, o, proj, ng, dproj)


def _ada_mod(c_all, w_ada, b_shard, name):
    L, D, Ns = w_ada.shape
    B = c_all.shape[0]

    def body(c_ref, w_ref, b_ref, o_ref):
        cv = c_ref[...]
        cond = (cv * _sigmoid(cv)).astype(BF16)
        o_ref[...] = _nn(cond, w_ref[...].astype(BF16)) + b_ref[...]

    return pl.pallas_call(
        body, name=name, grid=(L,),
        in_specs=[pl.BlockSpec((B, D), lambda l: (0, 0)), pl.BlockSpec((None, D, Ns), lambda l: (l, 0, 0)),
                  pl.BlockSpec((None, 1, Ns), lambda l: (l, 0, 0))],
        out_specs=pl.BlockSpec((None, B, Ns), lambda l: (l, 0, 0)),
        out_shape=jax.ShapeDtypeStruct((L, B, Ns), F32), compiler_params=_par(1),
    )(c_all, w_ada, b_shard)


def _ada_wgrad(c_all, dmod, name):
    L, B, Ns = dmod.shape
    D = c_all.shape[1]

    def body(c_ref, d_ref, o_ref):
        cv = c_ref[...]
        cond = (cv * _sigmoid(cv)).astype(BF16)
        o_ref[...] = _tn(cond, d_ref[...].astype(BF16))

    return pl.pallas_call(
        body, name=name, grid=(L,),
        in_specs=[pl.BlockSpec((B, D), lambda l: (0, 0)), pl.BlockSpec((None, B, Ns), lambda l: (l, 0, 0))],
        out_specs=pl.BlockSpec((None, D, Ns), lambda l: (l, 0, 0)),
        out_shape=jax.ShapeDtypeStruct((L, D, Ns), F32), compiler_params=_par(1),
    )(c_all, dmod)


W_IN_PIECES = ((0, 0, 1410), (1, 0, 1410), (2, 0, 252), (2, 772, 638), (3, 0, 1410), (2, 252, 512), (2, 764, 8))


def _reorder_w_in(w4, name):
    L, _, D, Cs = w4.shape
    tm = _pick(D, 256, 16)
    used = sum(p[2] for p in W_IN_PIECES)

    def body(w_ref, o_ref):
        shard = [w_ref[s] for s in range(4)]
        parts = [shard[s][:, lo:lo + n] for s, lo, n in W_IN_PIECES]
        o_ref[...] = jnp.concatenate(parts + [jnp.zeros((tm, P_END - used), w4.dtype)], axis=1)

    return pl.pallas_call(
        body, name=name, grid=(L, D // tm), in_specs=[pl.BlockSpec((None, 4, tm, Cs), lambda l, i: (l, 0, i, 0))],
        out_specs=pl.BlockSpec((None, tm, P_END), lambda l, i: (l, i, 0)),
        out_shape=jax.ShapeDtypeStruct((L, D, P_END), w4.dtype), compiler_params=_par(2),
    )(w4)


def _restore_w_in(g, name):
    D = g.shape[0]
    tm = _pick(D, 256, 16)

    def body(g_ref, o_ref, ob_ref):
        gv = g_ref[...]
        off = 0
        pieces = {}
        for s, lo, n in W_IN_PIECES:
            pieces.setdefault(s, []).append((lo, gv[:, off:off + n]))
            off += n
        for s, lst in pieces.items():
            lst.sort(key=lambda t: t[0])
            shard = lst[0][1] if len(lst) == 1 else jnp.concatenate([t[1] for t in lst], axis=1)
            o_ref[s] = shard
            ob_ref[s] = shard.astype(BF16)

    spec = pl.BlockSpec((4, tm, W_IN_SHARD), lambda i: (0, i, 0))
    return pl.pallas_call(
        body, name=name, grid=(D // tm,), in_specs=[pl.BlockSpec((tm, P_END), lambda i: (i, 0))],
        out_specs=[spec, spec],
        out_shape=[jax.ShapeDtypeStruct((4, D, W_IN_SHARD), g.dtype), jax.ShapeDtypeStruct((4, D, W_IN_SHARD), BF16)],
        compiler_params=_par(1),
    )(g)


def _adam_update(w, g, m, v):
    mn = ADAM_B1 * m + (1.0 - ADAM_B1) * g
    vn = ADAM_B2 * v + (1.0 - ADAM_B2) * (g * g)
    m_hat = mn / (1.0 - ADAM_B1 ** ADAM_STEP)
    v_hat = vn / (1.0 - ADAM_B2 ** ADAM_STEP)
    return -ADAM_LR * (m_hat / (jnp.sqrt(v_hat) + ADAM_EPS) + ADAM_WD * w), mn, vn


def _adamw(w, g, m, v, name):
    shape = w.shape
    C = shape[-1]
    R = w.size // C
    tm = _pick(R, 512, 8)
    spec = pl.BlockSpec((tm, C), lambda i: (i, 0))

    def body(w_ref, g_ref, m_ref, v_ref, d_ref, mo_ref, vo_ref):
        d_ref[...], mo_ref[...], vo_ref[...] = _adam_update(w_ref[...], g_ref[...], m_ref[...], v_ref[...])

    outs = pl.pallas_call(
        body, name=name, grid=(R // tm,), in_specs=[spec] * 4, out_specs=[spec] * 3,
        out_shape=[jax.ShapeDtypeStruct((R, C), F32)] * 3, compiler_params=_par(1),
    )(*(t.reshape(R, C) for t in (w, g, m, v)))
    return tuple(o.reshape(shape) for o in outs)


def _adamw_lead(w, g, m, v, name, tl):
    A, B, C = w.shape
    spec = pl.BlockSpec((tl, B, C), lambda i: (i, 0, 0))

    def body(w_ref, g_ref, m_ref, v_ref, d_ref, mo_ref, vo_ref):
        d_ref[...], mo_ref[...], vo_ref[...] = _adam_update(w_ref[...], g_ref[...], m_ref[...], v_ref[...])

    return pl.pallas_call(
        body, name=name, grid=(A // tl,), in_specs=[spec] * 4, out_specs=[spec] * 3,
        out_shape=[jax.ShapeDtypeStruct((A, B, C), F32)] * 3, compiler_params=_par(1),
    )(w, g, m, v)


def _adamw_layers(w, gs, m, v, name):
    L, R, C = w.shape
    tm = _pick(R, 256, 8)
    spec = pl.BlockSpec((None, tm, C), lambda l, i: (l, i, 0))
    g_specs = [pl.BlockSpec((tm, C), functools.partial(lambda ll, l, i: (jnp.where(l == ll, i, 0), 0), ll))
               for ll in range(L)]

    def body(w_ref, m_ref, v_ref, *rest):
        g_refs, (go_ref, d_ref, mo_ref, vo_ref) = rest[:L], rest[L:]
        l = pl.program_id(0)
        for ll in range(L):
            @pl.when(l == ll)
            def _():
                g = g_refs[ll][...]
                go_ref[...] = g
                d_ref[...], mo_ref[...], vo_ref[...] = _adam_update(w_ref[...], g, m_ref[...], v_ref[...])

    return pl.pallas_call(
        body, name=name, grid=(L, R // tm), in_specs=[spec] * 3 + g_specs, out_specs=[spec] * 4,
        out_shape=[jax.ShapeDtypeStruct((L, R, C), F32)] * 4, compiler_params=_arb(2),
    )(w, m, v, *gs)


def _pair_sums(a, where, b, name):
    NB, _, R, C = a.shape

    def body(where_ref, a_ref, b_ref, p_ref, own_ref):
        s = a_ref[...] + b_ref[...].astype(F32)
        p_ref[...] = s.astype(BF16)

        @pl.when(pl.program_id(0) == where_ref[1])
        def _():
            own_ref[...] = s

    return pl.pallas_call(
        body, name=name,
        grid_spec=pltpu.PrefetchScalarGridSpec(
            num_scalar_prefetch=1, grid=(NB,),
            in_specs=[pl.BlockSpec((None, None, R, C), lambda k, w: (k, w[0], 0, 0)),
                      pl.BlockSpec((None, R, C), lambda k, w: (k, 0, 0))],
            out_specs=[pl.BlockSpec((None, R, C), lambda k, w: (k, 0, 0)), pl.BlockSpec((R, C), lambda k, w: (0, 0))]),
        out_shape=[jax.ShapeDtypeStruct((NB, R, C), BF16), jax.ShapeDtypeStruct((R, C), F32)],
        compiler_params=_arb(1),
    )(where, a, b)


def _sum_own_and_received(own, recv, where, name):
    R, C = own.shape
    tm = _pick(R, 256, 16)

    def body(where_ref, p_ref, r_ref, o_ref):
        o_ref[...] = ((p_ref[...] + r_ref[0].astype(F32)) + r_ref[1].astype(F32)) + r_ref[2].astype(F32)

    return pl.pallas_call(
        body, name=name,
        grid_spec=pltpu.PrefetchScalarGridSpec(
            num_scalar_prefetch=1, grid=(R // tm,),
            in_specs=[pl.BlockSpec((tm, C), lambda i, w: (i, 0)), pl.BlockSpec((3, tm, C), lambda i, w: (0, i, 0))],
            out_specs=pl.BlockSpec((None, tm, C), lambda i, w: (w[0], i, 0))),
        out_shape=jax.ShapeDtypeStruct((2, R, C), F32), compiler_params=_par(1),
    )(where, own, recv)


def _position():
    return lax.axis_index("x"), lax.axis_index("y"), lax.axis_index("c")


def _other_chips(x, y):
    return [(1 - x, y), (x, 1 - y), (1 - x, 1 - y)]


HBM = pl.BlockSpec(memory_space=pl.ANY)


def _allgather8(blk, name, reduce_rows=None):
    M, N = blk.shape

    def body(x_ref, out_ref, *rest):
        if reduce_rows is None:
            send_sems, recv_sems, local_sem = rest
        else:
            sum_ref, send_sems, recv_sems, local_sem = rest
        x, y, c = _position()
        me, sibling = (x, y, c), (x, y, 1 - c)
        chips = _other_chips(x, y)

        def rows(px, py, pc):
            return out_ref.at[pl.ds((4 * px + 2 * py + pc) * M, M), :]

        def copy(k, block, to, src=None):
            return pltpu.make_async_remote_copy(
                src_ref=rows(*block) if src is None else src, dst_ref=rows(*block),
                send_sem=send_sems.at[k], recv_sem=recv_sems.at[k], device_id=to, device_id_type=MESH)

        mine = pltpu.make_async_copy(x_ref, rows(*me), local_sem)
        mine.start()
        first = [copy(0, me, sibling, src=x_ref)]
        first += [copy(1 + j, me, (*chip, c), src=x_ref) for j, chip in enumerate(chips)]
        for cp in first:
            cp.start()
        passed = [copy(4 + j, (*chip, c), sibling) for j, chip in enumerate(chips)]
        for j, chip in enumerate(chips):
            copy(1 + j, (*chip, c), me).wait_recv()
            passed[j].start()
        copy(0, sibling, me).wait_recv()
        for j, chip in enumerate(chips):
            copy(4 + j, (*chip, 1 - c), me).wait_recv()
        for cp in first + passed:
            cp.wait_send()
        mine.wait()
        if reduce_rows is not None:
            tot = out_ref[pl.ds(0, reduce_rows), :]
            for d in range(1, 8):
                tot = tot + out_ref[pl.ds(d * M, reduce_rows), :]
            sum_ref[...] = tot

    vmem = pl.BlockSpec(memory_space=pltpu.VMEM)
    out_shape = [jax.ShapeDtypeStruct((8 * M, N), blk.dtype)]
    if reduce_rows is not None:
        out_shape.append(jax.ShapeDtypeStruct((reduce_rows, N), blk.dtype))
    res = pl.pallas_call(
        body, name=name, out_shape=out_shape, in_specs=[vmem], out_specs=[vmem] * len(out_shape),
        scratch_shapes=[pltpu.SemaphoreType.DMA((7,)), pltpu.SemaphoreType.DMA((7,)), pltpu.SemaphoreType.DMA],
    )(blk)
    return res[0] if reduce_rows is None else (res[0], res[1])


def _gather_phases(layer, srcs, outs, send_sems, recv_sems, local_sems):
    n = len(srcs)
    x, y, c = _position()
    me, sibling = (x, y, c), (x, y, 1 - c)
    chips = _other_chips(x, y)

    def region(t, px, py, pc):
        return outs[t].at[2 * px + py, pc]

    def copy(t, k, block, to, own=False):
        return pltpu.make_async_remote_copy(
            src_ref=srcs[t].at[layer, c] if own else region(t, *block), dst_ref=region(t, *block),
            send_sem=send_sems.at[7 * t + k], recv_sem=recv_sems.at[7 * t + k], device_id=to, device_id_type=MESH)

    def local(t):
        return pltpu.make_async_copy(srcs[t].at[layer, c], region(t, *me), local_sems.at[t])

    def first(t):
        return [copy(t, 0, me, sibling, own=True)] + [copy(t, 1 + j, me, (*chip, c), own=True)
                                                       for j, chip in enumerate(chips)]

    def start():
        for t in range(n):
            local(t).start()
        for t in range(n):
            for cp in first(t):
                cp.start()

    def forward():
        for j, chip in enumerate(chips):
            for t in range(n):
                copy(t, 1 + j, (*chip, c), me).wait_recv()
                copy(t, 4 + j, (*chip, c), sibling).start()

    def finish():
        for t in range(n):
            copy(t, 0, sibling, me).wait_recv()
        for j, chip in enumerate(chips):
            for t in range(n):
                copy(t, 4 + j, (*chip, 1 - c), me).wait_recv()
        for t in range(n):
            for cp in first(t) + [copy(t, 4 + j, (*chip, c), sibling) for j, chip in enumerate(chips)]:
                cp.wait_send()
            local(t).wait()

    return start, forward, finish


def _gather_scratch(n):
    return [pltpu.SemaphoreType.DMA((7 * n,)), pltpu.SemaphoreType.DMA((7 * n,)), pltpu.SemaphoreType.DMA((n,))]


def _gather_out_shapes(shards):
    return [jax.ShapeDtypeStruct((4,) + s.shape[1:], s.dtype) for s in shards]


def _gather_weights(shards, layer, name):
    n = len(shards)

    def body(*refs):
        start, forward, finish = _gather_phases(layer, refs[:n], refs[n:2 * n], *refs[2 * n:])
        start()
        forward()
        finish()

    return pl.pallas_call(
        body, name=name, out_shape=_gather_out_shapes(shards), in_specs=[HBM] * n, out_specs=[HBM] * n,
        scratch_shapes=_gather_scratch(n),
    )(*shards)


def _sibling_phases(srcs, outs, send_sems, recv_sems):
    x, y, c = _position()
    copies = [pltpu.make_async_remote_copy(
        src_ref=srcs[t].at[k, 1 - c], dst_ref=outs[t].at[k], send_sem=send_sems.at[4 * t + k],
        recv_sem=recv_sems.at[4 * t + k], device_id=(x, y, 1 - c), device_id_type=MESH)
        for t in range(len(srcs)) for k in range(4)]

    def start():
        for cp in copies:
            cp.start()

    def finish():
        for cp in copies:
            cp.wait()

    return start, finish


def _sibling_scratch(n):
    return [pltpu.SemaphoreType.DMA((4 * n,)), pltpu.SemaphoreType.DMA((4 * n,))]


def _sibling_out_shapes(gs):
    return [jax.ShapeDtypeStruct((4,) + g.shape[2:], g.dtype) for g in gs]


def _rs_chips(ps, name):
    n = len(ps)

    def body(*refs):
        start, finish = _chips_phases(refs[:n], refs[n:2 * n], *refs[2 * n:])
        start()
        finish()

    return pl.pallas_call(
        body, name=name, out_shape=_chips_out_shapes(ps), in_specs=[HBM] * n, out_specs=[HBM] * n,
        scratch_shapes=_chips_scratch(n),
    )(*ps)


def _chips_phases(srcs, outs, send_sems, recv_sems):
    x, y, c = _position()
    copies = [pltpu.make_async_remote_copy(
        src_ref=srcs[t].at[2 * px + py], dst_ref=outs[t].at[j], send_sem=send_sems.at[3 * t + j],
        recv_sem=recv_sems.at[3 * t + j], device_id=(px, py, c), device_id_type=MESH)
        for t in range(len(srcs)) for j, (px, py) in enumerate(_other_chips(x, y))]

    def start():
        for cp in copies:
            cp.start()

    def finish():
        for cp in copies:
            cp.wait()

    return start, finish


def _chips_scratch(n):
    return [pltpu.SemaphoreType.DMA((3 * n,)), pltpu.SemaphoreType.DMA((3 * n,))]


def _chips_out_shapes(ps):
    return [jax.ShapeDtypeStruct((3,) + p.shape[1:], p.dtype) for p in ps]


def _rs_pair(hs, name):
    n = len(hs)

    def body(*refs):
        bufs = refs[n:2 * n]
        send_sems, recv_sems = refs[2 * n:]
        x, y, c = _position()

        def copy(t, half):
            return pltpu.make_async_remote_copy(
                src_ref=bufs[t].at[half], dst_ref=bufs[t].at[half], send_sem=send_sems.at[t], recv_sem=recv_sems.at[t],
                device_id=(x, y, 1 - c), device_id_type=MESH)

        for t in range(n):
            copy(t, c).start()
        for t in range(n):
            copy(t, 1 - c).wait_recv()
        for t in range(n):
            copy(t, c).wait_send()

    out_shape = [jax.ShapeDtypeStruct(h.shape, h.dtype) for h in hs]
    return pl.pallas_call(
        body, name=name, out_shape=out_shape, in_specs=[HBM] * n, out_specs=[HBM] * n,
        input_output_aliases={t: t for t in range(n)},
        scratch_shapes=[pltpu.SemaphoreType.DMA((n,)), pltpu.SemaphoreType.DMA((n,))],
    )(*hs)


BIG = ("w_in", "w_branch_a", "w_branch_b", "w_out", "w_ffn_in", "w_ffn_out")
CARRY_ATTN = ["w_in"]
CARRY_LOCAL = ["w_ffn_out"]
CARRY_SCAN = ["w_branch_a", "w_branch_b", "w_out"]
CARRY_GU = ["w_ffn_in"]
CARRY_DATTN = ["w_in", "w_ffn_in"]
CARRY_DLOCAL = ["w_branch_a", "w_branch_b", "w_out", "w_ffn_out"]


def _band_bias(rel_table, name, gather=None):
    L, H, n = rel_table.shape
    tab = jnp.pad(rel_table, ((0, 0), (0, 0), (0, NREL_PAD - n))).reshape(L * H, 1, NREL_PAD)
    band = (A_PAST + 1) * CHUNK

    shards, glayer = gather if gather is not None else ((), None)
    ng = len(shards)

    def body(t_ref, *rest):
        srcs, o_ref, gouts, sems = rest[:ng], rest[ng], rest[ng + 1:2 * ng + 1], rest[2 * ng + 1:]
        i = pl.program_id(0)
        done = (_carry(_gather_phases(glayer, srcs, gouts, *sems), i == 0, i == LATE * (L * H) // 8, i == L * H - 1)
                if ng else None)
        r = lax.broadcasted_iota(jnp.int32, (NREL_PAD, SKEW_W), 0)
        xi = lax.broadcasted_iota(jnp.int32, (NREL_PAD, SKEW_W), 1)
        diag = jnp.where(xi < KSPAN, xi, xi - SKEW_W)
        rel = jnp.clip(A_PAST * CHUNK - diag, -A_MAX_REL, A_MAX_REL) + A_MAX_REL
        e = _nn(t_ref[...], jnp.where(rel == r, 1.0, 0.0).astype(F32), HI)
        x = jnp.broadcast_to(e, (QBLK, SKEW_W))
        row = lax.broadcasted_iota(jnp.int32, (QBLK, SKEW_W), 0)
        for b in range(QBLK.bit_length() - 1):
            x = jnp.where(((row >> b) & 1) == 1, pltpu.roll(x, 1 << b, 1), x)
        x = x[:, :KSPAN]
        first = (lax.broadcasted_iota(jnp.int32, (QBLK, KSPAN), 0) // CHUNK) * CHUNK
        col = lax.broadcasted_iota(jnp.int32, (QBLK, KSPAN), 1)
        o_ref[...] = jnp.where((col >= first) & (col < first + band), x, NEG)
        if ng:
            done()

    res = pl.pallas_call(
        body, name=name, grid=(L * H,), in_specs=[pl.BlockSpec((None, 1, NREL_PAD), lambda i: (i, 0, 0))] + [HBM] * ng,
        out_specs=[pl.BlockSpec((None, QBLK, KSPAN), lambda i: (i, 0, 0))] + [HBM] * ng,
        out_shape=[jax.ShapeDtypeStruct((L * H, QBLK, KSPAN), F32)] + _gather_out_shapes(shards),
        scratch_shapes=_gather_scratch(ng) if ng else [], compiler_params=_arb(1),
    )(tab, *shards)
    return res[0].reshape(L, H, QBLK, KSPAN), list(res[1:])


def _col_row_forms(t, S):
    nc = S // CHUNK
    return t.T.reshape(nc, CHUNK, B_HEADS), t.reshape(B_HEADS, nc, CHUNK).transpose(1, 0, 2)


def _weight_view(name, gathered, tag):
    if name in ("w_out", "w_ffn_out"):
        return gathered.reshape(8 * gathered.shape[2], gathered.shape[3])
    stacked = gathered.reshape(4, 2 * gathered.shape[2], gathered.shape[3])
    return _reorder_w_in(stacked[None], f"w_in_cols_{tag}")[0] if name == "w_in" else stacked


def _layer_fwd(l, x, mod, W, P, big, gather=None, late=None):
    S, D = x.shape
    n = lambda s: f"{s}_l{l}"
    sh1, sc1, gt1, sh2, sc2, gt2 = (mod[i:i + 1] for i in range(6))
    h1 = _lnmod_fwd(x, P["norm1_g"][l:l + 1], sc1, sh1, n("ln1"))
    if late is None:
        proj = _matmul(h1, W["w_in"], "nn", F32, n("proj"), tn=1152)
    else:
        proj, got = _matmul(h1, W["w_in"], "nn", F32, n("proj"), tn=1152, gather=(late[1], l))
        W = {**W, **{k: _weight_view(k, t, f"l{l}") for k, t in zip(late[0], got)}}
    part = (lambda names: ([gather[0][BIG.index(k)] for k in names], gather[1])) if gather is not None else (lambda names: None)
    ya, got_a = _attn_fwd(proj, big, n("attn"), part(CARRY_ATTN))
    alog, dtb = P["a_log"][l].reshape(B_HEADS, 1), P["dt_bias"][l].reshape(B_HEADS, 1)
    beta, gam, b_t, a_t = _gdn_gates_fwd(proj, alog, dtb, n("gates"))
    bcol, _ = _col_row_forms(beta, S)
    gcol, grow = _col_row_forms(gam, S)
    qn, kn, v = _gdn_pre_fwd(proj, P["w_conv"][l], n("gdnpre"))
    tsave, amat, u, w, got_l = _gdn_local_fwd(qn, kn, v, bcol, gcol, grow, n("gdnlocal"), part(CARRY_LOCAL))
    o, ssave, got_s = _gdn_scan_fwd(qn, kn, u, w, amat, gcol, grow, n("gdnscan"), part(CARRY_SCAN))
    yb = _gdn_post_fwd(o, proj, P["gdn_norm_g"][l:l + 1], n("gdnpost"))
    pa = _matmul(ya, W["w_branch_a"], "nn", BF16, n("pa"), tm=2048, stacked=True)
    pb = _matmul(yb, W["w_branch_b"], "nn", BF16, n("pb"), tm=2048, stacked=True)
    merged = _merge_fwd(proj, pa, pb, n("merge"))
    ao = _matmul(merged, W["w_out"], "nn", F32, n("ao"))
    x1 = _gate_fwd(x, ao, gt1, n("res1"))
    h2 = _lnmod_fwd(x1, P["norm2_g"][l:l + 1], sc2, sh2, n("ln2"))
    gu = _matmul(h2, W["w_ffn_in"], "nn", BF16, n("gu"), stacked=True, gather=part(CARRY_GU))
    gu, got_g = gu if gather is not None else (gu, [])
    got = dict(zip(CARRY_ATTN + CARRY_LOCAL + CARRY_SCAN + CARRY_GU, got_a + got_l + got_s + got_g))
    gathered = {k: got[k] for k in BIG} if gather is not None else None
    act = _ffn_act_fwd(gu, n("act"))
    fo = _matmul(act, W["w_ffn_out"], "nn", F32, n("fo"), tk=1408)
    x2 = _gate_fwd(x1, fo, gt2, n("res2"))
    saved = dict(x=x, h1=h1, proj=proj, ya=ya, b_t=b_t, a_t=a_t, bcol=bcol, gcol=gcol, grow=grow,
                 qn=qn, kn=kn, v=v, o=o, tsave=tsave, ssave=ssave, amat=amat, u=u, w=w, yb=yb, pa=pa, pb=pb,
                 merged=merged, ao=ao, x1=x1,
                 h2=h2, gu=gu, act=act, fo=fo)
    return x2, saved, gathered, W


def _layer_bwd(l, dx2, sv, mod, W, P, big, exchange=()):
    S, D = dx2.shape
    n = lambda s: f"{s}_l{l}"
    sh1, sc1, gt1, sh2, sc2, gt2 = (mod[i:i + 1] for i in range(6))
    g, pay = {}, {}
    view = lambda t: t.reshape((4, 2, t.shape[-2] // (2 if t.ndim == 3 else 8), t.shape[-1]))
    dz2, dgt2 = _gate_bwd(dx2, sv["fo"], gt2, n("dres2"))
    g["w_ffn_out"], pay["w_ffn_out"] = map(view, _matmul(sv["act"], dz2, "tn", F32, n("dwfo"), tm=1408, also_bf16=True))
    dact = _matmul(dz2, W["w_ffn_out"], "nt", BF16, n("dact"), tn=1408)
    dgu = _ffn_act_bwd(sv["gu"], dact, n("dgu"))
    g["w_ffn_in"], pay["w_ffn_in"] = map(view, _matmul(sv["h2"], dgu, "tn", F32, n("dwfi"), out_stacked=True,
                                                       also_bf16=True))
    dh2 = _matmul(dgu, W["w_ffn_in"], "nt", F32, n("dh2"), stacked=True)
    dx1, dsh2, dsc2, dn2 = _lnmod_bwd(dh2, sv["x1"], P["norm2_g"][l:l + 1], sc2, dx2, n("dln2"))
    dz1, dgt1 = _gate_bwd(dx1, sv["ao"], gt1, n("dres1"))
    g["w_out"], pay["w_out"] = map(view, _matmul(sv["merged"], dz1, "tn", F32, n("dwo"), also_bf16=True))
    dmerged = _matmul(dz1, W["w_out"], "nt", BF16, n("dmerged"))
    dproj, dpa, dpb = _merge_bwd(sv["proj"], sv["pa"], sv["pb"], dmerged, n("dmerge"))
    g["w_branch_a"], pay["w_branch_a"] = map(view, _matmul(sv["ya"], dpa, "tn", F32, n("dwa"), out_stacked=True,
                                                           also_bf16=True))
    g["w_branch_b"], pay["w_branch_b"] = map(view, _matmul(sv["yb"], dpb, "tn", F32, n("dwb"), out_stacked=True,
                                                           also_bf16=True))
    dya = _matmul(dpa, W["w_branch_a"], "nt", BF16, n("dya"), tm=2048, stacked=True)
    dyb = _matmul(dpb, W["w_branch_b"], "nt", F32, n("dyb"), tm=2048, stacked=True)
    ex = (lambda names: [exchange[BIG.index(k)] for k in names]) if len(exchange) else (lambda names: ())
    dq, dk, dv, dbig, rec_a = _attn_bwd(sv["proj"], big, dya, n("dattn"), ex(CARRY_DATTN))
    g["rel_table"] = _rel_table_grad(dbig, n("drel"))[:, 0, :2 * A_MAX_REL + 1]
    dproj = _write_columns(dproj, [dq, dk, dv], 3 * dq.shape[1], P_QKVA // (3 * dq.shape[1]), n("dqkva"))
    do, dproj, dng = _gdn_post_bwd(dyb, sv["o"], sv["proj"], P["gdn_norm_g"][l:l + 1], dproj, n("dgdnpost"))
    g["gdn_norm_g"] = dng[0]
    du, dw, dqd, dkd, da, dgl = _gdn_scan_bwd(sv["qn"], sv["kn"], sv["u"], sv["w"], sv["amat"], sv["gcol"], sv["grow"],
                                              sv["ssave"], do, n("dgdnscan"))
    dqn, dkn, dvv, dbc, dgc, dgr, rec_l = _gdn_local_bwd(
        sv["qn"], sv["kn"], sv["v"], sv["bcol"], sv["gcol"], sv["grow"], sv["tsave"], du, dw, dqd, dkd, da, dgl,
        n("dgdnlocal"), ex(CARRY_DLOCAL))
    rec = dict(zip(CARRY_DATTN + CARRY_DLOCAL, rec_a + rec_l))
    received = [rec[k] for k in BIG] if len(exchange) else None
    dbeta_t = dbc.reshape(S, B_HEADS).T
    dgam_a = dgc.reshape(S, B_HEADS).T
    dgam_b = dgr.transpose(1, 0, 2).reshape(B_HEADS, S)
    alog, dtb = P["a_log"][l].reshape(B_HEADS, 1), P["dt_bias"][l].reshape(B_HEADS, 1)
    db_t, da_t, dal, ddt = _gdn_gates_bwd(dbeta_t, dgam_a, dgam_b, sv["b_t"], sv["a_t"], alog, dtb, n("dgates"))
    g["a_log"], g["dt_bias"] = dal[:, 0], ddt[:, 0]
    dproj, g["w_conv"] = _gdn_pre_bwd(sv["proj"], P["w_conv"][l], dqn, dkn, dvv, dproj, n("dgdnpre"))
    dba = jnp.concatenate([db_t.T, da_t.T, jnp.zeros((S, P_END - P_BA - 2 * B_HEADS), F32)], axis=1)
    dproj = _write_columns(dproj, [dba], dba.shape[1], P_BA // dba.shape[1], n("dba"))
    g["w_in"], pay["w_in"] = map(view, _restore_w_in(_matmul(sv["h1"], dproj, "tn", F32, n("dwin"), tn=1152),
                                                     n("dwin_cols")))
    dh1, from_sibling = _matmul(dproj, W["w_in"], "nt", F32, n("dh1"), tk=1152, sibling=[pay[k] for k in BIG])
    dx, dsh1, dsc1, dn1 = _lnmod_bwd(dh1, sv["x"], P["norm1_g"][l:l + 1], sc1, dx1, n("dln1"))
    g["norm1_g"], g["norm2_g"] = dn1[0], dn2[0]
    dmod = jnp.concatenate([dsh1, dsc1, dgt1, dsh2, dsc2, dgt2], axis=1)[0]
    return dx, g, from_sibling, dmod, received


SMALL = ("norm1_g", "norm2_g", "rel_table", "w_conv", "a_log", "dt_bias", "gdn_norm_g")
SMALL_PACK_C = 1024


def _as_rows(t):
    flat = t.reshape(-1)
    rows = -(-flat.shape[0] // SMALL_PACK_C)
    return jnp.pad(flat, (0, rows * SMALL_PACK_C - flat.shape[0])).reshape(rows, SMALL_PACK_C)


def _pack_rows(parts):
    blk = jnp.concatenate([_as_rows(p) for p in parts], axis=0)
    return jnp.pad(blk, ((0, -blk.shape[0] % 8), (0, 0)))


def _unpack_rows(blk, shapes):
    out, r = [], 0
    for shp in shapes:
        size = int(np.prod(shp))
        rows = -(-size // SMALL_PACK_C)
        out.append(blk[..., r:r + rows, :].reshape(blk.shape[:-2] + (rows * SMALL_PACK_C,))[..., :size]
                   .reshape(blk.shape[:-2] + tuple(shp)))
        r += rows
    return out


def kernel(x, c, w_ada, b_ada, norm1_g, norm2_g, w_in, rel_table, w_conv, a_log, dt_bias, gdn_norm_g, w_branch_a, w_branch_b, w_out, w_ffn_in, w_ffn_out, final_g, loss_target, m_w_ada, m_b_ada, m_norm1_g, m_norm2_g, m_w_in, m_rel_table, m_w_conv, m_a_log, m_dt_bias, m_gdn_norm_g, m_w_branch_a, m_w_branch_b, m_w_out, m_w_ffn_in, m_w_ffn_out, m_final_g, v_w_ada, v_b_ada, v_norm1_g, v_norm2_g, v_w_in, v_rel_table, v_w_conv, v_a_log, v_dt_bias, v_gdn_norm_g, v_w_branch_a, v_w_branch_b, v_w_out, v_w_ffn_in, v_w_ffn_out, v_final_g):
    weights = dict(w_ada=w_ada, b_ada=b_ada, norm1_g=norm1_g, norm2_g=norm2_g, w_in=w_in, rel_table=rel_table,
                   w_conv=w_conv, a_log=a_log, dt_bias=dt_bias, gdn_norm_g=gdn_norm_g, w_branch_a=w_branch_a,
                   w_branch_b=w_branch_b, w_out=w_out, w_ffn_in=w_ffn_in, w_ffn_out=w_ffn_out, final_g=final_g)
    mom_m = dict(w_ada=m_w_ada, b_ada=m_b_ada, norm1_g=m_norm1_g, norm2_g=m_norm2_g, w_in=m_w_in,
                 rel_table=m_rel_table, w_conv=m_w_conv, a_log=m_a_log, dt_bias=m_dt_bias, gdn_norm_g=m_gdn_norm_g,
                 w_branch_a=m_w_branch_a, w_branch_b=m_w_branch_b, w_out=m_w_out, w_ffn_in=m_w_ffn_in,
                 w_ffn_out=m_w_ffn_out, final_g=m_final_g)
    mom_v = dict(w_ada=v_w_ada, b_ada=v_b_ada, norm1_g=v_norm1_g, norm2_g=v_norm2_g, w_in=v_w_in,
                 rel_table=v_rel_table, w_conv=v_w_conv, a_log=v_a_log, dt_bias=v_dt_bias, gdn_norm_g=v_gdn_norm_g,
                 w_branch_a=v_w_branch_a, w_branch_b=v_w_branch_b, w_out=v_w_out, w_ffn_in=v_w_ffn_in,
                 w_ffn_out=v_w_ffn_out, final_g=v_final_g)
    xi, yi, ci = _position()
    chip = 2 * xi + yi
    dev = 2 * chip + ci
    L, D = norm1_g.shape
    NMOD = b_ada.shape[1] // D
    ns = w_ada.shape[2]
    cs = w_conv.shape[2]

    first_blk = _pack_rows([c, w_conv])
    first_all = _allgather8(first_blk, "gather_c").reshape(8, first_blk.shape[0], SMALL_PACK_C)
    c_all, w_conv_all = _unpack_rows(first_all, [(D,), w_conv.shape])
    w_conv_full = w_conv_all.reshape(4, 2, L, CONV_K, cs)[:, 0].transpose(1, 2, 0, 3).reshape(L, CONV_K, 4 * cs)
    b_shard = lax.dynamic_slice_in_dim(b_ada, chip * ns, ns, axis=1).reshape(L, 1, ns)
    mod_shard = _ada_mod(c_all, w_ada, b_shard, "ada_mod")
    mod_all = _allgather8(mod_shard.reshape(L * 8, ns), "gather_mod").reshape(4, 2, L, 8, ns)
    mod = lax.dynamic_index_in_dim(mod_all[:, 0], dev, axis=2, keepdims=False)
    mod = mod.transpose(1, 0, 2).reshape(L, NMOD, D)

    shards = [weights[k].astype(BF16) for k in BIG]
    shards = [s.reshape(s.shape[0], 2, s.shape[1] // 2, s.shape[2]) for s in shards]
    P = dict(norm1_g=norm1_g, norm2_g=norm2_g, w_conv=w_conv_full, a_log=a_log, dt_bias=dt_bias,
             gdn_norm_g=gdn_norm_g)
    shard_of = dict(zip(BIG, shards))

    big, got = _band_bias(rel_table, "band_bias", ([shard_of["w_in"]], 0))
    alone = ["w_branch_a", "w_branch_b", "w_out", "w_ffn_out"]
    got += _gather_weights([shard_of[k] for k in alone], 0, "gather_weights_l0")
    W = [{k: _weight_view(k, t, "l0") for k, t in zip(["w_in"] + alone, got)}]
    late = (["w_ffn_in"], [shard_of["w_ffn_in"]])
    xc = x[0]
    saved = []
    for l in range(L):
        xc, sv, gathered, W[l] = _layer_fwd(l, xc, mod[l], W[l], P, big[l], (shards, l + 1) if l + 1 < L else None,
                                           late if l == 0 else None)
        saved.append(sv)
        if l + 1 < L:
            W.append({k: _weight_view(k, gathered[k], f"l{l + 1}") for k in BIG})
    dx, loss_dev, dfinal = _loss_head(xc, final_g.reshape(1, D), loss_target[0], "loss_head")

    where = jnp.stack([ci, chip]).astype(jnp.int32)
    grads = [None] * L
    dmods = [None] * L
    shard_grads = {k: [None] * L for k in BIG}

    def finish_reduce_scatter(l, sums, from_chips):
        halves = [_sum_own_and_received(s_[1], r_, where, f"rs_sum_{k}_l{l}")
                  for k, s_, r_ in zip(BIG, sums, from_chips)]
        for k, t in zip(BIG, _rs_pair(halves, f"rs_pair_l{l}")):
            shard_grads[k][l] = t.reshape(2 * t.shape[1], t.shape[2])

    pending = None
    for l in reversed(range(L)):
        exchange = [s_[0] for s_ in pending] if pending is not None else ()
        dx, grads[l], from_sibling, dmods[l], received = _layer_bwd(l, dx, saved[l], mod[l], W[l], P, big[l], exchange)
        if pending is not None:
            finish_reduce_scatter(l + 1, pending, received)
        gs = [grads[l][k] for k in BIG]
        pending = [_pair_sums(g_, where, r_, f"rs_pair_sum_{k}_l{l}") for k, g_, r_ in zip(BIG, gs, from_sibling)]
    finish_reduce_scatter(0, pending, _rs_chips([s_[0] for s_ in pending], "rs_chips_l0"))
    dmod = jnp.stack(dmods)

    small = {k: jnp.stack([grads[l][k] for l in range(L)]) for k in SMALL}
    parts = [dmod] + [small[k] for k in SMALL] + [dfinal, loss_dev[0, :1]]
    small_blk = _pack_rows(parts)
    srows = small_blk.shape[0]
    small_all, small_sum = _allgather8(small_blk, "gather_small", reduce_rows=srows)
    shapes = [dmod.shape] + [small[k].shape for k in SMALL] + [(D,), (1,)]
    tot = _unpack_rows(small_sum, shapes)
    G = dict(zip(SMALL, tot[1:1 + len(SMALL)]))
    G["b_ada"] = tot[0].reshape(b_ada.shape)
    G["w_conv"] = lax.dynamic_slice_in_dim(G["w_conv"], chip * cs, cs, axis=2)
    G["final_g"] = tot[-2]
    loss = tot[-1][0]
    dmod_all = _unpack_rows(small_all.reshape(8, srows, SMALL_PACK_C), [dmod.shape])[0]
    dmod_cols = lax.dynamic_slice_in_dim(dmod_all, chip * ns, ns, axis=2).transpose(1, 0, 2)
    G["w_ada"] = _ada_wgrad(c_all, dmod_cols, "ada_wgrad")

    order = ["w_ada", "b_ada", "norm1_g", "norm2_g", "w_in", "rel_table", "w_conv", "a_log", "dt_bias", "gdn_norm_g",
             "w_branch_a", "w_branch_b", "w_out", "w_ffn_in", "w_ffn_out", "final_g"]
    deltas, new_m, new_v = {}, {}, {}
    for k in order:
        w = weights[k]
        if k == "w_in":
            to_cols = lambda t: jnp.transpose(t, (2, 0, 1))
            from_cols = lambda t: jnp.transpose(t, (1, 2, 0))
            gt = to_cols(jnp.stack(shard_grads[k]))
            d_, m_, v_ = _adamw_lead(to_cols(w), gt, to_cols(mom_m[k]), to_cols(mom_v[k]), f"adamw_{k}",
                                     W_IN_SHARD // 30)
            G[k], deltas[k], new_m[k], new_v[k] = from_cols(gt), from_cols(d_), from_cols(m_), from_cols(v_)
            continue
        if k in BIG:
            G[k], deltas[k], new_m[k], new_v[k] = _adamw_layers(w, shard_grads[k], mom_m[k], mom_v[k], f"adamw_{k}")
            continue
        as2d = (lambda t: t.reshape(1, -1)) if w.ndim == 1 else (lambda t: t)
        d_, m_, v_ = _adamw(as2d(w), as2d(G[k]), as2d(mom_m[k]), as2d(mom_v[k]), f"adamw_{k}")
        deltas[k], new_m[k], new_v[k] = d_.reshape(w.shape), m_.reshape(w.shape), v_.reshape(w.shape)
    return (loss, dx[None], *[G[k] for k in order], *[deltas[k] for k in order], *[new_m[k] for k in order],
            *[new_v[k] for k in order])
```

```python
import functools

import numpy as np
import jax
import jax.numpy as jnp
from jax import lax
from jax.experimental import pallas as pl
from jax.experimental.pallas import tpu as pltpu

F32 = jnp.float32
BF16 = jnp.bfloat16
HI = lax.Precision.HIGHEST
SOLVE_PREC = lax.Precision.HIGH
MESH = pl.DeviceIdType.MESH

EPS = 1e-6
CHUNK = 64
A_HEADS = 8
A_DH = 64
A_PAST = 8
A_MAX_REL = 128
B_HEADS = 4
B_DH = 128
CONV_K = 4
LANE = 128
QBLK = 4 * CHUNK
KSPAN = QBLK + A_PAST * CHUNK
NEG = -1e30

ADAM_LR = 0.001
ADAM_B1 = 0.9
ADAM_B2 = 0.999
ADAM_EPS = 1e-08
ADAM_WD = 0.01
ADAM_STEP = 10

P_QKVA, P_QKVB, P_GA, P_GB, P_Z, P_BA, P_END = 0, 1536, 3072, 4096, 5120, 5632, 5760
W_IN_SHARD = 1410


def _sigmoid(x):
    return 1.0 / (1.0 + jnp.exp(-x))


def _dot(a, b, ca, cb, prec):
    lead = a.ndim - 2
    batch = ((0,), (0,)) if lead else ((), ())
    return lax.dot_general(a, b, (((ca + lead,), (cb + lead,)), batch), precision=prec, preferred_element_type=F32)


def _nn(a, b, prec=None):
    return _dot(a, b, 1, 0, prec)


def _nt(a, b, prec=None):
    return _dot(a, b, 1, 1, prec)


def _tn(a, b, prec=None):
    return _dot(a, b, 0, 0, prec)


def _bnn(a, b):
    return _nn(a.astype(BF16), b.astype(BF16))


def _bnt(a, b):
    return _nt(a.astype(BF16), b.astype(BF16))


def _btn(a, b):
    return _tn(a.astype(BF16), b.astype(BF16))


def _pick(n, target, unit=LANE):
    best = None
    for t in range(unit, min(n, target) + 1, unit):
        if n % t == 0:
            best = t
    return best if best is not None else n


def _acc(ref, val, i):
    @pl.when(i == 0)
    def _():
        ref[...] = val

    @pl.when(i != 0)
    def _():
        ref[...] += val


def _arb(n):
    return pltpu.CompilerParams(dimension_semantics=("arbitrary",) * n)


def _par(n):
    return pltpu.CompilerParams(dimension_semantics=("parallel",) * n)


def _matmul(a, b, mode, out_dtype, name, tm=1024, tn=1024, tk=1024, layer=None, stacked=False, out_stacked=False,
            also_bf16=False, gather=None, sibling=None):
    bs = b.shape[1:] if layer is not None else b.shape
    if mode == "nn":
        M, K = a.shape
        N = 4 * bs[2] if stacked else bs[1]
        if stacked:
            tn = bs[2]
    elif mode == "nt":
        M, K = a.shape
        N = bs[1] if stacked else bs[0]
        if stacked:
            tk = bs[2]
    else:
        K, M = a.shape
        N = bs[1]
        if out_stacked:
            tn = N // 4
    tm, tn, tk = _pick(M, tm), _pick(N, tn), _pick(K, tk)
    nk = K // tk
    lead = () if layer is None else (layer,)
    lead_blk = () if layer is None else (None,)
    if mode == "nn":
        a_spec = pl.BlockSpec((tm, tk), lambda i, j, k: (i, k))
        if stacked:
            b_spec = pl.BlockSpec(lead_blk + (None, tk, tn), lambda i, j, k: lead + (j, k, 0))
        else:
            b_spec = pl.BlockSpec(lead_blk + (tk, tn), lambda i, j, k: lead + (k, j))
        dot = _nn
    elif mode == "nt":
        a_spec = pl.BlockSpec((tm, tk), lambda i, j, k: (i, k))
        if stacked:
            b_spec = pl.BlockSpec(lead_blk + (None, tn, tk), lambda i, j, k: lead + (k, j, 0))
        else:
            b_spec = pl.BlockSpec(lead_blk + (tn, tk), lambda i, j, k: lead + (j, k))
        dot = _nt
    else:
        a_spec = pl.BlockSpec((tk, tm), lambda i, j, k: (k, i))
        b_spec = pl.BlockSpec((tk, tn), lambda i, j, k: (k, j))
        dot = _tn
    if out_stacked:
        o_spec = pl.BlockSpec((None, tm, tn), lambda i, j, k: (j, i, 0))
        o_shape = jax.ShapeDtypeStruct((4, M, tn), out_dtype)
    else:
        o_spec = pl.BlockSpec((tm, tn), lambda i, j, k: (i, j))
        o_shape = jax.ShapeDtypeStruct((M, N), out_dtype)

    if gather is not None:
        shards, glayer = gather
        carried_shapes, carried_scratch = _gather_out_shapes(shards), _gather_scratch(len(shards))
    elif sibling is not None:
        shards = sibling
        carried_shapes, carried_scratch = _sibling_out_shapes(shards), _sibling_scratch(len(shards))
    else:
        shards, carried_shapes, carried_scratch = (), [], []
    ng = len(shards)
    o_shapes = [o_shape] + ([jax.ShapeDtypeStruct(o_shape.shape, BF16)] if also_bf16 else [])
    no = len(o_shapes)
    gi, gj = M // tm, N // tn

    def write(o_refs, val):
        for o_ref in o_refs:
            o_ref[...] = val.astype(o_ref.dtype)

    def body(a_ref, b_ref, *refs):
        srcs, o_refs, gouts, scratch = refs[:ng], refs[ng:ng + no], refs[ng + no:2 * ng + no], refs[2 * ng + no:]
        i, j, k = pl.program_id(0), pl.program_id(1), pl.program_id(2)
        if ng:
            start = (j == 0) & (k == 0)
            sems = scratch[len(scratch) - len(carried_scratch):]
            phases = (_gather_phases(glayer, srcs, gouts, *sems) if gather is not None
                      else _sibling_phases(srcs, gouts, *sems))
            done = _carry(phases, (i == 0) & start, (i == gi - 1) & start,
                          (i == gi - 1) & (j == gj - 1) & (k == nk - 1))
        if nk == 1:
            write(o_refs, dot(a_ref[...], b_ref[...]))
        else:
            acc_ref = scratch[0]

            @pl.when(k == 0)
            def _():
                acc_ref[...] = jnp.zeros_like(acc_ref)

            acc_ref[...] += dot(a_ref[...], b_ref[...])

            @pl.when(k == nk - 1)
            def _():
                write(o_refs, acc_ref[...])
        if ng:
            done()

    sem = ("arbitrary",) * 3 if ng else ("parallel", "parallel", "arbitrary")
    res = pl.pallas_call(
        body, name=name, grid=(gi, gj, nk), in_specs=[a_spec, b_spec] + [HBM] * ng,
        out_specs=[o_spec] * no + [HBM] * ng, out_shape=o_shapes + carried_shapes,
        scratch_shapes=([] if nk == 1 else [pltpu.VMEM((tm, tn), F32)]) + carried_scratch,
        compiler_params=pltpu.CompilerParams(dimension_semantics=sem),
    )(a, b, *shards)
    out = tuple(res[:no]) if also_bf16 else res[0]
    return (out, list(res[no:])) if ng else out


def _rows(tm, n, col=0):
    return pl.BlockSpec((tm, n), lambda i: (i, col))


def _vec(n):
    return pl.BlockSpec((1, n), lambda i: (0, 0))


def _lnmod_fwd(x, g, sc, sh, name):
    S, D = x.shape
    tm = _pick(S, 512, 8)

    def body(x_ref, g_ref, sc_ref, sh_ref, o_ref):
        xv = x_ref[...]
        r = lax.rsqrt(jnp.mean(xv * xv, axis=-1, keepdims=True) + EPS)
        o_ref[...] = ((xv * r * g_ref[...]) * (1.0 + sc_ref[...]) + sh_ref[...]).astype(BF16)

    return pl.pallas_call(
        body, name=name, grid=(S // tm,),
        in_specs=[_rows(tm, D), _vec(D), _vec(D), _vec(D)], out_specs=_rows(tm, D),
        out_shape=jax.ShapeDtypeStruct((S, D), BF16), compiler_params=_par(1),
    )(x, g, sc, sh)


def _lnmod_bwd(dh, x, g, sc, dres, name):
    S, D = x.shape
    tm = _pick(S, 512, 8)

    def body(dh_ref, x_ref, g_ref, sc_ref, dres_ref, dx_ref, dsh_ref, dsc_ref, dg_ref):
        i = pl.program_id(0)
        xv = x_ref[...]
        dh_ = dh_ref[...]
        r = lax.rsqrt(jnp.mean(xv * xv, axis=-1, keepdims=True) + EPS)
        xhat = xv * r
        gv = g_ref[...]
        dn = dh_ * (1.0 + sc_ref[...])
        dxhat = dn * gv
        dx_ref[...] = dres_ref[...] + r * (dxhat - xhat * jnp.mean(dxhat * xhat, axis=-1, keepdims=True))
        _acc(dsh_ref, jnp.sum(dh_, axis=0, keepdims=True), i)
        _acc(dsc_ref, jnp.sum(dh_ * (xhat * gv), axis=0, keepdims=True), i)
        _acc(dg_ref, jnp.sum(dn * xhat, axis=0, keepdims=True), i)

    return pl.pallas_call(
        body, name=name, grid=(S // tm,),
        in_specs=[_rows(tm, D), _rows(tm, D), _vec(D), _vec(D), _rows(tm, D)],
        out_specs=[_rows(tm, D), _vec(D), _vec(D), _vec(D)],
        out_shape=[jax.ShapeDtypeStruct((S, D), F32)] + [jax.ShapeDtypeStruct((1, D), F32)] * 3,
        compiler_params=_arb(1),
    )(dh, x, g, sc, dres)


def _gate_fwd(x, y, gt, name):
    S, D = x.shape
    tm = _pick(S, 512, 8)

    def body(x_ref, y_ref, gt_ref, o_ref):
        o_ref[...] = x_ref[...] + gt_ref[...] * y_ref[...]

    return pl.pallas_call(
        body, name=name, grid=(S // tm,), in_specs=[_rows(tm, D), _rows(tm, D), _vec(D)], out_specs=_rows(tm, D),
        out_shape=jax.ShapeDtypeStruct((S, D), F32), compiler_params=_par(1),
    )(x, y, gt)


def _gate_bwd(dx, y, gt, name):
    S, D = dx.shape
    tm = _pick(S, 512, 8)

    def body(dx_ref, y_ref, gt_ref, dz_ref, dgt_ref):
        i = pl.program_id(0)
        d = dx_ref[...]
        dz_ref[...] = (d * gt_ref[...]).astype(BF16)
        _acc(dgt_ref, jnp.sum(d * y_ref[...], axis=0, keepdims=True), i)

    return pl.pallas_call(
        body, name=name, grid=(S // tm,), in_specs=[_rows(tm, D), _rows(tm, D), _vec(D)],
        out_specs=[_rows(tm, D), _vec(D)],
        out_shape=[jax.ShapeDtypeStruct((S, D), BF16), jax.ShapeDtypeStruct((1, D), F32)],
        compiler_params=_arb(1),
    )(dx, y, gt)


def _ffn_act_fwd(gu, name):
    S, H2 = gu.shape
    H = H2 // 2
    tm = _pick(S, 256, 8)

    def body(g_ref, u_ref, o_ref):
        gv = g_ref[...].astype(F32)
        o_ref[...] = (gv * _sigmoid(gv) * u_ref[...].astype(F32)).astype(BF16)

    return pl.pallas_call(
        body, name=name, grid=(S // tm,), in_specs=[_rows(tm, H, 0), _rows(tm, H, 1)], out_specs=_rows(tm, H),
        out_shape=jax.ShapeDtypeStruct((S, H), BF16), compiler_params=_par(1),
    )(gu, gu)


def _ffn_act_bwd(gu, dact, name):
    S, H2 = gu.shape
    H = H2 // 2
    tm = _pick(S, 256, 8)

    def body(g_ref, u_ref, da_ref, o_ref):
        gv = g_ref[...].astype(F32)
        s = _sigmoid(gv)
        da = da_ref[...].astype(F32)
        o_ref[:, :H] = (da * u_ref[...].astype(F32) * (s * (1.0 + gv * (1.0 - s)))).astype(BF16)
        o_ref[:, H:] = (da * (gv * s)).astype(BF16)

    return pl.pallas_call(
        body, name=name, grid=(S // tm,), in_specs=[_rows(tm, H, 0), _rows(tm, H, 1), _rows(tm, H)],
        out_specs=_rows(tm, H2), out_shape=jax.ShapeDtypeStruct((S, H2), BF16), compiler_params=_par(1),
    )(gu, gu, dact)


def _merge_fwd(proj, pa, pb, name):
    S, D = pa.shape
    tm = _pick(S, 512, 8)

    def body(ga_ref, gb_ref, pa_ref, pb_ref, o_ref):
        o_ref[...] = (_sigmoid(ga_ref[...]) * pa_ref[...].astype(F32)
                      + _sigmoid(gb_ref[...]) * pb_ref[...].astype(F32)).astype(BF16)

    return pl.pallas_call(
        body, name=name, grid=(S // tm,),
        in_specs=[_rows(tm, D, P_GA // D), _rows(tm, D, P_GB // D), _rows(tm, D), _rows(tm, D)],
        out_specs=_rows(tm, D), out_shape=jax.ShapeDtypeStruct((S, D), BF16), compiler_params=_par(1),
    )(proj, proj, pa, pb)


def _merge_bwd(proj, pa, pb, dm, name):
    S, D = pa.shape
    tm = _pick(S, 512, 16)
    rows_j = pl.BlockSpec((tm, D), lambda i, j: (i, 0))

    def body(g_ref, pa_ref, pb_ref, dm_ref, dg_ref, dpa_ref, dpb_ref):
        d = dm_ref[...].astype(F32)
        s = _sigmoid(g_ref[...])
        for branch, p_ref, dp_ref in ((0, pa_ref, dpa_ref), (1, pb_ref, dpb_ref)):
            @pl.when(pl.program_id(1) == branch)
            def _():
                dg_ref[...] = (d * p_ref[...].astype(F32) * s * (1.0 - s)).astype(BF16)
                dp_ref[...] = (d * s).astype(BF16)

    return pl.pallas_call(
        body, name=name, grid=(S // tm, 2),
        in_specs=[pl.BlockSpec((tm, D), lambda i, j: (i, P_GA // D + j)), rows_j, rows_j, rows_j],
        out_specs=[pl.BlockSpec((tm, D), lambda i, j: (i, P_GA // D + j)), rows_j, rows_j],
        out_shape=[jax.ShapeDtypeStruct((S, P_END), BF16), jax.ShapeDtypeStruct((S, D), BF16),
                   jax.ShapeDtypeStruct((S, D), BF16)],
        compiler_params=_arb(2),
    )(proj, pa, pb, dm)


def _write_columns(buf, parts, width, colblk, name):
    S = buf.shape[0]
    tm = _pick(S, 512, 16)
    n = len(parts)

    def body(*refs):
        o_ref = refs[n + 1]
        off = 0
        for p_ref in refs[:n]:
            w = p_ref.shape[1]
            o_ref[:, off:off + w] = p_ref[...].astype(BF16)
            off += w

    return pl.pallas_call(
        body, name=name, grid=(S // tm,), in_specs=[_rows(tm, p.shape[1]) for p in parts] + [HBM],
        out_specs=_rows(tm, width, colblk), out_shape=jax.ShapeDtypeStruct(buf.shape, buf.dtype),
        input_output_aliases={n: 0}, compiler_params=_par(1),
    )(*parts, buf)


def _loss_head(x, g, target, name):
    S, D = x.shape
    tm = _pick(S, 512, 8)

    def body(x_ref, g_ref, t_ref, dx_ref, loss_ref, dg_ref):
        i = pl.program_id(0)
        xv = x_ref[...]
        gv = g_ref[...]
        r = lax.rsqrt(jnp.mean(xv * xv, axis=-1, keepdims=True) + EPS)
        xhat = xv * r
        err = xhat * gv - t_ref[...]
        part = 0.5 * jnp.sum(jnp.mean(err * err, axis=-1, keepdims=True), axis=0, keepdims=True)
        _acc(loss_ref, jnp.broadcast_to(part, (1, LANE)), i)
        dy = err * (1.0 / D)
        _acc(dg_ref, jnp.sum(dy * xhat, axis=0, keepdims=True), i)
        dxhat = dy * gv
        dx_ref[...] = r * (dxhat - xhat * jnp.mean(dxhat * xhat, axis=-1, keepdims=True))

    return pl.pallas_call(
        body, name=name, grid=(S // tm,), in_specs=[_rows(tm, D), _vec(D), _rows(tm, D)],
        out_specs=[_rows(tm, D), _vec(LANE), _vec(D)],
        out_shape=[jax.ShapeDtypeStruct((S, D), F32), jax.ShapeDtypeStruct((1, LANE), F32),
                   jax.ShapeDtypeStruct((1, D), F32)],
        compiler_params=_arb(1),
    )(x, g, target)


HEADS_PER_SLAB = LANE // A_DH
N_SLABS = A_HEADS // HEADS_PER_SLAB
SPAN_BLOCKS = KSPAN // QBLK


def _attn_specs(seg):
    q_spec = pl.BlockSpec((QBLK, LANE), lambda p, m: (m, seg[0] * N_SLABS + p))
    k_specs = [pl.BlockSpec((QBLK, LANE), functools.partial(
        lambda j, p, m: (jnp.maximum(m - (SPAN_BLOCKS - 1) + j, 0), seg[1] * N_SLABS + p), j)) for j in range(SPAN_BLOCKS)]
    v_specs = [pl.BlockSpec((QBLK, LANE), functools.partial(
        lambda j, p, m: (jnp.maximum(m - (SPAN_BLOCKS - 1) + j, 0), seg[2] * N_SLABS + p), j)) for j in range(SPAN_BLOCKS)]
    b_spec = pl.BlockSpec((HEADS_PER_SLAB, QBLK, KSPAN), lambda p, m: (p, 0, 0))
    return q_spec, k_specs, v_specs, b_spec


def _head_lanes(t, hh):
    lane = lax.broadcasted_iota(jnp.int32, t.shape, 1)
    return jnp.where((lane // A_DH) == hh, t, jnp.zeros_like(t))


def _front_mask(m):
    col = lax.broadcasted_iota(jnp.int32, (QBLK, KSPAN), 1)
    return jnp.where(col < (SPAN_BLOCKS - 1 - m) * QBLK, NEG, 0.0)


def _attn_probs(qk, bias, front):
    s = qk * (A_DH ** -0.5) + (bias + front)
    p = jnp.exp(s - jnp.max(s, axis=-1, keepdims=True))
    return p * (1.0 / jnp.sum(p, axis=-1, keepdims=True))


def _grid_ends(nq):
    p, m = pl.program_id(0), pl.program_id(1)
    return (p == 0) & (m == 0), (p == N_SLABS - 1) & (m == nq // 2), (p == N_SLABS - 1) & (m == nq - 1)


def _attn_fwd(proj, big, name, gather=None):
    S = proj.shape[0]
    q_spec, k_specs, v_specs, b_spec = _attn_specs((0, 1, 2))
    shards, layer = gather if gather is not None else ((), None)
    ng = len(shards)

    def body(q_ref, k0, k1, k2, v0, v1, v2, b_ref, *rest):
        srcs, o_ref, gouts, sems = rest[:ng], rest[ng], rest[ng + 1:2 * ng + 1], rest[2 * ng + 1:]
        done = _carry(_gather_phases(layer, srcs, gouts, *sems), *_grid_ends(S // QBLK)) if ng else None
        m = pl.program_id(1)
        q = q_ref[...].astype(BF16)
        k = jnp.concatenate([k0[...], k1[...], k2[...]], axis=0).astype(BF16)
        v = jnp.concatenate([v0[...], v1[...], v2[...]], axis=0).astype(BF16)
        front = _front_mask(m)
        heads = range(HEADS_PER_SLAB)
        scores = [_nt(_head_lanes(q, hh), k) for hh in heads]
        probs = [_attn_probs(scores[hh], b_ref[hh], front).astype(BF16) for hh in heads]
        outs = [_nn(probs[hh], v) for hh in heads]
        lane = lax.broadcasted_iota(jnp.int32, (QBLK, LANE), 1)
        o_ref[...] = jnp.where(lane < A_DH, outs[0], outs[1]).astype(BF16)
        if ng:
            done()

    res = pl.pallas_call(
        body, name=name, grid=(N_SLABS, S // QBLK), in_specs=[q_spec] + k_specs + v_specs + [b_spec] + [HBM] * ng,
        out_specs=[pl.BlockSpec((QBLK, LANE), lambda p, m: (m, p))] + [HBM] * ng,
        out_shape=[jax.ShapeDtypeStruct((S, A_HEADS * A_DH), BF16)] + _gather_out_shapes(shards),
        scratch_shapes=_gather_scratch(ng) if ng else [], compiler_params=_arb(2),
    )(proj, proj, proj, proj, proj, proj, proj, big, *shards)
    return res[0], list(res[1:])


def _attn_bwd(proj, big, dya, name, exchange=()):
    S = proj.shape[0]
    W = A_HEADS * A_DH
    q_spec, k_specs, v_specs, b_spec = _attn_specs((0, 1, 2))
    out_q = pl.BlockSpec((QBLK, LANE), lambda p, m: (m, p))
    out_kv = pl.BlockSpec((S, LANE), lambda p, m: (0, p))
    ne = len(exchange)

    def body(q_ref, k0, k1, k2, v0, v1, v2, b_ref, do_ref, *rest):
        srcs, (dq_ref, dk_ref, dv_ref, db_ref), eouts, sems = rest[:ne], rest[ne:ne + 4], rest[ne + 4:2 * ne + 4], rest[2 * ne + 4:]
        done = _carry(_chips_phases(srcs, eouts, *sems), *_grid_ends(S // QBLK)) if ne else None
        m = pl.program_id(1)

        @pl.when(m == 0)
        def _():
            dk_ref[...] = jnp.zeros_like(dk_ref)
            dv_ref[...] = jnp.zeros_like(dv_ref)
            db_ref[...] = jnp.zeros_like(db_ref)

        q = q_ref[...].astype(BF16)
        k = jnp.concatenate([k0[...], k1[...], k2[...]], axis=0).astype(BF16)
        v = jnp.concatenate([v0[...], v1[...], v2[...]], axis=0).astype(BF16)
        do = do_ref[...]
        front = _front_mask(m)
        heads = range(HEADS_PER_SLAB)
        qh = [_head_lanes(q, hh) for hh in heads]
        doh = [_head_lanes(do, hh) for hh in heads]
        scores = [_nt(qh[hh], k) for hh in heads]
        dps = [_nt(doh[hh], v) for hh in heads]
        ps = [_attn_probs(scores[hh], b_ref[hh], front) for hh in heads]
        dss = [ps[hh] * (dps[hh] - jnp.sum(ps[hh] * dps[hh], axis=-1, keepdims=True)) for hh in heads]
        for hh in heads:
            db_ref[hh] += dss[hh]
        dsb = [(dss[hh] * (A_DH ** -0.5)).astype(BF16) for hh in heads]
        dqs = [_nn(dsb[hh], k) for hh in heads]
        dk = sum(_tn(dsb[hh], qh[hh]) for hh in heads)
        dv = sum(_tn(ps[hh].astype(BF16), doh[hh]) for hh in heads)
        lane = lax.broadcasted_iota(jnp.int32, (QBLK, LANE), 1)
        dq_ref[...] = jnp.where(lane < A_DH, dqs[0], dqs[1])
        for j in range(SPAN_BLOCKS):
            blk = m - (SPAN_BLOCKS - 1) + j

            @pl.when(blk >= 0)
            def _():
                off = pl.multiple_of(blk * QBLK, QBLK)
                dk_ref[pl.ds(off, QBLK), :] += dk[j * QBLK:(j + 1) * QBLK]
                dv_ref[pl.ds(off, QBLK), :] += dv[j * QBLK:(j + 1) * QBLK]
        if ne:
            done()

    res = pl.pallas_call(
        body, name=name, grid=(N_SLABS, S // QBLK),
        in_specs=[q_spec] + k_specs + v_specs + [b_spec, pl.BlockSpec((QBLK, LANE), lambda p, m: (m, p))] + [HBM] * ne,
        out_specs=[out_q, out_kv, out_kv, b_spec] + [HBM] * ne,
        out_shape=[jax.ShapeDtypeStruct((S, W), F32)] * 3 + [jax.ShapeDtypeStruct((A_HEADS, QBLK, KSPAN), F32)]
        + _chips_out_shapes(exchange),
        scratch_shapes=_chips_scratch(ne) if ne else [], compiler_params=_arb(2),
    )(proj, proj, proj, proj, proj, proj, proj, big, dya, *exchange)
    return tuple(res[:4]) + (list(res[4:]),)


NREL_PAD = 3 * LANE
SKEW_W = 1024


def _rel_table_grad(dbig, name):
    H, R, C = dbig.shape

    def body(d_ref, o_ref):
        x = jnp.concatenate([d_ref[...], jnp.zeros((R, SKEW_W - C), F32)], axis=1)
        row = lax.broadcasted_iota(jnp.int32, (R, SKEW_W), 0)
        for b in range(R.bit_length() - 1):
            x = jnp.where(((row >> b) & 1) == 1, pltpu.roll(x, SKEW_W - (1 << b), 1), x)
        e = jnp.sum(x, axis=0, keepdims=True)
        xi = lax.broadcasted_iota(jnp.int32, (SKEW_W, NREL_PAD), 0)
        r = lax.broadcasted_iota(jnp.int32, (SKEW_W, NREL_PAD), 1)
        diag = jnp.where(xi < C, xi, xi - SKEW_W)
        rel = jnp.clip(A_PAST * CHUNK - diag, -A_MAX_REL, A_MAX_REL) + A_MAX_REL
        o_ref[...] = _nn(e, jnp.where(rel == r, 1.0, 0.0).astype(F32), HI)

    return pl.pallas_call(
        body, name=name, grid=(H,), in_specs=[pl.BlockSpec((None, R, C), lambda h: (h, 0, 0))],
        out_specs=pl.BlockSpec((None, 1, NREL_PAD), lambda h: (h, 0, 0)),
        out_shape=jax.ShapeDtypeStruct((H, 1, NREL_PAD), F32), compiler_params=_par(1),
    )(dbig)


def _chunk_cumsum_matrix(n, reverse):
    j = lax.broadcasted_iota(jnp.int32, (n, n), 0)
    i = lax.broadcasted_iota(jnp.int32, (n, n), 1)
    same = (j // CHUNK) == (i // CHUNK)
    return jnp.where(same & ((j >= i) if reverse else (j <= i)), 1.0, 0.0).astype(F32)


def _gdn_gates_fwd(proj, alog, dtb, name):
    Hh, S = B_HEADS, proj.shape[0]
    tl = _pick(S, 512)
    row = pl.BlockSpec((Hh, tl), lambda i: (0, i))
    col = pl.BlockSpec((Hh, 1), lambda i: (0, 0))

    def body(ba_ref, al_ref, dt_ref, beta_ref, gam_ref, b_ref, a_ref, t_ref):
        t_ref[...] = ba_ref[...].T
        b = t_ref[0:Hh, :]
        a = t_ref[Hh:2 * Hh, :]
        z = a + dt_ref[...]
        sp = jnp.maximum(z, 0.0) + jnp.log(1.0 + jnp.exp(-jnp.abs(z)))
        g = -jnp.exp(al_ref[...]) * sp
        beta_ref[...] = _sigmoid(b)
        gam_ref[...] = _nn(g, _chunk_cumsum_matrix(tl, False), HI)
        b_ref[...] = b
        a_ref[...] = a

    return pl.pallas_call(
        body, name=name, grid=(S // tl,), in_specs=[pl.BlockSpec((tl, LANE), lambda i: (i, P_BA // LANE)), col, col],
        out_specs=[row] * 4, out_shape=[jax.ShapeDtypeStruct((Hh, S), F32)] * 4,
        scratch_shapes=[pltpu.VMEM((LANE, tl), F32)], compiler_params=_par(1),
    )(proj, alog, dtb)


def _gdn_gates_bwd(dbeta, dgam_a, dgam_b, b_t, a_t, alog, dtb, name):
    Hh, S = b_t.shape
    tl = _pick(S, 512)
    row = pl.BlockSpec((Hh, tl), lambda i: (0, i))
    col = pl.BlockSpec((Hh, 1), lambda i: (0, 0))
    accs = pl.BlockSpec((Hh, LANE), lambda i: (0, 0))

    def body(dbeta_ref, dga_ref, dgb_ref, b_ref, a_ref, al_ref, dt_ref, db_ref, da_ref, dal_ref, ddt_ref):
        i = pl.program_id(0)
        z = a_ref[...] + dt_ref[...]
        sp = jnp.maximum(z, 0.0) + jnp.log(1.0 + jnp.exp(-jnp.abs(z)))
        ea = jnp.exp(al_ref[...])
        dg = _nn(dga_ref[...] + dgb_ref[...], _chunk_cumsum_matrix(tl, True), HI)
        da = dg * (-ea) * _sigmoid(z)
        beta = _sigmoid(b_ref[...])
        db_ref[...] = dbeta_ref[...] * beta * (1.0 - beta)
        da_ref[...] = da
        _acc(dal_ref, jnp.broadcast_to(jnp.sum(dg * (-ea * sp), axis=1, keepdims=True), (Hh, LANE)), i)
        _acc(ddt_ref, jnp.broadcast_to(jnp.sum(da, axis=1, keepdims=True), (Hh, LANE)), i)

    return pl.pallas_call(
        body, name=name, grid=(S // tl,), in_specs=[row] * 5 + [col, col], out_specs=[row, row, accs, accs],
        out_shape=[jax.ShapeDtypeStruct((Hh, S), F32)] * 2 + [jax.ShapeDtypeStruct((Hh, LANE), F32)] * 2,
        compiler_params=_arb(1),
    )(dbeta, dgam_a, dgam_b, b_t, a_t, alog, dtb)


HALO = 8


def _conv_silu(xx_ref, w_ref, tm):
    y = w_ref[0:1, :] * xx_ref[pl.ds(HALO - CONV_K + 1, tm), :]
    for j in range(1, CONV_K):
        y = y + w_ref[j:j + 1, :] * xx_ref[pl.ds(HALO - CONV_K + 1 + j, tm), :]
    return y, y * _sigmoid(y)


def _fill_prev_halo(xx_ref, x_ref, prev_ref, i, tm):
    xx_ref[pl.ds(HALO, tm), :] = x_ref[...]

    @pl.when(i == 0)
    def _():
        xx_ref[pl.ds(0, HALO), :] = jnp.zeros((HALO, xx_ref.shape[1]), F32)

    @pl.when(i != 0)
    def _():
        xx_ref[pl.ds(0, HALO), :] = prev_ref[...]


def _gdn_pre_specs(tm, C, colblk):
    cur = pl.BlockSpec((tm, C), lambda i: (i, colblk))
    prev = pl.BlockSpec((HALO, C), lambda i: (jnp.maximum(i * (tm // HALO) - 1, 0), colblk))
    return cur, prev


def _gdn_pre_fwd(proj, wconv, name):
    S = proj.shape[0]
    C = 3 * B_HEADS * B_DH
    W = B_HEADS * B_DH
    tm = _pick(S, 256, 8)
    cur, prev = _gdn_pre_specs(tm, C, P_QKVB // C)

    def body(x_ref, prev_ref, w_ref, q_ref, k_ref, v_ref, xx_ref):
        i = pl.program_id(0)
        _fill_prev_halo(xx_ref, x_ref, prev_ref, i, tm)
        _, sl = _conv_silu(xx_ref, w_ref, tm)
        for h in range(B_HEADS):
            hs = slice(h * B_DH, (h + 1) * B_DH)
            q = sl[:, h * B_DH:(h + 1) * B_DH]
            k = sl[:, W + h * B_DH:W + (h + 1) * B_DH]
            q_ref[:, hs] = q * (lax.rsqrt(jnp.sum(q * q, axis=-1, keepdims=True) + EPS) * (B_DH ** -0.5))
            k_ref[:, hs] = k * lax.rsqrt(jnp.sum(k * k, axis=-1, keepdims=True) + EPS)
        v_ref[...] = sl[:, 2 * W:]

    return pl.pallas_call(
        body, name=name, grid=(S // tm,), in_specs=[cur, prev, pl.BlockSpec((CONV_K, C), lambda i: (0, 0))],
        out_specs=[_rows(tm, W)] * 3, out_shape=[jax.ShapeDtypeStruct((S, W), F32)] * 3,
        scratch_shapes=[pltpu.VMEM((HALO + tm, C), F32)], compiler_params=_par(1),
    )(proj, proj, wconv)


def _conv_rows(xx_ref, w_ref, start, n):
    base = HALO - CONV_K + 1 + start
    y = w_ref[0:1, :] * xx_ref[pl.ds(base, n), :]
    for j in range(1, CONV_K):
        y = y + w_ref[j:j + 1, :] * xx_ref[pl.ds(base + j, n), :]
    return y


def _pre_dy(y, dq, dk, dv):
    W = B_HEADS * B_DH
    sg = _sigmoid(y)
    sl = y * sg
    dsilu = sg * (1.0 + y * (1.0 - sg))
    parts = []
    for base, d_all, c in ((0, dq, B_DH ** -0.5), (W, dk, 1.0)):
        for h in range(B_HEADS):
            lo = base + h * B_DH
            t = sl[:, lo:lo + B_DH]
            d = d_all[:, h * B_DH:(h + 1) * B_DH]
            r = lax.rsqrt(jnp.sum(t * t, axis=-1, keepdims=True) + EPS)
            parts.append((c * r) * (d - t * (r * r) * jnp.sum(d * t, axis=-1, keepdims=True)) * dsilu[:, lo:lo + B_DH])
    parts.append(dv * dsilu[:, 2 * W:])
    return jnp.concatenate(parts, axis=1)


def _gdn_pre_bwd(proj, wconv, dqn, dkn, dv, dproj, name):
    S = proj.shape[0]
    C = 3 * B_HEADS * B_DH
    W = B_HEADS * B_DH
    tm = _pick(S, 256, 16)
    nt_ = S // tm
    cur, prev = _gdn_pre_specs(tm, C, P_QKVB // C)
    after = lambda i: jnp.minimum((i + 1) * (tm // HALO), S // HALO - 1)
    next_x = pl.BlockSpec((HALO, C), lambda i: (after(i), P_QKVB // C))
    next_d = pl.BlockSpec((HALO, W), lambda i: (after(i), 0))

    def body(x_ref, prev_ref, nx_ref, w_ref, dq_ref, dk_ref, dv_ref, ndq_ref, ndk_ref, ndv_ref, _, dx_ref, dw_ref,
             xx_ref, dd_ref):
        i = pl.program_id(0)
        _fill_prev_halo(xx_ref, x_ref, prev_ref, i, tm)
        xx_ref[pl.ds(HALO + tm, HALO), :] = nx_ref[...]
        dyv = _pre_dy(_conv_rows(xx_ref, w_ref, 0, tm), dq_ref[...], dk_ref[...], dv_ref[...])
        dd_ref[pl.ds(0, tm), :] = dyv

        @pl.when(i == nt_ - 1)
        def _():
            dd_ref[pl.ds(tm, HALO), :] = jnp.zeros((HALO, C), F32)

        @pl.when(i != nt_ - 1)
        def _():
            dd_ref[pl.ds(tm, HALO), :] = _pre_dy(_conv_rows(xx_ref, w_ref, tm, HALO), ndq_ref[...], ndk_ref[...],
                                                 ndv_ref[...])

        dx = w_ref[0:1, :] * dd_ref[pl.ds(CONV_K - 1, tm), :]
        for j in range(1, CONV_K):
            dx = dx + w_ref[j:j + 1, :] * dd_ref[pl.ds(CONV_K - 1 - j, tm), :]
        dx_ref[...] = dx.astype(BF16)
        dw = jnp.concatenate(
            [jnp.sum(dyv * xx_ref[pl.ds(HALO - CONV_K + 1 + j, tm), :], axis=0, keepdims=True) for j in range(CONV_K)],
            axis=0)
        _acc(dw_ref, dw, i)

    return pl.pallas_call(
        body, name=name, grid=(nt_,),
        in_specs=[cur, prev, next_x, pl.BlockSpec((CONV_K, C), lambda i: (0, 0))] + [_rows(tm, W)] * 3 + [next_d] * 3
        + [HBM],
        out_specs=[_rows(tm, C, P_QKVB // C), pl.BlockSpec((CONV_K, C), lambda i: (0, 0))],
        out_shape=[jax.ShapeDtypeStruct(dproj.shape, BF16), jax.ShapeDtypeStruct((CONV_K, C), F32)],
        input_output_aliases={10: 0},
        scratch_shapes=[pltpu.VMEM((HALO + tm + HALO, C), F32), pltpu.VMEM((tm + HALO, C), F32)],
        compiler_params=_arb(1),
    )(proj, proj, proj, wconv, dqn, dkn, dv, dqn, dkn, dv, dproj)


def _chunk_masks():
    row = lax.broadcasted_iota(jnp.int32, (CHUNK, CHUNK), 0)
    col = lax.broadcasted_iota(jnp.int32, (CHUNK, CHUNK), 1)
    return row >= col, row > col


def _chunk_local(q, k, vv, bc, gc, gr, tri):
    dm = jnp.where(tri, jnp.exp(jnp.where(tri, gc - gr, 0.0)), 0.0)
    kk = _bnt(k, k)
    glast = gr[..., CHUNK - 1:CHUNK]
    ep = jnp.exp(gc)
    em = jnp.exp(glast - gc)
    el = jnp.exp(glast)
    return dm, kk, ep, em, el, vv * bc, k * (bc * ep)


def _unit_lower_inverse(low):
    row = lax.broadcasted_iota(jnp.int32, (CHUNK, CHUNK), 0)
    col = lax.broadcasted_iota(jnp.int32, (CHUNK, CHUNK), 1)
    p = -low
    t = jnp.where(row == col, 1.0, 0.0).astype(F32) + p
    steps = CHUNK.bit_length() - 2
    for _ in range(steps):
        p = _nn(p, p, SOLVE_PREC)
        t = t + _nn(t, p, SOLVE_PREC)
    return t


GROUP = 8


LATE = 7


def _carry(phases, first, middle, last):
    if len(phases) == 3:
        pl.when(first)(phases[0])
        pl.when(middle)(phases[1])
        return lambda: pl.when(last)(phases[2])
    pl.when(first)(phases[0])
    return lambda: pl.when(last)(phases[1])


def _pairs(nchunks):
    return [(c, h) for c in range(nchunks) for h in range(B_HEADS)]


def _tok(c):
    return slice(c * CHUNK, (c + 1) * CHUNK)


def _head(h):
    return slice(h * B_DH, (h + 1) * B_DH)


def _stack_tokens(ref, nchunks):
    return jnp.stack([ref[_tok(c), _head(h)] for c, h in _pairs(nchunks)])


def _stack_cols(ref, nchunks):
    per_chunk = [ref[c] for c in range(nchunks)] if len(ref.shape) == 3 else [ref[...]]
    return jnp.stack([per_chunk[c][:, h:h + 1] for c, h in _pairs(nchunks)])


def _stack_rows(ref, nchunks):
    if len(ref.shape) == 3:
        return jnp.stack([ref[c, h:h + 1, :] for c, h in _pairs(nchunks)])
    return jnp.stack([ref[h:h + 1, :] for _, h in _pairs(1)])


def _gdn_group_specs(ng_steps, W):
    tok = pl.BlockSpec((GROUP * CHUNK, W), lambda i: (i, 0))
    colv = pl.BlockSpec((GROUP, CHUNK, B_HEADS), lambda i: (i, 0, 0))
    rowv = pl.BlockSpec((GROUP, B_HEADS, CHUNK), lambda i: (i, 0, 0))
    mat = pl.BlockSpec((GROUP, B_HEADS, CHUNK, CHUNK), lambda i: (i, 0, 0, 0))
    return tok, colv, rowv, mat


def _gdn_local_fwd(qn, kn, v, bcol, gcol, grow, name, gather=None):
    S, Wd = qn.shape
    nc = S // CHUNK
    steps = nc // GROUP
    tok, colv, rowv, mat = _gdn_group_specs(steps, Wd)
    shards, layer = gather if gather is not None else ((), None)
    ng = len(shards)

    def body(q_ref, k_ref, v_ref, bc_ref, gc_ref, gr_ref, *rest):
        srcs, (t_ref, a_ref, u_ref, w_ref), gouts, sems = rest[:ng], rest[ng:ng + 4], rest[ng + 4:2 * ng + 4], rest[2 * ng + 4:]
        i = pl.program_id(0)
        done = _carry(_gather_phases(layer, srcs, gouts, *sems), i == 0, i == LATE * steps // 8, i == steps - 1) if ng else None
        tri, strict = _chunk_masks()
        q, k, vv = (_stack_tokens(r, GROUP) for r in (q_ref, k_ref, v_ref))
        bc, gc, gr = _stack_cols(bc_ref, GROUP), _stack_cols(gc_ref, GROUP), _stack_rows(gr_ref, GROUP)
        dm, kk, ep, em, el, vb, kb = _chunk_local(q, k, vv, bc, gc, gr, tri)
        t = _unit_lower_inverse(jnp.where(strict, bc * kk * dm, 0.0))
        a = _bnt(q, k) * dm
        u = _nn(t, vb, SOLVE_PREC)
        w = _nn(t, kb, SOLVE_PREC)
        for n, (c, h) in enumerate(_pairs(GROUP)):
            t_ref[c, h] = t[n]
            a_ref[c, h] = a[n]
            u_ref[_tok(c), _head(h)] = u[n]
            w_ref[_tok(c), _head(h)] = w[n]
        if ng:
            done()

    res = pl.pallas_call(
        body, name=name, grid=(steps,), in_specs=[tok, tok, tok, colv, colv, rowv] + [HBM] * ng,
        out_specs=[mat, mat, tok, tok] + [HBM] * ng,
        out_shape=[jax.ShapeDtypeStruct((nc, B_HEADS, CHUNK, CHUNK), F32)] * 2 + [jax.ShapeDtypeStruct((S, Wd), F32)] * 2
        + _gather_out_shapes(shards),
        scratch_shapes=_gather_scratch(ng) if ng else [], compiler_params=_arb(1),
    )(qn, kn, v, bcol, gcol, grow, *shards)
    return res[0], res[1], res[2], res[3], list(res[4:])


def _scan_decays(gc, gr):
    glast = gr[..., CHUNK - 1:CHUNK]
    return jnp.exp(gc), jnp.exp(glast - gc), jnp.exp(glast)


SCAN = 8


def _gdn_scan_specs(steps, rev):
    idx = (lambda i: steps - 1 - i) if rev else (lambda i: i)
    W = B_HEADS * B_DH
    tok = pl.BlockSpec((SCAN * CHUNK, W), lambda i: (idx(i), 0))
    colv = pl.BlockSpec((SCAN, CHUNK, B_HEADS), lambda i: (idx(i), 0, 0))
    rowv = pl.BlockSpec((SCAN, B_HEADS, CHUNK), lambda i: (idx(i), 0, 0))
    mat = pl.BlockSpec((SCAN, B_HEADS, CHUNK, CHUNK), lambda i: (idx(i), 0, 0, 0))
    smat = pl.BlockSpec((SCAN, B_HEADS, B_DH, B_DH), lambda i: (idx(i), 0, 0, 0))
    return tok, colv, rowv, mat, smat


def _chunk_rows(ref, c):
    return ref.at[pl.ds(c * CHUNK, CHUNK)]


def _gdn_scan_fwd(qn, kn, u, w, a, gcol, grow, name, gather=None):
    S, Wd = qn.shape
    nc = S // CHUNK
    steps = nc // SCAN
    tok, colv, rowv, mat, smat = _gdn_scan_specs(steps, False)
    shards, layer = gather if gather is not None else ((), None)
    ng = len(shards)

    def body(q_ref, k_ref, u_ref, w_ref, a_ref, gc_ref, gr_ref, *rest):
        srcs, (o_ref, sh_ref), gouts = rest[:ng], rest[ng:ng + 2], rest[ng + 2:2 * ng + 2]
        st_ref, sems = rest[2 * ng + 2], rest[2 * ng + 3:]
        i = pl.program_id(0)
        done = (_carry(_gather_phases(layer, srcs, gouts, *sems), i == 0, i == LATE * steps // 8, i == steps - 1)
                if ng else None)

        @pl.when(i == 0)
        def _():
            st_ref[...] = jnp.zeros_like(st_ref)

        for c in range(SCAN):
            ep, em, el = _scan_decays(_stack_cols(gc_ref.at[c], 1), _stack_rows(gr_ref.at[c], 1))
            q, k, u, w = (_stack_tokens(_chunk_rows(r, c), 1) for r in (q_ref, k_ref, u_ref, w_ref))
            s0 = st_ref[...]
            ut = u - _bnn(w, s0)
            o = _bnn(q * ep, s0) + _bnn(a_ref[c], ut)
            st_ref[...] = el * s0 + _btn(k * em, ut)
            sh_ref[c] = s0
            for h in range(B_HEADS):
                o_ref[_tok(c), _head(h)] = o[h]
        if ng:
            done()

    res = pl.pallas_call(
        body, name=name, grid=(steps,), in_specs=[tok, tok, tok, tok, mat, colv, rowv] + [HBM] * ng,
        out_specs=[tok, smat] + [HBM] * ng,
        out_shape=[jax.ShapeDtypeStruct((S, Wd), F32), jax.ShapeDtypeStruct((nc, B_HEADS, B_DH, B_DH), F32)]
        + _gather_out_shapes(shards),
        scratch_shapes=[pltpu.VMEM((B_HEADS, B_DH, B_DH), F32)] + (_gather_scratch(ng) if ng else []),
        compiler_params=_arb(1),
    )(qn, kn, u, w, a, gcol, grow, *shards)
    return res[0], res[1], list(res[2:])


def _gdn_scan_bwd(qn, kn, u, w, a, gcol, grow, ssave, do, name):
    S, Wd = qn.shape
    nc = S // CHUNK
    steps = nc // SCAN
    tok, colv, rowv, mat, smat = _gdn_scan_specs(steps, True)

    def body(q_ref, k_ref, u_ref, w_ref, a_ref, gc_ref, gr_ref, sh_ref, do_ref,
             du_ref, dw_ref, dqd_ref, dkd_ref, da_ref, dgl_ref, ds_ref):
        i = pl.program_id(0)

        @pl.when(i == 0)
        def _():
            ds_ref[...] = jnp.zeros_like(ds_ref)

        tri, _ = _chunk_masks()
        sub4 = lax.broadcasted_iota(jnp.int32, (B_HEADS, CHUNK), 0)
        lane_last = lax.broadcasted_iota(jnp.int32, (1, CHUNK), 1) == CHUNK - 1
        for c in reversed(range(SCAN)):
            ep, em, el = _scan_decays(_stack_cols(gc_ref.at[c], 1), _stack_rows(gr_ref.at[c], 1))
            q, k, u, w, dout = (_stack_tokens(_chunk_rows(r, c), 1) for r in (q_ref, k_ref, u_ref, w_ref, do_ref))
            s0 = sh_ref[c]
            ds = ds_ref[...]
            ut = u - _bnn(w, s0)
            dut = _btn(a_ref[c], dout) + _bnn(k * em, ds)
            ds_ref[...] = el * ds + _btn(q * ep, dout) - _btn(w, dut)
            dw = -_bnt(dut, s0)
            dqd = _bnt(dout, s0)
            dkd = _bnt(ut, ds)
            da_ref[c] = jnp.where(tri, _bnt(dout, ut), 0.0)
            d_el = jnp.sum(jnp.sum(s0 * ds, axis=-1, keepdims=True), axis=-2, keepdims=True)
            last = d_el * el
            dgl_acc = jnp.zeros((B_HEADS, CHUNK), F32)
            for h in range(B_HEADS):
                du_ref[_tok(c), _head(h)] = dut[h]
                dw_ref[_tok(c), _head(h)] = dw[h]
                dqd_ref[_tok(c), _head(h)] = dqd[h]
                dkd_ref[_tok(c), _head(h)] = dkd[h]
                dgl_acc = jnp.where(sub4 == h, jnp.where(lane_last, last[h], 0.0), dgl_acc)
            dgl_ref[c] = dgl_acc

    return pl.pallas_call(
        body, name=name, grid=(steps,), in_specs=[tok, tok, tok, tok, mat, colv, rowv, smat, tok],
        out_specs=[tok, tok, tok, tok, mat, rowv],
        out_shape=[jax.ShapeDtypeStruct((S, Wd), F32)] * 4 + [jax.ShapeDtypeStruct((nc, B_HEADS, CHUNK, CHUNK), F32),
                                                             jax.ShapeDtypeStruct((nc, B_HEADS, CHUNK), F32)],
        scratch_shapes=[pltpu.VMEM((B_HEADS, B_DH, B_DH), F32)], compiler_params=_arb(1),
    )(qn, kn, u, w, a, gcol, grow, ssave, do)


def _gdn_local_bwd(qn, kn, v, bcol, gcol, grow, tsave, du, dw, dqd, dkd, da, dgl, name, exchange=()):
    S, Wd = qn.shape
    nc = S // CHUNK
    steps = nc // GROUP
    tok, colv, rowv, mat = _gdn_group_specs(steps, Wd)
    ne = len(exchange)

    def body(q_ref, k_ref, v_ref, bc_ref, gc_ref, gr_ref, t_ref, du_ref, dw_ref, dqd_ref, dkd_ref, da_ref, dgl_ref, *rest):
        srcs, (dq_ref, dk_ref, dv_ref, dbc_ref, dgc_ref, dgr_ref) = rest[:ne], rest[ne:ne + 6]
        eouts, sems = rest[ne + 6:2 * ne + 6], rest[2 * ne + 6:]
        i = pl.program_id(0)
        done = _carry(_chips_phases(srcs, eouts, *sems), i == 0, None, i == steps - 1) if ne else None
        tri, strict = _chunk_masks()
        lane4 = lax.broadcasted_iota(jnp.int32, (CHUNK, B_HEADS), 1)
        sub4 = lax.broadcasted_iota(jnp.int32, (B_HEADS, CHUNK), 0)
        lane_last = lax.broadcasted_iota(jnp.int32, (1, CHUNK), 1) == CHUNK - 1
        q, k, vv, dut, dwv, dqd, dkd = (_stack_tokens(r, GROUP)
                                        for r in (q_ref, k_ref, v_ref, du_ref, dw_ref, dqd_ref, dkd_ref))
        bc, gc, gr = _stack_cols(bc_ref, GROUP), _stack_cols(gc_ref, GROUP), _stack_rows(gr_ref, GROUP)
        dm, kk, ep, em, el, vb, kb = _chunk_local(q, k, vv, bc, gc, gr, tri)
        t = jnp.stack([t_ref[c, h] for c, h in _pairs(GROUP)])
        dav = jnp.stack([da_ref[c, h] for c, h in _pairs(GROUP)])
        qk = _bnt(q, k)
        dt = _nt(dut, vb, SOLVE_PREC) + _nt(dwv, kb, SOLVE_PREC)
        dvb = _tn(t, dut, SOLVE_PREC)
        dkb = _tn(t, dwv, SOLVE_PREC)
        dl = jnp.where(strict, -_tn(t, _nt(dt, t, SOLVE_PREC), SOLVE_PREC), 0.0)
        g1 = dl * dm
        dkb_k = jnp.sum(dkb * k, axis=-1, keepdims=True)
        dbeta = jnp.sum(g1 * kk, axis=-1, keepdims=True) + jnp.sum(dvb * vv, axis=-1, keepdims=True) + dkb_k * ep
        dkk = g1 * bc
        ddm = dl * (bc * kk) + dav * qk
        dqk = dav * dm
        dq = _bnn(dqk, k) + dqd * ep
        dk = _btn(dqk, q) + _bnn(dkk, k) + _btn(dkk, k) + dkb * (bc * ep) + dkd * em
        dv = dvb * bc
        dep = dkb_k * bc + jnp.sum(dqd * q, axis=-1, keepdims=True)
        dem = jnp.sum(dkd * k, axis=-1, keepdims=True)
        mm = ddm * dm
        dgam_c = jnp.sum(mm, axis=-1, keepdims=True) + dep * ep - dem * em
        dglast = jnp.sum(dem * em, axis=-2, keepdims=True)
        dgam_r = -jnp.sum(mm, axis=-2, keepdims=True) + jnp.where(lane_last, dglast, 0.0)
        for c in range(GROUP):
            dbc_acc = jnp.zeros((CHUNK, B_HEADS), F32)
            dgc_acc = jnp.zeros((CHUNK, B_HEADS), F32)
            dgr_acc = jnp.zeros((B_HEADS, CHUNK), F32)
            for h in range(B_HEADS):
                n = c * B_HEADS + h
                dq_ref[_tok(c), _head(h)] = dq[n]
                dk_ref[_tok(c), _head(h)] = dk[n]
                dv_ref[_tok(c), _head(h)] = dv[n]
                dbc_acc = jnp.where(lane4 == h, dbeta[n], dbc_acc)
                dgc_acc = jnp.where(lane4 == h, dgam_c[n], dgc_acc)
                dgr_acc = jnp.where(sub4 == h, dgam_r[n], dgr_acc)
            dbc_ref[c] = dbc_acc
            dgc_ref[c] = dgc_acc
            dgr_ref[c] = dgr_acc + dgl_ref[c]
        if ne:
            done()

    res = pl.pallas_call(
        body, name=name, grid=(steps,),
        in_specs=[tok, tok, tok, colv, colv, rowv, mat, tok, tok, tok, tok, mat, rowv] + [HBM] * ne,
        out_specs=[tok, tok, tok, colv, colv, rowv] + [HBM] * ne,
        out_shape=[jax.ShapeDtypeStruct((S, Wd), F32)] * 3
        + [jax.ShapeDtypeStruct((nc, CHUNK, B_HEADS), F32)] * 2 + [jax.ShapeDtypeStruct((nc, B_HEADS, CHUNK), F32)]
        + _chips_out_shapes(exchange),
        scratch_shapes=_chips_scratch(ne) if ne else [], compiler_params=_arb(1),
    )(qn, kn, v, bcol, gcol, grow, tsave, du, dw, dqd, dkd, da, dgl, *exchange)
    return tuple(res[:6]) + (list(res[6:]),)


def _gdn_post_fwd(o, proj, ng, name):
    S, W = o.shape
    tm = _pick(S, 512, 8)

    def body(o_ref, z_ref, g_ref, y_ref):
        gv = g_ref[...]
        for h in range(B_HEADS):
            hs = slice(h * B_DH, (h + 1) * B_DH)
            oh = o_ref[:, hs]
            z = z_ref[:, hs]
            r = lax.rsqrt(jnp.mean(oh * oh, axis=-1, keepdims=True) + EPS)
            y_ref[:, hs] = (oh * r * gv * (z * _sigmoid(z))).astype(BF16)

    return pl.pallas_call(
        body, name=name, grid=(S // tm,), in_specs=[_rows(tm, W), _rows(tm, W, P_Z // W), _vec(B_DH)],
        out_specs=_rows(tm, W), out_shape=jax.ShapeDtypeStruct((S, W), BF16), compiler_params=_par(1),
    )(o, proj, ng)


def _gdn_post_bwd(dy, o, proj, ng, dproj, name):
    S, W = o.shape
    tm = _pick(S, 512, 16)

    def body(dy_ref, o_ref, z_ref, g_ref, _, do_ref, dz_ref, dg_ref):
        i = pl.program_id(0)
        gv = g_ref[...]
        dg = jnp.zeros((1, B_DH), F32)
        for h in range(B_HEADS):
            hs = slice(h * B_DH, (h + 1) * B_DH)
            oh = o_ref[:, hs]
            z = z_ref[:, hs]
            d = dy_ref[:, hs]
            r = lax.rsqrt(jnp.mean(oh * oh, axis=-1, keepdims=True) + EPS)
            n = oh * r
            sg = _sigmoid(z)
            sz = z * sg
            dn = d * gv * sz
            dg = dg + jnp.sum(d * n * sz, axis=0, keepdims=True)
            dz_ref[:, hs] = (d * n * gv * (sg * (1.0 + z * (1.0 - sg)))).astype(BF16)
            do_ref[:, hs] = r * (dn - n * jnp.mean(dn * n, axis=-1, keepdims=True))
        _acc(dg_ref, dg, i)

    return pl.pallas_call(
        body, name=name, grid=(S // tm,),
        in_specs=[_rows(tm, W), _rows(tm, W), _rows(tm, W, P_Z // W), _vec(B_DH), HBM],
        out_specs=[_rows(tm, W), _rows(tm, W, P_Z // W), _vec(B_DH)],
        out_shape=[jax.ShapeDtypeStruct((S, W), F32), jax.ShapeDtypeStruct(dproj.shape, BF16),
                   jax.ShapeDtypeStruct((1, B_DH), F32)],
        input_output_aliases={4: 1}, compiler_params=_arb(1),
    )(dy, o, proj, ng, dproj)


def _ada_mod(c_all, w_ada, b_shard, name):
    L, D, Ns = w_ada.shape
    B = c_all.shape[0]

    def body(c_ref, w_ref, b_ref, o_ref):
        cv = c_ref[...]
        cond = (cv * _sigmoid(cv)).astype(BF16)
        o_ref[...] = _nn(cond, w_ref[...].astype(BF16)) + b_ref[...]

    return pl.pallas_call(
        body, name=name, grid=(L,),
        in_specs=[pl.BlockSpec((B, D), lambda l: (0, 0)), pl.BlockSpec((None, D, Ns), lambda l: (l, 0, 0)),
                  pl.BlockSpec((None, 1, Ns), lambda l: (l, 0, 0))],
        out_specs=pl.BlockSpec((None, B, Ns), lambda l: (l, 0, 0)),
        out_shape=jax.ShapeDtypeStruct((L, B, Ns), F32), compiler_params=_par(1),
    )(c_all, w_ada, b_shard)


def _ada_wgrad(c_all, dmod, name):
    L, B, Ns = dmod.shape
    D = c_all.shape[1]

    def body(c_ref, d_ref, o_ref):
        cv = c_ref[...]
        cond = (cv * _sigmoid(cv)).astype(BF16)
        o_ref[...] = _tn(cond, d_ref[...].astype(BF16))

    return pl.pallas_call(
        body, name=name, grid=(L,),
        in_specs=[pl.BlockSpec((B, D), lambda l: (0, 0)), pl.BlockSpec((None, B, Ns), lambda l: (l, 0, 0))],
        out_specs=pl.BlockSpec((None, D, Ns), lambda l: (l, 0, 0)),
        out_shape=jax.ShapeDtypeStruct((L, D, Ns), F32), compiler_params=_par(1),
    )(c_all, dmod)


W_IN_PIECES = ((0, 0, 1410), (1, 0, 1410), (2, 0, 252), (2, 772, 638), (3, 0, 1410), (2, 252, 512), (2, 764, 8))


def _reorder_w_in(w4, name):
    L, _, D, Cs = w4.shape
    tm = _pick(D, 256, 16)
    used = sum(p[2] for p in W_IN_PIECES)

    def body(w_ref, o_ref):
        shard = [w_ref[s] for s in range(4)]
        parts = [shard[s][:, lo:lo + n] for s, lo, n in W_IN_PIECES]
        o_ref[...] = jnp.concatenate(parts + [jnp.zeros((tm, P_END - used), w4.dtype)], axis=1)

    return pl.pallas_call(
        body, name=name, grid=(L, D // tm), in_specs=[pl.BlockSpec((None, 4, tm, Cs), lambda l, i: (l, 0, i, 0))],
        out_specs=pl.BlockSpec((None, tm, P_END), lambda l, i: (l, i, 0)),
        out_shape=jax.ShapeDtypeStruct((L, D, P_END), w4.dtype), compiler_params=_par(2),
    )(w4)


def _restore_w_in(g, name):
    D = g.shape[0]
    tm = _pick(D, 256, 16)

    def body(g_ref, o_ref, ob_ref):
        gv = g_ref[...]
        off = 0
        pieces = {}
        for s, lo, n in W_IN_PIECES:
            pieces.setdefault(s, []).append((lo, gv[:, off:off + n]))
            off += n
        for s, lst in pieces.items():
            lst.sort(key=lambda t: t[0])
            shard = lst[0][1] if len(lst) == 1 else jnp.concatenate([t[1] for t in lst], axis=1)
            o_ref[s] = shard
            ob_ref[s] = shard.astype(BF16)

    spec = pl.BlockSpec((4, tm, W_IN_SHARD), lambda i: (0, i, 0))
    return pl.pallas_call(
        body, name=name, grid=(D // tm,), in_specs=[pl.BlockSpec((tm, P_END), lambda i: (i, 0))],
        out_specs=[spec, spec],
        out_shape=[jax.ShapeDtypeStruct((4, D, W_IN_SHARD), g.dtype), jax.ShapeDtypeStruct((4, D, W_IN_SHARD), BF16)],
        compiler_params=_par(1),
    )(g)


def _adam_update(w, g, m, v):
    mn = ADAM_B1 * m + (1.0 - ADAM_B1) * g
    vn = ADAM_B2 * v + (1.0 - ADAM_B2) * (g * g)
    m_hat = mn / (1.0 - ADAM_B1 ** ADAM_STEP)
    v_hat = vn / (1.0 - ADAM_B2 ** ADAM_STEP)
    return -ADAM_LR * (m_hat / (jnp.sqrt(v_hat) + ADAM_EPS) + ADAM_WD * w), mn, vn


def _adamw(w, g, m, v, name, exchange=()):
    shape = w.shape
    C = shape[-1]
    R = w.size // C
    tm = _pick(R, 512, 8)
    steps = R // tm
    spec = pl.BlockSpec((tm, C), lambda i: (i, 0))
    ne = len(exchange)

    def body(w_ref, g_ref, m_ref, v_ref, *rest):
        srcs, (d_ref, mo_ref, vo_ref), eouts, sems = rest[:ne], rest[ne:ne + 3], rest[ne + 3:2 * ne + 3], rest[2 * ne + 3:]
        i = pl.program_id(0)
        done = _carry(_chips_phases(srcs, eouts, *sems), i == 0, None, i == steps - 1) if ne else None
        d_ref[...], mo_ref[...], vo_ref[...] = _adam_update(w_ref[...], g_ref[...], m_ref[...], v_ref[...])
        if ne:
            done()

    outs = pl.pallas_call(
        body, name=name, grid=(steps,), in_specs=[spec] * 4 + [HBM] * ne, out_specs=[spec] * 3 + [HBM] * ne,
        out_shape=[jax.ShapeDtypeStruct((R, C), F32)] * 3 + _chips_out_shapes(exchange),
        scratch_shapes=_chips_scratch(ne) if ne else [], compiler_params=_arb(1) if ne else _par(1),
    )(*(t.reshape(R, C) for t in (w, g, m, v)), *exchange)
    res = tuple(o.reshape(shape) for o in outs[:3])
    return res + (list(outs[3:]),) if ne else res


def _adamw_lead(w, g, m, v, name, tl):
    A, B, C = w.shape
    spec = pl.BlockSpec((tl, B, C), lambda i: (i, 0, 0))

    def body(w_ref, g_ref, m_ref, v_ref, d_ref, mo_ref, vo_ref):
        d_ref[...], mo_ref[...], vo_ref[...] = _adam_update(w_ref[...], g_ref[...], m_ref[...], v_ref[...])

    return pl.pallas_call(
        body, name=name, grid=(A // tl,), in_specs=[spec] * 4, out_specs=[spec] * 3,
        out_shape=[jax.ShapeDtypeStruct((A, B, C), F32)] * 3, compiler_params=_par(1),
    )(w, g, m, v)


def _adamw_layers(w, gs, m, v, name):
    L, R, C = w.shape
    tm = _pick(R, 256, 8)
    spec = pl.BlockSpec((None, tm, C), lambda l, i: (l, i, 0))
    g_specs = [pl.BlockSpec((tm, C), functools.partial(lambda ll, l, i: (jnp.where(l == ll, i, 0), 0), ll))
               for ll in range(L)]

    def body(w_ref, m_ref, v_ref, *rest):
        g_refs, (go_ref, d_ref, mo_ref, vo_ref) = rest[:L], rest[L:]
        l = pl.program_id(0)
        for ll in range(L):
            @pl.when(l == ll)
            def _():
                g = g_refs[ll][...]
                go_ref[...] = g
                d_ref[...], mo_ref[...], vo_ref[...] = _adam_update(w_ref[...], g, m_ref[...], v_ref[...])

    return pl.pallas_call(
        body, name=name, grid=(L, R // tm), in_specs=[spec] * 3 + g_specs, out_specs=[spec] * 4,
        out_shape=[jax.ShapeDtypeStruct((L, R, C), F32)] * 4, compiler_params=_arb(2),
    )(w, m, v, *gs)


def _pair_sums(a, where, b, name):
    NB, _, R, C = a.shape

    def body(where_ref, a_ref, b_ref, p_ref, own_ref):
        s = a_ref[...] + b_ref[...].astype(F32)
        p_ref[...] = s.astype(BF16)

        @pl.when(pl.program_id(0) == where_ref[1])
        def _():
            own_ref[...] = s

    return pl.pallas_call(
        body, name=name,
        grid_spec=pltpu.PrefetchScalarGridSpec(
            num_scalar_prefetch=1, grid=(NB,),
            in_specs=[pl.BlockSpec((None, None, R, C), lambda k, w: (k, w[0], 0, 0)),
                      pl.BlockSpec((None, R, C), lambda k, w: (k, 0, 0))],
            out_specs=[pl.BlockSpec((None, R, C), lambda k, w: (k, 0, 0)), pl.BlockSpec((R, C), lambda k, w: (0, 0))]),
        out_shape=[jax.ShapeDtypeStruct((NB, R, C), BF16), jax.ShapeDtypeStruct((R, C), F32)],
        compiler_params=_arb(1),
    )(where, a, b)


def _sum_own_and_received(own, recv, where, name):
    R, C = own.shape
    tm = _pick(R, 256, 16)

    def body(where_ref, p_ref, r_ref, o_ref):
        o_ref[...] = ((p_ref[...] + r_ref[0].astype(F32)) + r_ref[1].astype(F32)) + r_ref[2].astype(F32)

    return pl.pallas_call(
        body, name=name,
        grid_spec=pltpu.PrefetchScalarGridSpec(
            num_scalar_prefetch=1, grid=(R // tm,),
            in_specs=[pl.BlockSpec((tm, C), lambda i, w: (i, 0)), pl.BlockSpec((3, tm, C), lambda i, w: (0, i, 0))],
            out_specs=pl.BlockSpec((None, tm, C), lambda i, w: (w[0], i, 0))),
        out_shape=jax.ShapeDtypeStruct((2, R, C), F32), compiler_params=_par(1),
    )(where, own, recv)


def _position():
    return lax.axis_index("x"), lax.axis_index("y"), lax.axis_index("c")


def _other_chips(x, y):
    return [(1 - x, y), (x, 1 - y), (1 - x, 1 - y)]


HBM = pl.BlockSpec(memory_space=pl.ANY)


def _allgather8(blk, name, reduce_rows=None):
    M, N = blk.shape

    def body(x_ref, out_ref, *rest):
        if reduce_rows is None:
            send_sems, recv_sems, local_sem = rest
        else:
            sum_ref, send_sems, recv_sems, local_sem = rest
        x, y, c = _position()
        me, sibling = (x, y, c), (x, y, 1 - c)
        chips = _other_chips(x, y)

        def rows(px, py, pc):
            return out_ref.at[pl.ds((4 * px + 2 * py + pc) * M, M), :]

        def copy(k, block, to, src=None):
            return pltpu.make_async_remote_copy(
                src_ref=rows(*block) if src is None else src, dst_ref=rows(*block),
                send_sem=send_sems.at[k], recv_sem=recv_sems.at[k], device_id=to, device_id_type=MESH)

        mine = pltpu.make_async_copy(x_ref, rows(*me), local_sem)
        mine.start()
        first = [copy(0, me, sibling, src=x_ref)]
        first += [copy(1 + j, me, (*chip, c), src=x_ref) for j, chip in enumerate(chips)]
        for cp in first:
            cp.start()
        passed = [copy(4 + j, (*chip, c), sibling) for j, chip in enumerate(chips)]
        for j, chip in enumerate(chips):
            copy(1 + j, (*chip, c), me).wait_recv()
            passed[j].start()
        copy(0, sibling, me).wait_recv()
        for j, chip in enumerate(chips):
            copy(4 + j, (*chip, 1 - c), me).wait_recv()
        for cp in first + passed:
            cp.wait_send()
        mine.wait()
        if reduce_rows is not None:
            tot = out_ref[pl.ds(0, reduce_rows), :]
            for d in range(1, 8):
                tot = tot + out_ref[pl.ds(d * M, reduce_rows), :]
            sum_ref[...] = tot

    vmem = pl.BlockSpec(memory_space=pltpu.VMEM)
    out_shape = [jax.ShapeDtypeStruct((8 * M, N), blk.dtype)]
    if reduce_rows is not None:
        out_shape.append(jax.ShapeDtypeStruct((reduce_rows, N), blk.dtype))
    res = pl.pallas_call(
        body, name=name, out_shape=out_shape, in_specs=[vmem], out_specs=[vmem] * len(out_shape),
        scratch_shapes=[pltpu.SemaphoreType.DMA((7,)), pltpu.SemaphoreType.DMA((7,)), pltpu.SemaphoreType.DMA],
    )(blk)
    return res[0] if reduce_rows is None else (res[0], res[1])


def _gather_phases(layer, srcs, outs, send_sems, recv_sems, local_sems):
    n = len(srcs)
    x, y, c = _position()
    me, sibling = (x, y, c), (x, y, 1 - c)
    chips = _other_chips(x, y)

    def region(t, px, py, pc):
        return outs[t].at[2 * px + py, pc]

    def copy(t, k, block, to, own=False):
        return pltpu.make_async_remote_copy(
            src_ref=srcs[t].at[layer, c] if own else region(t, *block), dst_ref=region(t, *block),
            send_sem=send_sems.at[7 * t + k], recv_sem=recv_sems.at[7 * t + k], device_id=to, device_id_type=MESH)

    def local(t):
        return pltpu.make_async_copy(srcs[t].at[layer, c], region(t, *me), local_sems.at[t])

    def first(t):
        return [copy(t, 0, me, sibling, own=True)] + [copy(t, 1 + j, me, (*chip, c), own=True)
                                                       for j, chip in enumerate(chips)]

    def start():
        for t in range(n):
            local(t).start()
        for t in range(n):
            for cp in first(t):
                cp.start()

    def forward():
        for j, chip in enumerate(chips):
            for t in range(n):
                copy(t, 1 + j, (*chip, c), me).wait_recv()
                copy(t, 4 + j, (*chip, c), sibling).start()

    def finish():
        for t in range(n):
            copy(t, 0, sibling, me).wait_recv()
        for j, chip in enumerate(chips):
            for t in range(n):
                copy(t, 4 + j, (*chip, 1 - c), me).wait_recv()
        for t in range(n):
            for cp in first(t) + [copy(t, 4 + j, (*chip, c), sibling) for j, chip in enumerate(chips)]:
                cp.wait_send()
            local(t).wait()

    return start, forward, finish


def _gather_scratch(n):
    return [pltpu.SemaphoreType.DMA((7 * n,)), pltpu.SemaphoreType.DMA((7 * n,)), pltpu.SemaphoreType.DMA((n,))]


def _gather_out_shapes(shards):
    return [jax.ShapeDtypeStruct((4,) + s.shape[1:], s.dtype) for s in shards]


def _gather_weights(shards, layer, name):
    n = len(shards)

    def body(*refs):
        start, forward, finish = _gather_phases(layer, refs[:n], refs[n:2 * n], *refs[2 * n:])
        start()
        forward()
        finish()

    return pl.pallas_call(
        body, name=name, out_shape=_gather_out_shapes(shards), in_specs=[HBM] * n, out_specs=[HBM] * n,
        scratch_shapes=_gather_scratch(n),
    )(*shards)


def _sibling_phases(srcs, outs, send_sems, recv_sems):
    x, y, c = _position()
    copies = [pltpu.make_async_remote_copy(
        src_ref=srcs[t].at[k, 1 - c], dst_ref=outs[t].at[k], send_sem=send_sems.at[4 * t + k],
        recv_sem=recv_sems.at[4 * t + k], device_id=(x, y, 1 - c), device_id_type=MESH)
        for t in range(len(srcs)) for k in range(4)]

    def start():
        for cp in copies:
            cp.start()

    def finish():
        for cp in copies:
            cp.wait()

    return start, finish


def _sibling_scratch(n):
    return [pltpu.SemaphoreType.DMA((4 * n,)), pltpu.SemaphoreType.DMA((4 * n,))]


def _sibling_out_shapes(gs):
    return [jax.ShapeDtypeStruct((4,) + g.shape[2:], g.dtype) for g in gs]


def _rs_chips(ps, name):
    n = len(ps)

    def body(*refs):
        start, finish = _chips_phases(refs[:n], refs[n:2 * n], *refs[2 * n:])
        start()
        finish()

    return pl.pallas_call(
        body, name=name, out_shape=_chips_out_shapes(ps), in_specs=[HBM] * n, out_specs=[HBM] * n,
        scratch_shapes=_chips_scratch(n),
    )(*ps)


def _chips_phases(srcs, outs, send_sems, recv_sems):
    x, y, c = _position()
    copies = [pltpu.make_async_remote_copy(
        src_ref=srcs[t].at[2 * px + py], dst_ref=outs[t].at[j], send_sem=send_sems.at[3 * t + j],
        recv_sem=recv_sems.at[3 * t + j], device_id=(px, py, c), device_id_type=MESH)
        for t in range(len(srcs)) for j, (px, py) in enumerate(_other_chips(x, y))]

    def start():
        for cp in copies:
            cp.start()

    def finish():
        for cp in copies:
            cp.wait()

    return start, finish


def _chips_scratch(n):
    return [pltpu.SemaphoreType.DMA((3 * n,)), pltpu.SemaphoreType.DMA((3 * n,))]


def _chips_out_shapes(ps):
    return [jax.ShapeDtypeStruct((3,) + p.shape[1:], p.dtype) for p in ps]


def _rs_pair(hs, name):
    n = len(hs)

    def body(*refs):
        bufs = refs[n:2 * n]
        send_sems, recv_sems = refs[2 * n:]
        x, y, c = _position()

        def copy(t, half):
            return pltpu.make_async_remote_copy(
                src_ref=bufs[t].at[half], dst_ref=bufs[t].at[half], send_sem=send_sems.at[t], recv_sem=recv_sems.at[t],
                device_id=(x, y, 1 - c), device_id_type=MESH)

        for t in range(n):
            copy(t, c).start()
        for t in range(n):
            copy(t, 1 - c).wait_recv()
        for t in range(n):
            copy(t, c).wait_send()

    out_shape = [jax.ShapeDtypeStruct(h.shape, h.dtype) for h in hs]
    return pl.pallas_call(
        body, name=name, out_shape=out_shape, in_specs=[HBM] * n, out_specs=[HBM] * n,
        input_output_aliases={t: t for t in range(n)},
        scratch_shapes=[pltpu.SemaphoreType.DMA((n,)), pltpu.SemaphoreType.DMA((n,))],
    )(*hs)


BIG = ("w_in", "w_branch_a", "w_branch_b", "w_out", "w_ffn_in", "w_ffn_out")
CARRY_ATTN = ["w_in"]
CARRY_LOCAL = ["w_ffn_out"]
CARRY_SCAN = ["w_branch_a", "w_branch_b", "w_out"]
CARRY_GU = ["w_ffn_in"]
CARRY_DATTN = ["w_in", "w_ffn_in"]
CARRY_DLOCAL = ["w_branch_a", "w_branch_b", "w_out", "w_ffn_out"]
CARRY_ADAMW = ["w_in", "w_ffn_in"]


def _band_bias(rel_table, name, gather=None):
    L, H, n = rel_table.shape
    tab = jnp.pad(rel_table, ((0, 0), (0, 0), (0, NREL_PAD - n))).reshape(L * H, 1, NREL_PAD)
    band = (A_PAST + 1) * CHUNK

    shards, glayer = gather if gather is not None else ((), None)
    ng = len(shards)

    def body(t_ref, *rest):
        srcs, o_ref, gouts, sems = rest[:ng], rest[ng], rest[ng + 1:2 * ng + 1], rest[2 * ng + 1:]
        i = pl.program_id(0)
        done = (_carry(_gather_phases(glayer, srcs, gouts, *sems), i == 0, i == LATE * (L * H) // 8, i == L * H - 1)
                if ng else None)
        r = lax.broadcasted_iota(jnp.int32, (NREL_PAD, SKEW_W), 0)
        xi = lax.broadcasted_iota(jnp.int32, (NREL_PAD, SKEW_W), 1)
        diag = jnp.where(xi < KSPAN, xi, xi - SKEW_W)
        rel = jnp.clip(A_PAST * CHUNK - diag, -A_MAX_REL, A_MAX_REL) + A_MAX_REL
        e = _nn(t_ref[...], jnp.where(rel == r, 1.0, 0.0).astype(F32), HI)
        x = jnp.broadcast_to(e, (QBLK, SKEW_W))
        row = lax.broadcasted_iota(jnp.int32, (QBLK, SKEW_W), 0)
        for b in range(QBLK.bit_length() - 1):
            x = jnp.where(((row >> b) & 1) == 1, pltpu.roll(x, 1 << b, 1), x)
        x = x[:, :KSPAN]
        first = (lax.broadcasted_iota(jnp.int32, (QBLK, KSPAN), 0) // CHUNK) * CHUNK
        col = lax.broadcasted_iota(jnp.int32, (QBLK, KSPAN), 1)
        o_ref[...] = jnp.where((col >= first) & (col < first + band), x, NEG)
        if ng:
            done()

    res = pl.pallas_call(
        body, name=name, grid=(L * H,), in_specs=[pl.BlockSpec((None, 1, NREL_PAD), lambda i: (i, 0, 0))] + [HBM] * ng,
        out_specs=[pl.BlockSpec((None, QBLK, KSPAN), lambda i: (i, 0, 0))] + [HBM] * ng,
        out_shape=[jax.ShapeDtypeStruct((L * H, QBLK, KSPAN), F32)] + _gather_out_shapes(shards),
        scratch_shapes=_gather_scratch(ng) if ng else [], compiler_params=_arb(1),
    )(tab, *shards)
    return res[0].reshape(L, H, QBLK, KSPAN), list(res[1:])


def _col_row_forms(t, S):
    nc = S // CHUNK
    return t.T.reshape(nc, CHUNK, B_HEADS), t.reshape(B_HEADS, nc, CHUNK).transpose(1, 0, 2)


def _weight_view(name, gathered, tag):
    if name in ("w_out", "w_ffn_out"):
        return gathered.reshape(8 * gathered.shape[2], gathered.shape[3])
    stacked = gathered.reshape(4, 2 * gathered.shape[2], gathered.shape[3])
    return _reorder_w_in(stacked[None], f"w_in_cols_{tag}")[0] if name == "w_in" else stacked


def _layer_fwd(l, x, mod, W, P, big, gather=None, late=None):
    S, D = x.shape
    n = lambda s: f"{s}_l{l}"
    sh1, sc1, gt1, sh2, sc2, gt2 = (mod[i:i + 1] for i in range(6))
    h1 = _lnmod_fwd(x, P["norm1_g"][l:l + 1], sc1, sh1, n("ln1"))
    if late is None:
        proj = _matmul(h1, W["w_in"], "nn", F32, n("proj"), tn=1152)
    else:
        proj, got = _matmul(h1, W["w_in"], "nn", F32, n("proj"), tn=1152, gather=(late[1], l))
        W = {**W, **{k: _weight_view(k, t, f"l{l}") for k, t in zip(late[0], got)}}
    part = (lambda names: ([gather[0][BIG.index(k)] for k in names], gather[1])) if gather is not None else (lambda names: None)
    ya, got_a = _attn_fwd(proj, big, n("attn"), part(CARRY_ATTN))
    alog, dtb = P["a_log"][l].reshape(B_HEADS, 1), P["dt_bias"][l].reshape(B_HEADS, 1)
    beta, gam, b_t, a_t = _gdn_gates_fwd(proj, alog, dtb, n("gates"))
    bcol, _ = _col_row_forms(beta, S)
    gcol, grow = _col_row_forms(gam, S)
    qn, kn, v = _gdn_pre_fwd(proj, P["w_conv"][l], n("gdnpre"))
    tsave, amat, u, w, got_l = _gdn_local_fwd(qn, kn, v, bcol, gcol, grow, n("gdnlocal"), part(CARRY_LOCAL))
    o, ssave, got_s = _gdn_scan_fwd(qn, kn, u, w, amat, gcol, grow, n("gdnscan"), part(CARRY_SCAN))
    yb = _gdn_post_fwd(o, proj, P["gdn_norm_g"][l:l + 1], n("gdnpost"))
    pa = _matmul(ya, W["w_branch_a"], "nn", BF16, n("pa"), tm=2048, stacked=True)
    pb = _matmul(yb, W["w_branch_b"], "nn", BF16, n("pb"), tm=2048, stacked=True)
    merged = _merge_fwd(proj, pa, pb, n("merge"))
    ao = _matmul(merged, W["w_out"], "nn", F32, n("ao"))
    x1 = _gate_fwd(x, ao, gt1, n("res1"))
    h2 = _lnmod_fwd(x1, P["norm2_g"][l:l + 1], sc2, sh2, n("ln2"))
    gu = _matmul(h2, W["w_ffn_in"], "nn", BF16, n("gu"), stacked=True, gather=part(CARRY_GU))
    gu, got_g = gu if gather is not None else (gu, [])
    got = dict(zip(CARRY_ATTN + CARRY_LOCAL + CARRY_SCAN + CARRY_GU, got_a + got_l + got_s + got_g))
    gathered = {k: got[k] for k in BIG} if gather is not None else None
    act = _ffn_act_fwd(gu, n("act"))
    fo = _matmul(act, W["w_ffn_out"], "nn", F32, n("fo"), tk=1408)
    x2 = _gate_fwd(x1, fo, gt2, n("res2"))
    saved = dict(x=x, h1=h1, proj=proj, ya=ya, b_t=b_t, a_t=a_t, bcol=bcol, gcol=gcol, grow=grow,
                 qn=qn, kn=kn, v=v, o=o, tsave=tsave, ssave=ssave, amat=amat, u=u, w=w, yb=yb, pa=pa, pb=pb,
                 merged=merged, ao=ao, x1=x1,
                 h2=h2, gu=gu, act=act, fo=fo)
    return x2, saved, gathered, W


def _layer_bwd(l, dx2, sv, mod, W, P, big, exchange=()):
    S, D = dx2.shape
    n = lambda s: f"{s}_l{l}"
    sh1, sc1, gt1, sh2, sc2, gt2 = (mod[i:i + 1] for i in range(6))
    g, pay = {}, {}
    view = lambda t: t.reshape((4, 2, t.shape[-2] // (2 if t.ndim == 3 else 8), t.shape[-1]))
    dz2, dgt2 = _gate_bwd(dx2, sv["fo"], gt2, n("dres2"))
    g["w_ffn_out"], pay["w_ffn_out"] = map(view, _matmul(sv["act"], dz2, "tn", F32, n("dwfo"), tm=1408, also_bf16=True))
    dact = _matmul(dz2, W["w_ffn_out"], "nt", BF16, n("dact"), tn=1408)
    dgu = _ffn_act_bwd(sv["gu"], dact, n("dgu"))
    g["w_ffn_in"], pay["w_ffn_in"] = map(view, _matmul(sv["h2"], dgu, "tn", F32, n("dwfi"), out_stacked=True,
                                                       also_bf16=True))
    dh2 = _matmul(dgu, W["w_ffn_in"], "nt", F32, n("dh2"), stacked=True)
    dx1, dsh2, dsc2, dn2 = _lnmod_bwd(dh2, sv["x1"], P["norm2_g"][l:l + 1], sc2, dx2, n("dln2"))
    dz1, dgt1 = _gate_bwd(dx1, sv["ao"], gt1, n("dres1"))
    g["w_out"], pay["w_out"] = map(view, _matmul(sv["merged"], dz1, "tn", F32, n("dwo"), also_bf16=True))
    dmerged = _matmul(dz1, W["w_out"], "nt", BF16, n("dmerged"))
    dproj, dpa, dpb = _merge_bwd(sv["proj"], sv["pa"], sv["pb"], dmerged, n("dmerge"))
    g["w_branch_a"], pay["w_branch_a"] = map(view, _matmul(sv["ya"], dpa, "tn", F32, n("dwa"), out_stacked=True,
                                                           also_bf16=True))
    g["w_branch_b"], pay["w_branch_b"] = map(view, _matmul(sv["yb"], dpb, "tn", F32, n("dwb"), out_stacked=True,
                                                           also_bf16=True))
    dya = _matmul(dpa, W["w_branch_a"], "nt", BF16, n("dya"), tm=2048, stacked=True)
    dyb = _matmul(dpb, W["w_branch_b"], "nt", F32, n("dyb"), tm=2048, stacked=True)
    ex = (lambda names: [exchange[BIG.index(k)] for k in names]) if len(exchange) else (lambda names: ())
    dq, dk, dv, dbig, rec_a = _attn_bwd(sv["proj"], big, dya, n("dattn"), ex(CARRY_DATTN))
    g["rel_table"] = _rel_table_grad(dbig, n("drel"))[:, 0, :2 * A_MAX_REL + 1]
    dproj = _write_columns(dproj, [dq, dk, dv], 3 * dq.shape[1], P_QKVA // (3 * dq.shape[1]), n("dqkva"))
    do, dproj, dng = _gdn_post_bwd(dyb, sv["o"], sv["proj"], P["gdn_norm_g"][l:l + 1], dproj, n("dgdnpost"))
    g["gdn_norm_g"] = dng[0]
    du, dw, dqd, dkd, da, dgl = _gdn_scan_bwd(sv["qn"], sv["kn"], sv["u"], sv["w"], sv["amat"], sv["gcol"], sv["grow"],
                                              sv["ssave"], do, n("dgdnscan"))
    dqn, dkn, dvv, dbc, dgc, dgr, rec_l = _gdn_local_bwd(
        sv["qn"], sv["kn"], sv["v"], sv["bcol"], sv["gcol"], sv["grow"], sv["tsave"], du, dw, dqd, dkd, da, dgl,
        n("dgdnlocal"), ex(CARRY_DLOCAL))
    rec = dict(zip(CARRY_DATTN + CARRY_DLOCAL, rec_a + rec_l))
    received = [rec[k] for k in BIG] if len(exchange) else None
    dbeta_t = dbc.reshape(S, B_HEADS).T
    dgam_a = dgc.reshape(S, B_HEADS).T
    dgam_b = dgr.transpose(1, 0, 2).reshape(B_HEADS, S)
    alog, dtb = P["a_log"][l].reshape(B_HEADS, 1), P["dt_bias"][l].reshape(B_HEADS, 1)
    db_t, da_t, dal, ddt = _gdn_gates_bwd(dbeta_t, dgam_a, dgam_b, sv["b_t"], sv["a_t"], alog, dtb, n("dgates"))
    g["a_log"], g["dt_bias"] = dal[:, 0], ddt[:, 0]
    dproj, g["w_conv"] = _gdn_pre_bwd(sv["proj"], P["w_conv"][l], dqn, dkn, dvv, dproj, n("dgdnpre"))
    dba = jnp.concatenate([db_t.T, da_t.T, jnp.zeros((S, P_END - P_BA - 2 * B_HEADS), F32)], axis=1)
    dproj = _write_columns(dproj, [dba], dba.shape[1], P_BA // dba.shape[1], n("dba"))
    g["w_in"], pay["w_in"] = map(view, _restore_w_in(_matmul(sv["h1"], dproj, "tn", F32, n("dwin"), tn=1152),
                                                     n("dwin_cols")))
    dh1, from_sibling = _matmul(dproj, W["w_in"], "nt", F32, n("dh1"), tk=1152, sibling=[pay[k] for k in BIG])
    dx, dsh1, dsc1, dn1 = _lnmod_bwd(dh1, sv["x"], P["norm1_g"][l:l + 1], sc1, dx1, n("dln1"))
    g["norm1_g"], g["norm2_g"] = dn1[0], dn2[0]
    dmod = jnp.concatenate([dsh1, dsc1, dgt1, dsh2, dsc2, dgt2], axis=1)[0]
    return dx, g, from_sibling, dmod, received


SMALL = ("norm1_g", "norm2_g", "rel_table", "w_conv", "a_log", "dt_bias", "gdn_norm_g")
SMALL_PACK_C = 1024


def _as_rows(t):
    flat = t.reshape(-1)
    rows = -(-flat.shape[0] // SMALL_PACK_C)
    return jnp.pad(flat, (0, rows * SMALL_PACK_C - flat.shape[0])).reshape(rows, SMALL_PACK_C)


def _pack_rows(parts):
    blk = jnp.concatenate([_as_rows(p) for p in parts], axis=0)
    return jnp.pad(blk, ((0, -blk.shape[0] % 8), (0, 0)))


def _unpack_rows(blk, shapes):
    out, r = [], 0
    for shp in shapes:
        size = int(np.prod(shp))
        rows = -(-size // SMALL_PACK_C)
        out.append(blk[..., r:r + rows, :].reshape(blk.shape[:-2] + (rows * SMALL_PACK_C,))[..., :size]
                   .reshape(blk.shape[:-2] + tuple(shp)))
        r += rows
    return out


def kernel(x, c, w_ada, b_ada, norm1_g, norm2_g, w_in, rel_table, w_conv, a_log, dt_bias, gdn_norm_g, w_branch_a, w_branch_b, w_out, w_ffn_in, w_ffn_out, final_g, loss_target, m_w_ada, m_b_ada, m_norm1_g, m_norm2_g, m_w_in, m_rel_table, m_w_conv, m_a_log, m_dt_bias, m_gdn_norm_g, m_w_branch_a, m_w_branch_b, m_w_out, m_w_ffn_in, m_w_ffn_out, m_final_g, v_w_ada, v_b_ada, v_norm1_g, v_norm2_g, v_w_in, v_rel_table, v_w_conv, v_a_log, v_dt_bias, v_gdn_norm_g, v_w_branch_a, v_w_branch_b, v_w_out, v_w_ffn_in, v_w_ffn_out, v_final_g):
    weights = dict(w_ada=w_ada, b_ada=b_ada, norm1_g=norm1_g, norm2_g=norm2_g, w_in=w_in, rel_table=rel_table,
                   w_conv=w_conv, a_log=a_log, dt_bias=dt_bias, gdn_norm_g=gdn_norm_g, w_branch_a=w_branch_a,
                   w_branch_b=w_branch_b, w_out=w_out, w_ffn_in=w_ffn_in, w_ffn_out=w_ffn_out, final_g=final_g)
    mom_m = dict(w_ada=m_w_ada, b_ada=m_b_ada, norm1_g=m_norm1_g, norm2_g=m_norm2_g, w_in=m_w_in,
                 rel_table=m_rel_table, w_conv=m_w_conv, a_log=m_a_log, dt_bias=m_dt_bias, gdn_norm_g=m_gdn_norm_g,
                 w_branch_a=m_w_branch_a, w_branch_b=m_w_branch_b, w_out=m_w_out, w_ffn_in=m_w_ffn_in,
                 w_ffn_out=m_w_ffn_out, final_g=m_final_g)
    mom_v = dict(w_ada=v_w_ada, b_ada=v_b_ada, norm1_g=v_norm1_g, norm2_g=v_norm2_g, w_in=v_w_in,
                 rel_table=v_rel_table, w_conv=v_w_conv, a_log=v_a_log, dt_bias=v_dt_bias, gdn_norm_g=v_gdn_norm_g,
                 w_branch_a=v_w_branch_a, w_branch_b=v_w_branch_b, w_out=v_w_out, w_ffn_in=v_w_ffn_in,
                 w_ffn_out=v_w_ffn_out, final_g=v_final_g)
    xi, yi, ci = _position()
    chip = 2 * xi + yi
    dev = 2 * chip + ci
    L, D = norm1_g.shape
    NMOD = b_ada.shape[1] // D
    ns = w_ada.shape[2]
    cs = w_conv.shape[2]

    first_blk = _pack_rows([c, w_conv])
    first_all = _allgather8(first_blk, "gather_c").reshape(8, first_blk.shape[0], SMALL_PACK_C)
    c_all, w_conv_all = _unpack_rows(first_all, [(D,), w_conv.shape])
    w_conv_full = w_conv_all.reshape(4, 2, L, CONV_K, cs)[:, 0].transpose(1, 2, 0, 3).reshape(L, CONV_K, 4 * cs)
    b_shard = lax.dynamic_slice_in_dim(b_ada, chip * ns, ns, axis=1).reshape(L, 1, ns)
    mod_shard = _ada_mod(c_all, w_ada, b_shard, "ada_mod")
    mod_all = _allgather8(mod_shard.reshape(L * 8, ns), "gather_mod").reshape(4, 2, L, 8, ns)
    mod = lax.dynamic_index_in_dim(mod_all[:, 0], dev, axis=2, keepdims=False)
    mod = mod.transpose(1, 0, 2).reshape(L, NMOD, D)

    shards = [weights[k].astype(BF16) for k in BIG]
    shards = [s.reshape(s.shape[0], 2, s.shape[1] // 2, s.shape[2]) for s in shards]
    P = dict(norm1_g=norm1_g, norm2_g=norm2_g, w_conv=w_conv_full, a_log=a_log, dt_bias=dt_bias,
             gdn_norm_g=gdn_norm_g)
    shard_of = dict(zip(BIG, shards))

    big, got = _band_bias(rel_table, "band_bias", ([shard_of["w_in"]], 0))
    alone = ["w_branch_a", "w_branch_b", "w_out", "w_ffn_out"]
    got += _gather_weights([shard_of[k] for k in alone], 0, "gather_weights_l0")
    W = [{k: _weight_view(k, t, "l0") for k, t in zip(["w_in"] + alone, got)}]
    late = (["w_ffn_in"], [shard_of["w_ffn_in"]])
    xc = x[0]
    saved = []
    for l in range(L):
        xc, sv, gathered, W[l] = _layer_fwd(l, xc, mod[l], W[l], P, big[l], (shards, l + 1) if l + 1 < L else None,
                                           late if l == 0 else None)
        saved.append(sv)
        if l + 1 < L:
            W.append({k: _weight_view(k, gathered[k], f"l{l + 1}") for k in BIG})
    dx, loss_dev, dfinal = _loss_head(xc, final_g.reshape(1, D), loss_target[0], "loss_head")

    where = jnp.stack([ci, chip]).astype(jnp.int32)
    grads = [None] * L
    dmods = [None] * L
    shard_grads = {k: [None] * L for k in BIG}

    def finish_reduce_scatter(l, sums, from_chips):
        halves = [_sum_own_and_received(s_[1], r_, where, f"rs_sum_{k}_l{l}")
                  for k, s_, r_ in zip(BIG, sums, from_chips)]
        for k, t in zip(BIG, _rs_pair(halves, f"rs_pair_l{l}")):
            shard_grads[k][l] = t.reshape(2 * t.shape[1], t.shape[2])

    pending = None
    for l in reversed(range(L)):
        exchange = [s_[0] for s_ in pending] if pending is not None else ()
        dx, grads[l], from_sibling, dmods[l], received = _layer_bwd(l, dx, saved[l], mod[l], W[l], P, big[l], exchange)
        if pending is not None:
            finish_reduce_scatter(l + 1, pending, received)
        gs = [grads[l][k] for k in BIG]
        pending = [_pair_sums(g_, where, r_, f"rs_pair_sum_{k}_l{l}") for k, g_, r_ in zip(BIG, gs, from_sibling)]
    dmod = jnp.stack(dmods)

    small = {k: jnp.stack([grads[l][k] for l in range(L)]) for k in SMALL}
    parts = [dmod] + [small[k] for k in SMALL] + [dfinal, loss_dev[0, :1]]
    small_blk = _pack_rows(parts)
    srows = small_blk.shape[0]
    small_all, small_sum = _allgather8(small_blk, "gather_small", reduce_rows=srows)
    shapes = [dmod.shape] + [small[k].shape for k in SMALL] + [(D,), (1,)]
    tot = _unpack_rows(small_sum, shapes)
    G = dict(zip(SMALL, tot[1:1 + len(SMALL)]))
    G["b_ada"] = tot[0].reshape(b_ada.shape)
    G["w_conv"] = lax.dynamic_slice_in_dim(G["w_conv"], chip * cs, cs, axis=2)
    G["final_g"] = tot[-2]
    loss = tot[-1][0]
    dmod_all = _unpack_rows(small_all.reshape(8, srows, SMALL_PACK_C), [dmod.shape])[0]
    dmod_cols = lax.dynamic_slice_in_dim(dmod_all, chip * ns, ns, axis=2).transpose(1, 0, 2)
    G["w_ada"] = _ada_wgrad(c_all, dmod_cols, "ada_wgrad")

    order = ["w_ada", "b_ada", "norm1_g", "norm2_g", "w_in", "rel_table", "w_conv", "a_log", "dt_bias", "gdn_norm_g",
             "w_branch_a", "w_branch_b", "w_out", "w_ffn_in", "w_ffn_out", "final_g"]
    deltas, new_m, new_v = {}, {}, {}
    payload = dict(zip(BIG, [s_[0] for s_ in pending]))
    alone = [k for k in BIG if k not in CARRY_ADAMW]
    rec = dict(zip(alone, _rs_chips([payload[k] for k in alone], "rs_chips_l0")))
    deltas["w_ada"], new_m["w_ada"], new_v["w_ada"], got = _adamw(
        w_ada, G["w_ada"], m_w_ada, v_w_ada, "adamw_w_ada", [payload[k] for k in CARRY_ADAMW])
    rec.update(zip(CARRY_ADAMW, got))
    finish_reduce_scatter(0, pending, [rec[k] for k in BIG])
    for k in order[1:]:
        w = weights[k]
        if k == "w_in":
            to_cols = lambda t: jnp.transpose(t, (2, 0, 1))
            from_cols = lambda t: jnp.transpose(t, (1, 2, 0))
            gt = to_cols(jnp.stack(shard_grads[k]))
            d_, m_, v_ = _adamw_lead(to_cols(w), gt, to_cols(mom_m[k]), to_cols(mom_v[k]), f"adamw_{k}",
                                     W_IN_SHARD // 30)
            G[k], deltas[k], new_m[k], new_v[k] = from_cols(gt), from_cols(d_), from_cols(m_), from_cols(v_)
            continue
        if k in BIG:
            G[k], deltas[k], new_m[k], new_v[k] = _adamw_layers(w, shard_grads[k], mom_m[k], mom_v[k], f"adamw_{k}")
            continue
        as2d = (lambda t: t.reshape(1, -1)) if w.ndim == 1 else (lambda t: t)
        d_, m_, v_ = _adamw(as2d(w), as2d(G[k]), as2d(mom_m[k]), as2d(mom_v[k]), f"adamw_{k}")
        deltas[k], new_m[k], new_v[k] = d_.reshape(w.shape), m_.reshape(w.shape), v_.reshape(w.shape)
    return (loss, dx[None], *[G[k] for k in order], *[deltas[k] for k in order], *[new_m[k] for k in order],
            *[new_v[k] for k in order])
```

```python
import functools

import numpy as np
import jax
import jax.numpy as jnp
from jax import lax
from jax.experimental import pallas as pl
from jax.experimental.pallas import tpu as pltpu

F32 = jnp.float32
BF16 = jnp.bfloat16
HI = lax.Precision.HIGHEST
SOLVE_PREC = lax.Precision.HIGH
MESH = pl.DeviceIdType.MESH

EPS = 1e-6
CHUNK = 64
A_HEADS = 8
A_DH = 64
A_PAST = 8
A_MAX_REL = 128
B_HEADS = 4
B_DH = 128
CONV_K = 4
LANE = 128
QBLK = 4 * CHUNK
KSPAN = QBLK + A_PAST * CHUNK
NEG = -1e30

ADAM_LR = 0.001
ADAM_B1 = 0.9
ADAM_B2 = 0.999
ADAM_EPS = 1e-08
ADAM_WD = 0.01
ADAM_STEP = 10

P_QKVA, P_QKVB, P_GA, P_GB, P_Z, P_BA, P_END = 0, 1536, 3072, 4096, 5120, 5632, 5760
W_IN_SHARD = 1410


def _sigmoid(x):
    return 1.0 / (1.0 + jnp.exp(-x))


def _dot(a, b, ca, cb, prec):
    lead = a.ndim - 2
    batch = ((0,), (0,)) if lead else ((), ())
    return lax.dot_general(a, b, (((ca + lead,), (cb + lead,)), batch), precision=prec, preferred_element_type=F32)


def _nn(a, b, prec=None):
    return _dot(a, b, 1, 0, prec)


def _nt(a, b, prec=None):
    return _dot(a, b, 1, 1, prec)


def _tn(a, b, prec=None):
    return _dot(a, b, 0, 0, prec)


def _bnn(a, b):
    return _nn(a.astype(BF16), b.astype(BF16))


def _bnt(a, b):
    return _nt(a.astype(BF16), b.astype(BF16))


def _btn(a, b):
    return _tn(a.astype(BF16), b.astype(BF16))


def _pick(n, target, unit=LANE):
    best = None
    for t in range(unit, min(n, target) + 1, unit):
        if n % t == 0:
            best = t
    return best if best is not None else n


def _acc(ref, val, i):
    @pl.when(i == 0)
    def _():
        ref[...] = val

    @pl.when(i != 0)
    def _():
        ref[...] += val


def _arb(n):
    return pltpu.CompilerParams(dimension_semantics=("arbitrary",) * n)


def _par(n):
    return pltpu.CompilerParams(dimension_semantics=("parallel",) * n)


def _matmul(a, b, mode, out_dtype, name, tm=1024, tn=1024, tk=1024, layer=None, stacked=False, out_stacked=False,
            also_bf16=False, gather=None, sibling=None, residual=None):
    bs = b.shape[1:] if layer is not None else b.shape
    if mode == "nn":
        M, K = a.shape
        N = 4 * bs[2] if stacked else bs[1]
        if stacked:
            tn = bs[2]
    elif mode == "nt":
        M, K = a.shape
        N = bs[1] if stacked else bs[0]
        if stacked:
            tk = bs[2]
    else:
        K, M = a.shape
        N = bs[1]
        if out_stacked:
            tn = N // 4
    tm, tn, tk = _pick(M, tm), _pick(N, tn), _pick(K, tk)
    nk = K // tk
    lead = () if layer is None else (layer,)
    lead_blk = () if layer is None else (None,)
    if mode == "nn":
        a_spec = pl.BlockSpec((tm, tk), lambda i, j, k: (i, k))
        if stacked:
            b_spec = pl.BlockSpec(lead_blk + (None, tk, tn), lambda i, j, k: lead + (j, k, 0))
        else:
            b_spec = pl.BlockSpec(lead_blk + (tk, tn), lambda i, j, k: lead + (k, j))
        dot = _nn
    elif mode == "nt":
        a_spec = pl.BlockSpec((tm, tk), lambda i, j, k: (i, k))
        if stacked:
            b_spec = pl.BlockSpec(lead_blk + (None, tn, tk), lambda i, j, k: lead + (k, j, 0))
        else:
            b_spec = pl.BlockSpec(lead_blk + (tn, tk), lambda i, j, k: lead + (j, k))
        dot = _nt
    else:
        a_spec = pl.BlockSpec((tk, tm), lambda i, j, k: (k, i))
        b_spec = pl.BlockSpec((tk, tn), lambda i, j, k: (k, j))
        dot = _tn
    if out_stacked:
        o_spec = pl.BlockSpec((None, tm, tn), lambda i, j, k: (j, i, 0))
        o_shape = jax.ShapeDtypeStruct((4, M, tn), out_dtype)
    else:
        o_spec = pl.BlockSpec((tm, tn), lambda i, j, k: (i, j))
        o_shape = jax.ShapeDtypeStruct((M, N), out_dtype)

    if gather is not None:
        shards, glayer = gather
        carried_shapes, carried_scratch = _gather_out_shapes(shards), _gather_scratch(len(shards))
    elif sibling is not None:
        shards = sibling
        carried_shapes, carried_scratch = _sibling_out_shapes(shards), _sibling_scratch(len(shards))
    else:
        shards, carried_shapes, carried_scratch = (), [], []
    ng = len(shards)
    o_shapes = [o_shape] + ([jax.ShapeDtypeStruct(o_shape.shape, BF16)] if also_bf16 else [])
    nr = 0 if residual is None else 2
    if nr:
        o_shapes.append(jax.ShapeDtypeStruct(o_shape.shape, F32))
    no = len(o_shapes)
    gi, gj = M // tm, N // tn

    def body(a_ref, b_ref, *refs):
        res_refs, refs = refs[:nr], refs[nr:]
        srcs, o_refs, gouts, scratch = refs[:ng], refs[ng:ng + no], refs[ng + no:2 * ng + no], refs[2 * ng + no:]

        def write(o_refs, val):
            for o_ref in o_refs[:no - (1 if nr else 0)]:
                o_ref[...] = val.astype(o_ref.dtype)
            if nr:
                o_refs[-1][...] = res_refs[0][...] + res_refs[1][...] * val

        i, j, k = pl.program_id(0), pl.program_id(1), pl.program_id(2)
        if ng:
            start = (j == 0) & (k == 0)
            sems = scratch[len(scratch) - len(carried_scratch):]
            phases = (_gather_phases(glayer, srcs, gouts, *sems) if gather is not None
                      else _sibling_phases(srcs, gouts, *sems))
            done = _carry(phases, (i == 0) & start, (i == gi - 1) & start,
                          (i == gi - 1) & (j == gj - 1) & (k == nk - 1))
        if nk == 1:
            write(o_refs, dot(a_ref[...], b_ref[...]))
        else:
            acc_ref = scratch[0]

            @pl.when(k == 0)
            def _():
                acc_ref[...] = jnp.zeros_like(acc_ref)

            acc_ref[...] += dot(a_ref[...], b_ref[...])

            @pl.when(k == nk - 1)
            def _():
                write(o_refs, acc_ref[...])
        if ng:
            done()

    sem = ("arbitrary",) * 3 if ng else ("parallel", "parallel", "arbitrary")
    res = pl.pallas_call(
        body, name=name, grid=(gi, gj, nk),
        in_specs=[a_spec, b_spec] + ([o_spec, pl.BlockSpec((1, tn), lambda i, j, k: (0, j))] if nr else []) + [HBM] * ng,
        out_specs=[o_spec] * no + [HBM] * ng, out_shape=o_shapes + carried_shapes,
        scratch_shapes=([] if nk == 1 else [pltpu.VMEM((tm, tn), F32)]) + carried_scratch,
        compiler_params=pltpu.CompilerParams(dimension_semantics=sem),
    )(a, b, *(residual or ()), *shards)
    out = tuple(res[:no]) if (also_bf16 or nr) else res[0]
    return (out, list(res[no:])) if ng else out


def _rows(tm, n, col=0):
    return pl.BlockSpec((tm, n), lambda i: (i, col))


def _vec(n):
    return pl.BlockSpec((1, n), lambda i: (0, 0))


def _lnmod_fwd(x, g, sc, sh, name):
    S, D = x.shape
    tm = _pick(S, 512, 8)

    def body(x_ref, g_ref, sc_ref, sh_ref, o_ref):
        xv = x_ref[...]
        r = lax.rsqrt(jnp.mean(xv * xv, axis=-1, keepdims=True) + EPS)
        o_ref[...] = ((xv * r * g_ref[...]) * (1.0 + sc_ref[...]) + sh_ref[...]).astype(BF16)

    return pl.pallas_call(
        body, name=name, grid=(S // tm,),
        in_specs=[_rows(tm, D), _vec(D), _vec(D), _vec(D)], out_specs=_rows(tm, D),
        out_shape=jax.ShapeDtypeStruct((S, D), BF16), compiler_params=_par(1),
    )(x, g, sc, sh)


def _lnmod_bwd(dh, x, g, sc, dres, name):
    S, D = x.shape
    tm = _pick(S, 512, 8)

    def body(dh_ref, x_ref, g_ref, sc_ref, dres_ref, dx_ref, dsh_ref, dsc_ref, dg_ref):
        i = pl.program_id(0)
        xv = x_ref[...]
        dh_ = dh_ref[...]
        r = lax.rsqrt(jnp.mean(xv * xv, axis=-1, keepdims=True) + EPS)
        xhat = xv * r
        gv = g_ref[...]
        dn = dh_ * (1.0 + sc_ref[...])
        dxhat = dn * gv
        dx_ref[...] = dres_ref[...] + r * (dxhat - xhat * jnp.mean(dxhat * xhat, axis=-1, keepdims=True))
        _acc(dsh_ref, jnp.sum(dh_, axis=0, keepdims=True), i)
        _acc(dsc_ref, jnp.sum(dh_ * (xhat * gv), axis=0, keepdims=True), i)
        _acc(dg_ref, jnp.sum(dn * xhat, axis=0, keepdims=True), i)

    return pl.pallas_call(
        body, name=name, grid=(S // tm,),
        in_specs=[_rows(tm, D), _rows(tm, D), _vec(D), _vec(D), _rows(tm, D)],
        out_specs=[_rows(tm, D), _vec(D), _vec(D), _vec(D)],
        out_shape=[jax.ShapeDtypeStruct((S, D), F32)] + [jax.ShapeDtypeStruct((1, D), F32)] * 3,
        compiler_params=_arb(1),
    )(dh, x, g, sc, dres)


def _gate_bwd(dx, y, gt, name):
    S, D = dx.shape
    tm = _pick(S, 512, 8)

    def body(dx_ref, y_ref, gt_ref, dz_ref, dgt_ref):
        i = pl.program_id(0)
        d = dx_ref[...]
        dz_ref[...] = (d * gt_ref[...]).astype(BF16)
        _acc(dgt_ref, jnp.sum(d * y_ref[...], axis=0, keepdims=True), i)

    return pl.pallas_call(
        body, name=name, grid=(S // tm,), in_specs=[_rows(tm, D), _rows(tm, D), _vec(D)],
        out_specs=[_rows(tm, D), _vec(D)],
        out_shape=[jax.ShapeDtypeStruct((S, D), BF16), jax.ShapeDtypeStruct((1, D), F32)],
        compiler_params=_arb(1),
    )(dx, y, gt)


def _ffn_act_fwd(gu, name):
    S, H2 = gu.shape
    H = H2 // 2
    tm = _pick(S, 256, 8)

    def body(g_ref, u_ref, o_ref):
        gv = g_ref[...].astype(F32)
        o_ref[...] = (gv * _sigmoid(gv) * u_ref[...].astype(F32)).astype(BF16)

    return pl.pallas_call(
        body, name=name, grid=(S // tm,), in_specs=[_rows(tm, H, 0), _rows(tm, H, 1)], out_specs=_rows(tm, H),
        out_shape=jax.ShapeDtypeStruct((S, H), BF16), compiler_params=_par(1),
    )(gu, gu)


def _ffn_act_bwd(gu, dact, name):
    S, H2 = gu.shape
    H = H2 // 2
    tm = _pick(S, 256, 8)

    def body(g_ref, u_ref, da_ref, o_ref):
        gv = g_ref[...].astype(F32)
        s = _sigmoid(gv)
        da = da_ref[...].astype(F32)
        o_ref[:, :H] = (da * u_ref[...].astype(F32) * (s * (1.0 + gv * (1.0 - s)))).astype(BF16)
        o_ref[:, H:] = (da * (gv * s)).astype(BF16)

    return pl.pallas_call(
        body, name=name, grid=(S // tm,), in_specs=[_rows(tm, H, 0), _rows(tm, H, 1), _rows(tm, H)],
        out_specs=_rows(tm, H2), out_shape=jax.ShapeDtypeStruct((S, H2), BF16), compiler_params=_par(1),
    )(gu, gu, dact)


def _merge_fwd(proj, pa, pb, name):
    S, D = pa.shape
    tm = _pick(S, 512, 8)

    def body(ga_ref, gb_ref, pa_ref, pb_ref, o_ref):
        o_ref[...] = (_sigmoid(ga_ref[...]) * pa_ref[...].astype(F32)
                      + _sigmoid(gb_ref[...]) * pb_ref[...].astype(F32)).astype(BF16)

    return pl.pallas_call(
        body, name=name, grid=(S // tm,),
        in_specs=[_rows(tm, D, P_GA // D), _rows(tm, D, P_GB // D), _rows(tm, D), _rows(tm, D)],
        out_specs=_rows(tm, D), out_shape=jax.ShapeDtypeStruct((S, D), BF16), compiler_params=_par(1),
    )(proj, proj, pa, pb)


def _merge_bwd(proj, pa, pb, dm, name):
    S, D = pa.shape
    tm = _pick(S, 512, 16)
    rows_j = pl.BlockSpec((tm, D), lambda i, j: (i, 0))

    def body(g_ref, pa_ref, pb_ref, dm_ref, dg_ref, dpa_ref, dpb_ref):
        d = dm_ref[...].astype(F32)
        s = _sigmoid(g_ref[...])
        for branch, p_ref, dp_ref in ((0, pa_ref, dpa_ref), (1, pb_ref, dpb_ref)):
            @pl.when(pl.program_id(1) == branch)
            def _():
                dg_ref[...] = (d * p_ref[...].astype(F32) * s * (1.0 - s)).astype(BF16)
                dp_ref[...] = (d * s).astype(BF16)

    return pl.pallas_call(
        body, name=name, grid=(S // tm, 2),
        in_specs=[pl.BlockSpec((tm, D), lambda i, j: (i, P_GA // D + j)), rows_j, rows_j, rows_j],
        out_specs=[pl.BlockSpec((tm, D), lambda i, j: (i, P_GA // D + j)), rows_j, rows_j],
        out_shape=[jax.ShapeDtypeStruct((S, P_END), BF16), jax.ShapeDtypeStruct((S, D), BF16),
                   jax.ShapeDtypeStruct((S, D), BF16)],
        compiler_params=_arb(2),
    )(proj, pa, pb, dm)


def _write_columns(buf, parts, width, colblk, name):
    S = buf.shape[0]
    tm = _pick(S, 512, 16)
    n = len(parts)

    def body(*refs):
        o_ref = refs[n + 1]
        off = 0
        for p_ref in refs[:n]:
            w = p_ref.shape[1]
            o_ref[:, off:off + w] = p_ref[...].astype(BF16)
            off += w

    return pl.pallas_call(
        body, name=name, grid=(S // tm,), in_specs=[_rows(tm, p.shape[1]) for p in parts] + [HBM],
        out_specs=_rows(tm, width, colblk), out_shape=jax.ShapeDtypeStruct(buf.shape, buf.dtype),
        input_output_aliases={n: 0}, compiler_params=_par(1),
    )(*parts, buf)


def _loss_head(x, g, target, name):
    S, D = x.shape
    tm = _pick(S, 512, 8)

    def body(x_ref, g_ref, t_ref, dx_ref, loss_ref, dg_ref):
        i = pl.program_id(0)
        xv = x_ref[...]
        gv = g_ref[...]
        r = lax.rsqrt(jnp.mean(xv * xv, axis=-1, keepdims=True) + EPS)
        xhat = xv * r
        err = xhat * gv - t_ref[...]
        part = 0.5 * jnp.sum(jnp.mean(err * err, axis=-1, keepdims=True), axis=0, keepdims=True)
        _acc(loss_ref, jnp.broadcast_to(part, (1, LANE)), i)
        dy = err * (1.0 / D)
        _acc(dg_ref, jnp.sum(dy * xhat, axis=0, keepdims=True), i)
        dxhat = dy * gv
        dx_ref[...] = r * (dxhat - xhat * jnp.mean(dxhat * xhat, axis=-1, keepdims=True))

    return pl.pallas_call(
        body, name=name, grid=(S // tm,), in_specs=[_rows(tm, D), _vec(D), _rows(tm, D)],
        out_specs=[_rows(tm, D), _vec(LANE), _vec(D)],
        out_shape=[jax.ShapeDtypeStruct((S, D), F32), jax.ShapeDtypeStruct((1, LANE), F32),
                   jax.ShapeDtypeStruct((1, D), F32)],
        compiler_params=_arb(1),
    )(x, g, target)


HEADS_PER_SLAB = LANE // A_DH
N_SLABS = A_HEADS // HEADS_PER_SLAB
SPAN_BLOCKS = KSPAN // QBLK


def _attn_specs(seg):
    q_spec = pl.BlockSpec((QBLK, LANE), lambda p, m: (m, seg[0] * N_SLABS + p))
    k_specs = [pl.BlockSpec((QBLK, LANE), functools.partial(
        lambda j, p, m: (jnp.maximum(m - (SPAN_BLOCKS - 1) + j, 0), seg[1] * N_SLABS + p), j)) for j in range(SPAN_BLOCKS)]
    v_specs = [pl.BlockSpec((QBLK, LANE), functools.partial(
        lambda j, p, m: (jnp.maximum(m - (SPAN_BLOCKS - 1) + j, 0), seg[2] * N_SLABS + p), j)) for j in range(SPAN_BLOCKS)]
    b_spec = pl.BlockSpec((HEADS_PER_SLAB, QBLK, KSPAN), lambda p, m: (p, 0, 0))
    return q_spec, k_specs, v_specs, b_spec


def _head_lanes(t, hh):
    lane = lax.broadcasted_iota(jnp.int32, t.shape, 1)
    return jnp.where((lane // A_DH) == hh, t, jnp.zeros_like(t))


def _front_mask(m):
    col = lax.broadcasted_iota(jnp.int32, (QBLK, KSPAN), 1)
    return jnp.where(col < (SPAN_BLOCKS - 1 - m) * QBLK, NEG, 0.0)


def _attn_probs(qk, bias, front):
    s = qk * (A_DH ** -0.5) + (bias + front)
    p = jnp.exp(s - jnp.max(s, axis=-1, keepdims=True))
    return p * (1.0 / jnp.sum(p, axis=-1, keepdims=True))


def _grid_ends(nq):
    p, m = pl.program_id(0), pl.program_id(1)
    return (p == 0) & (m == 0), (p == N_SLABS - 1) & (m == nq // 2), (p == N_SLABS - 1) & (m == nq - 1)


def _attn_fwd(proj, big, name, gather=None):
    S = proj.shape[0]
    q_spec, k_specs, v_specs, b_spec = _attn_specs((0, 1, 2))
    shards, layer = gather if gather is not None else ((), None)
    ng = len(shards)

    def body(q_ref, k0, k1, k2, v0, v1, v2, b_ref, *rest):
        srcs, o_ref, gouts, sems = rest[:ng], rest[ng], rest[ng + 1:2 * ng + 1], rest[2 * ng + 1:]
        done = _carry(_gather_phases(layer, srcs, gouts, *sems), *_grid_ends(S // QBLK)) if ng else None
        m = pl.program_id(1)
        q = q_ref[...].astype(BF16)
        k = jnp.concatenate([k0[...], k1[...], k2[...]], axis=0).astype(BF16)
        v = jnp.concatenate([v0[...], v1[...], v2[...]], axis=0).astype(BF16)
        front = _front_mask(m)
        heads = range(HEADS_PER_SLAB)
        scores = [_nt(_head_lanes(q, hh), k) for hh in heads]
        probs = [_attn_probs(scores[hh], b_ref[hh], front).astype(BF16) for hh in heads]
        outs = [_nn(probs[hh], v) for hh in heads]
        lane = lax.broadcasted_iota(jnp.int32, (QBLK, LANE), 1)
        o_ref[...] = jnp.where(lane < A_DH, outs[0], outs[1]).astype(BF16)
        if ng:
            done()

    res = pl.pallas_call(
        body, name=name, grid=(N_SLABS, S // QBLK), in_specs=[q_spec] + k_specs + v_specs + [b_spec] + [HBM] * ng,
        out_specs=[pl.BlockSpec((QBLK, LANE), lambda p, m: (m, p))] + [HBM] * ng,
        out_shape=[jax.ShapeDtypeStruct((S, A_HEADS * A_DH), BF16)] + _gather_out_shapes(shards),
        scratch_shapes=_gather_scratch(ng) if ng else [], compiler_params=_arb(2),
    )(proj, proj, proj, proj, proj, proj, proj, big, *shards)
    return res[0], list(res[1:])


def _attn_bwd(proj, big, dya, name, exchange=()):
    S = proj.shape[0]
    W = A_HEADS * A_DH
    q_spec, k_specs, v_specs, b_spec = _attn_specs((0, 1, 2))
    out_q = pl.BlockSpec((QBLK, LANE), lambda p, m: (m, p))
    out_kv = pl.BlockSpec((S, LANE), lambda p, m: (0, p))
    ne = len(exchange)

    def body(q_ref, k0, k1, k2, v0, v1, v2, b_ref, do_ref, *rest):
        srcs, (dq_ref, dk_ref, dv_ref, db_ref), eouts, sems = rest[:ne], rest[ne:ne + 4], rest[ne + 4:2 * ne + 4], rest[2 * ne + 4:]
        done = _carry(_chips_phases(srcs, eouts, *sems), *_grid_ends(S // QBLK)) if ne else None
        m = pl.program_id(1)

        @pl.when(m == 0)
        def _():
            dk_ref[...] = jnp.zeros_like(dk_ref)
            dv_ref[...] = jnp.zeros_like(dv_ref)
            db_ref[...] = jnp.zeros_like(db_ref)

        q = q_ref[...].astype(BF16)
        k = jnp.concatenate([k0[...], k1[...], k2[...]], axis=0).astype(BF16)
        v = jnp.concatenate([v0[...], v1[...], v2[...]], axis=0).astype(BF16)
        do = do_ref[...]
        front = _front_mask(m)
        heads = range(HEADS_PER_SLAB)
        qh = [_head_lanes(q, hh) for hh in heads]
        doh = [_head_lanes(do, hh) for hh in heads]
        scores = [_nt(qh[hh], k) for hh in heads]
        dps = [_nt(doh[hh], v) for hh in heads]
        ps = [_attn_probs(scores[hh], b_ref[hh], front) for hh in heads]
        dss = [ps[hh] * (dps[hh] - jnp.sum(ps[hh] * dps[hh], axis=-1, keepdims=True)) for hh in heads]
        for hh in heads:
            db_ref[hh] += dss[hh]
        dsb = [(dss[hh] * (A_DH ** -0.5)).astype(BF16) for hh in heads]
        dqs = [_nn(dsb[hh], k) for hh in heads]
        dk = sum(_tn(dsb[hh], qh[hh]) for hh in heads)
        dv = sum(_tn(ps[hh].astype(BF16), doh[hh]) for hh in heads)
        lane = lax.broadcasted_iota(jnp.int32, (QBLK, LANE), 1)
        dq_ref[...] = jnp.where(lane < A_DH, dqs[0], dqs[1])
        for j in range(SPAN_BLOCKS):
            blk = m - (SPAN_BLOCKS - 1) + j

            @pl.when(blk >= 0)
            def _():
                off = pl.multiple_of(blk * QBLK, QBLK)
                dk_ref[pl.ds(off, QBLK), :] += dk[j * QBLK:(j + 1) * QBLK]
                dv_ref[pl.ds(off, QBLK), :] += dv[j * QBLK:(j + 1) * QBLK]
        if ne:
            done()

    res = pl.pallas_call(
        body, name=name, grid=(N_SLABS, S // QBLK),
        in_specs=[q_spec] + k_specs + v_specs + [b_spec, pl.BlockSpec((QBLK, LANE), lambda p, m: (m, p))] + [HBM] * ne,
        out_specs=[out_q, out_kv, out_kv, b_spec] + [HBM] * ne,
        out_shape=[jax.ShapeDtypeStruct((S, W), F32)] * 3 + [jax.ShapeDtypeStruct((A_HEADS, QBLK, KSPAN), F32)]
        + _chips_out_shapes(exchange),
        scratch_shapes=_chips_scratch(ne) if ne else [], compiler_params=_arb(2),
    )(proj, proj, proj, proj, proj, proj, proj, big, dya, *exchange)
    return tuple(res[:4]) + (list(res[4:]),)


NREL_PAD = 3 * LANE
SKEW_W = 1024


def _rel_table_grad(dbig, name):
    H, R, C = dbig.shape

    def body(d_ref, o_ref):
        x = jnp.concatenate([d_ref[...], jnp.zeros((R, SKEW_W - C), F32)], axis=1)
        row = lax.broadcasted_iota(jnp.int32, (R, SKEW_W), 0)
        for b in range(R.bit_length() - 1):
            x = jnp.where(((row >> b) & 1) == 1, pltpu.roll(x, SKEW_W - (1 << b), 1), x)
        e = jnp.sum(x, axis=0, keepdims=True)
        xi = lax.broadcasted_iota(jnp.int32, (SKEW_W, NREL_PAD), 0)
        r = lax.broadcasted_iota(jnp.int32, (SKEW_W, NREL_PAD), 1)
        diag = jnp.where(xi < C, xi, xi - SKEW_W)
        rel = jnp.clip(A_PAST * CHUNK - diag, -A_MAX_REL, A_MAX_REL) + A_MAX_REL
        o_ref[...] = _nn(e, jnp.where(rel == r, 1.0, 0.0).astype(F32), HI)

    return pl.pallas_call(
        body, name=name, grid=(H,), in_specs=[pl.BlockSpec((None, R, C), lambda h: (h, 0, 0))],
        out_specs=pl.BlockSpec((None, 1, NREL_PAD), lambda h: (h, 0, 0)),
        out_shape=jax.ShapeDtypeStruct((H, 1, NREL_PAD), F32), compiler_params=_par(1),
    )(dbig)


def _chunk_cumsum_matrix(n, reverse):
    j = lax.broadcasted_iota(jnp.int32, (n, n), 0)
    i = lax.broadcasted_iota(jnp.int32, (n, n), 1)
    same = (j // CHUNK) == (i // CHUNK)
    return jnp.where(same & ((j >= i) if reverse else (j <= i)), 1.0, 0.0).astype(F32)


def _gdn_gates_fwd(proj, alog, dtb, name):
    Hh, S = B_HEADS, proj.shape[0]
    tl = _pick(S, 512)
    row = pl.BlockSpec((Hh, tl), lambda i: (0, i))
    col = pl.BlockSpec((Hh, 1), lambda i: (0, 0))

    def body(ba_ref, al_ref, dt_ref, beta_ref, gam_ref, b_ref, a_ref, t_ref):
        t_ref[...] = ba_ref[...].T
        b = t_ref[0:Hh, :]
        a = t_ref[Hh:2 * Hh, :]
        z = a + dt_ref[...]
        sp = jnp.maximum(z, 0.0) + jnp.log(1.0 + jnp.exp(-jnp.abs(z)))
        g = -jnp.exp(al_ref[...]) * sp
        beta_ref[...] = _sigmoid(b)
        gam_ref[...] = _nn(g, _chunk_cumsum_matrix(tl, False), HI)
        b_ref[...] = b
        a_ref[...] = a

    return pl.pallas_call(
        body, name=name, grid=(S // tl,), in_specs=[pl.BlockSpec((tl, LANE), lambda i: (i, P_BA // LANE)), col, col],
        out_specs=[row] * 4, out_shape=[jax.ShapeDtypeStruct((Hh, S), F32)] * 4,
        scratch_shapes=[pltpu.VMEM((LANE, tl), F32)], compiler_params=_par(1),
    )(proj, alog, dtb)


def _gdn_gates_bwd(dbeta, dgam_a, dgam_b, b_t, a_t, alog, dtb, name):
    Hh, S = b_t.shape
    tl = _pick(S, 512)
    row = pl.BlockSpec((Hh, tl), lambda i: (0, i))
    col = pl.BlockSpec((Hh, 1), lambda i: (0, 0))
    accs = pl.BlockSpec((Hh, LANE), lambda i: (0, 0))

    def body(dbeta_ref, dga_ref, dgb_ref, b_ref, a_ref, al_ref, dt_ref, db_ref, da_ref, dal_ref, ddt_ref):
        i = pl.program_id(0)
        z = a_ref[...] + dt_ref[...]
        sp = jnp.maximum(z, 0.0) + jnp.log(1.0 + jnp.exp(-jnp.abs(z)))
        ea = jnp.exp(al_ref[...])
        dg = _nn(dga_ref[...] + dgb_ref[...], _chunk_cumsum_matrix(tl, True), HI)
        da = dg * (-ea) * _sigmoid(z)
        beta = _sigmoid(b_ref[...])
        db_ref[...] = dbeta_ref[...] * beta * (1.0 - beta)
        da_ref[...] = da
        _acc(dal_ref, jnp.broadcast_to(jnp.sum(dg * (-ea * sp), axis=1, keepdims=True), (Hh, LANE)), i)
        _acc(ddt_ref, jnp.broadcast_to(jnp.sum(da, axis=1, keepdims=True), (Hh, LANE)), i)

    return pl.pallas_call(
        body, name=name, grid=(S // tl,), in_specs=[row] * 5 + [col, col], out_specs=[row, row, accs, accs],
        out_shape=[jax.ShapeDtypeStruct((Hh, S), F32)] * 2 + [jax.ShapeDtypeStruct((Hh, LANE), F32)] * 2,
        compiler_params=_arb(1),
    )(dbeta, dgam_a, dgam_b, b_t, a_t, alog, dtb)


HALO = 8


def _conv_silu(xx_ref, w_ref, tm):
    y = w_ref[0:1, :] * xx_ref[pl.ds(HALO - CONV_K + 1, tm), :]
    for j in range(1, CONV_K):
        y = y + w_ref[j:j + 1, :] * xx_ref[pl.ds(HALO - CONV_K + 1 + j, tm), :]
    return y, y * _sigmoid(y)


def _fill_prev_halo(xx_ref, x_ref, prev_ref, i, tm):
    xx_ref[pl.ds(HALO, tm), :] = x_ref[...]

    @pl.when(i == 0)
    def _():
        xx_ref[pl.ds(0, HALO), :] = jnp.zeros((HALO, xx_ref.shape[1]), F32)

    @pl.when(i != 0)
    def _():
        xx_ref[pl.ds(0, HALO), :] = prev_ref[...]


def _gdn_pre_specs(tm, C, colblk):
    cur = pl.BlockSpec((tm, C), lambda i: (i, colblk))
    prev = pl.BlockSpec((HALO, C), lambda i: (jnp.maximum(i * (tm // HALO) - 1, 0), colblk))
    return cur, prev


def _gdn_pre_fwd(proj, wconv, name):
    S = proj.shape[0]
    C = 3 * B_HEADS * B_DH
    W = B_HEADS * B_DH
    tm = _pick(S, 256, 8)
    cur, prev = _gdn_pre_specs(tm, C, P_QKVB // C)

    def body(x_ref, prev_ref, w_ref, q_ref, k_ref, v_ref, xx_ref):
        i = pl.program_id(0)
        _fill_prev_halo(xx_ref, x_ref, prev_ref, i, tm)
        _, sl = _conv_silu(xx_ref, w_ref, tm)
        for h in range(B_HEADS):
            hs = slice(h * B_DH, (h + 1) * B_DH)
            q = sl[:, h * B_DH:(h + 1) * B_DH]
            k = sl[:, W + h * B_DH:W + (h + 1) * B_DH]
            q_ref[:, hs] = q * (lax.rsqrt(jnp.sum(q * q, axis=-1, keepdims=True) + EPS) * (B_DH ** -0.5))
            k_ref[:, hs] = k * lax.rsqrt(jnp.sum(k * k, axis=-1, keepdims=True) + EPS)
        v_ref[...] = sl[:, 2 * W:]

    return pl.pallas_call(
        body, name=name, grid=(S // tm,), in_specs=[cur, prev, pl.BlockSpec((CONV_K, C), lambda i: (0, 0))],
        out_specs=[_rows(tm, W)] * 3, out_shape=[jax.ShapeDtypeStruct((S, W), F32)] * 3,
        scratch_shapes=[pltpu.VMEM((HALO + tm, C), F32)], compiler_params=_par(1),
    )(proj, proj, wconv)


def _conv_rows(xx_ref, w_ref, start, n):
    base = HALO - CONV_K + 1 + start
    y = w_ref[0:1, :] * xx_ref[pl.ds(base, n), :]
    for j in range(1, CONV_K):
        y = y + w_ref[j:j + 1, :] * xx_ref[pl.ds(base + j, n), :]
    return y


def _pre_dy(y, dq, dk, dv):
    W = B_HEADS * B_DH
    sg = _sigmoid(y)
    sl = y * sg
    dsilu = sg * (1.0 + y * (1.0 - sg))
    parts = []
    for base, d_all, c in ((0, dq, B_DH ** -0.5), (W, dk, 1.0)):
        for h in range(B_HEADS):
            lo = base + h * B_DH
            t = sl[:, lo:lo + B_DH]
            d = d_all[:, h * B_DH:(h + 1) * B_DH]
            r = lax.rsqrt(jnp.sum(t * t, axis=-1, keepdims=True) + EPS)
            parts.append((c * r) * (d - t * (r * r) * jnp.sum(d * t, axis=-1, keepdims=True)) * dsilu[:, lo:lo + B_DH])
    parts.append(dv * dsilu[:, 2 * W:])
    return jnp.concatenate(parts, axis=1)


def _gdn_pre_bwd(proj, wconv, dqn, dkn, dv, dproj, name):
    S = proj.shape[0]
    C = 3 * B_HEADS * B_DH
    W = B_HEADS * B_DH
    tm = _pick(S, 256, 16)
    nt_ = S // tm
    cur, prev = _gdn_pre_specs(tm, C, P_QKVB // C)
    after = lambda i: jnp.minimum((i + 1) * (tm // HALO), S // HALO - 1)
    next_x = pl.BlockSpec((HALO, C), lambda i: (after(i), P_QKVB // C))
    next_d = pl.BlockSpec((HALO, W), lambda i: (after(i), 0))

    def body(x_ref, prev_ref, nx_ref, w_ref, dq_ref, dk_ref, dv_ref, ndq_ref, ndk_ref, ndv_ref, _, dx_ref, dw_ref,
             xx_ref, dd_ref):
        i = pl.program_id(0)
        _fill_prev_halo(xx_ref, x_ref, prev_ref, i, tm)
        xx_ref[pl.ds(HALO + tm, HALO), :] = nx_ref[...]
        dyv = _pre_dy(_conv_rows(xx_ref, w_ref, 0, tm), dq_ref[...], dk_ref[...], dv_ref[...])
        dd_ref[pl.ds(0, tm), :] = dyv

        @pl.when(i == nt_ - 1)
        def _():
            dd_ref[pl.ds(tm, HALO), :] = jnp.zeros((HALO, C), F32)

        @pl.when(i != nt_ - 1)
        def _():
            dd_ref[pl.ds(tm, HALO), :] = _pre_dy(_conv_rows(xx_ref, w_ref, tm, HALO), ndq_ref[...], ndk_ref[...],
                                                 ndv_ref[...])

        dx = w_ref[0:1, :] * dd_ref[pl.ds(CONV_K - 1, tm), :]
        for j in range(1, CONV_K):
            dx = dx + w_ref[j:j + 1, :] * dd_ref[pl.ds(CONV_K - 1 - j, tm), :]
        dx_ref[...] = dx.astype(BF16)
        dw = jnp.concatenate(
            [jnp.sum(dyv * xx_ref[pl.ds(HALO - CONV_K + 1 + j, tm), :], axis=0, keepdims=True) for j in range(CONV_K)],
            axis=0)
        _acc(dw_ref, dw, i)

    return pl.pallas_call(
        body, name=name, grid=(nt_,),
        in_specs=[cur, prev, next_x, pl.BlockSpec((CONV_K, C), lambda i: (0, 0))] + [_rows(tm, W)] * 3 + [next_d] * 3
        + [HBM],
        out_specs=[_rows(tm, C, P_QKVB // C), pl.BlockSpec((CONV_K, C), lambda i: (0, 0))],
        out_shape=[jax.ShapeDtypeStruct(dproj.shape, BF16), jax.ShapeDtypeStruct((CONV_K, C), F32)],
        input_output_aliases={10: 0},
        scratch_shapes=[pltpu.VMEM((HALO + tm + HALO, C), F32), pltpu.VMEM((tm + HALO, C), F32)],
        compiler_params=_arb(1),
    )(proj, proj, proj, wconv, dqn, dkn, dv, dqn, dkn, dv, dproj)


def _chunk_masks():
    row = lax.broadcasted_iota(jnp.int32, (CHUNK, CHUNK), 0)
    col = lax.broadcasted_iota(jnp.int32, (CHUNK, CHUNK), 1)
    return row >= col, row > col


def _chunk_local(q, k, vv, bc, gc, gr, tri):
    dm = jnp.where(tri, jnp.exp(jnp.where(tri, gc - gr, 0.0)), 0.0)
    kk = _bnt(k, k)
    glast = gr[..., CHUNK - 1:CHUNK]
    ep = jnp.exp(gc)
    em = jnp.exp(glast - gc)
    el = jnp.exp(glast)
    return dm, kk, ep, em, el, vv * bc, k * (bc * ep)


def _unit_lower_inverse(low):
    row = lax.broadcasted_iota(jnp.int32, (CHUNK, CHUNK), 0)
    col = lax.broadcasted_iota(jnp.int32, (CHUNK, CHUNK), 1)
    p = -low
    t = jnp.where(row == col, 1.0, 0.0).astype(F32) + p
    steps = CHUNK.bit_length() - 2
    for _ in range(steps):
        p = _nn(p, p, SOLVE_PREC)
        t = t + _nn(t, p, SOLVE_PREC)
    return t


GROUP = 8


LATE = 7


def _carry(phases, first, middle, last):
    if len(phases) == 3:
        pl.when(first)(phases[0])
        pl.when(middle)(phases[1])
        return lambda: pl.when(last)(phases[2])
    pl.when(first)(phases[0])
    return lambda: pl.when(last)(phases[1])


def _pairs(nchunks):
    return [(c, h) for c in range(nchunks) for h in range(B_HEADS)]


def _tok(c):
    return slice(c * CHUNK, (c + 1) * CHUNK)


def _head(h):
    return slice(h * B_DH, (h + 1) * B_DH)


def _stack_tokens(ref, nchunks):
    return jnp.stack([ref[_tok(c), _head(h)] for c, h in _pairs(nchunks)])


def _stack_cols(ref, nchunks):
    per_chunk = [ref[c] for c in range(nchunks)] if len(ref.shape) == 3 else [ref[...]]
    return jnp.stack([per_chunk[c][:, h:h + 1] for c, h in _pairs(nchunks)])


def _stack_rows(ref, nchunks):
    if len(ref.shape) == 3:
        return jnp.stack([ref[c, h:h + 1, :] for c, h in _pairs(nchunks)])
    return jnp.stack([ref[h:h + 1, :] for _, h in _pairs(1)])


def _gdn_group_specs(ng_steps, W):
    tok = pl.BlockSpec((GROUP * CHUNK, W), lambda i: (i, 0))
    colv = pl.BlockSpec((GROUP, CHUNK, B_HEADS), lambda i: (i, 0, 0))
    rowv = pl.BlockSpec((GROUP, B_HEADS, CHUNK), lambda i: (i, 0, 0))
    mat = pl.BlockSpec((GROUP, B_HEADS, CHUNK, CHUNK), lambda i: (i, 0, 0, 0))
    return tok, colv, rowv, mat


def _gdn_local_fwd(qn, kn, v, bcol, gcol, grow, name, gather=None):
    S, Wd = qn.shape
    nc = S // CHUNK
    steps = nc // GROUP
    tok, colv, rowv, mat = _gdn_group_specs(steps, Wd)
    shards, layer = gather if gather is not None else ((), None)
    ng = len(shards)

    def body(q_ref, k_ref, v_ref, bc_ref, gc_ref, gr_ref, *rest):
        srcs, (t_ref, a_ref, u_ref, w_ref), gouts, sems = rest[:ng], rest[ng:ng + 4], rest[ng + 4:2 * ng + 4], rest[2 * ng + 4:]
        i = pl.program_id(0)
        done = _carry(_gather_phases(layer, srcs, gouts, *sems), i == 0, i == LATE * steps // 8, i == steps - 1) if ng else None
        tri, strict = _chunk_masks()
        q, k, vv = (_stack_tokens(r, GROUP) for r in (q_ref, k_ref, v_ref))
        bc, gc, gr = _stack_cols(bc_ref, GROUP), _stack_cols(gc_ref, GROUP), _stack_rows(gr_ref, GROUP)
        dm, kk, ep, em, el, vb, kb = _chunk_local(q, k, vv, bc, gc, gr, tri)
        t = _unit_lower_inverse(jnp.where(strict, bc * kk * dm, 0.0))
        a = _bnt(q, k) * dm
        u = _nn(t, vb, SOLVE_PREC)
        w = _nn(t, kb, SOLVE_PREC)
        for n, (c, h) in enumerate(_pairs(GROUP)):
            t_ref[c, h] = t[n]
            a_ref[c, h] = a[n]
            u_ref[_tok(c), _head(h)] = u[n]
            w_ref[_tok(c), _head(h)] = w[n]
        if ng:
            done()

    res = pl.pallas_call(
        body, name=name, grid=(steps,), in_specs=[tok, tok, tok, colv, colv, rowv] + [HBM] * ng,
        out_specs=[mat, mat, tok, tok] + [HBM] * ng,
        out_shape=[jax.ShapeDtypeStruct((nc, B_HEADS, CHUNK, CHUNK), F32)] * 2 + [jax.ShapeDtypeStruct((S, Wd), F32)] * 2
        + _gather_out_shapes(shards),
        scratch_shapes=_gather_scratch(ng) if ng else [], compiler_params=_arb(1),
    )(qn, kn, v, bcol, gcol, grow, *shards)
    return res[0], res[1], res[2], res[3], list(res[4:])


def _scan_decays(gc, gr):
    glast = gr[..., CHUNK - 1:CHUNK]
    return jnp.exp(gc), jnp.exp(glast - gc), jnp.exp(glast)


SCAN = 8


def _gdn_scan_specs(steps, rev):
    idx = (lambda i: steps - 1 - i) if rev else (lambda i: i)
    W = B_HEADS * B_DH
    tok = pl.BlockSpec((SCAN * CHUNK, W), lambda i: (idx(i), 0))
    colv = pl.BlockSpec((SCAN, CHUNK, B_HEADS), lambda i: (idx(i), 0, 0))
    rowv = pl.BlockSpec((SCAN, B_HEADS, CHUNK), lambda i: (idx(i), 0, 0))
    mat = pl.BlockSpec((SCAN, B_HEADS, CHUNK, CHUNK), lambda i: (idx(i), 0, 0, 0))
    smat = pl.BlockSpec((SCAN, B_HEADS, B_DH, B_DH), lambda i: (idx(i), 0, 0, 0))
    return tok, colv, rowv, mat, smat


def _chunk_rows(ref, c):
    return ref.at[pl.ds(c * CHUNK, CHUNK)]


def _gdn_scan_fwd(qn, kn, u, w, a, gcol, grow, name, gather=None):
    S, Wd = qn.shape
    nc = S // CHUNK
    steps = nc // SCAN
    tok, colv, rowv, mat, smat = _gdn_scan_specs(steps, False)
    shards, layer = gather if gather is not None else ((), None)
    ng = len(shards)

    def body(q_ref, k_ref, u_ref, w_ref, a_ref, gc_ref, gr_ref, *rest):
        srcs, (o_ref, sh_ref), gouts = rest[:ng], rest[ng:ng + 2], rest[ng + 2:2 * ng + 2]
        st_ref, sems = rest[2 * ng + 2], rest[2 * ng + 3:]
        i = pl.program_id(0)
        done = (_carry(_gather_phases(layer, srcs, gouts, *sems), i == 0, i == LATE * steps // 8, i == steps - 1)
                if ng else None)

        @pl.when(i == 0)
        def _():
            st_ref[...] = jnp.zeros_like(st_ref)

        for c in range(SCAN):
            ep, em, el = _scan_decays(_stack_cols(gc_ref.at[c], 1), _stack_rows(gr_ref.at[c], 1))
            q, k, u, w = (_stack_tokens(_chunk_rows(r, c), 1) for r in (q_ref, k_ref, u_ref, w_ref))
            s0 = st_ref[...]
            ut = u - _bnn(w, s0)
            o = _bnn(q * ep, s0) + _bnn(a_ref[c], ut)
            st_ref[...] = el * s0 + _btn(k * em, ut)
            sh_ref[c] = s0
            for h in range(B_HEADS):
                o_ref[_tok(c), _head(h)] = o[h]
        if ng:
            done()

    res = pl.pallas_call(
        body, name=name, grid=(steps,), in_specs=[tok, tok, tok, tok, mat, colv, rowv] + [HBM] * ng,
        out_specs=[tok, smat] + [HBM] * ng,
        out_shape=[jax.ShapeDtypeStruct((S, Wd), F32), jax.ShapeDtypeStruct((nc, B_HEADS, B_DH, B_DH), F32)]
        + _gather_out_shapes(shards),
        scratch_shapes=[pltpu.VMEM((B_HEADS, B_DH, B_DH), F32)] + (_gather_scratch(ng) if ng else []),
        compiler_params=_arb(1),
    )(qn, kn, u, w, a, gcol, grow, *shards)
    return res[0], res[1], list(res[2:])


def _gdn_scan_bwd(qn, kn, u, w, a, gcol, grow, ssave, do, name):
    S, Wd = qn.shape
    nc = S // CHUNK
    steps = nc // SCAN
    tok, colv, rowv, mat, smat = _gdn_scan_specs(steps, True)

    def body(q_ref, k_ref, u_ref, w_ref, a_ref, gc_ref, gr_ref, sh_ref, do_ref,
             du_ref, dw_ref, dqd_ref, dkd_ref, da_ref, dgl_ref, ds_ref):
        i = pl.program_id(0)

        @pl.when(i == 0)
        def _():
            ds_ref[...] = jnp.zeros_like(ds_ref)

        tri, _ = _chunk_masks()
        sub4 = lax.broadcasted_iota(jnp.int32, (B_HEADS, CHUNK), 0)
        lane_last = lax.broadcasted_iota(jnp.int32, (1, CHUNK), 1) == CHUNK - 1
        for c in reversed(range(SCAN)):
            ep, em, el = _scan_decays(_stack_cols(gc_ref.at[c], 1), _stack_rows(gr_ref.at[c], 1))
            q, k, u, w, dout = (_stack_tokens(_chunk_rows(r, c), 1) for r in (q_ref, k_ref, u_ref, w_ref, do_ref))
            s0 = sh_ref[c]
            ds = ds_ref[...]
            ut = u - _bnn(w, s0)
            dut = _btn(a_ref[c], dout) + _bnn(k * em, ds)
            ds_ref[...] = el * ds + _btn(q * ep, dout) - _btn(w, dut)
            dw = -_bnt(dut, s0)
            dqd = _bnt(dout, s0)
            dkd = _bnt(ut, ds)
            da_ref[c] = jnp.where(tri, _bnt(dout, ut), 0.0)
            d_el = jnp.sum(jnp.sum(s0 * ds, axis=-1, keepdims=True), axis=-2, keepdims=True)
            last = d_el * el
            dgl_acc = jnp.zeros((B_HEADS, CHUNK), F32)
            for h in range(B_HEADS):
                du_ref[_tok(c), _head(h)] = dut[h]
                dw_ref[_tok(c), _head(h)] = dw[h]
                dqd_ref[_tok(c), _head(h)] = dqd[h]
                dkd_ref[_tok(c), _head(h)] = dkd[h]
                dgl_acc = jnp.where(sub4 == h, jnp.where(lane_last, last[h], 0.0), dgl_acc)
            dgl_ref[c] = dgl_acc

    return pl.pallas_call(
        body, name=name, grid=(steps,), in_specs=[tok, tok, tok, tok, mat, colv, rowv, smat, tok],
        out_specs=[tok, tok, tok, tok, mat, rowv],
        out_shape=[jax.ShapeDtypeStruct((S, Wd), F32)] * 4 + [jax.ShapeDtypeStruct((nc, B_HEADS, CHUNK, CHUNK), F32),
                                                             jax.ShapeDtypeStruct((nc, B_HEADS, CHUNK), F32)],
        scratch_shapes=[pltpu.VMEM((B_HEADS, B_DH, B_DH), F32)], compiler_params=_arb(1),
    )(qn, kn, u, w, a, gcol, grow, ssave, do)


def _gdn_local_bwd(qn, kn, v, bcol, gcol, grow, tsave, du, dw, dqd, dkd, da, dgl, name, exchange=()):
    S, Wd = qn.shape
    nc = S // CHUNK
    steps = nc // GROUP
    tok, colv, rowv, mat = _gdn_group_specs(steps, Wd)
    ne = len(exchange)

    def body(q_ref, k_ref, v_ref, bc_ref, gc_ref, gr_ref, t_ref, du_ref, dw_ref, dqd_ref, dkd_ref, da_ref, dgl_ref, *rest):
        srcs, (dq_ref, dk_ref, dv_ref, dbc_ref, dgc_ref, dgr_ref) = rest[:ne], rest[ne:ne + 6]
        eouts, sems = rest[ne + 6:2 * ne + 6], rest[2 * ne + 6:]
        i = pl.program_id(0)
        done = _carry(_chips_phases(srcs, eouts, *sems), i == 0, None, i == steps - 1) if ne else None
        tri, strict = _chunk_masks()
        lane4 = lax.broadcasted_iota(jnp.int32, (CHUNK, B_HEADS), 1)
        sub4 = lax.broadcasted_iota(jnp.int32, (B_HEADS, CHUNK), 0)
        lane_last = lax.broadcasted_iota(jnp.int32, (1, CHUNK), 1) == CHUNK - 1
        q, k, vv, dut, dwv, dqd, dkd = (_stack_tokens(r, GROUP)
                                        for r in (q_ref, k_ref, v_ref, du_ref, dw_ref, dqd_ref, dkd_ref))
        bc, gc, gr = _stack_cols(bc_ref, GROUP), _stack_cols(gc_ref, GROUP), _stack_rows(gr_ref, GROUP)
        dm, kk, ep, em, el, vb, kb = _chunk_local(q, k, vv, bc, gc, gr, tri)
        t = jnp.stack([t_ref[c, h] for c, h in _pairs(GROUP)])
        dav = jnp.stack([da_ref[c, h] for c, h in _pairs(GROUP)])
        qk = _bnt(q, k)
        dt = _nt(dut, vb, SOLVE_PREC) + _nt(dwv, kb, SOLVE_PREC)
        dvb = _tn(t, dut, SOLVE_PREC)
        dkb = _tn(t, dwv, SOLVE_PREC)
        dl = jnp.where(strict, -_tn(t, _nt(dt, t, SOLVE_PREC), SOLVE_PREC), 0.0)
        g1 = dl * dm
        dkb_k = jnp.sum(dkb * k, axis=-1, keepdims=True)
        dbeta = jnp.sum(g1 * kk, axis=-1, keepdims=True) + jnp.sum(dvb * vv, axis=-1, keepdims=True) + dkb_k * ep
        dkk = g1 * bc
        ddm = dl * (bc * kk) + dav * qk
        dqk = dav * dm
        dq = _bnn(dqk, k) + dqd * ep
        dk = _btn(dqk, q) + _bnn(dkk, k) + _btn(dkk, k) + dkb * (bc * ep) + dkd * em
        dv = dvb * bc
        dep = dkb_k * bc + jnp.sum(dqd * q, axis=-1, keepdims=True)
        dem = jnp.sum(dkd * k, axis=-1, keepdims=True)
        mm = ddm * dm
        dgam_c = jnp.sum(mm, axis=-1, keepdims=True) + dep * ep - dem * em
        dglast = jnp.sum(dem * em, axis=-2, keepdims=True)
        dgam_r = -jnp.sum(mm, axis=-2, keepdims=True) + jnp.where(lane_last, dglast, 0.0)
        for c in range(GROUP):
            dbc_acc = jnp.zeros((CHUNK, B_HEADS), F32)
            dgc_acc = jnp.zeros((CHUNK, B_HEADS), F32)
            dgr_acc = jnp.zeros((B_HEADS, CHUNK), F32)
            for h in range(B_HEADS):
                n = c * B_HEADS + h
                dq_ref[_tok(c), _head(h)] = dq[n]
                dk_ref[_tok(c), _head(h)] = dk[n]
                dv_ref[_tok(c), _head(h)] = dv[n]
                dbc_acc = jnp.where(lane4 == h, dbeta[n], dbc_acc)
                dgc_acc = jnp.where(lane4 == h, dgam_c[n], dgc_acc)
                dgr_acc = jnp.where(sub4 == h, dgam_r[n], dgr_acc)
            dbc_ref[c] = dbc_acc
            dgc_ref[c] = dgc_acc
            dgr_ref[c] = dgr_acc + dgl_ref[c]
        if ne:
            done()

    res = pl.pallas_call(
        body, name=name, grid=(steps,),
        in_specs=[tok, tok, tok, colv, colv, rowv, mat, tok, tok, tok, tok, mat, rowv] + [HBM] * ne,
        out_specs=[tok, tok, tok, colv, colv, rowv] + [HBM] * ne,
        out_shape=[jax.ShapeDtypeStruct((S, Wd), F32)] * 3
        + [jax.ShapeDtypeStruct((nc, CHUNK, B_HEADS), F32)] * 2 + [jax.ShapeDtypeStruct((nc, B_HEADS, CHUNK), F32)]
        + _chips_out_shapes(exchange),
        scratch_shapes=_chips_scratch(ne) if ne else [], compiler_params=_arb(1),
    )(qn, kn, v, bcol, gcol, grow, tsave, du, dw, dqd, dkd, da, dgl, *exchange)
    return tuple(res[:6]) + (list(res[6:]),)


def _gdn_post_fwd(o, proj, ng, name):
    S, W = o.shape
    tm = _pick(S, 512, 8)

    def body(o_ref, z_ref, g_ref, y_ref):
        gv = g_ref[...]
        for h in range(B_HEADS):
            hs = slice(h * B_DH, (h + 1) * B_DH)
            oh = o_ref[:, hs]
            z = z_ref[:, hs]
            r = lax.rsqrt(jnp.mean(oh * oh, axis=-1, keepdims=True) + EPS)
            y_ref[:, hs] = (oh * r * gv * (z * _sigmoid(z))).astype(BF16)

    return pl.pallas_call(
        body, name=name, grid=(S // tm,), in_specs=[_rows(tm, W), _rows(tm, W, P_Z // W), _vec(B_DH)],
        out_specs=_rows(tm, W), out_shape=jax.ShapeDtypeStruct((S, W), BF16), compiler_params=_par(1),
    )(o, proj, ng)


def _gdn_post_bwd(dy, o, proj, ng, dproj, name):
    S, W = o.shape
    tm = _pick(S, 512, 16)

    def body(dy_ref, o_ref, z_ref, g_ref, _, do_ref, dz_ref, dg_ref):
        i = pl.program_id(0)
        gv = g_ref[...]
        dg = jnp.zeros((1, B_DH), F32)
        for h in range(B_HEADS):
            hs = slice(h * B_DH, (h + 1) * B_DH)
            oh = o_ref[:, hs]
            z = z_ref[:, hs]
            d = dy_ref[:, hs]
            r = lax.rsqrt(jnp.mean(oh * oh, axis=-1, keepdims=True) + EPS)
            n = oh * r
            sg = _sigmoid(z)
            sz = z * sg
            dn = d * gv * sz
            dg = dg + jnp.sum(d * n * sz, axis=0, keepdims=True)
            dz_ref[:, hs] = (d * n * gv * (sg * (1.0 + z * (1.0 - sg)))).astype(BF16)
            do_ref[:, hs] = r * (dn - n * jnp.mean(dn * n, axis=-1, keepdims=True))
        _acc(dg_ref, dg, i)

    return pl.pallas_call(
        body, name=name, grid=(S // tm,),
        in_specs=[_rows(tm, W), _rows(tm, W), _rows(tm, W, P_Z // W), _vec(B_DH), HBM],
        out_specs=[_rows(tm, W), _rows(tm, W, P_Z // W), _vec(B_DH)],
        out_shape=[jax.ShapeDtypeStruct((S, W), F32), jax.ShapeDtypeStruct(dproj.shape, BF16),
                   jax.ShapeDtypeStruct((1, B_DH), F32)],
        input_output_aliases={4: 1}, compiler_params=_arb(1),
    )(dy, o, proj, ng, dproj)


def _ada_mod(c_all, w_ada, b_shard, name):
    L, D, Ns = w_ada.shape
    B = c_all.shape[0]

    def body(c_ref, w_ref, b_ref, o_ref):
        cv = c_ref[...]
        cond = (cv * _sigmoid(cv)).astype(BF16)
        o_ref[...] = _nn(cond, w_ref[...].astype(BF16)) + b_ref[...]

    return pl.pallas_call(
        body, name=name, grid=(L,),
        in_specs=[pl.BlockSpec((B, D), lambda l: (0, 0)), pl.BlockSpec((None, D, Ns), lambda l: (l, 0, 0)),
                  pl.BlockSpec((None, 1, Ns), lambda l: (l, 0, 0))],
        out_specs=pl.BlockSpec((None, B, Ns), lambda l: (l, 0, 0)),
        out_shape=jax.ShapeDtypeStruct((L, B, Ns), F32), compiler_params=_par(1),
    )(c_all, w_ada, b_shard)


def _ada_wgrad(c_all, dmod, name):
    L, B, Ns = dmod.shape
    D = c_all.shape[1]

    def body(c_ref, d_ref, o_ref):
        cv = c_ref[...]
        cond = (cv * _sigmoid(cv)).astype(BF16)
        o_ref[...] = _tn(cond, d_ref[...].astype(BF16))

    return pl.pallas_call(
        body, name=name, grid=(L,),
        in_specs=[pl.BlockSpec((B, D), lambda l: (0, 0)), pl.BlockSpec((None, B, Ns), lambda l: (l, 0, 0))],
        out_specs=pl.BlockSpec((None, D, Ns), lambda l: (l, 0, 0)),
        out_shape=jax.ShapeDtypeStruct((L, D, Ns), F32), compiler_params=_par(1),
    )(c_all, dmod)


W_IN_PIECES = ((0, 0, 1410), (1, 0, 1410), (2, 0, 252), (2, 772, 638), (3, 0, 1410), (2, 252, 512), (2, 764, 8))


def _reorder_w_in(w4, name):
    L, _, D, Cs = w4.shape
    tm = _pick(D, 256, 16)
    used = sum(p[2] for p in W_IN_PIECES)

    def body(w_ref, o_ref):
        shard = [w_ref[s] for s in range(4)]
        parts = [shard[s][:, lo:lo + n] for s, lo, n in W_IN_PIECES]
        o_ref[...] = jnp.concatenate(parts + [jnp.zeros((tm, P_END - used), w4.dtype)], axis=1)

    return pl.pallas_call(
        body, name=name, grid=(L, D // tm), in_specs=[pl.BlockSpec((None, 4, tm, Cs), lambda l, i: (l, 0, i, 0))],
        out_specs=pl.BlockSpec((None, tm, P_END), lambda l, i: (l, i, 0)),
        out_shape=jax.ShapeDtypeStruct((L, D, P_END), w4.dtype), compiler_params=_par(2),
    )(w4)


def _restore_w_in(g, name):
    D = g.shape[0]
    tm = _pick(D, 256, 16)

    def body(g_ref, o_ref, ob_ref):
        gv = g_ref[...]
        off = 0
        pieces = {}
        for s, lo, n in W_IN_PIECES:
            pieces.setdefault(s, []).append((lo, gv[:, off:off + n]))
            off += n
        for s, lst in pieces.items():
            lst.sort(key=lambda t: t[0])
            shard = lst[0][1] if len(lst) == 1 else jnp.concatenate([t[1] for t in lst], axis=1)
            o_ref[s] = shard
            ob_ref[s] = shard.astype(BF16)

    spec = pl.BlockSpec((4, tm, W_IN_SHARD), lambda i: (0, i, 0))
    return pl.pallas_call(
        body, name=name, grid=(D // tm,), in_specs=[pl.BlockSpec((tm, P_END), lambda i: (i, 0))],
        out_specs=[spec, spec],
        out_shape=[jax.ShapeDtypeStruct((4, D, W_IN_SHARD), g.dtype), jax.ShapeDtypeStruct((4, D, W_IN_SHARD), BF16)],
        compiler_params=_par(1),
    )(g)


def _adam_update(w, g, m, v):
    mn = ADAM_B1 * m + (1.0 - ADAM_B1) * g
    vn = ADAM_B2 * v + (1.0 - ADAM_B2) * (g * g)
    m_hat = mn / (1.0 - ADAM_B1 ** ADAM_STEP)
    v_hat = vn / (1.0 - ADAM_B2 ** ADAM_STEP)
    return -ADAM_LR * (m_hat / (jnp.sqrt(v_hat) + ADAM_EPS) + ADAM_WD * w), mn, vn


def _adamw(w, g, m, v, name):
    shape = w.shape
    C = shape[-1]
    R = w.size // C
    tm = _pick(R, 512, 8)
    spec = pl.BlockSpec((tm, C), lambda i: (i, 0))

    def body(w_ref, g_ref, m_ref, v_ref, d_ref, mo_ref, vo_ref):
        d_ref[...], mo_ref[...], vo_ref[...] = _adam_update(w_ref[...], g_ref[...], m_ref[...], v_ref[...])

    outs = pl.pallas_call(
        body, name=name, grid=(R // tm,), in_specs=[spec] * 4, out_specs=[spec] * 3,
        out_shape=[jax.ShapeDtypeStruct((R, C), F32)] * 3, compiler_params=_par(1),
    )(*(t.reshape(R, C) for t in (w, g, m, v)))
    return tuple(o.reshape(shape) for o in outs)


def _adamw_lead(w, g, m, v, name, tl):
    A, B, C = w.shape
    spec = pl.BlockSpec((tl, B, C), lambda i: (i, 0, 0))

    def body(w_ref, g_ref, m_ref, v_ref, d_ref, mo_ref, vo_ref):
        d_ref[...], mo_ref[...], vo_ref[...] = _adam_update(w_ref[...], g_ref[...], m_ref[...], v_ref[...])

    return pl.pallas_call(
        body, name=name, grid=(A // tl,), in_specs=[spec] * 4, out_specs=[spec] * 3,
        out_shape=[jax.ShapeDtypeStruct((A, B, C), F32)] * 3, compiler_params=_par(1),
    )(w, g, m, v)


def _adamw_layers(w, gs, m, v, name):
    L, R, C = w.shape
    tm = _pick(R, 256, 8)
    spec = pl.BlockSpec((None, tm, C), lambda l, i: (l, i, 0))
    g_specs = [pl.BlockSpec((tm, C), functools.partial(lambda ll, l, i: (jnp.where(l == ll, i, 0), 0), ll))
               for ll in range(L)]

    def body(w_ref, m_ref, v_ref, *rest):
        g_refs, (go_ref, d_ref, mo_ref, vo_ref) = rest[:L], rest[L:]
        l = pl.program_id(0)
        for ll in range(L):
            @pl.when(l == ll)
            def _():
                g = g_refs[ll][...]
                go_ref[...] = g
                d_ref[...], mo_ref[...], vo_ref[...] = _adam_update(w_ref[...], g, m_ref[...], v_ref[...])

    return pl.pallas_call(
        body, name=name, grid=(L, R // tm), in_specs=[spec] * 3 + g_specs, out_specs=[spec] * 4,
        out_shape=[jax.ShapeDtypeStruct((L, R, C), F32)] * 4, compiler_params=_arb(2),
    )(w, m, v, *gs)


def _pair_sums(a, where, b, name):
    NB, _, R, C = a.shape

    def body(where_ref, a_ref, b_ref, p_ref, own_ref):
        s = a_ref[...] + b_ref[...].astype(F32)
        p_ref[...] = s.astype(BF16)

        @pl.when(pl.program_id(0) == where_ref[1])
        def _():
            own_ref[...] = s

    return pl.pallas_call(
        body, name=name,
        grid_spec=pltpu.PrefetchScalarGridSpec(
            num_scalar_prefetch=1, grid=(NB,),
            in_specs=[pl.BlockSpec((None, None, R, C), lambda k, w: (k, w[0], 0, 0)),
                      pl.BlockSpec((None, R, C), lambda k, w: (k, 0, 0))],
            out_specs=[pl.BlockSpec((None, R, C), lambda k, w: (k, 0, 0)), pl.BlockSpec((R, C), lambda k, w: (0, 0))]),
        out_shape=[jax.ShapeDtypeStruct((NB, R, C), BF16), jax.ShapeDtypeStruct((R, C), F32)],
        compiler_params=_arb(1),
    )(where, a, b)


def _sum_own_and_received(own, recv, where, name):
    R, C = own.shape
    tm = _pick(R, 256, 16)

    def body(where_ref, p_ref, r_ref, o_ref):
        o_ref[...] = ((p_ref[...] + r_ref[0].astype(F32)) + r_ref[1].astype(F32)) + r_ref[2].astype(F32)

    return pl.pallas_call(
        body, name=name,
        grid_spec=pltpu.PrefetchScalarGridSpec(
            num_scalar_prefetch=1, grid=(R // tm,),
            in_specs=[pl.BlockSpec((tm, C), lambda i, w: (i, 0)), pl.BlockSpec((3, tm, C), lambda i, w: (0, i, 0))],
            out_specs=pl.BlockSpec((None, tm, C), lambda i, w: (w[0], i, 0))),
        out_shape=jax.ShapeDtypeStruct((2, R, C), F32), compiler_params=_par(1),
    )(where, own, recv)


def _position():
    return lax.axis_index("x"), lax.axis_index("y"), lax.axis_index("c")


def _other_chips(x, y):
    return [(1 - x, y), (x, 1 - y), (1 - x, 1 - y)]


HBM = pl.BlockSpec(memory_space=pl.ANY)


def _allgather8(blk, name, reduce_rows=None):
    M, N = blk.shape

    def body(x_ref, out_ref, *rest):
        if reduce_rows is None:
            send_sems, recv_sems, local_sem = rest
        else:
            sum_ref, send_sems, recv_sems, local_sem = rest
        x, y, c = _position()
        me, sibling = (x, y, c), (x, y, 1 - c)
        chips = _other_chips(x, y)

        def rows(px, py, pc):
            return out_ref.at[pl.ds((4 * px + 2 * py + pc) * M, M), :]

        def copy(k, block, to, src=None):
            return pltpu.make_async_remote_copy(
                src_ref=rows(*block) if src is None else src, dst_ref=rows(*block),
                send_sem=send_sems.at[k], recv_sem=recv_sems.at[k], device_id=to, device_id_type=MESH)

        mine = pltpu.make_async_copy(x_ref, rows(*me), local_sem)
        mine.start()
        first = [copy(0, me, sibling, src=x_ref)]
        first += [copy(1 + j, me, (*chip, c), src=x_ref) for j, chip in enumerate(chips)]
        for cp in first:
            cp.start()
        passed = [copy(4 + j, (*chip, c), sibling) for j, chip in enumerate(chips)]
        for j, chip in enumerate(chips):
            copy(1 + j, (*chip, c), me).wait_recv()
            passed[j].start()
        copy(0, sibling, me).wait_recv()
        for j, chip in enumerate(chips):
            copy(4 + j, (*chip, 1 - c), me).wait_recv()
        for cp in first + passed:
            cp.wait_send()
        mine.wait()
        if reduce_rows is not None:
            tot = out_ref[pl.ds(0, reduce_rows), :]
            for d in range(1, 8):
                tot = tot + out_ref[pl.ds(d * M, reduce_rows), :]
            sum_ref[...] = tot

    vmem = pl.BlockSpec(memory_space=pltpu.VMEM)
    out_shape = [jax.ShapeDtypeStruct((8 * M, N), blk.dtype)]
    if reduce_rows is not None:
        out_shape.append(jax.ShapeDtypeStruct((reduce_rows, N), blk.dtype))
    res = pl.pallas_call(
        body, name=name, out_shape=out_shape, in_specs=[vmem], out_specs=[vmem] * len(out_shape),
        scratch_shapes=[pltpu.SemaphoreType.DMA((7,)), pltpu.SemaphoreType.DMA((7,)), pltpu.SemaphoreType.DMA],
    )(blk)
    return res[0] if reduce_rows is None else (res[0], res[1])


def _gather_phases(layer, srcs, outs, send_sems, recv_sems, local_sems):
    n = len(srcs)
    x, y, c = _position()
    me, sibling = (x, y, c), (x, y, 1 - c)
    chips = _other_chips(x, y)

    def region(t, px, py, pc):
        return outs[t].at[2 * px + py, pc]

    def copy(t, k, block, to, own=False):
        return pltpu.make_async_remote_copy(
            src_ref=srcs[t].at[layer, c] if own else region(t, *block), dst_ref=region(t, *block),
            send_sem=send_sems.at[7 * t + k], recv_sem=recv_sems.at[7 * t + k], device_id=to, device_id_type=MESH)

    def local(t):
        return pltpu.make_async_copy(srcs[t].at[layer, c], region(t, *me), local_sems.at[t])

    def first(t):
        return [copy(t, 0, me, sibling, own=True)] + [copy(t, 1 + j, me, (*chip, c), own=True)
                                                       for j, chip in enumerate(chips)]

    def start():
        for t in range(n):
            local(t).start()
        for t in range(n):
            for cp in first(t):
                cp.start()

    def forward():
        for j, chip in enumerate(chips):
            for t in range(n):
                copy(t, 1 + j, (*chip, c), me).wait_recv()
                copy(t, 4 + j, (*chip, c), sibling).start()

    def finish():
        for t in range(n):
            copy(t, 0, sibling, me).wait_recv()
        for j, chip in enumerate(chips):
            for t in range(n):
                copy(t, 4 + j, (*chip, 1 - c), me).wait_recv()
        for t in range(n):
            for cp in first(t) + [copy(t, 4 + j, (*chip, c), sibling) for j, chip in enumerate(chips)]:
                cp.wait_send()
            local(t).wait()

    return start, forward, finish


def _gather_scratch(n):
    return [pltpu.SemaphoreType.DMA((7 * n,)), pltpu.SemaphoreType.DMA((7 * n,)), pltpu.SemaphoreType.DMA((n,))]


def _gather_out_shapes(shards):
    return [jax.ShapeDtypeStruct((4,) + s.shape[1:], s.dtype) for s in shards]


def _gather_weights(shards, layer, name):
    n = len(shards)

    def body(*refs):
        start, forward, finish = _gather_phases(layer, refs[:n], refs[n:2 * n], *refs[2 * n:])
        start()
        forward()
        finish()

    return pl.pallas_call(
        body, name=name, out_shape=_gather_out_shapes(shards), in_specs=[HBM] * n, out_specs=[HBM] * n,
        scratch_shapes=_gather_scratch(n),
    )(*shards)


def _sibling_phases(srcs, outs, send_sems, recv_sems):
    x, y, c = _position()
    copies = [pltpu.make_async_remote_copy(
        src_ref=srcs[t].at[k, 1 - c], dst_ref=outs[t].at[k], send_sem=send_sems.at[4 * t + k],
        recv_sem=recv_sems.at[4 * t + k], device_id=(x, y, 1 - c), device_id_type=MESH)
        for t in range(len(srcs)) for k in range(4)]

    def start():
        for cp in copies:
            cp.start()

    def finish():
        for cp in copies:
            cp.wait()

    return start, finish


def _sibling_scratch(n):
    return [pltpu.SemaphoreType.DMA((4 * n,)), pltpu.SemaphoreType.DMA((4 * n,))]


def _sibling_out_shapes(gs):
    return [jax.ShapeDtypeStruct((4,) + g.shape[2:], g.dtype) for g in gs]


def _rs_chips(ps, name):
    n = len(ps)

    def body(*refs):
        start, finish = _chips_phases(refs[:n], refs[n:2 * n], *refs[2 * n:])
        start()
        finish()

    return pl.pallas_call(
        body, name=name, out_shape=_chips_out_shapes(ps), in_specs=[HBM] * n, out_specs=[HBM] * n,
        scratch_shapes=_chips_scratch(n),
    )(*ps)


def _chips_phases(srcs, outs, send_sems, recv_sems):
    x, y, c = _position()
    copies = [pltpu.make_async_remote_copy(
        src_ref=srcs[t].at[2 * px + py], dst_ref=outs[t].at[j], send_sem=send_sems.at[3 * t + j],
        recv_sem=recv_sems.at[3 * t + j], device_id=(px, py, c), device_id_type=MESH)
        for t in range(len(srcs)) for j, (px, py) in enumerate(_other_chips(x, y))]

    def start():
        for cp in copies:
            cp.start()

    def finish():
        for cp in copies:
            cp.wait()

    return start, finish


def _chips_scratch(n):
    return [pltpu.SemaphoreType.DMA((3 * n,)), pltpu.SemaphoreType.DMA((3 * n,))]


def _chips_out_shapes(ps):
    return [jax.ShapeDtypeStruct((3,) + p.shape[1:], p.dtype) for p in ps]


def _rs_pair(hs, name):
    n = len(hs)

    def body(*refs):
        bufs = refs[n:2 * n]
        send_sems, recv_sems = refs[2 * n:]
        x, y, c = _position()

        def copy(t, half):
            return pltpu.make_async_remote_copy(
                src_ref=bufs[t].at[half], dst_ref=bufs[t].at[half], send_sem=send_sems.at[t], recv_sem=recv_sems.at[t],
                device_id=(x, y, 1 - c), device_id_type=MESH)

        for t in range(n):
            copy(t, c).start()
        for t in range(n):
            copy(t, 1 - c).wait_recv()
        for t in range(n):
            copy(t, c).wait_send()

    out_shape = [jax.ShapeDtypeStruct(h.shape, h.dtype) for h in hs]
    return pl.pallas_call(
        body, name=name, out_shape=out_shape, in_specs=[HBM] * n, out_specs=[HBM] * n,
        input_output_aliases={t: t for t in range(n)},
        scratch_shapes=[pltpu.SemaphoreType.DMA((n,)), pltpu.SemaphoreType.DMA((n,))],
    )(*hs)


BIG = ("w_in", "w_branch_a", "w_branch_b", "w_out", "w_ffn_in", "w_ffn_out")
CARRY_ATTN = ["w_in"]
CARRY_LOCAL = ["w_ffn_out"]
CARRY_SCAN = ["w_branch_a", "w_branch_b", "w_out"]
CARRY_GU = ["w_ffn_in"]
CARRY_DATTN = ["w_in", "w_ffn_in"]
CARRY_DLOCAL = ["w_branch_a", "w_branch_b", "w_out", "w_ffn_out"]


def _band_bias(rel_table, name, gather=None):
    L, H, n = rel_table.shape
    tab = jnp.pad(rel_table, ((0, 0), (0, 0), (0, NREL_PAD - n))).reshape(L * H, 1, NREL_PAD)
    band = (A_PAST + 1) * CHUNK

    shards, glayer = gather if gather is not None else ((), None)
    ng = len(shards)

    def body(t_ref, *rest):
        srcs, o_ref, gouts, sems = rest[:ng], rest[ng], rest[ng + 1:2 * ng + 1], rest[2 * ng + 1:]
        i = pl.program_id(0)
        done = (_carry(_gather_phases(glayer, srcs, gouts, *sems), i == 0, i == LATE * (L * H) // 8, i == L * H - 1)
                if ng else None)
        r = lax.broadcasted_iota(jnp.int32, (NREL_PAD, SKEW_W), 0)
        xi = lax.broadcasted_iota(jnp.int32, (NREL_PAD, SKEW_W), 1)
        diag = jnp.where(xi < KSPAN, xi, xi - SKEW_W)
        rel = jnp.clip(A_PAST * CHUNK - diag, -A_MAX_REL, A_MAX_REL) + A_MAX_REL
        e = _nn(t_ref[...], jnp.where(rel == r, 1.0, 0.0).astype(F32), HI)
        x = jnp.broadcast_to(e, (QBLK, SKEW_W))
        row = lax.broadcasted_iota(jnp.int32, (QBLK, SKEW_W), 0)
        for b in range(QBLK.bit_length() - 1):
            x = jnp.where(((row >> b) & 1) == 1, pltpu.roll(x, 1 << b, 1), x)
        x = x[:, :KSPAN]
        first = (lax.broadcasted_iota(jnp.int32, (QBLK, KSPAN), 0) // CHUNK) * CHUNK
        col = lax.broadcasted_iota(jnp.int32, (QBLK, KSPAN), 1)
        o_ref[...] = jnp.where((col >= first) & (col < first + band), x, NEG)
        if ng:
            done()

    res = pl.pallas_call(
        body, name=name, grid=(L * H,), in_specs=[pl.BlockSpec((None, 1, NREL_PAD), lambda i: (i, 0, 0))] + [HBM] * ng,
        out_specs=[pl.BlockSpec((None, QBLK, KSPAN), lambda i: (i, 0, 0))] + [HBM] * ng,
        out_shape=[jax.ShapeDtypeStruct((L * H, QBLK, KSPAN), F32)] + _gather_out_shapes(shards),
        scratch_shapes=_gather_scratch(ng) if ng else [], compiler_params=_arb(1),
    )(tab, *shards)
    return res[0].reshape(L, H, QBLK, KSPAN), list(res[1:])


def _col_row_forms(t, S):
    nc = S // CHUNK
    return t.T.reshape(nc, CHUNK, B_HEADS), t.reshape(B_HEADS, nc, CHUNK).transpose(1, 0, 2)


def _weight_view(name, gathered, tag):
    if name in ("w_out", "w_ffn_out"):
        return gathered.reshape(8 * gathered.shape[2], gathered.shape[3])
    stacked = gathered.reshape(4, 2 * gathered.shape[2], gathered.shape[3])
    return _reorder_w_in(stacked[None], f"w_in_cols_{tag}")[0] if name == "w_in" else stacked


def _layer_fwd(l, x, mod, W, P, big, gather=None, late=None):
    S, D = x.shape
    n = lambda s: f"{s}_l{l}"
    sh1, sc1, gt1, sh2, sc2, gt2 = (mod[i:i + 1] for i in range(6))
    h1 = _lnmod_fwd(x, P["norm1_g"][l:l + 1], sc1, sh1, n("ln1"))
    if late is None:
        proj = _matmul(h1, W["w_in"], "nn", F32, n("proj"), tn=1152)
    else:
        proj, got = _matmul(h1, W["w_in"], "nn", F32, n("proj"), tn=1152, gather=(late[1], l))
        W = {**W, **{k: _weight_view(k, t, f"l{l}") for k, t in zip(late[0], got)}}
    part = (lambda names: ([gather[0][BIG.index(k)] for k in names], gather[1])) if gather is not None else (lambda names: None)
    ya, got_a = _attn_fwd(proj, big, n("attn"), part(CARRY_ATTN))
    alog, dtb = P["a_log"][l].reshape(B_HEADS, 1), P["dt_bias"][l].reshape(B_HEADS, 1)
    beta, gam, b_t, a_t = _gdn_gates_fwd(proj, alog, dtb, n("gates"))
    bcol, _ = _col_row_forms(beta, S)
    gcol, grow = _col_row_forms(gam, S)
    qn, kn, v = _gdn_pre_fwd(proj, P["w_conv"][l], n("gdnpre"))
    tsave, amat, u, w, got_l = _gdn_local_fwd(qn, kn, v, bcol, gcol, grow, n("gdnlocal"), part(CARRY_LOCAL))
    o, ssave, got_s = _gdn_scan_fwd(qn, kn, u, w, amat, gcol, grow, n("gdnscan"), part(CARRY_SCAN))
    yb = _gdn_post_fwd(o, proj, P["gdn_norm_g"][l:l + 1], n("gdnpost"))
    pa = _matmul(ya, W["w_branch_a"], "nn", BF16, n("pa"), tm=2048, stacked=True)
    pb = _matmul(yb, W["w_branch_b"], "nn", BF16, n("pb"), tm=2048, stacked=True)
    merged = _merge_fwd(proj, pa, pb, n("merge"))
    ao, x1 = _matmul(merged, W["w_out"], "nn", F32, n("ao"), residual=(x, gt1))
    h2 = _lnmod_fwd(x1, P["norm2_g"][l:l + 1], sc2, sh2, n("ln2"))
    gu = _matmul(h2, W["w_ffn_in"], "nn", BF16, n("gu"), stacked=True, gather=part(CARRY_GU))
    gu, got_g = gu if gather is not None else (gu, [])
    got = dict(zip(CARRY_ATTN + CARRY_LOCAL + CARRY_SCAN + CARRY_GU, got_a + got_l + got_s + got_g))
    gathered = {k: got[k] for k in BIG} if gather is not None else None
    act = _ffn_act_fwd(gu, n("act"))
    fo, x2 = _matmul(act, W["w_ffn_out"], "nn", F32, n("fo"), tk=1408, residual=(x1, gt2))
    saved = dict(x=x, h1=h1, proj=proj, ya=ya, b_t=b_t, a_t=a_t, bcol=bcol, gcol=gcol, grow=grow,
                 qn=qn, kn=kn, v=v, o=o, tsave=tsave, ssave=ssave, amat=amat, u=u, w=w, yb=yb, pa=pa, pb=pb,
                 merged=merged, ao=ao, x1=x1,
                 h2=h2, gu=gu, act=act, fo=fo)
    return x2, saved, gathered, W


def _layer_bwd(l, dx2, sv, mod, W, P, big, exchange=()):
    S, D = dx2.shape
    n = lambda s: f"{s}_l{l}"
    sh1, sc1, gt1, sh2, sc2, gt2 = (mod[i:i + 1] for i in range(6))
    g, pay = {}, {}
    view = lambda t: t.reshape((4, 2, t.shape[-2] // (2 if t.ndim == 3 else 8), t.shape[-1]))
    dz2, dgt2 = _gate_bwd(dx2, sv["fo"], gt2, n("dres2"))
    g["w_ffn_out"], pay["w_ffn_out"] = map(view, _matmul(sv["act"], dz2, "tn", F32, n("dwfo"), tm=1408, also_bf16=True))
    dact = _matmul(dz2, W["w_ffn_out"], "nt", BF16, n("dact"), tn=1408)
    dgu = _ffn_act_bwd(sv["gu"], dact, n("dgu"))
    g["w_ffn_in"], pay["w_ffn_in"] = map(view, _matmul(sv["h2"], dgu, "tn", F32, n("dwfi"), out_stacked=True,
                                                       also_bf16=True))
    dh2 = _matmul(dgu, W["w_ffn_in"], "nt", F32, n("dh2"), stacked=True)
    dx1, dsh2, dsc2, dn2 = _lnmod_bwd(dh2, sv["x1"], P["norm2_g"][l:l + 1], sc2, dx2, n("dln2"))
    dz1, dgt1 = _gate_bwd(dx1, sv["ao"], gt1, n("dres1"))
    g["w_out"], pay["w_out"] = map(view, _matmul(sv["merged"], dz1, "tn", F32, n("dwo"), also_bf16=True))
    dmerged = _matmul(dz1, W["w_out"], "nt", BF16, n("dmerged"))
    dproj, dpa, dpb = _merge_bwd(sv["proj"], sv["pa"], sv["pb"], dmerged, n("dmerge"))
    g["w_branch_a"], pay["w_branch_a"] = map(view, _matmul(sv["ya"], dpa, "tn", F32, n("dwa"), out_stacked=True,
                                                           also_bf16=True))
    g["w_branch_b"], pay["w_branch_b"] = map(view, _matmul(sv["yb"], dpb, "tn", F32, n("dwb"), out_stacked=True,
                                                           also_bf16=True))
    dya = _matmul(dpa, W["w_branch_a"], "nt", BF16, n("dya"), tm=2048, stacked=True)
    dyb = _matmul(dpb, W["w_branch_b"], "nt", F32, n("dyb"), tm=2048, stacked=True)
    ex = (lambda names: [exchange[BIG.index(k)] for k in names]) if len(exchange) else (lambda names: ())
    dq, dk, dv, dbig, rec_a = _attn_bwd(sv["proj"], big, dya, n("dattn"), ex(CARRY_DATTN))
    g["rel_table"] = _rel_table_grad(dbig, n("drel"))[:, 0, :2 * A_MAX_REL + 1]
    dproj = _write_columns(dproj, [dq, dk, dv], 3 * dq.shape[1], P_QKVA // (3 * dq.shape[1]), n("dqkva"))
    do, dproj, dng = _gdn_post_bwd(dyb, sv["o"], sv["proj"], P["gdn_norm_g"][l:l + 1], dproj, n("dgdnpost"))
    g["gdn_norm_g"] = dng[0]
    du, dw, dqd, dkd, da, dgl = _gdn_scan_bwd(sv["qn"], sv["kn"], sv["u"], sv["w"], sv["amat"], sv["gcol"], sv["grow"],
                                              sv["ssave"], do, n("dgdnscan"))
    dqn, dkn, dvv, dbc, dgc, dgr, rec_l = _gdn_local_bwd(
        sv["qn"], sv["kn"], sv["v"], sv["bcol"], sv["gcol"], sv["grow"], sv["tsave"], du, dw, dqd, dkd, da, dgl,
        n("dgdnlocal"), ex(CARRY_DLOCAL))
    rec = dict(zip(CARRY_DATTN + CARRY_DLOCAL, rec_a + rec_l))
    received = [rec[k] for k in BIG] if len(exchange) else None
    dbeta_t = dbc.reshape(S, B_HEADS).T
    dgam_a = dgc.reshape(S, B_HEADS).T
    dgam_b = dgr.transpose(1, 0, 2).reshape(B_HEADS, S)
    alog, dtb = P["a_log"][l].reshape(B_HEADS, 1), P["dt_bias"][l].reshape(B_HEADS, 1)
    db_t, da_t, dal, ddt = _gdn_gates_bwd(dbeta_t, dgam_a, dgam_b, sv["b_t"], sv["a_t"], alog, dtb, n("dgates"))
    g["a_log"], g["dt_bias"] = dal[:, 0], ddt[:, 0]
    dproj, g["w_conv"] = _gdn_pre_bwd(sv["proj"], P["w_conv"][l], dqn, dkn, dvv, dproj, n("dgdnpre"))
    dba = jnp.concatenate([db_t.T, da_t.T, jnp.zeros((S, P_END - P_BA - 2 * B_HEADS), F32)], axis=1)
    dproj = _write_columns(dproj, [dba], dba.shape[1], P_BA // dba.shape[1], n("dba"))
    g["w_in"], pay["w_in"] = map(view, _restore_w_in(_matmul(sv["h1"], dproj, "tn", F32, n("dwin"), tn=1152),
                                                     n("dwin_cols")))
    dh1, from_sibling = _matmul(dproj, W["w_in"], "nt", F32, n("dh1"), tk=1152, sibling=[pay[k] for k in BIG])
    dx, dsh1, dsc1, dn1 = _lnmod_bwd(dh1, sv["x"], P["norm1_g"][l:l + 1], sc1, dx1, n("dln1"))
    g["norm1_g"], g["norm2_g"] = dn1[0], dn2[0]
    dmod = jnp.concatenate([dsh1, dsc1, dgt1, dsh2, dsc2, dgt2], axis=1)[0]
    return dx, g, from_sibling, dmod, received


SMALL = ("norm1_g", "norm2_g", "rel_table", "w_conv", "a_log", "dt_bias", "gdn_norm_g")
SMALL_PACK_C = 1024


def _as_rows(t):
    flat = t.reshape(-1)
    rows = -(-flat.shape[0] // SMALL_PACK_C)
    return jnp.pad(flat, (0, rows * SMALL_PACK_C - flat.shape[0])).reshape(rows, SMALL_PACK_C)


def _pack_rows(parts):
    blk = jnp.concatenate([_as_rows(p) for p in parts], axis=0)
    return jnp.pad(blk, ((0, -blk.shape[0] % 8), (0, 0)))


def _unpack_rows(blk, shapes):
    out, r = [], 0
    for shp in shapes:
        size = int(np.prod(shp))
        rows = -(-size // SMALL_PACK_C)
        out.append(blk[..., r:r + rows, :].reshape(blk.shape[:-2] + (rows * SMALL_PACK_C,))[..., :size]
                   .reshape(blk.shape[:-2] + tuple(shp)))
        r += rows
    return out


def kernel(x, c, w_ada, b_ada, norm1_g, norm2_g, w_in, rel_table, w_conv, a_log, dt_bias, gdn_norm_g, w_branch_a, w_branch_b, w_out, w_ffn_in, w_ffn_out, final_g, loss_target, m_w_ada, m_b_ada, m_norm1_g, m_norm2_g, m_w_in, m_rel_table, m_w_conv, m_a_log, m_dt_bias, m_gdn_norm_g, m_w_branch_a, m_w_branch_b, m_w_out, m_w_ffn_in, m_w_ffn_out, m_final_g, v_w_ada, v_b_ada, v_norm1_g, v_norm2_g, v_w_in, v_rel_table, v_w_conv, v_a_log, v_dt_bias, v_gdn_norm_g, v_w_branch_a, v_w_branch_b, v_w_out, v_w_ffn_in, v_w_ffn_out, v_final_g):
    weights = dict(w_ada=w_ada, b_ada=b_ada, norm1_g=norm1_g, norm2_g=norm2_g, w_in=w_in, rel_table=rel_table,
                   w_conv=w_conv, a_log=a_log, dt_bias=dt_bias, gdn_norm_g=gdn_norm_g, w_branch_a=w_branch_a,
                   w_branch_b=w_branch_b, w_out=w_out, w_ffn_in=w_ffn_in, w_ffn_out=w_ffn_out, final_g=final_g)
    mom_m = dict(w_ada=m_w_ada, b_ada=m_b_ada, norm1_g=m_norm1_g, norm2_g=m_norm2_g, w_in=m_w_in,
                 rel_table=m_rel_table, w_conv=m_w_conv, a_log=m_a_log, dt_bias=m_dt_bias, gdn_norm_g=m_gdn_norm_g,
                 w_branch_a=m_w_branch_a, w_branch_b=m_w_branch_b, w_out=m_w_out, w_ffn_in=m_w_ffn_in,
                 w_ffn_out=m_w_ffn_out, final_g=m_final_g)
    mom_v = dict(w_ada=v_w_ada, b_ada=v_b_ada, norm1_g=v_norm1_g, norm2_g=v_norm2_g, w_in=v_w_in,
                 rel_table=v_rel_table, w_conv=v_w_conv, a_log=v_a_log, dt_bias=v_dt_bias, gdn_norm_g=v_gdn_norm_g,
                 w_branch_a=v_w_branch_a, w_branch_b=v_w_branch_b, w_out=v_w_out, w_ffn_in=v_w_ffn_in,
                 w_ffn_out=v_w_ffn_out, final_g=v_final_g)
    xi, yi, ci = _position()
    chip = 2 * xi + yi
    dev = 2 * chip + ci
    L, D = norm1_g.shape
    NMOD = b_ada.shape[1] // D
    ns = w_ada.shape[2]
    cs = w_conv.shape[2]

    first_blk = _pack_rows([c, w_conv])
    first_all = _allgather8(first_blk, "gather_c").reshape(8, first_blk.shape[0], SMALL_PACK_C)
    c_all, w_conv_all = _unpack_rows(first_all, [(D,), w_conv.shape])
    w_conv_full = w_conv_all.reshape(4, 2, L, CONV_K, cs)[:, 0].transpose(1, 2, 0, 3).reshape(L, CONV_K, 4 * cs)
    b_shard = lax.dynamic_slice_in_dim(b_ada, chip * ns, ns, axis=1).reshape(L, 1, ns)
    mod_shard = _ada_mod(c_all, w_ada, b_shard, "ada_mod")
    mod_all = _allgather8(mod_shard.reshape(L * 8, ns), "gather_mod").reshape(4, 2, L, 8, ns)
    mod = lax.dynamic_index_in_dim(mod_all[:, 0], dev, axis=2, keepdims=False)
    mod = mod.transpose(1, 0, 2).reshape(L, NMOD, D)

    shards = [weights[k].astype(BF16) for k in BIG]
    shards = [s.reshape(s.shape[0], 2, s.shape[1] // 2, s.shape[2]) for s in shards]
    P = dict(norm1_g=norm1_g, norm2_g=norm2_g, w_conv=w_conv_full, a_log=a_log, dt_bias=dt_bias,
             gdn_norm_g=gdn_norm_g)
    shard_of = dict(zip(BIG, shards))

    big, got = _band_bias(rel_table, "band_bias", ([shard_of["w_in"]], 0))
    alone = ["w_branch_a", "w_branch_b", "w_out", "w_ffn_out"]
    got += _gather_weights([shard_of[k] for k in alone], 0, "gather_weights_l0")
    W = [{k: _weight_view(k, t, "l0") for k, t in zip(["w_in"] + alone, got)}]
    late = (["w_ffn_in"], [shard_of["w_ffn_in"]])
    xc = x[0]
    saved = []
    for l in range(L):
        xc, sv, gathered, W[l] = _layer_fwd(l, xc, mod[l], W[l], P, big[l], (shards, l + 1) if l + 1 < L else None,
                                           late if l == 0 else None)
        saved.append(sv)
        if l + 1 < L:
            W.append({k: _weight_view(k, gathered[k], f"l{l + 1}") for k in BIG})
    dx, loss_dev, dfinal = _loss_head(xc, final_g.reshape(1, D), loss_target[0], "loss_head")

    where = jnp.stack([ci, chip]).astype(jnp.int32)
    grads = [None] * L
    dmods = [None] * L
    shard_grads = {k: [None] * L for k in BIG}

    def finish_reduce_scatter(l, sums, from_chips):
        halves = [_sum_own_and_received(s_[1], r_, where, f"rs_sum_{k}_l{l}")
                  for k, s_, r_ in zip(BIG, sums, from_chips)]
        for k, t in zip(BIG, _rs_pair(halves, f"rs_pair_l{l}")):
            shard_grads[k][l] = t.reshape(2 * t.shape[1], t.shape[2])

    pending = None
    for l in reversed(range(L)):
        exchange = [s_[0] for s_ in pending] if pending is not None else ()
        dx, grads[l], from_sibling, dmods[l], received = _layer_bwd(l, dx, saved[l], mod[l], W[l], P, big[l], exchange)
        if pending is not None:
            finish_reduce_scatter(l + 1, pending, received)
        gs = [grads[l][k] for k in BIG]
        pending = [_pair_sums(g_, where, r_, f"rs_pair_sum_{k}_l{l}") for k, g_, r_ in zip(BIG, gs, from_sibling)]
    finish_reduce_scatter(0, pending, _rs_chips([s_[0] for s_ in pending], "rs_chips_l0"))
    dmod = jnp.stack(dmods)

    small = {k: jnp.stack([grads[l][k] for l in range(L)]) for k in SMALL}
    parts = [dmod] + [small[k] for k in SMALL] + [dfinal, loss_dev[0, :1]]
    small_blk = _pack_rows(parts)
    srows = small_blk.shape[0]
    small_all, small_sum = _allgather8(small_blk, "gather_small", reduce_rows=srows)
    shapes = [dmod.shape] + [small[k].shape for k in SMALL] + [(D,), (1,)]
    tot = _unpack_rows(small_sum, shapes)
    G = dict(zip(SMALL, tot[1:1 + len(SMALL)]))
    G["b_ada"] = tot[0].reshape(b_ada.shape)
    G["w_conv"] = lax.dynamic_slice_in_dim(G["w_conv"], chip * cs, cs, axis=2)
    G["final_g"] = tot[-2]
    loss = tot[-1][0]
    dmod_all = _unpack_rows(small_all.reshape(8, srows, SMALL_PACK_C), [dmod.shape])[0]
    dmod_cols = lax.dynamic_slice_in_dim(dmod_all, chip * ns, ns, axis=2).transpose(1, 0, 2)
    G["w_ada"] = _ada_wgrad(c_all, dmod_cols, "ada_wgrad")

    order = ["w_ada", "b_ada", "norm1_g", "norm2_g", "w_in", "rel_table", "w_conv", "a_log", "dt_bias", "gdn_norm_g",
             "w_branch_a", "w_branch_b", "w_out", "w_ffn_in", "w_ffn_out", "final_g"]
    deltas, new_m, new_v = {}, {}, {}
    for k in order:
        w = weights[k]
        if k == "w_in":
            to_cols = lambda t: jnp.transpose(t, (2, 0, 1))
            from_cols = lambda t: jnp.transpose(t, (1, 2, 0))
            gt = to_cols(jnp.stack(shard_grads[k]))
            d_, m_, v_ = _adamw_lead(to_cols(w), gt, to_cols(mom_m[k]), to_cols(mom_v[k]), f"adamw_{k}",
                                     W_IN_SHARD // 30)
            G[k], deltas[k], new_m[k], new_v[k] = from_cols(gt), from_cols(d_), from_cols(m_), from_cols(v_)
            continue
        if k in BIG:
            G[k], deltas[k], new_m[k], new_v[k] = _adamw_layers(w, shard_grads[k], mom_m[k], mom_v[k], f"adamw_{k}")
            continue
        as2d = (lambda t: t.reshape(1, -1)) if w.ndim == 1 else (lambda t: t)
        d_, m_, v_ = _adamw(as2d(w), as2d(G[k]), as2d(mom_m[k]), as2d(mom_v[k]), f"adamw_{k}")
        deltas[k], new_m[k], new_v[k] = d_.reshape(w.shape), m_.reshape(w.shape), v_.reshape(w.shape)
    return (loss, dx[None], *[G[k] for k in order], *[deltas[k] for k in order], *[new_m[k] for k in order],
            *[new_v[k] for k in order])
```
